```python
import math
import jax, jax.numpy as jnp
from jax import lax
import numpy as np

D_MODEL = 1024
BATCH = 8
SEQ = 8192
DEPTH = 2

CHUNK = 64
CONV_WIDTH = 512
CONV_GROUPS = 8
CONV_TAPS = 3
N_HEADS = 8
HEAD_DIM = 64
ATTN_WIDTH = N_HEADS * HEAD_DIM
D_FF = 4 * D_MODEL
Q_BLOCK = 128
N_MOD = 6
EPS = 1e-6
IN_COLS = 3 * CONV_WIDTH + 3 * ATTN_WIDTH + 2 * D_MODEL

kernel_name = "hybrid_shortconv_stickbreaking_block"


def rms_norm(x, g):
    xf = x.astype(jnp.float32)
    y = xf * lax.rsqrt(jnp.mean(xf * xf, axis=-1, keepdims=True) + EPS)
    return (y * g.astype(jnp.float32)).astype(x.dtype)


def modulate(h, shift, scale):
    return h * (1.0 + scale[:, None, :]) + shift[:, None, :]


def short_conv_branch(b_gate, c_gate, u, conv_w):
    s = u.shape[1]
    v = c_gate * u
    vp = jnp.pad(v, ((0, 0), (CONV_TAPS - 1, 0), (0, 0)))
    y = sum(conv_w[k] * vp[:, k:k + s, :] for k in range(CONV_TAPS))
    return b_gate * y


def stick_breaking_attention(q, k, v):
    b, s, h, dh = q.shape
    qh = jnp.transpose(q, (0, 2, 1, 3))
    kh = jnp.transpose(k, (0, 2, 1, 3))
    vh = jnp.transpose(v, (0, 2, 1, 3))
    inv_sqrt = 1.0 / math.sqrt(dh)
    key_pos = jnp.arange(s)
    n_blocks = s // Q_BLOCK

    def block(i):
        start = i * Q_BLOCK
        q_blk = lax.dynamic_slice_in_dim(qh, start, Q_BLOCK, axis=2)
        z = jnp.einsum('bhqd,bhkd->bhqk', q_blk, kh).astype(jnp.float32) * inv_sqrt
        q_pos = start + jnp.arange(Q_BLOCK)
        mask = key_pos[None, :] < q_pos[:, None]
        log_1m_beta = jnp.where(mask, jax.nn.log_sigmoid(-z), 0.0)
        suffix = lax.cumsum(log_1m_beta, axis=3, reverse=True) - log_1m_beta
        a = jnp.where(mask, jnp.exp(jax.nn.log_sigmoid(z) + suffix), 0.0)
        return jnp.einsum('bhqk,bhkd->bhqd', a, vh.astype(jnp.float32)).astype(q.dtype)

    o = lax.map(block, jnp.arange(n_blocks))
    o = jnp.transpose(o, (1, 0, 3, 2, 4))
    return o.reshape(b, s, h * dh)


def _fwd_setup_inputs(seed: int = 0) -> dict:
    key = jax.random.key(seed)
    ks = jax.random.split(key, 16)
    f32 = jnp.float32
    L, D = DEPTH, D_MODEL

    def nrm(k, shape, fan_in):
        return jax.random.normal(k, shape, f32) * (fan_in ** -0.5)

    def gain(k):
        return 1.0 + 0.05 * jax.random.normal(k, (L, D), f32)

    return {
        "x": jax.random.normal(ks[0], (BATCH, SEQ, D), f32),
        "c": jax.random.normal(ks[1], (BATCH, D), f32),
        "w_ada": nrm(ks[2], (L, D, N_MOD * D), D) * 0.5,
        "b_ada": 0.01 * jax.random.normal(ks[3], (L, N_MOD * D), f32),
        "g_pre_mix": gain(ks[4]),
        "g_post_mix": gain(ks[5]),
        "g_pre_mlp": gain(ks[6]),
        "g_post_mlp": gain(ks[7]),
        "w_in": nrm(ks[8], (L, D, IN_COLS), D),
        "conv_w": nrm(ks[9], (L, CONV_TAPS, CONV_WIDTH), CONV_TAPS),
        "w_proj_conv": nrm(ks[10], (L, CONV_WIDTH, D), CONV_WIDTH),
        "w_proj_attn": nrm(ks[11], (L, ATTN_WIDTH, D), ATTN_WIDTH),
        "w_out": nrm(ks[12], (L, D, D), D),
        "w_mlp_in": nrm(ks[13], (L, D, D_FF), D),
        "w_mlp_out": nrm(ks[14], (L, D_FF, D), D_FF),
    }


def _fwd_reference(x, c, w_ada, b_ada, g_pre_mix, g_post_mix, g_pre_mlp, g_post_mlp,
              w_in, conv_w, w_proj_conv, w_proj_attn, w_out, w_mlp_in, w_mlp_out):
    b, s, d = x.shape
    split_pts = np.cumsum([CONV_WIDTH, CONV_WIDTH, CONV_WIDTH,
                           ATTN_WIDTH, ATTN_WIDTH, ATTN_WIDTH, D_MODEL])
    for l in range(DEPTH):
        mod = c @ w_ada[l] + b_ada[l]
        sh1, sc1, gt1, sh2, sc2, gt2 = jnp.split(mod, N_MOD, axis=-1)

        h = modulate(rms_norm(x, g_pre_mix[l]), sh1, sc1)
        proj = h @ w_in[l]
        bg, cg, u, q, k, v, ga, gb = jnp.split(proj, split_pts, axis=-1)
        y_conv = short_conv_branch(bg, cg, u, conv_w[l]) @ w_proj_conv[l]
        o = stick_breaking_attention(q.reshape(b, s, N_HEADS, HEAD_DIM),
                                     k.reshape(b, s, N_HEADS, HEAD_DIM),
                                     v.reshape(b, s, N_HEADS, HEAD_DIM))
        y_attn = o @ w_proj_attn[l]
        merged = jax.nn.sigmoid(ga) * y_conv + jax.nn.sigmoid(gb) * y_attn
        mix_out = merged @ w_out[l]
        x = x + gt1[:, None, :] * rms_norm(mix_out, g_post_mix[l])

        h2 = modulate(rms_norm(x, g_pre_mlp[l]), sh2, sc2)
        ff = jnp.square(jax.nn.relu(h2 @ w_mlp_in[l])) @ w_mlp_out[l]
        x = x + gt2[:, None, :] * rms_norm(ff, g_post_mlp[l])
    return x


import jax as _jax
import jax.numpy as _jnp

TWIN_FORMAT = 'train_step'
FWD_PARAMS = ['x', 'c', 'w_ada', 'b_ada', 'g_pre_mix', 'g_post_mix', 'g_pre_mlp', 'g_post_mlp', 'w_in', 'conv_w', 'w_proj_conv', 'w_proj_attn', 'w_out', 'w_mlp_in', 'w_mlp_out']
TWIN_WEIGHTS = ['w_ada', 'b_ada', 'g_pre_mix', 'g_post_mix', 'g_pre_mlp', 'g_post_mlp', 'w_in', 'conv_w', 'w_proj_conv', 'w_proj_attn', 'w_out', 'w_mlp_in', 'w_mlp_out']
TWIN_DIFF_INPUT = 'x'
TWIN_INPUTS = ['x', 'c', 'w_ada', 'b_ada', 'g_pre_mix', 'g_post_mix', 'g_pre_mlp', 'g_post_mlp', 'w_in', 'conv_w', 'w_proj_conv', 'w_proj_attn', 'w_out', 'w_mlp_in', 'w_mlp_out', 'loss_target', 'm_w_ada', 'm_b_ada', 'm_g_pre_mix', 'm_g_post_mix', 'm_g_pre_mlp', 'm_g_post_mlp', 'm_w_in', 'm_conv_w', 'm_w_proj_conv', 'm_w_proj_attn', 'm_w_out', 'm_w_mlp_in', 'm_w_mlp_out', 'v_w_ada', 'v_b_ada', 'v_g_pre_mix', 'v_g_post_mix', 'v_g_pre_mlp', 'v_g_post_mlp', 'v_w_in', 'v_conv_w', 'v_w_proj_conv', 'v_w_proj_attn', 'v_w_out', 'v_w_mlp_in', 'v_w_mlp_out']
TWIN_OUTPUTS = ['loss', 'grad_x', 'grad_w_ada', 'grad_b_ada', 'grad_g_pre_mix', 'grad_g_post_mix', 'grad_g_pre_mlp', 'grad_g_post_mlp', 'grad_w_in', 'grad_conv_w', 'grad_w_proj_conv', 'grad_w_proj_attn', 'grad_w_out', 'grad_w_mlp_in', 'grad_w_mlp_out', 'delta_w_ada', 'delta_b_ada', 'delta_g_pre_mix', 'delta_g_post_mix', 'delta_g_pre_mlp', 'delta_g_post_mlp', 'delta_w_in', 'delta_conv_w', 'delta_w_proj_conv', 'delta_w_proj_attn', 'delta_w_out', 'delta_w_mlp_in', 'delta_w_mlp_out', 'new_m_w_ada', 'new_m_b_ada', 'new_m_g_pre_mix', 'new_m_g_post_mix', 'new_m_g_pre_mlp', 'new_m_g_post_mlp', 'new_m_w_in', 'new_m_conv_w', 'new_m_w_proj_conv', 'new_m_w_proj_attn', 'new_m_w_out', 'new_m_w_mlp_in', 'new_m_w_mlp_out', 'new_v_w_ada', 'new_v_b_ada', 'new_v_g_pre_mix', 'new_v_g_post_mix', 'new_v_g_pre_mlp', 'new_v_g_post_mlp', 'new_v_w_in', 'new_v_conv_w', 'new_v_w_proj_conv', 'new_v_w_proj_attn', 'new_v_w_out', 'new_v_w_mlp_in', 'new_v_w_mlp_out']
TWIN_LEAF_KINDS = {'loss': 'loss', 'grad_x': 'grad_x', 'grad_w_ada': 'grad_w', 'grad_b_ada': 'grad_w', 'grad_g_pre_mix': 'grad_w', 'grad_g_post_mix': 'grad_w', 'grad_g_pre_mlp': 'grad_w', 'grad_g_post_mlp': 'grad_w', 'grad_w_in': 'grad_w', 'grad_conv_w': 'grad_w', 'grad_w_proj_conv': 'grad_w', 'grad_w_proj_attn': 'grad_w', 'grad_w_out': 'grad_w', 'grad_w_mlp_in': 'grad_w', 'grad_w_mlp_out': 'grad_w', 'delta_w_ada': 'delta_w', 'delta_b_ada': 'delta_w', 'delta_g_pre_mix': 'delta_w', 'delta_g_post_mix': 'delta_w', 'delta_g_pre_mlp': 'delta_w', 'delta_g_post_mlp': 'delta_w', 'delta_w_in': 'delta_w', 'delta_conv_w': 'delta_w', 'delta_w_proj_conv': 'delta_w', 'delta_w_proj_attn': 'delta_w', 'delta_w_out': 'delta_w', 'delta_w_mlp_in': 'delta_w', 'delta_w_mlp_out': 'delta_w', 'new_m_w_ada': 'new_m', 'new_m_b_ada': 'new_m', 'new_m_g_pre_mix': 'new_m', 'new_m_g_post_mix': 'new_m', 'new_m_g_pre_mlp': 'new_m', 'new_m_g_post_mlp': 'new_m', 'new_m_w_in': 'new_m', 'new_m_conv_w': 'new_m', 'new_m_w_proj_conv': 'new_m', 'new_m_w_proj_attn': 'new_m', 'new_m_w_out': 'new_m', 'new_m_w_mlp_in': 'new_m', 'new_m_w_mlp_out': 'new_m', 'new_v_w_ada': 'new_v', 'new_v_b_ada': 'new_v', 'new_v_g_pre_mix': 'new_v', 'new_v_g_post_mix': 'new_v', 'new_v_g_pre_mlp': 'new_v', 'new_v_g_post_mlp': 'new_v', 'new_v_w_in': 'new_v', 'new_v_conv_w': 'new_v', 'new_v_w_proj_conv': 'new_v', 'new_v_w_proj_attn': 'new_v', 'new_v_w_out': 'new_v', 'new_v_w_mlp_in': 'new_v', 'new_v_w_mlp_out': 'new_v'}


def _forward(args):
    return _fwd_reference(*[args[k] for k in FWD_PARAMS])


def _output_shape():
    def fwd():
        inp = _fwd_setup_inputs(0)
        return _fwd_reference(*[inp[k] for k in FWD_PARAMS])
    out = _jax.eval_shape(fwd)
    return out.shape, out.dtype

N_MICROBATCH = 1
ADAM_LR = 0.001
ADAM_B1 = 0.9
ADAM_B2 = 0.999
ADAM_EPS = 1e-08
ADAM_WD = 0.01
ADAM_STEP = 10
PER_EXAMPLE_BATCH_AXIS = {'x': 0, 'c': 0, 'loss_target': 0}
SHARED_INPUTS = []
_WEIGHT_DTYPES = {'w_ada': _jnp.float32, 'b_ada': _jnp.float32, 'g_pre_mix': _jnp.float32, 'g_post_mix': _jnp.float32, 'g_pre_mlp': _jnp.float32, 'g_post_mlp': _jnp.float32, 'w_in': _jnp.float32, 'conv_w': _jnp.float32, 'w_proj_conv': _jnp.float32, 'w_proj_attn': _jnp.float32, 'w_out': _jnp.float32, 'w_mlp_in': _jnp.float32, 'w_mlp_out': _jnp.float32}
MOMENT_SCALE = {'w_ada': 7.085163e+00, 'b_ada': 6.896024e+00, 'g_pre_mix': 5.447835e-01, 'g_post_mix': 1.828351e+01, 'g_pre_mlp': 5.383182e-01, 'g_post_mlp': 1.866388e+01, 'w_in': 4.565809e-01, 'conv_w': 5.148526e-01, 'w_proj_conv': 4.017789e-01, 'w_proj_attn': 1.107747e+00, 'w_out': 1.025132e+00, 'w_mlp_in': 7.608210e-01, 'w_mlp_out': 2.675975e+00}


def _to_microbatches(a, axis):
    t = _jnp.moveaxis(a, axis, 0)
    t = t.reshape((N_MICROBATCH, t.shape[0] // N_MICROBATCH) + t.shape[1:])
    return _jnp.moveaxis(t, 1, axis + 1)


def setup_inputs(seed: int = 0) -> dict:
    inp = _fwd_setup_inputs(seed)
    key = _jax.random.fold_in(_jax.random.key(seed), 7919)
    shape, _ = _output_shape()
    out = dict(inp)
    out["loss_target"] = _jax.random.normal(_jax.random.fold_in(key, 0), shape, _jnp.float32)
    for i, name in enumerate(TWIN_WEIGHTS):
        w = inp[name].astype(_jnp.float32)
        if MOMENT_SCALE is None:
            s = _jnp.sqrt(_jnp.mean(_jnp.square(w)) + 1e-30)
        else:
            s = MOMENT_SCALE[name]
        km, kv = _jax.random.split(_jax.random.fold_in(key, i + 1))
        out[name] = w
        out["m_" + name] = s * _jax.random.normal(km, w.shape, _jnp.float32)
        out["v_" + name] = (s * s) * _jax.random.uniform(kv, w.shape, _jnp.float32, 0.5, 1.5)
    if N_MICROBATCH > 1:
        for name, axis in PER_EXAMPLE_BATCH_AXIS.items():
            out[name] = _to_microbatches(out[name], axis)
    return {'x': out['x'], 'c': out['c'], 'w_ada': out['w_ada'], 'b_ada': out['b_ada'], 'g_pre_mix': out['g_pre_mix'], 'g_post_mix': out['g_post_mix'], 'g_pre_mlp': out['g_pre_mlp'], 'g_post_mlp': out['g_post_mlp'], 'w_in': out['w_in'], 'conv_w': out['conv_w'], 'w_proj_conv': out['w_proj_conv'], 'w_proj_attn': out['w_proj_attn'], 'w_out': out['w_out'], 'w_mlp_in': out['w_mlp_in'], 'w_mlp_out': out['w_mlp_out'], 'loss_target': out['loss_target'], 'm_w_ada': out['m_w_ada'], 'm_b_ada': out['m_b_ada'], 'm_g_pre_mix': out['m_g_pre_mix'], 'm_g_post_mix': out['m_g_post_mix'], 'm_g_pre_mlp': out['m_g_pre_mlp'], 'm_g_post_mlp': out['m_g_post_mlp'], 'm_w_in': out['m_w_in'], 'm_conv_w': out['m_conv_w'], 'm_w_proj_conv': out['m_w_proj_conv'], 'm_w_proj_attn': out['m_w_proj_attn'], 'm_w_out': out['m_w_out'], 'm_w_mlp_in': out['m_w_mlp_in'], 'm_w_mlp_out': out['m_w_mlp_out'], 'v_w_ada': out['v_w_ada'], 'v_b_ada': out['v_b_ada'], 'v_g_pre_mix': out['v_g_pre_mix'], 'v_g_post_mix': out['v_g_post_mix'], 'v_g_pre_mlp': out['v_g_pre_mlp'], 'v_g_post_mlp': out['v_g_post_mlp'], 'v_w_in': out['v_w_in'], 'v_conv_w': out['v_conv_w'], 'v_w_proj_conv': out['v_w_proj_conv'], 'v_w_proj_attn': out['v_w_proj_attn'], 'v_w_out': out['v_w_out'], 'v_w_mlp_in': out['v_w_mlp_in'], 'v_w_mlp_out': out['v_w_mlp_out']}


def _loss(weights, diff, rest, loss_target):
    with _jax.named_scope("forward"):
        args = {**rest, TWIN_DIFF_INPUT: diff, **{k: w.astype(_WEIGHT_DTYPES[k]) for k, w in weights.items()}}
        y = _forward(args)
    with _jax.named_scope("loss_head"):
        err = _jnp.square(y.astype(_jnp.float32) - loss_target)
        return 0.5 * _jnp.sum(_jnp.mean(err, axis=-1)) if err.ndim else 0.5 * err


def _adamw(w, g, m, v):
    m = ADAM_B1 * m + (1.0 - ADAM_B1) * g
    v = ADAM_B2 * v + (1.0 - ADAM_B2) * _jnp.square(g)
    m_hat = m / (1.0 - ADAM_B1 ** ADAM_STEP)
    v_hat = v / (1.0 - ADAM_B2 ** ADAM_STEP)
    delta = -ADAM_LR * (m_hat / (_jnp.sqrt(v_hat) + ADAM_EPS) + ADAM_WD * w)
    return delta, m, v


def reference(x, c, w_ada, b_ada, g_pre_mix, g_post_mix, g_pre_mlp, g_post_mlp, w_in, conv_w, w_proj_conv, w_proj_attn, w_out, w_mlp_in, w_mlp_out, loss_target, m_w_ada, m_b_ada, m_g_pre_mix, m_g_post_mix, m_g_pre_mlp, m_g_post_mlp, m_w_in, m_conv_w, m_w_proj_conv, m_w_proj_attn, m_w_out, m_w_mlp_in, m_w_mlp_out, v_w_ada, v_b_ada, v_g_pre_mix, v_g_post_mix, v_g_pre_mlp, v_g_post_mlp, v_w_in, v_conv_w, v_w_proj_conv, v_w_proj_attn, v_w_out, v_w_mlp_in, v_w_mlp_out):
    given = dict(x=x, c=c, w_ada=w_ada, b_ada=b_ada, g_pre_mix=g_pre_mix, g_post_mix=g_post_mix, g_pre_mlp=g_pre_mlp, g_post_mlp=g_post_mlp, w_in=w_in, conv_w=conv_w, w_proj_conv=w_proj_conv, w_proj_attn=w_proj_attn, w_out=w_out, w_mlp_in=w_mlp_in, w_mlp_out=w_mlp_out, loss_target=loss_target, m_w_ada=m_w_ada, m_b_ada=m_b_ada, m_g_pre_mix=m_g_pre_mix, m_g_post_mix=m_g_post_mix, m_g_pre_mlp=m_g_pre_mlp, m_g_post_mlp=m_g_post_mlp, m_w_in=m_w_in, m_conv_w=m_conv_w, m_w_proj_conv=m_w_proj_conv, m_w_proj_attn=m_w_proj_attn, m_w_out=m_w_out, m_w_mlp_in=m_w_mlp_in, m_w_mlp_out=m_w_mlp_out, v_w_ada=v_w_ada, v_b_ada=v_b_ada, v_g_pre_mix=v_g_pre_mix, v_g_post_mix=v_g_post_mix, v_g_pre_mlp=v_g_pre_mlp, v_g_post_mlp=v_g_post_mlp, v_w_in=v_w_in, v_conv_w=v_conv_w, v_w_proj_conv=v_w_proj_conv, v_w_proj_attn=v_w_proj_attn, v_w_out=v_w_out, v_w_mlp_in=v_w_mlp_in, v_w_mlp_out=v_w_mlp_out)
    weights = {n: given[n] for n in TWIN_WEIGHTS}
    shared = {n: given[n] for n in SHARED_INPUTS}
    per_example = {n: given[n] for n in ['x', 'c']}
    grad_fn = _jax.value_and_grad(_loss, argnums=(0, 1))

    def one_microbatch(ex, loss_target):
        ex = dict(ex)
        diff = ex.pop(TWIN_DIFF_INPUT)
        return grad_fn(weights, diff, {**shared, **ex}, loss_target)

    if N_MICROBATCH == 1:
        loss, (grad_w, grad_x) = one_microbatch(per_example, given["loss_target"])
    else:
        def body(carry, xs):
            loss_sum, grad_sum = carry
            l_k, (gw_k, gx_k) = one_microbatch(xs[0], xs[1])
            with _jax.named_scope("update"):
                return (loss_sum + l_k, _jax.tree.map(_jnp.add, grad_sum, gw_k)), gx_k

        init = (_jnp.zeros((), _jnp.float32), _jax.tree.map(_jnp.zeros_like, weights))
        (loss, grad_w), grad_x = _jax.lax.scan(body, init, (per_example, given["loss_target"]))
    with _jax.named_scope("update"):
        delta_w, new_m, new_v = {}, {}, {}
        for n in TWIN_WEIGHTS:
            delta_w[n], new_m[n], new_v[n] = _adamw(weights[n], grad_w[n], given["m_" + n], given["v_" + n])
    return (loss, grad_x, *[grad_w[n] for n in TWIN_WEIGHTS], *[delta_w[n] for n in TWIN_WEIGHTS],
            *[new_m[n] for n in TWIN_WEIGHTS], *[new_v[n] for n in TWIN_WEIGHTS])
```

```python
import functools

import jax
import jax.numpy as jnp
from jax import lax
from jax.experimental import pallas as pl
from jax.experimental.pallas import tpu as pltpu

F32 = jnp.float32
BF16 = jnp.bfloat16
EPS = 1e-6
N_MOD = 6
HEAD_DIM = 64
LANES = 128
ATTN_SCALE = 1.0 / 8.0
ADAM_LR = 0.001
ADAM_B1 = 0.9
ADAM_B2 = 0.999
ADAM_EPS = 1e-08
ADAM_WD = 0.01
ADAM_STEP = 10
VMEM_LIMIT = 56 * 1024 * 1024
MESH = pl.DeviceIdType.MESH
OTHER_CHIPS = ((1, 0), (0, 1), (1, 1))

_NT = (((1,), (1,)), ((), ()))
_TN = (((0,), (0,)), ((), ()))


def _sds(shape, dtype):
    return jax.ShapeDtypeStruct(shape, dtype)


def _params(sem):
    return pltpu.CompilerParams(dimension_semantics=sem, vmem_limit_bytes=VMEM_LIMIT)


def _fit(t, n):
    t = min(t, n)
    while n % t:
        t //= 2
    return t


def _vec_spec(d, nargs):
    if nargs == 1:
        return pl.BlockSpec((1, d), lambda i: (0, 0))
    return pl.BlockSpec((1, d), lambda i, j: (0, 0))


def _softplus_parts(z):
    t = jnp.exp(-jnp.abs(z))
    sp = jnp.maximum(z, 0.0) + jnp.log(1.0 + t)
    sig = jnp.where(z >= 0.0, 1.0, t) / (1.0 + t)
    return sp, sig


def _sigmoid(z):
    t = jnp.exp(-jnp.abs(z))
    return jnp.where(z >= 0.0, 1.0, t) / (1.0 + t)


def _split_bf16(a):
    hi = a.astype(BF16)
    lo = (a - hi.astype(F32)).astype(BF16)
    return hi, lo


def _rms_bwd(dn, xin, g):
    r = lax.rsqrt(jnp.mean(xin * xin, axis=-1, keepdims=True) + EPS)
    xh = xin * r
    dxh = dn * g
    dxin = r * (dxh - xh * jnp.mean(dxh * xh, axis=-1, keepdims=True))
    return dxin, xh


def _colsum(a):
    return jnp.sum(a, axis=0, keepdims=True)


def _norm_mod_matmul(x, g, sc, sh, w, *, name, tm=1024, tn=512):
    s, d = x.shape
    n = w.shape[1]
    tm, tn = _fit(tm, s), _fit(tn, n)

    def body(x_ref, g_ref, sc_ref, sh_ref, w_ref, h_ref, o_ref):
        @pl.when(pl.program_id(1) == 0)
        def _():
            xv = x_ref[...]
            r = lax.rsqrt(jnp.mean(xv * xv, axis=-1, keepdims=True) + EPS)
            h_ref[...] = ((xv * r * g_ref[...]) * (1.0 + sc_ref[...]) + sh_ref[...]).astype(BF16)
        o_ref[...] = jnp.dot(h_ref[...], w_ref[...], preferred_element_type=F32).astype(BF16)

    return pl.pallas_call(
        body, name=name, grid=(s // tm, n // tn),
        in_specs=[pl.BlockSpec((tm, d), lambda i, j: (i, 0)), _vec_spec(d, 2), _vec_spec(d, 2), _vec_spec(d, 2),
                  pl.BlockSpec((d, tn), lambda i, j: (0, j))],
        out_specs=[pl.BlockSpec((tm, d), lambda i, j: (i, 0)), pl.BlockSpec((tm, tn), lambda i, j: (i, j))],
        out_shape=[_sds((s, d), BF16), _sds((s, n), BF16)],
        compiler_params=_params(("parallel", "arbitrary")),
    )(x, g, sc, sh, w)


HALO = 16


def _conv_fwd(proj, conv_w, *, name, tm=512):
    s = proj.shape[0]
    cw = conv_w.shape[1]
    tm = min(tm, s)
    nb = tm // HALO

    def body(bg_ref, cg_ref, u_ref, cgh_ref, uh_ref, w_ref, yc_ref, vbuf):
        i = pl.program_id(0)
        vv = cg_ref[...].astype(F32) * u_ref[...].astype(F32)
        halo = cgh_ref[...].astype(F32) * uh_ref[...].astype(F32)
        vbuf[0:HALO, :] = jnp.where(i > 0, halo, 0.0)
        vbuf[HALO:HALO + tm, :] = vv
        v1 = vbuf[HALO - 1:HALO - 1 + tm, :]
        v2 = vbuf[HALO - 2:HALO - 2 + tm, :]
        w = w_ref[...]
        y = w[2:3, :] * vv + w[1:2, :] * v1 + w[0:1, :] * v2
        yc_ref[...] = (bg_ref[...].astype(F32) * y).astype(BF16)

    def prev(i):
        return jnp.maximum(i * nb - 1, 0)

    return pl.pallas_call(
        body, name=name, grid=(s // tm,),
        in_specs=[pl.BlockSpec((tm, cw), lambda i: (i, 0)), pl.BlockSpec((tm, cw), lambda i: (i, 1)),
                  pl.BlockSpec((tm, cw), lambda i: (i, 2)),
                  pl.BlockSpec((HALO, cw), lambda i: (prev(i), 1)), pl.BlockSpec((HALO, cw), lambda i: (prev(i), 2)),
                  pl.BlockSpec((3, cw), lambda i: (0, 0))],
        out_specs=pl.BlockSpec((tm, cw), lambda i: (i, 0)),
        out_shape=_sds((s, cw), BF16),
        scratch_shapes=[pltpu.VMEM((HALO + tm, cw), F32)],
        compiler_params=_params(("arbitrary",)),
    )(proj, proj, proj, proj, proj, conv_w)


def _tri(qb):
    r = lax.broadcasted_iota(jnp.int32, (qb, qb), 0)
    c = lax.broadcasted_iota(jnp.int32, (qb, qb), 1)
    return (r >= c).astype(BF16)


def _head_mask(h):
    lane = lax.broadcasted_iota(jnp.int32, (1, LANES), 1)
    return (lane >= HEAD_DIM * h) & (lane < HEAD_DIM * (h + 1))


def _attn_cols(d):
    cw = d // 2
    hp = (d // 2) // LANES
    q0 = (3 * cw) // LANES
    return q0, q0 + hp, q0 + 2 * hp, hp


def _attn_fwd(proj, tri, *, d, name, qb=256):
    s = proj.shape[0]
    qb = min(qb, s)
    q0, k0, v0, hp = _attn_cols(d)

    def body(q_ref, k_ref, v_ref, tri_ref, o_ref):
        i = pl.program_id(1)
        row = lax.broadcasted_iota(jnp.int32, (qb, qb), 0)
        col = lax.broadcasted_iota(jnp.int32, (qb, qb), 1)
        causal = col < row
        tri_m = tri_ref[...]
        q = q_ref[...]
        out = jnp.zeros((qb, LANES), F32)
        for h in range(2):
            hm = _head_mask(h)
            qm = jnp.where(hm, q * ATTN_SCALE, 0).astype(BF16)

            def block(j, carry, diag, qm=qm, hm=hm):
                run, acc = carry
                rows = pl.ds(pl.multiple_of(j * qb, qb), qb)
                kb = k_ref[rows, :]
                vb = jnp.where(hm, v_ref[rows, :], 0).astype(BF16)
                z = lax.dot_general(qm, kb, _NT, preferred_element_type=F32)
                sp, _ = _softplus_parts(z)
                lg = -sp
                if diag:
                    lg = jnp.where(causal, lg, 0.0)
                hi, lo = _split_bf16(lg)
                cs = (jnp.dot(hi, tri_m, preferred_element_type=F32)
                      + jnp.dot(lo, tri_m, preferred_element_type=F32))
                a = jnp.exp(z + cs + run)
                if diag:
                    a = jnp.where(causal, a, 0.0)
                acc = acc + jnp.dot(a.astype(BF16), vb, preferred_element_type=F32)
                return run + cs[:, 0:1], acc

            carry = block(i, (jnp.zeros((qb, 1), F32), jnp.zeros((qb, LANES), F32)), True)
            carry = lax.fori_loop(0, i, lambda t, cr: block(i - 1 - t, cr, False), carry)
            out = out + carry[1]
        o_ref[...] = out

    return pl.pallas_call(
        body, name=name, grid=(hp, s // qb),
        in_specs=[pl.BlockSpec((qb, LANES), lambda p, i: (i, q0 + p)),
                  pl.BlockSpec((s, LANES), lambda p, i: (0, k0 + p)),
                  pl.BlockSpec((s, LANES), lambda p, i: (0, v0 + p)),
                  pl.BlockSpec((qb, qb), lambda p, i: (0, 0))],
        out_specs=pl.BlockSpec((qb, LANES), lambda p, i: (i, p)),
        out_shape=_sds((s, hp * LANES), F32),
        compiler_params=_params(("parallel", "arbitrary")),
    )(proj, proj, proj, tri)


def _mix_out(yc, o, proj, x, wpc, wpa, wout, g, gt, *, name, tm=256):
    s, d = x.shape
    cw = yc.shape[1]
    tm = min(tm, s)
    ga_blk = (3 * cw + 3 * (d // 2)) // d

    def body(yc_ref, o_ref, ga_ref, gb_ref, x_ref, wpc_ref, wpa_ref, wout_ref, g_ref, gt_ref,
             ycv_ref, yat_ref, mg_ref, mix_ref, x1_ref):
        y_conv = jnp.dot(yc_ref[...], wpc_ref[...], preferred_element_type=F32)
        y_attn = jnp.dot(o_ref[...].astype(BF16), wpa_ref[...], preferred_element_type=F32)
        merged = (_sigmoid(ga_ref[...].astype(F32)) * y_conv + _sigmoid(gb_ref[...].astype(F32)) * y_attn)
        mg = merged.astype(BF16)
        mix = jnp.dot(mg, wout_ref[...], preferred_element_type=F32)
        r = lax.rsqrt(jnp.mean(mix * mix, axis=-1, keepdims=True) + EPS)
        ycv_ref[...] = y_conv.astype(BF16)
        yat_ref[...] = y_attn.astype(BF16)
        mg_ref[...] = mg
        mix_ref[...] = mix
        x1_ref[...] = x_ref[...] + gt_ref[...] * (mix * r * g_ref[...])

    def rows(w):
        return pl.BlockSpec((tm, w), lambda i: (i, 0))

    def full(a):
        return pl.BlockSpec(a.shape, lambda i: (0, 0))

    return pl.pallas_call(
        body, name=name, grid=(s // tm,),
        in_specs=[rows(cw), rows(d // 2), pl.BlockSpec((tm, d), lambda i: (i, ga_blk)),
                  pl.BlockSpec((tm, d), lambda i: (i, ga_blk + 1)), rows(d),
                  full(wpc), full(wpa), full(wout), _vec_spec(d, 1), _vec_spec(d, 1)],
        out_specs=[rows(d), rows(d), rows(d), rows(d), rows(d)],
        out_shape=[_sds((s, d), BF16), _sds((s, d), BF16), _sds((s, d), BF16), _sds((s, d), F32), _sds((s, d), F32)],
        compiler_params=_params(("parallel",)),
    )(yc, o, proj, proj, x, wpc, wpa, wout, g, gt)


def _relu2(a):
    r = jnp.maximum(a.astype(F32), 0.0)
    return (r * r).astype(BF16)


def _mlp_out(a, x, w2, g, gt, *, name, tm=512):
    s, d = x.shape
    dff = a.shape[1]
    tm = min(tm, s)

    def body(a_ref, x_ref, w_ref, g_ref, gt_ref, ff_ref, x2_ref):
        ff = jnp.dot(_relu2(a_ref[...]), w_ref[...], preferred_element_type=F32)
        r = lax.rsqrt(jnp.mean(ff * ff, axis=-1, keepdims=True) + EPS)
        ff_ref[...] = ff
        x2_ref[...] = x_ref[...] + gt_ref[...] * (ff * r * g_ref[...])

    return pl.pallas_call(
        body, name=name, grid=(s // tm,),
        in_specs=[pl.BlockSpec((tm, dff), lambda i: (i, 0)), pl.BlockSpec((tm, d), lambda i: (i, 0)),
                  pl.BlockSpec((dff, d), lambda i: (0, 0)), _vec_spec(d, 1), _vec_spec(d, 1)],
        out_specs=[pl.BlockSpec((tm, d), lambda i: (i, 0)), pl.BlockSpec((tm, d), lambda i: (i, 0))],
        out_shape=[_sds((s, d), F32), _sds((s, d), F32)],
        compiler_params=_params(("parallel",)),
    )(a, x, w2, g, gt)


def _loss_grad(y, target, *, name, tm=512):
    s, d = y.shape
    tm = min(tm, s)

    def body(y_ref, t_ref, dy_ref, loss_ref):
        @pl.when(pl.program_id(0) == 0)
        def _():
            loss_ref[...] = jnp.zeros_like(loss_ref)
        e = y_ref[...] - t_ref[...]
        dy_ref[...] = e * (1.0 / d)
        loss_ref[...] += 0.5 * jnp.sum(jnp.mean(e * e, axis=-1, keepdims=True), axis=0, keepdims=True)

    return pl.pallas_call(
        body, name=name, grid=(s // tm,),
        in_specs=[pl.BlockSpec((tm, d), lambda i: (i, 0)), pl.BlockSpec((tm, d), lambda i: (i, 0))],
        out_specs=[pl.BlockSpec((tm, d), lambda i: (i, 0)), pl.BlockSpec((1, 1), lambda i: (0, 0))],
        out_shape=[_sds((s, d), F32), _sds((1, 1), F32)],
        compiler_params=_params(("arbitrary",)),
    )(y, target)


def _mlp_out_bwd(dx, ff, a, w2, g, gt, *, name, tm=256):
    s, d = dx.shape
    dff = a.shape[1]
    tm = min(tm, s)

    def body(dx_ref, ff_ref, a_ref, w_ref, g_ref, gt_ref, dff_ref, da_ref, dgt_ref, dg_ref):
        @pl.when(pl.program_id(0) == 0)
        def _():
            dgt_ref[...] = jnp.zeros_like(dgt_ref)
            dg_ref[...] = jnp.zeros_like(dg_ref)
        dxv = dx_ref[...]
        dn = dxv * gt_ref[...]
        dffv, xh = _rms_bwd(dn, ff_ref[...], g_ref[...])
        dgt_ref[...] += _colsum(dxv * (xh * g_ref[...]))
        dg_ref[...] += _colsum(dn * xh)
        dffb = dffv.astype(BF16)
        dff_ref[...] = dffb
        drr = lax.dot_general(dffb, w_ref[...], _NT, preferred_element_type=F32)
        da_ref[...] = (drr * (2.0 * jnp.maximum(a_ref[...].astype(F32), 0.0))).astype(BF16)

    return pl.pallas_call(
        body, name=name, grid=(s // tm,),
        in_specs=[pl.BlockSpec((tm, d), lambda i: (i, 0)), pl.BlockSpec((tm, d), lambda i: (i, 0)),
                  pl.BlockSpec((tm, dff), lambda i: (i, 0)), pl.BlockSpec((dff, d), lambda i: (0, 0)),
                  _vec_spec(d, 1), _vec_spec(d, 1)],
        out_specs=[pl.BlockSpec((tm, d), lambda i: (i, 0)), pl.BlockSpec((tm, dff), lambda i: (i, 0)),
                   _vec_spec(d, 1), _vec_spec(d, 1)],
        out_shape=[_sds((s, d), BF16), _sds((s, dff), BF16), _sds((1, d), F32), _sds((1, d), F32)],
        compiler_params=_params(("arbitrary",)),
    )(dx, ff, a, w2, g, gt)


def _matmul_nt_norm_bwd(dy, w, x, dres, g, sc, *, name, tm=512, tn=512):
    s, n = dy.shape
    d = w.shape[0]
    tm, tn = _fit(tm, s), _fit(tn, n)
    nj = n // tn

    def body(dy_ref, w_ref, x_ref, dres_ref, g_ref, sc_ref, dx_ref, dsh_ref, dsc_ref, dg_ref, acc):
        i, j = pl.program_id(0), pl.program_id(1)

        @pl.when((i == 0) & (j == 0))
        def _():
            dsh_ref[...] = jnp.zeros_like(dsh_ref)
            dsc_ref[...] = jnp.zeros_like(dsc_ref)
            dg_ref[...] = jnp.zeros_like(dg_ref)

        @pl.when(j == 0)
        def _():
            acc[...] = jnp.zeros_like(acc)

        acc[...] += lax.dot_general(dy_ref[...], w_ref[...], _NT, preferred_element_type=F32)

        @pl.when(j == nj - 1)
        def _():
            dh = acc[...]
            dn = dh * (1.0 + sc_ref[...])
            dxin, xh = _rms_bwd(dn, x_ref[...], g_ref[...])
            dsh_ref[...] += _colsum(dh)
            dsc_ref[...] += _colsum(dh * (xh * g_ref[...]))
            dg_ref[...] += _colsum(dn * xh)
            dx_ref[...] = dres_ref[...] + dxin

    return pl.pallas_call(
        body, name=name, grid=(s // tm, nj),
        in_specs=[pl.BlockSpec((tm, tn), lambda i, j: (i, j)), pl.BlockSpec((d, tn), lambda i, j: (0, j)),
                  pl.BlockSpec((tm, d), lambda i, j: (i, 0)), pl.BlockSpec((tm, d), lambda i, j: (i, 0)),
                  _vec_spec(d, 2), _vec_spec(d, 2)],
        out_specs=[pl.BlockSpec((tm, d), lambda i, j: (i, 0)), _vec_spec(d, 2), _vec_spec(d, 2), _vec_spec(d, 2)],
        out_shape=[_sds((s, d), F32), _sds((1, d), F32), _sds((1, d), F32), _sds((1, d), F32)],
        scratch_shapes=[pltpu.VMEM((tm, d), F32)],
        compiler_params=_params(("arbitrary", "arbitrary")),
    )(dy, w, x, dres, g, sc)


def _matmul_tn(a, b, *, name, tk=1024, tn=1024, ts=512, relu2=False):
    s, k = a.shape
    n = b.shape[1]
    tk, tn, ts = _fit(tk, k), _fit(tn, n), _fit(ts, s)

    def body(a_ref, b_ref, o_ref):
        @pl.when(pl.program_id(2) == 0)
        def _():
            o_ref[...] = jnp.zeros_like(o_ref)
        av = a_ref[...]
        if relu2:
            av = _relu2(av)
        o_ref[...] += lax.dot_general(av, b_ref[...], _TN, preferred_element_type=F32)

    return pl.pallas_call(
        body, name=name, grid=(k // tk, n // tn, s // ts),
        in_specs=[pl.BlockSpec((ts, tk), lambda i, j, t: (t, i)), pl.BlockSpec((ts, tn), lambda i, j, t: (t, j))],
        out_specs=pl.BlockSpec((tk, tn), lambda i, j, t: (i, j)),
        out_shape=_sds((k, n), F32),
        compiler_params=_params(("parallel", "parallel", "arbitrary")),
    )(a, b)


def _mix_out_bwd(dx, mix, proj, ycv, yat, wout, wpc, wpa, g, gt, *, name, tm=256):
    s, d = dx.shape
    cw = wpc.shape[0]
    aw = wpa.shape[0]
    tm = min(tm, s)
    ga_blk = (3 * cw + 3 * aw) // d

    def body(dx_ref, mix_ref, ga_ref, gb_ref, ycv_ref, yat_ref, wout_ref, wpc_ref, wpa_ref, g_ref, gt_ref,
             dmix_ref, dycv_ref, dyat_ref, dyc_ref, do_ref, dga_ref, dgb_ref, dgt_ref, dg_ref):
        @pl.when(pl.program_id(0) == 0)
        def _():
            dgt_ref[...] = jnp.zeros_like(dgt_ref)
            dg_ref[...] = jnp.zeros_like(dg_ref)
        dxv = dx_ref[...]
        dn = dxv * gt_ref[...]
        dmix, xh = _rms_bwd(dn, mix_ref[...], g_ref[...])
        dgt_ref[...] += _colsum(dxv * (xh * g_ref[...]))
        dg_ref[...] += _colsum(dn * xh)
        dmixb = dmix.astype(BF16)
        dmix_ref[...] = dmixb
        dmerged = lax.dot_general(dmixb, wout_ref[...], _NT, preferred_element_type=F32)
        sga = _sigmoid(ga_ref[...].astype(F32))
        sgb = _sigmoid(gb_ref[...].astype(F32))
        dycv = (dmerged * sga).astype(BF16)
        dyat = (dmerged * sgb).astype(BF16)
        dycv_ref[...] = dycv
        dyat_ref[...] = dyat
        dga_ref[...] = (dmerged * ycv_ref[...].astype(F32) * (sga * (1.0 - sga))).astype(BF16)
        dgb_ref[...] = (dmerged * yat_ref[...].astype(F32) * (sgb * (1.0 - sgb))).astype(BF16)
        dyc_ref[...] = lax.dot_general(dycv, wpc_ref[...], _NT, preferred_element_type=F32).astype(BF16)
        do_ref[...] = lax.dot_general(dyat, wpa_ref[...], _NT, preferred_element_type=F32).astype(BF16)

    def rows(w):
        return pl.BlockSpec((tm, w), lambda i: (i, 0))

    def full(a):
        return pl.BlockSpec(a.shape, lambda i: (0, 0))

    return pl.pallas_call(
        body, name=name, grid=(s // tm,),
        in_specs=[rows(d), rows(d), pl.BlockSpec((tm, d), lambda i: (i, ga_blk)),
                  pl.BlockSpec((tm, d), lambda i: (i, ga_blk + 1)), rows(d), rows(d),
                  full(wout), full(wpc), full(wpa), _vec_spec(d, 1), _vec_spec(d, 1)],
        out_specs=[rows(d), rows(d), rows(d), rows(cw), rows(aw), rows(d), rows(d), _vec_spec(d, 1), _vec_spec(d, 1)],
        out_shape=[_sds((s, d), BF16), _sds((s, d), BF16), _sds((s, d), BF16), _sds((s, cw), BF16),
                   _sds((s, aw), BF16), _sds((s, d), BF16), _sds((s, d), BF16), _sds((1, d), F32), _sds((1, d), F32)],
        compiler_params=_params(("arbitrary",)),
    )(dx, mix, proj, proj, ycv, yat, wout, wpc, wpa, g, gt)


def _conv_bwd(dyc, proj, conv_w, *, name, tm=512):
    s = proj.shape[0]
    cw = conv_w.shape[1]
    tm = min(tm, s)
    nb = tm // HALO
    nt = s // tm
    last_blk = s // HALO - 1

    def body(dyc_ref, bg_ref, cg_ref, u_ref, cgh_ref, uh_ref, dych_ref, bgh_ref, w_ref,
             dbg_ref, dcg_ref, du_ref, dw_ref, vbuf, gbuf):
        i = pl.program_id(0)

        @pl.when(i == 0)
        def _():
            dw_ref[...] = jnp.zeros_like(dw_ref)

        cg = cg_ref[...].astype(F32)
        u = u_ref[...].astype(F32)
        vv = cg * u
        halo = cgh_ref[...].astype(F32) * uh_ref[...].astype(F32)
        vbuf[0:HALO, :] = jnp.where(i > 0, halo, 0.0)
        vbuf[HALO:HALO + tm, :] = vv
        v1 = vbuf[HALO - 1:HALO - 1 + tm, :]
        v2 = vbuf[HALO - 2:HALO - 2 + tm, :]
        w = w_ref[...]
        y = w[2:3, :] * vv + w[1:2, :] * v1 + w[0:1, :] * v2
        dyc = dyc_ref[...].astype(F32)
        dbg_ref[...] = (dyc * y).astype(BF16)
        gy = dyc * bg_ref[...].astype(F32)
        nxt = dych_ref[...].astype(F32) * bgh_ref[...].astype(F32)
        gbuf[0:tm, :] = gy
        gbuf[tm:tm + HALO, :] = jnp.where(i < nt - 1, nxt, 0.0)
        g1 = gbuf[1:1 + tm, :]
        g2 = gbuf[2:2 + tm, :]
        dvv = w[2:3, :] * gy + w[1:2, :] * g1 + w[0:1, :] * g2
        dcg_ref[...] = (dvv * u).astype(BF16)
        du_ref[...] = (dvv * cg).astype(BF16)
        dw_ref[0:1, :] += _colsum(gy * v2)
        dw_ref[1:2, :] += _colsum(gy * v1)
        dw_ref[2:3, :] += _colsum(gy * vv)

    def prev(i):
        return jnp.maximum(i * nb - 1, 0)

    def nxt_blk(i):
        return jnp.minimum((i + 1) * nb, last_blk)

    def col(c):
        return pl.BlockSpec((tm, cw), lambda i: (i, c))

    return pl.pallas_call(
        body, name=name, grid=(nt,),
        in_specs=[col(0), col(0), col(1), col(2),
                  pl.BlockSpec((HALO, cw), lambda i: (prev(i), 1)), pl.BlockSpec((HALO, cw), lambda i: (prev(i), 2)),
                  pl.BlockSpec((HALO, cw), lambda i: (nxt_blk(i), 0)), pl.BlockSpec((HALO, cw), lambda i: (nxt_blk(i), 0)),
                  pl.BlockSpec((3, cw), lambda i: (0, 0))],
        out_specs=[col(0), col(0), col(0), pl.BlockSpec((3, cw), lambda i: (0, 0))],
        out_shape=[_sds((s, cw), BF16), _sds((s, cw), BF16), _sds((s, cw), BF16), _sds((3, cw), F32)],
        scratch_shapes=[pltpu.VMEM((HALO + tm, cw), F32), pltpu.VMEM((tm + HALO, cw), F32)],
        compiler_params=_params(("arbitrary",)),
    )(dyc, proj, proj, proj, proj, proj, dyc, proj, conv_w)


def _attn_bwd(proj, o, do, tri, *, d, name, qb=256):
    s = proj.shape[0]
    qb = min(qb, s)
    nq = s // qb
    q0, k0, v0, hp = _attn_cols(d)

    def body(q_ref, k_ref, v_ref, o_ref, do_ref, tri_ref, dq_ref, dk_ref, dv_ref, dk_acc, dv_acc):
        i = pl.program_id(1)

        @pl.when(i == 0)
        def _():
            dk_acc[...] = jnp.zeros_like(dk_acc)
            dv_acc[...] = jnp.zeros_like(dv_acc)

        row = lax.broadcasted_iota(jnp.int32, (qb, qb), 0)
        col = lax.broadcasted_iota(jnp.int32, (qb, qb), 1)
        causal = col < row
        tri_m = tri_ref[...]
        q = q_ref[...]
        dov = do_ref[...]
        ov = o_ref[...]
        dq_out = jnp.zeros((qb, LANES), F32)
        for h in range(2):
            hm = _head_mask(h)
            qm = jnp.where(hm, q * ATTN_SCALE, 0).astype(BF16)
            dom = jnp.where(hm, dov, 0).astype(BF16)
            dtot = jnp.sum(jnp.where(hm, dov.astype(F32) * ov, 0.0), axis=-1, keepdims=True)

            def block(j, carry, diag, qm=qm, dom=dom, dtot=dtot, hm=hm):
                run, grun, dq_acc = carry
                rows = pl.ds(pl.multiple_of(j * qb, qb), qb)
                kb = k_ref[rows, :]
                vb = v_ref[rows, :]
                z = lax.dot_general(qm, kb, _NT, preferred_element_type=F32)
                sp, beta = _softplus_parts(z)
                lg = -sp
                if diag:
                    lg = jnp.where(causal, lg, 0.0)
                hi, lo = _split_bf16(lg)
                cs = (jnp.dot(hi, tri_m, preferred_element_type=F32)
                      + jnp.dot(lo, tri_m, preferred_element_type=F32))
                a = jnp.exp(z + cs + run)
                if diag:
                    a = jnp.where(causal, a, 0.0)
                ab = a.astype(BF16)
                da = lax.dot_general(dom, vb, _NT, preferred_element_type=F32)
                gg = ab.astype(F32) * da
                ghi, glo = _split_bf16(gg)
                gcs = (jnp.dot(ghi, tri_m, preferred_element_type=F32)
                       + jnp.dot(glo, tri_m, preferred_element_type=F32))
                pre = (dtot - grun) - gcs
                dz = gg - beta * (gg + pre)
                if diag:
                    dz = jnp.where(causal, dz, 0.0)
                dzb = dz.astype(BF16)
                kbm = jnp.where(hm, kb, 0).astype(BF16)
                dq_acc = dq_acc + jnp.dot(dzb, kbm, preferred_element_type=F32)
                dk_acc[rows, :] += lax.dot_general(dzb, qm, _TN, preferred_element_type=F32)
                dv_acc[rows, :] += lax.dot_general(ab, dom, _TN, preferred_element_type=F32)
                return run + cs[:, 0:1], grun + gcs[:, 0:1], dq_acc

            zero = jnp.zeros((qb, 1), F32)
            carry = block(i, (zero, zero, jnp.zeros((qb, LANES), F32)), True)
            carry = lax.fori_loop(0, i, lambda t, cr: block(i - 1 - t, cr, False), carry)
            dq_out = dq_out + carry[2]
        dq_ref[...] = (dq_out * ATTN_SCALE).astype(BF16)

        @pl.when(i == nq - 1)
        def _():
            dk_ref[...] = dk_acc[...].astype(BF16)
            dv_ref[...] = dv_acc[...].astype(BF16)

    aw = hp * LANES
    return pl.pallas_call(
        body, name=name, grid=(hp, nq),
        in_specs=[pl.BlockSpec((qb, LANES), lambda p, i: (i, q0 + p)),
                  pl.BlockSpec((s, LANES), lambda p, i: (0, k0 + p)),
                  pl.BlockSpec((s, LANES), lambda p, i: (0, v0 + p)),
                  pl.BlockSpec((qb, LANES), lambda p, i: (i, p)),
                  pl.BlockSpec((qb, LANES), lambda p, i: (i, p)),
                  pl.BlockSpec((qb, qb), lambda p, i: (0, 0))],
        out_specs=[pl.BlockSpec((qb, LANES), lambda p, i: (i, p)),
                   pl.BlockSpec((s, LANES), lambda p, i: (0, p)),
                   pl.BlockSpec((s, LANES), lambda p, i: (0, p))],
        out_shape=[_sds((s, aw), BF16), _sds((s, aw), BF16), _sds((s, aw), BF16)],
        scratch_shapes=[pltpu.VMEM((s, LANES), F32), pltpu.VMEM((s, LANES), F32)],
        compiler_params=_params(("arbitrary", "arbitrary")),
    )(proj, proj, proj, o, do, tri)


def _layer_fwd(x, mod, gains, w, tri, *, tag):
    sh1, sc1, gt1, sh2, sc2, gt2 = mod
    g_pre_mix, g_post_mix, g_pre_mlp, g_post_mlp = gains
    d = x.shape[1]
    h, proj = _norm_mod_matmul(x, g_pre_mix, sc1, sh1, w["w_in"], name=f"in_proj_{tag}")
    yc = _conv_fwd(proj, w["conv_w"], name=f"conv_fwd_{tag}")
    o = _attn_fwd(proj, tri, d=d, name=f"attn_fwd_{tag}")
    ycv, yat, merged, mix, x1 = _mix_out(yc, o, proj, x, w["w_proj_conv"], w["w_proj_attn"], w["w_out"],
                                         g_post_mix, gt1, name=f"mix_out_{tag}")
    h2, a = _norm_mod_matmul(x1, g_pre_mlp, sc2, sh2, w["w_mlp_in"], name=f"mlp_in_{tag}")
    ff, x2 = _mlp_out(a, x1, w["w_mlp_out"], g_post_mlp, gt2, name=f"mlp_out_{tag}")
    saved = dict(x=x, h=h, proj=proj, yc=yc, o=o, ycv=ycv, yat=yat, merged=merged, mix=mix, x1=x1, h2=h2, a=a, ff=ff)
    return x2, saved


def _layer_bwd(dx2, sv, mod, gains, w, tri, *, tag):
    sh1, sc1, gt1, sh2, sc2, gt2 = mod
    g_pre_mix, g_post_mix, g_pre_mlp, g_post_mlp = gains
    d = dx2.shape[1]
    dff, da, dgt2, dg_post_mlp = _mlp_out_bwd(dx2, sv["ff"], sv["a"], w["w_mlp_out"], g_post_mlp, gt2,
                                              name=f"mlp_out_bwd_{tag}")
    gw_mlp_out = _matmul_tn(sv["a"], dff, relu2=True, name=f"gw_mlp_out_{tag}")
    dx1, dsh2, dsc2, dg_pre_mlp = _matmul_nt_norm_bwd(da, w["w_mlp_in"], sv["x1"], dx2, g_pre_mlp, sc2,
                                                      name=f"mlp_in_bwd_{tag}")
    gw_mlp_in = _matmul_tn(sv["h2"], da, name=f"gw_mlp_in_{tag}")
    dmix, dycv, dyat, dyc, do, dga, dgb, dgt1, dg_post_mix = _mix_out_bwd(
        dx1, sv["mix"], sv["proj"], sv["ycv"], sv["yat"], w["w_out"], w["w_proj_conv"], w["w_proj_attn"],
        g_post_mix, gt1, name=f"mix_out_bwd_{tag}")
    gw_out = _matmul_tn(sv["merged"], dmix, name=f"gw_out_{tag}")
    gw_proj_conv = _matmul_tn(sv["yc"], dycv, name=f"gw_proj_conv_{tag}")
    gw_proj_attn = _matmul_tn(sv["o"].astype(BF16), dyat, name=f"gw_proj_attn_{tag}")
    dbg, dcg, du, g_conv_w = _conv_bwd(dyc, sv["proj"], w["conv_w"], name=f"conv_bwd_{tag}")
    dq, dk, dv = _attn_bwd(sv["proj"], sv["o"], do, tri, d=d, name=f"attn_bwd_{tag}")
    dproj = jnp.concatenate([dbg, dcg, du, dq, dk, dv, dga, dgb], axis=1)
    dx0, dsh1, dsc1, dg_pre_mix = _matmul_nt_norm_bwd(dproj, w["w_in"], sv["x"], dx1, g_pre_mix, sc1,
                                                      name=f"in_proj_bwd_{tag}")
    gw_in = _matmul_tn(sv["h"], dproj, name=f"gw_in_{tag}")
    grads = dict(w_in=gw_in, conv_w=g_conv_w, w_proj_conv=gw_proj_conv, w_proj_attn=gw_proj_attn, w_out=gw_out,
                 w_mlp_in=gw_mlp_in, w_mlp_out=gw_mlp_out)
    dmod = jnp.concatenate([dsh1, dsc1, dgt1, dsh2, dsc2, dgt2], axis=0)
    dgains = jnp.concatenate([dg_pre_mix, dg_post_mix, dg_pre_mlp, dg_post_mlp], axis=0)
    return dx0, grads, dmod, dgains


BIG = ("w_in", "w_proj_conv", "w_proj_attn", "w_out", "w_mlp_in", "w_mlp_out")
SHARD_AXIS = dict(w_in=2, w_proj_conv=2, w_proj_attn=2, w_out=1, w_mlp_in=2, w_mlp_out=1)


def _local_step(x, target, mods, gains, wfull, conv_w):
    depth = mods.shape[0]
    tri = _tri(min(256, x.shape[0]))
    saved = []
    for l in range(depth):
        wl = {k: wfull[k][l] for k in BIG}
        wl["conv_w"] = conv_w[l]
        mod = [mods[l, k:k + 1] for k in range(N_MOD)]
        gl = [gains[l, k:k + 1] for k in range(4)]
        x, sv = _layer_fwd(x, mod, gl, wl, tri, tag=str(l))
        saved.append((sv, mod, gl, wl))
    dx, loss = _loss_grad(x, target, name="loss_grad")
    grads, dmods, dgains = [], [], []
    for l in reversed(range(depth)):
        sv, mod, gl, wl = saved[l]
        dx, g, dmod, dgain = _layer_bwd(dx, sv, mod, gl, wl, tri, tag=str(l))
        grads.insert(0, g)
        dmods.insert(0, dmod)
        dgains.insert(0, dgain)
    gstack = {k: jnp.stack([g[k] for g in grads]) for k in grads[0]}
    return loss, dx, gstack, jnp.stack(dmods), jnp.stack(dgains)


def _coords():
    return lax.axis_index("x"), lax.axis_index("y"), lax.axis_index("c")


def _flip(v, f):
    return 1 - v if f else v


def _all_gather_small(v, *, name):
    r, c_ = v.shape

    def body(v_ref, out_ref, send_sems, recv_sems, local_sem):
        x, y, c = _coords()
        me = 4 * x + 2 * y + c
        mine = pltpu.make_async_copy(v_ref, out_ref.at[me], local_sem)
        mine.start()
        copies = []
        for k in range(1, 8):
            fx, fy, fc = (k >> 2) & 1, (k >> 1) & 1, k & 1
            px, py, pc = _flip(x, fx), _flip(y, fy), _flip(c, fc)
            out = pltpu.make_async_remote_copy(src_ref=v_ref, dst_ref=out_ref.at[me], send_sem=send_sems.at[k - 1],
                                               recv_sem=recv_sems.at[k - 1], device_id=(px, py, pc), device_id_type=MESH)
            out.start()
            back = pltpu.make_async_remote_copy(src_ref=v_ref, dst_ref=out_ref.at[4 * px + 2 * py + pc],
                                                send_sem=send_sems.at[k - 1], recv_sem=recv_sems.at[k - 1],
                                                device_id=(px, py, pc), device_id_type=MESH)
            copies.append((out, back))
        for out, back in copies:
            back.wait_recv()
        for out, back in copies:
            out.wait_send()
        mine.wait()

    return pl.pallas_call(
        body, name=name,
        in_specs=[pl.BlockSpec(memory_space=pltpu.VMEM)],
        out_specs=pl.BlockSpec(memory_space=pltpu.VMEM),
        out_shape=_sds((8, r, c_), F32),
        scratch_shapes=[pltpu.SemaphoreType.DMA((7,)), pltpu.SemaphoreType.DMA((7,)), pltpu.SemaphoreType.DMA],
    )(v)


def _shard_window(ref, axis, chip, half, rows, cols):
    r0, rn = (0, rows) if half is None else (half * (rows // 2), rows // 2)
    if axis == 2:
        return ref.at[:, pl.ds(r0, rn), pl.ds(chip * cols, cols)]
    return ref.at[:, pl.ds(chip * rows + r0, rn), :]


def _gather_weights(shards, axes, *, name):
    n = len(shards)
    out_shapes = []
    for sh, ax in zip(shards, axes):
        l, r, c_ = sh.shape
        out_shapes.append(_sds((l, r * 4, c_) if ax == 1 else (l, r, c_ * 4), sh.dtype))

    def body(*refs):
        ins, outs = refs[:n], refs[n:2 * n]
        send_sems, recv_sems, local_sems = refs[2 * n:]
        x, y, c = _coords()
        chip = 2 * x + y
        sibling = (x, y, 1 - c)
        started = []
        landing = []
        for w in range(n):
            _, rows, cols = ins[w].shape
            win = functools.partial(_shard_window, outs[w], axes[w], rows=rows, cols=cols)
            mine = pltpu.make_async_copy(ins[w], win(chip, None), local_sems.at[w])
            mine.start()
            started.append(mine)
            my_half = ins[w].at[:, pl.ds(c * (rows // 2), rows // 2), :]
            for j, (fx, fy) in enumerate(OTHER_CHIPS):
                px, py = _flip(x, fx), _flip(y, fy)
                send = pltpu.make_async_remote_copy(src_ref=my_half, dst_ref=win(chip, c), send_sem=send_sems.at[w, j],
                                                    recv_sem=recv_sems.at[w, j], device_id=(px, py, c), device_id_type=MESH)
                send.start()
                started.append(send)
                landing.append((w, j, 2 * px + py, win))
        passed = []
        for w, j, pchip, win in landing:
            got = pltpu.make_async_remote_copy(src_ref=win(pchip, c), dst_ref=win(pchip, c), send_sem=send_sems.at[w, 3 + j],
                                               recv_sem=recv_sems.at[w, j], device_id=sibling, device_id_type=MESH)
            got.wait_recv()
            on = pltpu.make_async_remote_copy(src_ref=win(pchip, c), dst_ref=win(pchip, c), send_sem=send_sems.at[w, 3 + j],
                                              recv_sem=recv_sems.at[w, 3 + j], device_id=sibling, device_id_type=MESH)
            on.start()
            passed.append(on)
        for w, j, pchip, win in landing:
            other = pltpu.make_async_remote_copy(src_ref=win(pchip, 1 - c), dst_ref=win(pchip, 1 - c),
                                                 send_sem=send_sems.at[w, 3 + j], recv_sem=recv_sems.at[w, 3 + j],
                                                 device_id=sibling, device_id_type=MESH)
            other.wait_recv()
        for cp in passed:
            cp.wait_send()
        for w in range(n):
            started[4 * w].wait()
            for j in range(3):
                started[4 * w + 1 + j].wait_send()

    hbm = pl.BlockSpec(memory_space=pltpu.HBM)
    return pl.pallas_call(
        body, name=name,
        in_specs=[hbm] * n, out_specs=[hbm] * n, out_shape=out_shapes,
        scratch_shapes=[pltpu.SemaphoreType.DMA((n, 6)), pltpu.SemaphoreType.DMA((n, 6)), pltpu.SemaphoreType.DMA((n,))],
    )(*shards)


def _piece_shape(g, axis):
    l, k, n = g.shape
    return (l, k // 2, n // 4) if axis == 2 else (l, k // 8, n)


def _rs_to_sibling(grads, axes, *, name):
    n = len(grads)
    out_shapes = []
    for g, ax in zip(grads, axes):
        ps = _piece_shape(g, ax)
        out_shapes += [_sds((4,) + ps, g.dtype), _sds((4,) + ps, g.dtype)]

    def body(*refs):
        ins, outs = refs[:n], refs[n:3 * n]
        send_sems, recv_sems, local_sems = refs[3 * n:]
        x, y, c = _coords()
        sibling = (x, y, 1 - c)
        copies = []
        for w in range(n):
            _, rows2, cols = outs[2 * w].shape[1:]
            for j in range(4):
                win = functools.partial(_shard_window, ins[w], axes[w], j, rows=rows2 * 2, cols=cols)
                keep = pltpu.make_async_copy(win(c), outs[2 * w].at[j], local_sems.at[w, j])
                keep.start()
                give = pltpu.make_async_remote_copy(src_ref=win(1 - c), dst_ref=outs[2 * w + 1].at[j],
                                                    send_sem=send_sems.at[w, j], recv_sem=recv_sems.at[w, j],
                                                    device_id=sibling, device_id_type=MESH)
                give.start()
                copies.append((keep, give))
        for keep, give in copies:
            give.wait_recv()
        for keep, give in copies:
            give.wait_send()
            keep.wait()

    hbm = pl.BlockSpec(memory_space=pltpu.HBM)
    return pl.pallas_call(
        body, name=name,
        in_specs=[hbm] * n, out_specs=[hbm] * (2 * n), out_shape=out_shapes,
        scratch_shapes=[pltpu.SemaphoreType.DMA((n, 4)), pltpu.SemaphoreType.DMA((n, 4)), pltpu.SemaphoreType.DMA((n, 4))],
    )(*grads)


def _rs_to_chips(pieces, *, name):
    n = len(pieces)
    out_shapes = [_sds(p.shape, p.dtype) for p in pieces]

    def body(*refs):
        ins, outs = refs[:n], refs[n:2 * n]
        send_sems, recv_sems, local_sems = refs[2 * n:]
        x, y, c = _coords()
        chip = 2 * x + y
        copies = []
        for w in range(n):
            keep = pltpu.make_async_copy(ins[w].at[chip], outs[w].at[chip], local_sems.at[w])
            keep.start()
            copies.append(keep)
            for j, (fx, fy) in enumerate(OTHER_CHIPS):
                px, py = _flip(x, fx), _flip(y, fy)
                pchip = 2 * px + py
                send = pltpu.make_async_remote_copy(src_ref=ins[w].at[pchip], dst_ref=outs[w].at[chip],
                                                    send_sem=send_sems.at[w, j], recv_sem=recv_sems.at[w, j],
                                                    device_id=(px, py, c), device_id_type=MESH)
                send.start()
                recv = pltpu.make_async_remote_copy(src_ref=ins[w].at[pchip], dst_ref=outs[w].at[pchip],
                                                    send_sem=send_sems.at[w, j], recv_sem=recv_sems.at[w, j],
                                                    device_id=(px, py, c), device_id_type=MESH)
                copies.append((send, recv))
        for cp in copies:
            if isinstance(cp, tuple):
                cp[1].wait_recv()
        for cp in copies:
            if isinstance(cp, tuple):
                cp[0].wait_send()
            else:
                cp.wait()

    hbm = pl.BlockSpec(memory_space=pltpu.HBM)
    return pl.pallas_call(
        body, name=name,
        in_specs=[hbm] * n, out_specs=[hbm] * n, out_shape=out_shapes,
        scratch_shapes=[pltpu.SemaphoreType.DMA((n, 3)), pltpu.SemaphoreType.DMA((n, 3)), pltpu.SemaphoreType.DMA((n,))],
    )(*pieces)


def _rs_join_halves(halves, *, name):
    n = len(halves)
    out_shapes = [_sds((h.shape[0], 2 * h.shape[1], h.shape[2]), h.dtype) for h in halves]

    def body(*refs):
        ins, outs = refs[:n], refs[n:2 * n]
        send_sems, recv_sems, local_sems = refs[2 * n:]
        x, y, c = _coords()
        sibling = (x, y, 1 - c)
        copies = []
        for w in range(n):
            rows2 = ins[w].shape[1]
            mine = outs[w].at[:, pl.ds(c * rows2, rows2), :]
            theirs = outs[w].at[:, pl.ds((1 - c) * rows2, rows2), :]
            keep = pltpu.make_async_copy(ins[w], mine, local_sems.at[w])
            keep.start()
            send = pltpu.make_async_remote_copy(src_ref=ins[w], dst_ref=mine, send_sem=send_sems.at[w],
                                                recv_sem=recv_sems.at[w], device_id=sibling, device_id_type=MESH)
            send.start()
            recv = pltpu.make_async_remote_copy(src_ref=ins[w], dst_ref=theirs, send_sem=send_sems.at[w],
                                                recv_sem=recv_sems.at[w], device_id=sibling, device_id_type=MESH)
            copies.append((keep, send, recv))
        for keep, send, recv in copies:
            recv.wait_recv()
        for keep, send, recv in copies:
            send.wait_send()
            keep.wait()

    hbm = pl.BlockSpec(memory_space=pltpu.HBM)
    return pl.pallas_call(
        body, name=name,
        in_specs=[hbm] * n, out_specs=[hbm] * n, out_shape=out_shapes,
        scratch_shapes=[pltpu.SemaphoreType.DMA((n,)), pltpu.SemaphoreType.DMA((n,)), pltpu.SemaphoreType.DMA((n,))],
    )(*halves)


def _flat_rows(shape):
    rows = 1
    for s in shape[:-1]:
        rows *= s
    return rows, shape[-1]


def _row_tile(rows, cols, cap_bytes=2 * 1024 * 1024):
    t = rows
    while t * cols * 4 > cap_bytes and t % 16 == 0:
        t //= 2
    return t


def _add2(a, b, *, name):
    shape = a.shape
    rows, cols = _flat_rows(shape)
    tr = _row_tile(rows, cols)

    def body(a_ref, b_ref, o_ref):
        o_ref[...] = a_ref[...] + b_ref[...]

    spec = pl.BlockSpec((tr, cols), lambda i: (i, 0))
    out = pl.pallas_call(body, name=name, grid=(rows // tr,), in_specs=[spec, spec], out_specs=spec,
                         out_shape=_sds((rows, cols), a.dtype), compiler_params=_params(("parallel",)),
                         )(a.reshape(rows, cols), b.reshape(rows, cols))
    return out.reshape(shape)


def _sum_slots(p, *, name):
    k = p.shape[0]
    shape = p.shape[1:]
    rows, cols = _flat_rows(shape)
    tr = _row_tile(rows, cols, cap_bytes=1024 * 1024)

    def body(p_ref, o_ref):
        acc = p_ref[0]
        for j in range(1, k):
            acc = acc + p_ref[j]
        o_ref[...] = acc

    out = pl.pallas_call(body, name=name, grid=(rows // tr,),
                         in_specs=[pl.BlockSpec((k, tr, cols), lambda i: (0, i, 0))],
                         out_specs=pl.BlockSpec((tr, cols), lambda i: (i, 0)),
                         out_shape=_sds((rows, cols), p.dtype), compiler_params=_params(("parallel",)),
                         )(p.reshape(k, rows, cols))
    return out.reshape(shape)


def _reduce_scatter(grads, axes):
    n = len(grads)
    kept_got = _rs_to_sibling(grads, axes, name="rs_to_sibling")
    pair = [_add2(kept_got[2 * w], kept_got[2 * w + 1], name=f"rs_add_pair_{w}") for w in range(n)]
    slots = _rs_to_chips(pair, name="rs_to_chips")
    halves = [_sum_slots(slots[w], name=f"rs_sum_chips_{w}") for w in range(n)]
    return _rs_join_halves(halves, name="rs_join_halves")


def _ada_fwd(c_all, w_ada, b_loc, *, name, tn=512):
    l, d, nl = w_ada.shape
    b = c_all.shape[0]
    tn = min(tn, nl)

    def body(c_ref, w_ref, b_ref, o_ref):
        o_ref[0] = jnp.dot(c_ref[...], w_ref[0], preferred_element_type=F32,
                           precision=lax.Precision.HIGHEST) + b_ref[0]

    return pl.pallas_call(
        body, name=name, grid=(l, nl // tn),
        in_specs=[pl.BlockSpec((b, d), lambda i, j: (0, 0)), pl.BlockSpec((1, d, tn), lambda i, j: (i, 0, j)),
                  pl.BlockSpec((1, 1, tn), lambda i, j: (i, 0, j))],
        out_specs=pl.BlockSpec((1, b, tn), lambda i, j: (i, 0, j)),
        out_shape=_sds((l, b, nl), F32),
        compiler_params=_params(("parallel", "parallel")),
    )(c_all, w_ada, b_loc)


def _ada_bwd(c_t, dmod_loc, *, name, tn=512):
    d, b = c_t.shape
    l, _, nl = dmod_loc.shape
    tn = min(tn, nl)

    def body(c_ref, dm_ref, o_ref):
        cv = c_ref[...]
        dm = dm_ref[0]
        acc = cv[:, 0:1] * dm[0:1, :]
        for k in range(1, b):
            acc = acc + cv[:, k:k + 1] * dm[k:k + 1, :]
        o_ref[0] = acc

    return pl.pallas_call(
        body, name=name, grid=(l, nl // tn),
        in_specs=[pl.BlockSpec((d, b), lambda i, j: (0, 0)), pl.BlockSpec((1, b, tn), lambda i, j: (i, 0, j))],
        out_specs=pl.BlockSpec((1, d, tn), lambda i, j: (i, 0, j)),
        out_shape=_sds((l, d, nl), F32),
        compiler_params=_params(("parallel", "parallel")),
    )(c_t, dmod_loc)


def _sum_devices(p, *, name):
    k, r, c_ = p.shape

    def body(p_ref, o_ref):
        acc = p_ref[0]
        for j in range(1, k):
            acc = acc + p_ref[j]
        o_ref[...] = acc

    return pl.pallas_call(body, name=name, out_shape=_sds((r, c_), F32),
                          in_specs=[pl.BlockSpec(memory_space=pltpu.VMEM)],
                          out_specs=pl.BlockSpec(memory_space=pltpu.VMEM))(p)


def _adamw(w, g, m, v, *, name):
    shape = w.shape
    rows, cols = _flat_rows(shape)
    tr = _row_tile(rows, cols, cap_bytes=1024 * 1024)
    c1 = 1.0 / (1.0 - ADAM_B1 ** ADAM_STEP)
    c2 = 1.0 / (1.0 - ADAM_B2 ** ADAM_STEP)

    def body(w_ref, g_ref, m_ref, v_ref, d_ref, nm_ref, nv_ref):
        gv = g_ref[...]
        nm = ADAM_B1 * m_ref[...] + (1.0 - ADAM_B1) * gv
        nv = ADAM_B2 * v_ref[...] + (1.0 - ADAM_B2) * (gv * gv)
        m_hat = nm * c1
        v_hat = nv * c2
        d_ref[...] = -ADAM_LR * (m_hat / (jnp.sqrt(v_hat) + ADAM_EPS) + ADAM_WD * w_ref[...])
        nm_ref[...] = nm
        nv_ref[...] = nv

    spec = pl.BlockSpec((tr, cols), lambda i: (i, 0))
    flat = lambda a: a.reshape(rows, cols)
    outs = pl.pallas_call(body, name=name, grid=(rows // tr,), in_specs=[spec] * 4, out_specs=[spec] * 3,
                          out_shape=[_sds((rows, cols), F32)] * 3, compiler_params=_params(("parallel",)),
                          )(flat(w), flat(g), flat(m), flat(v))
    return tuple(o.reshape(shape) for o in outs)


WEIGHTS = ("w_ada", "b_ada", "g_pre_mix", "g_post_mix", "g_pre_mlp", "g_post_mlp", "w_in", "conv_w",
           "w_proj_conv", "w_proj_attn", "w_out", "w_mlp_in", "w_mlp_out")
GAINS = ("g_pre_mix", "g_post_mix", "g_pre_mlp", "g_post_mlp")


def kernel(x, c, w_ada, b_ada, g_pre_mix, g_post_mix, g_pre_mlp, g_post_mlp, w_in, conv_w, w_proj_conv, w_proj_attn, w_out, w_mlp_in, w_mlp_out, loss_target, m_w_ada, m_b_ada, m_g_pre_mix, m_g_post_mix, m_g_pre_mlp, m_g_post_mlp, m_w_in, m_conv_w, m_w_proj_conv, m_w_proj_attn, m_w_out, m_w_mlp_in, m_w_mlp_out, v_w_ada, v_b_ada, v_g_pre_mix, v_g_post_mix, v_g_pre_mlp, v_g_post_mlp, v_w_in, v_conv_w, v_w_proj_conv, v_w_proj_attn, v_w_out, v_w_mlp_in, v_w_mlp_out):
    params = dict(w_ada=w_ada, b_ada=b_ada, g_pre_mix=g_pre_mix, g_post_mix=g_post_mix, g_pre_mlp=g_pre_mlp,
                  g_post_mlp=g_post_mlp, w_in=w_in, conv_w=conv_w, w_proj_conv=w_proj_conv, w_proj_attn=w_proj_attn,
                  w_out=w_out, w_mlp_in=w_mlp_in, w_mlp_out=w_mlp_out)
    m_in = dict(w_ada=m_w_ada, b_ada=m_b_ada, g_pre_mix=m_g_pre_mix, g_post_mix=m_g_post_mix, g_pre_mlp=m_g_pre_mlp,
                g_post_mlp=m_g_post_mlp, w_in=m_w_in, conv_w=m_conv_w, w_proj_conv=m_w_proj_conv,
                w_proj_attn=m_w_proj_attn, w_out=m_w_out, w_mlp_in=m_w_mlp_in, w_mlp_out=m_w_mlp_out)
    v_in = dict(w_ada=v_w_ada, b_ada=v_b_ada, g_pre_mix=v_g_pre_mix, g_post_mix=v_g_post_mix, g_pre_mlp=v_g_pre_mlp,
                g_post_mlp=v_g_post_mlp, w_in=v_w_in, conv_w=v_conv_w, w_proj_conv=v_w_proj_conv,
                w_proj_attn=v_w_proj_attn, w_out=v_w_out, w_mlp_in=v_w_mlp_in, w_mlp_out=v_w_mlp_out)

    depth, d, nl_ada = w_ada.shape
    ix, iy, ic = lax.axis_index("x"), lax.axis_index("y"), lax.axis_index("c")
    chip = 2 * ix + iy
    me = 4 * ix + 2 * iy + ic
    xs = x[0]
    target = loss_target[0]

    c_all = _all_gather_small(jnp.broadcast_to(c, (8, d)), name="gather_c")[:, 0, :]
    b_loc = lax.dynamic_slice_in_dim(b_ada, chip * nl_ada, nl_ada, axis=1)[:, None, :]
    mod_loc = _ada_fwd(c_all, w_ada, b_loc, name="ada_fwd")
    mod_all = _all_gather_small(mod_loc.reshape(depth * 8, nl_ada), name="gather_mod")
    mod_all = mod_all.reshape(4, 2, depth, 8, nl_ada)[:, 0]
    mod_me = lax.dynamic_index_in_dim(mod_all, me, axis=2, keepdims=False)
    mods = jnp.transpose(mod_me, (1, 0, 2)).reshape(depth, N_MOD, d)

    shards = [params[k].astype(BF16) for k in BIG]
    full = _gather_weights(shards, [SHARD_AXIS[k] for k in BIG], name="gather_weights")
    wfull = dict(zip(BIG, full))
    conv_full = _all_gather_small(
        jnp.pad(conv_w.reshape(depth * 3, -1), ((0, 8 - depth * 3), (0, 0))), name="gather_conv_w")
    conv_full = conv_full.reshape(4, 2, 8, -1)[:, 0, :depth * 3]
    conv_full = jnp.transpose(conv_full, (1, 0, 2)).reshape(depth, 3, -1)

    gains = jnp.stack([params[k] for k in GAINS], axis=1)
    loss, dx, gbig, dmods, dgains = _local_step(xs, target, mods, gains, wfull, conv_full)

    cw = conv_full.shape[2]
    rows = [dmods.reshape(depth * N_MOD, d), dgains.reshape(depth * 4, d),
            gbig["conv_w"].reshape(-1, d), jnp.broadcast_to(loss, (1, d))]
    payload = jnp.concatenate(rows, axis=0)
    n_rows = payload.shape[0]
    pad = (-n_rows) % 8
    payload = jnp.pad(payload, ((0, pad), (0, 0)))
    everyone = _all_gather_small(payload, name="gather_small_grads")
    total = _sum_devices(everyone, name="sum_small_grads")
    r0 = depth * N_MOD
    grads = {}
    grads["b_ada"] = total[:r0].reshape(depth, N_MOD * d)
    gsum = total[r0:r0 + depth * 4].reshape(depth, 4, d)
    for k, name in enumerate(GAINS):
        grads[name] = gsum[:, k]
    r1 = r0 + depth * 4
    n_conv = (depth * 3 * cw) // d
    conv_g = total[r1:r1 + n_conv].reshape(depth, 3, cw)
    grads["conv_w"] = lax.dynamic_slice_in_dim(conv_g, chip * (cw // 4), cw // 4, axis=2)
    loss_out = total[r1 + n_conv, 0]
    dmod_all = everyone[:, :r0].reshape(8, depth, N_MOD * d)
    dmod_loc = lax.dynamic_slice_in_dim(dmod_all, chip * nl_ada, nl_ada, axis=2)
    grads["w_ada"] = _ada_bwd(c_all.T, jnp.transpose(dmod_loc, (1, 0, 2)), name="ada_bwd")

    reduced = _reduce_scatter([gbig[k] for k in BIG], [SHARD_AXIS[k] for k in BIG])
    for k, g in zip(BIG, reduced):
        grads[k] = g

    deltas, new_m, new_v = {}, {}, {}
    for k in WEIGHTS:
        deltas[k], new_m[k], new_v[k] = _adamw(params[k], grads[k], m_in[k], v_in[k], name=f"adamw_{k}")

    return (loss_out, dx[None], *[grads[k] for k in WEIGHTS], *[deltas[k] for k in WEIGHTS],
            *[new_m[k] for k in WEIGHTS], *[new_v[k] for k in WEIGHTS])
```

```python
import functools

import jax
import jax.numpy as jnp
from jax import lax
from jax.experimental import pallas as pl
from jax.experimental.pallas import tpu as pltpu

F32 = jnp.float32
BF16 = jnp.bfloat16
EPS = 1e-6
N_MOD = 6
HEAD_DIM = 64
LANES = 128
ATTN_SCALE = 1.0 / 8.0
UNDERFLOW_LOG = -90.0
ADAM_LR = 0.001
ADAM_B1 = 0.9
ADAM_B2 = 0.999
ADAM_EPS = 1e-08
ADAM_WD = 0.01
ADAM_STEP = 10
VMEM_LIMIT = 56 * 1024 * 1024
MESH = pl.DeviceIdType.MESH
OTHER_CHIPS = ((1, 0), (0, 1), (1, 1))

_NT = (((1,), (1,)), ((), ()))
_TN = (((0,), (0,)), ((), ()))


def _sds(shape, dtype):
    return jax.ShapeDtypeStruct(shape, dtype)


def _params(sem):
    return pltpu.CompilerParams(dimension_semantics=sem, vmem_limit_bytes=VMEM_LIMIT)


def _fit(t, n):
    t = min(t, n)
    while n % t:
        t //= 2
    return t


def _vec_spec(d, nargs):
    if nargs == 1:
        return pl.BlockSpec((1, d), lambda i: (0, 0))
    return pl.BlockSpec((1, d), lambda i, j: (0, 0))


def _softplus_parts(z):
    t = jnp.exp(-jnp.abs(z))
    sp = jnp.maximum(z, 0.0) + jnp.log(1.0 + t)
    sig = jnp.where(z >= 0.0, 1.0, t) / (1.0 + t)
    return sp, sig


def _sigmoid(z):
    t = jnp.exp(-jnp.abs(z))
    return jnp.where(z >= 0.0, 1.0, t) / (1.0 + t)


def _split_bf16(a):
    hi = a.astype(BF16)
    lo = (a - hi.astype(F32)).astype(BF16)
    return hi, lo


def _rms_bwd(dn, xin, g):
    r = lax.rsqrt(jnp.mean(xin * xin, axis=-1, keepdims=True) + EPS)
    xh = xin * r
    dxh = dn * g
    dxin = r * (dxh - xh * jnp.mean(dxh * xh, axis=-1, keepdims=True))
    return dxin, xh


def _colsum(a):
    return jnp.sum(a, axis=0, keepdims=True)


def _norm_mod_matmul(x, g, sc, sh, w, *, name, tm=1024, tn=512):
    s, d = x.shape
    n = w.shape[1]
    tm, tn = _fit(tm, s), _fit(tn, n)

    def body(x_ref, g_ref, sc_ref, sh_ref, w_ref, h_ref, o_ref):
        @pl.when(pl.program_id(1) == 0)
        def _():
            xv = x_ref[...]
            r = lax.rsqrt(jnp.mean(xv * xv, axis=-1, keepdims=True) + EPS)
            h_ref[...] = ((xv * r * g_ref[...]) * (1.0 + sc_ref[...]) + sh_ref[...]).astype(BF16)
        o_ref[...] = jnp.dot(h_ref[...], w_ref[...], preferred_element_type=F32).astype(BF16)

    return pl.pallas_call(
        body, name=name, grid=(s // tm, n // tn),
        in_specs=[pl.BlockSpec((tm, d), lambda i, j: (i, 0)), _vec_spec(d, 2), _vec_spec(d, 2), _vec_spec(d, 2),
                  pl.BlockSpec((d, tn), lambda i, j: (0, j))],
        out_specs=[pl.BlockSpec((tm, d), lambda i, j: (i, 0)), pl.BlockSpec((tm, tn), lambda i, j: (i, j))],
        out_shape=[_sds((s, d), BF16), _sds((s, n), BF16)],
        compiler_params=_params(("parallel", "arbitrary")),
    )(x, g, sc, sh, w)


HALO = 16


def _conv_fwd(proj, conv_w, *, name, tm=512):
    s = proj.shape[0]
    cw = conv_w.shape[1]
    tm = min(tm, s)
    nb = tm // HALO

    def body(bg_ref, cg_ref, u_ref, cgh_ref, uh_ref, w_ref, yc_ref, vbuf):
        i = pl.program_id(0)
        vv = cg_ref[...].astype(F32) * u_ref[...].astype(F32)
        halo = cgh_ref[...].astype(F32) * uh_ref[...].astype(F32)
        vbuf[0:HALO, :] = jnp.where(i > 0, halo, 0.0)
        vbuf[HALO:HALO + tm, :] = vv
        v1 = vbuf[HALO - 1:HALO - 1 + tm, :]
        v2 = vbuf[HALO - 2:HALO - 2 + tm, :]
        w = w_ref[...]
        y = w[2:3, :] * vv + w[1:2, :] * v1 + w[0:1, :] * v2
        yc_ref[...] = (bg_ref[...].astype(F32) * y).astype(BF16)

    def prev(i):
        return jnp.maximum(i * nb - 1, 0)

    return pl.pallas_call(
        body, name=name, grid=(s // tm,),
        in_specs=[pl.BlockSpec((tm, cw), lambda i: (i, 0)), pl.BlockSpec((tm, cw), lambda i: (i, 1)),
                  pl.BlockSpec((tm, cw), lambda i: (i, 2)),
                  pl.BlockSpec((HALO, cw), lambda i: (prev(i), 1)), pl.BlockSpec((HALO, cw), lambda i: (prev(i), 2)),
                  pl.BlockSpec((3, cw), lambda i: (0, 0))],
        out_specs=pl.BlockSpec((tm, cw), lambda i: (i, 0)),
        out_shape=_sds((s, cw), BF16),
        scratch_shapes=[pltpu.VMEM((HALO + tm, cw), F32)],
        compiler_params=_params(("arbitrary",)),
    )(proj, proj, proj, proj, proj, conv_w)


def _tri(qb):
    r = lax.broadcasted_iota(jnp.int32, (qb, qb), 0)
    c = lax.broadcasted_iota(jnp.int32, (qb, qb), 1)
    return (r >= c).astype(BF16)


def _head_mask(h):
    lane = lax.broadcasted_iota(jnp.int32, (1, LANES), 1)
    return (lane >= HEAD_DIM * h) & (lane < HEAD_DIM * (h + 1))


def _attn_cols(d):
    cw = d // 2
    hp = (d // 2) // LANES
    q0 = (3 * cw) // LANES
    return q0, q0 + hp, q0 + 2 * hp, hp


def _attn_fwd(proj, tri, *, d, name, qb=256):
    s = proj.shape[0]
    qb = min(qb, s)
    q0, k0, v0, hp = _attn_cols(d)

    def body(q_ref, k_ref, v_ref, tri_ref, o_ref):
        i = pl.program_id(1)
        row = lax.broadcasted_iota(jnp.int32, (qb, qb), 0)
        col = lax.broadcasted_iota(jnp.int32, (qb, qb), 1)
        causal = col < row
        tri_m = tri_ref[...]
        q = q_ref[...]
        out = jnp.zeros((qb, LANES), F32)
        for h in range(2):
            hm = _head_mask(h)
            qm = jnp.where(hm, q * ATTN_SCALE, 0).astype(BF16)

            def block(j, carry, diag, qm=qm, hm=hm):
                run, acc = carry
                rows = pl.ds(pl.multiple_of(j * qb, qb), qb)
                kb = k_ref[rows, :]
                vb = jnp.where(hm, v_ref[rows, :], 0).astype(BF16)
                z = lax.dot_general(qm, kb, _NT, preferred_element_type=F32)
                sp, _ = _softplus_parts(z)
                lg = -sp
                if diag:
                    lg = jnp.where(causal, lg, 0.0)
                hi, lo = _split_bf16(lg)
                cs = (jnp.dot(hi, tri_m, preferred_element_type=F32)
                      + jnp.dot(lo, tri_m, preferred_element_type=F32))
                a = jnp.exp(z + cs + run)
                if diag:
                    a = jnp.where(causal, a, 0.0)
                acc = acc + jnp.dot(a.astype(BF16), vb, preferred_element_type=F32)
                return run + cs[:, 0:1], acc

            carry = block(i, (jnp.zeros((qb, 1), F32), jnp.zeros((qb, LANES), F32)), True)
            carry = lax.while_loop(
                lambda st: (st[0] >= 0) & (jnp.max(st[1]) > UNDERFLOW_LOG),
                lambda st: (st[0] - 1, *block(st[0], st[1:], False)),
                (i - 1, *carry))
            out = out + carry[2]
        o_ref[...] = out

    return pl.pallas_call(
        body, name=name, grid=(hp, s // qb),
        in_specs=[pl.BlockSpec((qb, LANES), lambda p, i: (i, q0 + p)),
                  pl.BlockSpec((s, LANES), lambda p, i: (0, k0 + p)),
                  pl.BlockSpec((s, LANES), lambda p, i: (0, v0 + p)),
                  pl.BlockSpec((qb, qb), lambda p, i: (0, 0))],
        out_specs=pl.BlockSpec((qb, LANES), lambda p, i: (i, p)),
        out_shape=_sds((s, hp * LANES), F32),
        compiler_params=_params(("parallel", "arbitrary")),
    )(proj, proj, proj, tri)


def _mix_out(yc, o, proj, x, wpc, wpa, wout, g, gt, *, name, tm=256):
    s, d = x.shape
    cw = yc.shape[1]
    tm = min(tm, s)
    ga_blk = (3 * cw + 3 * (d // 2)) // d

    def body(yc_ref, o_ref, ga_ref, gb_ref, x_ref, wpc_ref, wpa_ref, wout_ref, g_ref, gt_ref,
             ycv_ref, yat_ref, mg_ref, mix_ref, x1_ref):
        y_conv = jnp.dot(yc_ref[...], wpc_ref[...], preferred_element_type=F32)
        y_attn = jnp.dot(o_ref[...].astype(BF16), wpa_ref[...], preferred_element_type=F32)
        merged = (_sigmoid(ga_ref[...].astype(F32)) * y_conv + _sigmoid(gb_ref[...].astype(F32)) * y_attn)
        mg = merged.astype(BF16)
        mix = jnp.dot(mg, wout_ref[...], preferred_element_type=F32)
        r = lax.rsqrt(jnp.mean(mix * mix, axis=-1, keepdims=True) + EPS)
        ycv_ref[...] = y_conv.astype(BF16)
        yat_ref[...] = y_attn.astype(BF16)
        mg_ref[...] = mg
        mix_ref[...] = mix
        x1_ref[...] = x_ref[...] + gt_ref[...] * (mix * r * g_ref[...])

    def rows(w):
        return pl.BlockSpec((tm, w), lambda i: (i, 0))

    def full(a):
        return pl.BlockSpec(a.shape, lambda i: (0, 0))

    return pl.pallas_call(
        body, name=name, grid=(s // tm,),
        in_specs=[rows(cw), rows(d // 2), pl.BlockSpec((tm, d), lambda i: (i, ga_blk)),
                  pl.BlockSpec((tm, d), lambda i: (i, ga_blk + 1)), rows(d),
                  full(wpc), full(wpa), full(wout), _vec_spec(d, 1), _vec_spec(d, 1)],
        out_specs=[rows(d), rows(d), rows(d), rows(d), rows(d)],
        out_shape=[_sds((s, d), BF16), _sds((s, d), BF16), _sds((s, d), BF16), _sds((s, d), F32), _sds((s, d), F32)],
        compiler_params=_params(("parallel",)),
    )(yc, o, proj, proj, x, wpc, wpa, wout, g, gt)


def _relu2(a):
    r = jnp.maximum(a.astype(F32), 0.0)
    return (r * r).astype(BF16)


def _mlp_out(a, x, w2, g, gt, *, name, tm=512):
    s, d = x.shape
    dff = a.shape[1]
    tm = min(tm, s)

    def body(a_ref, x_ref, w_ref, g_ref, gt_ref, ff_ref, x2_ref):
        ff = jnp.dot(_relu2(a_ref[...]), w_ref[...], preferred_element_type=F32)
        r = lax.rsqrt(jnp.mean(ff * ff, axis=-1, keepdims=True) + EPS)
        ff_ref[...] = ff
        x2_ref[...] = x_ref[...] + gt_ref[...] * (ff * r * g_ref[...])

    return pl.pallas_call(
        body, name=name, grid=(s // tm,),
        in_specs=[pl.BlockSpec((tm, dff), lambda i: (i, 0)), pl.BlockSpec((tm, d), lambda i: (i, 0)),
                  pl.BlockSpec((dff, d), lambda i: (0, 0)), _vec_spec(d, 1), _vec_spec(d, 1)],
        out_specs=[pl.BlockSpec((tm, d), lambda i: (i, 0)), pl.BlockSpec((tm, d), lambda i: (i, 0))],
        out_shape=[_sds((s, d), F32), _sds((s, d), F32)],
        compiler_params=_params(("parallel",)),
    )(a, x, w2, g, gt)


def _loss_grad(y, target, *, name, tm=512):
    s, d = y.shape
    tm = min(tm, s)

    def body(y_ref, t_ref, dy_ref, loss_ref):
        @pl.when(pl.program_id(0) == 0)
        def _():
            loss_ref[...] = jnp.zeros_like(loss_ref)
        e = y_ref[...] - t_ref[...]
        dy_ref[...] = e * (1.0 / d)
        loss_ref[...] += 0.5 * jnp.sum(jnp.mean(e * e, axis=-1, keepdims=True), axis=0, keepdims=True)

    return pl.pallas_call(
        body, name=name, grid=(s // tm,),
        in_specs=[pl.BlockSpec((tm, d), lambda i: (i, 0)), pl.BlockSpec((tm, d), lambda i: (i, 0))],
        out_specs=[pl.BlockSpec((tm, d), lambda i: (i, 0)), pl.BlockSpec((1, 1), lambda i: (0, 0))],
        out_shape=[_sds((s, d), F32), _sds((1, 1), F32)],
        compiler_params=_params(("arbitrary",)),
    )(y, target)


def _mlp_out_bwd(dx, ff, a, w2, g, gt, *, name, tm=256):
    s, d = dx.shape
    dff = a.shape[1]
    tm = min(tm, s)

    def body(dx_ref, ff_ref, a_ref, w_ref, g_ref, gt_ref, dff_ref, da_ref, dgt_ref, dg_ref):
        @pl.when(pl.program_id(0) == 0)
        def _():
            dgt_ref[...] = jnp.zeros_like(dgt_ref)
            dg_ref[...] = jnp.zeros_like(dg_ref)
        dxv = dx_ref[...]
        dn = dxv * gt_ref[...]
        dffv, xh = _rms_bwd(dn, ff_ref[...], g_ref[...])
        dgt_ref[...] += _colsum(dxv * (xh * g_ref[...]))
        dg_ref[...] += _colsum(dn * xh)
        dffb = dffv.astype(BF16)
        dff_ref[...] = dffb
        drr = lax.dot_general(dffb, w_ref[...], _NT, preferred_element_type=F32)
        da_ref[...] = (drr * (2.0 * jnp.maximum(a_ref[...].astype(F32), 0.0))).astype(BF16)

    return pl.pallas_call(
        body, name=name, grid=(s // tm,),
        in_specs=[pl.BlockSpec((tm, d), lambda i: (i, 0)), pl.BlockSpec((tm, d), lambda i: (i, 0)),
                  pl.BlockSpec((tm, dff), lambda i: (i, 0)), pl.BlockSpec((dff, d), lambda i: (0, 0)),
                  _vec_spec(d, 1), _vec_spec(d, 1)],
        out_specs=[pl.BlockSpec((tm, d), lambda i: (i, 0)), pl.BlockSpec((tm, dff), lambda i: (i, 0)),
                   _vec_spec(d, 1), _vec_spec(d, 1)],
        out_shape=[_sds((s, d), BF16), _sds((s, dff), BF16), _sds((1, d), F32), _sds((1, d), F32)],
        compiler_params=_params(("arbitrary",)),
    )(dx, ff, a, w2, g, gt)


def _matmul_nt_norm_bwd(dy, w, x, dres, g, sc, *, name, tm=512, tn=512):
    s, n = dy.shape
    d = w.shape[0]
    tm, tn = _fit(tm, s), _fit(tn, n)
    nj = n // tn

    def body(dy_ref, w_ref, x_ref, dres_ref, g_ref, sc_ref, dx_ref, dsh_ref, dsc_ref, dg_ref, acc):
        i, j = pl.program_id(0), pl.program_id(1)

        @pl.when((i == 0) & (j == 0))
        def _():
            dsh_ref[...] = jnp.zeros_like(dsh_ref)
            dsc_ref[...] = jnp.zeros_like(dsc_ref)
            dg_ref[...] = jnp.zeros_like(dg_ref)

        @pl.when(j == 0)
        def _():
            acc[...] = jnp.zeros_like(acc)

        acc[...] += lax.dot_general(dy_ref[...], w_ref[...], _NT, preferred_element_type=F32)

        @pl.when(j == nj - 1)
        def _():
            dh = acc[...]
            dn = dh * (1.0 + sc_ref[...])
            dxin, xh = _rms_bwd(dn, x_ref[...], g_ref[...])
            dsh_ref[...] += _colsum(dh)
            dsc_ref[...] += _colsum(dh * (xh * g_ref[...]))
            dg_ref[...] += _colsum(dn * xh)
            dx_ref[...] = dres_ref[...] + dxin

    return pl.pallas_call(
        body, name=name, grid=(s // tm, nj),
        in_specs=[pl.BlockSpec((tm, tn), lambda i, j: (i, j)), pl.BlockSpec((d, tn), lambda i, j: (0, j)),
                  pl.BlockSpec((tm, d), lambda i, j: (i, 0)), pl.BlockSpec((tm, d), lambda i, j: (i, 0)),
                  _vec_spec(d, 2), _vec_spec(d, 2)],
        out_specs=[pl.BlockSpec((tm, d), lambda i, j: (i, 0)), _vec_spec(d, 2), _vec_spec(d, 2), _vec_spec(d, 2)],
        out_shape=[_sds((s, d), F32), _sds((1, d), F32), _sds((1, d), F32), _sds((1, d), F32)],
        scratch_shapes=[pltpu.VMEM((tm, d), F32)],
        compiler_params=_params(("arbitrary", "arbitrary")),
    )(dy, w, x, dres, g, sc)


def _matmul_tn(a, b, *, name, tk=1024, tn=1024, ts=512, relu2=False):
    s, k = a.shape
    n = b.shape[1]
    tk, tn, ts = _fit(tk, k), _fit(tn, n), _fit(ts, s)

    def body(a_ref, b_ref, o_ref):
        @pl.when(pl.program_id(2) == 0)
        def _():
            o_ref[...] = jnp.zeros_like(o_ref)
        av = a_ref[...]
        if relu2:
            av = _relu2(av)
        o_ref[...] += lax.dot_general(av, b_ref[...], _TN, preferred_element_type=F32)

    return pl.pallas_call(
        body, name=name, grid=(k // tk, n // tn, s // ts),
        in_specs=[pl.BlockSpec((ts, tk), lambda i, j, t: (t, i)), pl.BlockSpec((ts, tn), lambda i, j, t: (t, j))],
        out_specs=pl.BlockSpec((tk, tn), lambda i, j, t: (i, j)),
        out_shape=_sds((k, n), F32),
        compiler_params=_params(("parallel", "parallel", "arbitrary")),
    )(a, b)


def _mix_out_bwd(dx, mix, proj, ycv, yat, wout, wpc, wpa, g, gt, *, name, tm=256):
    s, d = dx.shape
    cw = wpc.shape[0]
    aw = wpa.shape[0]
    tm = min(tm, s)
    ga_blk = (3 * cw + 3 * aw) // d

    def body(dx_ref, mix_ref, ga_ref, gb_ref, ycv_ref, yat_ref, wout_ref, wpc_ref, wpa_ref, g_ref, gt_ref,
             dmix_ref, dycv_ref, dyat_ref, dyc_ref, do_ref, dga_ref, dgb_ref, dgt_ref, dg_ref):
        @pl.when(pl.program_id(0) == 0)
        def _():
            dgt_ref[...] = jnp.zeros_like(dgt_ref)
            dg_ref[...] = jnp.zeros_like(dg_ref)
        dxv = dx_ref[...]
        dn = dxv * gt_ref[...]
        dmix, xh = _rms_bwd(dn, mix_ref[...], g_ref[...])
        dgt_ref[...] += _colsum(dxv * (xh * g_ref[...]))
        dg_ref[...] += _colsum(dn * xh)
        dmixb = dmix.astype(BF16)
        dmix_ref[...] = dmixb
        dmerged = lax.dot_general(dmixb, wout_ref[...], _NT, preferred_element_type=F32)
        sga = _sigmoid(ga_ref[...].astype(F32))
        sgb = _sigmoid(gb_ref[...].astype(F32))
        dycv = (dmerged * sga).astype(BF16)
        dyat = (dmerged * sgb).astype(BF16)
        dycv_ref[...] = dycv
        dyat_ref[...] = dyat
        dga_ref[...] = (dmerged * ycv_ref[...].astype(F32) * (sga * (1.0 - sga))).astype(BF16)
        dgb_ref[...] = (dmerged * yat_ref[...].astype(F32) * (sgb * (1.0 - sgb))).astype(BF16)
        dyc_ref[...] = lax.dot_general(dycv, wpc_ref[...], _NT, preferred_element_type=F32).astype(BF16)
        do_ref[...] = lax.dot_general(dyat, wpa_ref[...], _NT, preferred_element_type=F32).astype(BF16)

    def rows(w):
        return pl.BlockSpec((tm, w), lambda i: (i, 0))

    def full(a):
        return pl.BlockSpec(a.shape, lambda i: (0, 0))

    return pl.pallas_call(
        body, name=name, grid=(s // tm,),
        in_specs=[rows(d), rows(d), pl.BlockSpec((tm, d), lambda i: (i, ga_blk)),
                  pl.BlockSpec((tm, d), lambda i: (i, ga_blk + 1)), rows(d), rows(d),
                  full(wout), full(wpc), full(wpa), _vec_spec(d, 1), _vec_spec(d, 1)],
        out_specs=[rows(d), rows(d), rows(d), rows(cw), rows(aw), rows(d), rows(d), _vec_spec(d, 1), _vec_spec(d, 1)],
        out_shape=[_sds((s, d), BF16), _sds((s, d), BF16), _sds((s, d), BF16), _sds((s, cw), BF16),
                   _sds((s, aw), BF16), _sds((s, d), BF16), _sds((s, d), BF16), _sds((1, d), F32), _sds((1, d), F32)],
        compiler_params=_params(("arbitrary",)),
    )(dx, mix, proj, proj, ycv, yat, wout, wpc, wpa, g, gt)


def _conv_bwd(dyc, proj, conv_w, *, name, tm=512):
    s = proj.shape[0]
    cw = conv_w.shape[1]
    tm = min(tm, s)
    nb = tm // HALO
    nt = s // tm
    last_blk = s // HALO - 1

    def body(dyc_ref, bg_ref, cg_ref, u_ref, cgh_ref, uh_ref, dych_ref, bgh_ref, w_ref,
             dbg_ref, dcg_ref, du_ref, dw_ref, vbuf, gbuf):
        i = pl.program_id(0)

        @pl.when(i == 0)
        def _():
            dw_ref[...] = jnp.zeros_like(dw_ref)

        cg = cg_ref[...].astype(F32)
        u = u_ref[...].astype(F32)
        vv = cg * u
        halo = cgh_ref[...].astype(F32) * uh_ref[...].astype(F32)
        vbuf[0:HALO, :] = jnp.where(i > 0, halo, 0.0)
        vbuf[HALO:HALO + tm, :] = vv
        v1 = vbuf[HALO - 1:HALO - 1 + tm, :]
        v2 = vbuf[HALO - 2:HALO - 2 + tm, :]
        w = w_ref[...]
        y = w[2:3, :] * vv + w[1:2, :] * v1 + w[0:1, :] * v2
        dyc = dyc_ref[...].astype(F32)
        dbg_ref[...] = (dyc * y).astype(BF16)
        gy = dyc * bg_ref[...].astype(F32)
        nxt = dych_ref[...].astype(F32) * bgh_ref[...].astype(F32)
        gbuf[0:tm, :] = gy
        gbuf[tm:tm + HALO, :] = jnp.where(i < nt - 1, nxt, 0.0)
        g1 = gbuf[1:1 + tm, :]
        g2 = gbuf[2:2 + tm, :]
        dvv = w[2:3, :] * gy + w[1:2, :] * g1 + w[0:1, :] * g2
        dcg_ref[...] = (dvv * u).astype(BF16)
        du_ref[...] = (dvv * cg).astype(BF16)
        dw_ref[0:1, :] += _colsum(gy * v2)
        dw_ref[1:2, :] += _colsum(gy * v1)
        dw_ref[2:3, :] += _colsum(gy * vv)

    def prev(i):
        return jnp.maximum(i * nb - 1, 0)

    def nxt_blk(i):
        return jnp.minimum((i + 1) * nb, last_blk)

    def col(c):
        return pl.BlockSpec((tm, cw), lambda i: (i, c))

    return pl.pallas_call(
        body, name=name, grid=(nt,),
        in_specs=[col(0), col(0), col(1), col(2),
                  pl.BlockSpec((HALO, cw), lambda i: (prev(i), 1)), pl.BlockSpec((HALO, cw), lambda i: (prev(i), 2)),
                  pl.BlockSpec((HALO, cw), lambda i: (nxt_blk(i), 0)), pl.BlockSpec((HALO, cw), lambda i: (nxt_blk(i), 0)),
                  pl.BlockSpec((3, cw), lambda i: (0, 0))],
        out_specs=[col(0), col(0), col(0), pl.BlockSpec((3, cw), lambda i: (0, 0))],
        out_shape=[_sds((s, cw), BF16), _sds((s, cw), BF16), _sds((s, cw), BF16), _sds((3, cw), F32)],
        scratch_shapes=[pltpu.VMEM((HALO + tm, cw), F32), pltpu.VMEM((tm + HALO, cw), F32)],
        compiler_params=_params(("arbitrary",)),
    )(dyc, proj, proj, proj, proj, proj, dyc, proj, conv_w)


def _attn_bwd(proj, o, do, tri, *, d, name, qb=256):
    s = proj.shape[0]
    qb = min(qb, s)
    nq = s // qb
    q0, k0, v0, hp = _attn_cols(d)

    def body(q_ref, k_ref, v_ref, o_ref, do_ref, tri_ref, dq_ref, dk_ref, dv_ref, dk_acc, dv_acc):
        i = pl.program_id(1)

        @pl.when(i == 0)
        def _():
            dk_acc[...] = jnp.zeros_like(dk_acc)
            dv_acc[...] = jnp.zeros_like(dv_acc)

        row = lax.broadcasted_iota(jnp.int32, (qb, qb), 0)
        col = lax.broadcasted_iota(jnp.int32, (qb, qb), 1)
        causal = col < row
        tri_m = tri_ref[...]
        q = q_ref[...]
        dov = do_ref[...]
        ov = o_ref[...]
        dq_out = jnp.zeros((qb, LANES), F32)
        for h in range(2):
            hm = _head_mask(h)
            qm = jnp.where(hm, q * ATTN_SCALE, 0).astype(BF16)
            dom = jnp.where(hm, dov, 0).astype(BF16)
            dtot = jnp.sum(jnp.where(hm, dov.astype(F32) * ov, 0.0), axis=-1, keepdims=True)

            def block(j, carry, diag, qm=qm, dom=dom, dtot=dtot, hm=hm):
                run, grun, dq_acc = carry
                rows = pl.ds(pl.multiple_of(j * qb, qb), qb)
                kb = k_ref[rows, :]
                vb = v_ref[rows, :]
                z = lax.dot_general(qm, kb, _NT, preferred_element_type=F32)
                sp, beta = _softplus_parts(z)
                lg = -sp
                if diag:
                    lg = jnp.where(causal, lg, 0.0)
                hi, lo = _split_bf16(lg)
                cs = (jnp.dot(hi, tri_m, preferred_element_type=F32)
                      + jnp.dot(lo, tri_m, preferred_element_type=F32))
                a = jnp.exp(z + cs + run)
                if diag:
                    a = jnp.where(causal, a, 0.0)
                ab = a.astype(BF16)
                da = lax.dot_general(dom, vb, _NT, preferred_element_type=F32)
                gg = ab.astype(F32) * da
                ghi, glo = _split_bf16(gg)
                gcs = (jnp.dot(ghi, tri_m, preferred_element_type=F32)
                       + jnp.dot(glo, tri_m, preferred_element_type=F32))
                pre = (dtot - grun) - gcs
                dz = gg - beta * (gg + pre)
                if diag:
                    dz = jnp.where(causal, dz, 0.0)
                dzb = dz.astype(BF16)
                kbm = jnp.where(hm, kb, 0).astype(BF16)
                dq_acc = dq_acc + jnp.dot(dzb, kbm, preferred_element_type=F32)
                dk_acc[rows, :] += lax.dot_general(dzb, qm, _TN, preferred_element_type=F32)
                dv_acc[rows, :] += lax.dot_general(ab, dom, _TN, preferred_element_type=F32)
                return run + cs[:, 0:1], grun + gcs[:, 0:1], dq_acc

            zero = jnp.zeros((qb, 1), F32)
            carry = block(i, (zero, zero, jnp.zeros((qb, LANES), F32)), True)
            carry = lax.while_loop(
                lambda st: (st[0] >= 0) & (jnp.max(st[1]) > UNDERFLOW_LOG),
                lambda st: (st[0] - 1, *block(st[0], st[1:], False)),
                (i - 1, *carry))
            dq_out = dq_out + carry[3]
        dq_ref[...] = (dq_out * ATTN_SCALE).astype(BF16)

        @pl.when(i == nq - 1)
        def _():
            dk_ref[...] = dk_acc[...].astype(BF16)
            dv_ref[...] = dv_acc[...].astype(BF16)

    aw = hp * LANES
    return pl.pallas_call(
        body, name=name, grid=(hp, nq),
        in_specs=[pl.BlockSpec((qb, LANES), lambda p, i: (i, q0 + p)),
                  pl.BlockSpec((s, LANES), lambda p, i: (0, k0 + p)),
                  pl.BlockSpec((s, LANES), lambda p, i: (0, v0 + p)),
                  pl.BlockSpec((qb, LANES), lambda p, i: (i, p)),
                  pl.BlockSpec((qb, LANES), lambda p, i: (i, p)),
                  pl.BlockSpec((qb, qb), lambda p, i: (0, 0))],
        out_specs=[pl.BlockSpec((qb, LANES), lambda p, i: (i, p)),
                   pl.BlockSpec((s, LANES), lambda p, i: (0, p)),
                   pl.BlockSpec((s, LANES), lambda p, i: (0, p))],
        out_shape=[_sds((s, aw), BF16), _sds((s, aw), BF16), _sds((s, aw), BF16)],
        scratch_shapes=[pltpu.VMEM((s, LANES), F32), pltpu.VMEM((s, LANES), F32)],
        compiler_params=_params(("arbitrary", "arbitrary")),
    )(proj, proj, proj, o, do, tri)


def _layer_fwd(x, mod, gains, w, tri, *, tag):
    sh1, sc1, gt1, sh2, sc2, gt2 = mod
    g_pre_mix, g_post_mix, g_pre_mlp, g_post_mlp = gains
    d = x.shape[1]
    h, proj = _norm_mod_matmul(x, g_pre_mix, sc1, sh1, w["w_in"], name=f"in_proj_{tag}")
    yc = _conv_fwd(proj, w["conv_w"], name=f"conv_fwd_{tag}")
    o = _attn_fwd(proj, tri, d=d, name=f"attn_fwd_{tag}")
    ycv, yat, merged, mix, x1 = _mix_out(yc, o, proj, x, w["w_proj_conv"], w["w_proj_attn"], w["w_out"],
                                         g_post_mix, gt1, name=f"mix_out_{tag}")
    h2, a = _norm_mod_matmul(x1, g_pre_mlp, sc2, sh2, w["w_mlp_in"], name=f"mlp_in_{tag}")
    ff, x2 = _mlp_out(a, x1, w["w_mlp_out"], g_post_mlp, gt2, name=f"mlp_out_{tag}")
    saved = dict(x=x, h=h, proj=proj, yc=yc, o=o, ycv=ycv, yat=yat, merged=merged, mix=mix, x1=x1, h2=h2, a=a, ff=ff)
    return x2, saved


def _layer_bwd(dx2, sv, mod, gains, w, tri, *, tag):
    sh1, sc1, gt1, sh2, sc2, gt2 = mod
    g_pre_mix, g_post_mix, g_pre_mlp, g_post_mlp = gains
    d = dx2.shape[1]
    dff, da, dgt2, dg_post_mlp = _mlp_out_bwd(dx2, sv["ff"], sv["a"], w["w_mlp_out"], g_post_mlp, gt2,
                                              name=f"mlp_out_bwd_{tag}")
    gw_mlp_out = _matmul_tn(sv["a"], dff, relu2=True, name=f"gw_mlp_out_{tag}")
    dx1, dsh2, dsc2, dg_pre_mlp = _matmul_nt_norm_bwd(da, w["w_mlp_in"], sv["x1"], dx2, g_pre_mlp, sc2,
                                                      name=f"mlp_in_bwd_{tag}")
    gw_mlp_in = _matmul_tn(sv["h2"], da, name=f"gw_mlp_in_{tag}")
    dmix, dycv, dyat, dyc, do, dga, dgb, dgt1, dg_post_mix = _mix_out_bwd(
        dx1, sv["mix"], sv["proj"], sv["ycv"], sv["yat"], w["w_out"], w["w_proj_conv"], w["w_proj_attn"],
        g_post_mix, gt1, name=f"mix_out_bwd_{tag}")
    gw_out = _matmul_tn(sv["merged"], dmix, name=f"gw_out_{tag}")
    gw_proj_conv = _matmul_tn(sv["yc"], dycv, name=f"gw_proj_conv_{tag}")
    gw_proj_attn = _matmul_tn(sv["o"].astype(BF16), dyat, name=f"gw_proj_attn_{tag}")
    dbg, dcg, du, g_conv_w = _conv_bwd(dyc, sv["proj"], w["conv_w"], name=f"conv_bwd_{tag}")
    dq, dk, dv = _attn_bwd(sv["proj"], sv["o"], do, tri, d=d, name=f"attn_bwd_{tag}")
    dproj = jnp.concatenate([dbg, dcg, du, dq, dk, dv, dga, dgb], axis=1)
    dx0, dsh1, dsc1, dg_pre_mix = _matmul_nt_norm_bwd(dproj, w["w_in"], sv["x"], dx1, g_pre_mix, sc1,
                                                      name=f"in_proj_bwd_{tag}")
    gw_in = _matmul_tn(sv["h"], dproj, name=f"gw_in_{tag}")
    grads = dict(w_in=gw_in, conv_w=g_conv_w, w_proj_conv=gw_proj_conv, w_proj_attn=gw_proj_attn, w_out=gw_out,
                 w_mlp_in=gw_mlp_in, w_mlp_out=gw_mlp_out)
    dmod = jnp.concatenate([dsh1, dsc1, dgt1, dsh2, dsc2, dgt2], axis=0)
    dgains = jnp.concatenate([dg_pre_mix, dg_post_mix, dg_pre_mlp, dg_post_mlp], axis=0)
    return dx0, grads, dmod, dgains


BIG = ("w_in", "w_proj_conv", "w_proj_attn", "w_out", "w_mlp_in", "w_mlp_out")
SHARD_AXIS = dict(w_in=2, w_proj_conv=2, w_proj_attn=2, w_out=1, w_mlp_in=2, w_mlp_out=1)


def _local_step(x, target, mods, gains, wfull, conv_w):
    depth = mods.shape[0]
    tri = _tri(min(256, x.shape[0]))
    saved = []
    for l in range(depth):
        wl = {k: wfull[k][l] for k in BIG}
        wl["conv_w"] = conv_w[l]
        mod = [mods[l, k:k + 1] for k in range(N_MOD)]
        gl = [gains[l, k:k + 1] for k in range(4)]
        x, sv = _layer_fwd(x, mod, gl, wl, tri, tag=str(l))
        saved.append((sv, mod, gl, wl))
    dx, loss = _loss_grad(x, target, name="loss_grad")
    grads, dmods, dgains = [], [], []
    for l in reversed(range(depth)):
        sv, mod, gl, wl = saved[l]
        dx, g, dmod, dgain = _layer_bwd(dx, sv, mod, gl, wl, tri, tag=str(l))
        grads.insert(0, g)
        dmods.insert(0, dmod)
        dgains.insert(0, dgain)
    gstack = {k: jnp.stack([g[k] for g in grads]) for k in grads[0]}
    return loss, dx, gstack, jnp.stack(dmods), jnp.stack(dgains)


def _coords():
    return lax.axis_index("x"), lax.axis_index("y"), lax.axis_index("c")


def _flip(v, f):
    return 1 - v if f else v


def _all_gather_small(v, *, name):
    r, c_ = v.shape

    def body(v_ref, out_ref, send_sems, recv_sems, local_sem):
        x, y, c = _coords()
        me = 4 * x + 2 * y + c
        mine = pltpu.make_async_copy(v_ref, out_ref.at[me], local_sem)
        mine.start()
        copies = []
        for k in range(1, 8):
            fx, fy, fc = (k >> 2) & 1, (k >> 1) & 1, k & 1
            px, py, pc = _flip(x, fx), _flip(y, fy), _flip(c, fc)
            out = pltpu.make_async_remote_copy(src_ref=v_ref, dst_ref=out_ref.at[me], send_sem=send_sems.at[k - 1],
                                               recv_sem=recv_sems.at[k - 1], device_id=(px, py, pc), device_id_type=MESH)
            out.start()
            back = pltpu.make_async_remote_copy(src_ref=v_ref, dst_ref=out_ref.at[4 * px + 2 * py + pc],
                                                send_sem=send_sems.at[k - 1], recv_sem=recv_sems.at[k - 1],
                                                device_id=(px, py, pc), device_id_type=MESH)
            copies.append((out, back))
        for out, back in copies:
            back.wait_recv()
        for out, back in copies:
            out.wait_send()
        mine.wait()

    return pl.pallas_call(
        body, name=name,
        in_specs=[pl.BlockSpec(memory_space=pltpu.VMEM)],
        out_specs=pl.BlockSpec(memory_space=pltpu.VMEM),
        out_shape=_sds((8, r, c_), F32),
        scratch_shapes=[pltpu.SemaphoreType.DMA((7,)), pltpu.SemaphoreType.DMA((7,)), pltpu.SemaphoreType.DMA],
    )(v)


def _shard_window(ref, axis, chip, half, rows, cols):
    r0, rn = (0, rows) if half is None else (half * (rows // 2), rows // 2)
    if axis == 2:
        return ref.at[:, pl.ds(r0, rn), pl.ds(chip * cols, cols)]
    return ref.at[:, pl.ds(chip * rows + r0, rn), :]


def _gather_weights(shards, axes, *, name):
    n = len(shards)
    out_shapes = []
    for sh, ax in zip(shards, axes):
        l, r, c_ = sh.shape
        out_shapes.append(_sds((l, r * 4, c_) if ax == 1 else (l, r, c_ * 4), sh.dtype))

    def body(*refs):
        ins, outs = refs[:n], refs[n:2 * n]
        send_sems, recv_sems, local_sems = refs[2 * n:]
        x, y, c = _coords()
        chip = 2 * x + y
        sibling = (x, y, 1 - c)
        started = []
        landing = []
        for w in range(n):
            _, rows, cols = ins[w].shape
            win = functools.partial(_shard_window, outs[w], axes[w], rows=rows, cols=cols)
            mine = pltpu.make_async_copy(ins[w], win(chip, None), local_sems.at[w])
            mine.start()
            started.append(mine)
            my_half = ins[w].at[:, pl.ds(c * (rows // 2), rows // 2), :]
            for j, (fx, fy) in enumerate(OTHER_CHIPS):
                px, py = _flip(x, fx), _flip(y, fy)
                send = pltpu.make_async_remote_copy(src_ref=my_half, dst_ref=win(chip, c), send_sem=send_sems.at[w, j],
                                                    recv_sem=recv_sems.at[w, j], device_id=(px, py, c), device_id_type=MESH)
                send.start()
                started.append(send)
                landing.append((w, j, 2 * px + py, win))
        passed = []
        for w, j, pchip, win in landing:
            got = pltpu.make_async_remote_copy(src_ref=win(pchip, c), dst_ref=win(pchip, c), send_sem=send_sems.at[w, 3 + j],
                                               recv_sem=recv_sems.at[w, j], device_id=sibling, device_id_type=MESH)
            got.wait_recv()
            on = pltpu.make_async_remote_copy(src_ref=win(pchip, c), dst_ref=win(pchip, c), send_sem=send_sems.at[w, 3 + j],
                                              recv_sem=recv_sems.at[w, 3 + j], device_id=sibling, device_id_type=MESH)
            on.start()
            passed.append(on)
        for w, j, pchip, win in landing:
            other = pltpu.make_async_remote_copy(src_ref=win(pchip, 1 - c), dst_ref=win(pchip, 1 - c),
                                                 send_sem=send_sems.at[w, 3 + j], recv_sem=recv_sems.at[w, 3 + j],
                                                 device_id=sibling, device_id_type=MESH)
            other.wait_recv()
        for cp in passed:
            cp.wait_send()
        for w in range(n):
            started[4 * w].wait()
            for j in range(3):
                started[4 * w + 1 + j].wait_send()

    hbm = pl.BlockSpec(memory_space=pltpu.HBM)
    return pl.pallas_call(
        body, name=name,
        in_specs=[hbm] * n, out_specs=[hbm] * n, out_shape=out_shapes,
        scratch_shapes=[pltpu.SemaphoreType.DMA((n, 6)), pltpu.SemaphoreType.DMA((n, 6)), pltpu.SemaphoreType.DMA((n,))],
    )(*shards)


def _piece_shape(g, axis):
    l, k, n = g.shape
    return (l, k // 2, n // 4) if axis == 2 else (l, k // 8, n)


def _rs_to_sibling(grads, axes, *, name):
    n = len(grads)
    out_shapes = []
    for g, ax in zip(grads, axes):
        ps = _piece_shape(g, ax)
        out_shapes += [_sds((4,) + ps, g.dtype), _sds((4,) + ps, g.dtype)]

    def body(*refs):
        ins, outs = refs[:n], refs[n:3 * n]
        send_sems, recv_sems, local_sems = refs[3 * n:]
        x, y, c = _coords()
        sibling = (x, y, 1 - c)
        copies = []
        for w in range(n):
            _, rows2, cols = outs[2 * w].shape[1:]
            for j in range(4):
                win = functools.partial(_shard_window, ins[w], axes[w], j, rows=rows2 * 2, cols=cols)
                keep = pltpu.make_async_copy(win(c), outs[2 * w].at[j], local_sems.at[w, j])
                keep.start()
                give = pltpu.make_async_remote_copy(src_ref=win(1 - c), dst_ref=outs[2 * w + 1].at[j],
                                                    send_sem=send_sems.at[w, j], recv_sem=recv_sems.at[w, j],
                                                    device_id=sibling, device_id_type=MESH)
                give.start()
                copies.append((keep, give))
        for keep, give in copies:
            give.wait_recv()
        for keep, give in copies:
            give.wait_send()
            keep.wait()

    hbm = pl.BlockSpec(memory_space=pltpu.HBM)
    return pl.pallas_call(
        body, name=name,
        in_specs=[hbm] * n, out_specs=[hbm] * (2 * n), out_shape=out_shapes,
        scratch_shapes=[pltpu.SemaphoreType.DMA((n, 4)), pltpu.SemaphoreType.DMA((n, 4)), pltpu.SemaphoreType.DMA((n, 4))],
    )(*grads)


def _rs_to_chips(pieces, *, name):
    n = len(pieces)
    out_shapes = [_sds(p.shape, p.dtype) for p in pieces]

    def body(*refs):
        ins, outs = refs[:n], refs[n:2 * n]
        send_sems, recv_sems, local_sems = refs[2 * n:]
        x, y, c = _coords()
        chip = 2 * x + y
        copies = []
        for w in range(n):
            keep = pltpu.make_async_copy(ins[w].at[chip], outs[w].at[chip], local_sems.at[w])
            keep.start()
            copies.append(keep)
            for j, (fx, fy) in enumerate(OTHER_CHIPS):
                px, py = _flip(x, fx), _flip(y, fy)
                pchip = 2 * px + py
                send = pltpu.make_async_remote_copy(src_ref=ins[w].at[pchip], dst_ref=outs[w].at[chip],
                                                    send_sem=send_sems.at[w, j], recv_sem=recv_sems.at[w, j],
                                                    device_id=(px, py, c), device_id_type=MESH)
                send.start()
                recv = pltpu.make_async_remote_copy(src_ref=ins[w].at[pchip], dst_ref=outs[w].at[pchip],
                                                    send_sem=send_sems.at[w, j], recv_sem=recv_sems.at[w, j],
                                                    device_id=(px, py, c), device_id_type=MESH)
                copies.append((send, recv))
        for cp in copies:
            if isinstance(cp, tuple):
                cp[1].wait_recv()
        for cp in copies:
            if isinstance(cp, tuple):
                cp[0].wait_send()
            else:
                cp.wait()

    hbm = pl.BlockSpec(memory_space=pltpu.HBM)
    return pl.pallas_call(
        body, name=name,
        in_specs=[hbm] * n, out_specs=[hbm] * n, out_shape=out_shapes,
        scratch_shapes=[pltpu.SemaphoreType.DMA((n, 3)), pltpu.SemaphoreType.DMA((n, 3)), pltpu.SemaphoreType.DMA((n,))],
    )(*pieces)


def _rs_join_halves(halves, *, name):
    n = len(halves)
    out_shapes = [_sds((h.shape[0], 2 * h.shape[1], h.shape[2]), h.dtype) for h in halves]

    def body(*refs):
        ins, outs = refs[:n], refs[n:2 * n]
        send_sems, recv_sems, local_sems = refs[2 * n:]
        x, y, c = _coords()
        sibling = (x, y, 1 - c)
        copies = []
        for w in range(n):
            rows2 = ins[w].shape[1]
            mine = outs[w].at[:, pl.ds(c * rows2, rows2), :]
            theirs = outs[w].at[:, pl.ds((1 - c) * rows2, rows2), :]
            keep = pltpu.make_async_copy(ins[w], mine, local_sems.at[w])
            keep.start()
            send = pltpu.make_async_remote_copy(src_ref=ins[w], dst_ref=mine, send_sem=send_sems.at[w],
                                                recv_sem=recv_sems.at[w], device_id=sibling, device_id_type=MESH)
            send.start()
            recv = pltpu.make_async_remote_copy(src_ref=ins[w], dst_ref=theirs, send_sem=send_sems.at[w],
                                                recv_sem=recv_sems.at[w], device_id=sibling, device_id_type=MESH)
            copies.append((keep, send, recv))
        for keep, send, recv in copies:
            recv.wait_recv()
        for keep, send, recv in copies:
            send.wait_send()
            keep.wait()

    hbm = pl.BlockSpec(memory_space=pltpu.HBM)
    return pl.pallas_call(
        body, name=name,
        in_specs=[hbm] * n, out_specs=[hbm] * n, out_shape=out_shapes,
        scratch_shapes=[pltpu.SemaphoreType.DMA((n,)), pltpu.SemaphoreType.DMA((n,)), pltpu.SemaphoreType.DMA((n,))],
    )(*halves)


def _flat_rows(shape):
    rows = 1
    for s in shape[:-1]:
        rows *= s
    return rows, shape[-1]


def _row_tile(rows, cols, cap_bytes=2 * 1024 * 1024):
    t = rows
    while t * cols * 4 > cap_bytes and t % 16 == 0:
        t //= 2
    return t


def _add2(a, b, *, name):
    shape = a.shape
    rows, cols = _flat_rows(shape)
    tr = _row_tile(rows, cols)

    def body(a_ref, b_ref, o_ref):
        o_ref[...] = a_ref[...] + b_ref[...]

    spec = pl.BlockSpec((tr, cols), lambda i: (i, 0))
    out = pl.pallas_call(body, name=name, grid=(rows // tr,), in_specs=[spec, spec], out_specs=spec,
                         out_shape=_sds((rows, cols), a.dtype), compiler_params=_params(("parallel",)),
                         )(a.reshape(rows, cols), b.reshape(rows, cols))
    return out.reshape(shape)


def _sum_slots(p, *, name):
    k = p.shape[0]
    shape = p.shape[1:]
    rows, cols = _flat_rows(shape)
    tr = _row_tile(rows, cols, cap_bytes=1024 * 1024)

    def body(p_ref, o_ref):
        acc = p_ref[0]
        for j in range(1, k):
            acc = acc + p_ref[j]
        o_ref[...] = acc

    out = pl.pallas_call(body, name=name, grid=(rows // tr,),
                         in_specs=[pl.BlockSpec((k, tr, cols), lambda i: (0, i, 0))],
                         out_specs=pl.BlockSpec((tr, cols), lambda i: (i, 0)),
                         out_shape=_sds((rows, cols), p.dtype), compiler_params=_params(("parallel",)),
                         )(p.reshape(k, rows, cols))
    return out.reshape(shape)


def _reduce_scatter(grads, axes):
    n = len(grads)
    kept_got = _rs_to_sibling(grads, axes, name="rs_to_sibling")
    pair = [_add2(kept_got[2 * w], kept_got[2 * w + 1], name=f"rs_add_pair_{w}") for w in range(n)]
    slots = _rs_to_chips(pair, name="rs_to_chips")
    halves = [_sum_slots(slots[w], name=f"rs_sum_chips_{w}") for w in range(n)]
    return _rs_join_halves(halves, name="rs_join_halves")


def _ada_fwd(c_all, w_ada, b_loc, *, name, tn=512):
    l, d, nl = w_ada.shape
    b = c_all.shape[0]
    tn = min(tn, nl)

    def body(c_ref, w_ref, b_ref, o_ref):
        o_ref[0] = jnp.dot(c_ref[...], w_ref[0], preferred_element_type=F32,
                           precision=lax.Precision.HIGHEST) + b_ref[0]

    return pl.pallas_call(
        body, name=name, grid=(l, nl // tn),
        in_specs=[pl.BlockSpec((b, d), lambda i, j: (0, 0)), pl.BlockSpec((1, d, tn), lambda i, j: (i, 0, j)),
                  pl.BlockSpec((1, 1, tn), lambda i, j: (i, 0, j))],
        out_specs=pl.BlockSpec((1, b, tn), lambda i, j: (i, 0, j)),
        out_shape=_sds((l, b, nl), F32),
        compiler_params=_params(("parallel", "parallel")),
    )(c_all, w_ada, b_loc)


def _ada_bwd(c_t, dmod_loc, *, name, tn=512):
    d, b = c_t.shape
    l, _, nl = dmod_loc.shape
    tn = min(tn, nl)

    def body(c_ref, dm_ref, o_ref):
        cv = c_ref[...]
        dm = dm_ref[0]
        acc = cv[:, 0:1] * dm[0:1, :]
        for k in range(1, b):
            acc = acc + cv[:, k:k + 1] * dm[k:k + 1, :]
        o_ref[0] = acc

    return pl.pallas_call(
        body, name=name, grid=(l, nl // tn),
        in_specs=[pl.BlockSpec((d, b), lambda i, j: (0, 0)), pl.BlockSpec((1, b, tn), lambda i, j: (i, 0, j))],
        out_specs=pl.BlockSpec((1, d, tn), lambda i, j: (i, 0, j)),
        out_shape=_sds((l, d, nl), F32),
        compiler_params=_params(("parallel", "parallel")),
    )(c_t, dmod_loc)


def _sum_devices(p, *, name):
    k, r, c_ = p.shape

    def body(p_ref, o_ref):
        acc = p_ref[0]
        for j in range(1, k):
            acc = acc + p_ref[j]
        o_ref[...] = acc

    return pl.pallas_call(body, name=name, out_shape=_sds((r, c_), F32),
                          in_specs=[pl.BlockSpec(memory_space=pltpu.VMEM)],
                          out_specs=pl.BlockSpec(memory_space=pltpu.VMEM))(p)


def _adamw(w, g, m, v, *, name):
    shape = w.shape
    rows, cols = _flat_rows(shape)
    tr = _row_tile(rows, cols, cap_bytes=1024 * 1024)
    c1 = 1.0 / (1.0 - ADAM_B1 ** ADAM_STEP)
    c2 = 1.0 / (1.0 - ADAM_B2 ** ADAM_STEP)

    def body(w_ref, g_ref, m_ref, v_ref, d_ref, nm_ref, nv_ref):
        gv = g_ref[...]
        nm = ADAM_B1 * m_ref[...] + (1.0 - ADAM_B1) * gv
        nv = ADAM_B2 * v_ref[...] + (1.0 - ADAM_B2) * (gv * gv)
        m_hat = nm * c1
        v_hat = nv * c2
        d_ref[...] = -ADAM_LR * (m_hat / (jnp.sqrt(v_hat) + ADAM_EPS) + ADAM_WD * w_ref[...])
        nm_ref[...] = nm
        nv_ref[...] = nv

    spec = pl.BlockSpec((tr, cols), lambda i: (i, 0))
    flat = lambda a: a.reshape(rows, cols)
    outs = pl.pallas_call(body, name=name, grid=(rows // tr,), in_specs=[spec] * 4, out_specs=[spec] * 3,
                          out_shape=[_sds((rows, cols), F32)] * 3, compiler_params=_params(("parallel",)),
                          )(flat(w), flat(g), flat(m), flat(v))
    return tuple(o.reshape(shape) for o in outs)


WEIGHTS = ("w_ada", "b_ada", "g_pre_mix", "g_post_mix", "g_pre_mlp", "g_post_mlp", "w_in", "conv_w",
           "w_proj_conv", "w_proj_attn", "w_out", "w_mlp_in", "w_mlp_out")
GAINS = ("g_pre_mix", "g_post_mix", "g_pre_mlp", "g_post_mlp")


def kernel(x, c, w_ada, b_ada, g_pre_mix, g_post_mix, g_pre_mlp, g_post_mlp, w_in, conv_w, w_proj_conv, w_proj_attn, w_out, w_mlp_in, w_mlp_out, loss_target, m_w_ada, m_b_ada, m_g_pre_mix, m_g_post_mix, m_g_pre_mlp, m_g_post_mlp, m_w_in, m_conv_w, m_w_proj_conv, m_w_proj_attn, m_w_out, m_w_mlp_in, m_w_mlp_out, v_w_ada, v_b_ada, v_g_pre_mix, v_g_post_mix, v_g_pre_mlp, v_g_post_mlp, v_w_in, v_conv_w, v_w_proj_conv, v_w_proj_attn, v_w_out, v_w_mlp_in, v_w_mlp_out):
    params = dict(w_ada=w_ada, b_ada=b_ada, g_pre_mix=g_pre_mix, g_post_mix=g_post_mix, g_pre_mlp=g_pre_mlp,
                  g_post_mlp=g_post_mlp, w_in=w_in, conv_w=conv_w, w_proj_conv=w_proj_conv, w_proj_attn=w_proj_attn,
                  w_out=w_out, w_mlp_in=w_mlp_in, w_mlp_out=w_mlp_out)
    m_in = dict(w_ada=m_w_ada, b_ada=m_b_ada, g_pre_mix=m_g_pre_mix, g_post_mix=m_g_post_mix, g_pre_mlp=m_g_pre_mlp,
                g_post_mlp=m_g_post_mlp, w_in=m_w_in, conv_w=m_conv_w, w_proj_conv=m_w_proj_conv,
                w_proj_attn=m_w_proj_attn, w_out=m_w_out, w_mlp_in=m_w_mlp_in, w_mlp_out=m_w_mlp_out)
    v_in = dict(w_ada=v_w_ada, b_ada=v_b_ada, g_pre_mix=v_g_pre_mix, g_post_mix=v_g_post_mix, g_pre_mlp=v_g_pre_mlp,
                g_post_mlp=v_g_post_mlp, w_in=v_w_in, conv_w=v_conv_w, w_proj_conv=v_w_proj_conv,
                w_proj_attn=v_w_proj_attn, w_out=v_w_out, w_mlp_in=v_w_mlp_in, w_mlp_out=v_w_mlp_out)

    depth, d, nl_ada = w_ada.shape
    ix, iy, ic = lax.axis_index("x"), lax.axis_index("y"), lax.axis_index("c")
    chip = 2 * ix + iy
    me = 4 * ix + 2 * iy + ic
    xs = x[0]
    target = loss_target[0]

    c_all = _all_gather_small(jnp.broadcast_to(c, (8, d)), name="gather_c")[:, 0, :]
    b_loc = lax.dynamic_slice_in_dim(b_ada, chip * nl_ada, nl_ada, axis=1)[:, None, :]
    mod_loc = _ada_fwd(c_all, w_ada, b_loc, name="ada_fwd")
    mod_all = _all_gather_small(mod_loc.reshape(depth * 8, nl_ada), name="gather_mod")
    mod_all = mod_all.reshape(4, 2, depth, 8, nl_ada)[:, 0]
    mod_me = lax.dynamic_index_in_dim(mod_all, me, axis=2, keepdims=False)
    mods = jnp.transpose(mod_me, (1, 0, 2)).reshape(depth, N_MOD, d)

    shards = [params[k].astype(BF16) for k in BIG]
    full = _gather_weights(shards, [SHARD_AXIS[k] for k in BIG], name="gather_weights")
    wfull = dict(zip(BIG, full))
    conv_full = _all_gather_small(
        jnp.pad(conv_w.reshape(depth * 3, -1), ((0, 8 - depth * 3), (0, 0))), name="gather_conv_w")
    conv_full = conv_full.reshape(4, 2, 8, -1)[:, 0, :depth * 3]
    conv_full = jnp.transpose(conv_full, (1, 0, 2)).reshape(depth, 3, -1)

    gains = jnp.stack([params[k] for k in GAINS], axis=1)
    loss, dx, gbig, dmods, dgains = _local_step(xs, target, mods, gains, wfull, conv_full)

    cw = conv_full.shape[2]
    rows = [dmods.reshape(depth * N_MOD, d), dgains.reshape(depth * 4, d),
            gbig["conv_w"].reshape(-1, d), jnp.broadcast_to(loss, (1, d))]
    payload = jnp.concatenate(rows, axis=0)
    n_rows = payload.shape[0]
    pad = (-n_rows) % 8
    payload = jnp.pad(payload, ((0, pad), (0, 0)))
    everyone = _all_gather_small(payload, name="gather_small_grads")
    total = _sum_devices(everyone, name="sum_small_grads")
    r0 = depth * N_MOD
    grads = {}
    grads["b_ada"] = total[:r0].reshape(depth, N_MOD * d)
    gsum = total[r0:r0 + depth * 4].reshape(depth, 4, d)
    for k, name in enumerate(GAINS):
        grads[name] = gsum[:, k]
    r1 = r0 + depth * 4
    n_conv = (depth * 3 * cw) // d
    conv_g = total[r1:r1 + n_conv].reshape(depth, 3, cw)
    grads["conv_w"] = lax.dynamic_slice_in_dim(conv_g, chip * (cw // 4), cw // 4, axis=2)
    loss_out = total[r1 + n_conv, 0]
    dmod_all = everyone[:, :r0].reshape(8, depth, N_MOD * d)
    dmod_loc = lax.dynamic_slice_in_dim(dmod_all, chip * nl_ada, nl_ada, axis=2)
    grads["w_ada"] = _ada_bwd(c_all.T, jnp.transpose(dmod_loc, (1, 0, 2)), name="ada_bwd")

    reduced = _reduce_scatter([gbig[k] for k in BIG], [SHARD_AXIS[k] for k in BIG])
    for k, g in zip(BIG, reduced):
        grads[k] = g

    deltas, new_m, new_v = {}, {}, {}
    for k in WEIGHTS:
        deltas[k], new_m[k], new_v[k] = _adamw(params[k], grads[k], m_in[k], v_in[k], name=f"adamw_{k}")

    return (loss_out, dx[None], *[grads[k] for k in WEIGHTS], *[deltas[k] for k in WEIGHTS],
            *[new_m[k] for k in WEIGHTS], *[new_v[k] for k in WEIGHTS])
```

```python
import functools

import jax
import jax.numpy as jnp
from jax import lax
from jax.experimental import pallas as pl
from jax.experimental.pallas import tpu as pltpu

F32 = jnp.float32
BF16 = jnp.bfloat16
EPS = 1e-6
N_MOD = 6
HEAD_DIM = 64
LANES = 128
ATTN_SCALE = 1.0 / 8.0
UNDERFLOW_LOG = -90.0
ADAM_LR = 0.001
ADAM_B1 = 0.9
ADAM_B2 = 0.999
ADAM_EPS = 1e-08
ADAM_WD = 0.01
ADAM_STEP = 10
VMEM_LIMIT = 56 * 1024 * 1024
MESH = pl.DeviceIdType.MESH
OTHER_CHIPS = ((1, 0), (0, 1), (1, 1))

_NT = (((1,), (1,)), ((), ()))
_TN = (((0,), (0,)), ((), ()))


def _sds(shape, dtype):
    return jax.ShapeDtypeStruct(shape, dtype)


def _params(sem):
    return pltpu.CompilerParams(dimension_semantics=sem, vmem_limit_bytes=VMEM_LIMIT)


def _fit(t, n):
    t = min(t, n)
    while n % t:
        t //= 2
    return t


def _vec_spec(d, nargs):
    if nargs == 1:
        return pl.BlockSpec((1, d), lambda i: (0, 0))
    return pl.BlockSpec((1, d), lambda i, j: (0, 0))


def _softplus_parts(z):
    t = jnp.exp(-jnp.abs(z))
    sp = jnp.maximum(z, 0.0) + jnp.log(1.0 + t)
    sig = jnp.where(z >= 0.0, 1.0, t) / (1.0 + t)
    return sp, sig


def _sigmoid(z):
    t = jnp.exp(-jnp.abs(z))
    return jnp.where(z >= 0.0, 1.0, t) / (1.0 + t)


def _split_bf16(a):
    hi = a.astype(BF16)
    lo = (a - hi.astype(F32)).astype(BF16)
    return hi, lo


def _rms_bwd(dn, xin, g):
    r = lax.rsqrt(jnp.mean(xin * xin, axis=-1, keepdims=True) + EPS)
    xh = xin * r
    dxh = dn * g
    dxin = r * (dxh - xh * jnp.mean(dxh * xh, axis=-1, keepdims=True))
    return dxin, xh


def _colsum(a):
    return jnp.sum(a, axis=0, keepdims=True)


def _norm_mod_matmul(x, g, sc, sh, w, *, name, tm=1024, tn=512):
    s, d = x.shape
    n = w.shape[1]
    tm, tn = _fit(tm, s), _fit(tn, n)

    def body(x_ref, g_ref, sc_ref, sh_ref, w_ref, h_ref, o_ref):
        @pl.when(pl.program_id(1) == 0)
        def _():
            xv = x_ref[...]
            r = lax.rsqrt(jnp.mean(xv * xv, axis=-1, keepdims=True) + EPS)
            h_ref[...] = ((xv * r * g_ref[...]) * (1.0 + sc_ref[...]) + sh_ref[...]).astype(BF16)
        o_ref[...] = jnp.dot(h_ref[...], w_ref[...], preferred_element_type=F32).astype(BF16)

    return pl.pallas_call(
        body, name=name, grid=(s // tm, n // tn),
        in_specs=[pl.BlockSpec((tm, d), lambda i, j: (i, 0)), _vec_spec(d, 2), _vec_spec(d, 2), _vec_spec(d, 2),
                  pl.BlockSpec((d, tn), lambda i, j: (0, j))],
        out_specs=[pl.BlockSpec((tm, d), lambda i, j: (i, 0)), pl.BlockSpec((tm, tn), lambda i, j: (i, j))],
        out_shape=[_sds((s, d), BF16), _sds((s, n), BF16)],
        compiler_params=_params(("parallel", "arbitrary")),
    )(x, g, sc, sh, w)


HALO = 16


def _conv_fwd(proj, conv_w, *, name, tm=512):
    s = proj.shape[0]
    cw = conv_w.shape[1]
    tm = min(tm, s)
    nb = tm // HALO

    def body(bg_ref, cg_ref, u_ref, cgh_ref, uh_ref, w_ref, yc_ref, vbuf):
        i = pl.program_id(0)
        vv = cg_ref[...].astype(F32) * u_ref[...].astype(F32)
        halo = cgh_ref[...].astype(F32) * uh_ref[...].astype(F32)
        vbuf[0:HALO, :] = jnp.where(i > 0, halo, 0.0)
        vbuf[HALO:HALO + tm, :] = vv
        v1 = vbuf[HALO - 1:HALO - 1 + tm, :]
        v2 = vbuf[HALO - 2:HALO - 2 + tm, :]
        w = w_ref[...]
        y = w[2:3, :] * vv + w[1:2, :] * v1 + w[0:1, :] * v2
        yc_ref[...] = (bg_ref[...].astype(F32) * y).astype(BF16)

    def prev(i):
        return jnp.maximum(i * nb - 1, 0)

    return pl.pallas_call(
        body, name=name, grid=(s // tm,),
        in_specs=[pl.BlockSpec((tm, cw), lambda i: (i, 0)), pl.BlockSpec((tm, cw), lambda i: (i, 1)),
                  pl.BlockSpec((tm, cw), lambda i: (i, 2)),
                  pl.BlockSpec((HALO, cw), lambda i: (prev(i), 1)), pl.BlockSpec((HALO, cw), lambda i: (prev(i), 2)),
                  pl.BlockSpec((3, cw), lambda i: (0, 0))],
        out_specs=pl.BlockSpec((tm, cw), lambda i: (i, 0)),
        out_shape=_sds((s, cw), BF16),
        scratch_shapes=[pltpu.VMEM((HALO + tm, cw), F32)],
        compiler_params=_params(("arbitrary",)),
    )(proj, proj, proj, proj, proj, conv_w)


def _tri(qb):
    r = lax.broadcasted_iota(jnp.int32, (qb, qb), 0)
    c = lax.broadcasted_iota(jnp.int32, (qb, qb), 1)
    return (r >= c).astype(BF16)


def _head_mask(h):
    lane = lax.broadcasted_iota(jnp.int32, (1, LANES), 1)
    return (lane >= HEAD_DIM * h) & (lane < HEAD_DIM * (h + 1))


def _attn_cols(d):
    cw = d // 2
    hp = (d // 2) // LANES
    q0 = (3 * cw) // LANES
    return q0, q0 + hp, q0 + 2 * hp, hp


def _attn_fwd(proj, tri, *, d, name, qb=256):
    s = proj.shape[0]
    qb = min(qb, s)
    q0, k0, v0, hp = _attn_cols(d)

    def body(q_ref, k_ref, v_ref, tri_ref, o_ref):
        i = pl.program_id(1)
        row = lax.broadcasted_iota(jnp.int32, (qb, qb), 0)
        col = lax.broadcasted_iota(jnp.int32, (qb, qb), 1)
        causal = col < row
        tri_m = tri_ref[...]
        q = q_ref[...]
        out = jnp.zeros((qb, LANES), F32)
        for h in range(2):
            hm = _head_mask(h)
            qm = jnp.where(hm, q * ATTN_SCALE, 0).astype(BF16)

            def block(j, carry, diag, qm=qm, hm=hm):
                run, acc = carry
                rows = pl.ds(pl.multiple_of(j * qb, qb), qb)
                kb = k_ref[rows, :]
                vb = jnp.where(hm, v_ref[rows, :], 0).astype(BF16)
                z = lax.dot_general(qm, kb, _NT, preferred_element_type=F32)
                sp, _ = _softplus_parts(z)
                lg = -sp
                if diag:
                    lg = jnp.where(causal, lg, 0.0)
                hi, lo = _split_bf16(lg)
                cs = (jnp.dot(hi, tri_m, preferred_element_type=F32)
                      + jnp.dot(lo, tri_m, preferred_element_type=F32))
                a = jnp.exp(z + cs + run)
                if diag:
                    a = jnp.where(causal, a, 0.0)
                acc = acc + jnp.dot(a.astype(BF16), vb, preferred_element_type=F32)
                return run + cs[:, 0:1], acc

            carry = block(i, (jnp.zeros((qb, 1), F32), jnp.zeros((qb, LANES), F32)), True)
            carry = lax.while_loop(
                lambda st: (st[0] >= 0) & (jnp.max(st[1]) > UNDERFLOW_LOG),
                lambda st: (st[0] - 1, *block(st[0], st[1:], False)),
                (i - 1, *carry))
            out = out + carry[2]
        o_ref[...] = out

    return pl.pallas_call(
        body, name=name, grid=(hp, s // qb),
        in_specs=[pl.BlockSpec((qb, LANES), lambda p, i: (i, q0 + p)),
                  pl.BlockSpec((s, LANES), lambda p, i: (0, k0 + p)),
                  pl.BlockSpec((s, LANES), lambda p, i: (0, v0 + p)),
                  pl.BlockSpec((qb, qb), lambda p, i: (0, 0))],
        out_specs=pl.BlockSpec((qb, LANES), lambda p, i: (i, p)),
        out_shape=_sds((s, hp * LANES), F32),
        compiler_params=_params(("parallel", "arbitrary")),
    )(proj, proj, proj, tri)


def _mix_out(yc, o, proj, x, wpc, wpa, wout, g, gt, *, name, tm=256):
    s, d = x.shape
    cw = yc.shape[1]
    tm = min(tm, s)
    ga_blk = (3 * cw + 3 * (d // 2)) // d

    def body(yc_ref, o_ref, ga_ref, gb_ref, x_ref, wpc_ref, wpa_ref, wout_ref, g_ref, gt_ref,
             ycv_ref, yat_ref, mg_ref, mix_ref, x1_ref):
        y_conv = jnp.dot(yc_ref[...], wpc_ref[...], preferred_element_type=F32)
        y_attn = jnp.dot(o_ref[...].astype(BF16), wpa_ref[...], preferred_element_type=F32)
        merged = (_sigmoid(ga_ref[...].astype(F32)) * y_conv + _sigmoid(gb_ref[...].astype(F32)) * y_attn)
        mg = merged.astype(BF16)
        mix = jnp.dot(mg, wout_ref[...], preferred_element_type=F32)
        r = lax.rsqrt(jnp.mean(mix * mix, axis=-1, keepdims=True) + EPS)
        ycv_ref[...] = y_conv.astype(BF16)
        yat_ref[...] = y_attn.astype(BF16)
        mg_ref[...] = mg
        mix_ref[...] = mix
        x1_ref[...] = x_ref[...] + gt_ref[...] * (mix * r * g_ref[...])

    def rows(w):
        return pl.BlockSpec((tm, w), lambda i: (i, 0))

    def full(a):
        return pl.BlockSpec(a.shape, lambda i: (0, 0))

    return pl.pallas_call(
        body, name=name, grid=(s // tm,),
        in_specs=[rows(cw), rows(d // 2), pl.BlockSpec((tm, d), lambda i: (i, ga_blk)),
                  pl.BlockSpec((tm, d), lambda i: (i, ga_blk + 1)), rows(d),
                  full(wpc), full(wpa), full(wout), _vec_spec(d, 1), _vec_spec(d, 1)],
        out_specs=[rows(d), rows(d), rows(d), rows(d), rows(d)],
        out_shape=[_sds((s, d), BF16), _sds((s, d), BF16), _sds((s, d), BF16), _sds((s, d), F32), _sds((s, d), F32)],
        compiler_params=_params(("parallel",)),
    )(yc, o, proj, proj, x, wpc, wpa, wout, g, gt)


def _relu2(a):
    r = jnp.maximum(a.astype(F32), 0.0)
    return (r * r).astype(BF16)


def _mlp_out(a, x, w2, g, gt, *, name, tm=512):
    s, d = x.shape
    dff = a.shape[1]
    tm = min(tm, s)

    def body(a_ref, x_ref, w_ref, g_ref, gt_ref, ff_ref, x2_ref):
        ff = jnp.dot(_relu2(a_ref[...]), w_ref[...], preferred_element_type=F32)
        r = lax.rsqrt(jnp.mean(ff * ff, axis=-1, keepdims=True) + EPS)
        ff_ref[...] = ff
        x2_ref[...] = x_ref[...] + gt_ref[...] * (ff * r * g_ref[...])

    return pl.pallas_call(
        body, name=name, grid=(s // tm,),
        in_specs=[pl.BlockSpec((tm, dff), lambda i: (i, 0)), pl.BlockSpec((tm, d), lambda i: (i, 0)),
                  pl.BlockSpec((dff, d), lambda i: (0, 0)), _vec_spec(d, 1), _vec_spec(d, 1)],
        out_specs=[pl.BlockSpec((tm, d), lambda i: (i, 0)), pl.BlockSpec((tm, d), lambda i: (i, 0))],
        out_shape=[_sds((s, d), F32), _sds((s, d), F32)],
        compiler_params=_params(("parallel",)),
    )(a, x, w2, g, gt)


def _loss_grad(y, target, *, name, tm=512):
    s, d = y.shape
    tm = min(tm, s)

    def body(y_ref, t_ref, dy_ref, loss_ref):
        @pl.when(pl.program_id(0) == 0)
        def _():
            loss_ref[...] = jnp.zeros_like(loss_ref)
        e = y_ref[...] - t_ref[...]
        dy_ref[...] = e * (1.0 / d)
        loss_ref[...] += 0.5 * jnp.sum(jnp.mean(e * e, axis=-1, keepdims=True), axis=0, keepdims=True)

    return pl.pallas_call(
        body, name=name, grid=(s // tm,),
        in_specs=[pl.BlockSpec((tm, d), lambda i: (i, 0)), pl.BlockSpec((tm, d), lambda i: (i, 0))],
        out_specs=[pl.BlockSpec((tm, d), lambda i: (i, 0)), pl.BlockSpec((1, 1), lambda i: (0, 0))],
        out_shape=[_sds((s, d), F32), _sds((1, 1), F32)],
        compiler_params=_params(("arbitrary",)),
    )(y, target)


def _mlp_out_bwd(dx, ff, a, w2, g, gt, *, name, tm=256):
    s, d = dx.shape
    dff = a.shape[1]
    tm = min(tm, s)

    def body(dx_ref, ff_ref, a_ref, w_ref, g_ref, gt_ref, dff_ref, da_ref, dgt_ref, dg_ref):
        @pl.when(pl.program_id(0) == 0)
        def _():
            dgt_ref[...] = jnp.zeros_like(dgt_ref)
            dg_ref[...] = jnp.zeros_like(dg_ref)
        dxv = dx_ref[...]
        dn = dxv * gt_ref[...]
        dffv, xh = _rms_bwd(dn, ff_ref[...], g_ref[...])
        dgt_ref[...] += _colsum(dxv * (xh * g_ref[...]))
        dg_ref[...] += _colsum(dn * xh)
        dffb = dffv.astype(BF16)
        dff_ref[...] = dffb
        drr = lax.dot_general(dffb, w_ref[...], _NT, preferred_element_type=F32)
        da_ref[...] = (drr * (2.0 * jnp.maximum(a_ref[...].astype(F32), 0.0))).astype(BF16)

    return pl.pallas_call(
        body, name=name, grid=(s // tm,),
        in_specs=[pl.BlockSpec((tm, d), lambda i: (i, 0)), pl.BlockSpec((tm, d), lambda i: (i, 0)),
                  pl.BlockSpec((tm, dff), lambda i: (i, 0)), pl.BlockSpec((dff, d), lambda i: (0, 0)),
                  _vec_spec(d, 1), _vec_spec(d, 1)],
        out_specs=[pl.BlockSpec((tm, d), lambda i: (i, 0)), pl.BlockSpec((tm, dff), lambda i: (i, 0)),
                   _vec_spec(d, 1), _vec_spec(d, 1)],
        out_shape=[_sds((s, d), BF16), _sds((s, dff), BF16), _sds((1, d), F32), _sds((1, d), F32)],
        compiler_params=_params(("arbitrary",)),
    )(dx, ff, a, w2, g, gt)


def _matmul_nt_norm_bwd(dy, w, x, dres, g, sc, *, name, tm=512, tn=512):
    s, n = dy.shape
    d = w.shape[0]
    tm, tn = _fit(tm, s), _fit(tn, n)
    nj = n // tn

    def body(dy_ref, w_ref, x_ref, dres_ref, g_ref, sc_ref, dx_ref, dsh_ref, dsc_ref, dg_ref, acc):
        i, j = pl.program_id(0), pl.program_id(1)

        @pl.when((i == 0) & (j == 0))
        def _():
            dsh_ref[...] = jnp.zeros_like(dsh_ref)
            dsc_ref[...] = jnp.zeros_like(dsc_ref)
            dg_ref[...] = jnp.zeros_like(dg_ref)

        @pl.when(j == 0)
        def _():
            acc[...] = jnp.zeros_like(acc)

        acc[...] += lax.dot_general(dy_ref[...], w_ref[...], _NT, preferred_element_type=F32)

        @pl.when(j == nj - 1)
        def _():
            dh = acc[...]
            dn = dh * (1.0 + sc_ref[...])
            dxin, xh = _rms_bwd(dn, x_ref[...], g_ref[...])
            dsh_ref[...] += _colsum(dh)
            dsc_ref[...] += _colsum(dh * (xh * g_ref[...]))
            dg_ref[...] += _colsum(dn * xh)
            dx_ref[...] = dres_ref[...] + dxin

    return pl.pallas_call(
        body, name=name, grid=(s // tm, nj),
        in_specs=[pl.BlockSpec((tm, tn), lambda i, j: (i, j)), pl.BlockSpec((d, tn), lambda i, j: (0, j)),
                  pl.BlockSpec((tm, d), lambda i, j: (i, 0)), pl.BlockSpec((tm, d), lambda i, j: (i, 0)),
                  _vec_spec(d, 2), _vec_spec(d, 2)],
        out_specs=[pl.BlockSpec((tm, d), lambda i, j: (i, 0)), _vec_spec(d, 2), _vec_spec(d, 2), _vec_spec(d, 2)],
        out_shape=[_sds((s, d), F32), _sds((1, d), F32), _sds((1, d), F32), _sds((1, d), F32)],
        scratch_shapes=[pltpu.VMEM((tm, d), F32)],
        compiler_params=_params(("arbitrary", "arbitrary")),
    )(dy, w, x, dres, g, sc)


def _matmul_tn(a, b, *, name, tk=1024, tn=1024, ts=512, relu2=False):
    s, k = a.shape
    n = b.shape[1]
    tk, tn, ts = _fit(tk, k), _fit(tn, n), _fit(ts, s)
    nt = s // ts

    def body(a_ref, b_ref, o_ref, acc):
        t = pl.program_id(2)

        @pl.when(t == 0)
        def _():
            acc[...] = jnp.zeros_like(acc)
        av = a_ref[...]
        if relu2:
            av = _relu2(av)
        acc[...] += lax.dot_general(av, b_ref[...], _TN, preferred_element_type=F32)

        @pl.when(t == nt - 1)
        def _():
            o_ref[...] = acc[...].astype(BF16)

    return pl.pallas_call(
        body, name=name, grid=(k // tk, n // tn, nt),
        in_specs=[pl.BlockSpec((ts, tk), lambda i, j, t: (t, i)), pl.BlockSpec((ts, tn), lambda i, j, t: (t, j))],
        out_specs=pl.BlockSpec((tk, tn), lambda i, j, t: (i, j)),
        out_shape=_sds((k, n), BF16),
        scratch_shapes=[pltpu.VMEM((tk, tn), F32)],
        compiler_params=_params(("parallel", "parallel", "arbitrary")),
    )(a, b)


def _mix_out_bwd(dx, mix, proj, ycv, yat, wout, wpc, wpa, g, gt, *, name, tm=256):
    s, d = dx.shape
    cw = wpc.shape[0]
    aw = wpa.shape[0]
    tm = min(tm, s)
    ga_blk = (3 * cw + 3 * aw) // d

    def body(dx_ref, mix_ref, ga_ref, gb_ref, ycv_ref, yat_ref, wout_ref, wpc_ref, wpa_ref, g_ref, gt_ref,
             dmix_ref, dycv_ref, dyat_ref, dyc_ref, do_ref, dga_ref, dgb_ref, dgt_ref, dg_ref):
        @pl.when(pl.program_id(0) == 0)
        def _():
            dgt_ref[...] = jnp.zeros_like(dgt_ref)
            dg_ref[...] = jnp.zeros_like(dg_ref)
        dxv = dx_ref[...]
        dn = dxv * gt_ref[...]
        dmix, xh = _rms_bwd(dn, mix_ref[...], g_ref[...])
        dgt_ref[...] += _colsum(dxv * (xh * g_ref[...]))
        dg_ref[...] += _colsum(dn * xh)
        dmixb = dmix.astype(BF16)
        dmix_ref[...] = dmixb
        dmerged = lax.dot_general(dmixb, wout_ref[...], _NT, preferred_element_type=F32)
        sga = _sigmoid(ga_ref[...].astype(F32))
        sgb = _sigmoid(gb_ref[...].astype(F32))
        dycv = (dmerged * sga).astype(BF16)
        dyat = (dmerged * sgb).astype(BF16)
        dycv_ref[...] = dycv
        dyat_ref[...] = dyat
        dga_ref[...] = (dmerged * ycv_ref[...].astype(F32) * (sga * (1.0 - sga))).astype(BF16)
        dgb_ref[...] = (dmerged * yat_ref[...].astype(F32) * (sgb * (1.0 - sgb))).astype(BF16)
        dyc_ref[...] = lax.dot_general(dycv, wpc_ref[...], _NT, preferred_element_type=F32).astype(BF16)
        do_ref[...] = lax.dot_general(dyat, wpa_ref[...], _NT, preferred_element_type=F32).astype(BF16)

    def rows(w):
        return pl.BlockSpec((tm, w), lambda i: (i, 0))

    def full(a):
        return pl.BlockSpec(a.shape, lambda i: (0, 0))

    return pl.pallas_call(
        body, name=name, grid=(s // tm,),
        in_specs=[rows(d), rows(d), pl.BlockSpec((tm, d), lambda i: (i, ga_blk)),
                  pl.BlockSpec((tm, d), lambda i: (i, ga_blk + 1)), rows(d), rows(d),
                  full(wout), full(wpc), full(wpa), _vec_spec(d, 1), _vec_spec(d, 1)],
        out_specs=[rows(d), rows(d), rows(d), rows(cw), rows(aw), rows(d), rows(d), _vec_spec(d, 1), _vec_spec(d, 1)],
        out_shape=[_sds((s, d), BF16), _sds((s, d), BF16), _sds((s, d), BF16), _sds((s, cw), BF16),
                   _sds((s, aw), BF16), _sds((s, d), BF16), _sds((s, d), BF16), _sds((1, d), F32), _sds((1, d), F32)],
        compiler_params=_params(("arbitrary",)),
    )(dx, mix, proj, proj, ycv, yat, wout, wpc, wpa, g, gt)


def _conv_bwd(dyc, proj, conv_w, *, name, tm=512):
    s = proj.shape[0]
    cw = conv_w.shape[1]
    tm = min(tm, s)
    nb = tm // HALO
    nt = s // tm
    last_blk = s // HALO - 1

    def body(dyc_ref, bg_ref, cg_ref, u_ref, cgh_ref, uh_ref, dych_ref, bgh_ref, w_ref,
             dbg_ref, dcg_ref, du_ref, dw_ref, vbuf, gbuf):
        i = pl.program_id(0)

        @pl.when(i == 0)
        def _():
            dw_ref[...] = jnp.zeros_like(dw_ref)

        cg = cg_ref[...].astype(F32)
        u = u_ref[...].astype(F32)
        vv = cg * u
        halo = cgh_ref[...].astype(F32) * uh_ref[...].astype(F32)
        vbuf[0:HALO, :] = jnp.where(i > 0, halo, 0.0)
        vbuf[HALO:HALO + tm, :] = vv
        v1 = vbuf[HALO - 1:HALO - 1 + tm, :]
        v2 = vbuf[HALO - 2:HALO - 2 + tm, :]
        w = w_ref[...]
        y = w[2:3, :] * vv + w[1:2, :] * v1 + w[0:1, :] * v2
        dyc = dyc_ref[...].astype(F32)
        dbg_ref[...] = (dyc * y).astype(BF16)
        gy = dyc * bg_ref[...].astype(F32)
        nxt = dych_ref[...].astype(F32) * bgh_ref[...].astype(F32)
        gbuf[0:tm, :] = gy
        gbuf[tm:tm + HALO, :] = jnp.where(i < nt - 1, nxt, 0.0)
        g1 = gbuf[1:1 + tm, :]
        g2 = gbuf[2:2 + tm, :]
        dvv = w[2:3, :] * gy + w[1:2, :] * g1 + w[0:1, :] * g2
        dcg_ref[...] = (dvv * u).astype(BF16)
        du_ref[...] = (dvv * cg).astype(BF16)
        dw_ref[0:1, :] += _colsum(gy * v2)
        dw_ref[1:2, :] += _colsum(gy * v1)
        dw_ref[2:3, :] += _colsum(gy * vv)

    def prev(i):
        return jnp.maximum(i * nb - 1, 0)

    def nxt_blk(i):
        return jnp.minimum((i + 1) * nb, last_blk)

    def col(c):
        return pl.BlockSpec((tm, cw), lambda i: (i, c))

    return pl.pallas_call(
        body, name=name, grid=(nt,),
        in_specs=[col(0), col(0), col(1), col(2),
                  pl.BlockSpec((HALO, cw), lambda i: (prev(i), 1)), pl.BlockSpec((HALO, cw), lambda i: (prev(i), 2)),
                  pl.BlockSpec((HALO, cw), lambda i: (nxt_blk(i), 0)), pl.BlockSpec((HALO, cw), lambda i: (nxt_blk(i), 0)),
                  pl.BlockSpec((3, cw), lambda i: (0, 0))],
        out_specs=[col(0), col(0), col(0), pl.BlockSpec((3, cw), lambda i: (0, 0))],
        out_shape=[_sds((s, cw), BF16), _sds((s, cw), BF16), _sds((s, cw), BF16), _sds((3, cw), F32)],
        scratch_shapes=[pltpu.VMEM((HALO + tm, cw), F32), pltpu.VMEM((tm + HALO, cw), F32)],
        compiler_params=_params(("arbitrary",)),
    )(dyc, proj, proj, proj, proj, proj, dyc, proj, conv_w)


def _attn_bwd(proj, o, do, tri, *, d, name, qb=256):
    s = proj.shape[0]
    qb = min(qb, s)
    nq = s // qb
    q0, k0, v0, hp = _attn_cols(d)

    def body(q_ref, k_ref, v_ref, o_ref, do_ref, tri_ref, dq_ref, dk_ref, dv_ref, dk_acc, dv_acc):
        i = pl.program_id(1)

        @pl.when(i == 0)
        def _():
            dk_acc[...] = jnp.zeros_like(dk_acc)
            dv_acc[...] = jnp.zeros_like(dv_acc)

        row = lax.broadcasted_iota(jnp.int32, (qb, qb), 0)
        col = lax.broadcasted_iota(jnp.int32, (qb, qb), 1)
        causal = col < row
        tri_m = tri_ref[...]
        q = q_ref[...]
        dov = do_ref[...]
        ov = o_ref[...]
        dq_out = jnp.zeros((qb, LANES), F32)
        for h in range(2):
            hm = _head_mask(h)
            qm = jnp.where(hm, q * ATTN_SCALE, 0).astype(BF16)
            dom = jnp.where(hm, dov, 0).astype(BF16)
            dtot = jnp.sum(jnp.where(hm, dov.astype(F32) * ov, 0.0), axis=-1, keepdims=True)

            def block(j, carry, diag, qm=qm, dom=dom, dtot=dtot, hm=hm):
                run, grun, dq_acc = carry
                rows = pl.ds(pl.multiple_of(j * qb, qb), qb)
                kb = k_ref[rows, :]
                vb = v_ref[rows, :]
                z = lax.dot_general(qm, kb, _NT, preferred_element_type=F32)
                sp, beta = _softplus_parts(z)
                lg = -sp
                if diag:
                    lg = jnp.where(causal, lg, 0.0)
                hi, lo = _split_bf16(lg)
                cs = (jnp.dot(hi, tri_m, preferred_element_type=F32)
                      + jnp.dot(lo, tri_m, preferred_element_type=F32))
                a = jnp.exp(z + cs + run)
                if diag:
                    a = jnp.where(causal, a, 0.0)
                ab = a.astype(BF16)
                da = lax.dot_general(dom, vb, _NT, preferred_element_type=F32)
                gg = ab.astype(F32) * da
                ghi, glo = _split_bf16(gg)
                gcs = (jnp.dot(ghi, tri_m, preferred_element_type=F32)
                       + jnp.dot(glo, tri_m, preferred_element_type=F32))
                pre = (dtot - grun) - gcs
                dz = gg - beta * (gg + pre)
                if diag:
                    dz = jnp.where(causal, dz, 0.0)
                dzb = dz.astype(BF16)
                kbm = jnp.where(hm, kb, 0).astype(BF16)
                dq_acc = dq_acc + jnp.dot(dzb, kbm, preferred_element_type=F32)
                dk_acc[rows, :] += lax.dot_general(dzb, qm, _TN, preferred_element_type=F32)
                dv_acc[rows, :] += lax.dot_general(ab, dom, _TN, preferred_element_type=F32)
                return run + cs[:, 0:1], grun + gcs[:, 0:1], dq_acc

            zero = jnp.zeros((qb, 1), F32)
            carry = block(i, (zero, zero, jnp.zeros((qb, LANES), F32)), True)
            carry = lax.while_loop(
                lambda st: (st[0] >= 0) & (jnp.max(st[1]) > UNDERFLOW_LOG),
                lambda st: (st[0] - 1, *block(st[0], st[1:], False)),
                (i - 1, *carry))
            dq_out = dq_out + carry[3]
        dq_ref[...] = (dq_out * ATTN_SCALE).astype(BF16)

        @pl.when(i == nq - 1)
        def _():
            dk_ref[...] = dk_acc[...].astype(BF16)
            dv_ref[...] = dv_acc[...].astype(BF16)

    aw = hp * LANES
    return pl.pallas_call(
        body, name=name, grid=(hp, nq),
        in_specs=[pl.BlockSpec((qb, LANES), lambda p, i: (i, q0 + p)),
                  pl.BlockSpec((s, LANES), lambda p, i: (0, k0 + p)),
                  pl.BlockSpec((s, LANES), lambda p, i: (0, v0 + p)),
                  pl.BlockSpec((qb, LANES), lambda p, i: (i, p)),
                  pl.BlockSpec((qb, LANES), lambda p, i: (i, p)),
                  pl.BlockSpec((qb, qb), lambda p, i: (0, 0))],
        out_specs=[pl.BlockSpec((qb, LANES), lambda p, i: (i, p)),
                   pl.BlockSpec((s, LANES), lambda p, i: (0, p)),
                   pl.BlockSpec((s, LANES), lambda p, i: (0, p))],
        out_shape=[_sds((s, aw), BF16), _sds((s, aw), BF16), _sds((s, aw), BF16)],
        scratch_shapes=[pltpu.VMEM((s, LANES), F32), pltpu.VMEM((s, LANES), F32)],
        compiler_params=_params(("arbitrary", "arbitrary")),
    )(proj, proj, proj, o, do, tri)


def _layer_fwd(x, mod, gains, w, tri, *, tag):
    sh1, sc1, gt1, sh2, sc2, gt2 = mod
    g_pre_mix, g_post_mix, g_pre_mlp, g_post_mlp = gains
    d = x.shape[1]
    h, proj = _norm_mod_matmul(x, g_pre_mix, sc1, sh1, w["w_in"], name=f"in_proj_{tag}")
    yc = _conv_fwd(proj, w["conv_w"], name=f"conv_fwd_{tag}")
    o = _attn_fwd(proj, tri, d=d, name=f"attn_fwd_{tag}")
    ycv, yat, merged, mix, x1 = _mix_out(yc, o, proj, x, w["w_proj_conv"], w["w_proj_attn"], w["w_out"],
                                         g_post_mix, gt1, name=f"mix_out_{tag}")
    h2, a = _norm_mod_matmul(x1, g_pre_mlp, sc2, sh2, w["w_mlp_in"], name=f"mlp_in_{tag}")
    ff, x2 = _mlp_out(a, x1, w["w_mlp_out"], g_post_mlp, gt2, name=f"mlp_out_{tag}")
    saved = dict(x=x, h=h, proj=proj, yc=yc, o=o, ycv=ycv, yat=yat, merged=merged, mix=mix, x1=x1, h2=h2, a=a, ff=ff)
    return x2, saved


def _layer_bwd(dx2, sv, mod, gains, w, tri, *, tag):
    sh1, sc1, gt1, sh2, sc2, gt2 = mod
    g_pre_mix, g_post_mix, g_pre_mlp, g_post_mlp = gains
    d = dx2.shape[1]
    dff, da, dgt2, dg_post_mlp = _mlp_out_bwd(dx2, sv["ff"], sv["a"], w["w_mlp_out"], g_post_mlp, gt2,
                                              name=f"mlp_out_bwd_{tag}")
    gw_mlp_out = _matmul_tn(sv["a"], dff, relu2=True, name=f"gw_mlp_out_{tag}")
    dx1, dsh2, dsc2, dg_pre_mlp = _matmul_nt_norm_bwd(da, w["w_mlp_in"], sv["x1"], dx2, g_pre_mlp, sc2,
                                                      name=f"mlp_in_bwd_{tag}")
    gw_mlp_in = _matmul_tn(sv["h2"], da, name=f"gw_mlp_in_{tag}")
    dmix, dycv, dyat, dyc, do, dga, dgb, dgt1, dg_post_mix = _mix_out_bwd(
        dx1, sv["mix"], sv["proj"], sv["ycv"], sv["yat"], w["w_out"], w["w_proj_conv"], w["w_proj_attn"],
        g_post_mix, gt1, name=f"mix_out_bwd_{tag}")
    gw_out = _matmul_tn(sv["merged"], dmix, name=f"gw_out_{tag}")
    gw_proj_conv = _matmul_tn(sv["yc"], dycv, name=f"gw_proj_conv_{tag}")
    gw_proj_attn = _matmul_tn(sv["o"].astype(BF16), dyat, name=f"gw_proj_attn_{tag}")
    dbg, dcg, du, g_conv_w = _conv_bwd(dyc, sv["proj"], w["conv_w"], name=f"conv_bwd_{tag}")
    dq, dk, dv = _attn_bwd(sv["proj"], sv["o"], do, tri, d=d, name=f"attn_bwd_{tag}")
    dproj = jnp.concatenate([dbg, dcg, du, dq, dk, dv, dga, dgb], axis=1)
    dx0, dsh1, dsc1, dg_pre_mix = _matmul_nt_norm_bwd(dproj, w["w_in"], sv["x"], dx1, g_pre_mix, sc1,
                                                      name=f"in_proj_bwd_{tag}")
    gw_in = _matmul_tn(sv["h"], dproj, name=f"gw_in_{tag}")
    grads = dict(w_in=gw_in, conv_w=g_conv_w, w_proj_conv=gw_proj_conv, w_proj_attn=gw_proj_attn, w_out=gw_out,
                 w_mlp_in=gw_mlp_in, w_mlp_out=gw_mlp_out)
    dmod = jnp.concatenate([dsh1, dsc1, dgt1, dsh2, dsc2, dgt2], axis=0)
    dgains = jnp.concatenate([dg_pre_mix, dg_post_mix, dg_pre_mlp, dg_post_mlp], axis=0)
    return dx0, grads, dmod, dgains


BIG = ("w_in", "w_proj_conv", "w_proj_attn", "w_out", "w_mlp_in", "w_mlp_out")
SHARD_AXIS = dict(w_in=1, w_proj_conv=1, w_proj_attn=1, w_out=0, w_mlp_in=1, w_mlp_out=0)


def _local_step(x, target, mods, gains, wlayers, conv_w, on_grads=None):
    depth = mods.shape[0]
    tri = _tri(min(256, x.shape[0]))
    saved = []
    for l in range(depth):
        wl = dict(wlayers[l])
        wl["conv_w"] = conv_w[l]
        mod = [mods[l, k:k + 1] for k in range(N_MOD)]
        gl = [gains[l, k:k + 1] for k in range(4)]
        x, sv = _layer_fwd(x, mod, gl, wl, tri, tag=str(l))
        saved.append((sv, mod, gl, wl))
    dx, loss = _loss_grad(x, target, name="loss_grad")
    grads, dmods, dgains = [None] * depth, [None] * depth, [None] * depth
    for l in reversed(range(depth)):
        sv, mod, gl, wl = saved[l]
        dx, g, dmods[l], dgains[l] = _layer_bwd(dx, sv, mod, gl, wl, tri, tag=str(l))
        grads[l] = on_grads(l, g) if on_grads is not None else g
    return loss, dx, grads, jnp.stack(dmods), jnp.stack(dgains)


def _coords():
    return lax.axis_index("x"), lax.axis_index("y"), lax.axis_index("c")


def _flip(v, f):
    return 1 - v if f else v


def _all_gather_small(v, *, name):
    r, c_ = v.shape

    def body(v_ref, out_ref, send_sems, recv_sems, local_sem):
        x, y, c = _coords()
        me = 4 * x + 2 * y + c
        mine = pltpu.make_async_copy(v_ref, out_ref.at[me], local_sem)
        mine.start()
        copies = []
        for k in range(1, 8):
            fx, fy, fc = (k >> 2) & 1, (k >> 1) & 1, k & 1
            px, py, pc = _flip(x, fx), _flip(y, fy), _flip(c, fc)
            out = pltpu.make_async_remote_copy(src_ref=v_ref, dst_ref=out_ref.at[me], send_sem=send_sems.at[k - 1],
                                               recv_sem=recv_sems.at[k - 1], device_id=(px, py, pc), device_id_type=MESH)
            out.start()
            back = pltpu.make_async_remote_copy(src_ref=v_ref, dst_ref=out_ref.at[4 * px + 2 * py + pc],
                                                send_sem=send_sems.at[k - 1], recv_sem=recv_sems.at[k - 1],
                                                device_id=(px, py, pc), device_id_type=MESH)
            copies.append((out, back))
        for out, back in copies:
            back.wait_recv()
        for out, back in copies:
            out.wait_send()
        mine.wait()

    return pl.pallas_call(
        body, name=name,
        in_specs=[pl.BlockSpec(memory_space=pltpu.VMEM)],
        out_specs=pl.BlockSpec(memory_space=pltpu.VMEM),
        out_shape=_sds((8, r, c_), F32),
        scratch_shapes=[pltpu.SemaphoreType.DMA((7,)), pltpu.SemaphoreType.DMA((7,)), pltpu.SemaphoreType.DMA],
    )(v)


def _shard_dims(full_shape, axis):
    k, n = full_shape
    return (k // 4, n) if axis == 0 else (k, n // 4)


def _shard_window(ref, axis, chip, half, rows, cols):
    r0, rn = (0, rows) if half is None else (half * (rows // 2), rows // 2)
    if axis == 1:
        return ref.at[pl.ds(r0, rn), pl.ds(chip * cols, cols)]
    return ref.at[pl.ds(chip * rows + r0, rn), :]


def _cast_place(w, layer, axis, chip_arr, *, name, tr=256):
    _, rows, cols = w.shape
    tr = _fit(tr, rows)
    nb = rows // tr
    full = (rows * 4, cols) if axis == 0 else (rows, cols * 4)

    def body(chip_ref, w_ref, o_ref):
        o_ref[...] = w_ref[0].astype(BF16)

    if axis == 1:
        out_map = lambda i, chip: (i, chip[0])
    else:
        out_map = lambda i, chip: (chip[0] * nb + i, 0)
    grid_spec = pltpu.PrefetchScalarGridSpec(
        num_scalar_prefetch=1, grid=(nb,),
        in_specs=[pl.BlockSpec((1, tr, cols), lambda i, chip: (layer, i, 0))],
        out_specs=pl.BlockSpec((tr, cols), out_map))
    return pl.pallas_call(body, name=name, grid_spec=grid_spec, out_shape=_sds(full, BF16),
                          compiler_params=_params(("arbitrary",)))(chip_arr, w)


def _gather_weights(fulls, axes, *, name):
    n = len(fulls)

    def body(*refs):
        outs = refs[n:2 * n]
        send_sems, recv_sems = refs[2 * n:]
        x, y, c = _coords()
        chip = 2 * x + y
        sibling = (x, y, 1 - c)
        sends = []
        landing = []
        for w in range(n):
            rows, cols = _shard_dims(outs[w].shape, axes[w])
            win = functools.partial(_shard_window, outs[w], axes[w], rows=rows, cols=cols)
            for j, (fx, fy) in enumerate(OTHER_CHIPS):
                px, py = _flip(x, fx), _flip(y, fy)
                send = pltpu.make_async_remote_copy(src_ref=win(chip, c), dst_ref=win(chip, c), send_sem=send_sems.at[w, j],
                                                    recv_sem=recv_sems.at[w, j], device_id=(px, py, c), device_id_type=MESH)
                send.start()
                sends.append(send)
                landing.append((w, j, 2 * px + py, win))
        for w, j, pchip, win in landing:
            got = pltpu.make_async_remote_copy(src_ref=win(pchip, c), dst_ref=win(pchip, c), send_sem=send_sems.at[w, 3 + j],
                                               recv_sem=recv_sems.at[w, j], device_id=sibling, device_id_type=MESH)
            got.wait_recv()
            on = pltpu.make_async_remote_copy(src_ref=win(pchip, c), dst_ref=win(pchip, c), send_sem=send_sems.at[w, 3 + j],
                                              recv_sem=recv_sems.at[w, 3 + j], device_id=sibling, device_id_type=MESH)
            on.start()
            sends.append(on)
        for w, j, pchip, win in landing:
            other = pltpu.make_async_remote_copy(src_ref=win(pchip, 1 - c), dst_ref=win(pchip, 1 - c),
                                                 send_sem=send_sems.at[w, 3 + j], recv_sem=recv_sems.at[w, 3 + j],
                                                 device_id=sibling, device_id_type=MESH)
            other.wait_recv()
        for cp in sends:
            cp.wait_send()

    hbm = pl.BlockSpec(memory_space=pltpu.HBM)
    return pl.pallas_call(
        body, name=name,
        in_specs=[hbm] * n, out_specs=[hbm] * n, out_shape=[_sds(f.shape, f.dtype) for f in fulls],
        input_output_aliases={w: w for w in range(n)},
        scratch_shapes=[pltpu.SemaphoreType.DMA((n, 6)), pltpu.SemaphoreType.DMA((n, 6))],
    )(*fulls)


def _rs_exchange(grads, axes, *, name):
    n = len(grads)
    out_shapes = []
    for g, ax in zip(grads, axes):
        rows, cols = _shard_dims(g.shape, ax)
        out_shapes.append(_sds((7, rows // 2, cols), g.dtype))

    def body(*refs):
        ins, outs = refs[:n], refs[n:2 * n]
        send_sems, recv_sems = refs[2 * n:]
        x, y, c = _coords()
        copies = []
        for w in range(n):
            rows, cols = _shard_dims(ins[w].shape, axes[w])
            for k in range(1, 8):
                fx, fy, fc = (k >> 2) & 1, (k >> 1) & 1, k & 1
                px, py, pc = _flip(x, fx), _flip(y, fy), _flip(c, fc)
                piece = _shard_window(ins[w], axes[w], 2 * px + py, pc, rows, cols)
                cp = pltpu.make_async_remote_copy(src_ref=piece, dst_ref=outs[w].at[k - 1], send_sem=send_sems.at[w, k - 1],
                                                  recv_sem=recv_sems.at[w, k - 1], device_id=(px, py, pc), device_id_type=MESH)
                cp.start()
                copies.append(cp)
        for cp in copies:
            cp.wait_recv()
        for cp in copies:
            cp.wait_send()

    hbm = pl.BlockSpec(memory_space=pltpu.HBM)
    return pl.pallas_call(
        body, name=name,
        in_specs=[hbm] * n, out_specs=[hbm] * n, out_shape=out_shapes,
        scratch_shapes=[pltpu.SemaphoreType.DMA((n, 7)), pltpu.SemaphoreType.DMA((n, 7))],
    )(*grads)


def _rs_sum_join(g, got, out_prev, layer, depth, axis, ids, *, name, tr=256):
    _, rows2, cols = got.shape
    tr = _fit(tr, rows2)
    nt = rows2 // tr
    if axis == 1:
        own_map = lambda i, ids_: (ids_[1] * nt + i, ids_[0])
    else:
        own_map = lambda i, ids_: ((ids_[0] * 2 + ids_[1]) * nt + i, 0)

    def body(ids_ref, g_ref, got_ref, *rest):
        out_ref, buf, local_sems, send_sems, recv_sem = rest[-5:]
        i = pl.program_id(0)
        x, y, c = _coords()
        sibling = (x, y, 1 - c)

        def copies(step, slot):
            rows_mine = pl.ds(c * rows2 + step * tr, tr)
            dst = out_ref.at[layer, rows_mine, :]
            keep = pltpu.make_async_copy(buf.at[slot], dst, local_sems.at[slot])
            give = pltpu.make_async_remote_copy(src_ref=buf.at[slot], dst_ref=dst, send_sem=send_sems.at[slot],
                                                recv_sem=recv_sem, device_id=sibling, device_id_type=MESH)
            return keep, give

        def drain(step, slot):
            keep, give = copies(step, slot)
            keep.wait()
            give.wait_send()

        slot = i % 2

        @pl.when(i >= 2)
        def _():
            drain(i - 2, slot)

        acc = g_ref[...].astype(F32)
        for k in range(7):
            acc = acc + got_ref[k].astype(F32)
        buf[slot] = acc
        keep, give = copies(i, slot)
        keep.start()
        give.start()

        @pl.when(i == nt - 1)
        def _():
            if nt >= 2:
                drain(nt - 2, (nt - 2) % 2)
            drain(nt - 1, (nt - 1) % 2)
            theirs = out_ref.at[layer, pl.ds((1 - c) * rows2, rows2), :]
            pltpu.make_async_remote_copy(src_ref=theirs, dst_ref=theirs, send_sem=send_sems.at[0], recv_sem=recv_sem,
                                         device_id=sibling, device_id_type=MESH).wait_recv()

    hbm = pl.BlockSpec(memory_space=pltpu.HBM)
    in_specs = [pl.BlockSpec((tr, cols), own_map), pl.BlockSpec((7, tr, cols), lambda i, ids_: (0, i, 0))]
    operands = [ids, g, got]
    aliases = {}
    if out_prev is not None:
        in_specs.append(hbm)
        operands.append(out_prev)
        aliases = {3: 0}
    grid_spec = pltpu.PrefetchScalarGridSpec(
        num_scalar_prefetch=1, grid=(nt,), in_specs=in_specs, out_specs=hbm,
        scratch_shapes=[pltpu.VMEM((2, tr, cols), F32), pltpu.SemaphoreType.DMA((2,)), pltpu.SemaphoreType.DMA((2,)),
                        pltpu.SemaphoreType.DMA])
    return pl.pallas_call(body, name=name, grid_spec=grid_spec, out_shape=_sds((depth, 2 * rows2, cols), F32),
                          input_output_aliases=aliases, compiler_params=_params(("arbitrary",)))(*operands)


def _reduce_layer(grads, axes, layer, depth, ids, out_prev):
    got = _rs_exchange(grads, axes, name=f"rs_exchange_{layer}")
    outs = []
    for w, (g, ax) in enumerate(zip(grads, axes)):
        prev = None if out_prev is None else out_prev[w]
        outs.append(_rs_sum_join(g, got[w], prev, layer, depth, ax, ids, name=f"rs_sum_join_{layer}_{w}"))
    return outs


def _flat_rows(shape):
    rows = 1
    for s in shape[:-1]:
        rows *= s
    return rows, shape[-1]


def _row_tile(rows, cols, cap_bytes=2 * 1024 * 1024):
    t = rows
    while t * cols * 4 > cap_bytes and t % 16 == 0:
        t //= 2
    return t


def _ada_fwd(c_all, w_ada, b_loc, *, name, tn=512):
    l, d, nl = w_ada.shape
    b = c_all.shape[0]
    tn = min(tn, nl)

    def body(c_ref, w_ref, b_ref, o_ref):
        o_ref[0] = jnp.dot(c_ref[...], w_ref[0], preferred_element_type=F32,
                           precision=lax.Precision.HIGHEST) + b_ref[0]

    return pl.pallas_call(
        body, name=name, grid=(l, nl // tn),
        in_specs=[pl.BlockSpec((b, d), lambda i, j: (0, 0)), pl.BlockSpec((1, d, tn), lambda i, j: (i, 0, j)),
                  pl.BlockSpec((1, 1, tn), lambda i, j: (i, 0, j))],
        out_specs=pl.BlockSpec((1, b, tn), lambda i, j: (i, 0, j)),
        out_shape=_sds((l, b, nl), F32),
        compiler_params=_params(("parallel", "parallel")),
    )(c_all, w_ada, b_loc)


def _ada_bwd(c_t, dmod_loc, *, name, tn=512):
    d, b = c_t.shape
    l, _, nl = dmod_loc.shape
    tn = min(tn, nl)

    def body(c_ref, dm_ref, o_ref):
        cv = c_ref[...]
        dm = dm_ref[0]
        acc = cv[:, 0:1] * dm[0:1, :]
        for k in range(1, b):
            acc = acc + cv[:, k:k + 1] * dm[k:k + 1, :]
        o_ref[0] = acc

    return pl.pallas_call(
        body, name=name, grid=(l, nl // tn),
        in_specs=[pl.BlockSpec((d, b), lambda i, j: (0, 0)), pl.BlockSpec((1, b, tn), lambda i, j: (i, 0, j))],
        out_specs=pl.BlockSpec((1, d, tn), lambda i, j: (i, 0, j)),
        out_shape=_sds((l, d, nl), F32),
        compiler_params=_params(("parallel", "parallel")),
    )(c_t, dmod_loc)


def _sum_devices(p, *, name):
    k, r, c_ = p.shape

    def body(p_ref, o_ref):
        acc = p_ref[0]
        for j in range(1, k):
            acc = acc + p_ref[j]
        o_ref[...] = acc

    return pl.pallas_call(body, name=name, out_shape=_sds((r, c_), F32),
                          in_specs=[pl.BlockSpec(memory_space=pltpu.VMEM)],
                          out_specs=pl.BlockSpec(memory_space=pltpu.VMEM))(p)


def _adamw(w, g, m, v, *, name):
    shape = w.shape
    rows, cols = _flat_rows(shape)
    tr = _row_tile(rows, cols, cap_bytes=1024 * 1024)
    c1 = 1.0 / (1.0 - ADAM_B1 ** ADAM_STEP)
    c2 = 1.0 / (1.0 - ADAM_B2 ** ADAM_STEP)

    def body(w_ref, g_ref, m_ref, v_ref, d_ref, nm_ref, nv_ref):
        gv = g_ref[...]
        nm = ADAM_B1 * m_ref[...] + (1.0 - ADAM_B1) * gv
        nv = ADAM_B2 * v_ref[...] + (1.0 - ADAM_B2) * (gv * gv)
        m_hat = nm * c1
        v_hat = nv * c2
        d_ref[...] = -ADAM_LR * (m_hat / (jnp.sqrt(v_hat) + ADAM_EPS) + ADAM_WD * w_ref[...])
        nm_ref[...] = nm
        nv_ref[...] = nv

    spec = pl.BlockSpec((tr, cols), lambda i: (i, 0))
    flat = lambda a: a.reshape(rows, cols)
    outs = pl.pallas_call(body, name=name, grid=(rows // tr,), in_specs=[spec] * 4, out_specs=[spec] * 3,
                          out_shape=[_sds((rows, cols), F32)] * 3, compiler_params=_params(("parallel",)),
                          )(flat(w), flat(g), flat(m), flat(v))
    return tuple(o.reshape(shape) for o in outs)


WEIGHTS = ("w_ada", "b_ada", "g_pre_mix", "g_post_mix", "g_pre_mlp", "g_post_mlp", "w_in", "conv_w",
           "w_proj_conv", "w_proj_attn", "w_out", "w_mlp_in", "w_mlp_out")
GAINS = ("g_pre_mix", "g_post_mix", "g_pre_mlp", "g_post_mlp")


def kernel(x, c, w_ada, b_ada, g_pre_mix, g_post_mix, g_pre_mlp, g_post_mlp, w_in, conv_w, w_proj_conv, w_proj_attn, w_out, w_mlp_in, w_mlp_out, loss_target, m_w_ada, m_b_ada, m_g_pre_mix, m_g_post_mix, m_g_pre_mlp, m_g_post_mlp, m_w_in, m_conv_w, m_w_proj_conv, m_w_proj_attn, m_w_out, m_w_mlp_in, m_w_mlp_out, v_w_ada, v_b_ada, v_g_pre_mix, v_g_post_mix, v_g_pre_mlp, v_g_post_mlp, v_w_in, v_conv_w, v_w_proj_conv, v_w_proj_attn, v_w_out, v_w_mlp_in, v_w_mlp_out):
    params = dict(w_ada=w_ada, b_ada=b_ada, g_pre_mix=g_pre_mix, g_post_mix=g_post_mix, g_pre_mlp=g_pre_mlp,
                  g_post_mlp=g_post_mlp, w_in=w_in, conv_w=conv_w, w_proj_conv=w_proj_conv, w_proj_attn=w_proj_attn,
                  w_out=w_out, w_mlp_in=w_mlp_in, w_mlp_out=w_mlp_out)
    m_in = dict(w_ada=m_w_ada, b_ada=m_b_ada, g_pre_mix=m_g_pre_mix, g_post_mix=m_g_post_mix, g_pre_mlp=m_g_pre_mlp,
                g_post_mlp=m_g_post_mlp, w_in=m_w_in, conv_w=m_conv_w, w_proj_conv=m_w_proj_conv,
                w_proj_attn=m_w_proj_attn, w_out=m_w_out, w_mlp_in=m_w_mlp_in, w_mlp_out=m_w_mlp_out)
    v_in = dict(w_ada=v_w_ada, b_ada=v_b_ada, g_pre_mix=v_g_pre_mix, g_post_mix=v_g_post_mix, g_pre_mlp=v_g_pre_mlp,
                g_post_mlp=v_g_post_mlp, w_in=v_w_in, conv_w=v_conv_w, w_proj_conv=v_w_proj_conv,
                w_proj_attn=v_w_proj_attn, w_out=v_w_out, w_mlp_in=v_w_mlp_in, w_mlp_out=v_w_mlp_out)

    depth, d, nl_ada = w_ada.shape
    ix, iy, ic = lax.axis_index("x"), lax.axis_index("y"), lax.axis_index("c")
    chip = 2 * ix + iy
    me = 4 * ix + 2 * iy + ic
    xs = x[0]
    target = loss_target[0]

    c_all = _all_gather_small(jnp.broadcast_to(c, (8, d)), name="gather_c")[:, 0, :]
    b_loc = lax.dynamic_slice_in_dim(b_ada, chip * nl_ada, nl_ada, axis=1)[:, None, :]
    mod_loc = _ada_fwd(c_all, w_ada, b_loc, name="ada_fwd")
    mod_all = _all_gather_small(mod_loc.reshape(depth * 8, nl_ada), name="gather_mod")
    mod_all = mod_all.reshape(4, 2, depth, 8, nl_ada)[:, 0]
    mod_me = lax.dynamic_index_in_dim(mod_all, me, axis=2, keepdims=False)
    mods = jnp.transpose(mod_me, (1, 0, 2)).reshape(depth, N_MOD, d)

    chip_arr = jnp.reshape(chip, (1,)).astype(jnp.int32)
    ids = jnp.stack([chip, ic]).astype(jnp.int32)
    axes = [SHARD_AXIS[k] for k in BIG]
    placed = [_cast_place(params[k], l, SHARD_AXIS[k], chip_arr, name=f"place_{k}_{l}")
              for l in range(depth) for k in BIG]
    full = _gather_weights(placed, axes * depth, name="gather_weights")
    wlayers = [dict(zip(BIG, full[l * len(BIG):(l + 1) * len(BIG)])) for l in range(depth)]
    conv_full = _all_gather_small(
        jnp.pad(conv_w.reshape(depth * 3, -1), ((0, 8 - depth * 3), (0, 0))), name="gather_conv_w")
    conv_full = conv_full.reshape(4, 2, 8, -1)[:, 0, :depth * 3]
    conv_full = jnp.transpose(conv_full, (1, 0, 2)).reshape(depth, 3, -1)

    gains = jnp.stack([params[k] for k in GAINS], axis=1)
    reduced = [None]

    def reduce_now(l, g):
        reduced[0] = _reduce_layer([g[k] for k in BIG], axes, l, depth, ids, reduced[0])
        return g["conv_w"]

    loss, dx, conv_grads, dmods, dgains = _local_step(xs, target, mods, gains, wlayers, conv_full, reduce_now)

    cw = conv_full.shape[2]
    rows = [dmods.reshape(depth * N_MOD, d), dgains.reshape(depth * 4, d),
            jnp.stack(conv_grads).reshape(-1, d), jnp.broadcast_to(loss, (1, d))]
    payload = jnp.concatenate(rows, axis=0)
    n_rows = payload.shape[0]
    pad = (-n_rows) % 8
    payload = jnp.pad(payload, ((0, pad), (0, 0)))
    everyone = _all_gather_small(payload, name="gather_small_grads")
    total = _sum_devices(everyone, name="sum_small_grads")
    r0 = depth * N_MOD
    grads = {}
    grads["b_ada"] = total[:r0].reshape(depth, N_MOD * d)
    gsum = total[r0:r0 + depth * 4].reshape(depth, 4, d)
    for k, name in enumerate(GAINS):
        grads[name] = gsum[:, k]
    r1 = r0 + depth * 4
    n_conv = (depth * 3 * cw) // d
    conv_g = total[r1:r1 + n_conv].reshape(depth, 3, cw)
    grads["conv_w"] = lax.dynamic_slice_in_dim(conv_g, chip * (cw // 4), cw // 4, axis=2)
    loss_out = total[r1 + n_conv, 0]
    dmod_all = everyone[:, :r0].reshape(8, depth, N_MOD * d)
    dmod_loc = lax.dynamic_slice_in_dim(dmod_all, chip * nl_ada, nl_ada, axis=2)
    grads["w_ada"] = _ada_bwd(c_all.T, jnp.transpose(dmod_loc, (1, 0, 2)), name="ada_bwd")

    for k, g in zip(BIG, reduced[0]):
        grads[k] = g

    deltas, new_m, new_v = {}, {}, {}
    for k in WEIGHTS:
        deltas[k], new_m[k], new_v[k] = _adamw(params[k], grads[k], m_in[k], v_in[k], name=f"adamw_{k}")

    return (loss_out, dx[None], *[grads[k] for k in WEIGHTS], *[deltas[k] for k in WEIGHTS],
            *[new_m[k] for k in WEIGHTS], *[new_v[k] for k in WEIGHTS])
```

```python
import functools

import jax
import jax.numpy as jnp
from jax import lax
from jax.experimental import pallas as pl
from jax.experimental.pallas import tpu as pltpu

F32 = jnp.float32
BF16 = jnp.bfloat16
EPS = 1e-6
N_MOD = 6
HEAD_DIM = 64
LANES = 128
ATTN_SCALE = 1.0 / 8.0
UNDERFLOW_LOG = -90.0
ADAM_LR = 0.001
ADAM_B1 = 0.9
ADAM_B2 = 0.999
ADAM_EPS = 1e-08
ADAM_WD = 0.01
ADAM_STEP = 10
VMEM_LIMIT = 56 * 1024 * 1024
MESH = pl.DeviceIdType.MESH
OTHER_CHIPS = ((1, 0), (0, 1), (1, 1))

_NT = (((1,), (1,)), ((), ()))
_TN = (((0,), (0,)), ((), ()))


def _sds(shape, dtype):
    return jax.ShapeDtypeStruct(shape, dtype)


def _params(sem):
    return pltpu.CompilerParams(dimension_semantics=sem, vmem_limit_bytes=VMEM_LIMIT)


def _fit(t, n):
    t = min(t, n)
    while n % t:
        t //= 2
    return t


def _vec_spec(d, nargs):
    if nargs == 1:
        return pl.BlockSpec((1, d), lambda i: (0, 0))
    return pl.BlockSpec((1, d), lambda i, j: (0, 0))


def _log_one_minus_sigmoid(z):
    return -jnp.log(1.0 + jnp.exp(-jnp.abs(z))) - jnp.maximum(z, 0.0)


def _sigmoid(z):
    t = jnp.exp(-jnp.abs(z))
    return jnp.where(z >= 0.0, 1.0, t) / (1.0 + t)


def _split_bf16(a):
    hi = a.astype(BF16)
    lo = (a - hi.astype(F32)).astype(BF16)
    return hi, lo


def _rms_bwd(dn, xin, g):
    r = lax.rsqrt(jnp.mean(xin * xin, axis=-1, keepdims=True) + EPS)
    xh = xin * r
    dxh = dn * g
    dxin = r * (dxh - xh * jnp.mean(dxh * xh, axis=-1, keepdims=True))
    return dxin, xh


def _colsum(a):
    return jnp.sum(a, axis=0, keepdims=True)


def _norm_mod_matmul(x, g, sc, sh, w, *, name, tm=1024, tn=512):
    s, d = x.shape
    n = w.shape[1]
    tm, tn = _fit(tm, s), _fit(tn, n)

    def body(x_ref, g_ref, sc_ref, sh_ref, w_ref, h_ref, o_ref):
        @pl.when(pl.program_id(1) == 0)
        def _():
            xv = x_ref[...]
            r = lax.rsqrt(jnp.mean(xv * xv, axis=-1, keepdims=True) + EPS)
            h_ref[...] = ((xv * r * g_ref[...]) * (1.0 + sc_ref[...]) + sh_ref[...]).astype(BF16)
        o_ref[...] = jnp.dot(h_ref[...], w_ref[...], preferred_element_type=F32).astype(BF16)

    return pl.pallas_call(
        body, name=name, grid=(s // tm, n // tn),
        in_specs=[pl.BlockSpec((tm, d), lambda i, j: (i, 0)), _vec_spec(d, 2), _vec_spec(d, 2), _vec_spec(d, 2),
                  pl.BlockSpec((d, tn), lambda i, j: (0, j))],
        out_specs=[pl.BlockSpec((tm, d), lambda i, j: (i, 0)), pl.BlockSpec((tm, tn), lambda i, j: (i, j))],
        out_shape=[_sds((s, d), BF16), _sds((s, n), BF16)],
        compiler_params=_params(("parallel", "arbitrary")),
    )(x, g, sc, sh, w)


HALO = 16


def _conv_fwd(proj, conv_w, *, name, tm=512):
    s = proj.shape[0]
    cw = conv_w.shape[1]
    tm = min(tm, s)
    nb = tm // HALO

    def body(bg_ref, cg_ref, u_ref, cgh_ref, uh_ref, w_ref, yc_ref, vbuf):
        i = pl.program_id(0)
        vv = cg_ref[...].astype(F32) * u_ref[...].astype(F32)
        halo = cgh_ref[...].astype(F32) * uh_ref[...].astype(F32)
        vbuf[0:HALO, :] = jnp.where(i > 0, halo, 0.0)
        vbuf[HALO:HALO + tm, :] = vv
        v1 = vbuf[HALO - 1:HALO - 1 + tm, :]
        v2 = vbuf[HALO - 2:HALO - 2 + tm, :]
        w = w_ref[...]
        y = w[2:3, :] * vv + w[1:2, :] * v1 + w[0:1, :] * v2
        yc_ref[...] = (bg_ref[...].astype(F32) * y).astype(BF16)

    def prev(i):
        return jnp.maximum(i * nb - 1, 0)

    return pl.pallas_call(
        body, name=name, grid=(s // tm,),
        in_specs=[pl.BlockSpec((tm, cw), lambda i: (i, 0)), pl.BlockSpec((tm, cw), lambda i: (i, 1)),
                  pl.BlockSpec((tm, cw), lambda i: (i, 2)),
                  pl.BlockSpec((HALO, cw), lambda i: (prev(i), 1)), pl.BlockSpec((HALO, cw), lambda i: (prev(i), 2)),
                  pl.BlockSpec((3, cw), lambda i: (0, 0))],
        out_specs=pl.BlockSpec((tm, cw), lambda i: (i, 0)),
        out_shape=_sds((s, cw), BF16),
        scratch_shapes=[pltpu.VMEM((HALO + tm, cw), F32)],
        compiler_params=_params(("arbitrary",)),
    )(proj, proj, proj, proj, proj, conv_w)


def _tri(qb):
    r = lax.broadcasted_iota(jnp.int32, (qb, qb), 0)
    c = lax.broadcasted_iota(jnp.int32, (qb, qb), 1)
    return (r >= c).astype(BF16)


def _head_mask(h):
    lane = lax.broadcasted_iota(jnp.int32, (1, LANES), 1)
    return (lane >= HEAD_DIM * h) & (lane < HEAD_DIM * (h + 1))


def _stack_heads(a, masks):
    return jnp.concatenate([jnp.where(m, a, 0).astype(BF16) for m in masks], axis=0)


def _heads_to_lanes(a, qb):
    return jnp.concatenate([a[:qb], a[qb:]], axis=1)


def _stacked_causal(qb):
    row = lax.broadcasted_iota(jnp.int32, (2 * qb, qb), 0)
    col = lax.broadcasted_iota(jnp.int32, (2 * qb, qb), 1)
    return col < jnp.where(row >= qb, row - qb, row)


def _running_sum(a, tri_m):
    n = a.shape[0]
    hi, lo = _split_bf16(a)
    both = jnp.dot(jnp.concatenate([hi, lo], axis=0), tri_m, preferred_element_type=F32)
    return both[:n] + both[n:]


def _attn_cols(d):
    cw = d // 2
    hp = (d // 2) // LANES
    q0 = (3 * cw) // LANES
    return q0, q0 + hp, q0 + 2 * hp, hp


def _attn_fwd(proj, tri, *, d, name, qb=256):
    s = proj.shape[0]
    qb = min(qb, s)
    q0, k0, v0, hp = _attn_cols(d)

    def body(q_ref, k_ref, v_ref, tri_ref, o_ref):
        i = pl.program_id(1)
        causal = _stacked_causal(qb)
        tri_m = tri_ref[...]
        masks = [_head_mask(h) for h in range(2)]
        qs = _stack_heads(q_ref[...] * ATTN_SCALE, masks)

        def block(j, state, diag):
            run, acc = state
            rows = pl.ds(pl.multiple_of(j * qb, qb), qb)
            z = lax.dot_general(qs, k_ref[rows, :], _NT, preferred_element_type=F32)
            lg = _log_one_minus_sigmoid(z)
            if diag:
                lg = jnp.where(causal, lg, 0.0)
            cs = _running_sum(lg, tri_m)
            a = jnp.exp(z + cs + run)
            if diag:
                a = jnp.where(causal, a, 0.0)
            ab = a.astype(BF16)
            acc = acc + jnp.dot(_heads_to_lanes(ab, qb), _stack_heads(v_ref[rows, :], masks),
                                preferred_element_type=F32)
            return run + cs[:, 0:1], acc

        state = block(i, (jnp.zeros((2 * qb, 1), F32), jnp.zeros((qb, LANES), F32)), True)
        state = lax.while_loop(
            lambda st: (st[0] >= 0) & (jnp.max(st[1]) > UNDERFLOW_LOG),
            lambda st: (st[0] - 1, *block(st[0], st[1:], False)),
            (i - 1, *state))
        o_ref[...] = state[2]

    return pl.pallas_call(
        body, name=name, grid=(hp, s // qb),
        in_specs=[pl.BlockSpec((qb, LANES), lambda p, i: (i, q0 + p)),
                  pl.BlockSpec((s, LANES), lambda p, i: (0, k0 + p)),
                  pl.BlockSpec((s, LANES), lambda p, i: (0, v0 + p)),
                  pl.BlockSpec((qb, qb), lambda p, i: (0, 0))],
        out_specs=pl.BlockSpec((qb, LANES), lambda p, i: (i, p)),
        out_shape=_sds((s, hp * LANES), F32),
        compiler_params=_params(("parallel", "arbitrary")),
    )(proj, proj, proj, tri)


def _mix_out(yc, o, proj, x, wpc, wpa, wout, g, gt, *, name, tm=256):
    s, d = x.shape
    cw = yc.shape[1]
    tm = min(tm, s)
    ga_blk = (3 * cw + 3 * (d // 2)) // d

    def body(yc_ref, o_ref, ga_ref, gb_ref, x_ref, wpc_ref, wpa_ref, wout_ref, g_ref, gt_ref,
             ycv_ref, yat_ref, mg_ref, mix_ref, x1_ref):
        y_conv = jnp.dot(yc_ref[...], wpc_ref[...], preferred_element_type=F32)
        y_attn = jnp.dot(o_ref[...].astype(BF16), wpa_ref[...], preferred_element_type=F32)
        merged = (_sigmoid(ga_ref[...].astype(F32)) * y_conv + _sigmoid(gb_ref[...].astype(F32)) * y_attn)
        mg = merged.astype(BF16)
        mix = jnp.dot(mg, wout_ref[...], preferred_element_type=F32)
        r = lax.rsqrt(jnp.mean(mix * mix, axis=-1, keepdims=True) + EPS)
        ycv_ref[...] = y_conv.astype(BF16)
        yat_ref[...] = y_attn.astype(BF16)
        mg_ref[...] = mg
        mix_ref[...] = mix
        x1_ref[...] = x_ref[...] + gt_ref[...] * (mix * r * g_ref[...])

    def rows(w):
        return pl.BlockSpec((tm, w), lambda i: (i, 0))

    def full(a):
        return pl.BlockSpec(a.shape, lambda i: (0, 0))

    return pl.pallas_call(
        body, name=name, grid=(s // tm,),
        in_specs=[rows(cw), rows(d // 2), pl.BlockSpec((tm, d), lambda i: (i, ga_blk)),
                  pl.BlockSpec((tm, d), lambda i: (i, ga_blk + 1)), rows(d),
                  full(wpc), full(wpa), full(wout), _vec_spec(d, 1), _vec_spec(d, 1)],
        out_specs=[rows(d), rows(d), rows(d), rows(d), rows(d)],
        out_shape=[_sds((s, d), BF16), _sds((s, d), BF16), _sds((s, d), BF16), _sds((s, d), F32), _sds((s, d), F32)],
        compiler_params=_params(("parallel",)),
    )(yc, o, proj, proj, x, wpc, wpa, wout, g, gt)


def _relu2(a):
    r = jnp.maximum(a.astype(F32), 0.0)
    return (r * r).astype(BF16)


def _mlp_out(a, x, w2, g, gt, *, name, tm=512):
    s, d = x.shape
    dff = a.shape[1]
    tm = min(tm, s)

    def body(a_ref, x_ref, w_ref, g_ref, gt_ref, ff_ref, x2_ref):
        ff = jnp.dot(_relu2(a_ref[...]), w_ref[...], preferred_element_type=F32)
        r = lax.rsqrt(jnp.mean(ff * ff, axis=-1, keepdims=True) + EPS)
        ff_ref[...] = ff
        x2_ref[...] = x_ref[...] + gt_ref[...] * (ff * r * g_ref[...])

    return pl.pallas_call(
        body, name=name, grid=(s // tm,),
        in_specs=[pl.BlockSpec((tm, dff), lambda i: (i, 0)), pl.BlockSpec((tm, d), lambda i: (i, 0)),
                  pl.BlockSpec((dff, d), lambda i: (0, 0)), _vec_spec(d, 1), _vec_spec(d, 1)],
        out_specs=[pl.BlockSpec((tm, d), lambda i: (i, 0)), pl.BlockSpec((tm, d), lambda i: (i, 0))],
        out_shape=[_sds((s, d), F32), _sds((s, d), F32)],
        compiler_params=_params(("parallel",)),
    )(a, x, w2, g, gt)


def _loss_grad(y, target, *, name, tm=512):
    s, d = y.shape
    tm = min(tm, s)

    def body(y_ref, t_ref, dy_ref, loss_ref):
        @pl.when(pl.program_id(0) == 0)
        def _():
            loss_ref[...] = jnp.zeros_like(loss_ref)
        e = y_ref[...] - t_ref[...]
        dy_ref[...] = e * (1.0 / d)
        loss_ref[...] += 0.5 * jnp.sum(jnp.mean(e * e, axis=-1, keepdims=True), axis=0, keepdims=True)

    return pl.pallas_call(
        body, name=name, grid=(s // tm,),
        in_specs=[pl.BlockSpec((tm, d), lambda i: (i, 0)), pl.BlockSpec((tm, d), lambda i: (i, 0))],
        out_specs=[pl.BlockSpec((tm, d), lambda i: (i, 0)), pl.BlockSpec((1, 1), lambda i: (0, 0))],
        out_shape=[_sds((s, d), F32), _sds((1, 1), F32)],
        compiler_params=_params(("arbitrary",)),
    )(y, target)


def _mlp_out_bwd(dx, ff, a, w2, g, gt, *, name, tm=256):
    s, d = dx.shape
    dff = a.shape[1]
    tm = min(tm, s)

    def body(dx_ref, ff_ref, a_ref, w_ref, g_ref, gt_ref, dff_ref, da_ref, dgt_ref, dg_ref):
        @pl.when(pl.program_id(0) == 0)
        def _():
            dgt_ref[...] = jnp.zeros_like(dgt_ref)
            dg_ref[...] = jnp.zeros_like(dg_ref)
        dxv = dx_ref[...]
        dn = dxv * gt_ref[...]
        dffv, xh = _rms_bwd(dn, ff_ref[...], g_ref[...])
        dgt_ref[...] += _colsum(dxv * (xh * g_ref[...]))
        dg_ref[...] += _colsum(dn * xh)
        dffb = dffv.astype(BF16)
        dff_ref[...] = dffb
        drr = lax.dot_general(dffb, w_ref[...], _NT, preferred_element_type=F32)
        da_ref[...] = (drr * (2.0 * jnp.maximum(a_ref[...].astype(F32), 0.0))).astype(BF16)

    return pl.pallas_call(
        body, name=name, grid=(s // tm,),
        in_specs=[pl.BlockSpec((tm, d), lambda i: (i, 0)), pl.BlockSpec((tm, d), lambda i: (i, 0)),
                  pl.BlockSpec((tm, dff), lambda i: (i, 0)), pl.BlockSpec((dff, d), lambda i: (0, 0)),
                  _vec_spec(d, 1), _vec_spec(d, 1)],
        out_specs=[pl.BlockSpec((tm, d), lambda i: (i, 0)), pl.BlockSpec((tm, dff), lambda i: (i, 0)),
                   _vec_spec(d, 1), _vec_spec(d, 1)],
        out_shape=[_sds((s, d), BF16), _sds((s, dff), BF16), _sds((1, d), F32), _sds((1, d), F32)],
        compiler_params=_params(("arbitrary",)),
    )(dx, ff, a, w2, g, gt)


def _matmul_nt_norm_bwd(dy, w, x, dres, g, sc, *, name, tm=512, tn=512):
    s, n = dy.shape
    d = w.shape[0]
    tm, tn = _fit(tm, s), _fit(tn, n)
    nj = n // tn

    def body(dy_ref, w_ref, x_ref, dres_ref, g_ref, sc_ref, dx_ref, dsh_ref, dsc_ref, dg_ref, acc):
        i, j = pl.program_id(0), pl.program_id(1)

        @pl.when((i == 0) & (j == 0))
        def _():
            dsh_ref[...] = jnp.zeros_like(dsh_ref)
            dsc_ref[...] = jnp.zeros_like(dsc_ref)
            dg_ref[...] = jnp.zeros_like(dg_ref)

        @pl.when(j == 0)
        def _():
            acc[...] = jnp.zeros_like(acc)

        acc[...] += lax.dot_general(dy_ref[...], w_ref[...], _NT, preferred_element_type=F32)

        @pl.when(j == nj - 1)
        def _():
            dh = acc[...]
            dn = dh * (1.0 + sc_ref[...])
            dxin, xh = _rms_bwd(dn, x_ref[...], g_ref[...])
            dsh_ref[...] += _colsum(dh)
            dsc_ref[...] += _colsum(dh * (xh * g_ref[...]))
            dg_ref[...] += _colsum(dn * xh)
            dx_ref[...] = dres_ref[...] + dxin

    return pl.pallas_call(
        body, name=name, grid=(s // tm, nj),
        in_specs=[pl.BlockSpec((tm, tn), lambda i, j: (i, j)), pl.BlockSpec((d, tn), lambda i, j: (0, j)),
                  pl.BlockSpec((tm, d), lambda i, j: (i, 0)), pl.BlockSpec((tm, d), lambda i, j: (i, 0)),
                  _vec_spec(d, 2), _vec_spec(d, 2)],
        out_specs=[pl.BlockSpec((tm, d), lambda i, j: (i, 0)), _vec_spec(d, 2), _vec_spec(d, 2), _vec_spec(d, 2)],
        out_shape=[_sds((s, d), F32), _sds((1, d), F32), _sds((1, d), F32), _sds((1, d), F32)],
        scratch_shapes=[pltpu.VMEM((tm, d), F32)],
        compiler_params=_params(("arbitrary", "arbitrary")),
    )(dy, w, x, dres, g, sc)


def _matmul_tn(a, b, *, name, tk=1024, tn=1024, ts=512, relu2=False):
    s, k = a.shape
    n = b.shape[1]
    tk, tn, ts = _fit(tk, k), _fit(tn, n), _fit(ts, s)
    nt = s // ts

    def body(a_ref, b_ref, o_ref, acc):
        t = pl.program_id(2)

        @pl.when(t == 0)
        def _():
            acc[...] = jnp.zeros_like(acc)
        av = a_ref[...]
        if relu2:
            av = _relu2(av)
        acc[...] += lax.dot_general(av, b_ref[...], _TN, preferred_element_type=F32)

        @pl.when(t == nt - 1)
        def _():
            o_ref[...] = acc[...].astype(BF16)

    return pl.pallas_call(
        body, name=name, grid=(k // tk, n // tn, nt),
        in_specs=[pl.BlockSpec((ts, tk), lambda i, j, t: (t, i)), pl.BlockSpec((ts, tn), lambda i, j, t: (t, j))],
        out_specs=pl.BlockSpec((tk, tn), lambda i, j, t: (i, j)),
        out_shape=_sds((k, n), BF16),
        scratch_shapes=[pltpu.VMEM((tk, tn), F32)],
        compiler_params=_params(("parallel", "parallel", "arbitrary")),
    )(a, b)


def _mix_out_bwd(dx, mix, proj, ycv, yat, wout, wpc, wpa, g, gt, *, name, tm=256):
    s, d = dx.shape
    cw = wpc.shape[0]
    aw = wpa.shape[0]
    tm = min(tm, s)
    ga_blk = (3 * cw + 3 * aw) // d

    def body(dx_ref, mix_ref, ga_ref, gb_ref, ycv_ref, yat_ref, wout_ref, wpc_ref, wpa_ref, g_ref, gt_ref,
             dmix_ref, dycv_ref, dyat_ref, dyc_ref, do_ref, dga_ref, dgb_ref, dgt_ref, dg_ref):
        @pl.when(pl.program_id(0) == 0)
        def _():
            dgt_ref[...] = jnp.zeros_like(dgt_ref)
            dg_ref[...] = jnp.zeros_like(dg_ref)
        dxv = dx_ref[...]
        dn = dxv * gt_ref[...]
        dmix, xh = _rms_bwd(dn, mix_ref[...], g_ref[...])
        dgt_ref[...] += _colsum(dxv * (xh * g_ref[...]))
        dg_ref[...] += _colsum(dn * xh)
        dmixb = dmix.astype(BF16)
        dmix_ref[...] = dmixb
        dmerged = lax.dot_general(dmixb, wout_ref[...], _NT, preferred_element_type=F32)
        sga = _sigmoid(ga_ref[...].astype(F32))
        sgb = _sigmoid(gb_ref[...].astype(F32))
        dycv = (dmerged * sga).astype(BF16)
        dyat = (dmerged * sgb).astype(BF16)
        dycv_ref[...] = dycv
        dyat_ref[...] = dyat
        dga_ref[...] = (dmerged * ycv_ref[...].astype(F32) * (sga * (1.0 - sga))).astype(BF16)
        dgb_ref[...] = (dmerged * yat_ref[...].astype(F32) * (sgb * (1.0 - sgb))).astype(BF16)
        dyc_ref[...] = lax.dot_general(dycv, wpc_ref[...], _NT, preferred_element_type=F32).astype(BF16)
        do_ref[...] = lax.dot_general(dyat, wpa_ref[...], _NT, preferred_element_type=F32).astype(BF16)

    def rows(w):
        return pl.BlockSpec((tm, w), lambda i: (i, 0))

    def full(a):
        return pl.BlockSpec(a.shape, lambda i: (0, 0))

    return pl.pallas_call(
        body, name=name, grid=(s // tm,),
        in_specs=[rows(d), rows(d), pl.BlockSpec((tm, d), lambda i: (i, ga_blk)),
                  pl.BlockSpec((tm, d), lambda i: (i, ga_blk + 1)), rows(d), rows(d),
                  full(wout), full(wpc), full(wpa), _vec_spec(d, 1), _vec_spec(d, 1)],
        out_specs=[rows(d), rows(d), rows(d), rows(cw), rows(aw), rows(d), rows(d), _vec_spec(d, 1), _vec_spec(d, 1)],
        out_shape=[_sds((s, d), BF16), _sds((s, d), BF16), _sds((s, d), BF16), _sds((s, cw), BF16),
                   _sds((s, aw), BF16), _sds((s, d), BF16), _sds((s, d), BF16), _sds((1, d), F32), _sds((1, d), F32)],
        compiler_params=_params(("arbitrary",)),
    )(dx, mix, proj, proj, ycv, yat, wout, wpc, wpa, g, gt)


def _conv_bwd(dyc, proj, conv_w, *, name, tm=512):
    s = proj.shape[0]
    cw = conv_w.shape[1]
    tm = min(tm, s)
    nb = tm // HALO
    nt = s // tm
    last_blk = s // HALO - 1

    def body(dyc_ref, bg_ref, cg_ref, u_ref, cgh_ref, uh_ref, dych_ref, bgh_ref, w_ref,
             dbg_ref, dcg_ref, du_ref, dw_ref, vbuf, gbuf):
        i = pl.program_id(0)

        @pl.when(i == 0)
        def _():
            dw_ref[...] = jnp.zeros_like(dw_ref)

        cg = cg_ref[...].astype(F32)
        u = u_ref[...].astype(F32)
        vv = cg * u
        halo = cgh_ref[...].astype(F32) * uh_ref[...].astype(F32)
        vbuf[0:HALO, :] = jnp.where(i > 0, halo, 0.0)
        vbuf[HALO:HALO + tm, :] = vv
        v1 = vbuf[HALO - 1:HALO - 1 + tm, :]
        v2 = vbuf[HALO - 2:HALO - 2 + tm, :]
        w = w_ref[...]
        y = w[2:3, :] * vv + w[1:2, :] * v1 + w[0:1, :] * v2
        dyc = dyc_ref[...].astype(F32)
        dbg_ref[...] = (dyc * y).astype(BF16)
        gy = dyc * bg_ref[...].astype(F32)
        nxt = dych_ref[...].astype(F32) * bgh_ref[...].astype(F32)
        gbuf[0:tm, :] = gy
        gbuf[tm:tm + HALO, :] = jnp.where(i < nt - 1, nxt, 0.0)
        g1 = gbuf[1:1 + tm, :]
        g2 = gbuf[2:2 + tm, :]
        dvv = w[2:3, :] * gy + w[1:2, :] * g1 + w[0:1, :] * g2
        dcg_ref[...] = (dvv * u).astype(BF16)
        du_ref[...] = (dvv * cg).astype(BF16)
        dw_ref[0:1, :] += _colsum(gy * v2)
        dw_ref[1:2, :] += _colsum(gy * v1)
        dw_ref[2:3, :] += _colsum(gy * vv)

    def prev(i):
        return jnp.maximum(i * nb - 1, 0)

    def nxt_blk(i):
        return jnp.minimum((i + 1) * nb, last_blk)

    def col(c):
        return pl.BlockSpec((tm, cw), lambda i: (i, c))

    return pl.pallas_call(
        body, name=name, grid=(nt,),
        in_specs=[col(0), col(0), col(1), col(2),
                  pl.BlockSpec((HALO, cw), lambda i: (prev(i), 1)), pl.BlockSpec((HALO, cw), lambda i: (prev(i), 2)),
                  pl.BlockSpec((HALO, cw), lambda i: (nxt_blk(i), 0)), pl.BlockSpec((HALO, cw), lambda i: (nxt_blk(i), 0)),
                  pl.BlockSpec((3, cw), lambda i: (0, 0))],
        out_specs=[col(0), col(0), col(0), pl.BlockSpec((3, cw), lambda i: (0, 0))],
        out_shape=[_sds((s, cw), BF16), _sds((s, cw), BF16), _sds((s, cw), BF16), _sds((3, cw), F32)],
        scratch_shapes=[pltpu.VMEM((HALO + tm, cw), F32), pltpu.VMEM((tm + HALO, cw), F32)],
        compiler_params=_params(("arbitrary",)),
    )(dyc, proj, proj, proj, proj, proj, dyc, proj, conv_w)


def _attn_bwd(proj, o, do, tri, *, d, name, qb=256):
    s = proj.shape[0]
    qb = min(qb, s)
    nq = s // qb
    q0, k0, v0, hp = _attn_cols(d)

    def body(q_ref, k_ref, v_ref, o_ref, do_ref, tri_ref, dq_ref, dk_ref, dv_ref, dk_acc, dv_acc):
        i = pl.program_id(1)

        @pl.when(i == 0)
        def _():
            dk_acc[...] = jnp.zeros_like(dk_acc)
            dv_acc[...] = jnp.zeros_like(dv_acc)

        causal = _stacked_causal(qb)
        tri_m = tri_ref[...]
        dov = do_ref[...]
        masks = [_head_mask(h) for h in range(2)]
        qs = _stack_heads(q_ref[...] * ATTN_SCALE, masks)
        dos = _stack_heads(dov, masks)
        dprod = dov.astype(F32) * o_ref[...]
        dtot = jnp.concatenate([jnp.sum(jnp.where(m, dprod, 0.0), axis=-1, keepdims=True) for m in masks], axis=0)

        def block(j, state, diag):
            run, grun, dq_acc = state
            rows = pl.ds(pl.multiple_of(j * qb, qb), qb)
            kb = k_ref[rows, :]
            z = lax.dot_general(qs, kb, _NT, preferred_element_type=F32)
            lg = _log_one_minus_sigmoid(z)
            beta = 1.0 - jnp.exp(lg)
            if diag:
                lg = jnp.where(causal, lg, 0.0)
            cs = _running_sum(lg, tri_m)
            a = jnp.exp(z + cs + run)
            if diag:
                a = jnp.where(causal, a, 0.0)
            ab = a.astype(BF16)
            da = lax.dot_general(dos, v_ref[rows, :], _NT, preferred_element_type=F32)
            gg = ab.astype(F32) * da
            gcs = _running_sum(gg, tri_m)
            left = jnp.where(run > UNDERFLOW_LOG, dtot - grun, 0.0)
            dz = gg - beta * (gg + (left - gcs))
            if diag:
                dz = jnp.where(causal, dz, 0.0)
            dzb = dz.astype(BF16)
            dq_acc = dq_acc + jnp.dot(_heads_to_lanes(dzb, qb), _stack_heads(kb, masks), preferred_element_type=F32)
            dk_acc[rows, :] += lax.dot_general(dzb, qs, _TN, preferred_element_type=F32)
            dv_acc[rows, :] += lax.dot_general(ab, dos, _TN, preferred_element_type=F32)
            return run + cs[:, 0:1], grun + gcs[:, 0:1], dq_acc

        zero = jnp.zeros((2 * qb, 1), F32)
        state = block(i, (zero, zero, jnp.zeros((qb, LANES), F32)), True)
        state = lax.while_loop(
            lambda st: (st[0] >= 0) & (jnp.max(st[1]) > UNDERFLOW_LOG),
            lambda st: (st[0] - 1, *block(st[0], st[1:], False)),
            (i - 1, *state))
        dq_ref[...] = (state[3] * ATTN_SCALE).astype(BF16)

        @pl.when(i == nq - 1)
        def _():
            dk_ref[...] = dk_acc[...].astype(BF16)
            dv_ref[...] = dv_acc[...].astype(BF16)

    aw = hp * LANES
    return pl.pallas_call(
        body, name=name, grid=(hp, nq),
        in_specs=[pl.BlockSpec((qb, LANES), lambda p, i: (i, q0 + p)),
                  pl.BlockSpec((s, LANES), lambda p, i: (0, k0 + p)),
                  pl.BlockSpec((s, LANES), lambda p, i: (0, v0 + p)),
                  pl.BlockSpec((qb, LANES), lambda p, i: (i, p)),
                  pl.BlockSpec((qb, LANES), lambda p, i: (i, p)),
                  pl.BlockSpec((qb, qb), lambda p, i: (0, 0))],
        out_specs=[pl.BlockSpec((qb, LANES), lambda p, i: (i, p)),
                   pl.BlockSpec((s, LANES), lambda p, i: (0, p)),
                   pl.BlockSpec((s, LANES), lambda p, i: (0, p))],
        out_shape=[_sds((s, aw), BF16), _sds((s, aw), BF16), _sds((s, aw), BF16)],
        scratch_shapes=[pltpu.VMEM((s, LANES), F32), pltpu.VMEM((s, LANES), F32)],
        compiler_params=_params(("arbitrary", "arbitrary")),
    )(proj, proj, proj, o, do, tri)


def _layer_fwd(x, mod, gains, w, tri, *, tag):
    sh1, sc1, gt1, sh2, sc2, gt2 = mod
    g_pre_mix, g_post_mix, g_pre_mlp, g_post_mlp = gains
    d = x.shape[1]
    h, proj = _norm_mod_matmul(x, g_pre_mix, sc1, sh1, w["w_in"], name=f"in_proj_{tag}")
    yc = _conv_fwd(proj, w["conv_w"], name=f"conv_fwd_{tag}")
    o = _attn_fwd(proj, tri, d=d, name=f"attn_fwd_{tag}")
    ycv, yat, merged, mix, x1 = _mix_out(yc, o, proj, x, w["w_proj_conv"], w["w_proj_attn"], w["w_out"],
                                         g_post_mix, gt1, name=f"mix_out_{tag}")
    h2, a = _norm_mod_matmul(x1, g_pre_mlp, sc2, sh2, w["w_mlp_in"], name=f"mlp_in_{tag}")
    ff, x2 = _mlp_out(a, x1, w["w_mlp_out"], g_post_mlp, gt2, name=f"mlp_out_{tag}")
    saved = dict(x=x, h=h, proj=proj, yc=yc, o=o, ycv=ycv, yat=yat, merged=merged, mix=mix, x1=x1, h2=h2, a=a, ff=ff)
    return x2, saved


def _layer_bwd(dx2, sv, mod, gains, w, tri, *, tag):
    sh1, sc1, gt1, sh2, sc2, gt2 = mod
    g_pre_mix, g_post_mix, g_pre_mlp, g_post_mlp = gains
    d = dx2.shape[1]
    dff, da, dgt2, dg_post_mlp = _mlp_out_bwd(dx2, sv["ff"], sv["a"], w["w_mlp_out"], g_post_mlp, gt2,
                                              name=f"mlp_out_bwd_{tag}")
    gw_mlp_out = _matmul_tn(sv["a"], dff, relu2=True, name=f"gw_mlp_out_{tag}")
    dx1, dsh2, dsc2, dg_pre_mlp = _matmul_nt_norm_bwd(da, w["w_mlp_in"], sv["x1"], dx2, g_pre_mlp, sc2,
                                                      name=f"mlp_in_bwd_{tag}")
    gw_mlp_in = _matmul_tn(sv["h2"], da, name=f"gw_mlp_in_{tag}")
    dmix, dycv, dyat, dyc, do, dga, dgb, dgt1, dg_post_mix = _mix_out_bwd(
        dx1, sv["mix"], sv["proj"], sv["ycv"], sv["yat"], w["w_out"], w["w_proj_conv"], w["w_proj_attn"],
        g_post_mix, gt1, name=f"mix_out_bwd_{tag}")
    gw_out = _matmul_tn(sv["merged"], dmix, name=f"gw_out_{tag}")
    gw_proj_conv = _matmul_tn(sv["yc"], dycv, name=f"gw_proj_conv_{tag}")
    gw_proj_attn = _matmul_tn(sv["o"].astype(BF16), dyat, name=f"gw_proj_attn_{tag}")
    dbg, dcg, du, g_conv_w = _conv_bwd(dyc, sv["proj"], w["conv_w"], name=f"conv_bwd_{tag}")
    dq, dk, dv = _attn_bwd(sv["proj"], sv["o"], do, tri, d=d, name=f"attn_bwd_{tag}")
    dproj = jnp.concatenate([dbg, dcg, du, dq, dk, dv, dga, dgb], axis=1)
    dx0, dsh1, dsc1, dg_pre_mix = _matmul_nt_norm_bwd(dproj, w["w_in"], sv["x"], dx1, g_pre_mix, sc1,
                                                      name=f"in_proj_bwd_{tag}")
    gw_in = _matmul_tn(sv["h"], dproj, name=f"gw_in_{tag}")
    grads = dict(w_in=gw_in, conv_w=g_conv_w, w_proj_conv=gw_proj_conv, w_proj_attn=gw_proj_attn, w_out=gw_out,
                 w_mlp_in=gw_mlp_in, w_mlp_out=gw_mlp_out)
    dmod = jnp.concatenate([dsh1, dsc1, dgt1, dsh2, dsc2, dgt2], axis=0)
    dgains = jnp.concatenate([dg_pre_mix, dg_post_mix, dg_pre_mlp, dg_post_mlp], axis=0)
    return dx0, grads, dmod, dgains


BIG = ("w_in", "w_proj_conv", "w_proj_attn", "w_out", "w_mlp_in", "w_mlp_out")
SHARD_AXIS = dict(w_in=1, w_proj_conv=1, w_proj_attn=1, w_out=0, w_mlp_in=1, w_mlp_out=0)


def _local_step(x, target, mods, gains, wlayers, conv_w, on_grads=None):
    depth = mods.shape[0]
    tri = _tri(min(256, x.shape[0]))
    saved = []
    for l in range(depth):
        wl = dict(wlayers[l])
        wl["conv_w"] = conv_w[l]
        mod = [mods[l, k:k + 1] for k in range(N_MOD)]
        gl = [gains[l, k:k + 1] for k in range(4)]
        x, sv = _layer_fwd(x, mod, gl, wl, tri, tag=str(l))
        saved.append((sv, mod, gl, wl))
    dx, loss = _loss_grad(x, target, name="loss_grad")
    grads, dmods, dgains = [None] * depth, [None] * depth, [None] * depth
    for l in reversed(range(depth)):
        sv, mod, gl, wl = saved[l]
        dx, g, dmods[l], dgains[l] = _layer_bwd(dx, sv, mod, gl, wl, tri, tag=str(l))
        grads[l] = on_grads(l, g) if on_grads is not None else g
    return loss, dx, grads, jnp.stack(dmods), jnp.stack(dgains)


def _coords():
    return lax.axis_index("x"), lax.axis_index("y"), lax.axis_index("c")


def _flip(v, f):
    return 1 - v if f else v


def _all_gather_small(v, *, name):
    r, c_ = v.shape

    def body(v_ref, out_ref, send_sems, recv_sems, local_sem):
        x, y, c = _coords()
        me = 4 * x + 2 * y + c
        mine = pltpu.make_async_copy(v_ref, out_ref.at[me], local_sem)
        mine.start()
        copies = []
        for k in range(1, 8):
            fx, fy, fc = (k >> 2) & 1, (k >> 1) & 1, k & 1
            px, py, pc = _flip(x, fx), _flip(y, fy), _flip(c, fc)
            out = pltpu.make_async_remote_copy(src_ref=v_ref, dst_ref=out_ref.at[me], send_sem=send_sems.at[k - 1],
                                               recv_sem=recv_sems.at[k - 1], device_id=(px, py, pc), device_id_type=MESH)
            out.start()
            back = pltpu.make_async_remote_copy(src_ref=v_ref, dst_ref=out_ref.at[4 * px + 2 * py + pc],
                                                send_sem=send_sems.at[k - 1], recv_sem=recv_sems.at[k - 1],
                                                device_id=(px, py, pc), device_id_type=MESH)
            copies.append((out, back))
        for out, back in copies:
            back.wait_recv()
        for out, back in copies:
            out.wait_send()
        mine.wait()

    return pl.pallas_call(
        body, name=name,
        in_specs=[pl.BlockSpec(memory_space=pltpu.VMEM)],
        out_specs=pl.BlockSpec(memory_space=pltpu.VMEM),
        out_shape=_sds((8, r, c_), F32),
        scratch_shapes=[pltpu.SemaphoreType.DMA((7,)), pltpu.SemaphoreType.DMA((7,)), pltpu.SemaphoreType.DMA],
    )(v)


def _shard_dims(full_shape, axis):
    k, n = full_shape
    return (k // 4, n) if axis == 0 else (k, n // 4)


def _shard_window(ref, axis, chip, half, rows, cols):
    r0, rn = (0, rows) if half is None else (half * (rows // 2), rows // 2)
    if axis == 1:
        return ref.at[pl.ds(r0, rn), pl.ds(chip * cols, cols)]
    return ref.at[pl.ds(chip * rows + r0, rn), :]


def _cast_place(w, layer, axis, chip_arr, *, name, tr=256):
    _, rows, cols = w.shape
    tr = _fit(tr, rows)
    nb = rows // tr
    full = (rows * 4, cols) if axis == 0 else (rows, cols * 4)

    def body(chip_ref, w_ref, o_ref):
        o_ref[...] = w_ref[0].astype(BF16)

    if axis == 1:
        out_map = lambda i, chip: (i, chip[0])
    else:
        out_map = lambda i, chip: (chip[0] * nb + i, 0)
    grid_spec = pltpu.PrefetchScalarGridSpec(
        num_scalar_prefetch=1, grid=(nb,),
        in_specs=[pl.BlockSpec((1, tr, cols), lambda i, chip: (layer, i, 0))],
        out_specs=pl.BlockSpec((tr, cols), out_map))
    return pl.pallas_call(body, name=name, grid_spec=grid_spec, out_shape=_sds(full, BF16),
                          compiler_params=_params(("arbitrary",)))(chip_arr, w)


def _gather_weights(fulls, axes, *, name):
    n = len(fulls)

    def body(*refs):
        outs = refs[n:2 * n]
        send_sems, recv_sems = refs[2 * n:]
        x, y, c = _coords()
        chip = 2 * x + y
        sibling = (x, y, 1 - c)
        sends = []
        landing = []
        for w in range(n):
            rows, cols = _shard_dims(outs[w].shape, axes[w])
            win = functools.partial(_shard_window, outs[w], axes[w], rows=rows, cols=cols)
            for j, (fx, fy) in enumerate(OTHER_CHIPS):
                px, py = _flip(x, fx), _flip(y, fy)
                send = pltpu.make_async_remote_copy(src_ref=win(chip, c), dst_ref=win(chip, c), send_sem=send_sems.at[w, j],
                                                    recv_sem=recv_sems.at[w, j], device_id=(px, py, c), device_id_type=MESH)
                send.start()
                sends.append(send)
                landing.append((w, j, 2 * px + py, win))
        for w, j, pchip, win in landing:
            got = pltpu.make_async_remote_copy(src_ref=win(pchip, c), dst_ref=win(pchip, c), send_sem=send_sems.at[w, 3 + j],
                                               recv_sem=recv_sems.at[w, j], device_id=sibling, device_id_type=MESH)
            got.wait_recv()
            on = pltpu.make_async_remote_copy(src_ref=win(pchip, c), dst_ref=win(pchip, c), send_sem=send_sems.at[w, 3 + j],
                                              recv_sem=recv_sems.at[w, 3 + j], device_id=sibling, device_id_type=MESH)
            on.start()
            sends.append(on)
        for w, j, pchip, win in landing:
            other = pltpu.make_async_remote_copy(src_ref=win(pchip, 1 - c), dst_ref=win(pchip, 1 - c),
                                                 send_sem=send_sems.at[w, 3 + j], recv_sem=recv_sems.at[w, 3 + j],
                                                 device_id=sibling, device_id_type=MESH)
            other.wait_recv()
        for cp in sends:
            cp.wait_send()

    hbm = pl.BlockSpec(memory_space=pltpu.HBM)
    return pl.pallas_call(
        body, name=name,
        in_specs=[hbm] * n, out_specs=[hbm] * n, out_shape=[_sds(f.shape, f.dtype) for f in fulls],
        input_output_aliases={w: w for w in range(n)},
        scratch_shapes=[pltpu.SemaphoreType.DMA((n, 6)), pltpu.SemaphoreType.DMA((n, 6))],
    )(*fulls)


def _rs_exchange(grads, axes, *, name):
    n = len(grads)
    out_shapes = []
    for g, ax in zip(grads, axes):
        rows, cols = _shard_dims(g.shape, ax)
        out_shapes.append(_sds((7, rows // 2, cols), g.dtype))

    def body(*refs):
        ins, outs = refs[:n], refs[n:2 * n]
        send_sems, recv_sems = refs[2 * n:]
        x, y, c = _coords()
        copies = []
        for w in range(n):
            rows, cols = _shard_dims(ins[w].shape, axes[w])
            for k in range(1, 8):
                fx, fy, fc = (k >> 2) & 1, (k >> 1) & 1, k & 1
                px, py, pc = _flip(x, fx), _flip(y, fy), _flip(c, fc)
                piece = _shard_window(ins[w], axes[w], 2 * px + py, pc, rows, cols)
                cp = pltpu.make_async_remote_copy(src_ref=piece, dst_ref=outs[w].at[k - 1], send_sem=send_sems.at[w, k - 1],
                                                  recv_sem=recv_sems.at[w, k - 1], device_id=(px, py, pc), device_id_type=MESH)
                cp.start()
                copies.append(cp)
        for cp in copies:
            cp.wait_recv()
        for cp in copies:
            cp.wait_send()

    hbm = pl.BlockSpec(memory_space=pltpu.HBM)
    return pl.pallas_call(
        body, name=name,
        in_specs=[hbm] * n, out_specs=[hbm] * n, out_shape=out_shapes,
        scratch_shapes=[pltpu.SemaphoreType.DMA((n, 7)), pltpu.SemaphoreType.DMA((n, 7))],
    )(*grads)


def _rs_sum_join(g, got, out_prev, layer, depth, axis, ids, *, name, tr=256):
    _, rows2, cols = got.shape
    tr = _fit(tr, rows2)
    nt = rows2 // tr
    if axis == 1:
        own_map = lambda i, ids_: (ids_[1] * nt + i, ids_[0])
    else:
        own_map = lambda i, ids_: ((ids_[0] * 2 + ids_[1]) * nt + i, 0)

    def body(ids_ref, g_ref, got_ref, *rest):
        out_ref, buf, local_sems, send_sems, recv_sem = rest[-5:]
        i = pl.program_id(0)
        x, y, c = _coords()
        sibling = (x, y, 1 - c)

        def copies(step, slot):
            rows_mine = pl.ds(c * rows2 + step * tr, tr)
            dst = out_ref.at[layer, rows_mine, :]
            keep = pltpu.make_async_copy(buf.at[slot], dst, local_sems.at[slot])
            give = pltpu.make_async_remote_copy(src_ref=buf.at[slot], dst_ref=dst, send_sem=send_sems.at[slot],
                                                recv_sem=recv_sem, device_id=sibling, device_id_type=MESH)
            return keep, give

        def drain(step, slot):
            keep, give = copies(step, slot)
            keep.wait()
            give.wait_send()

        slot = i % 2

        @pl.when(i >= 2)
        def _():
            drain(i - 2, slot)

        acc = g_ref[...].astype(F32)
        for k in range(7):
            acc = acc + got_ref[k].astype(F32)
        buf[slot] = acc
        keep, give = copies(i, slot)
        keep.start()
        give.start()

        @pl.when(i == nt - 1)
        def _():
            if nt >= 2:
                drain(nt - 2, (nt - 2) % 2)
            drain(nt - 1, (nt - 1) % 2)
            theirs = out_ref.at[layer, pl.ds((1 - c) * rows2, rows2), :]
            pltpu.make_async_remote_copy(src_ref=theirs, dst_ref=theirs, send_sem=send_sems.at[0], recv_sem=recv_sem,
                                         device_id=sibling, device_id_type=MESH).wait_recv()

    hbm = pl.BlockSpec(memory_space=pltpu.HBM)
    in_specs = [pl.BlockSpec((tr, cols), own_map), pl.BlockSpec((7, tr, cols), lambda i, ids_: (0, i, 0))]
    operands = [ids, g, got]
    aliases = {}
    if out_prev is not None:
        in_specs.append(hbm)
        operands.append(out_prev)
        aliases = {3: 0}
    grid_spec = pltpu.PrefetchScalarGridSpec(
        num_scalar_prefetch=1, grid=(nt,), in_specs=in_specs, out_specs=hbm,
        scratch_shapes=[pltpu.VMEM((2, tr, cols), F32), pltpu.SemaphoreType.DMA((2,)), pltpu.SemaphoreType.DMA((2,)),
                        pltpu.SemaphoreType.DMA])
    return pl.pallas_call(body, name=name, grid_spec=grid_spec, out_shape=_sds((depth, 2 * rows2, cols), F32),
                          input_output_aliases=aliases, compiler_params=_params(("arbitrary",)))(*operands)


def _reduce_layer(grads, axes, layer, depth, ids, out_prev):
    got = _rs_exchange(grads, axes, name=f"rs_exchange_{layer}")
    outs = []
    for w, (g, ax) in enumerate(zip(grads, axes)):
        prev = None if out_prev is None else out_prev[w]
        outs.append(_rs_sum_join(g, got[w], prev, layer, depth, ax, ids, name=f"rs_sum_join_{layer}_{w}"))
    return outs


def _flat_rows(shape):
    rows = 1
    for s in shape[:-1]:
        rows *= s
    return rows, shape[-1]


def _row_tile(rows, cols, cap_bytes=2 * 1024 * 1024):
    t = rows
    while t * cols * 4 > cap_bytes and t % 16 == 0:
        t //= 2
    return t


def _ada_fwd(c_all, w_ada, b_loc, *, name, tn=512):
    l, d, nl = w_ada.shape
    b = c_all.shape[0]
    tn = min(tn, nl)

    def body(c_ref, w_ref, b_ref, o_ref):
        o_ref[0] = jnp.dot(c_ref[...], w_ref[0], preferred_element_type=F32,
                           precision=lax.Precision.HIGHEST) + b_ref[0]

    return pl.pallas_call(
        body, name=name, grid=(l, nl // tn),
        in_specs=[pl.BlockSpec((b, d), lambda i, j: (0, 0)), pl.BlockSpec((1, d, tn), lambda i, j: (i, 0, j)),
                  pl.BlockSpec((1, 1, tn), lambda i, j: (i, 0, j))],
        out_specs=pl.BlockSpec((1, b, tn), lambda i, j: (i, 0, j)),
        out_shape=_sds((l, b, nl), F32),
        compiler_params=_params(("parallel", "parallel")),
    )(c_all, w_ada, b_loc)


def _ada_bwd(c_t, dmod_loc, *, name, tn=512):
    d, b = c_t.shape
    l, _, nl = dmod_loc.shape
    tn = min(tn, nl)

    def body(c_ref, dm_ref, o_ref):
        cv = c_ref[...]
        dm = dm_ref[0]
        acc = cv[:, 0:1] * dm[0:1, :]
        for k in range(1, b):
            acc = acc + cv[:, k:k + 1] * dm[k:k + 1, :]
        o_ref[0] = acc

    return pl.pallas_call(
        body, name=name, grid=(l, nl // tn),
        in_specs=[pl.BlockSpec((d, b), lambda i, j: (0, 0)), pl.BlockSpec((1, b, tn), lambda i, j: (i, 0, j))],
        out_specs=pl.BlockSpec((1, d, tn), lambda i, j: (i, 0, j)),
        out_shape=_sds((l, d, nl), F32),
        compiler_params=_params(("parallel", "parallel")),
    )(c_t, dmod_loc)


def _sum_devices(p, *, name):
    k, r, c_ = p.shape

    def body(p_ref, o_ref):
        acc = p_ref[0]
        for j in range(1, k):
            acc = acc + p_ref[j]
        o_ref[...] = acc

    return pl.pallas_call(body, name=name, out_shape=_sds((r, c_), F32),
                          in_specs=[pl.BlockSpec(memory_space=pltpu.VMEM)],
                          out_specs=pl.BlockSpec(memory_space=pltpu.VMEM))(p)


def _adamw(w, g, m, v, *, name):
    shape = w.shape
    rows, cols = _flat_rows(shape)
    tr = _row_tile(rows, cols, cap_bytes=1024 * 1024)
    c1 = 1.0 / (1.0 - ADAM_B1 ** ADAM_STEP)
    c2 = 1.0 / (1.0 - ADAM_B2 ** ADAM_STEP)

    def body(w_ref, g_ref, m_ref, v_ref, d_ref, nm_ref, nv_ref):
        gv = g_ref[...]
        nm = ADAM_B1 * m_ref[...] + (1.0 - ADAM_B1) * gv
        nv = ADAM_B2 * v_ref[...] + (1.0 - ADAM_B2) * (gv * gv)
        m_hat = nm * c1
        v_hat = nv * c2
        d_ref[...] = -ADAM_LR * (m_hat / (jnp.sqrt(v_hat) + ADAM_EPS) + ADAM_WD * w_ref[...])
        nm_ref[...] = nm
        nv_ref[...] = nv

    spec = pl.BlockSpec((tr, cols), lambda i: (i, 0))
    flat = lambda a: a.reshape(rows, cols)
    outs = pl.pallas_call(body, name=name, grid=(rows // tr,), in_specs=[spec] * 4, out_specs=[spec] * 3,
                          out_shape=[_sds((rows, cols), F32)] * 3, compiler_params=_params(("parallel",)),
                          )(flat(w), flat(g), flat(m), flat(v))
    return tuple(o.reshape(shape) for o in outs)


WEIGHTS = ("w_ada", "b_ada", "g_pre_mix", "g_post_mix", "g_pre_mlp", "g_post_mlp", "w_in", "conv_w",
           "w_proj_conv", "w_proj_attn", "w_out", "w_mlp_in", "w_mlp_out")
GAINS = ("g_pre_mix", "g_post_mix", "g_pre_mlp", "g_post_mlp")


def kernel(x, c, w_ada, b_ada, g_pre_mix, g_post_mix, g_pre_mlp, g_post_mlp, w_in, conv_w, w_proj_conv, w_proj_attn, w_out, w_mlp_in, w_mlp_out, loss_target, m_w_ada, m_b_ada, m_g_pre_mix, m_g_post_mix, m_g_pre_mlp, m_g_post_mlp, m_w_in, m_conv_w, m_w_proj_conv, m_w_proj_attn, m_w_out, m_w_mlp_in, m_w_mlp_out, v_w_ada, v_b_ada, v_g_pre_mix, v_g_post_mix, v_g_pre_mlp, v_g_post_mlp, v_w_in, v_conv_w, v_w_proj_conv, v_w_proj_attn, v_w_out, v_w_mlp_in, v_w_mlp_out):
    params = dict(w_ada=w_ada, b_ada=b_ada, g_pre_mix=g_pre_mix, g_post_mix=g_post_mix, g_pre_mlp=g_pre_mlp,
                  g_post_mlp=g_post_mlp, w_in=w_in, conv_w=conv_w, w_proj_conv=w_proj_conv, w_proj_attn=w_proj_attn,
                  w_out=w_out, w_mlp_in=w_mlp_in, w_mlp_out=w_mlp_out)
    m_in = dict(w_ada=m_w_ada, b_ada=m_b_ada, g_pre_mix=m_g_pre_mix, g_post_mix=m_g_post_mix, g_pre_mlp=m_g_pre_mlp,
                g_post_mlp=m_g_post_mlp, w_in=m_w_in, conv_w=m_conv_w, w_proj_conv=m_w_proj_conv,
                w_proj_attn=m_w_proj_attn, w_out=m_w_out, w_mlp_in=m_w_mlp_in, w_mlp_out=m_w_mlp_out)
    v_in = dict(w_ada=v_w_ada, b_ada=v_b_ada, g_pre_mix=v_g_pre_mix, g_post_mix=v_g_post_mix, g_pre_mlp=v_g_pre_mlp,
                g_post_mlp=v_g_post_mlp, w_in=v_w_in, conv_w=v_conv_w, w_proj_conv=v_w_proj_conv,
                w_proj_attn=v_w_proj_attn, w_out=v_w_out, w_mlp_in=v_w_mlp_in, w_mlp_out=v_w_mlp_out)

    depth, d, nl_ada = w_ada.shape
    ix, iy, ic = lax.axis_index("x"), lax.axis_index("y"), lax.axis_index("c")
    chip = 2 * ix + iy
    me = 4 * ix + 2 * iy + ic
    xs = x[0]
    target = loss_target[0]

    c_all = _all_gather_small(jnp.broadcast_to(c, (8, d)), name="gather_c")[:, 0, :]
    b_loc = lax.dynamic_slice_in_dim(b_ada, chip * nl_ada, nl_ada, axis=1)[:, None, :]
    mod_loc = _ada_fwd(c_all, w_ada, b_loc, name="ada_fwd")
    mod_all = _all_gather_small(mod_loc.reshape(depth * 8, nl_ada), name="gather_mod")
    mod_all = mod_all.reshape(4, 2, depth, 8, nl_ada)[:, 0]
    mod_me = lax.dynamic_index_in_dim(mod_all, me, axis=2, keepdims=False)
    mods = jnp.transpose(mod_me, (1, 0, 2)).reshape(depth, N_MOD, d)

    chip_arr = jnp.reshape(chip, (1,)).astype(jnp.int32)
    ids = jnp.stack([chip, ic]).astype(jnp.int32)
    axes = [SHARD_AXIS[k] for k in BIG]
    placed = [_cast_place(params[k], l, SHARD_AXIS[k], chip_arr, name=f"place_{k}_{l}")
              for l in range(depth) for k in BIG]
    full = _gather_weights(placed, axes * depth, name="gather_weights")
    wlayers = [dict(zip(BIG, full[l * len(BIG):(l + 1) * len(BIG)])) for l in range(depth)]
    conv_full = _all_gather_small(
        jnp.pad(conv_w.reshape(depth * 3, -1), ((0, 8 - depth * 3), (0, 0))), name="gather_conv_w")
    conv_full = conv_full.reshape(4, 2, 8, -1)[:, 0, :depth * 3]
    conv_full = jnp.transpose(conv_full, (1, 0, 2)).reshape(depth, 3, -1)

    gains = jnp.stack([params[k] for k in GAINS], axis=1)
    reduced = [None]

    def reduce_now(l, g):
        reduced[0] = _reduce_layer([g[k] for k in BIG], axes, l, depth, ids, reduced[0])
        return g["conv_w"]

    loss, dx, conv_grads, dmods, dgains = _local_step(xs, target, mods, gains, wlayers, conv_full, reduce_now)

    cw = conv_full.shape[2]
    rows = [dmods.reshape(depth * N_MOD, d), dgains.reshape(depth * 4, d),
            jnp.stack(conv_grads).reshape(-1, d), jnp.broadcast_to(loss, (1, d))]
    payload = jnp.concatenate(rows, axis=0)
    n_rows = payload.shape[0]
    pad = (-n_rows) % 8
    payload = jnp.pad(payload, ((0, pad), (0, 0)))
    everyone = _all_gather_small(payload, name="gather_small_grads")
    total = _sum_devices(everyone, name="sum_small_grads")
    r0 = depth * N_MOD
    grads = {}
    grads["b_ada"] = total[:r0].reshape(depth, N_MOD * d)
    gsum = total[r0:r0 + depth * 4].reshape(depth, 4, d)
    for k, name in enumerate(GAINS):
        grads[name] = gsum[:, k]
    r1 = r0 + depth * 4
    n_conv = (depth * 3 * cw) // d
    conv_g = total[r1:r1 + n_conv].reshape(depth, 3, cw)
    grads["conv_w"] = lax.dynamic_slice_in_dim(conv_g, chip * (cw // 4), cw // 4, axis=2)
    loss_out = total[r1 + n_conv, 0]
    dmod_all = everyone[:, :r0].reshape(8, depth, N_MOD * d)
    dmod_loc = lax.dynamic_slice_in_dim(dmod_all, chip * nl_ada, nl_ada, axis=2)
    grads["w_ada"] = _ada_bwd(c_all.T, jnp.transpose(dmod_loc, (1, 0, 2)), name="ada_bwd")

    for k, g in zip(BIG, reduced[0]):
        grads[k] = g

    deltas, new_m, new_v = {}, {}, {}
    for k in WEIGHTS:
        deltas[k], new_m[k], new_v[k] = _adamw(params[k], grads[k], m_in[k], v_in[k], name=f"adamw_{k}")

    return (loss_out, dx[None], *[grads[k] for k in WEIGHTS], *[deltas[k] for k in WEIGHTS],
            *[new_m[k] for k in WEIGHTS], *[new_v[k] for k in WEIGHTS])
```

```python
import functools

import jax
import jax.numpy as jnp
from jax import lax
from jax.experimental import pallas as pl
from jax.experimental.pallas import tpu as pltpu

F32 = jnp.float32
BF16 = jnp.bfloat16
EPS = 1e-6
N_MOD = 6
HEAD_DIM = 64
LANES = 128
ATTN_SCALE = 1.0 / 8.0
UNDERFLOW_LOG = -90.0
ADAM_LR = 0.001
ADAM_B1 = 0.9
ADAM_B2 = 0.999
ADAM_EPS = 1e-08
ADAM_WD = 0.01
ADAM_STEP = 10
VMEM_LIMIT = 56 * 1024 * 1024
MESH = pl.DeviceIdType.MESH
OTHER_CHIPS = ((1, 0), (0, 1), (1, 1))

_NT = (((1,), (1,)), ((), ()))
_TN = (((0,), (0,)), ((), ()))


def _sds(shape, dtype):
    return jax.ShapeDtypeStruct(shape, dtype)


def _params(sem):
    return pltpu.CompilerParams(dimension_semantics=sem, vmem_limit_bytes=VMEM_LIMIT)


def _fit(t, n):
    t = min(t, n)
    while n % t:
        t //= 2
    return t


def _vec_spec(d, nargs):
    if nargs == 1:
        return pl.BlockSpec((1, d), lambda i: (0, 0))
    return pl.BlockSpec((1, d), lambda i, j: (0, 0))


def _log_one_minus_sigmoid(z):
    return -jnp.log(1.0 + jnp.exp(-jnp.abs(z))) - jnp.maximum(z, 0.0)


def _sigmoid(z):
    t = jnp.exp(-jnp.abs(z))
    return jnp.where(z >= 0.0, 1.0, t) / (1.0 + t)


def _split_bf16(a):
    hi = a.astype(BF16)
    lo = (a - hi.astype(F32)).astype(BF16)
    return hi, lo


def _rms_bwd(dn, xin, g):
    r = lax.rsqrt(jnp.mean(xin * xin, axis=-1, keepdims=True) + EPS)
    xh = xin * r
    dxh = dn * g
    dxin = r * (dxh - xh * jnp.mean(dxh * xh, axis=-1, keepdims=True))
    return dxin, xh


def _colsum(a):
    return jnp.sum(a, axis=0, keepdims=True)


def _norm_mod_matmul(x, g, sc, sh, w, *, name, tm=1024, tn=512):
    s, d = x.shape
    n = w.shape[1]
    tm, tn = _fit(tm, s), _fit(tn, n)

    def body(x_ref, g_ref, sc_ref, sh_ref, w_ref, h_ref, o_ref):
        @pl.when(pl.program_id(1) == 0)
        def _():
            xv = x_ref[...]
            r = lax.rsqrt(jnp.mean(xv * xv, axis=-1, keepdims=True) + EPS)
            h_ref[...] = ((xv * r * g_ref[...]) * (1.0 + sc_ref[...]) + sh_ref[...]).astype(BF16)
        o_ref[...] = jnp.dot(h_ref[...], w_ref[...], preferred_element_type=F32).astype(BF16)

    return pl.pallas_call(
        body, name=name, grid=(s // tm, n // tn),
        in_specs=[pl.BlockSpec((tm, d), lambda i, j: (i, 0)), _vec_spec(d, 2), _vec_spec(d, 2), _vec_spec(d, 2),
                  pl.BlockSpec((d, tn), lambda i, j: (0, j))],
        out_specs=[pl.BlockSpec((tm, d), lambda i, j: (i, 0)), pl.BlockSpec((tm, tn), lambda i, j: (i, j))],
        out_shape=[_sds((s, d), BF16), _sds((s, n), BF16)],
        compiler_params=_params(("parallel", "arbitrary")),
    )(x, g, sc, sh, w)


HALO = 16


def _conv_fwd(proj, conv_w, *, name, tm=512):
    s = proj.shape[0]
    cw = conv_w.shape[1]
    tm = min(tm, s)
    nb = tm // HALO

    def body(bg_ref, cg_ref, u_ref, cgh_ref, uh_ref, w_ref, yc_ref, vbuf):
        i = pl.program_id(0)
        vv = cg_ref[...].astype(F32) * u_ref[...].astype(F32)
        halo = cgh_ref[...].astype(F32) * uh_ref[...].astype(F32)
        vbuf[0:HALO, :] = jnp.where(i > 0, halo, 0.0)
        vbuf[HALO:HALO + tm, :] = vv
        v1 = vbuf[HALO - 1:HALO - 1 + tm, :]
        v2 = vbuf[HALO - 2:HALO - 2 + tm, :]
        w = w_ref[...]
        y = w[2:3, :] * vv + w[1:2, :] * v1 + w[0:1, :] * v2
        yc_ref[...] = (bg_ref[...].astype(F32) * y).astype(BF16)

    def prev(i):
        return jnp.maximum(i * nb - 1, 0)

    return pl.pallas_call(
        body, name=name, grid=(s // tm,),
        in_specs=[pl.BlockSpec((tm, cw), lambda i: (i, 0)), pl.BlockSpec((tm, cw), lambda i: (i, 1)),
                  pl.BlockSpec((tm, cw), lambda i: (i, 2)),
                  pl.BlockSpec((HALO, cw), lambda i: (prev(i), 1)), pl.BlockSpec((HALO, cw), lambda i: (prev(i), 2)),
                  pl.BlockSpec((3, cw), lambda i: (0, 0))],
        out_specs=pl.BlockSpec((tm, cw), lambda i: (i, 0)),
        out_shape=_sds((s, cw), BF16),
        scratch_shapes=[pltpu.VMEM((HALO + tm, cw), F32)],
        compiler_params=_params(("arbitrary",)),
    )(proj, proj, proj, proj, proj, conv_w)


def _tri(qb):
    r = lax.broadcasted_iota(jnp.int32, (qb, qb), 0)
    c = lax.broadcasted_iota(jnp.int32, (qb, qb), 1)
    return (r >= c).astype(BF16)


def _head_mask(h):
    lane = lax.broadcasted_iota(jnp.int32, (1, LANES), 1)
    return (lane >= HEAD_DIM * h) & (lane < HEAD_DIM * (h + 1))


def _stack_heads(a, masks):
    return jnp.concatenate([jnp.where(m, a, 0).astype(BF16) for m in masks], axis=0)


def _heads_to_lanes(a, qb):
    return jnp.concatenate([a[:qb], a[qb:]], axis=1)


def _stacked_causal(qb):
    row = lax.broadcasted_iota(jnp.int32, (2 * qb, qb), 0)
    col = lax.broadcasted_iota(jnp.int32, (2 * qb, qb), 1)
    return col < jnp.where(row >= qb, row - qb, row)


def _running_sum(a, tri_m):
    n = a.shape[0]
    hi, lo = _split_bf16(a)
    both = jnp.dot(jnp.concatenate([hi, lo], axis=0), tri_m, preferred_element_type=F32)
    return both[:n] + both[n:]


def _attn_cols(d):
    cw = d // 2
    hp = (d // 2) // LANES
    q0 = (3 * cw) // LANES
    return q0, q0 + hp, q0 + 2 * hp, hp


class _Side:
    def __init__(self, operands, out_shapes, aliases, scratch, start, mid, finish):
        self.operands, self.out_shapes, self.aliases, self.scratch = list(operands), list(out_shapes), aliases, list(scratch)
        self.start, self.mid, self.finish = start, mid, finish


def _side_call(side, *, name):
    n_in, n_out = len(side.operands), len(side.out_shapes)

    def body(*refs):
        parts = refs[:n_in], refs[n_in:n_in + n_out], refs[n_in + n_out:]
        side.start(*parts)
        if side.mid is not None:
            side.mid(*parts)
        side.finish(*parts)

    hbm = pl.BlockSpec(memory_space=pltpu.HBM)
    return pl.pallas_call(body, name=name, in_specs=[hbm] * n_in, out_specs=[hbm] * n_out, out_shape=side.out_shapes,
                          input_output_aliases=dict(side.aliases), scratch_shapes=side.scratch)(*side.operands)


def _host_side(side, n_in, n_out, n_scratch, refs, first, late, last):
    if side is None:
        return refs, lambda: None
    s_in, s_out = len(side.operands), len(side.out_shapes)
    ins = refs[:n_in]
    side_in = refs[n_in:n_in + s_in]
    outs = refs[n_in + s_in:n_in + s_in + n_out]
    side_out = refs[n_in + s_in + n_out:n_in + s_in + n_out + s_out]
    rest = refs[n_in + s_in + n_out + s_out:]
    scratch, sems = rest[:n_scratch], rest[n_scratch:]
    parts = (side_in, side_out, sems)
    pl.when(first)(lambda: side.start(*parts))

    def run_late_phases():
        if side.mid is not None:
            pl.when(late)(lambda: side.mid(*parts))
        pl.when(last)(lambda: side.finish(*parts))

    return (*ins, *outs, *scratch), run_late_phases


def _side_specs(side, n_in, n_out):
    if side is None:
        return [], [], [], {}, []
    hbm = pl.BlockSpec(memory_space=pltpu.HBM)
    s_in = len(side.operands)
    aliases = {n_in + a: n_out + b for a, b in side.aliases.items()}
    return [hbm] * s_in, [hbm] * len(side.out_shapes), side.out_shapes, aliases, side.scratch


def _attn_fwd(proj, tri, *, d, name, qb=256, side=None):
    s = proj.shape[0]
    qb = min(qb, s)
    nq = s // qb
    q0, k0, v0, hp = _attn_cols(d)

    def body(*refs):
        p, i = pl.program_id(0), pl.program_id(1)
        (q_ref, k_ref, v_ref, tri_ref, o_ref), late_phases = _host_side(
            side, 4, 1, 0, refs, (p == 0) & (i == 0), (p == hp - 1) & (i == 0), (p == hp - 1) & (i == nq - 1))
        causal = _stacked_causal(qb)
        tri_m = tri_ref[...]
        masks = [_head_mask(h) for h in range(2)]
        qs = _stack_heads(q_ref[...] * ATTN_SCALE, masks)

        def block(j, state, diag):
            run, acc = state
            rows = pl.ds(pl.multiple_of(j * qb, qb), qb)
            z = lax.dot_general(qs, k_ref[rows, :], _NT, preferred_element_type=F32)
            lg = _log_one_minus_sigmoid(z)
            if diag:
                lg = jnp.where(causal, lg, 0.0)
            cs = _running_sum(lg, tri_m)
            a = jnp.exp(z + cs + run)
            if diag:
                a = jnp.where(causal, a, 0.0)
            ab = a.astype(BF16)
            acc = acc + jnp.dot(_heads_to_lanes(ab, qb), _stack_heads(v_ref[rows, :], masks),
                                preferred_element_type=F32)
            return run + cs[:, 0:1], acc

        state = block(i, (jnp.zeros((2 * qb, 1), F32), jnp.zeros((qb, LANES), F32)), True)
        state = lax.while_loop(
            lambda st: (st[0] >= 0) & (jnp.max(st[1]) > UNDERFLOW_LOG),
            lambda st: (st[0] - 1, *block(st[0], st[1:], False)),
            (i - 1, *state))
        o_ref[...] = state[2]
        late_phases()

    s_in, s_out, s_shapes, aliases, s_scratch = _side_specs(side, 4, 1)
    res = pl.pallas_call(
        body, name=name, grid=(hp, nq),
        in_specs=[pl.BlockSpec((qb, LANES), lambda p, i: (i, q0 + p)),
                  pl.BlockSpec((s, LANES), lambda p, i: (0, k0 + p)),
                  pl.BlockSpec((s, LANES), lambda p, i: (0, v0 + p)),
                  pl.BlockSpec((qb, qb), lambda p, i: (0, 0))] + s_in,
        out_specs=[pl.BlockSpec((qb, LANES), lambda p, i: (i, p))] + s_out,
        out_shape=[_sds((s, hp * LANES), F32)] + s_shapes,
        input_output_aliases=aliases, scratch_shapes=s_scratch,
        compiler_params=_params(("arbitrary", "arbitrary")),
    )(proj, proj, proj, tri, *([] if side is None else side.operands))
    return res[0], res[1:]


def _mix_out(yc, o, proj, x, wpc, wpa, wout, g, gt, *, name, tm=256):
    s, d = x.shape
    cw = yc.shape[1]
    tm = min(tm, s)
    ga_blk = (3 * cw + 3 * (d // 2)) // d

    def body(yc_ref, o_ref, ga_ref, gb_ref, x_ref, wpc_ref, wpa_ref, wout_ref, g_ref, gt_ref,
             ycv_ref, yat_ref, mg_ref, mix_ref, x1_ref):
        y_conv = jnp.dot(yc_ref[...], wpc_ref[...], preferred_element_type=F32)
        y_attn = jnp.dot(o_ref[...].astype(BF16), wpa_ref[...], preferred_element_type=F32)
        merged = (_sigmoid(ga_ref[...].astype(F32)) * y_conv + _sigmoid(gb_ref[...].astype(F32)) * y_attn)
        mg = merged.astype(BF16)
        mix = jnp.dot(mg, wout_ref[...], preferred_element_type=F32)
        r = lax.rsqrt(jnp.mean(mix * mix, axis=-1, keepdims=True) + EPS)
        ycv_ref[...] = y_conv.astype(BF16)
        yat_ref[...] = y_attn.astype(BF16)
        mg_ref[...] = mg
        mix_ref[...] = mix
        x1_ref[...] = x_ref[...] + gt_ref[...] * (mix * r * g_ref[...])

    def rows(w):
        return pl.BlockSpec((tm, w), lambda i: (i, 0))

    def full(a):
        return pl.BlockSpec(a.shape, lambda i: (0, 0))

    return pl.pallas_call(
        body, name=name, grid=(s // tm,),
        in_specs=[rows(cw), rows(d // 2), pl.BlockSpec((tm, d), lambda i: (i, ga_blk)),
                  pl.BlockSpec((tm, d), lambda i: (i, ga_blk + 1)), rows(d),
                  full(wpc), full(wpa), full(wout), _vec_spec(d, 1), _vec_spec(d, 1)],
        out_specs=[rows(d), rows(d), rows(d), rows(d), rows(d)],
        out_shape=[_sds((s, d), BF16), _sds((s, d), BF16), _sds((s, d), BF16), _sds((s, d), F32), _sds((s, d), F32)],
        compiler_params=_params(("parallel",)),
    )(yc, o, proj, proj, x, wpc, wpa, wout, g, gt)


def _relu2(a):
    r = jnp.maximum(a.astype(F32), 0.0)
    return (r * r).astype(BF16)


def _mlp_out(a, x, w2, g, gt, *, name, tm=512):
    s, d = x.shape
    dff = a.shape[1]
    tm = min(tm, s)

    def body(a_ref, x_ref, w_ref, g_ref, gt_ref, ff_ref, x2_ref):
        ff = jnp.dot(_relu2(a_ref[...]), w_ref[...], preferred_element_type=F32)
        r = lax.rsqrt(jnp.mean(ff * ff, axis=-1, keepdims=True) + EPS)
        ff_ref[...] = ff
        x2_ref[...] = x_ref[...] + gt_ref[...] * (ff * r * g_ref[...])

    return pl.pallas_call(
        body, name=name, grid=(s // tm,),
        in_specs=[pl.BlockSpec((tm, dff), lambda i: (i, 0)), pl.BlockSpec((tm, d), lambda i: (i, 0)),
                  pl.BlockSpec((dff, d), lambda i: (0, 0)), _vec_spec(d, 1), _vec_spec(d, 1)],
        out_specs=[pl.BlockSpec((tm, d), lambda i: (i, 0)), pl.BlockSpec((tm, d), lambda i: (i, 0))],
        out_shape=[_sds((s, d), F32), _sds((s, d), F32)],
        compiler_params=_params(("parallel",)),
    )(a, x, w2, g, gt)


def _loss_grad(y, target, *, name, tm=512):
    s, d = y.shape
    tm = min(tm, s)

    def body(y_ref, t_ref, dy_ref, loss_ref):
        @pl.when(pl.program_id(0) == 0)
        def _():
            loss_ref[...] = jnp.zeros_like(loss_ref)
        e = y_ref[...] - t_ref[...]
        dy_ref[...] = e * (1.0 / d)
        loss_ref[...] += 0.5 * jnp.sum(jnp.mean(e * e, axis=-1, keepdims=True), axis=0, keepdims=True)

    return pl.pallas_call(
        body, name=name, grid=(s // tm,),
        in_specs=[pl.BlockSpec((tm, d), lambda i: (i, 0)), pl.BlockSpec((tm, d), lambda i: (i, 0))],
        out_specs=[pl.BlockSpec((tm, d), lambda i: (i, 0)), pl.BlockSpec((1, 1), lambda i: (0, 0))],
        out_shape=[_sds((s, d), F32), _sds((1, 1), F32)],
        compiler_params=_params(("arbitrary",)),
    )(y, target)


def _mlp_out_bwd(dx, ff, a, w2, g, gt, *, name, tm=256):
    s, d = dx.shape
    dff = a.shape[1]
    tm = min(tm, s)

    def body(dx_ref, ff_ref, a_ref, w_ref, g_ref, gt_ref, dff_ref, da_ref, dgt_ref, dg_ref):
        @pl.when(pl.program_id(0) == 0)
        def _():
            dgt_ref[...] = jnp.zeros_like(dgt_ref)
            dg_ref[...] = jnp.zeros_like(dg_ref)
        dxv = dx_ref[...]
        dn = dxv * gt_ref[...]
        dffv, xh = _rms_bwd(dn, ff_ref[...], g_ref[...])
        dgt_ref[...] += _colsum(dxv * (xh * g_ref[...]))
        dg_ref[...] += _colsum(dn * xh)
        dffb = dffv.astype(BF16)
        dff_ref[...] = dffb
        drr = lax.dot_general(dffb, w_ref[...], _NT, preferred_element_type=F32)
        da_ref[...] = (drr * (2.0 * jnp.maximum(a_ref[...].astype(F32), 0.0))).astype(BF16)

    return pl.pallas_call(
        body, name=name, grid=(s // tm,),
        in_specs=[pl.BlockSpec((tm, d), lambda i: (i, 0)), pl.BlockSpec((tm, d), lambda i: (i, 0)),
                  pl.BlockSpec((tm, dff), lambda i: (i, 0)), pl.BlockSpec((dff, d), lambda i: (0, 0)),
                  _vec_spec(d, 1), _vec_spec(d, 1)],
        out_specs=[pl.BlockSpec((tm, d), lambda i: (i, 0)), pl.BlockSpec((tm, dff), lambda i: (i, 0)),
                   _vec_spec(d, 1), _vec_spec(d, 1)],
        out_shape=[_sds((s, d), BF16), _sds((s, dff), BF16), _sds((1, d), F32), _sds((1, d), F32)],
        compiler_params=_params(("arbitrary",)),
    )(dx, ff, a, w2, g, gt)


def _matmul_nt_norm_bwd(dy, w, x, dres, g, sc, *, name, tm=512, tn=512):
    s, n = dy.shape
    d = w.shape[0]
    tm, tn = _fit(tm, s), _fit(tn, n)
    nj = n // tn

    def body(dy_ref, w_ref, x_ref, dres_ref, g_ref, sc_ref, dx_ref, dsh_ref, dsc_ref, dg_ref, acc):
        i, j = pl.program_id(0), pl.program_id(1)

        @pl.when((i == 0) & (j == 0))
        def _():
            dsh_ref[...] = jnp.zeros_like(dsh_ref)
            dsc_ref[...] = jnp.zeros_like(dsc_ref)
            dg_ref[...] = jnp.zeros_like(dg_ref)

        @pl.when(j == 0)
        def _():
            acc[...] = jnp.zeros_like(acc)

        acc[...] += lax.dot_general(dy_ref[...], w_ref[...], _NT, preferred_element_type=F32)

        @pl.when(j == nj - 1)
        def _():
            dh = acc[...]
            dn = dh * (1.0 + sc_ref[...])
            dxin, xh = _rms_bwd(dn, x_ref[...], g_ref[...])
            dsh_ref[...] += _colsum(dh)
            dsc_ref[...] += _colsum(dh * (xh * g_ref[...]))
            dg_ref[...] += _colsum(dn * xh)
            dx_ref[...] = dres_ref[...] + dxin

    return pl.pallas_call(
        body, name=name, grid=(s // tm, nj),
        in_specs=[pl.BlockSpec((tm, tn), lambda i, j: (i, j)), pl.BlockSpec((d, tn), lambda i, j: (0, j)),
                  pl.BlockSpec((tm, d), lambda i, j: (i, 0)), pl.BlockSpec((tm, d), lambda i, j: (i, 0)),
                  _vec_spec(d, 2), _vec_spec(d, 2)],
        out_specs=[pl.BlockSpec((tm, d), lambda i, j: (i, 0)), _vec_spec(d, 2), _vec_spec(d, 2), _vec_spec(d, 2)],
        out_shape=[_sds((s, d), F32), _sds((1, d), F32), _sds((1, d), F32), _sds((1, d), F32)],
        scratch_shapes=[pltpu.VMEM((tm, d), F32)],
        compiler_params=_params(("arbitrary", "arbitrary")),
    )(dy, w, x, dres, g, sc)


def _matmul_tn(a, b, *, name, tk=1024, tn=1024, ts=512, relu2=False):
    s, k = a.shape
    n = b.shape[1]
    tk, tn, ts = _fit(tk, k), _fit(tn, n), _fit(ts, s)
    nt = s // ts

    def body(a_ref, b_ref, o_ref, acc):
        t = pl.program_id(2)

        @pl.when(t == 0)
        def _():
            acc[...] = jnp.zeros_like(acc)
        av = a_ref[...]
        if relu2:
            av = _relu2(av)
        acc[...] += lax.dot_general(av, b_ref[...], _TN, preferred_element_type=F32)

        @pl.when(t == nt - 1)
        def _():
            o_ref[...] = acc[...].astype(BF16)

    return pl.pallas_call(
        body, name=name, grid=(k // tk, n // tn, nt),
        in_specs=[pl.BlockSpec((ts, tk), lambda i, j, t: (t, i)), pl.BlockSpec((ts, tn), lambda i, j, t: (t, j))],
        out_specs=pl.BlockSpec((tk, tn), lambda i, j, t: (i, j)),
        out_shape=_sds((k, n), BF16),
        scratch_shapes=[pltpu.VMEM((tk, tn), F32)],
        compiler_params=_params(("parallel", "parallel", "arbitrary")),
    )(a, b)


def _mix_out_bwd(dx, mix, proj, ycv, yat, wout, wpc, wpa, g, gt, *, name, tm=256):
    s, d = dx.shape
    cw = wpc.shape[0]
    aw = wpa.shape[0]
    tm = min(tm, s)
    ga_blk = (3 * cw + 3 * aw) // d

    def body(dx_ref, mix_ref, ga_ref, gb_ref, ycv_ref, yat_ref, wout_ref, wpc_ref, wpa_ref, g_ref, gt_ref,
             dmix_ref, dycv_ref, dyat_ref, dyc_ref, do_ref, dga_ref, dgb_ref, dgt_ref, dg_ref):
        @pl.when(pl.program_id(0) == 0)
        def _():
            dgt_ref[...] = jnp.zeros_like(dgt_ref)
            dg_ref[...] = jnp.zeros_like(dg_ref)
        dxv = dx_ref[...]
        dn = dxv * gt_ref[...]
        dmix, xh = _rms_bwd(dn, mix_ref[...], g_ref[...])
        dgt_ref[...] += _colsum(dxv * (xh * g_ref[...]))
        dg_ref[...] += _colsum(dn * xh)
        dmixb = dmix.astype(BF16)
        dmix_ref[...] = dmixb
        dmerged = lax.dot_general(dmixb, wout_ref[...], _NT, preferred_element_type=F32)
        sga = _sigmoid(ga_ref[...].astype(F32))
        sgb = _sigmoid(gb_ref[...].astype(F32))
        dycv = (dmerged * sga).astype(BF16)
        dyat = (dmerged * sgb).astype(BF16)
        dycv_ref[...] = dycv
        dyat_ref[...] = dyat
        dga_ref[...] = (dmerged * ycv_ref[...].astype(F32) * (sga * (1.0 - sga))).astype(BF16)
        dgb_ref[...] = (dmerged * yat_ref[...].astype(F32) * (sgb * (1.0 - sgb))).astype(BF16)
        dyc_ref[...] = lax.dot_general(dycv, wpc_ref[...], _NT, preferred_element_type=F32).astype(BF16)
        do_ref[...] = lax.dot_general(dyat, wpa_ref[...], _NT, preferred_element_type=F32).astype(BF16)

    def rows(w):
        return pl.BlockSpec((tm, w), lambda i: (i, 0))

    def full(a):
        return pl.BlockSpec(a.shape, lambda i: (0, 0))

    return pl.pallas_call(
        body, name=name, grid=(s // tm,),
        in_specs=[rows(d), rows(d), pl.BlockSpec((tm, d), lambda i: (i, ga_blk)),
                  pl.BlockSpec((tm, d), lambda i: (i, ga_blk + 1)), rows(d), rows(d),
                  full(wout), full(wpc), full(wpa), _vec_spec(d, 1), _vec_spec(d, 1)],
        out_specs=[rows(d), rows(d), rows(d), rows(cw), rows(aw), rows(d), rows(d), _vec_spec(d, 1), _vec_spec(d, 1)],
        out_shape=[_sds((s, d), BF16), _sds((s, d), BF16), _sds((s, d), BF16), _sds((s, cw), BF16),
                   _sds((s, aw), BF16), _sds((s, d), BF16), _sds((s, d), BF16), _sds((1, d), F32), _sds((1, d), F32)],
        compiler_params=_params(("arbitrary",)),
    )(dx, mix, proj, proj, ycv, yat, wout, wpc, wpa, g, gt)


def _conv_bwd(dyc, proj, conv_w, *, name, tm=512):
    s = proj.shape[0]
    cw = conv_w.shape[1]
    tm = min(tm, s)
    nb = tm // HALO
    nt = s // tm
    last_blk = s // HALO - 1

    def body(dyc_ref, bg_ref, cg_ref, u_ref, cgh_ref, uh_ref, dych_ref, bgh_ref, w_ref,
             dbg_ref, dcg_ref, du_ref, dw_ref, vbuf, gbuf):
        i = pl.program_id(0)

        @pl.when(i == 0)
        def _():
            dw_ref[...] = jnp.zeros_like(dw_ref)

        cg = cg_ref[...].astype(F32)
        u = u_ref[...].astype(F32)
        vv = cg * u
        halo = cgh_ref[...].astype(F32) * uh_ref[...].astype(F32)
        vbuf[0:HALO, :] = jnp.where(i > 0, halo, 0.0)
        vbuf[HALO:HALO + tm, :] = vv
        v1 = vbuf[HALO - 1:HALO - 1 + tm, :]
        v2 = vbuf[HALO - 2:HALO - 2 + tm, :]
        w = w_ref[...]
        y = w[2:3, :] * vv + w[1:2, :] * v1 + w[0:1, :] * v2
        dyc = dyc_ref[...].astype(F32)
        dbg_ref[...] = (dyc * y).astype(BF16)
        gy = dyc * bg_ref[...].astype(F32)
        nxt = dych_ref[...].astype(F32) * bgh_ref[...].astype(F32)
        gbuf[0:tm, :] = gy
        gbuf[tm:tm + HALO, :] = jnp.where(i < nt - 1, nxt, 0.0)
        g1 = gbuf[1:1 + tm, :]
        g2 = gbuf[2:2 + tm, :]
        dvv = w[2:3, :] * gy + w[1:2, :] * g1 + w[0:1, :] * g2
        dcg_ref[...] = (dvv * u).astype(BF16)
        du_ref[...] = (dvv * cg).astype(BF16)
        dw_ref[0:1, :] += _colsum(gy * v2)
        dw_ref[1:2, :] += _colsum(gy * v1)
        dw_ref[2:3, :] += _colsum(gy * vv)

    def prev(i):
        return jnp.maximum(i * nb - 1, 0)

    def nxt_blk(i):
        return jnp.minimum((i + 1) * nb, last_blk)

    def col(c):
        return pl.BlockSpec((tm, cw), lambda i: (i, c))

    return pl.pallas_call(
        body, name=name, grid=(nt,),
        in_specs=[col(0), col(0), col(1), col(2),
                  pl.BlockSpec((HALO, cw), lambda i: (prev(i), 1)), pl.BlockSpec((HALO, cw), lambda i: (prev(i), 2)),
                  pl.BlockSpec((HALO, cw), lambda i: (nxt_blk(i), 0)), pl.BlockSpec((HALO, cw), lambda i: (nxt_blk(i), 0)),
                  pl.BlockSpec((3, cw), lambda i: (0, 0))],
        out_specs=[col(0), col(0), col(0), pl.BlockSpec((3, cw), lambda i: (0, 0))],
        out_shape=[_sds((s, cw), BF16), _sds((s, cw), BF16), _sds((s, cw), BF16), _sds((3, cw), F32)],
        scratch_shapes=[pltpu.VMEM((HALO + tm, cw), F32), pltpu.VMEM((tm + HALO, cw), F32)],
        compiler_params=_params(("arbitrary",)),
    )(dyc, proj, proj, proj, proj, proj, dyc, proj, conv_w)


def _attn_bwd(proj, o, do, tri, *, d, name, qb=256, side=None):
    s = proj.shape[0]
    qb = min(qb, s)
    nq = s // qb
    q0, k0, v0, hp = _attn_cols(d)

    def body(*refs):
        p, i = pl.program_id(0), pl.program_id(1)
        own, late_phases = _host_side(
            side, 6, 3, 2, refs, (p == 0) & (i == 0), (p == hp - 1) & (i == 0), (p == hp - 1) & (i == nq - 1))
        q_ref, k_ref, v_ref, o_ref, do_ref, tri_ref, dq_ref, dk_ref, dv_ref, dk_acc, dv_acc = own

        @pl.when(i == 0)
        def _():
            dk_acc[...] = jnp.zeros_like(dk_acc)
            dv_acc[...] = jnp.zeros_like(dv_acc)

        causal = _stacked_causal(qb)
        tri_m = tri_ref[...]
        dov = do_ref[...]
        masks = [_head_mask(h) for h in range(2)]
        qs = _stack_heads(q_ref[...] * ATTN_SCALE, masks)
        dos = _stack_heads(dov, masks)
        dprod = dov.astype(F32) * o_ref[...]
        dtot = jnp.concatenate([jnp.sum(jnp.where(m, dprod, 0.0), axis=-1, keepdims=True) for m in masks], axis=0)

        def block(j, state, diag):
            run, grun, dq_acc = state
            rows = pl.ds(pl.multiple_of(j * qb, qb), qb)
            kb = k_ref[rows, :]
            z = lax.dot_general(qs, kb, _NT, preferred_element_type=F32)
            lg = _log_one_minus_sigmoid(z)
            beta = 1.0 - jnp.exp(lg)
            if diag:
                lg = jnp.where(causal, lg, 0.0)
            cs = _running_sum(lg, tri_m)
            a = jnp.exp(z + cs + run)
            if diag:
                a = jnp.where(causal, a, 0.0)
            ab = a.astype(BF16)
            da = lax.dot_general(dos, v_ref[rows, :], _NT, preferred_element_type=F32)
            gg = ab.astype(F32) * da
            gcs = _running_sum(gg, tri_m)
            left = jnp.where(run > UNDERFLOW_LOG, dtot - grun, 0.0)
            dz = gg - beta * (gg + (left - gcs))
            if diag:
                dz = jnp.where(causal, dz, 0.0)
            dzb = dz.astype(BF16)
            dq_acc = dq_acc + jnp.dot(_heads_to_lanes(dzb, qb), _stack_heads(kb, masks), preferred_element_type=F32)
            dk_acc[rows, :] += lax.dot_general(dzb, qs, _TN, preferred_element_type=F32)
            dv_acc[rows, :] += lax.dot_general(ab, dos, _TN, preferred_element_type=F32)
            return run + cs[:, 0:1], grun + gcs[:, 0:1], dq_acc

        zero = jnp.zeros((2 * qb, 1), F32)
        state = block(i, (zero, zero, jnp.zeros((qb, LANES), F32)), True)
        state = lax.while_loop(
            lambda st: (st[0] >= 0) & (jnp.max(st[1]) > UNDERFLOW_LOG),
            lambda st: (st[0] - 1, *block(st[0], st[1:], False)),
            (i - 1, *state))
        dq_ref[...] = (state[3] * ATTN_SCALE).astype(BF16)

        @pl.when(i == nq - 1)
        def _():
            dk_ref[...] = dk_acc[...].astype(BF16)
            dv_ref[...] = dv_acc[...].astype(BF16)

        late_phases()

    aw = hp * LANES
    s_in, s_out, s_shapes, aliases, s_scratch = _side_specs(side, 6, 3)
    res = pl.pallas_call(
        body, name=name, grid=(hp, nq),
        in_specs=[pl.BlockSpec((qb, LANES), lambda p, i: (i, q0 + p)),
                  pl.BlockSpec((s, LANES), lambda p, i: (0, k0 + p)),
                  pl.BlockSpec((s, LANES), lambda p, i: (0, v0 + p)),
                  pl.BlockSpec((qb, LANES), lambda p, i: (i, p)),
                  pl.BlockSpec((qb, LANES), lambda p, i: (i, p)),
                  pl.BlockSpec((qb, qb), lambda p, i: (0, 0))] + s_in,
        out_specs=[pl.BlockSpec((qb, LANES), lambda p, i: (i, p)),
                   pl.BlockSpec((s, LANES), lambda p, i: (0, p)),
                   pl.BlockSpec((s, LANES), lambda p, i: (0, p))] + s_out,
        out_shape=[_sds((s, aw), BF16), _sds((s, aw), BF16), _sds((s, aw), BF16)] + s_shapes,
        input_output_aliases=aliases,
        scratch_shapes=[pltpu.VMEM((s, LANES), F32), pltpu.VMEM((s, LANES), F32)] + s_scratch,
        compiler_params=_params(("arbitrary", "arbitrary")),
    )(proj, proj, proj, o, do, tri, *([] if side is None else side.operands))
    return res[:3], res[3:]


def _layer_fwd(x, mod, gains, w, tri, *, tag, side=None):
    sh1, sc1, gt1, sh2, sc2, gt2 = mod
    g_pre_mix, g_post_mix, g_pre_mlp, g_post_mlp = gains
    d = x.shape[1]
    h, proj = _norm_mod_matmul(x, g_pre_mix, sc1, sh1, w["w_in"], name=f"in_proj_{tag}")
    yc = _conv_fwd(proj, w["conv_w"], name=f"conv_fwd_{tag}")
    o, side_out = _attn_fwd(proj, tri, d=d, name=f"attn_fwd_{tag}", side=side)
    ycv, yat, merged, mix, x1 = _mix_out(yc, o, proj, x, w["w_proj_conv"], w["w_proj_attn"], w["w_out"],
                                         g_post_mix, gt1, name=f"mix_out_{tag}")
    h2, a = _norm_mod_matmul(x1, g_pre_mlp, sc2, sh2, w["w_mlp_in"], name=f"mlp_in_{tag}")
    ff, x2 = _mlp_out(a, x1, w["w_mlp_out"], g_post_mlp, gt2, name=f"mlp_out_{tag}")
    saved = dict(x=x, h=h, proj=proj, yc=yc, o=o, ycv=ycv, yat=yat, merged=merged, mix=mix, x1=x1, h2=h2, a=a, ff=ff)
    return x2, saved, side_out


def _layer_bwd(dx2, sv, mod, gains, w, tri, *, tag, side=None):
    sh1, sc1, gt1, sh2, sc2, gt2 = mod
    g_pre_mix, g_post_mix, g_pre_mlp, g_post_mlp = gains
    d = dx2.shape[1]
    dff, da, dgt2, dg_post_mlp = _mlp_out_bwd(dx2, sv["ff"], sv["a"], w["w_mlp_out"], g_post_mlp, gt2,
                                              name=f"mlp_out_bwd_{tag}")
    gw_mlp_out = _matmul_tn(sv["a"], dff, relu2=True, name=f"gw_mlp_out_{tag}")
    dx1, dsh2, dsc2, dg_pre_mlp = _matmul_nt_norm_bwd(da, w["w_mlp_in"], sv["x1"], dx2, g_pre_mlp, sc2,
                                                      name=f"mlp_in_bwd_{tag}")
    gw_mlp_in = _matmul_tn(sv["h2"], da, name=f"gw_mlp_in_{tag}")
    dmix, dycv, dyat, dyc, do, dga, dgb, dgt1, dg_post_mix = _mix_out_bwd(
        dx1, sv["mix"], sv["proj"], sv["ycv"], sv["yat"], w["w_out"], w["w_proj_conv"], w["w_proj_attn"],
        g_post_mix, gt1, name=f"mix_out_bwd_{tag}")
    gw_out = _matmul_tn(sv["merged"], dmix, name=f"gw_out_{tag}")
    gw_proj_conv = _matmul_tn(sv["yc"], dycv, name=f"gw_proj_conv_{tag}")
    gw_proj_attn = _matmul_tn(sv["o"].astype(BF16), dyat, name=f"gw_proj_attn_{tag}")
    dbg, dcg, du, g_conv_w = _conv_bwd(dyc, sv["proj"], w["conv_w"], name=f"conv_bwd_{tag}")
    (dq, dk, dv), side_out = _attn_bwd(sv["proj"], sv["o"], do, tri, d=d, name=f"attn_bwd_{tag}", side=side)
    dproj = jnp.concatenate([dbg, dcg, du, dq, dk, dv, dga, dgb], axis=1)
    dx0, dsh1, dsc1, dg_pre_mix = _matmul_nt_norm_bwd(dproj, w["w_in"], sv["x"], dx1, g_pre_mix, sc1,
                                                      name=f"in_proj_bwd_{tag}")
    gw_in = _matmul_tn(sv["h"], dproj, name=f"gw_in_{tag}")
    grads = dict(w_in=gw_in, conv_w=g_conv_w, w_proj_conv=gw_proj_conv, w_proj_attn=gw_proj_attn, w_out=gw_out,
                 w_mlp_in=gw_mlp_in, w_mlp_out=gw_mlp_out)
    dmod = jnp.concatenate([dsh1, dsc1, dgt1, dsh2, dsc2, dgt2], axis=0)
    dgains = jnp.concatenate([dg_pre_mix, dg_post_mix, dg_pre_mlp, dg_post_mlp], axis=0)
    return dx0, grads, dmod, dgains, side_out


BIG =("w_in", "w_proj_conv", "w_proj_attn", "w_out", "w_mlp_in", "w_mlp_out")
SHARD_AXIS = dict(w_in=1, w_proj_conv=1, w_proj_attn=1, w_out=0, w_mlp_in=1, w_mlp_out=0)


class _LocalWeights:
    def __init__(self, wlayers):
        self.wlayers = wlayers

    def weights(self, l):
        return self.wlayers[l]

    def fwd_side(self, l):
        return None

    def fwd_side_done(self, l, outs):
        pass

    def bwd_side(self, l):
        return None

    def bwd_side_done(self, l, outs):
        pass

    def grads(self, l, g):
        return g


def _local_step(x, target, mods, gains, conv_w, hooks):
    depth = mods.shape[0]
    tri = _tri(min(256, x.shape[0]))
    saved = []
    for l in range(depth):
        wl = dict(hooks.weights(l))
        wl["conv_w"] = conv_w[l]
        mod = [mods[l, k:k + 1] for k in range(N_MOD)]
        gl = [gains[l, k:k + 1] for k in range(4)]
        x, sv, side_out = _layer_fwd(x, mod, gl, wl, tri, tag=str(l), side=hooks.fwd_side(l))
        hooks.fwd_side_done(l, side_out)
        saved.append((sv, mod, gl, wl))
    dx, loss = _loss_grad(x, target, name="loss_grad")
    grads, dmods, dgains = [None] * depth, [None] * depth, [None] * depth
    for l in reversed(range(depth)):
        sv, mod, gl, wl = saved[l]
        dx, g, dmods[l], dgains[l], side_out = _layer_bwd(dx, sv, mod, gl, wl, tri, tag=str(l), side=hooks.bwd_side(l))
        hooks.bwd_side_done(l, side_out)
        grads[l] = hooks.grads(l, g)
    return loss, dx, grads, jnp.stack(dmods), jnp.stack(dgains)


def _coords():
    return lax.axis_index("x"), lax.axis_index("y"), lax.axis_index("c")


def _flip(v, f):
    return 1 - v if f else v


def _all_gather_small(v, *, name):
    r, c_ = v.shape

    def body(v_ref, out_ref, send_sems, recv_sems, local_sem):
        x, y, c = _coords()
        me = 4 * x + 2 * y + c
        mine = pltpu.make_async_copy(v_ref, out_ref.at[me], local_sem)
        mine.start()
        copies = []
        for k in range(1, 8):
            fx, fy, fc = (k >> 2) & 1, (k >> 1) & 1, k & 1
            px, py, pc = _flip(x, fx), _flip(y, fy), _flip(c, fc)
            out = pltpu.make_async_remote_copy(src_ref=v_ref, dst_ref=out_ref.at[me], send_sem=send_sems.at[k - 1],
                                               recv_sem=recv_sems.at[k - 1], device_id=(px, py, pc), device_id_type=MESH)
            out.start()
            back = pltpu.make_async_remote_copy(src_ref=v_ref, dst_ref=out_ref.at[4 * px + 2 * py + pc],
                                                send_sem=send_sems.at[k - 1], recv_sem=recv_sems.at[k - 1],
                                                device_id=(px, py, pc), device_id_type=MESH)
            copies.append((out, back))
        for out, back in copies:
            back.wait_recv()
        for out, back in copies:
            out.wait_send()
        mine.wait()

    return pl.pallas_call(
        body, name=name,
        in_specs=[pl.BlockSpec(memory_space=pltpu.VMEM)],
        out_specs=pl.BlockSpec(memory_space=pltpu.VMEM),
        out_shape=_sds((8, r, c_), F32),
        scratch_shapes=[pltpu.SemaphoreType.DMA((7,)), pltpu.SemaphoreType.DMA((7,)), pltpu.SemaphoreType.DMA],
    )(v)


def _shard_dims(full_shape, axis):
    k, n = full_shape
    return (k // 4, n) if axis == 0 else (k, n // 4)


def _shard_window(ref, axis, chip, half, rows, cols):
    r0, rn = (0, rows) if half is None else (half * (rows // 2), rows // 2)
    if axis == 1:
        return ref.at[pl.ds(r0, rn), pl.ds(chip * cols, cols)]
    return ref.at[pl.ds(chip * rows + r0, rn), :]


def _cast_place(w, layer, axis, chip_arr, *, name, tr=256):
    _, rows, cols = w.shape
    tr = _fit(tr, rows)
    nb = rows // tr
    full = (rows * 4, cols) if axis == 0 else (rows, cols * 4)

    def body(chip_ref, w_ref, o_ref):
        o_ref[...] = w_ref[0].astype(BF16)

    if axis == 1:
        out_map = lambda i, chip: (i, chip[0])
    else:
        out_map = lambda i, chip: (chip[0] * nb + i, 0)
    grid_spec = pltpu.PrefetchScalarGridSpec(
        num_scalar_prefetch=1, grid=(nb,),
        in_specs=[pl.BlockSpec((1, tr, cols), lambda i, chip: (layer, i, 0))],
        out_specs=pl.BlockSpec((tr, cols), out_map))
    return pl.pallas_call(body, name=name, grid_spec=grid_spec, out_shape=_sds(full, BF16),
                          compiler_params=_params(("arbitrary",)))(chip_arr, w)


def _gather_side(fulls, axes):
    n = len(fulls)

    def copies(outs, sems):
        send_sems, recv_sems = sems
        x, y, c = _coords()
        chip = 2 * x + y
        sibling = (x, y, 1 - c)
        table = []
        for w in range(n):
            rows, cols = _shard_dims(outs[w].shape, axes[w])
            win = functools.partial(_shard_window, outs[w], axes[w], rows=rows, cols=cols)
            for j, (fx, fy) in enumerate(OTHER_CHIPS):
                px, py = _flip(x, fx), _flip(y, fy)
                pchip = 2 * px + py

                def copy(piece, sem, to):
                    return pltpu.make_async_remote_copy(src_ref=piece, dst_ref=piece, send_sem=send_sems.at[w, sem],
                                                        recv_sem=recv_sems.at[w, sem], device_id=to, device_id_type=MESH)

                table.append((copy(win(chip, c), j, (px, py, c)), copy(win(pchip, c), j, (px, py, c)),
                              copy(win(pchip, c), 3 + j, sibling), copy(win(pchip, 1 - c), 3 + j, sibling)))
        return table

    def start(ins, outs, sems):
        for send, _, _, _ in copies(outs, sems):
            send.start()

    def mid(ins, outs, sems):
        for _, landed, pass_on, _ in copies(outs, sems):
            landed.wait_recv()
            pass_on.start()

    def finish(ins, outs, sems):
        table = copies(outs, sems)
        for _, _, _, from_sibling in table:
            from_sibling.wait_recv()
        for send, _, pass_on, _ in table:
            send.wait_send()
            pass_on.wait_send()

    return _Side(fulls, [_sds(f.shape, f.dtype) for f in fulls], {w: w for w in range(n)},
                 [pltpu.SemaphoreType.DMA((n, 6)), pltpu.SemaphoreType.DMA((n, 6))], start, mid, finish)


def _exchange_side(grads, axes):
    n = len(grads)
    out_shapes = []
    for g, ax in zip(grads, axes):
        rows, cols = _shard_dims(g.shape, ax)
        out_shapes.append(_sds((7, rows // 2, cols), g.dtype))

    def copies(ins, outs, sems):
        send_sems, recv_sems = sems
        x, y, c = _coords()
        table = []
        for w in range(n):
            rows, cols = _shard_dims(ins[w].shape, axes[w])
            for k in range(1, 8):
                fx, fy, fc = (k >> 2) & 1, (k >> 1) & 1, k & 1
                px, py, pc = _flip(x, fx), _flip(y, fy), _flip(c, fc)
                piece = _shard_window(ins[w], axes[w], 2 * px + py, pc, rows, cols)
                table.append(pltpu.make_async_remote_copy(
                    src_ref=piece, dst_ref=outs[w].at[k - 1], send_sem=send_sems.at[w, k - 1],
                    recv_sem=recv_sems.at[w, k - 1], device_id=(px, py, pc), device_id_type=MESH))
        return table

    def start(ins, outs, sems):
        for cp in copies(ins, outs, sems):
            cp.start()

    def finish(ins, outs, sems):
        table = copies(ins, outs, sems)
        for cp in table:
            cp.wait_recv()
        for cp in table:
            cp.wait_send()

    return _Side(grads, out_shapes, {}, [pltpu.SemaphoreType.DMA((n, 7)), pltpu.SemaphoreType.DMA((n, 7))],
                 start, None, finish)


def _rs_sum_join(g, got, out_prev, layer, depth, axis, ids, *, name, tr=256):
    _, rows2, cols = got.shape
    tr = _fit(tr, rows2)
    nt = rows2 // tr
    if axis == 1:
        own_map = lambda i, ids_: (ids_[1] * nt + i, ids_[0])
    else:
        own_map = lambda i, ids_: ((ids_[0] * 2 + ids_[1]) * nt + i, 0)

    def body(ids_ref, g_ref, got_ref, *rest):
        out_ref, buf, local_sems, send_sems, recv_sem = rest[-5:]
        i = pl.program_id(0)
        x, y, c = _coords()
        sibling = (x, y, 1 - c)

        def copies(step, slot):
            rows_mine = pl.ds(c * rows2 + step * tr, tr)
            dst = out_ref.at[layer, rows_mine, :]
            keep = pltpu.make_async_copy(buf.at[slot], dst, local_sems.at[slot])
            give = pltpu.make_async_remote_copy(src_ref=buf.at[slot], dst_ref=dst, send_sem=send_sems.at[slot],
                                                recv_sem=recv_sem, device_id=sibling, device_id_type=MESH)
            return keep, give

        def drain(step, slot):
            keep, give = copies(step, slot)
            keep.wait()
            give.wait_send()

        slot = i % 2

        @pl.when(i >= 2)
        def _():
            drain(i - 2, slot)

        acc = g_ref[...].astype(F32)
        for k in range(7):
            acc = acc + got_ref[k].astype(F32)
        buf[slot] = acc
        keep, give = copies(i, slot)
        keep.start()
        give.start()

        @pl.when(i == nt - 1)
        def _():
            if nt >= 2:
                drain(nt - 2, (nt - 2) % 2)
            drain(nt - 1, (nt - 1) % 2)
            theirs = out_ref.at[layer, pl.ds((1 - c) * rows2, rows2), :]
            pltpu.make_async_remote_copy(src_ref=theirs, dst_ref=theirs, send_sem=send_sems.at[0], recv_sem=recv_sem,
                                         device_id=sibling, device_id_type=MESH).wait_recv()

    hbm = pl.BlockSpec(memory_space=pltpu.HBM)
    in_specs = [pl.BlockSpec((tr, cols), own_map), pl.BlockSpec((7, tr, cols), lambda i, ids_: (0, i, 0))]
    operands = [ids, g, got]
    aliases = {}
    if out_prev is not None:
        in_specs.append(hbm)
        operands.append(out_prev)
        aliases = {3: 0}
    grid_spec = pltpu.PrefetchScalarGridSpec(
        num_scalar_prefetch=1, grid=(nt,), in_specs=in_specs, out_specs=hbm,
        scratch_shapes=[pltpu.VMEM((2, tr, cols), F32), pltpu.SemaphoreType.DMA((2,)), pltpu.SemaphoreType.DMA((2,)),
                        pltpu.SemaphoreType.DMA])
    return pl.pallas_call(body, name=name, grid_spec=grid_spec, out_shape=_sds((depth, 2 * rows2, cols), F32),
                          input_output_aliases=aliases, compiler_params=_params(("arbitrary",)))(*operands)


def _reduce_layer(grads, got, axes, layer, depth, ids, out_prev):
    outs = []
    for w, (g, ax) in enumerate(zip(grads, axes)):
        prev = None if out_prev is None else out_prev[w]
        outs.append(_rs_sum_join(g, got[w], prev, layer, depth, ax, ids, name=f"rs_sum_join_{layer}_{w}"))
    return outs


def _flat_rows(shape):
    rows = 1
    for s in shape[:-1]:
        rows *= s
    return rows, shape[-1]


def _row_tile(rows, cols, cap_bytes=2 * 1024 * 1024):
    t = rows
    while t * cols * 4 > cap_bytes and t % 16 == 0:
        t //= 2
    return t


def _ada_fwd(c_all, w_ada, b_loc, *, name, tn=512):
    l, d, nl = w_ada.shape
    b = c_all.shape[0]
    tn = min(tn, nl)

    def body(c_ref, w_ref, b_ref, o_ref):
        o_ref[0] = jnp.dot(c_ref[...], w_ref[0], preferred_element_type=F32,
                           precision=lax.Precision.HIGHEST) + b_ref[0]

    return pl.pallas_call(
        body, name=name, grid=(l, nl // tn),
        in_specs=[pl.BlockSpec((b, d), lambda i, j: (0, 0)), pl.BlockSpec((1, d, tn), lambda i, j: (i, 0, j)),
                  pl.BlockSpec((1, 1, tn), lambda i, j: (i, 0, j))],
        out_specs=pl.BlockSpec((1, b, tn), lambda i, j: (i, 0, j)),
        out_shape=_sds((l, b, nl), F32),
        compiler_params=_params(("parallel", "parallel")),
    )(c_all, w_ada, b_loc)


def _ada_bwd(c_t, dmod_loc, *, name, tn=512):
    d, b = c_t.shape
    l, _, nl = dmod_loc.shape
    tn = min(tn, nl)

    def body(c_ref, dm_ref, o_ref):
        cv = c_ref[...]
        dm = dm_ref[0]
        acc = cv[:, 0:1] * dm[0:1, :]
        for k in range(1, b):
            acc = acc + cv[:, k:k + 1] * dm[k:k + 1, :]
        o_ref[0] = acc

    return pl.pallas_call(
        body, name=name, grid=(l, nl // tn),
        in_specs=[pl.BlockSpec((d, b), lambda i, j: (0, 0)), pl.BlockSpec((1, b, tn), lambda i, j: (i, 0, j))],
        out_specs=pl.BlockSpec((1, d, tn), lambda i, j: (i, 0, j)),
        out_shape=_sds((l, d, nl), F32),
        compiler_params=_params(("parallel", "parallel")),
    )(c_t, dmod_loc)


def _sum_devices(p, *, name):
    k, r, c_ = p.shape

    def body(p_ref, o_ref):
        acc = p_ref[0]
        for j in range(1, k):
            acc = acc + p_ref[j]
        o_ref[...] = acc

    return pl.pallas_call(body, name=name, out_shape=_sds((r, c_), F32),
                          in_specs=[pl.BlockSpec(memory_space=pltpu.VMEM)],
                          out_specs=pl.BlockSpec(memory_space=pltpu.VMEM))(p)


def _adamw(w, g, m, v, *, name):
    shape = w.shape
    rows, cols = _flat_rows(shape)
    tr = _row_tile(rows, cols, cap_bytes=1024 * 1024)
    c1 = 1.0 / (1.0 - ADAM_B1 ** ADAM_STEP)
    c2 = 1.0 / (1.0 - ADAM_B2 ** ADAM_STEP)

    def body(w_ref, g_ref, m_ref, v_ref, d_ref, nm_ref, nv_ref):
        gv = g_ref[...]
        nm = ADAM_B1 * m_ref[...] + (1.0 - ADAM_B1) * gv
        nv = ADAM_B2 * v_ref[...] + (1.0 - ADAM_B2) * (gv * gv)
        m_hat = nm * c1
        v_hat = nv * c2
        d_ref[...] = -ADAM_LR * (m_hat / (jnp.sqrt(v_hat) + ADAM_EPS) + ADAM_WD * w_ref[...])
        nm_ref[...] = nm
        nv_ref[...] = nv

    spec = pl.BlockSpec((tr, cols), lambda i: (i, 0))
    flat = lambda a: a.reshape(rows, cols)
    outs = pl.pallas_call(body, name=name, grid=(rows // tr,), in_specs=[spec] * 4, out_specs=[spec] * 3,
                          out_shape=[_sds((rows, cols), F32)] * 3, compiler_params=_params(("parallel",)),
                          )(flat(w), flat(g), flat(m), flat(v))
    return tuple(o.reshape(shape) for o in outs)


WEIGHTS = ("w_ada", "b_ada", "g_pre_mix", "g_post_mix", "g_pre_mlp", "g_post_mlp", "w_in", "conv_w",
           "w_proj_conv", "w_proj_attn", "w_out", "w_mlp_in", "w_mlp_out")
GAINS = ("g_pre_mix", "g_post_mix", "g_pre_mlp", "g_post_mlp")


def kernel(x, c, w_ada, b_ada, g_pre_mix, g_post_mix, g_pre_mlp, g_post_mlp, w_in, conv_w, w_proj_conv, w_proj_attn, w_out, w_mlp_in, w_mlp_out, loss_target, m_w_ada, m_b_ada, m_g_pre_mix, m_g_post_mix, m_g_pre_mlp, m_g_post_mlp, m_w_in, m_conv_w, m_w_proj_conv, m_w_proj_attn, m_w_out, m_w_mlp_in, m_w_mlp_out, v_w_ada, v_b_ada, v_g_pre_mix, v_g_post_mix, v_g_pre_mlp, v_g_post_mlp, v_w_in, v_conv_w, v_w_proj_conv, v_w_proj_attn, v_w_out, v_w_mlp_in, v_w_mlp_out):
    params = dict(w_ada=w_ada, b_ada=b_ada, g_pre_mix=g_pre_mix, g_post_mix=g_post_mix, g_pre_mlp=g_pre_mlp,
                  g_post_mlp=g_post_mlp, w_in=w_in, conv_w=conv_w, w_proj_conv=w_proj_conv, w_proj_attn=w_proj_attn,
                  w_out=w_out, w_mlp_in=w_mlp_in, w_mlp_out=w_mlp_out)
    m_in = dict(w_ada=m_w_ada, b_ada=m_b_ada, g_pre_mix=m_g_pre_mix, g_post_mix=m_g_post_mix, g_pre_mlp=m_g_pre_mlp,
                g_post_mlp=m_g_post_mlp, w_in=m_w_in, conv_w=m_conv_w, w_proj_conv=m_w_proj_conv,
                w_proj_attn=m_w_proj_attn, w_out=m_w_out, w_mlp_in=m_w_mlp_in, w_mlp_out=m_w_mlp_out)
    v_in = dict(w_ada=v_w_ada, b_ada=v_b_ada, g_pre_mix=v_g_pre_mix, g_post_mix=v_g_post_mix, g_pre_mlp=v_g_pre_mlp,
                g_post_mlp=v_g_post_mlp, w_in=v_w_in, conv_w=v_conv_w, w_proj_conv=v_w_proj_conv,
                w_proj_attn=v_w_proj_attn, w_out=v_w_out, w_mlp_in=v_w_mlp_in, w_mlp_out=v_w_mlp_out)

    depth, d, nl_ada = w_ada.shape
    ix, iy, ic = lax.axis_index("x"), lax.axis_index("y"), lax.axis_index("c")
    chip = 2 * ix + iy
    me = 4 * ix + 2 * iy + ic
    xs = x[0]
    target = loss_target[0]

    c_all = _all_gather_small(jnp.broadcast_to(c, (8, d)), name="gather_c")[:, 0, :]
    b_loc = lax.dynamic_slice_in_dim(b_ada, chip * nl_ada, nl_ada, axis=1)[:, None, :]
    mod_loc = _ada_fwd(c_all, w_ada, b_loc, name="ada_fwd")
    mod_all = _all_gather_small(mod_loc.reshape(depth * 8, nl_ada), name="gather_mod")
    mod_all = mod_all.reshape(4, 2, depth, 8, nl_ada)[:, 0]
    mod_me = lax.dynamic_index_in_dim(mod_all, me, axis=2, keepdims=False)
    mods = jnp.transpose(mod_me, (1, 0, 2)).reshape(depth, N_MOD, d)

    chip_arr = jnp.reshape(chip, (1,)).astype(jnp.int32)
    ids = jnp.stack([chip, ic]).astype(jnp.int32)
    axes = [SHARD_AXIS[k] for k in BIG]
    placed = [[_cast_place(params[k], l, SHARD_AXIS[k], chip_arr, name=f"place_{k}_{l}") for k in BIG]
              for l in range(depth)]
    conv_full = _all_gather_small(
        jnp.pad(conv_w.reshape(depth * 3, -1), ((0, 8 - depth * 3), (0, 0))), name="gather_conv_w")
    conv_full = conv_full.reshape(4, 2, 8, -1)[:, 0, :depth * 3]
    conv_full = jnp.transpose(conv_full, (1, 0, 2)).reshape(depth, 3, -1)

    gains = jnp.stack([params[k] for k in GAINS], axis=1)
    class Schedule:
        def __init__(self):
            self.full = {0: _side_call(_gather_side(placed[0], axes), name="gather_weights_0")}
            self.pending = None
            self.reduced = None

        def weights(self, l):
            return dict(zip(BIG, self.full[l]))

        def fwd_side(self, l):
            return _gather_side(placed[l + 1], axes) if l + 1 < depth else None

        def fwd_side_done(self, l, outs):
            if l + 1 < depth:
                self.full[l + 1] = outs

        def bwd_side(self, l):
            return None if self.pending is None else _exchange_side(self.pending[1], axes)

        def bwd_side_done(self, l, outs):
            if self.pending is not None:
                layer, g = self.pending
                self.reduced = _reduce_layer(g, outs, axes, layer, depth, ids, self.reduced)
                self.pending = None

        def grads(self, l, g):
            big = [g[k] for k in BIG]
            if l > 0:
                self.pending = (l, big)
            else:
                got = _side_call(_exchange_side(big, axes), name=f"rs_exchange_{l}")
                self.reduced = _reduce_layer(big, got, axes, l, depth, ids, self.reduced)
            return g["conv_w"]

    schedule = Schedule()
    loss, dx, conv_grads, dmods, dgains = _local_step(xs, target, mods, gains, conv_full, schedule)
    reduced = [schedule.reduced]

    cw = conv_full.shape[2]
    rows = [dmods.reshape(depth * N_MOD, d), dgains.reshape(depth * 4, d),
            jnp.stack(conv_grads).reshape(-1, d), jnp.broadcast_to(loss, (1, d))]
    payload = jnp.concatenate(rows, axis=0)
    n_rows = payload.shape[0]
    pad = (-n_rows) % 8
    payload = jnp.pad(payload, ((0, pad), (0, 0)))
    everyone = _all_gather_small(payload, name="gather_small_grads")
    total = _sum_devices(everyone, name="sum_small_grads")
    r0 = depth * N_MOD
    grads = {}
    grads["b_ada"] = total[:r0].reshape(depth, N_MOD * d)
    gsum = total[r0:r0 + depth * 4].reshape(depth, 4, d)
    for k, name in enumerate(GAINS):
        grads[name] = gsum[:, k]
    r1 = r0 + depth * 4
    n_conv = (depth * 3 * cw) // d
    conv_g = total[r1:r1 + n_conv].reshape(depth, 3, cw)
    grads["conv_w"] = lax.dynamic_slice_in_dim(conv_g, chip * (cw // 4), cw // 4, axis=2)
    loss_out = total[r1 + n_conv, 0]
    dmod_all = everyone[:, :r0].reshape(8, depth, N_MOD * d)
    dmod_loc = lax.dynamic_slice_in_dim(dmod_all, chip * nl_ada, nl_ada, axis=2)
    grads["w_ada"] = _ada_bwd(c_all.T, jnp.transpose(dmod_loc, (1, 0, 2)), name="ada_bwd")

    for k, g in zip(BIG, reduced[0]):
        grads[k] = g

    deltas, new_m, new_v = {}, {}, {}
    for k in WEIGHTS:
        deltas[k], new_m[k], new_v[k] = _adamw(params[k], grads[k], m_in[k], v_in[k], name=f"adamw_{k}")

    return (loss_out, dx[None], *[grads[k] for k in WEIGHTS], *[deltas[k] for k in WEIGHTS],
            *[new_m[k] for k in WEIGHTS], *[new_v[k] for k in WEIGHTS])
```

```python
import functools

import jax
import jax.numpy as jnp
from jax import lax
from jax.experimental import pallas as pl
from jax.experimental.pallas import tpu as pltpu

F32 = jnp.float32
BF16 = jnp.bfloat16
EPS = 1e-6
N_MOD = 6
HEAD_DIM = 64
LANES = 128
ATTN_SCALE = 1.0 / 8.0
UNDERFLOW_LOG = -90.0
ADAM_LR = 0.001
ADAM_B1 = 0.9
ADAM_B2 = 0.999
ADAM_EPS = 1e-08
ADAM_WD = 0.01
ADAM_STEP = 10
VMEM_LIMIT = 56 * 1024 * 1024
MESH = pl.DeviceIdType.MESH
OTHER_CHIPS = ((1, 0), (0, 1), (1, 1))

_NT = (((1,), (1,)), ((), ()))
_TN = (((0,), (0,)), ((), ()))


def _sds(shape, dtype):
    return jax.ShapeDtypeStruct(shape, dtype)


def _params(sem):
    return pltpu.CompilerParams(dimension_semantics=sem, vmem_limit_bytes=VMEM_LIMIT)


def _fit(t, n):
    t = min(t, n)
    while n % t:
        t //= 2
    return t


def _vec_spec(d, nargs):
    if nargs == 1:
        return pl.BlockSpec((1, d), lambda i: (0, 0))
    return pl.BlockSpec((1, d), lambda i, j: (0, 0))


def _log_one_minus_sigmoid(z):
    return -jnp.log(1.0 + jnp.exp(-jnp.abs(z))) - jnp.maximum(z, 0.0)


def _sigmoid(z):
    t = jnp.exp(-jnp.abs(z))
    return jnp.where(z >= 0.0, 1.0, t) / (1.0 + t)


def _split_bf16(a):
    hi = a.astype(BF16)
    lo = (a - hi.astype(F32)).astype(BF16)
    return hi, lo


def _rms_bwd(dn, xin, g):
    r = lax.rsqrt(jnp.mean(xin * xin, axis=-1, keepdims=True) + EPS)
    xh = xin * r
    dxh = dn * g
    dxin = r * (dxh - xh * jnp.mean(dxh * xh, axis=-1, keepdims=True))
    return dxin, xh


def _colsum(a):
    return jnp.sum(a, axis=0, keepdims=True)


def _norm_mod_matmul(x, g, sc, sh, w, *, name, tm=256, side=None):
    s, d = x.shape
    n = w.shape[1]
    tm = _fit(tm, s)
    nt = s // tm

    def body(*refs):
        i = pl.program_id(0)
        (x_ref, g_ref, sc_ref, sh_ref, w_ref, h_ref, o_ref), late_phases = _host_side(
            side, 5, 2, 0, refs, i == 0, i == (3 * nt) // 4, i == nt - 1)
        xv = x_ref[...]
        r = lax.rsqrt(jnp.mean(xv * xv, axis=-1, keepdims=True) + EPS)
        h = ((xv * r * g_ref[...]) * (1.0 + sc_ref[...]) + sh_ref[...]).astype(BF16)
        h_ref[...] = h
        o_ref[...] = jnp.dot(h, w_ref[...], preferred_element_type=F32).astype(BF16)
        late_phases()

    s_in, s_out, s_shapes, aliases, s_scratch = _side_specs(side, 5, 2)
    res = pl.pallas_call(
        body, name=name, grid=(nt,),
        in_specs=[pl.BlockSpec((tm, d), lambda i: (i, 0)), _vec_spec(d, 1), _vec_spec(d, 1), _vec_spec(d, 1),
                  pl.BlockSpec((d, n), lambda i: (0, 0))] + s_in,
        out_specs=[pl.BlockSpec((tm, d), lambda i: (i, 0)), pl.BlockSpec((tm, n), lambda i: (i, 0))] + s_out,
        out_shape=[_sds((s, d), BF16), _sds((s, n), BF16)] + s_shapes,
        input_output_aliases=aliases, scratch_shapes=s_scratch,
        compiler_params=_params(("arbitrary",)),
    )(x, g, sc, sh, w, *([] if side is None else side.operands))
    return res[:2], res[2:]


HALO = 16


def _conv_fwd(proj, conv_w, *, name, tm=512):
    s = proj.shape[0]
    cw = conv_w.shape[1]
    tm = min(tm, s)
    nb = tm // HALO

    def body(bg_ref, cg_ref, u_ref, cgh_ref, uh_ref, w_ref, yc_ref, vbuf):
        i = pl.program_id(0)
        vv = cg_ref[...].astype(F32) * u_ref[...].astype(F32)
        halo = cgh_ref[...].astype(F32) * uh_ref[...].astype(F32)
        vbuf[0:HALO, :] = jnp.where(i > 0, halo, 0.0)
        vbuf[HALO:HALO + tm, :] = vv
        v1 = vbuf[HALO - 1:HALO - 1 + tm, :]
        v2 = vbuf[HALO - 2:HALO - 2 + tm, :]
        w = w_ref[...]
        y = w[2:3, :] * vv + w[1:2, :] * v1 + w[0:1, :] * v2
        yc_ref[...] = (bg_ref[...].astype(F32) * y).astype(BF16)

    def prev(i):
        return jnp.maximum(i * nb - 1, 0)

    return pl.pallas_call(
        body, name=name, grid=(s // tm,),
        in_specs=[pl.BlockSpec((tm, cw), lambda i: (i, 0)), pl.BlockSpec((tm, cw), lambda i: (i, 1)),
                  pl.BlockSpec((tm, cw), lambda i: (i, 2)),
                  pl.BlockSpec((HALO, cw), lambda i: (prev(i), 1)), pl.BlockSpec((HALO, cw), lambda i: (prev(i), 2)),
                  pl.BlockSpec((3, cw), lambda i: (0, 0))],
        out_specs=pl.BlockSpec((tm, cw), lambda i: (i, 0)),
        out_shape=_sds((s, cw), BF16),
        scratch_shapes=[pltpu.VMEM((HALO + tm, cw), F32)],
        compiler_params=_params(("arbitrary",)),
    )(proj, proj, proj, proj, proj, conv_w)


def _tri(qb):
    r = lax.broadcasted_iota(jnp.int32, (qb, qb), 0)
    c = lax.broadcasted_iota(jnp.int32, (qb, qb), 1)
    return (r >= c).astype(BF16)


def _head_mask(h):
    lane = lax.broadcasted_iota(jnp.int32, (1, LANES), 1)
    return (lane >= HEAD_DIM * h) & (lane < HEAD_DIM * (h + 1))


def _stack_heads(a, masks):
    return jnp.concatenate([jnp.where(m, a, 0).astype(BF16) for m in masks], axis=0)


def _heads_to_lanes(a, qb):
    return jnp.concatenate([a[:qb], a[qb:]], axis=1)


def _stacked_causal(qb):
    row = lax.broadcasted_iota(jnp.int32, (2 * qb, qb), 0)
    col = lax.broadcasted_iota(jnp.int32, (2 * qb, qb), 1)
    return col < jnp.where(row >= qb, row - qb, row)


def _running_sum(a, tri_m):
    n = a.shape[0]
    hi, lo = _split_bf16(a)
    both = jnp.dot(jnp.concatenate([hi, lo], axis=0), tri_m, preferred_element_type=F32)
    return both[:n] + both[n:]


def _attn_cols(d):
    cw = d // 2
    hp = (d // 2) // LANES
    q0 = (3 * cw) // LANES
    return q0, q0 + hp, q0 + 2 * hp, hp


class _Side:
    def __init__(self, operands, out_shapes, aliases, scratch, start, mid, finish):
        self.operands, self.out_shapes, self.aliases, self.scratch = list(operands), list(out_shapes), aliases, list(scratch)
        self.start, self.mid, self.finish = start, mid, finish


def _side_call(side, *, name):
    n_in, n_out = len(side.operands), len(side.out_shapes)

    def body(*refs):
        parts = refs[:n_in], refs[n_in:n_in + n_out], refs[n_in + n_out:]
        side.start(*parts)
        if side.mid is not None:
            side.mid(*parts)
        side.finish(*parts)

    hbm = pl.BlockSpec(memory_space=pltpu.HBM)
    return pl.pallas_call(body, name=name, in_specs=[hbm] * n_in, out_specs=[hbm] * n_out, out_shape=side.out_shapes,
                          input_output_aliases=dict(side.aliases), scratch_shapes=side.scratch)(*side.operands)


def _host_side(side, n_in, n_out, n_scratch, refs, first, late, last):
    if side is None:
        return refs, lambda: None
    s_in, s_out = len(side.operands), len(side.out_shapes)
    ins = refs[:n_in]
    side_in = refs[n_in:n_in + s_in]
    outs = refs[n_in + s_in:n_in + s_in + n_out]
    side_out = refs[n_in + s_in + n_out:n_in + s_in + n_out + s_out]
    rest = refs[n_in + s_in + n_out + s_out:]
    scratch, sems = rest[:n_scratch], rest[n_scratch:]
    parts = (side_in, side_out, sems)
    pl.when(first)(lambda: side.start(*parts))

    def run_late_phases():
        if side.mid is not None:
            pl.when(late)(lambda: side.mid(*parts))
        pl.when(last)(lambda: side.finish(*parts))

    return (*ins, *outs, *scratch), run_late_phases


def _side_specs(side, n_in, n_out):
    if side is None:
        return [], [], [], {}, []
    hbm = pl.BlockSpec(memory_space=pltpu.HBM)
    s_in = len(side.operands)
    aliases = {n_in + a: n_out + b for a, b in side.aliases.items()}
    return [hbm] * s_in, [hbm] * len(side.out_shapes), side.out_shapes, aliases, side.scratch


def _attn_fwd(proj, tri, *, d, name, qb=256, side=None):
    s = proj.shape[0]
    qb = min(qb, s)
    nq = s // qb
    q0, k0, v0, hp = _attn_cols(d)

    def body(*refs):
        p, i = pl.program_id(0), pl.program_id(1)
        (q_ref, k_ref, v_ref, tri_ref, o_ref), late_phases = _host_side(
            side, 4, 1, 0, refs, (p == 0) & (i == 0), (p == hp - 1) & (i == 0), (p == hp - 1) & (i == nq - 1))
        causal = _stacked_causal(qb)
        tri_m = tri_ref[...]
        masks = [_head_mask(h) for h in range(2)]
        qs = _stack_heads(q_ref[...] * ATTN_SCALE, masks)

        def block(j, state, diag):
            run, acc = state
            rows = pl.ds(pl.multiple_of(j * qb, qb), qb)
            z = lax.dot_general(qs, k_ref[rows, :], _NT, preferred_element_type=F32)
            lg = _log_one_minus_sigmoid(z)
            if diag:
                lg = jnp.where(causal, lg, 0.0)
            cs = _running_sum(lg, tri_m)
            a = jnp.exp(z + cs + run)
            if diag:
                a = jnp.where(causal, a, 0.0)
            ab = a.astype(BF16)
            acc = acc + jnp.dot(_heads_to_lanes(ab, qb), _stack_heads(v_ref[rows, :], masks),
                                preferred_element_type=F32)
            return run + cs[:, 0:1], acc

        state = block(i, (jnp.zeros((2 * qb, 1), F32), jnp.zeros((qb, LANES), F32)), True)
        state = lax.while_loop(
            lambda st: (st[0] >= 0) & (jnp.max(st[1]) > UNDERFLOW_LOG),
            lambda st: (st[0] - 1, *block(st[0], st[1:], False)),
            (i - 1, *state))
        o_ref[...] = state[2]
        late_phases()

    s_in, s_out, s_shapes, aliases, s_scratch = _side_specs(side, 4, 1)
    res = pl.pallas_call(
        body, name=name, grid=(hp, nq),
        in_specs=[pl.BlockSpec((qb, LANES), lambda p, i: (i, q0 + p)),
                  pl.BlockSpec((s, LANES), lambda p, i: (0, k0 + p)),
                  pl.BlockSpec((s, LANES), lambda p, i: (0, v0 + p)),
                  pl.BlockSpec((qb, qb), lambda p, i: (0, 0))] + s_in,
        out_specs=[pl.BlockSpec((qb, LANES), lambda p, i: (i, p))] + s_out,
        out_shape=[_sds((s, hp * LANES), F32)] + s_shapes,
        input_output_aliases=aliases, scratch_shapes=s_scratch,
        compiler_params=_params(("arbitrary", "arbitrary")),
    )(proj, proj, proj, tri, *([] if side is None else side.operands))
    return res[0], res[1:]


def _mix_out(yc, o, proj, x, wpc, wpa, wout, g, gt, *, name, tm=256):
    s, d = x.shape
    cw = yc.shape[1]
    tm = min(tm, s)
    ga_blk = (3 * cw + 3 * (d // 2)) // d

    def body(yc_ref, o_ref, ga_ref, gb_ref, x_ref, wpc_ref, wpa_ref, wout_ref, g_ref, gt_ref,
             ycv_ref, yat_ref, mg_ref, mix_ref, x1_ref):
        y_conv = jnp.dot(yc_ref[...], wpc_ref[...], preferred_element_type=F32)
        y_attn = jnp.dot(o_ref[...].astype(BF16), wpa_ref[...], preferred_element_type=F32)
        merged = (_sigmoid(ga_ref[...].astype(F32)) * y_conv + _sigmoid(gb_ref[...].astype(F32)) * y_attn)
        mg = merged.astype(BF16)
        mix = jnp.dot(mg, wout_ref[...], preferred_element_type=F32)
        r = lax.rsqrt(jnp.mean(mix * mix, axis=-1, keepdims=True) + EPS)
        ycv_ref[...] = y_conv.astype(BF16)
        yat_ref[...] = y_attn.astype(BF16)
        mg_ref[...] = mg
        mix_ref[...] = mix
        x1_ref[...] = x_ref[...] + gt_ref[...] * (mix * r * g_ref[...])

    def rows(w):
        return pl.BlockSpec((tm, w), lambda i: (i, 0))

    def full(a):
        return pl.BlockSpec(a.shape, lambda i: (0, 0))

    return pl.pallas_call(
        body, name=name, grid=(s // tm,),
        in_specs=[rows(cw), rows(d // 2), pl.BlockSpec((tm, d), lambda i: (i, ga_blk)),
                  pl.BlockSpec((tm, d), lambda i: (i, ga_blk + 1)), rows(d),
                  full(wpc), full(wpa), full(wout), _vec_spec(d, 1), _vec_spec(d, 1)],
        out_specs=[rows(d), rows(d), rows(d), rows(d), rows(d)],
        out_shape=[_sds((s, d), BF16), _sds((s, d), BF16), _sds((s, d), BF16), _sds((s, d), F32), _sds((s, d), F32)],
        compiler_params=_params(("parallel",)),
    )(yc, o, proj, proj, x, wpc, wpa, wout, g, gt)


def _relu2(a):
    r = jnp.maximum(a.astype(F32), 0.0)
    return (r * r).astype(BF16)


def _mlp_out(a, x, w2, g, gt, *, name, tm=512):
    s, d = x.shape
    dff = a.shape[1]
    tm = min(tm, s)

    def body(a_ref, x_ref, w_ref, g_ref, gt_ref, ff_ref, x2_ref):
        ff = jnp.dot(_relu2(a_ref[...]), w_ref[...], preferred_element_type=F32)
        r = lax.rsqrt(jnp.mean(ff * ff, axis=-1, keepdims=True) + EPS)
        ff_ref[...] = ff
        x2_ref[...] = x_ref[...] + gt_ref[...] * (ff * r * g_ref[...])

    return pl.pallas_call(
        body, name=name, grid=(s // tm,),
        in_specs=[pl.BlockSpec((tm, dff), lambda i: (i, 0)), pl.BlockSpec((tm, d), lambda i: (i, 0)),
                  pl.BlockSpec((dff, d), lambda i: (0, 0)), _vec_spec(d, 1), _vec_spec(d, 1)],
        out_specs=[pl.BlockSpec((tm, d), lambda i: (i, 0)), pl.BlockSpec((tm, d), lambda i: (i, 0))],
        out_shape=[_sds((s, d), F32), _sds((s, d), F32)],
        compiler_params=_params(("parallel",)),
    )(a, x, w2, g, gt)


def _loss_grad(y, target, *, name, tm=512):
    s, d = y.shape
    tm = min(tm, s)

    def body(y_ref, t_ref, dy_ref, loss_ref):
        @pl.when(pl.program_id(0) == 0)
        def _():
            loss_ref[...] = jnp.zeros_like(loss_ref)
        e = y_ref[...] - t_ref[...]
        dy_ref[...] = e * (1.0 / d)
        loss_ref[...] += 0.5 * jnp.sum(jnp.mean(e * e, axis=-1, keepdims=True), axis=0, keepdims=True)

    return pl.pallas_call(
        body, name=name, grid=(s // tm,),
        in_specs=[pl.BlockSpec((tm, d), lambda i: (i, 0)), pl.BlockSpec((tm, d), lambda i: (i, 0))],
        out_specs=[pl.BlockSpec((tm, d), lambda i: (i, 0)), pl.BlockSpec((1, 1), lambda i: (0, 0))],
        out_shape=[_sds((s, d), F32), _sds((1, 1), F32)],
        compiler_params=_params(("arbitrary",)),
    )(y, target)


def _mlp_out_bwd(dx, ff, a, w2, g, gt, *, name, tm=256):
    s, d = dx.shape
    dff = a.shape[1]
    tm = min(tm, s)

    def body(dx_ref, ff_ref, a_ref, w_ref, g_ref, gt_ref, dff_ref, da_ref, dgt_ref, dg_ref):
        @pl.when(pl.program_id(0) == 0)
        def _():
            dgt_ref[...] = jnp.zeros_like(dgt_ref)
            dg_ref[...] = jnp.zeros_like(dg_ref)
        dxv = dx_ref[...]
        dn = dxv * gt_ref[...]
        dffv, xh = _rms_bwd(dn, ff_ref[...], g_ref[...])
        dgt_ref[...] += _colsum(dxv * (xh * g_ref[...]))
        dg_ref[...] += _colsum(dn * xh)
        dffb = dffv.astype(BF16)
        dff_ref[...] = dffb
        drr = lax.dot_general(dffb, w_ref[...], _NT, preferred_element_type=F32)
        da_ref[...] = (drr * (2.0 * jnp.maximum(a_ref[...].astype(F32), 0.0))).astype(BF16)

    return pl.pallas_call(
        body, name=name, grid=(s // tm,),
        in_specs=[pl.BlockSpec((tm, d), lambda i: (i, 0)), pl.BlockSpec((tm, d), lambda i: (i, 0)),
                  pl.BlockSpec((tm, dff), lambda i: (i, 0)), pl.BlockSpec((dff, d), lambda i: (0, 0)),
                  _vec_spec(d, 1), _vec_spec(d, 1)],
        out_specs=[pl.BlockSpec((tm, d), lambda i: (i, 0)), pl.BlockSpec((tm, dff), lambda i: (i, 0)),
                   _vec_spec(d, 1), _vec_spec(d, 1)],
        out_shape=[_sds((s, d), BF16), _sds((s, dff), BF16), _sds((1, d), F32), _sds((1, d), F32)],
        compiler_params=_params(("arbitrary",)),
    )(dx, ff, a, w2, g, gt)


def _matmul_nt_norm_bwd(dy, w, x, dres, g, sc, *, name, tm=256, side=None):
    s, n = dy.shape
    d = w.shape[0]
    tm = _fit(tm, s)
    nt = s // tm

    def body(*refs):
        i = pl.program_id(0)
        own, late_phases = _host_side(side, 6, 4, 0, refs, i == 0, i == (3 * nt) // 4, i == nt - 1)
        dy_ref, w_ref, x_ref, dres_ref, g_ref, sc_ref, dx_ref, dsh_ref, dsc_ref, dg_ref = own

        @pl.when(i == 0)
        def _():
            dsh_ref[...] = jnp.zeros_like(dsh_ref)
            dsc_ref[...] = jnp.zeros_like(dsc_ref)
            dg_ref[...] = jnp.zeros_like(dg_ref)

        dh = lax.dot_general(dy_ref[...], w_ref[...], _NT, preferred_element_type=F32)
        dn = dh * (1.0 + sc_ref[...])
        dxin, xh = _rms_bwd(dn, x_ref[...], g_ref[...])
        dsh_ref[...] += _colsum(dh)
        dsc_ref[...] += _colsum(dh * (xh * g_ref[...]))
        dg_ref[...] += _colsum(dn * xh)
        dx_ref[...] = dres_ref[...] + dxin
        late_phases()

    s_in, s_out, s_shapes, aliases, s_scratch = _side_specs(side, 6, 4)
    res = pl.pallas_call(
        body, name=name, grid=(nt,),
        in_specs=[pl.BlockSpec((tm, n), lambda i: (i, 0)), pl.BlockSpec((d, n), lambda i: (0, 0)),
                  pl.BlockSpec((tm, d), lambda i: (i, 0)), pl.BlockSpec((tm, d), lambda i: (i, 0)),
                  _vec_spec(d, 1), _vec_spec(d, 1)] + s_in,
        out_specs=[pl.BlockSpec((tm, d), lambda i: (i, 0)), _vec_spec(d, 1), _vec_spec(d, 1), _vec_spec(d, 1)] + s_out,
        out_shape=[_sds((s, d), F32), _sds((1, d), F32), _sds((1, d), F32), _sds((1, d), F32)] + s_shapes,
        input_output_aliases=aliases, scratch_shapes=s_scratch,
        compiler_params=_params(("arbitrary",)),
    )(dy, w, x, dres, g, sc, *([] if side is None else side.operands))
    return res[:4], res[4:]


def _matmul_tn(a, b, *, name, tk=1024, tn=1024, ts=512, relu2=False):
    s, k = a.shape
    n = b.shape[1]
    tk, tn, ts = _fit(tk, k), _fit(tn, n), _fit(ts, s)
    nt = s // ts

    def body(a_ref, b_ref, o_ref, acc):
        t = pl.program_id(2)

        @pl.when(t == 0)
        def _():
            acc[...] = jnp.zeros_like(acc)
        av = a_ref[...]
        if relu2:
            av = _relu2(av)
        acc[...] += lax.dot_general(av, b_ref[...], _TN, preferred_element_type=F32)

        @pl.when(t == nt - 1)
        def _():
            o_ref[...] = acc[...].astype(BF16)

    return pl.pallas_call(
        body, name=name, grid=(k // tk, n // tn, nt),
        in_specs=[pl.BlockSpec((ts, tk), lambda i, j, t: (t, i)), pl.BlockSpec((ts, tn), lambda i, j, t: (t, j))],
        out_specs=pl.BlockSpec((tk, tn), lambda i, j, t: (i, j)),
        out_shape=_sds((k, n), BF16),
        scratch_shapes=[pltpu.VMEM((tk, tn), F32)],
        compiler_params=_params(("parallel", "parallel", "arbitrary")),
    )(a, b)


def _mix_out_bwd(dx, mix, proj, ycv, yat, wout, wpc, wpa, g, gt, *, name, tm=256):
    s, d = dx.shape
    cw = wpc.shape[0]
    aw = wpa.shape[0]
    tm = min(tm, s)
    ga_blk = (3 * cw + 3 * aw) // d

    def body(dx_ref, mix_ref, ga_ref, gb_ref, ycv_ref, yat_ref, wout_ref, wpc_ref, wpa_ref, g_ref, gt_ref,
             dmix_ref, dycv_ref, dyat_ref, dyc_ref, do_ref, dga_ref, dgb_ref, dgt_ref, dg_ref):
        @pl.when(pl.program_id(0) == 0)
        def _():
            dgt_ref[...] = jnp.zeros_like(dgt_ref)
            dg_ref[...] = jnp.zeros_like(dg_ref)
        dxv = dx_ref[...]
        dn = dxv * gt_ref[...]
        dmix, xh = _rms_bwd(dn, mix_ref[...], g_ref[...])
        dgt_ref[...] += _colsum(dxv * (xh * g_ref[...]))
        dg_ref[...] += _colsum(dn * xh)
        dmixb = dmix.astype(BF16)
        dmix_ref[...] = dmixb
        dmerged = lax.dot_general(dmixb, wout_ref[...], _NT, preferred_element_type=F32)
        sga = _sigmoid(ga_ref[...].astype(F32))
        sgb = _sigmoid(gb_ref[...].astype(F32))
        dycv = (dmerged * sga).astype(BF16)
        dyat = (dmerged * sgb).astype(BF16)
        dycv_ref[...] = dycv
        dyat_ref[...] = dyat
        dga_ref[...] = (dmerged * ycv_ref[...].astype(F32) * (sga * (1.0 - sga))).astype(BF16)
        dgb_ref[...] = (dmerged * yat_ref[...].astype(F32) * (sgb * (1.0 - sgb))).astype(BF16)
        dyc_ref[...] = lax.dot_general(dycv, wpc_ref[...], _NT, preferred_element_type=F32).astype(BF16)
        do_ref[...] = lax.dot_general(dyat, wpa_ref[...], _NT, preferred_element_type=F32).astype(BF16)

    def rows(w):
        return pl.BlockSpec((tm, w), lambda i: (i, 0))

    def full(a):
        return pl.BlockSpec(a.shape, lambda i: (0, 0))

    return pl.pallas_call(
        body, name=name, grid=(s // tm,),
        in_specs=[rows(d), rows(d), pl.BlockSpec((tm, d), lambda i: (i, ga_blk)),
                  pl.BlockSpec((tm, d), lambda i: (i, ga_blk + 1)), rows(d), rows(d),
                  full(wout), full(wpc), full(wpa), _vec_spec(d, 1), _vec_spec(d, 1)],
        out_specs=[rows(d), rows(d), rows(d), rows(cw), rows(aw), rows(d), rows(d), _vec_spec(d, 1), _vec_spec(d, 1)],
        out_shape=[_sds((s, d), BF16), _sds((s, d), BF16), _sds((s, d), BF16), _sds((s, cw), BF16),
                   _sds((s, aw), BF16), _sds((s, d), BF16), _sds((s, d), BF16), _sds((1, d), F32), _sds((1, d), F32)],
        compiler_params=_params(("arbitrary",)),
    )(dx, mix, proj, proj, ycv, yat, wout, wpc, wpa, g, gt)


def _conv_bwd(dyc, proj, conv_w, *, name, tm=512):
    s = proj.shape[0]
    cw = conv_w.shape[1]
    tm = min(tm, s)
    nb = tm // HALO
    nt = s // tm
    last_blk = s // HALO - 1

    def body(dyc_ref, bg_ref, cg_ref, u_ref, cgh_ref, uh_ref, dych_ref, bgh_ref, w_ref,
             dbg_ref, dcg_ref, du_ref, dw_ref, vbuf, gbuf):
        i = pl.program_id(0)

        @pl.when(i == 0)
        def _():
            dw_ref[...] = jnp.zeros_like(dw_ref)

        cg = cg_ref[...].astype(F32)
        u = u_ref[...].astype(F32)
        vv = cg * u
        halo = cgh_ref[...].astype(F32) * uh_ref[...].astype(F32)
        vbuf[0:HALO, :] = jnp.where(i > 0, halo, 0.0)
        vbuf[HALO:HALO + tm, :] = vv
        v1 = vbuf[HALO - 1:HALO - 1 + tm, :]
        v2 = vbuf[HALO - 2:HALO - 2 + tm, :]
        w = w_ref[...]
        y = w[2:3, :] * vv + w[1:2, :] * v1 + w[0:1, :] * v2
        dyc = dyc_ref[...].astype(F32)
        dbg_ref[...] = (dyc * y).astype(BF16)
        gy = dyc * bg_ref[...].astype(F32)
        nxt = dych_ref[...].astype(F32) * bgh_ref[...].astype(F32)
        gbuf[0:tm, :] = gy
        gbuf[tm:tm + HALO, :] = jnp.where(i < nt - 1, nxt, 0.0)
        g1 = gbuf[1:1 + tm, :]
        g2 = gbuf[2:2 + tm, :]
        dvv = w[2:3, :] * gy + w[1:2, :] * g1 + w[0:1, :] * g2
        dcg_ref[...] = (dvv * u).astype(BF16)
        du_ref[...] = (dvv * cg).astype(BF16)
        dw_ref[0:1, :] += _colsum(gy * v2)
        dw_ref[1:2, :] += _colsum(gy * v1)
        dw_ref[2:3, :] += _colsum(gy * vv)

    def prev(i):
        return jnp.maximum(i * nb - 1, 0)

    def nxt_blk(i):
        return jnp.minimum((i + 1) * nb, last_blk)

    def col(c):
        return pl.BlockSpec((tm, cw), lambda i: (i, c))

    return pl.pallas_call(
        body, name=name, grid=(nt,),
        in_specs=[col(0), col(0), col(1), col(2),
                  pl.BlockSpec((HALO, cw), lambda i: (prev(i), 1)), pl.BlockSpec((HALO, cw), lambda i: (prev(i), 2)),
                  pl.BlockSpec((HALO, cw), lambda i: (nxt_blk(i), 0)), pl.BlockSpec((HALO, cw), lambda i: (nxt_blk(i), 0)),
                  pl.BlockSpec((3, cw), lambda i: (0, 0))],
        out_specs=[col(0), col(0), col(0), pl.BlockSpec((3, cw), lambda i: (0, 0))],
        out_shape=[_sds((s, cw), BF16), _sds((s, cw), BF16), _sds((s, cw), BF16), _sds((3, cw), F32)],
        scratch_shapes=[pltpu.VMEM((HALO + tm, cw), F32), pltpu.VMEM((tm + HALO, cw), F32)],
        compiler_params=_params(("arbitrary",)),
    )(dyc, proj, proj, proj, proj, proj, dyc, proj, conv_w)


def _attn_bwd(proj, o, do, tri, *, d, name, qb=256, side=None):
    s = proj.shape[0]
    qb = min(qb, s)
    nq = s // qb
    q0, k0, v0, hp = _attn_cols(d)

    def body(*refs):
        p, i = pl.program_id(0), pl.program_id(1)
        own, late_phases = _host_side(
            side, 6, 3, 2, refs, (p == 0) & (i == 0), (p == hp - 1) & (i == 0), (p == hp - 1) & (i == nq - 1))
        q_ref, k_ref, v_ref, o_ref, do_ref, tri_ref, dq_ref, dk_ref, dv_ref, dk_acc, dv_acc = own

        @pl.when(i == 0)
        def _():
            dk_acc[...] = jnp.zeros_like(dk_acc)
            dv_acc[...] = jnp.zeros_like(dv_acc)

        causal = _stacked_causal(qb)
        tri_m = tri_ref[...]
        dov = do_ref[...]
        masks = [_head_mask(h) for h in range(2)]
        qs = _stack_heads(q_ref[...] * ATTN_SCALE, masks)
        dos = _stack_heads(dov, masks)
        dprod = dov.astype(F32) * o_ref[...]
        dtot = jnp.concatenate([jnp.sum(jnp.where(m, dprod, 0.0), axis=-1, keepdims=True) for m in masks], axis=0)

        def block(j, state, diag):
            run, grun, dq_acc = state
            rows = pl.ds(pl.multiple_of(j * qb, qb), qb)
            kb = k_ref[rows, :]
            z = lax.dot_general(qs, kb, _NT, preferred_element_type=F32)
            lg = _log_one_minus_sigmoid(z)
            beta = 1.0 - jnp.exp(lg)
            if diag:
                lg = jnp.where(causal, lg, 0.0)
            cs = _running_sum(lg, tri_m)
            a = jnp.exp(z + cs + run)
            if diag:
                a = jnp.where(causal, a, 0.0)
            ab = a.astype(BF16)
            da = lax.dot_general(dos, v_ref[rows, :], _NT, preferred_element_type=F32)
            gg = ab.astype(F32) * da
            gcs = _running_sum(gg, tri_m)
            left = jnp.where(run > UNDERFLOW_LOG, dtot - grun, 0.0)
            dz = gg - beta * (gg + (left - gcs))
            if diag:
                dz = jnp.where(causal, dz, 0.0)
            dzb = dz.astype(BF16)
            dq_acc = dq_acc + jnp.dot(_heads_to_lanes(dzb, qb), _stack_heads(kb, masks), preferred_element_type=F32)
            dk_acc[rows, :] += lax.dot_general(dzb, qs, _TN, preferred_element_type=F32)
            dv_acc[rows, :] += lax.dot_general(ab, dos, _TN, preferred_element_type=F32)
            return run + cs[:, 0:1], grun + gcs[:, 0:1], dq_acc

        zero = jnp.zeros((2 * qb, 1), F32)
        state = block(i, (zero, zero, jnp.zeros((qb, LANES), F32)), True)
        state = lax.while_loop(
            lambda st: (st[0] >= 0) & (jnp.max(st[1]) > UNDERFLOW_LOG),
            lambda st: (st[0] - 1, *block(st[0], st[1:], False)),
            (i - 1, *state))
        dq_ref[...] = (state[3] * ATTN_SCALE).astype(BF16)

        @pl.when(i == nq - 1)
        def _():
            dk_ref[...] = dk_acc[...].astype(BF16)
            dv_ref[...] = dv_acc[...].astype(BF16)

        late_phases()

    aw = hp * LANES
    s_in, s_out, s_shapes, aliases, s_scratch = _side_specs(side, 6, 3)
    res = pl.pallas_call(
        body, name=name, grid=(hp, nq),
        in_specs=[pl.BlockSpec((qb, LANES), lambda p, i: (i, q0 + p)),
                  pl.BlockSpec((s, LANES), lambda p, i: (0, k0 + p)),
                  pl.BlockSpec((s, LANES), lambda p, i: (0, v0 + p)),
                  pl.BlockSpec((qb, LANES), lambda p, i: (i, p)),
                  pl.BlockSpec((qb, LANES), lambda p, i: (i, p)),
                  pl.BlockSpec((qb, qb), lambda p, i: (0, 0))] + s_in,
        out_specs=[pl.BlockSpec((qb, LANES), lambda p, i: (i, p)),
                   pl.BlockSpec((s, LANES), lambda p, i: (0, p)),
                   pl.BlockSpec((s, LANES), lambda p, i: (0, p))] + s_out,
        out_shape=[_sds((s, aw), BF16), _sds((s, aw), BF16), _sds((s, aw), BF16)] + s_shapes,
        input_output_aliases=aliases,
        scratch_shapes=[pltpu.VMEM((s, LANES), F32), pltpu.VMEM((s, LANES), F32)] + s_scratch,
        compiler_params=_params(("arbitrary", "arbitrary")),
    )(proj, proj, proj, o, do, tri, *([] if side is None else side.operands))
    return res[:3], res[3:]


def _hosted(hooks, kind, l, fn, *args, **kw):
    res, side_out = fn(*args, side=hooks.side(kind, l), **kw)
    hooks.done(kind, l, side_out)
    return res


def _layer_fwd(x, mod, gains, conv_w, tri, *, l, hooks):
    sh1, sc1, gt1, sh2, sc2, gt2 = mod
    g_pre_mix, g_post_mix, g_pre_mlp, g_post_mlp = gains
    d = x.shape[1]
    w = functools.partial(hooks.weight, l)
    h, proj = _hosted(hooks, "in_proj", l, _norm_mod_matmul, x, g_pre_mix, sc1, sh1, w("w_in"), name=f"in_proj_{l}")
    yc = _conv_fwd(proj, conv_w, name=f"conv_fwd_{l}")
    o = _hosted(hooks, "attn_fwd", l, _attn_fwd, proj, tri, d=d, name=f"attn_fwd_{l}")
    ycv, yat, merged, mix, x1 = _mix_out(yc, o, proj, x, w("w_proj_conv"), w("w_proj_attn"), w("w_out"),
                                         g_post_mix, gt1, name=f"mix_out_{l}")
    (h2, a), _ = _norm_mod_matmul(x1, g_pre_mlp, sc2, sh2, w("w_mlp_in"), name=f"mlp_in_{l}")
    ff, x2 = _mlp_out(a, x1, w("w_mlp_out"), g_post_mlp, gt2, name=f"mlp_out_{l}")
    saved = dict(x=x, h=h, proj=proj, yc=yc, o=o, ycv=ycv, yat=yat, merged=merged, mix=mix, x1=x1, h2=h2, a=a, ff=ff,
                 conv_w=conv_w, **{k: w(k) for k in BIG})
    return x2, saved


def _layer_bwd(dx2, sv, mod, gains, tri, *, l, hooks):
    sh1, sc1, gt1, sh2, sc2, gt2 = mod
    g_pre_mix, g_post_mix, g_pre_mlp, g_post_mlp = gains
    d = dx2.shape[1]
    dff, da, dgt2, dg_post_mlp = _mlp_out_bwd(dx2, sv["ff"], sv["a"], sv["w_mlp_out"], g_post_mlp, gt2,
                                              name=f"mlp_out_bwd_{l}")
    hooks.grad(l, "w_mlp_out", _matmul_tn(sv["a"], dff, relu2=True, name=f"gw_mlp_out_{l}"))
    dx1, dsh2, dsc2, dg_pre_mlp = _hosted(hooks, "mlp_in_bwd", l, _matmul_nt_norm_bwd, da, sv["w_mlp_in"], sv["x1"], dx2,
                                          g_pre_mlp, sc2, name=f"mlp_in_bwd_{l}")
    hooks.grad(l, "w_mlp_in", _matmul_tn(sv["h2"], da, name=f"gw_mlp_in_{l}"))
    dmix, dycv, dyat, dyc, do, dga, dgb, dgt1, dg_post_mix = _mix_out_bwd(
        dx1, sv["mix"], sv["proj"], sv["ycv"], sv["yat"], sv["w_out"], sv["w_proj_conv"], sv["w_proj_attn"],
        g_post_mix, gt1, name=f"mix_out_bwd_{l}")
    hooks.grad(l, "w_out", _matmul_tn(sv["merged"], dmix, name=f"gw_out_{l}"))
    hooks.grad(l, "w_proj_conv", _matmul_tn(sv["yc"], dycv, name=f"gw_proj_conv_{l}"))
    hooks.grad(l, "w_proj_attn", _matmul_tn(sv["o"].astype(BF16), dyat, name=f"gw_proj_attn_{l}"))
    dbg, dcg, du, g_conv_w = _conv_bwd(dyc, sv["proj"], sv["conv_w"], name=f"conv_bwd_{l}")
    dq, dk, dv = _hosted(hooks, "attn_bwd", l, _attn_bwd, sv["proj"], sv["o"], do, tri, d=d, name=f"attn_bwd_{l}")
    dproj = jnp.concatenate([dbg, dcg, du, dq, dk, dv, dga, dgb], axis=1)
    (dx0, dsh1, dsc1, dg_pre_mix), _ = _matmul_nt_norm_bwd(dproj, sv["w_in"], sv["x"], dx1, g_pre_mix, sc1,
                                                           name=f"in_proj_bwd_{l}")
    hooks.grad(l, "w_in", _matmul_tn(sv["h"], dproj, name=f"gw_in_{l}"))
    dmod = jnp.concatenate([dsh1, dsc1, dgt1, dsh2, dsc2, dgt2], axis=0)
    dgains = jnp.concatenate([dg_pre_mix, dg_post_mix, dg_pre_mlp, dg_post_mlp], axis=0)
    return dx0, g_conv_w, dmod, dgains


BIG = ("w_in", "w_proj_conv", "w_proj_attn", "w_out", "w_mlp_in", "w_mlp_out")
SHARD_AXIS = dict(w_in=1, w_proj_conv=1, w_proj_attn=1, w_out=0, w_mlp_in=1, w_mlp_out=0)


class _LocalWeights:
    def __init__(self, wlayers):
        self.wlayers = wlayers
        self.grads = {}

    def weight(self, l, name):
        return self.wlayers[l][name]

    def side(self, kind, l):
        return None

    def done(self, kind, l, outs):
        pass

    def grad(self, l, name, g):
        self.grads[(l, name)] = g


def _local_step(x, target, mods, gains, conv_w, hooks):
    depth = mods.shape[0]
    tri = _tri(min(256, x.shape[0]))
    saved = []
    for l in range(depth):
        mod = [mods[l, k:k + 1] for k in range(N_MOD)]
        gl = [gains[l, k:k + 1] for k in range(4)]
        x, sv = _layer_fwd(x, mod, gl, conv_w[l], tri, l=l, hooks=hooks)
        saved.append((sv, mod, gl))
    dx, loss = _loss_grad(x, target, name="loss_grad")
    dconv, dmods, dgains = [None] * depth, [None] * depth, [None] * depth
    for l in reversed(range(depth)):
        sv, mod, gl = saved[l]
        dx, dconv[l], dmods[l], dgains[l] = _layer_bwd(dx, sv, mod, gl, tri, l=l, hooks=hooks)
    return loss, dx, jnp.stack(dconv), jnp.stack(dmods), jnp.stack(dgains)


def _coords():
    return lax.axis_index("x"), lax.axis_index("y"), lax.axis_index("c")


def _flip(v, f):
    return 1 - v if f else v


def _all_gather_small(v, *, name):
    r, c_ = v.shape

    def body(v_ref, out_ref, send_sems, recv_sems, local_sem):
        x, y, c = _coords()
        me = 4 * x + 2 * y + c
        mine = pltpu.make_async_copy(v_ref, out_ref.at[me], local_sem)
        mine.start()
        copies = []
        for k in range(1, 8):
            fx, fy, fc = (k >> 2) & 1, (k >> 1) & 1, k & 1
            px, py, pc = _flip(x, fx), _flip(y, fy), _flip(c, fc)
            out = pltpu.make_async_remote_copy(src_ref=v_ref, dst_ref=out_ref.at[me], send_sem=send_sems.at[k - 1],
                                               recv_sem=recv_sems.at[k - 1], device_id=(px, py, pc), device_id_type=MESH)
            out.start()
            back = pltpu.make_async_remote_copy(src_ref=v_ref, dst_ref=out_ref.at[4 * px + 2 * py + pc],
                                                send_sem=send_sems.at[k - 1], recv_sem=recv_sems.at[k - 1],
                                                device_id=(px, py, pc), device_id_type=MESH)
            copies.append((out, back))
        for out, back in copies:
            back.wait_recv()
        for out, back in copies:
            out.wait_send()
        mine.wait()

    return pl.pallas_call(
        body, name=name,
        in_specs=[pl.BlockSpec(memory_space=pltpu.VMEM)],
        out_specs=pl.BlockSpec(memory_space=pltpu.VMEM),
        out_shape=_sds((8, r, c_), F32),
        scratch_shapes=[pltpu.SemaphoreType.DMA((7,)), pltpu.SemaphoreType.DMA((7,)), pltpu.SemaphoreType.DMA],
    )(v)


def _shard_dims(full_shape, axis):
    k, n = full_shape
    return (k // 4, n) if axis == 0 else (k, n // 4)


def _shard_window(ref, axis, chip, half, rows, cols):
    r0, rn = (0, rows) if half is None else (half * (rows // 2), rows // 2)
    if axis == 1:
        return ref.at[pl.ds(r0, rn), pl.ds(chip * cols, cols)]
    return ref.at[pl.ds(chip * rows + r0, rn), :]


def _cast_place(w, layer, axis, chip_arr, *, name, tr=256):
    _, rows, cols = w.shape
    tr = _fit(tr, rows)
    nb = rows // tr
    full = (rows * 4, cols) if axis == 0 else (rows, cols * 4)

    def body(chip_ref, w_ref, o_ref):
        o_ref[...] = w_ref[0].astype(BF16)

    if axis == 1:
        out_map = lambda i, chip: (i, chip[0])
    else:
        out_map = lambda i, chip: (chip[0] * nb + i, 0)
    grid_spec = pltpu.PrefetchScalarGridSpec(
        num_scalar_prefetch=1, grid=(nb,),
        in_specs=[pl.BlockSpec((1, tr, cols), lambda i, chip: (layer, i, 0))],
        out_specs=pl.BlockSpec((tr, cols), out_map))
    return pl.pallas_call(body, name=name, grid_spec=grid_spec, out_shape=_sds(full, BF16),
                          compiler_params=_params(("arbitrary",)))(chip_arr, w)


def _gather_side(fulls, axes):
    n = len(fulls)

    def copies(outs, sems):
        send_sems, recv_sems = sems
        x, y, c = _coords()
        chip = 2 * x + y
        sibling = (x, y, 1 - c)
        table = []
        for w in range(n):
            rows, cols = _shard_dims(outs[w].shape, axes[w])
            win = functools.partial(_shard_window, outs[w], axes[w], rows=rows, cols=cols)
            for j, (fx, fy) in enumerate(OTHER_CHIPS):
                px, py = _flip(x, fx), _flip(y, fy)
                pchip = 2 * px + py

                def copy(piece, sem, to):
                    return pltpu.make_async_remote_copy(src_ref=piece, dst_ref=piece, send_sem=send_sems.at[w, sem],
                                                        recv_sem=recv_sems.at[w, sem], device_id=to, device_id_type=MESH)

                table.append((copy(win(chip, c), j, (px, py, c)), copy(win(pchip, c), j, (px, py, c)),
                              copy(win(pchip, c), 3 + j, sibling), copy(win(pchip, 1 - c), 3 + j, sibling)))
        return table

    def start(ins, outs, sems):
        for send, _, _, _ in copies(outs, sems):
            send.start()

    def mid(ins, outs, sems):
        for _, landed, pass_on, _ in copies(outs, sems):
            landed.wait_recv()
            pass_on.start()

    def finish(ins, outs, sems):
        table = copies(outs, sems)
        for _, _, _, from_sibling in table:
            from_sibling.wait_recv()
        for send, _, pass_on, _ in table:
            send.wait_send()
            pass_on.wait_send()

    return _Side(fulls, [_sds(f.shape, f.dtype) for f in fulls], {w: w for w in range(n)},
                 [pltpu.SemaphoreType.DMA((n, 6)), pltpu.SemaphoreType.DMA((n, 6))], start, mid, finish)


def _exchange_side(grads, axes):
    n = len(grads)
    out_shapes = []
    for g, ax in zip(grads, axes):
        rows, cols = _shard_dims(g.shape, ax)
        out_shapes.append(_sds((7, rows // 2, cols), g.dtype))

    def copies(ins, outs, sems):
        send_sems, recv_sems = sems
        x, y, c = _coords()
        table = []
        for w in range(n):
            rows, cols = _shard_dims(ins[w].shape, axes[w])
            for k in range(1, 8):
                fx, fy, fc = (k >> 2) & 1, (k >> 1) & 1, k & 1
                px, py, pc = _flip(x, fx), _flip(y, fy), _flip(c, fc)
                piece = _shard_window(ins[w], axes[w], 2 * px + py, pc, rows, cols)
                table.append(pltpu.make_async_remote_copy(
                    src_ref=piece, dst_ref=outs[w].at[k - 1], send_sem=send_sems.at[w, k - 1],
                    recv_sem=recv_sems.at[w, k - 1], device_id=(px, py, pc), device_id_type=MESH))
        return table

    def start(ins, outs, sems):
        for cp in copies(ins, outs, sems):
            cp.start()

    def finish(ins, outs, sems):
        table = copies(ins, outs, sems)
        for cp in table:
            cp.wait_recv()
        for cp in table:
            cp.wait_send()

    return _Side(grads, out_shapes, {}, [pltpu.SemaphoreType.DMA((n, 7)), pltpu.SemaphoreType.DMA((n, 7))],
                 start, None, finish)


def _rs_sum_join(g, got, out_prev, layer, depth, axis, ids, *, name, tr=256):
    _, rows2, cols = got.shape
    tr = _fit(tr, rows2)
    nt = rows2 // tr
    if axis == 1:
        own_map = lambda i, ids_: (ids_[1] * nt + i, ids_[0])
    else:
        own_map = lambda i, ids_: ((ids_[0] * 2 + ids_[1]) * nt + i, 0)

    def body(ids_ref, g_ref, got_ref, *rest):
        out_ref, buf, local_sems, send_sems, recv_sem = rest[-5:]
        i = pl.program_id(0)
        x, y, c = _coords()
        sibling = (x, y, 1 - c)

        def copies(step, slot):
            rows_mine = pl.ds(c * rows2 + step * tr, tr)
            dst = out_ref.at[layer, rows_mine, :]
            keep = pltpu.make_async_copy(buf.at[slot], dst, local_sems.at[slot])
            give = pltpu.make_async_remote_copy(src_ref=buf.at[slot], dst_ref=dst, send_sem=send_sems.at[slot],
                                                recv_sem=recv_sem, device_id=sibling, device_id_type=MESH)
            return keep, give

        def drain(step, slot):
            keep, give = copies(step, slot)
            keep.wait()
            give.wait_send()

        slot = i % 2

        @pl.when(i >= 2)
        def _():
            drain(i - 2, slot)

        acc = g_ref[...].astype(F32)
        for k in range(7):
            acc = acc + got_ref[k].astype(F32)
        buf[slot] = acc
        keep, give = copies(i, slot)
        keep.start()
        give.start()

        @pl.when(i == nt - 1)
        def _():
            if nt >= 2:
                drain(nt - 2, (nt - 2) % 2)
            drain(nt - 1, (nt - 1) % 2)
            theirs = out_ref.at[layer, pl.ds((1 - c) * rows2, rows2), :]
            pltpu.make_async_remote_copy(src_ref=theirs, dst_ref=theirs, send_sem=send_sems.at[0], recv_sem=recv_sem,
                                         device_id=sibling, device_id_type=MESH).wait_recv()

    hbm = pl.BlockSpec(memory_space=pltpu.HBM)
    in_specs = [pl.BlockSpec((tr, cols), own_map), pl.BlockSpec((7, tr, cols), lambda i, ids_: (0, i, 0))]
    operands = [ids, g, got]
    aliases = {}
    if out_prev is not None:
        in_specs.append(hbm)
        operands.append(out_prev)
        aliases = {3: 0}
    grid_spec = pltpu.PrefetchScalarGridSpec(
        num_scalar_prefetch=1, grid=(nt,), in_specs=in_specs, out_specs=hbm,
        scratch_shapes=[pltpu.VMEM((2, tr, cols), F32), pltpu.SemaphoreType.DMA((2,)), pltpu.SemaphoreType.DMA((2,)),
                        pltpu.SemaphoreType.DMA])
    return pl.pallas_call(body, name=name, grid_spec=grid_spec, out_shape=_sds((depth, 2 * rows2, cols), F32),
                          input_output_aliases=aliases, compiler_params=_params(("arbitrary",)))(*operands)


MIX = ("w_proj_conv", "w_proj_attn", "w_out")


class _Schedule:
    def __init__(self, placed, depth, ids):
        self.placed, self.depth, self.ids = placed, depth, ids
        self.full, self.g, self.carried = {}, {}, None
        self.reduced = {k: None for k in BIG}
        first = [(0, "w_in")]
        self._landed(first, _side_call(self._gather(first), name="gather_w_in_0"))

    def _gather(self, keys):
        return _gather_side([self.placed[k] for k in keys], [SHARD_AXIS[k[1]] for k in keys])

    def _landed(self, keys, outs):
        for k, o in zip(keys, outs):
            self.full[k] = o

    def _exchange(self, keys):
        return _exchange_side([self.g[k] for k in keys], [SHARD_AXIS[k[1]] for k in keys])

    def _reduce(self, keys, got):
        for (l, name), pieces in zip(keys, got):
            self.reduced[name] = _rs_sum_join(self.g[(l, name)], pieces, self.reduced[name], l, self.depth,
                                              SHARD_AXIS[name], self.ids, name=f"rs_sum_join_{l}_{name}")

    def weight(self, l, name):
        return self.full[(l, name)]

    def grad(self, l, name, g):
        self.g[(l, name)] = g

    def side(self, kind, l):
        nxt = [(l + 1, "w_in")] if l + 1 < self.depth else []
        if kind == "in_proj":
            keys, make = [(l, k) for k in MIX + ("w_mlp_in",)], self._gather
        elif kind == "attn_fwd":
            keys, make = [(l, "w_mlp_out")] + nxt, self._gather
        elif kind == "attn_bwd":
            keys, make = [(l, k) for k in ("w_mlp_out", "w_mlp_in") + MIX], self._exchange
        else:
            keys, make = nxt, self._exchange
        self.carried = keys
        return make(keys) if keys else None

    def done(self, kind, l, outs):
        if self.carried:
            (self._landed if kind in ("in_proj", "attn_fwd") else self._reduce)(self.carried, outs)
        self.carried = None

    def finish(self):
        last = [(0, "w_in")]
        self._reduce(last, _side_call(self._exchange(last), name="rs_exchange_w_in_0"))


def _flat_rows(shape):
    rows = 1
    for s in shape[:-1]:
        rows *= s
    return rows, shape[-1]


def _row_tile(rows, cols, cap_bytes=2 * 1024 * 1024):
    t = rows
    while t * cols * 4 > cap_bytes and t % 16 == 0:
        t //= 2
    return t


def _ada_fwd(c_all, w_ada, b_loc, *, name, tn=512):
    l, d, nl = w_ada.shape
    b = c_all.shape[0]
    tn = min(tn, nl)

    def body(c_ref, w_ref, b_ref, o_ref):
        o_ref[0] = jnp.dot(c_ref[...], w_ref[0], preferred_element_type=F32,
                           precision=lax.Precision.HIGHEST) + b_ref[0]

    return pl.pallas_call(
        body, name=name, grid=(l, nl // tn),
        in_specs=[pl.BlockSpec((b, d), lambda i, j: (0, 0)), pl.BlockSpec((1, d, tn), lambda i, j: (i, 0, j)),
                  pl.BlockSpec((1, 1, tn), lambda i, j: (i, 0, j))],
        out_specs=pl.BlockSpec((1, b, tn), lambda i, j: (i, 0, j)),
        out_shape=_sds((l, b, nl), F32),
        compiler_params=_params(("parallel", "parallel")),
    )(c_all, w_ada, b_loc)


def _ada_bwd(c_t, dmod_loc, *, name, tn=512):
    d, b = c_t.shape
    l, _, nl = dmod_loc.shape
    tn = min(tn, nl)

    def body(c_ref, dm_ref, o_ref):
        cv = c_ref[...]
        dm = dm_ref[0]
        acc = cv[:, 0:1] * dm[0:1, :]
        for k in range(1, b):
            acc = acc + cv[:, k:k + 1] * dm[k:k + 1, :]
        o_ref[0] = acc

    return pl.pallas_call(
        body, name=name, grid=(l, nl // tn),
        in_specs=[pl.BlockSpec((d, b), lambda i, j: (0, 0)), pl.BlockSpec((1, b, tn), lambda i, j: (i, 0, j))],
        out_specs=pl.BlockSpec((1, d, tn), lambda i, j: (i, 0, j)),
        out_shape=_sds((l, d, nl), F32),
        compiler_params=_params(("parallel", "parallel")),
    )(c_t, dmod_loc)


def _sum_devices(p, *, name):
    k, r, c_ = p.shape

    def body(p_ref, o_ref):
        acc = p_ref[0]
        for j in range(1, k):
            acc = acc + p_ref[j]
        o_ref[...] = acc

    return pl.pallas_call(body, name=name, out_shape=_sds((r, c_), F32),
                          in_specs=[pl.BlockSpec(memory_space=pltpu.VMEM)],
                          out_specs=pl.BlockSpec(memory_space=pltpu.VMEM))(p)


def _adamw(w, g, m, v, *, name):
    shape = w.shape
    rows, cols = _flat_rows(shape)
    tr = _row_tile(rows, cols, cap_bytes=1024 * 1024)
    c1 = 1.0 / (1.0 - ADAM_B1 ** ADAM_STEP)
    c2 = 1.0 / (1.0 - ADAM_B2 ** ADAM_STEP)

    def body(w_ref, g_ref, m_ref, v_ref, d_ref, nm_ref, nv_ref):
        gv = g_ref[...]
        nm = ADAM_B1 * m_ref[...] + (1.0 - ADAM_B1) * gv
        nv = ADAM_B2 * v_ref[...] + (1.0 - ADAM_B2) * (gv * gv)
        m_hat = nm * c1
        v_hat = nv * c2
        d_ref[...] = -ADAM_LR * (m_hat / (jnp.sqrt(v_hat) + ADAM_EPS) + ADAM_WD * w_ref[...])
        nm_ref[...] = nm
        nv_ref[...] = nv

    spec = pl.BlockSpec((tr, cols), lambda i: (i, 0))
    flat = lambda a: a.reshape(rows, cols)
    outs = pl.pallas_call(body, name=name, grid=(rows // tr,), in_specs=[spec] * 4, out_specs=[spec] * 3,
                          out_shape=[_sds((rows, cols), F32)] * 3, compiler_params=_params(("parallel",)),
                          )(flat(w), flat(g), flat(m), flat(v))
    return tuple(o.reshape(shape) for o in outs)


WEIGHTS = ("w_ada", "b_ada", "g_pre_mix", "g_post_mix", "g_pre_mlp", "g_post_mlp", "w_in", "conv_w",
           "w_proj_conv", "w_proj_attn", "w_out", "w_mlp_in", "w_mlp_out")
GAINS = ("g_pre_mix", "g_post_mix", "g_pre_mlp", "g_post_mlp")


def kernel(x, c, w_ada, b_ada, g_pre_mix, g_post_mix, g_pre_mlp, g_post_mlp, w_in, conv_w, w_proj_conv, w_proj_attn, w_out, w_mlp_in, w_mlp_out, loss_target, m_w_ada, m_b_ada, m_g_pre_mix, m_g_post_mix, m_g_pre_mlp, m_g_post_mlp, m_w_in, m_conv_w, m_w_proj_conv, m_w_proj_attn, m_w_out, m_w_mlp_in, m_w_mlp_out, v_w_ada, v_b_ada, v_g_pre_mix, v_g_post_mix, v_g_pre_mlp, v_g_post_mlp, v_w_in, v_conv_w, v_w_proj_conv, v_w_proj_attn, v_w_out, v_w_mlp_in, v_w_mlp_out):
    params = dict(w_ada=w_ada, b_ada=b_ada, g_pre_mix=g_pre_mix, g_post_mix=g_post_mix, g_pre_mlp=g_pre_mlp,
                  g_post_mlp=g_post_mlp, w_in=w_in, conv_w=conv_w, w_proj_conv=w_proj_conv, w_proj_attn=w_proj_attn,
                  w_out=w_out, w_mlp_in=w_mlp_in, w_mlp_out=w_mlp_out)
    m_in = dict(w_ada=m_w_ada, b_ada=m_b_ada, g_pre_mix=m_g_pre_mix, g_post_mix=m_g_post_mix, g_pre_mlp=m_g_pre_mlp,
                g_post_mlp=m_g_post_mlp, w_in=m_w_in, conv_w=m_conv_w, w_proj_conv=m_w_proj_conv,
                w_proj_attn=m_w_proj_attn, w_out=m_w_out, w_mlp_in=m_w_mlp_in, w_mlp_out=m_w_mlp_out)
    v_in = dict(w_ada=v_w_ada, b_ada=v_b_ada, g_pre_mix=v_g_pre_mix, g_post_mix=v_g_post_mix, g_pre_mlp=v_g_pre_mlp,
                g_post_mlp=v_g_post_mlp, w_in=v_w_in, conv_w=v_conv_w, w_proj_conv=v_w_proj_conv,
                w_proj_attn=v_w_proj_attn, w_out=v_w_out, w_mlp_in=v_w_mlp_in, w_mlp_out=v_w_mlp_out)

    depth, d, nl_ada = w_ada.shape
    ix, iy, ic = lax.axis_index("x"), lax.axis_index("y"), lax.axis_index("c")
    chip = 2 * ix + iy
    me = 4 * ix + 2 * iy + ic
    xs = x[0]
    target = loss_target[0]

    c_all = _all_gather_small(jnp.broadcast_to(c, (8, d)), name="gather_c")[:, 0, :]
    b_loc = lax.dynamic_slice_in_dim(b_ada, chip * nl_ada, nl_ada, axis=1)[:, None, :]
    mod_loc = _ada_fwd(c_all, w_ada, b_loc, name="ada_fwd")
    mod_all = _all_gather_small(mod_loc.reshape(depth * 8, nl_ada), name="gather_mod")
    mod_all = mod_all.reshape(4, 2, depth, 8, nl_ada)[:, 0]
    mod_me = lax.dynamic_index_in_dim(mod_all, me, axis=2, keepdims=False)
    mods = jnp.transpose(mod_me, (1, 0, 2)).reshape(depth, N_MOD, d)

    chip_arr = jnp.reshape(chip, (1,)).astype(jnp.int32)
    ids = jnp.stack([chip, ic]).astype(jnp.int32)
    placed = {(l, k): _cast_place(params[k], l, SHARD_AXIS[k], chip_arr, name=f"place_{k}_{l}")
              for l in range(depth) for k in BIG}
    conv_full = _all_gather_small(
        jnp.pad(conv_w.reshape(depth * 3, -1), ((0, 8 - depth * 3), (0, 0))), name="gather_conv_w")
    conv_full = conv_full.reshape(4, 2, 8, -1)[:, 0, :depth * 3]
    conv_full = jnp.transpose(conv_full, (1, 0, 2)).reshape(depth, 3, -1)

    gains = jnp.stack([params[k] for k in GAINS], axis=1)
    schedule = _Schedule(placed, depth, ids)
    loss, dx, conv_grads, dmods, dgains = _local_step(xs, target, mods, gains, conv_full, schedule)
    schedule.finish()

    cw = conv_full.shape[2]
    rows = [dmods.reshape(depth * N_MOD, d), dgains.reshape(depth * 4, d),
            conv_grads.reshape(-1, d), jnp.broadcast_to(loss, (1, d))]
    payload = jnp.concatenate(rows, axis=0)
    n_rows = payload.shape[0]
    pad = (-n_rows) % 8
    payload = jnp.pad(payload, ((0, pad), (0, 0)))
    everyone = _all_gather_small(payload, name="gather_small_grads")
    total = _sum_devices(everyone, name="sum_small_grads")
    r0 = depth * N_MOD
    grads = {}
    grads["b_ada"] = total[:r0].reshape(depth, N_MOD * d)
    gsum = total[r0:r0 + depth * 4].reshape(depth, 4, d)
    for k, name in enumerate(GAINS):
        grads[name] = gsum[:, k]
    r1 = r0 + depth * 4
    n_conv = (depth * 3 * cw) // d
    conv_g = total[r1:r1 + n_conv].reshape(depth, 3, cw)
    grads["conv_w"] = lax.dynamic_slice_in_dim(conv_g, chip * (cw // 4), cw // 4, axis=2)
    loss_out = total[r1 + n_conv, 0]
    dmod_all = everyone[:, :r0].reshape(8, depth, N_MOD * d)
    dmod_loc = lax.dynamic_slice_in_dim(dmod_all, chip * nl_ada, nl_ada, axis=2)
    grads["w_ada"] = _ada_bwd(c_all.T, jnp.transpose(dmod_loc, (1, 0, 2)), name="ada_bwd")

    for k in BIG:
        grads[k] = schedule.reduced[k]

    deltas, new_m, new_v = {}, {}, {}
    for k in WEIGHTS:
        deltas[k], new_m[k], new_v[k] = _adamw(params[k], grads[k], m_in[k], v_in[k], name=f"adamw_{k}")

    return (loss_out, dx[None], *[grads[k] for k in WEIGHTS], *[deltas[k] for k in WEIGHTS],
            *[new_m[k] for k in WEIGHTS], *[new_v[k] for k in WEIGHTS])
```

```python
import functools

import jax
import jax.numpy as jnp
from jax import lax
from jax.experimental import pallas as pl
from jax.experimental.pallas import tpu as pltpu

F32 = jnp.float32
BF16 = jnp.bfloat16
EPS = 1e-6
N_MOD = 6
HEAD_DIM = 64
LANES = 128
ATTN_SCALE = 1.0 / 8.0
UNDERFLOW_LOG = -90.0
ADAM_LR = 0.001
ADAM_B1 = 0.9
ADAM_B2 = 0.999
ADAM_EPS = 1e-08
ADAM_WD = 0.01
ADAM_STEP = 10
VMEM_LIMIT = 56 * 1024 * 1024
MESH = pl.DeviceIdType.MESH
OTHER_CHIPS = ((1, 0), (0, 1), (1, 1))

_NT = (((1,), (1,)), ((), ()))
_TN = (((0,), (0,)), ((), ()))


def _sds(shape, dtype):
    return jax.ShapeDtypeStruct(shape, dtype)


def _params(sem):
    return pltpu.CompilerParams(dimension_semantics=sem, vmem_limit_bytes=VMEM_LIMIT)


def _fit(t, n):
    t = min(t, n)
    while n % t:
        t //= 2
    return t


def _vec_spec(d, nargs):
    if nargs == 1:
        return pl.BlockSpec((1, d), lambda i: (0, 0))
    return pl.BlockSpec((1, d), lambda i, j: (0, 0))


def _log_one_minus_sigmoid(z):
    return -jnp.log(1.0 + jnp.exp(-jnp.abs(z))) - jnp.maximum(z, 0.0)


def _sigmoid(z):
    t = jnp.exp(-jnp.abs(z))
    return jnp.where(z >= 0.0, 1.0, t) / (1.0 + t)


def _split_bf16(a):
    hi = a.astype(BF16)
    lo = (a - hi.astype(F32)).astype(BF16)
    return hi, lo


def _rms_bwd(dn, xin, g):
    r = lax.rsqrt(jnp.mean(xin * xin, axis=-1, keepdims=True) + EPS)
    xh = xin * r
    dxh = dn * g
    dxin = r * (dxh - xh * jnp.mean(dxh * xh, axis=-1, keepdims=True))
    return dxin, xh


def _colsum(a):
    return jnp.sum(a, axis=0, keepdims=True)


def _norm_mod_matmul(x, g, sc, sh, w, *, name, tm=256, side=None):
    s, d = x.shape
    n = w.shape[1]
    tm = _fit(tm, s)
    nt = s // tm

    def body(*refs):
        i = pl.program_id(0)
        (x_ref, g_ref, sc_ref, sh_ref, w_ref, h_ref, o_ref), late_phases = _host_side(
            side, 5, 2, 0, refs, i == 0, i == (3 * nt) // 4, i == nt - 1)
        xv = x_ref[...]
        r = lax.rsqrt(jnp.mean(xv * xv, axis=-1, keepdims=True) + EPS)
        h = ((xv * r * g_ref[...]) * (1.0 + sc_ref[...]) + sh_ref[...]).astype(BF16)
        h_ref[...] = h
        o_ref[...] = jnp.dot(h, w_ref[...], preferred_element_type=F32).astype(BF16)
        late_phases()

    s_in, s_out, s_shapes, aliases, s_scratch = _side_specs(side, 5, 2)
    res = pl.pallas_call(
        body, name=name, grid=(nt,),
        in_specs=[pl.BlockSpec((tm, d), lambda i: (i, 0)), _vec_spec(d, 1), _vec_spec(d, 1), _vec_spec(d, 1),
                  pl.BlockSpec((d, n), lambda i: (0, 0))] + s_in,
        out_specs=[pl.BlockSpec((tm, d), lambda i: (i, 0)), pl.BlockSpec((tm, n), lambda i: (i, 0))] + s_out,
        out_shape=[_sds((s, d), BF16), _sds((s, n), BF16)] + s_shapes,
        input_output_aliases=aliases, scratch_shapes=s_scratch,
        compiler_params=_params(("arbitrary",)),
    )(x, g, sc, sh, w, *([] if side is None else side.operands))
    return res[:2], res[2:]


HALO = 16


def _conv_fwd(proj, conv_w, *, name, tm=512):
    s = proj.shape[0]
    cw = conv_w.shape[1]
    tm = min(tm, s)
    nb = tm // HALO

    def body(bg_ref, cg_ref, u_ref, cgh_ref, uh_ref, w_ref, yc_ref, vbuf):
        i = pl.program_id(0)
        vv = cg_ref[...].astype(F32) * u_ref[...].astype(F32)
        halo = cgh_ref[...].astype(F32) * uh_ref[...].astype(F32)
        vbuf[0:HALO, :] = jnp.where(i > 0, halo, 0.0)
        vbuf[HALO:HALO + tm, :] = vv
        v1 = vbuf[HALO - 1:HALO - 1 + tm, :]
        v2 = vbuf[HALO - 2:HALO - 2 + tm, :]
        w = w_ref[...]
        y = w[2:3, :] * vv + w[1:2, :] * v1 + w[0:1, :] * v2
        yc_ref[...] = (bg_ref[...].astype(F32) * y).astype(BF16)

    def prev(i):
        return jnp.maximum(i * nb - 1, 0)

    return pl.pallas_call(
        body, name=name, grid=(s // tm,),
        in_specs=[pl.BlockSpec((tm, cw), lambda i: (i, 0)), pl.BlockSpec((tm, cw), lambda i: (i, 1)),
                  pl.BlockSpec((tm, cw), lambda i: (i, 2)),
                  pl.BlockSpec((HALO, cw), lambda i: (prev(i), 1)), pl.BlockSpec((HALO, cw), lambda i: (prev(i), 2)),
                  pl.BlockSpec((3, cw), lambda i: (0, 0))],
        out_specs=pl.BlockSpec((tm, cw), lambda i: (i, 0)),
        out_shape=_sds((s, cw), BF16),
        scratch_shapes=[pltpu.VMEM((HALO + tm, cw), F32)],
        compiler_params=_params(("arbitrary",)),
    )(proj, proj, proj, proj, proj, conv_w)


def _tri(qb):
    r = lax.broadcasted_iota(jnp.int32, (qb, qb), 0)
    c = lax.broadcasted_iota(jnp.int32, (qb, qb), 1)
    return (r >= c).astype(BF16)


def _head_mask(h):
    lane = lax.broadcasted_iota(jnp.int32, (1, LANES), 1)
    return (lane >= HEAD_DIM * h) & (lane < HEAD_DIM * (h + 1))


def _stack_heads(a, masks):
    return jnp.concatenate([jnp.where(m, a, 0).astype(BF16) for m in masks], axis=0)


def _heads_to_lanes(a, qb):
    return jnp.concatenate([a[:qb], a[qb:]], axis=1)


def _stacked_causal(qb, width, first_key, first_query):
    row = lax.broadcasted_iota(jnp.int32, (2 * qb, width), 0)
    col = lax.broadcasted_iota(jnp.int32, (2 * qb, width), 1)
    return first_key + col < first_query + jnp.where(row >= qb, row - qb, row)


def _running_sum(a, tri_m):
    rows, qb = a.shape[0], tri_m.shape[0]
    n = a.shape[1] // qb
    hi, lo = _split_bf16(a)
    stacked = jnp.concatenate([p[:, s * qb:(s + 1) * qb] for s in range(n) for p in (hi, lo)], axis=0)
    both = jnp.dot(stacked, tri_m, preferred_element_type=F32)
    parts = [both[(2 * s) * rows:(2 * s + 1) * rows] + both[(2 * s + 1) * rows:(2 * s + 2) * rows] for s in range(n)]
    later = None
    for s in reversed(range(n)):
        if later is not None:
            parts[s] = parts[s] + later
        later = parts[s][:, 0:1]
    return (parts[0] if n == 1 else jnp.concatenate(parts, axis=1)), later


def _attn_cols(d):
    cw = d // 2
    hp = (d // 2) // LANES
    q0 = (3 * cw) // LANES
    return q0, q0 + hp, q0 + 2 * hp, hp


class _Side:
    def __init__(self, operands, out_shapes, aliases, scratch, start, mid, finish):
        self.operands, self.out_shapes, self.aliases, self.scratch = list(operands), list(out_shapes), aliases, list(scratch)
        self.start, self.mid, self.finish = start, mid, finish


def _side_call(side, *, name):
    n_in, n_out = len(side.operands), len(side.out_shapes)

    def body(*refs):
        parts = refs[:n_in], refs[n_in:n_in + n_out], refs[n_in + n_out:]
        side.start(*parts)
        if side.mid is not None:
            side.mid(*parts)
        side.finish(*parts)

    hbm = pl.BlockSpec(memory_space=pltpu.HBM)
    return pl.pallas_call(body, name=name, in_specs=[hbm] * n_in, out_specs=[hbm] * n_out, out_shape=side.out_shapes,
                          input_output_aliases=dict(side.aliases), scratch_shapes=side.scratch)(*side.operands)


def _host_side(side, n_in, n_out, n_scratch, refs, first, late, last):
    if side is None:
        return refs, lambda: None
    s_in, s_out = len(side.operands), len(side.out_shapes)
    ins = refs[:n_in]
    side_in = refs[n_in:n_in + s_in]
    outs = refs[n_in + s_in:n_in + s_in + n_out]
    side_out = refs[n_in + s_in + n_out:n_in + s_in + n_out + s_out]
    rest = refs[n_in + s_in + n_out + s_out:]
    scratch, sems = rest[:n_scratch], rest[n_scratch:]
    parts = (side_in, side_out, sems)
    pl.when(first)(lambda: side.start(*parts))

    def run_late_phases():
        if side.mid is not None:
            pl.when(late)(lambda: side.mid(*parts))
        pl.when(last)(lambda: side.finish(*parts))

    return (*ins, *outs, *scratch), run_late_phases


def _side_specs(side, n_in, n_out):
    if side is None:
        return [], [], [], {}, []
    hbm = pl.BlockSpec(memory_space=pltpu.HBM)
    s_in = len(side.operands)
    aliases = {n_in + a: n_out + b for a, b in side.aliases.items()}
    return [hbm] * s_in, [hbm] * len(side.out_shapes), side.out_shapes, aliases, side.scratch


def _attn_fwd(proj, tri, *, d, name, qb=256, side=None):
    s = proj.shape[0]
    qb = min(qb, s)
    nq = s // qb
    q0, k0, v0, hp = _attn_cols(d)

    def body(*refs):
        p, i = pl.program_id(0), pl.program_id(1)
        (q_ref, k_ref, v_ref, tri_ref, o_ref), late_phases = _host_side(
            side, 4, 1, 0, refs, (p == 0) & (i == 0), (p == hp - 1) & (i == 0), (p == hp - 1) & (i == nq - 1))
        tri_m = tri_ref[...]
        masks = [_head_mask(h) for h in range(2)]
        qs = _stack_heads(q_ref[...] * ATTN_SCALE, masks)

        def strip(first_key, width, state, causal):
            run, acc = state
            rows = pl.ds(pl.multiple_of(first_key, qb), width)
            z = lax.dot_general(qs, k_ref[rows, :], _NT, preferred_element_type=F32)
            lg = _log_one_minus_sigmoid(z)
            if causal is not None:
                lg = jnp.where(causal, lg, 0.0)
            cs, total = _running_sum(lg, tri_m)
            a = jnp.exp(z + cs + run)
            if causal is not None:
                a = jnp.where(causal, a, 0.0)
            ab = a.astype(BF16)
            acc = acc + jnp.dot(_heads_to_lanes(ab, qb), _stack_heads(v_ref[rows, :], masks),
                                preferred_element_type=F32)
            return run + total, acc

        state = strip(i * qb, qb, (jnp.zeros((2 * qb, 1), F32), jnp.zeros((qb, LANES), F32)),
                      _stacked_causal(qb, qb, 0, 0))
        state = lax.while_loop(
            lambda st: (st[0] >= 0) & (jnp.max(st[1]) > UNDERFLOW_LOG),
            lambda st: (st[0] - 1, *strip(st[0] * qb, qb, st[1:], None)),
            (i - 1, *state))
        o_ref[...] = state[2]
        late_phases()

    s_in, s_out, s_shapes, aliases, s_scratch = _side_specs(side, 4, 1)
    res = pl.pallas_call(
        body, name=name, grid=(hp, nq),
        in_specs=[pl.BlockSpec((qb, LANES), lambda p, i: (i, q0 + p)),
                  pl.BlockSpec((s, LANES), lambda p, i: (0, k0 + p)),
                  pl.BlockSpec((s, LANES), lambda p, i: (0, v0 + p)),
                  pl.BlockSpec((qb, qb), lambda p, i: (0, 0))] + s_in,
        out_specs=[pl.BlockSpec((qb, LANES), lambda p, i: (i, p))] + s_out,
        out_shape=[_sds((s, hp * LANES), F32)] + s_shapes,
        input_output_aliases=aliases, scratch_shapes=s_scratch,
        compiler_params=_params(("arbitrary", "arbitrary")),
    )(proj, proj, proj, tri, *([] if side is None else side.operands))
    return res[0], res[1:]


def _mix_out(yc, o, proj, x, wpc, wpa, wout, g, gt, *, name, tm=256):
    s, d = x.shape
    cw = yc.shape[1]
    tm = min(tm, s)
    ga_blk = (3 * cw + 3 * (d // 2)) // d

    def body(yc_ref, o_ref, ga_ref, gb_ref, x_ref, wpc_ref, wpa_ref, wout_ref, g_ref, gt_ref,
             ycv_ref, yat_ref, mg_ref, mix_ref, x1_ref):
        y_conv = jnp.dot(yc_ref[...], wpc_ref[...], preferred_element_type=F32)
        y_attn = jnp.dot(o_ref[...].astype(BF16), wpa_ref[...], preferred_element_type=F32)
        merged = (_sigmoid(ga_ref[...].astype(F32)) * y_conv + _sigmoid(gb_ref[...].astype(F32)) * y_attn)
        mg = merged.astype(BF16)
        mix = jnp.dot(mg, wout_ref[...], preferred_element_type=F32)
        r = lax.rsqrt(jnp.mean(mix * mix, axis=-1, keepdims=True) + EPS)
        ycv_ref[...] = y_conv.astype(BF16)
        yat_ref[...] = y_attn.astype(BF16)
        mg_ref[...] = mg
        mix_ref[...] = mix
        x1_ref[...] = x_ref[...] + gt_ref[...] * (mix * r * g_ref[...])

    def rows(w):
        return pl.BlockSpec((tm, w), lambda i: (i, 0))

    def full(a):
        return pl.BlockSpec(a.shape, lambda i: (0, 0))

    return pl.pallas_call(
        body, name=name, grid=(s // tm,),
        in_specs=[rows(cw), rows(d // 2), pl.BlockSpec((tm, d), lambda i: (i, ga_blk)),
                  pl.BlockSpec((tm, d), lambda i: (i, ga_blk + 1)), rows(d),
                  full(wpc), full(wpa), full(wout), _vec_spec(d, 1), _vec_spec(d, 1)],
        out_specs=[rows(d), rows(d), rows(d), rows(d), rows(d)],
        out_shape=[_sds((s, d), BF16), _sds((s, d), BF16), _sds((s, d), BF16), _sds((s, d), F32), _sds((s, d), F32)],
        compiler_params=_params(("parallel",)),
    )(yc, o, proj, proj, x, wpc, wpa, wout, g, gt)


def _relu2(a):
    r = jnp.maximum(a.astype(F32), 0.0)
    return (r * r).astype(BF16)


def _mlp_out(a, x, w2, g, gt, *, name, tm=512):
    s, d = x.shape
    dff = a.shape[1]
    tm = min(tm, s)

    def body(a_ref, x_ref, w_ref, g_ref, gt_ref, ff_ref, x2_ref):
        ff = jnp.dot(_relu2(a_ref[...]), w_ref[...], preferred_element_type=F32)
        r = lax.rsqrt(jnp.mean(ff * ff, axis=-1, keepdims=True) + EPS)
        ff_ref[...] = ff
        x2_ref[...] = x_ref[...] + gt_ref[...] * (ff * r * g_ref[...])

    return pl.pallas_call(
        body, name=name, grid=(s // tm,),
        in_specs=[pl.BlockSpec((tm, dff), lambda i: (i, 0)), pl.BlockSpec((tm, d), lambda i: (i, 0)),
                  pl.BlockSpec((dff, d), lambda i: (0, 0)), _vec_spec(d, 1), _vec_spec(d, 1)],
        out_specs=[pl.BlockSpec((tm, d), lambda i: (i, 0)), pl.BlockSpec((tm, d), lambda i: (i, 0))],
        out_shape=[_sds((s, d), F32), _sds((s, d), F32)],
        compiler_params=_params(("parallel",)),
    )(a, x, w2, g, gt)


def _loss_grad(y, target, *, name, tm=512):
    s, d = y.shape
    tm = min(tm, s)

    def body(y_ref, t_ref, dy_ref, loss_ref):
        @pl.when(pl.program_id(0) == 0)
        def _():
            loss_ref[...] = jnp.zeros_like(loss_ref)
        e = y_ref[...] - t_ref[...]
        dy_ref[...] = e * (1.0 / d)
        loss_ref[...] += 0.5 * jnp.sum(jnp.mean(e * e, axis=-1, keepdims=True), axis=0, keepdims=True)

    return pl.pallas_call(
        body, name=name, grid=(s // tm,),
        in_specs=[pl.BlockSpec((tm, d), lambda i: (i, 0)), pl.BlockSpec((tm, d), lambda i: (i, 0))],
        out_specs=[pl.BlockSpec((tm, d), lambda i: (i, 0)), pl.BlockSpec((1, 1), lambda i: (0, 0))],
        out_shape=[_sds((s, d), F32), _sds((1, 1), F32)],
        compiler_params=_params(("arbitrary",)),
    )(y, target)


def _mlp_out_bwd(dx, ff, a, w2, g, gt, *, name, tm=256):
    s, d = dx.shape
    dff = a.shape[1]
    tm = min(tm, s)

    def body(dx_ref, ff_ref, a_ref, w_ref, g_ref, gt_ref, dff_ref, da_ref, dgt_ref, dg_ref):
        @pl.when(pl.program_id(0) == 0)
        def _():
            dgt_ref[...] = jnp.zeros_like(dgt_ref)
            dg_ref[...] = jnp.zeros_like(dg_ref)
        dxv = dx_ref[...]
        dn = dxv * gt_ref[...]
        dffv, xh = _rms_bwd(dn, ff_ref[...], g_ref[...])
        dgt_ref[...] += _colsum(dxv * (xh * g_ref[...]))
        dg_ref[...] += _colsum(dn * xh)
        dffb = dffv.astype(BF16)
        dff_ref[...] = dffb
        drr = lax.dot_general(dffb, w_ref[...], _NT, preferred_element_type=F32)
        da_ref[...] = (drr * (2.0 * jnp.maximum(a_ref[...].astype(F32), 0.0))).astype(BF16)

    return pl.pallas_call(
        body, name=name, grid=(s // tm,),
        in_specs=[pl.BlockSpec((tm, d), lambda i: (i, 0)), pl.BlockSpec((tm, d), lambda i: (i, 0)),
                  pl.BlockSpec((tm, dff), lambda i: (i, 0)), pl.BlockSpec((dff, d), lambda i: (0, 0)),
                  _vec_spec(d, 1), _vec_spec(d, 1)],
        out_specs=[pl.BlockSpec((tm, d), lambda i: (i, 0)), pl.BlockSpec((tm, dff), lambda i: (i, 0)),
                   _vec_spec(d, 1), _vec_spec(d, 1)],
        out_shape=[_sds((s, d), BF16), _sds((s, dff), BF16), _sds((1, d), F32), _sds((1, d), F32)],
        compiler_params=_params(("arbitrary",)),
    )(dx, ff, a, w2, g, gt)


def _matmul_nt_norm_bwd(dys, w, x, dres, g, sc, *, name, tm=256, side=None):
    s = dys[0].shape[0]
    widths = [dy.shape[1] for dy in dys]
    d, n = w.shape
    assert sum(widths) == n, (widths, n)
    tm = _fit(tm, s)
    nt = s // tm
    np_ = len(dys)

    def body(*refs):
        i = pl.program_id(0)
        own, late_phases = _host_side(side, np_ + 5, 4, 0, refs, i == 0, i == (3 * nt) // 4, i == nt - 1)
        dy_refs = own[:np_]
        w_ref, x_ref, dres_ref, g_ref, sc_ref, dx_ref, dsh_ref, dsc_ref, dg_ref = own[np_:]

        @pl.when(i == 0)
        def _():
            dsh_ref[...] = jnp.zeros_like(dsh_ref)
            dsc_ref[...] = jnp.zeros_like(dsc_ref)
            dg_ref[...] = jnp.zeros_like(dg_ref)

        dh = None
        for p, dy_ref in enumerate(dy_refs):
            cols = slice(sum(widths[:p]), sum(widths[:p + 1]))
            part = lax.dot_general(dy_ref[...], w_ref[:, cols], _NT, preferred_element_type=F32)
            dh = part if dh is None else dh + part
        dn = dh * (1.0 + sc_ref[...])
        dxin, xh = _rms_bwd(dn, x_ref[...], g_ref[...])
        dsh_ref[...] += _colsum(dh)
        dsc_ref[...] += _colsum(dh * (xh * g_ref[...]))
        dg_ref[...] += _colsum(dn * xh)
        dx_ref[...] = dres_ref[...] + dxin
        late_phases()

    s_in, s_out, s_shapes, aliases, s_scratch = _side_specs(side, np_ + 5, 4)
    res = pl.pallas_call(
        body, name=name, grid=(nt,),
        in_specs=[pl.BlockSpec((tm, wd), lambda i: (i, 0)) for wd in widths]
        + [pl.BlockSpec((d, n), lambda i: (0, 0)),
           pl.BlockSpec((tm, d), lambda i: (i, 0)), pl.BlockSpec((tm, d), lambda i: (i, 0)),
           _vec_spec(d, 1), _vec_spec(d, 1)] + s_in,
        out_specs=[pl.BlockSpec((tm, d), lambda i: (i, 0)), _vec_spec(d, 1), _vec_spec(d, 1), _vec_spec(d, 1)] + s_out,
        out_shape=[_sds((s, d), F32), _sds((1, d), F32), _sds((1, d), F32), _sds((1, d), F32)] + s_shapes,
        input_output_aliases=aliases, scratch_shapes=s_scratch,
        compiler_params=_params(("arbitrary",)),
    )(*dys, w, x, dres, g, sc, *([] if side is None else side.operands))
    return res[:4], res[4:]


def _matmul_tn(a, bs, *, name, tk=1024, tn=1024, ts=512, relu2=False):
    s, k = a.shape
    widths = [b.shape[1] for b in bs]
    n = sum(widths)
    tk, ts = _fit(tk, k), _fit(ts, s)
    for w in widths:
        tn = _fit(tn, w)
    nt = s // ts
    assert all(sum(widths[:p]) % tn == 0 for p in range(len(bs))), (widths, tn)
    first = [sum(widths[:p]) // tn for p in range(len(bs))]
    tiles = [w // tn for w in widths]

    def body(a_ref, *rest):
        b_refs, o_ref, acc = rest[:len(bs)], rest[-2], rest[-1]
        j, t = pl.program_id(1), pl.program_id(2)

        @pl.when(t == 0)
        def _():
            acc[...] = jnp.zeros_like(acc)
        av = a_ref[...]
        av = _relu2(av) if relu2 else av.astype(BF16)
        for p, b_ref in enumerate(b_refs):
            def add(b_ref=b_ref):
                acc[...] += lax.dot_general(av, b_ref[...], _TN, preferred_element_type=F32)
            if len(bs) == 1:
                add()
            else:
                pl.when((j >= first[p]) & (j < first[p] + tiles[p]))(add)

        @pl.when(t == nt - 1)
        def _():
            o_ref[...] = acc[...].astype(BF16)

    def piece_spec(p):
        def index(i, j, t):
            mine = (j >= first[p]) & (j < first[p] + tiles[p])
            return jnp.where(mine, t, 0), jnp.where(mine, j - first[p], 0)
        return pl.BlockSpec((ts, tn), index)

    return pl.pallas_call(
        body, name=name, grid=(k // tk, n // tn, nt),
        in_specs=[pl.BlockSpec((ts, tk), lambda i, j, t: (t, i))] + [piece_spec(p) for p in range(len(bs))],
        out_specs=pl.BlockSpec((tk, tn), lambda i, j, t: (i, j)),
        out_shape=_sds((k, n), BF16),
        scratch_shapes=[pltpu.VMEM((tk, tn), F32)],
        compiler_params=_params(("parallel", "parallel", "arbitrary")),
    )(a, *bs)


def _mix_out_bwd(dx, mix, proj, ycv, yat, wout, wpc, wpa, g, gt, *, name, tm=256):
    s, d = dx.shape
    cw = wpc.shape[0]
    aw = wpa.shape[0]
    tm = min(tm, s)
    ga_blk = (3 * cw + 3 * aw) // d

    def body(dx_ref, mix_ref, ga_ref, gb_ref, ycv_ref, yat_ref, wout_ref, wpc_ref, wpa_ref, g_ref, gt_ref,
             dmix_ref, dycv_ref, dyat_ref, dyc_ref, do_ref, dgate_ref, dgt_ref, dg_ref):
        @pl.when(pl.program_id(0) == 0)
        def _():
            dgt_ref[...] = jnp.zeros_like(dgt_ref)
            dg_ref[...] = jnp.zeros_like(dg_ref)
        dxv = dx_ref[...]
        dn = dxv * gt_ref[...]
        dmix, xh = _rms_bwd(dn, mix_ref[...], g_ref[...])
        dgt_ref[...] += _colsum(dxv * (xh * g_ref[...]))
        dg_ref[...] += _colsum(dn * xh)
        dmixb = dmix.astype(BF16)
        dmix_ref[...] = dmixb
        dmerged = lax.dot_general(dmixb, wout_ref[...], _NT, preferred_element_type=F32)
        sga = _sigmoid(ga_ref[...].astype(F32))
        sgb = _sigmoid(gb_ref[...].astype(F32))
        dycv = (dmerged * sga).astype(BF16)
        dyat = (dmerged * sgb).astype(BF16)
        dycv_ref[...] = dycv
        dyat_ref[...] = dyat
        dgate_ref[:, 0:d] = (dmerged * ycv_ref[...].astype(F32) * (sga * (1.0 - sga))).astype(BF16)
        dgate_ref[:, d:2 * d] = (dmerged * yat_ref[...].astype(F32) * (sgb * (1.0 - sgb))).astype(BF16)
        dyc_ref[...] = lax.dot_general(dycv, wpc_ref[...], _NT, preferred_element_type=F32).astype(BF16)
        do_ref[...] = lax.dot_general(dyat, wpa_ref[...], _NT, preferred_element_type=F32).astype(BF16)

    def rows(w):
        return pl.BlockSpec((tm, w), lambda i: (i, 0))

    def full(a):
        return pl.BlockSpec(a.shape, lambda i: (0, 0))

    return pl.pallas_call(
        body, name=name, grid=(s // tm,),
        in_specs=[rows(d), rows(d), pl.BlockSpec((tm, d), lambda i: (i, ga_blk)),
                  pl.BlockSpec((tm, d), lambda i: (i, ga_blk + 1)), rows(d), rows(d),
                  full(wout), full(wpc), full(wpa), _vec_spec(d, 1), _vec_spec(d, 1)],
        out_specs=[rows(d), rows(d), rows(d), rows(cw), rows(aw), rows(2 * d), _vec_spec(d, 1), _vec_spec(d, 1)],
        out_shape=[_sds((s, d), BF16), _sds((s, d), BF16), _sds((s, d), BF16), _sds((s, cw), BF16),
                   _sds((s, aw), BF16), _sds((s, 2 * d), BF16), _sds((1, d), F32), _sds((1, d), F32)],
        compiler_params=_params(("arbitrary",)),
    )(dx, mix, proj, proj, ycv, yat, wout, wpc, wpa, g, gt)


def _conv_bwd(dyc, proj, conv_w, *, name, tm=512):
    s = proj.shape[0]
    cw = conv_w.shape[1]
    tm = min(tm, s)
    nb = tm // HALO
    nt = s // tm
    last_blk = s // HALO - 1

    def body(dyc_ref, bg_ref, cg_ref, u_ref, cgh_ref, uh_ref, dych_ref, bgh_ref, w_ref,
             dconv_ref, dw_ref, vbuf, gbuf):
        i = pl.program_id(0)

        @pl.when(i == 0)
        def _():
            dw_ref[...] = jnp.zeros_like(dw_ref)

        cg = cg_ref[...].astype(F32)
        u = u_ref[...].astype(F32)
        vv = cg * u
        halo = cgh_ref[...].astype(F32) * uh_ref[...].astype(F32)
        vbuf[0:HALO, :] = jnp.where(i > 0, halo, 0.0)
        vbuf[HALO:HALO + tm, :] = vv
        v1 = vbuf[HALO - 1:HALO - 1 + tm, :]
        v2 = vbuf[HALO - 2:HALO - 2 + tm, :]
        w = w_ref[...]
        y = w[2:3, :] * vv + w[1:2, :] * v1 + w[0:1, :] * v2
        dyc = dyc_ref[...].astype(F32)
        dconv_ref[:, 0:cw] = (dyc * y).astype(BF16)
        gy = dyc * bg_ref[...].astype(F32)
        nxt = dych_ref[...].astype(F32) * bgh_ref[...].astype(F32)
        gbuf[0:tm, :] = gy
        gbuf[tm:tm + HALO, :] = jnp.where(i < nt - 1, nxt, 0.0)
        g1 = gbuf[1:1 + tm, :]
        g2 = gbuf[2:2 + tm, :]
        dvv = w[2:3, :] * gy + w[1:2, :] * g1 + w[0:1, :] * g2
        dconv_ref[:, cw:2 * cw] = (dvv * u).astype(BF16)
        dconv_ref[:, 2 * cw:3 * cw] = (dvv * cg).astype(BF16)
        dw_ref[0:1, :] += _colsum(gy * v2)
        dw_ref[1:2, :] += _colsum(gy * v1)
        dw_ref[2:3, :] += _colsum(gy * vv)

    def prev(i):
        return jnp.maximum(i * nb - 1, 0)

    def nxt_blk(i):
        return jnp.minimum((i + 1) * nb, last_blk)

    def col(c):
        return pl.BlockSpec((tm, cw), lambda i: (i, c))

    return pl.pallas_call(
        body, name=name, grid=(nt,),
        in_specs=[col(0), col(0), col(1), col(2),
                  pl.BlockSpec((HALO, cw), lambda i: (prev(i), 1)), pl.BlockSpec((HALO, cw), lambda i: (prev(i), 2)),
                  pl.BlockSpec((HALO, cw), lambda i: (nxt_blk(i), 0)), pl.BlockSpec((HALO, cw), lambda i: (nxt_blk(i), 0)),
                  pl.BlockSpec((3, cw), lambda i: (0, 0))],
        out_specs=[pl.BlockSpec((tm, 3 * cw), lambda i: (i, 0)), pl.BlockSpec((3, cw), lambda i: (0, 0))],
        out_shape=[_sds((s, 3 * cw), BF16), _sds((3, cw), F32)],
        scratch_shapes=[pltpu.VMEM((HALO + tm, cw), F32), pltpu.VMEM((tm + HALO, cw), F32)],
        compiler_params=_params(("arbitrary",)),
    )(dyc, proj, proj, proj, proj, proj, dyc, proj, conv_w)


def _attn_bwd(proj, o, do, tri, *, d, name, qb=256, side=None):
    s = proj.shape[0]
    qb = min(qb, s)
    nq = s // qb
    q0, k0, v0, hp = _attn_cols(d)

    def body(*refs):
        p, i = pl.program_id(0), pl.program_id(1)
        own, late_phases = _host_side(
            side, 6, 3, 2, refs, (p == 0) & (i == 0), (p == hp - 1) & (i == 0), (p == hp - 1) & (i == nq - 1))
        q_ref, k_ref, v_ref, o_ref, do_ref, tri_ref, dq_ref, dk_ref, dv_ref, dk_acc, dv_acc = own

        @pl.when(i == 0)
        def _():
            dk_acc[...] = jnp.zeros_like(dk_acc)
            dv_acc[...] = jnp.zeros_like(dv_acc)

        tri_m = tri_ref[...]
        dov = do_ref[...]
        masks = [_head_mask(h) for h in range(2)]
        qs = _stack_heads(q_ref[...] * ATTN_SCALE, masks)
        dos = _stack_heads(dov, masks)
        dprod = dov.astype(F32) * o_ref[...]
        dtot = jnp.concatenate([jnp.sum(jnp.where(m, dprod, 0.0), axis=-1, keepdims=True) for m in masks], axis=0)

        def strip(first_key, width, state, causal):
            run, grun, dq_acc = state
            rows = pl.ds(pl.multiple_of(first_key, qb), width)
            kb = k_ref[rows, :]
            z = lax.dot_general(qs, kb, _NT, preferred_element_type=F32)
            lg = _log_one_minus_sigmoid(z)
            beta = 1.0 - jnp.exp(lg)
            if causal is not None:
                lg = jnp.where(causal, lg, 0.0)
            cs, total = _running_sum(lg, tri_m)
            a = jnp.exp(z + cs + run)
            if causal is not None:
                a = jnp.where(causal, a, 0.0)
            ab = a.astype(BF16)
            da = lax.dot_general(dos, v_ref[rows, :], _NT, preferred_element_type=F32)
            gg = ab.astype(F32) * da
            gcs, gtotal = _running_sum(gg, tri_m)
            left = jnp.where(run > UNDERFLOW_LOG, dtot - grun, 0.0)
            dz = gg - beta * (gg + (left - gcs))
            if causal is not None:
                dz = jnp.where(causal, dz, 0.0)
            dzb = dz.astype(BF16)
            dq_acc = dq_acc + jnp.dot(_heads_to_lanes(dzb, qb), _stack_heads(kb, masks), preferred_element_type=F32)
            dk_acc[rows, :] += lax.dot_general(dzb, qs, _TN, preferred_element_type=F32)
            dv_acc[rows, :] += lax.dot_general(ab, dos, _TN, preferred_element_type=F32)
            return run + total, grun + gtotal, dq_acc

        zero = jnp.zeros((2 * qb, 1), F32)
        state = strip(i * qb, qb, (zero, zero, jnp.zeros((qb, LANES), F32)), _stacked_causal(qb, qb, 0, 0))
        state = lax.while_loop(
            lambda st: (st[0] >= 0) & (jnp.max(st[1]) > UNDERFLOW_LOG),
            lambda st: (st[0] - 1, *strip(st[0] * qb, qb, st[1:], None)),
            (i - 1, *state))
        dq_ref[...] = (state[3] * ATTN_SCALE).astype(BF16)

        @pl.when(i == nq - 1)
        def _():
            dk_ref[...] = dk_acc[...].astype(BF16)
            dv_ref[...] = dv_acc[...].astype(BF16)

        late_phases()

    aw = hp * LANES
    s_in, s_out, s_shapes, aliases, s_scratch = _side_specs(side, 6, 3)
    res = pl.pallas_call(
        body, name=name, grid=(hp, nq),
        in_specs=[pl.BlockSpec((qb, LANES), lambda p, i: (i, q0 + p)),
                  pl.BlockSpec((s, LANES), lambda p, i: (0, k0 + p)),
                  pl.BlockSpec((s, LANES), lambda p, i: (0, v0 + p)),
                  pl.BlockSpec((qb, LANES), lambda p, i: (i, p)),
                  pl.BlockSpec((qb, LANES), lambda p, i: (i, p)),
                  pl.BlockSpec((qb, qb), lambda p, i: (0, 0))] + s_in,
        out_specs=[pl.BlockSpec((qb, LANES), lambda p, i: (i, p)),
                   pl.BlockSpec((s, LANES), lambda p, i: (0, p)),
                   pl.BlockSpec((s, LANES), lambda p, i: (0, p))] + s_out,
        out_shape=[_sds((s, aw), BF16), _sds((s, aw), BF16), _sds((s, aw), BF16)] + s_shapes,
        input_output_aliases=aliases,
        scratch_shapes=[pltpu.VMEM((s, LANES), F32), pltpu.VMEM((s, LANES), F32)] + s_scratch,
        compiler_params=_params(("arbitrary", "arbitrary")),
    )(proj, proj, proj, o, do, tri, *([] if side is None else side.operands))
    return res[:3], res[3:]


def _hosted(hooks, kind, l, fn, *args, **kw):
    res, side_out = fn(*args, side=hooks.side(kind, l), **kw)
    hooks.done(kind, l, side_out)
    return res


def _layer_fwd(x, mod, gains, conv_w, tri, *, l, hooks):
    sh1, sc1, gt1, sh2, sc2, gt2 = mod
    g_pre_mix, g_post_mix, g_pre_mlp, g_post_mlp = gains
    d = x.shape[1]
    w = functools.partial(hooks.weight, l)
    h, proj = _hosted(hooks, "in_proj", l, _norm_mod_matmul, x, g_pre_mix, sc1, sh1, w("w_in"), name=f"in_proj_{l}")
    yc = _conv_fwd(proj, conv_w, name=f"conv_fwd_{l}")
    o = _hosted(hooks, "attn_fwd", l, _attn_fwd, proj, tri, d=d, name=f"attn_fwd_{l}")
    ycv, yat, merged, mix, x1 = _mix_out(yc, o, proj, x, w("w_proj_conv"), w("w_proj_attn"), w("w_out"),
                                         g_post_mix, gt1, name=f"mix_out_{l}")
    (h2, a), _ = _norm_mod_matmul(x1, g_pre_mlp, sc2, sh2, w("w_mlp_in"), name=f"mlp_in_{l}")
    ff, x2 = _mlp_out(a, x1, w("w_mlp_out"), g_post_mlp, gt2, name=f"mlp_out_{l}")
    saved = dict(x=x, h=h, proj=proj, yc=yc, o=o, ycv=ycv, yat=yat, merged=merged, mix=mix, x1=x1, h2=h2, a=a, ff=ff,
                 conv_w=conv_w, **{k: w(k) for k in BIG})
    return x2, saved


def _layer_bwd(dx2, sv, mod, gains, tri, *, l, hooks):
    sh1, sc1, gt1, sh2, sc2, gt2 = mod
    g_pre_mix, g_post_mix, g_pre_mlp, g_post_mlp = gains
    d = dx2.shape[1]
    dff, da, dgt2, dg_post_mlp = _mlp_out_bwd(dx2, sv["ff"], sv["a"], sv["w_mlp_out"], g_post_mlp, gt2,
                                              name=f"mlp_out_bwd_{l}")
    hooks.grad(l, "w_mlp_out", _matmul_tn(sv["a"], [dff], relu2=True, name=f"gw_mlp_out_{l}"))
    (dx1, dsh2, dsc2, dg_pre_mlp), _ = _matmul_nt_norm_bwd([da], sv["w_mlp_in"], sv["x1"], dx2, g_pre_mlp, sc2,
                                                           name=f"mlp_in_bwd_{l}")
    hooks.grad(l, "w_mlp_in", _matmul_tn(sv["h2"], [da], name=f"gw_mlp_in_{l}"))
    dmix, dycv, dyat, dyc, do, dgate, dgt1, dg_post_mix = _mix_out_bwd(
        dx1, sv["mix"], sv["proj"], sv["ycv"], sv["yat"], sv["w_out"], sv["w_proj_conv"], sv["w_proj_attn"],
        g_post_mix, gt1, name=f"mix_out_bwd_{l}")
    hooks.grad(l, "w_out", _matmul_tn(sv["merged"], [dmix], name=f"gw_out_{l}"))
    hooks.grad(l, "w_proj_conv", _matmul_tn(sv["yc"], [dycv], name=f"gw_proj_conv_{l}"))
    hooks.grad(l, "w_proj_attn", _matmul_tn(sv["o"], [dyat], name=f"gw_proj_attn_{l}"))
    dconv, g_conv_w = _conv_bwd(dyc, sv["proj"], sv["conv_w"], name=f"conv_bwd_{l}")
    dq, dk, dv = _hosted(hooks, "attn_bwd", l, _attn_bwd, sv["proj"], sv["o"], do, tri, d=d, name=f"attn_bwd_{l}")
    dproj = [dconv, dq, dk, dv, dgate]
    hooks.grad(l, "w_in", _matmul_tn(sv["h"], dproj, name=f"gw_in_{l}"))
    dx0, dsh1, dsc1, dg_pre_mix = _hosted(hooks, "in_proj_bwd", l, _matmul_nt_norm_bwd, dproj, sv["w_in"], sv["x"], dx1,
                                          g_pre_mix, sc1, name=f"in_proj_bwd_{l}")
    dmod = jnp.concatenate([dsh1, dsc1, dgt1, dsh2, dsc2, dgt2], axis=0)
    dgains = jnp.concatenate([dg_pre_mix, dg_post_mix, dg_pre_mlp, dg_post_mlp], axis=0)
    return dx0, g_conv_w, dmod, dgains


BIG = ("w_in", "w_proj_conv", "w_proj_attn", "w_out", "w_mlp_in", "w_mlp_out")
SHARD_AXIS = dict(w_in=1, w_proj_conv=1, w_proj_attn=1, w_out=0, w_mlp_in=1, w_mlp_out=0)


class _LocalWeights:
    def __init__(self, wlayers):
        self.wlayers = wlayers
        self.grads = {}

    def weight(self, l, name):
        return self.wlayers[l][name]

    def side(self, kind, l):
        return None

    def done(self, kind, l, outs):
        pass

    def grad(self, l, name, g):
        self.grads[(l, name)] = g


def _local_step(x, target, mods, gains, conv_w, hooks):
    depth = mods.shape[0]
    tri = _tri(min(256, x.shape[0]))
    saved = []
    for l in range(depth):
        mod = [mods[l, k:k + 1] for k in range(N_MOD)]
        gl = [gains[l, k:k + 1] for k in range(4)]
        x, sv = _layer_fwd(x, mod, gl, conv_w[l], tri, l=l, hooks=hooks)
        saved.append((sv, mod, gl))
    dx, loss = _loss_grad(x, target, name="loss_grad")
    dconv, dmods, dgains = [None] * depth, [None] * depth, [None] * depth
    for l in reversed(range(depth)):
        sv, mod, gl = saved[l]
        dx, dconv[l], dmods[l], dgains[l] = _layer_bwd(dx, sv, mod, gl, tri, l=l, hooks=hooks)
    return loss, dx, jnp.stack(dconv), jnp.stack(dmods), jnp.stack(dgains)


def _coords():
    return lax.axis_index("x"), lax.axis_index("y"), lax.axis_index("c")


def _flip(v, f):
    return 1 - v if f else v


def _all_gather_small(v, *, name):
    r, c_ = v.shape

    def body(v_ref, out_ref, send_sems, recv_sems, local_sem):
        x, y, c = _coords()
        me = 4 * x + 2 * y + c
        mine = pltpu.make_async_copy(v_ref, out_ref.at[me], local_sem)
        mine.start()
        copies = []
        for k in range(1, 8):
            fx, fy, fc = (k >> 2) & 1, (k >> 1) & 1, k & 1
            px, py, pc = _flip(x, fx), _flip(y, fy), _flip(c, fc)
            out = pltpu.make_async_remote_copy(src_ref=v_ref, dst_ref=out_ref.at[me], send_sem=send_sems.at[k - 1],
                                               recv_sem=recv_sems.at[k - 1], device_id=(px, py, pc), device_id_type=MESH)
            out.start()
            back = pltpu.make_async_remote_copy(src_ref=v_ref, dst_ref=out_ref.at[4 * px + 2 * py + pc],
                                                send_sem=send_sems.at[k - 1], recv_sem=recv_sems.at[k - 1],
                                                device_id=(px, py, pc), device_id_type=MESH)
            copies.append((out, back))
        for out, back in copies:
            back.wait_recv()
        for out, back in copies:
            out.wait_send()
        mine.wait()

    return pl.pallas_call(
        body, name=name,
        in_specs=[pl.BlockSpec(memory_space=pltpu.VMEM)],
        out_specs=pl.BlockSpec(memory_space=pltpu.VMEM),
        out_shape=_sds((8, r, c_), F32),
        scratch_shapes=[pltpu.SemaphoreType.DMA((7,)), pltpu.SemaphoreType.DMA((7,)), pltpu.SemaphoreType.DMA],
    )(v)


def _shard_dims(full_shape, axis):
    k, n = full_shape
    return (k // 4, n) if axis == 0 else (k, n // 4)


def _shard_window(ref, axis, chip, half, rows, cols):
    r0, rn = (0, rows) if half is None else (half * (rows // 2), rows // 2)
    if axis == 1:
        return ref.at[pl.ds(r0, rn), pl.ds(chip * cols, cols)]
    return ref.at[pl.ds(chip * rows + r0, rn), :]


def _cast_place(w, layer, axis, chip_arr, *, name, tr=256):
    _, rows, cols = w.shape
    tr = _fit(tr, rows)
    nb = rows // tr
    full = (rows * 4, cols) if axis == 0 else (rows, cols * 4)

    def body(chip_ref, w_ref, o_ref):
        o_ref[...] = w_ref[0].astype(BF16)

    if axis == 1:
        out_map = lambda i, chip: (i, chip[0])
    else:
        out_map = lambda i, chip: (chip[0] * nb + i, 0)
    grid_spec = pltpu.PrefetchScalarGridSpec(
        num_scalar_prefetch=1, grid=(nb,),
        in_specs=[pl.BlockSpec((1, tr, cols), lambda i, chip: (layer, i, 0))],
        out_specs=pl.BlockSpec((tr, cols), out_map))
    return pl.pallas_call(body, name=name, grid_spec=grid_spec, out_shape=_sds(full, BF16),
                          compiler_params=_params(("arbitrary",)))(chip_arr, w)


def _gather_side(fulls, axes):
    n = len(fulls)

    def copies(outs, sems):
        send_sems, recv_sems = sems
        x, y, c = _coords()
        chip = 2 * x + y
        sibling = (x, y, 1 - c)
        table = []
        for w in range(n):
            rows, cols = _shard_dims(outs[w].shape, axes[w])
            win = functools.partial(_shard_window, outs[w], axes[w], rows=rows, cols=cols)
            for j, (fx, fy) in enumerate(OTHER_CHIPS):
                px, py = _flip(x, fx), _flip(y, fy)
                pchip = 2 * px + py

                def copy(piece, sem, to):
                    return pltpu.make_async_remote_copy(src_ref=piece, dst_ref=piece, send_sem=send_sems.at[w, sem],
                                                        recv_sem=recv_sems.at[w, sem], device_id=to, device_id_type=MESH)

                table.append((copy(win(chip, c), j, (px, py, c)), copy(win(pchip, c), j, (px, py, c)),
                              copy(win(pchip, c), 3 + j, sibling), copy(win(pchip, 1 - c), 3 + j, sibling)))
        return table

    def start(ins, outs, sems):
        for send, _, _, _ in copies(outs, sems):
            send.start()

    def mid(ins, outs, sems):
        for _, landed, pass_on, _ in copies(outs, sems):
            landed.wait_recv()
            pass_on.start()

    def finish(ins, outs, sems):
        table = copies(outs, sems)
        for _, _, _, from_sibling in table:
            from_sibling.wait_recv()
        for send, _, pass_on, _ in table:
            send.wait_send()
            pass_on.wait_send()

    return _Side(fulls, [_sds(f.shape, f.dtype) for f in fulls], {w: w for w in range(n)},
                 [pltpu.SemaphoreType.DMA((n, 6)), pltpu.SemaphoreType.DMA((n, 6))], start, mid, finish)


def _exchange_side(grads, axes):
    n = len(grads)
    out_shapes = []
    for g, ax in zip(grads, axes):
        rows, cols = _shard_dims(g.shape, ax)
        out_shapes.append(_sds((7, rows // 2, cols), g.dtype))

    def copies(ins, outs, sems):
        send_sems, recv_sems = sems
        x, y, c = _coords()
        table = []
        for w in range(n):
            rows, cols = _shard_dims(ins[w].shape, axes[w])
            for k in range(1, 8):
                fx, fy, fc = (k >> 2) & 1, (k >> 1) & 1, k & 1
                px, py, pc = _flip(x, fx), _flip(y, fy), _flip(c, fc)
                piece = _shard_window(ins[w], axes[w], 2 * px + py, pc, rows, cols)
                table.append(pltpu.make_async_remote_copy(
                    src_ref=piece, dst_ref=outs[w].at[k - 1], send_sem=send_sems.at[w, k - 1],
                    recv_sem=recv_sems.at[w, k - 1], device_id=(px, py, pc), device_id_type=MESH))
        return table

    def start(ins, outs, sems):
        for cp in copies(ins, outs, sems):
            cp.start()

    def finish(ins, outs, sems):
        table = copies(ins, outs, sems)
        for cp in table:
            cp.wait_recv()
        for cp in table:
            cp.wait_send()

    return _Side(grads, out_shapes, {}, [pltpu.SemaphoreType.DMA((n, 7)), pltpu.SemaphoreType.DMA((n, 7))],
                 start, None, finish)


def _rs_sum_join(g, got, out_prev, layer, depth, axis, ids, *, name, tr=256):
    _, rows2, cols = got.shape
    tr = _fit(tr, rows2)
    nt = rows2 // tr
    if axis == 1:
        own_map = lambda i, ids_: (ids_[1] * nt + i, ids_[0])
    else:
        own_map = lambda i, ids_: ((ids_[0] * 2 + ids_[1]) * nt + i, 0)

    def body(ids_ref, g_ref, got_ref, *rest):
        out_ref, buf, local_sems, send_sems, recv_sem = rest[-5:]
        i = pl.program_id(0)
        x, y, c = _coords()
        sibling = (x, y, 1 - c)

        def copies(step, slot):
            rows_mine = pl.ds(c * rows2 + step * tr, tr)
            dst = out_ref.at[layer, rows_mine, :]
            keep = pltpu.make_async_copy(buf.at[slot], dst, local_sems.at[slot])
            give = pltpu.make_async_remote_copy(src_ref=buf.at[slot], dst_ref=dst, send_sem=send_sems.at[slot],
                                                recv_sem=recv_sem, device_id=sibling, device_id_type=MESH)
            return keep, give

        def drain(step, slot):
            keep, give = copies(step, slot)
            keep.wait()
            give.wait_send()

        slot = i % 2

        @pl.when(i >= 2)
        def _():
            drain(i - 2, slot)

        acc = g_ref[...].astype(F32)
        for k in range(7):
            acc = acc + got_ref[k].astype(F32)
        buf[slot] = acc
        keep, give = copies(i, slot)
        keep.start()
        give.start()

        @pl.when(i == nt - 1)
        def _():
            if nt >= 2:
                drain(nt - 2, (nt - 2) % 2)
            drain(nt - 1, (nt - 1) % 2)
            theirs = out_ref.at[layer, pl.ds((1 - c) * rows2, rows2), :]
            pltpu.make_async_remote_copy(src_ref=theirs, dst_ref=theirs, send_sem=send_sems.at[0], recv_sem=recv_sem,
                                         device_id=sibling, device_id_type=MESH).wait_recv()

    hbm = pl.BlockSpec(memory_space=pltpu.HBM)
    in_specs = [pl.BlockSpec((tr, cols), own_map), pl.BlockSpec((7, tr, cols), lambda i, ids_: (0, i, 0))]
    operands = [ids, g, got]
    aliases = {}
    if out_prev is not None:
        in_specs.append(hbm)
        operands.append(out_prev)
        aliases = {3: 0}
    grid_spec = pltpu.PrefetchScalarGridSpec(
        num_scalar_prefetch=1, grid=(nt,), in_specs=in_specs, out_specs=hbm,
        scratch_shapes=[pltpu.VMEM((2, tr, cols), F32), pltpu.SemaphoreType.DMA((2,)), pltpu.SemaphoreType.DMA((2,)),
                        pltpu.SemaphoreType.DMA])
    return pl.pallas_call(body, name=name, grid_spec=grid_spec, out_shape=_sds((depth, 2 * rows2, cols), F32),
                          input_output_aliases=aliases, compiler_params=_params(("arbitrary",)))(*operands)


MIX = ("w_proj_conv", "w_proj_attn", "w_out")


class _Schedule:
    def __init__(self, placed, depth, ids):
        self.placed, self.depth, self.ids = placed, depth, ids
        self.full, self.g, self.carried = {}, {}, None
        self.reduced = {k: None for k in BIG}
        first = [(0, "w_in")]
        self._landed(first, _side_call(self._gather(first), name="gather_w_in_0"))

    def _gather(self, keys):
        return _gather_side([self.placed[k] for k in keys], [SHARD_AXIS[k[1]] for k in keys])

    def _landed(self, keys, outs):
        for k, o in zip(keys, outs):
            self.full[k] = o

    def _exchange(self, keys):
        return _exchange_side([self.g[k] for k in keys], [SHARD_AXIS[k[1]] for k in keys])

    def _reduce(self, keys, got):
        for (l, name), pieces in zip(keys, got):
            self.reduced[name] = _rs_sum_join(self.g[(l, name)], pieces, self.reduced[name], l, self.depth,
                                              SHARD_AXIS[name], self.ids, name=f"rs_sum_join_{l}_{name}")

    def weight(self, l, name):
        return self.full[(l, name)]

    def grad(self, l, name, g):
        self.g[(l, name)] = g

    def side(self, kind, l):
        nxt = [(l + 1, "w_in")] if l + 1 < self.depth else []
        if kind == "in_proj":
            keys, make = [(l, k) for k in MIX + ("w_mlp_in",)], self._gather
        elif kind == "attn_fwd":
            keys, make = [(l, "w_mlp_out")] + nxt, self._gather
        elif kind == "attn_bwd":
            keys, make = [(l, k) for k in ("w_mlp_out", "w_mlp_in") + MIX], self._exchange
        else:
            keys, make = [(l, "w_in")], self._exchange
        self.carried = keys
        return make(keys)

    def done(self, kind, l, outs):
        (self._landed if kind in ("in_proj", "attn_fwd") else self._reduce)(self.carried, outs)


def _flat_rows(shape):
    rows = 1
    for s in shape[:-1]:
        rows *= s
    return rows, shape[-1]


def _row_tile(rows, cols, cap_bytes=2 * 1024 * 1024):
    t = rows
    while t * cols * 4 > cap_bytes and t % 16 == 0:
        t //= 2
    return t


def _ada_fwd(c_all, w_ada, b_loc, *, name, tn=512):
    l, d, nl = w_ada.shape
    b = c_all.shape[0]
    tn = min(tn, nl)

    def body(c_ref, w_ref, b_ref, o_ref):
        o_ref[0] = jnp.dot(c_ref[...], w_ref[0], preferred_element_type=F32,
                           precision=lax.Precision.HIGHEST) + b_ref[0]

    return pl.pallas_call(
        body, name=name, grid=(l, nl // tn),
        in_specs=[pl.BlockSpec((b, d), lambda i, j: (0, 0)), pl.BlockSpec((1, d, tn), lambda i, j: (i, 0, j)),
                  pl.BlockSpec((1, 1, tn), lambda i, j: (i, 0, j))],
        out_specs=pl.BlockSpec((1, b, tn), lambda i, j: (i, 0, j)),
        out_shape=_sds((l, b, nl), F32),
        compiler_params=_params(("parallel", "parallel")),
    )(c_all, w_ada, b_loc)


def _ada_bwd(c_t, dmod_loc, *, name, tn=512):
    d, b = c_t.shape
    l, _, nl = dmod_loc.shape
    tn = min(tn, nl)

    def body(c_ref, dm_ref, o_ref):
        cv = c_ref[...]
        dm = dm_ref[0]
        acc = cv[:, 0:1] * dm[0:1, :]
        for k in range(1, b):
            acc = acc + cv[:, k:k + 1] * dm[k:k + 1, :]
        o_ref[0] = acc

    return pl.pallas_call(
        body, name=name, grid=(l, nl // tn),
        in_specs=[pl.BlockSpec((d, b), lambda i, j: (0, 0)), pl.BlockSpec((1, b, tn), lambda i, j: (i, 0, j))],
        out_specs=pl.BlockSpec((1, d, tn), lambda i, j: (i, 0, j)),
        out_shape=_sds((l, d, nl), F32),
        compiler_params=_params(("parallel", "parallel")),
    )(c_t, dmod_loc)


def _sum_devices(p, *, name):
    k, r, c_ = p.shape

    def body(p_ref, o_ref):
        acc = p_ref[0]
        for j in range(1, k):
            acc = acc + p_ref[j]
        o_ref[...] = acc

    return pl.pallas_call(body, name=name, out_shape=_sds((r, c_), F32),
                          in_specs=[pl.BlockSpec(memory_space=pltpu.VMEM)],
                          out_specs=pl.BlockSpec(memory_space=pltpu.VMEM))(p)


def _adamw(w, g, m, v, *, name):
    shape = w.shape
    rows, cols = _flat_rows(shape)
    tr = _row_tile(rows, cols, cap_bytes=1024 * 1024)
    c1 = 1.0 / (1.0 - ADAM_B1 ** ADAM_STEP)
    c2 = 1.0 / (1.0 - ADAM_B2 ** ADAM_STEP)

    def body(w_ref, g_ref, m_ref, v_ref, d_ref, nm_ref, nv_ref):
        gv = g_ref[...]
        nm = ADAM_B1 * m_ref[...] + (1.0 - ADAM_B1) * gv
        nv = ADAM_B2 * v_ref[...] + (1.0 - ADAM_B2) * (gv * gv)
        m_hat = nm * c1
        v_hat = nv * c2
        d_ref[...] = -ADAM_LR * (m_hat / (jnp.sqrt(v_hat) + ADAM_EPS) + ADAM_WD * w_ref[...])
        nm_ref[...] = nm
        nv_ref[...] = nv

    spec = pl.BlockSpec((tr, cols), lambda i: (i, 0))
    flat = lambda a: a.reshape(rows, cols)
    outs = pl.pallas_call(body, name=name, grid=(rows // tr,), in_specs=[spec] * 4, out_specs=[spec] * 3,
                          out_shape=[_sds((rows, cols), F32)] * 3, compiler_params=_params(("parallel",)),
                          )(flat(w), flat(g), flat(m), flat(v))
    return tuple(o.reshape(shape) for o in outs)


WEIGHTS = ("w_ada", "b_ada", "g_pre_mix", "g_post_mix", "g_pre_mlp", "g_post_mlp", "w_in", "conv_w",
           "w_proj_conv", "w_proj_attn", "w_out", "w_mlp_in", "w_mlp_out")
GAINS = ("g_pre_mix", "g_post_mix", "g_pre_mlp", "g_post_mlp")


def kernel(x, c, w_ada, b_ada, g_pre_mix, g_post_mix, g_pre_mlp, g_post_mlp, w_in, conv_w, w_proj_conv, w_proj_attn, w_out, w_mlp_in, w_mlp_out, loss_target, m_w_ada, m_b_ada, m_g_pre_mix, m_g_post_mix, m_g_pre_mlp, m_g_post_mlp, m_w_in, m_conv_w, m_w_proj_conv, m_w_proj_attn, m_w_out, m_w_mlp_in, m_w_mlp_out, v_w_ada, v_b_ada, v_g_pre_mix, v_g_post_mix, v_g_pre_mlp, v_g_post_mlp, v_w_in, v_conv_w, v_w_proj_conv, v_w_proj_attn, v_w_out, v_w_mlp_in, v_w_mlp_out):
    params = dict(w_ada=w_ada, b_ada=b_ada, g_pre_mix=g_pre_mix, g_post_mix=g_post_mix, g_pre_mlp=g_pre_mlp,
                  g_post_mlp=g_post_mlp, w_in=w_in, conv_w=conv_w, w_proj_conv=w_proj_conv, w_proj_attn=w_proj_attn,
                  w_out=w_out, w_mlp_in=w_mlp_in, w_mlp_out=w_mlp_out)
    m_in = dict(w_ada=m_w_ada, b_ada=m_b_ada, g_pre_mix=m_g_pre_mix, g_post_mix=m_g_post_mix, g_pre_mlp=m_g_pre_mlp,
                g_post_mlp=m_g_post_mlp, w_in=m_w_in, conv_w=m_conv_w, w_proj_conv=m_w_proj_conv,
                w_proj_attn=m_w_proj_attn, w_out=m_w_out, w_mlp_in=m_w_mlp_in, w_mlp_out=m_w_mlp_out)
    v_in = dict(w_ada=v_w_ada, b_ada=v_b_ada, g_pre_mix=v_g_pre_mix, g_post_mix=v_g_post_mix, g_pre_mlp=v_g_pre_mlp,
                g_post_mlp=v_g_post_mlp, w_in=v_w_in, conv_w=v_conv_w, w_proj_conv=v_w_proj_conv,
                w_proj_attn=v_w_proj_attn, w_out=v_w_out, w_mlp_in=v_w_mlp_in, w_mlp_out=v_w_mlp_out)

    depth, d, nl_ada = w_ada.shape
    ix, iy, ic = lax.axis_index("x"), lax.axis_index("y"), lax.axis_index("c")
    chip = 2 * ix + iy
    me = 4 * ix + 2 * iy + ic
    xs = x[0]
    target = loss_target[0]

    c_all = _all_gather_small(jnp.broadcast_to(c, (8, d)), name="gather_c")[:, 0, :]
    b_loc = lax.dynamic_slice_in_dim(b_ada, chip * nl_ada, nl_ada, axis=1)[:, None, :]
    mod_loc = _ada_fwd(c_all, w_ada, b_loc, name="ada_fwd")
    mod_all = _all_gather_small(mod_loc.reshape(depth * 8, nl_ada), name="gather_mod")
    mod_all = mod_all.reshape(4, 2, depth, 8, nl_ada)[:, 0]
    mod_me = lax.dynamic_index_in_dim(mod_all, me, axis=2, keepdims=False)
    mods = jnp.transpose(mod_me, (1, 0, 2)).reshape(depth, N_MOD, d)

    chip_arr = jnp.reshape(chip, (1,)).astype(jnp.int32)
    ids = jnp.stack([chip, ic]).astype(jnp.int32)
    placed = {(l, k): _cast_place(params[k], l, SHARD_AXIS[k], chip_arr, name=f"place_{k}_{l}")
              for l in range(depth) for k in BIG}
    conv_full = _all_gather_small(
        jnp.pad(conv_w.reshape(depth * 3, -1), ((0, 8 - depth * 3), (0, 0))), name="gather_conv_w")
    conv_full = conv_full.reshape(4, 2, 8, -1)[:, 0, :depth * 3]
    conv_full = jnp.transpose(conv_full, (1, 0, 2)).reshape(depth, 3, -1)

    gains = jnp.stack([params[k] for k in GAINS], axis=1)
    schedule = _Schedule(placed, depth, ids)
    loss, dx, conv_grads, dmods, dgains = _local_step(xs, target, mods, gains, conv_full, schedule)

    cw = conv_full.shape[2]
    rows = [dmods.reshape(depth * N_MOD, d), dgains.reshape(depth * 4, d),
            conv_grads.reshape(-1, d), jnp.broadcast_to(loss, (1, d))]
    payload = jnp.concatenate(rows, axis=0)
    n_rows = payload.shape[0]
    pad = (-n_rows) % 8
    payload = jnp.pad(payload, ((0, pad), (0, 0)))
    everyone = _all_gather_small(payload, name="gather_small_grads")
    total = _sum_devices(everyone, name="sum_small_grads")
    r0 = depth * N_MOD
    grads = {}
    grads["b_ada"] = total[:r0].reshape(depth, N_MOD * d)
    gsum = total[r0:r0 + depth * 4].reshape(depth, 4, d)
    for k, name in enumerate(GAINS):
        grads[name] = gsum[:, k]
    r1 = r0 + depth * 4
    n_conv = (depth * 3 * cw) // d
    conv_g = total[r1:r1 + n_conv].reshape(depth, 3, cw)
    grads["conv_w"] = lax.dynamic_slice_in_dim(conv_g, chip * (cw // 4), cw // 4, axis=2)
    loss_out = total[r1 + n_conv, 0]
    dmod_all = everyone[:, :r0].reshape(8, depth, N_MOD * d)
    dmod_loc = lax.dynamic_slice_in_dim(dmod_all, chip * nl_ada, nl_ada, axis=2)
    grads["w_ada"] = _ada_bwd(c_all.T, jnp.transpose(dmod_loc, (1, 0, 2)), name="ada_bwd")

    for k in BIG:
        grads[k] = schedule.reduced[k]

    deltas, new_m, new_v = {}, {}, {}
    for k in WEIGHTS:
        deltas[k], new_m[k], new_v[k] = _adamw(params[k], grads[k], m_in[k], v_in[k], name=f"adamw_{k}")

    return (loss_out, dx[None], *[grads[k] for k in WEIGHTS], *[deltas[k] for k in WEIGHTS],
            *[new_m[k] for k in WEIGHTS], *[new_v[k] for k in WEIGHTS])
```

```python
import functools

import jax
import jax.numpy as jnp
from jax import lax
from jax.experimental import pallas as pl
from jax.experimental.pallas import tpu as pltpu

F32 = jnp.float32
BF16 = jnp.bfloat16
EPS = 1e-6
N_MOD = 6
HEAD_DIM = 64
LANES = 128
ATTN_SCALE = 1.0 / 8.0
UNDERFLOW_LOG = -90.0
ATTN_BLOCK = 128
ATTN_CHAINS = 8
ADAM_LR = 0.001
ADAM_B1 = 0.9
ADAM_B2 = 0.999
ADAM_EPS = 1e-08
ADAM_WD = 0.01
ADAM_STEP = 10
VMEM_LIMIT = 56 * 1024 * 1024
MESH = pl.DeviceIdType.MESH
OTHER_CHIPS = ((1, 0), (0, 1), (1, 1))

_NT = (((1,), (1,)), ((), ()))
_TN = (((0,), (0,)), ((), ()))


def _sds(shape, dtype):
    return jax.ShapeDtypeStruct(shape, dtype)


def _params(sem):
    return pltpu.CompilerParams(dimension_semantics=sem, vmem_limit_bytes=VMEM_LIMIT)


def _fit(t, n):
    t = min(t, n)
    while n % t:
        t //= 2
    return t


def _vec_spec(d, nargs):
    if nargs == 1:
        return pl.BlockSpec((1, d), lambda i: (0, 0))
    return pl.BlockSpec((1, d), lambda i, j: (0, 0))


def _log_one_minus_sigmoid(z):
    return -jnp.log(1.0 + jnp.exp(-jnp.abs(z))) - jnp.maximum(z, 0.0)


def _sigmoid(z):
    t = jnp.exp(-jnp.abs(z))
    return jnp.where(z >= 0.0, 1.0, t) / (1.0 + t)


def _split_bf16(a):
    hi = a.astype(BF16)
    lo = (a - hi.astype(F32)).astype(BF16)
    return hi, lo


def _rms_bwd(dn, xin, g):
    r = lax.rsqrt(jnp.mean(xin * xin, axis=-1, keepdims=True) + EPS)
    xh = xin * r
    dxh = dn * g
    dxin = r * (dxh - xh * jnp.mean(dxh * xh, axis=-1, keepdims=True))
    return dxin, xh


def _colsum(a):
    return jnp.sum(a, axis=0, keepdims=True)


def _norm_mod_matmul(x, g, sc, sh, w, *, name, tm=256, side=None):
    s, d = x.shape
    n = w.shape[1]
    tm = _fit(tm, s)
    nt = s // tm

    def body(*refs):
        i = pl.program_id(0)
        (x_ref, g_ref, sc_ref, sh_ref, w_ref, h_ref, o_ref), late_phases = _host_side(
            side, 5, 2, 0, refs, i == 0, i == (3 * nt) // 4, i == nt - 1)
        xv = x_ref[...]
        r = lax.rsqrt(jnp.mean(xv * xv, axis=-1, keepdims=True) + EPS)
        h = ((xv * r * g_ref[...]) * (1.0 + sc_ref[...]) + sh_ref[...]).astype(BF16)
        h_ref[...] = h
        o_ref[...] = jnp.dot(h, w_ref[...], preferred_element_type=F32).astype(BF16)
        late_phases()

    s_in, s_out, s_shapes, aliases, s_scratch = _side_specs(side, 5, 2)
    res = pl.pallas_call(
        body, name=name, grid=(nt,),
        in_specs=[pl.BlockSpec((tm, d), lambda i: (i, 0)), _vec_spec(d, 1), _vec_spec(d, 1), _vec_spec(d, 1),
                  pl.BlockSpec((d, n), lambda i: (0, 0))] + s_in,
        out_specs=[pl.BlockSpec((tm, d), lambda i: (i, 0)), pl.BlockSpec((tm, n), lambda i: (i, 0))] + s_out,
        out_shape=[_sds((s, d), BF16), _sds((s, n), BF16)] + s_shapes,
        input_output_aliases=aliases, scratch_shapes=s_scratch,
        compiler_params=_params(("arbitrary",)),
    )(x, g, sc, sh, w, *([] if side is None else side.operands))
    return res[:2], res[2:]


HALO = 16


def _conv_fwd(proj, conv_w, *, name, tm=512):
    s = proj.shape[0]
    cw = conv_w.shape[1]
    tm = min(tm, s)
    nb = tm // HALO

    def body(bg_ref, cg_ref, u_ref, cgh_ref, uh_ref, w_ref, yc_ref, vbuf):
        i = pl.program_id(0)
        vv = cg_ref[...].astype(F32) * u_ref[...].astype(F32)
        halo = cgh_ref[...].astype(F32) * uh_ref[...].astype(F32)
        vbuf[0:HALO, :] = jnp.where(i > 0, halo, 0.0)
        vbuf[HALO:HALO + tm, :] = vv
        v1 = vbuf[HALO - 1:HALO - 1 + tm, :]
        v2 = vbuf[HALO - 2:HALO - 2 + tm, :]
        w = w_ref[...]
        y = w[2:3, :] * vv + w[1:2, :] * v1 + w[0:1, :] * v2
        yc_ref[...] = (bg_ref[...].astype(F32) * y).astype(BF16)

    def prev(i):
        return jnp.maximum(i * nb - 1, 0)

    return pl.pallas_call(
        body, name=name, grid=(s // tm,),
        in_specs=[pl.BlockSpec((tm, cw), lambda i: (i, 0)), pl.BlockSpec((tm, cw), lambda i: (i, 1)),
                  pl.BlockSpec((tm, cw), lambda i: (i, 2)),
                  pl.BlockSpec((HALO, cw), lambda i: (prev(i), 1)), pl.BlockSpec((HALO, cw), lambda i: (prev(i), 2)),
                  pl.BlockSpec((3, cw), lambda i: (0, 0))],
        out_specs=pl.BlockSpec((tm, cw), lambda i: (i, 0)),
        out_shape=_sds((s, cw), BF16),
        scratch_shapes=[pltpu.VMEM((HALO + tm, cw), F32)],
        compiler_params=_params(("arbitrary",)),
    )(proj, proj, proj, proj, proj, conv_w)


def _tri(qb):
    r = lax.broadcasted_iota(jnp.int32, (qb, qb), 0)
    c = lax.broadcasted_iota(jnp.int32, (qb, qb), 1)
    return (r >= c).astype(BF16)


def _head_mask(h):
    lane = lax.broadcasted_iota(jnp.int32, (1, LANES), 1)
    return (lane >= HEAD_DIM * h) & (lane < HEAD_DIM * (h + 1))


def _stack_heads(a, masks):
    return jnp.concatenate([jnp.where(m, a, 0).astype(BF16) for m in masks], axis=0)


def _heads_to_lanes(a, qb):
    return jnp.concatenate([a[:qb], a[qb:]], axis=1)


def _stacked_causal(qb, width, first_key, first_query):
    row = lax.broadcasted_iota(jnp.int32, (2 * qb, width), 0)
    col = lax.broadcasted_iota(jnp.int32, (2 * qb, width), 1)
    return first_key + col < first_query + jnp.where(row >= qb, row - qb, row)


def _running_sum(a, tri_m):
    rows, qb = a.shape[0], tri_m.shape[0]
    n = a.shape[1] // qb
    hi, lo = _split_bf16(a)
    stacked = jnp.concatenate([p[:, s * qb:(s + 1) * qb] for s in range(n) for p in (hi, lo)], axis=0)
    both = jnp.dot(stacked, tri_m, preferred_element_type=F32)
    parts = [both[(2 * s) * rows:(2 * s + 1) * rows] + both[(2 * s + 1) * rows:(2 * s + 2) * rows] for s in range(n)]
    later = None
    for s in reversed(range(n)):
        if later is not None:
            parts[s] = parts[s] + later
        later = parts[s][:, 0:1]
    return (parts[0] if n == 1 else jnp.concatenate(parts, axis=1)), later


def _attn_cols(d):
    cw = d // 2
    hp = (d // 2) // LANES
    q0 = (3 * cw) // LANES
    return q0, q0 + hp, q0 + 2 * hp, hp


class _Side:
    def __init__(self, operands, out_shapes, aliases, scratch, start, mid, finish):
        self.operands, self.out_shapes, self.aliases, self.scratch = list(operands), list(out_shapes), aliases, list(scratch)
        self.start, self.mid, self.finish = start, mid, finish


def _side_call(side, *, name):
    n_in, n_out = len(side.operands), len(side.out_shapes)

    def body(*refs):
        parts = refs[:n_in], refs[n_in:n_in + n_out], refs[n_in + n_out:]
        side.start(*parts)
        if side.mid is not None:
            side.mid(*parts)
        side.finish(*parts)

    hbm = pl.BlockSpec(memory_space=pltpu.HBM)
    return pl.pallas_call(body, name=name, in_specs=[hbm] * n_in, out_specs=[hbm] * n_out, out_shape=side.out_shapes,
                          input_output_aliases=dict(side.aliases), scratch_shapes=side.scratch)(*side.operands)


def _host_side(side, n_in, n_out, n_scratch, refs, first, late, last):
    if side is None:
        return refs, lambda: None
    s_in, s_out = len(side.operands), len(side.out_shapes)
    ins = refs[:n_in]
    side_in = refs[n_in:n_in + s_in]
    outs = refs[n_in + s_in:n_in + s_in + n_out]
    side_out = refs[n_in + s_in + n_out:n_in + s_in + n_out + s_out]
    rest = refs[n_in + s_in + n_out + s_out:]
    scratch, sems = rest[:n_scratch], rest[n_scratch:]
    parts = (side_in, side_out, sems)
    pl.when(first)(lambda: side.start(*parts))

    def run_late_phases():
        if side.mid is not None:
            pl.when(late)(lambda: side.mid(*parts))
        pl.when(last)(lambda: side.finish(*parts))

    return (*ins, *outs, *scratch), run_late_phases


def _side_specs(side, n_in, n_out):
    if side is None:
        return [], [], [], {}, []
    hbm = pl.BlockSpec(memory_space=pltpu.HBM)
    s_in = len(side.operands)
    aliases = {n_in + a: n_out + b for a, b in side.aliases.items()}
    return [hbm] * s_in, [hbm] * len(side.out_shapes), side.out_shapes, aliases, side.scratch


def _attn_fwd(proj, tri, *, d, name, side=None):
    s = proj.shape[0]
    qb = tri.shape[0]
    chains = _fit(ATTN_CHAINS, s // qb)
    ng = s // (qb * chains)
    q0, k0, v0, hp = _attn_cols(d)

    def body(*refs):
        p, g = pl.program_id(0), pl.program_id(1)
        (q_ref, k_ref, v_ref, tri_ref, o_ref), late_phases = _host_side(
            side, 4, 1, 0, refs, (p == 0) & (g == 0), (p == hp - 1) & (g == 0), (p == hp - 1) & (g == ng - 1))
        tri_m = tri_ref[...]
        masks = [_head_mask(h) for h in range(2)]

        def first_step(u):
            i = g * chains + u
            qs = _stack_heads(q_ref[u * qb:(u + 1) * qb, :] * ATTN_SCALE, masks)

            def strip(first_key, width, state, causal):
                run, acc = state
                rows = pl.ds(pl.multiple_of(first_key, qb), width)
                z = lax.dot_general(qs, k_ref[rows, :], _NT, preferred_element_type=F32)
                lg = _log_one_minus_sigmoid(z)
                if causal is not None:
                    lg = jnp.where(causal, lg, 0.0)
                cs, total = _running_sum(lg, tri_m)
                a = jnp.exp(z + cs + run)
                if causal is not None:
                    a = jnp.where(causal, a, 0.0)
                ab = a.astype(BF16)
                acc = acc + jnp.dot(_heads_to_lanes(ab, qb), _stack_heads(v_ref[rows, :], masks),
                                    preferred_element_type=F32)
                return run + total, acc

            first_key = jnp.maximum(i - 1, 0) * qb
            state = strip(first_key, 2 * qb, (jnp.zeros((2 * qb, 1), F32), jnp.zeros((qb, LANES), F32)),
                          _stacked_causal(qb, 2 * qb, first_key, i * qb))
            return i, strip, state

        started = [first_step(u) for u in range(chains)]
        for u, (i, strip, state) in enumerate(started):
            state = lax.while_loop(
                lambda st: (st[0] >= 0) & (jnp.max(st[1]) > UNDERFLOW_LOG),
                lambda st, strip=strip: (st[0] - 1, *strip(st[0] * qb, qb, st[1:], None)),
                (i - 2, *state))
            o_ref[u * qb:(u + 1) * qb, :] = state[2]
        late_phases()

    s_in, s_out, s_shapes, aliases, s_scratch = _side_specs(side, 4, 1)
    tq = qb * chains
    res = pl.pallas_call(
        body, name=name, grid=(hp, ng),
        in_specs=[pl.BlockSpec((tq, LANES), lambda p, i: (i, q0 + p)),
                  pl.BlockSpec((s, LANES), lambda p, i: (0, k0 + p)),
                  pl.BlockSpec((s, LANES), lambda p, i: (0, v0 + p)),
                  pl.BlockSpec((qb, qb), lambda p, i: (0, 0))] + s_in,
        out_specs=[pl.BlockSpec((tq, LANES), lambda p, i: (i, p))] + s_out,
        out_shape=[_sds((s, hp * LANES), F32)] + s_shapes,
        input_output_aliases=aliases, scratch_shapes=s_scratch,
        compiler_params=_params(("arbitrary", "arbitrary")),
    )(proj, proj, proj, tri, *([] if side is None else side.operands))
    return res[0], res[1:]


def _mix_out(yc, o, proj, x, wpc, wpa, wout, g, gt, *, name, tm=256):
    s, d = x.shape
    cw = yc.shape[1]
    tm = min(tm, s)
    ga_blk = (3 * cw + 3 * (d // 2)) // d

    def body(yc_ref, o_ref, ga_ref, gb_ref, x_ref, wpc_ref, wpa_ref, wout_ref, g_ref, gt_ref,
             ycv_ref, yat_ref, mg_ref, mix_ref, x1_ref):
        y_conv = jnp.dot(yc_ref[...], wpc_ref[...], preferred_element_type=F32)
        y_attn = jnp.dot(o_ref[...].astype(BF16), wpa_ref[...], preferred_element_type=F32)
        merged = (_sigmoid(ga_ref[...].astype(F32)) * y_conv + _sigmoid(gb_ref[...].astype(F32)) * y_attn)
        mg = merged.astype(BF16)
        mix = jnp.dot(mg, wout_ref[...], preferred_element_type=F32)
        r = lax.rsqrt(jnp.mean(mix * mix, axis=-1, keepdims=True) + EPS)
        ycv_ref[...] = y_conv.astype(BF16)
        yat_ref[...] = y_attn.astype(BF16)
        mg_ref[...] = mg
        mix_ref[...] = mix
        x1_ref[...] = x_ref[...] + gt_ref[...] * (mix * r * g_ref[...])

    def rows(w):
        return pl.BlockSpec((tm, w), lambda i: (i, 0))

    def full(a):
        return pl.BlockSpec(a.shape, lambda i: (0, 0))

    return pl.pallas_call(
        body, name=name, grid=(s // tm,),
        in_specs=[rows(cw), rows(d // 2), pl.BlockSpec((tm, d), lambda i: (i, ga_blk)),
                  pl.BlockSpec((tm, d), lambda i: (i, ga_blk + 1)), rows(d),
                  full(wpc), full(wpa), full(wout), _vec_spec(d, 1), _vec_spec(d, 1)],
        out_specs=[rows(d), rows(d), rows(d), rows(d), rows(d)],
        out_shape=[_sds((s, d), BF16), _sds((s, d), BF16), _sds((s, d), BF16), _sds((s, d), F32), _sds((s, d), F32)],
        compiler_params=_params(("parallel",)),
    )(yc, o, proj, proj, x, wpc, wpa, wout, g, gt)


def _relu2(a):
    r = jnp.maximum(a.astype(F32), 0.0)
    return (r * r).astype(BF16)


def _mlp_out(a, x, w2, g, gt, *, name, tm=512):
    s, d = x.shape
    dff = a.shape[1]
    tm = min(tm, s)

    def body(a_ref, x_ref, w_ref, g_ref, gt_ref, ff_ref, x2_ref):
        ff = jnp.dot(_relu2(a_ref[...]), w_ref[...], preferred_element_type=F32)
        r = lax.rsqrt(jnp.mean(ff * ff, axis=-1, keepdims=True) + EPS)
        ff_ref[...] = ff
        x2_ref[...] = x_ref[...] + gt_ref[...] * (ff * r * g_ref[...])

    return pl.pallas_call(
        body, name=name, grid=(s // tm,),
        in_specs=[pl.BlockSpec((tm, dff), lambda i: (i, 0)), pl.BlockSpec((tm, d), lambda i: (i, 0)),
                  pl.BlockSpec((dff, d), lambda i: (0, 0)), _vec_spec(d, 1), _vec_spec(d, 1)],
        out_specs=[pl.BlockSpec((tm, d), lambda i: (i, 0)), pl.BlockSpec((tm, d), lambda i: (i, 0))],
        out_shape=[_sds((s, d), F32), _sds((s, d), F32)],
        compiler_params=_params(("parallel",)),
    )(a, x, w2, g, gt)


def _loss_grad(y, target, *, name, tm=512):
    s, d = y.shape
    tm = min(tm, s)

    def body(y_ref, t_ref, dy_ref, loss_ref):
        @pl.when(pl.program_id(0) == 0)
        def _():
            loss_ref[...] = jnp.zeros_like(loss_ref)
        e = y_ref[...] - t_ref[...]
        dy_ref[...] = e * (1.0 / d)
        loss_ref[...] += 0.5 * jnp.sum(jnp.mean(e * e, axis=-1, keepdims=True), axis=0, keepdims=True)

    return pl.pallas_call(
        body, name=name, grid=(s // tm,),
        in_specs=[pl.BlockSpec((tm, d), lambda i: (i, 0)), pl.BlockSpec((tm, d), lambda i: (i, 0))],
        out_specs=[pl.BlockSpec((tm, d), lambda i: (i, 0)), pl.BlockSpec((1, 1), lambda i: (0, 0))],
        out_shape=[_sds((s, d), F32), _sds((1, 1), F32)],
        compiler_params=_params(("arbitrary",)),
    )(y, target)


def _mlp_out_bwd(dx, ff, a, w2, g, gt, *, name, tm=256):
    s, d = dx.shape
    dff = a.shape[1]
    tm = min(tm, s)

    def body(dx_ref, ff_ref, a_ref, w_ref, g_ref, gt_ref, dff_ref, da_ref, dgt_ref, dg_ref):
        @pl.when(pl.program_id(0) == 0)
        def _():
            dgt_ref[...] = jnp.zeros_like(dgt_ref)
            dg_ref[...] = jnp.zeros_like(dg_ref)
        dxv = dx_ref[...]
        dn = dxv * gt_ref[...]
        dffv, xh = _rms_bwd(dn, ff_ref[...], g_ref[...])
        dgt_ref[...] += _colsum(dxv * (xh * g_ref[...]))
        dg_ref[...] += _colsum(dn * xh)
        dffb = dffv.astype(BF16)
        dff_ref[...] = dffb
        drr = lax.dot_general(dffb, w_ref[...], _NT, preferred_element_type=F32)
        da_ref[...] = (drr * (2.0 * jnp.maximum(a_ref[...].astype(F32), 0.0))).astype(BF16)

    return pl.pallas_call(
        body, name=name, grid=(s // tm,),
        in_specs=[pl.BlockSpec((tm, d), lambda i: (i, 0)), pl.BlockSpec((tm, d), lambda i: (i, 0)),
                  pl.BlockSpec((tm, dff), lambda i: (i, 0)), pl.BlockSpec((dff, d), lambda i: (0, 0)),
                  _vec_spec(d, 1), _vec_spec(d, 1)],
        out_specs=[pl.BlockSpec((tm, d), lambda i: (i, 0)), pl.BlockSpec((tm, dff), lambda i: (i, 0)),
                   _vec_spec(d, 1), _vec_spec(d, 1)],
        out_shape=[_sds((s, d), BF16), _sds((s, dff), BF16), _sds((1, d), F32), _sds((1, d), F32)],
        compiler_params=_params(("arbitrary",)),
    )(dx, ff, a, w2, g, gt)


def _matmul_nt_norm_bwd(dys, w, x, dres, g, sc, *, name, tm=256, side=None):
    s = dys[0].shape[0]
    widths = [dy.shape[1] for dy in dys]
    d, n = w.shape
    assert sum(widths) == n, (widths, n)
    tm = _fit(tm, s)
    nt = s // tm
    np_ = len(dys)

    def body(*refs):
        i = pl.program_id(0)
        own, late_phases = _host_side(side, np_ + 5, 4, 0, refs, i == 0, i == (3 * nt) // 4, i == nt - 1)
        dy_refs = own[:np_]
        w_ref, x_ref, dres_ref, g_ref, sc_ref, dx_ref, dsh_ref, dsc_ref, dg_ref = own[np_:]

        @pl.when(i == 0)
        def _():
            dsh_ref[...] = jnp.zeros_like(dsh_ref)
            dsc_ref[...] = jnp.zeros_like(dsc_ref)
            dg_ref[...] = jnp.zeros_like(dg_ref)

        dh = None
        for p, dy_ref in enumerate(dy_refs):
            cols = slice(sum(widths[:p]), sum(widths[:p + 1]))
            part = lax.dot_general(dy_ref[...], w_ref[:, cols], _NT, preferred_element_type=F32)
            dh = part if dh is None else dh + part
        dn = dh * (1.0 + sc_ref[...])
        dxin, xh = _rms_bwd(dn, x_ref[...], g_ref[...])
        dsh_ref[...] += _colsum(dh)
        dsc_ref[...] += _colsum(dh * (xh * g_ref[...]))
        dg_ref[...] += _colsum(dn * xh)
        dx_ref[...] = dres_ref[...] + dxin
        late_phases()

    s_in, s_out, s_shapes, aliases, s_scratch = _side_specs(side, np_ + 5, 4)
    res = pl.pallas_call(
        body, name=name, grid=(nt,),
        in_specs=[pl.BlockSpec((tm, wd), lambda i: (i, 0)) for wd in widths]
        + [pl.BlockSpec((d, n), lambda i: (0, 0)),
           pl.BlockSpec((tm, d), lambda i: (i, 0)), pl.BlockSpec((tm, d), lambda i: (i, 0)),
           _vec_spec(d, 1), _vec_spec(d, 1)] + s_in,
        out_specs=[pl.BlockSpec((tm, d), lambda i: (i, 0)), _vec_spec(d, 1), _vec_spec(d, 1), _vec_spec(d, 1)] + s_out,
        out_shape=[_sds((s, d), F32), _sds((1, d), F32), _sds((1, d), F32), _sds((1, d), F32)] + s_shapes,
        input_output_aliases=aliases, scratch_shapes=s_scratch,
        compiler_params=_params(("arbitrary",)),
    )(*dys, w, x, dres, g, sc, *([] if side is None else side.operands))
    return res[:4], res[4:]


def _matmul_tn(a, bs, *, name, tk=1024, tn=1024, ts=512, relu2=False, into=None, col0=0, n_total=None):
    s, k = a.shape
    widths = [b.shape[1] for b in bs]
    n = sum(widths)
    tk, ts = _fit(tk, k), _fit(ts, s)
    for w in widths:
        tn = _fit(tn, w)
    while col0 % tn:
        tn //= 2
    nt = s // ts
    assert tn % LANES == 0 and all(sum(widths[:p]) % tn == 0 for p in range(len(bs))), (widths, tn)
    first = [sum(widths[:p]) // tn for p in range(len(bs))]
    tiles = [w // tn for w in widths]
    tile0 = col0 // tn

    def body(a_ref, *rest):
        b_refs, o_ref, acc = rest[:len(bs)], rest[-2], rest[-1]
        j, t = pl.program_id(1), pl.program_id(2)

        @pl.when(t == 0)
        def _():
            acc[...] = jnp.zeros_like(acc)
        av = a_ref[...]
        av = _relu2(av) if relu2 else av.astype(BF16)
        for p, b_ref in enumerate(b_refs):
            def add(b_ref=b_ref):
                acc[...] += lax.dot_general(av, b_ref[...], _TN, preferred_element_type=F32)
            if len(bs) == 1:
                add()
            else:
                pl.when((j >= first[p]) & (j < first[p] + tiles[p]))(add)

        @pl.when(t == nt - 1)
        def _():
            o_ref[...] = acc[...].astype(BF16)

    def piece_spec(p):
        def index(i, j, t):
            mine = (j >= first[p]) & (j < first[p] + tiles[p])
            return jnp.where(mine, t, 0), jnp.where(mine, j - first[p], 0)
        return pl.BlockSpec((ts, tn), index)

    operands, extra_specs, aliases = [a, *bs], [], {}
    if into is not None:
        operands.append(into)
        extra_specs = [pl.BlockSpec(memory_space=pltpu.HBM)]
        aliases = {len(operands) - 1: 0}
    return pl.pallas_call(
        body, name=name, grid=(k // tk, n // tn, nt),
        in_specs=[pl.BlockSpec((ts, tk), lambda i, j, t: (t, i))] + [piece_spec(p) for p in range(len(bs))] + extra_specs,
        out_specs=pl.BlockSpec((tk, tn), lambda i, j, t: (i, tile0 + j)),
        out_shape=_sds((k, n_total or n), BF16),
        input_output_aliases=aliases,
        scratch_shapes=[pltpu.VMEM((tk, tn), F32)],
        compiler_params=_params(("parallel", "parallel", "arbitrary")),
    )(*operands)


def _mix_out_bwd(dx, mix, proj, ycv, yat, wout, wpc, wpa, g, gt, *, name, tm=256):
    s, d = dx.shape
    cw = wpc.shape[0]
    aw = wpa.shape[0]
    tm = min(tm, s)
    ga_blk = (3 * cw + 3 * aw) // d

    def body(dx_ref, mix_ref, ga_ref, gb_ref, ycv_ref, yat_ref, wout_ref, wpc_ref, wpa_ref, g_ref, gt_ref,
             dmix_ref, dycv_ref, dyat_ref, dyc_ref, do_ref, dgate_ref, dgt_ref, dg_ref):
        @pl.when(pl.program_id(0) == 0)
        def _():
            dgt_ref[...] = jnp.zeros_like(dgt_ref)
            dg_ref[...] = jnp.zeros_like(dg_ref)
        dxv = dx_ref[...]
        dn = dxv * gt_ref[...]
        dmix, xh = _rms_bwd(dn, mix_ref[...], g_ref[...])
        dgt_ref[...] += _colsum(dxv * (xh * g_ref[...]))
        dg_ref[...] += _colsum(dn * xh)
        dmixb = dmix.astype(BF16)
        dmix_ref[...] = dmixb
        dmerged = lax.dot_general(dmixb, wout_ref[...], _NT, preferred_element_type=F32)
        sga = _sigmoid(ga_ref[...].astype(F32))
        sgb = _sigmoid(gb_ref[...].astype(F32))
        dycv = (dmerged * sga).astype(BF16)
        dyat = (dmerged * sgb).astype(BF16)
        dycv_ref[...] = dycv
        dyat_ref[...] = dyat
        dgate_ref[:, 0:d] = (dmerged * ycv_ref[...].astype(F32) * (sga * (1.0 - sga))).astype(BF16)
        dgate_ref[:, d:2 * d] = (dmerged * yat_ref[...].astype(F32) * (sgb * (1.0 - sgb))).astype(BF16)
        dyc_ref[...] = lax.dot_general(dycv, wpc_ref[...], _NT, preferred_element_type=F32).astype(BF16)
        do_ref[...] = lax.dot_general(dyat, wpa_ref[...], _NT, preferred_element_type=F32).astype(BF16)

    def rows(w):
        return pl.BlockSpec((tm, w), lambda i: (i, 0))

    def full(a):
        return pl.BlockSpec(a.shape, lambda i: (0, 0))

    return pl.pallas_call(
        body, name=name, grid=(s // tm,),
        in_specs=[rows(d), rows(d), pl.BlockSpec((tm, d), lambda i: (i, ga_blk)),
                  pl.BlockSpec((tm, d), lambda i: (i, ga_blk + 1)), rows(d), rows(d),
                  full(wout), full(wpc), full(wpa), _vec_spec(d, 1), _vec_spec(d, 1)],
        out_specs=[rows(d), rows(d), rows(d), rows(cw), rows(aw), rows(2 * d), _vec_spec(d, 1), _vec_spec(d, 1)],
        out_shape=[_sds((s, d), BF16), _sds((s, d), BF16), _sds((s, d), BF16), _sds((s, cw), BF16),
                   _sds((s, aw), BF16), _sds((s, 2 * d), BF16), _sds((1, d), F32), _sds((1, d), F32)],
        compiler_params=_params(("arbitrary",)),
    )(dx, mix, proj, proj, ycv, yat, wout, wpc, wpa, g, gt)


def _conv_bwd(dyc, proj, conv_w, *, name, tm=512):
    s = proj.shape[0]
    cw = conv_w.shape[1]
    tm = min(tm, s)
    nb = tm // HALO
    nt = s // tm
    last_blk = s // HALO - 1

    def body(dyc_ref, bg_ref, cg_ref, u_ref, cgh_ref, uh_ref, dych_ref, bgh_ref, w_ref,
             dconv_ref, dw_ref, vbuf, gbuf):
        i = pl.program_id(0)

        @pl.when(i == 0)
        def _():
            dw_ref[...] = jnp.zeros_like(dw_ref)

        cg = cg_ref[...].astype(F32)
        u = u_ref[...].astype(F32)
        vv = cg * u
        halo = cgh_ref[...].astype(F32) * uh_ref[...].astype(F32)
        vbuf[0:HALO, :] = jnp.where(i > 0, halo, 0.0)
        vbuf[HALO:HALO + tm, :] = vv
        v1 = vbuf[HALO - 1:HALO - 1 + tm, :]
        v2 = vbuf[HALO - 2:HALO - 2 + tm, :]
        w = w_ref[...]
        y = w[2:3, :] * vv + w[1:2, :] * v1 + w[0:1, :] * v2
        dyc = dyc_ref[...].astype(F32)
        dconv_ref[:, 0:cw] = (dyc * y).astype(BF16)
        gy = dyc * bg_ref[...].astype(F32)
        nxt = dych_ref[...].astype(F32) * bgh_ref[...].astype(F32)
        gbuf[0:tm, :] = gy
        gbuf[tm:tm + HALO, :] = jnp.where(i < nt - 1, nxt, 0.0)
        g1 = gbuf[1:1 + tm, :]
        g2 = gbuf[2:2 + tm, :]
        dvv = w[2:3, :] * gy + w[1:2, :] * g1 + w[0:1, :] * g2
        dconv_ref[:, cw:2 * cw] = (dvv * u).astype(BF16)
        dconv_ref[:, 2 * cw:3 * cw] = (dvv * cg).astype(BF16)
        dw_ref[0:1, :] += _colsum(gy * v2)
        dw_ref[1:2, :] += _colsum(gy * v1)
        dw_ref[2:3, :] += _colsum(gy * vv)

    def prev(i):
        return jnp.maximum(i * nb - 1, 0)

    def nxt_blk(i):
        return jnp.minimum((i + 1) * nb, last_blk)

    def col(c):
        return pl.BlockSpec((tm, cw), lambda i: (i, c))

    return pl.pallas_call(
        body, name=name, grid=(nt,),
        in_specs=[col(0), col(0), col(1), col(2),
                  pl.BlockSpec((HALO, cw), lambda i: (prev(i), 1)), pl.BlockSpec((HALO, cw), lambda i: (prev(i), 2)),
                  pl.BlockSpec((HALO, cw), lambda i: (nxt_blk(i), 0)), pl.BlockSpec((HALO, cw), lambda i: (nxt_blk(i), 0)),
                  pl.BlockSpec((3, cw), lambda i: (0, 0))],
        out_specs=[pl.BlockSpec((tm, 3 * cw), lambda i: (i, 0)), pl.BlockSpec((3, cw), lambda i: (0, 0))],
        out_shape=[_sds((s, 3 * cw), BF16), _sds((3, cw), F32)],
        scratch_shapes=[pltpu.VMEM((HALO + tm, cw), F32), pltpu.VMEM((tm + HALO, cw), F32)],
        compiler_params=_params(("arbitrary",)),
    )(dyc, proj, proj, proj, proj, proj, dyc, proj, conv_w)


def _attn_bwd(proj, o, do, tri, *, d, name, side=None):
    s = proj.shape[0]
    qb = tri.shape[0]
    chains = _fit(ATTN_CHAINS, s // qb)
    ng = s // (qb * chains)
    q0, k0, v0, hp = _attn_cols(d)

    def body(*refs):
        p, g = pl.program_id(0), pl.program_id(1)
        own, late_phases = _host_side(
            side, 6, 3, 2, refs, (p == 0) & (g == 0), (p == hp - 1) & (g == 0), (p == hp - 1) & (g == ng - 1))
        q_ref, k_ref, v_ref, o_ref, do_ref, tri_ref, dq_ref, dk_ref, dv_ref, dk_acc, dv_acc = own

        @pl.when(g == 0)
        def _():
            dk_acc[...] = jnp.zeros_like(dk_acc)
            dv_acc[...] = jnp.zeros_like(dv_acc)

        tri_m = tri_ref[...]
        masks = [_head_mask(h) for h in range(2)]

        def first_step(u):
            i = g * chains + u
            mine = slice(u * qb, (u + 1) * qb)
            dov = do_ref[mine, :]
            qs = _stack_heads(q_ref[mine, :] * ATTN_SCALE, masks)
            dos = _stack_heads(dov, masks)
            dprod = dov.astype(F32) * o_ref[mine, :]
            dtot = jnp.concatenate([jnp.sum(jnp.where(m, dprod, 0.0), axis=-1, keepdims=True) for m in masks], axis=0)

            def strip(first_key, width, state, causal):
                run, grun, dq_acc = state
                rows = pl.ds(pl.multiple_of(first_key, qb), width)
                kb = k_ref[rows, :]
                z = lax.dot_general(qs, kb, _NT, preferred_element_type=F32)
                lg = _log_one_minus_sigmoid(z)
                beta = 1.0 - jnp.exp(lg)
                if causal is not None:
                    lg = jnp.where(causal, lg, 0.0)
                cs, total = _running_sum(lg, tri_m)
                a = jnp.exp(z + cs + run)
                if causal is not None:
                    a = jnp.where(causal, a, 0.0)
                ab = a.astype(BF16)
                da = lax.dot_general(dos, v_ref[rows, :], _NT, preferred_element_type=F32)
                gg = ab.astype(F32) * da
                gcs, gtotal = _running_sum(gg, tri_m)
                left = jnp.where(run > UNDERFLOW_LOG, dtot - grun, 0.0)
                dz = gg - beta * (gg + (left - gcs))
                if causal is not None:
                    dz = jnp.where(causal, dz, 0.0)
                dzb = dz.astype(BF16)
                dq_acc = dq_acc + jnp.dot(_heads_to_lanes(dzb, qb), _stack_heads(kb, masks),
                                          preferred_element_type=F32)
                dk_add = lax.dot_general(dzb, qs, _TN, preferred_element_type=F32)
                dv_add = lax.dot_general(ab, dos, _TN, preferred_element_type=F32)
                return (run + total, grun + gtotal, dq_acc), rows, dk_add, dv_add

            zero = jnp.zeros((2 * qb, 1), F32)
            first_key = jnp.maximum(i - 1, 0) * qb
            return (i, strip) + strip(first_key, 2 * qb, (zero, zero, jnp.zeros((qb, LANES), F32)),
                                      _stacked_causal(qb, 2 * qb, first_key, i * qb))

        started = [first_step(u) for u in range(chains)]
        for u, (i, strip, state, rows, dk_add, dv_add) in enumerate(started):
            dk_acc[rows, :] += dk_add
            dv_acc[rows, :] += dv_add

            def more(st, strip=strip):
                state, rows, dk_add, dv_add = strip(st[0] * qb, qb, st[1:], None)
                dk_acc[rows, :] += dk_add
                dv_acc[rows, :] += dv_add
                return (st[0] - 1, *state)

            state = lax.while_loop(lambda st: (st[0] >= 0) & (jnp.max(st[1]) > UNDERFLOW_LOG), more, (i - 2, *state))
            dq_ref[u * qb:(u + 1) * qb, :] = (state[3] * ATTN_SCALE).astype(BF16)

        @pl.when(g == ng - 1)
        def _():
            dk_ref[...] = dk_acc[...].astype(BF16)
            dv_ref[...] = dv_acc[...].astype(BF16)

        late_phases()

    aw = hp * LANES
    tq = qb * chains
    s_in, s_out, s_shapes, aliases, s_scratch = _side_specs(side, 6, 3)
    res = pl.pallas_call(
        body, name=name, grid=(hp, ng),
        in_specs=[pl.BlockSpec((tq, LANES), lambda p, i: (i, q0 + p)),
                  pl.BlockSpec((s, LANES), lambda p, i: (0, k0 + p)),
                  pl.BlockSpec((s, LANES), lambda p, i: (0, v0 + p)),
                  pl.BlockSpec((tq, LANES), lambda p, i: (i, p)),
                  pl.BlockSpec((tq, LANES), lambda p, i: (i, p)),
                  pl.BlockSpec((qb, qb), lambda p, i: (0, 0))] + s_in,
        out_specs=[pl.BlockSpec((tq, LANES), lambda p, i: (i, p)),
                   pl.BlockSpec((s, LANES), lambda p, i: (0, p)),
                   pl.BlockSpec((s, LANES), lambda p, i: (0, p))] + s_out,
        out_shape=[_sds((s, aw), BF16), _sds((s, aw), BF16), _sds((s, aw), BF16)] + s_shapes,
        input_output_aliases=aliases,
        scratch_shapes=[pltpu.VMEM((s, LANES), F32), pltpu.VMEM((s, LANES), F32)] + s_scratch,
        compiler_params=_params(("arbitrary", "arbitrary")),
    )(proj, proj, proj, o, do, tri, *([] if side is None else side.operands))
    return res[:3], res[3:]


def _hosted(hooks, kind, l, fn, *args, **kw):
    res, side_out = fn(*args, side=hooks.side(kind, l), **kw)
    hooks.done(kind, l, side_out)
    return res


def _layer_fwd(x, mod, gains, conv_w, tri, *, l, hooks):
    sh1, sc1, gt1, sh2, sc2, gt2 = mod
    g_pre_mix, g_post_mix, g_pre_mlp, g_post_mlp = gains
    d = x.shape[1]
    w = functools.partial(hooks.weight, l)
    h, proj = _hosted(hooks, "in_proj", l, _norm_mod_matmul, x, g_pre_mix, sc1, sh1, w("w_in"), name=f"in_proj_{l}")
    yc = _conv_fwd(proj, conv_w, name=f"conv_fwd_{l}")
    o = _hosted(hooks, "attn_fwd", l, _attn_fwd, proj, tri, d=d, name=f"attn_fwd_{l}")
    ycv, yat, merged, mix, x1 = _mix_out(yc, o, proj, x, w("w_proj_conv"), w("w_proj_attn"), w("w_out"),
                                         g_post_mix, gt1, name=f"mix_out_{l}")
    (h2, a), _ = _norm_mod_matmul(x1, g_pre_mlp, sc2, sh2, w("w_mlp_in"), name=f"mlp_in_{l}")
    ff, x2 = _mlp_out(a, x1, w("w_mlp_out"), g_post_mlp, gt2, name=f"mlp_out_{l}")
    saved = dict(x=x, h=h, proj=proj, yc=yc, o=o, ycv=ycv, yat=yat, merged=merged, mix=mix, x1=x1, h2=h2, a=a, ff=ff,
                 conv_w=conv_w, **{k: w(k) for k in BIG})
    return x2, saved


def _layer_bwd(dx2, sv, mod, gains, tri, *, l, hooks):
    sh1, sc1, gt1, sh2, sc2, gt2 = mod
    g_pre_mix, g_post_mix, g_pre_mlp, g_post_mlp = gains
    d = dx2.shape[1]
    dff, da, dgt2, dg_post_mlp = _mlp_out_bwd(dx2, sv["ff"], sv["a"], sv["w_mlp_out"], g_post_mlp, gt2,
                                              name=f"mlp_out_bwd_{l}")
    hooks.grad(l, "w_mlp_out", _matmul_tn(sv["a"], [dff], relu2=True, name=f"gw_mlp_out_{l}"))
    (dx1, dsh2, dsc2, dg_pre_mlp), _ = _matmul_nt_norm_bwd([da], sv["w_mlp_in"], sv["x1"], dx2, g_pre_mlp, sc2,
                                                           name=f"mlp_in_bwd_{l}")
    hooks.grad(l, "w_mlp_in", _matmul_tn(sv["h2"], [da], name=f"gw_mlp_in_{l}"))
    dmix, dycv, dyat, dyc, do, dgate, dgt1, dg_post_mix = _mix_out_bwd(
        dx1, sv["mix"], sv["proj"], sv["ycv"], sv["yat"], sv["w_out"], sv["w_proj_conv"], sv["w_proj_attn"],
        g_post_mix, gt1, name=f"mix_out_bwd_{l}")
    hooks.grad(l, "w_out", _matmul_tn(sv["merged"], [dmix], name=f"gw_out_{l}"))
    hooks.grad(l, "w_proj_conv", _matmul_tn(sv["yc"], [dycv], name=f"gw_proj_conv_{l}"))
    hooks.grad(l, "w_proj_attn", _matmul_tn(sv["o"], [dyat], name=f"gw_proj_attn_{l}"))
    dconv, g_conv_w = _conv_bwd(dyc, sv["proj"], sv["conv_w"], name=f"conv_bwd_{l}")
    dq, dk, dv = _hosted(hooks, "attn_bwd", l, _attn_bwd, sv["proj"], sv["o"], do, tri, d=d, name=f"attn_bwd_{l}")
    dproj = [dconv, dq, dk, dv, dgate]
    n_in = sv["w_in"].shape[1]
    gw_in = _matmul_tn(sv["h"], [dconv], tn=768, n_total=n_in, name=f"gw_in_conv_{l}")
    gw_in = _matmul_tn(sv["h"], [dq, dk, dv], into=gw_in, col0=dconv.shape[1], n_total=n_in, name=f"gw_in_attn_{l}")
    gw_in = _matmul_tn(sv["h"], [dgate], into=gw_in, col0=n_in - dgate.shape[1], n_total=n_in, name=f"gw_in_gate_{l}")
    hooks.grad(l, "w_in", gw_in)
    dx0, dsh1, dsc1, dg_pre_mix = _hosted(hooks, "in_proj_bwd", l, _matmul_nt_norm_bwd, dproj, sv["w_in"], sv["x"], dx1,
                                          g_pre_mix, sc1, name=f"in_proj_bwd_{l}")
    dmod = jnp.concatenate([dsh1, dsc1, dgt1, dsh2, dsc2, dgt2], axis=0)
    dgains = jnp.concatenate([dg_pre_mix, dg_post_mix, dg_pre_mlp, dg_post_mlp], axis=0)
    return dx0, g_conv_w, dmod, dgains


BIG = ("w_in", "w_proj_conv", "w_proj_attn", "w_out", "w_mlp_in", "w_mlp_out")
SHARD_AXIS = dict(w_in=1, w_proj_conv=1, w_proj_attn=1, w_out=0, w_mlp_in=1, w_mlp_out=0)


class _LocalWeights:
    def __init__(self, wlayers):
        self.wlayers = wlayers
        self.grads = {}

    def weight(self, l, name):
        return self.wlayers[l][name]

    def side(self, kind, l):
        return None

    def done(self, kind, l, outs):
        pass

    def grad(self, l, name, g):
        self.grads[(l, name)] = g


def _local_step(x, target, mods, gains, conv_w, hooks):
    depth = mods.shape[0]
    tri = _tri(ATTN_BLOCK)
    saved = []
    for l in range(depth):
        mod = [mods[l, k:k + 1] for k in range(N_MOD)]
        gl = [gains[l, k:k + 1] for k in range(4)]
        x, sv = _layer_fwd(x, mod, gl, conv_w[l], tri, l=l, hooks=hooks)
        saved.append((sv, mod, gl))
    dx, loss = _loss_grad(x, target, name="loss_grad")
    dconv, dmods, dgains = [None] * depth, [None] * depth, [None] * depth
    for l in reversed(range(depth)):
        sv, mod, gl = saved[l]
        dx, dconv[l], dmods[l], dgains[l] = _layer_bwd(dx, sv, mod, gl, tri, l=l, hooks=hooks)
    return loss, dx, jnp.stack(dconv), jnp.stack(dmods), jnp.stack(dgains)


def _coords():
    return lax.axis_index("x"), lax.axis_index("y"), lax.axis_index("c")


def _flip(v, f):
    return 1 - v if f else v


def _all_gather_small(v, *, name):
    r, c_ = v.shape

    def body(v_ref, out_ref, send_sems, recv_sems, local_sem):
        x, y, c = _coords()
        me = 4 * x + 2 * y + c
        mine = pltpu.make_async_copy(v_ref, out_ref.at[me], local_sem)
        mine.start()
        copies = []
        for k in range(1, 8):
            fx, fy, fc = (k >> 2) & 1, (k >> 1) & 1, k & 1
            px, py, pc = _flip(x, fx), _flip(y, fy), _flip(c, fc)
            out = pltpu.make_async_remote_copy(src_ref=v_ref, dst_ref=out_ref.at[me], send_sem=send_sems.at[k - 1],
                                               recv_sem=recv_sems.at[k - 1], device_id=(px, py, pc), device_id_type=MESH)
            out.start()
            back = pltpu.make_async_remote_copy(src_ref=v_ref, dst_ref=out_ref.at[4 * px + 2 * py + pc],
                                                send_sem=send_sems.at[k - 1], recv_sem=recv_sems.at[k - 1],
                                                device_id=(px, py, pc), device_id_type=MESH)
            copies.append((out, back))
        for out, back in copies:
            back.wait_recv()
        for out, back in copies:
            out.wait_send()
        mine.wait()

    return pl.pallas_call(
        body, name=name,
        in_specs=[pl.BlockSpec(memory_space=pltpu.VMEM)],
        out_specs=pl.BlockSpec(memory_space=pltpu.VMEM),
        out_shape=_sds((8, r, c_), F32),
        scratch_shapes=[pltpu.SemaphoreType.DMA((7,)), pltpu.SemaphoreType.DMA((7,)), pltpu.SemaphoreType.DMA],
    )(v)


def _shard_dims(full_shape, axis):
    k, n = full_shape
    return (k // 4, n) if axis == 0 else (k, n // 4)


def _shard_window(ref, axis, chip, half, rows, cols):
    r0, rn = (0, rows) if half is None else (half * (rows // 2), rows // 2)
    if axis == 1:
        return ref.at[pl.ds(r0, rn), pl.ds(chip * cols, cols)]
    return ref.at[pl.ds(chip * rows + r0, rn), :]


def _cast_place(w, layer, axis, chip_arr, *, name, tr=256):
    _, rows, cols = w.shape
    tr = _fit(tr, rows)
    nb = rows // tr
    full = (rows * 4, cols) if axis == 0 else (rows, cols * 4)

    def body(chip_ref, w_ref, o_ref):
        o_ref[...] = w_ref[0].astype(BF16)

    if axis == 1:
        out_map = lambda i, chip: (i, chip[0])
    else:
        out_map = lambda i, chip: (chip[0] * nb + i, 0)
    grid_spec = pltpu.PrefetchScalarGridSpec(
        num_scalar_prefetch=1, grid=(nb,),
        in_specs=[pl.BlockSpec((1, tr, cols), lambda i, chip: (layer, i, 0))],
        out_specs=pl.BlockSpec((tr, cols), out_map))
    return pl.pallas_call(body, name=name, grid_spec=grid_spec, out_shape=_sds(full, BF16),
                          compiler_params=_params(("arbitrary",)))(chip_arr, w)


def _gather_side(fulls, axes):
    n = len(fulls)

    def copies(outs, sems):
        send_sems, recv_sems = sems
        x, y, c = _coords()
        chip = 2 * x + y
        sibling = (x, y, 1 - c)
        table = []
        for w in range(n):
            rows, cols = _shard_dims(outs[w].shape, axes[w])
            win = functools.partial(_shard_window, outs[w], axes[w], rows=rows, cols=cols)
            for j, (fx, fy) in enumerate(OTHER_CHIPS):
                px, py = _flip(x, fx), _flip(y, fy)
                pchip = 2 * px + py

                def copy(piece, sem, to):
                    return pltpu.make_async_remote_copy(src_ref=piece, dst_ref=piece, send_sem=send_sems.at[w, sem],
                                                        recv_sem=recv_sems.at[w, sem], device_id=to, device_id_type=MESH)

                table.append((copy(win(chip, c), j, (px, py, c)), copy(win(pchip, c), j, (px, py, c)),
                              copy(win(pchip, c), 3 + j, sibling), copy(win(pchip, 1 - c), 3 + j, sibling)))
        return table

    def start(ins, outs, sems):
        for send, _, _, _ in copies(outs, sems):
            send.start()

    def mid(ins, outs, sems):
        for _, landed, pass_on, _ in copies(outs, sems):
            landed.wait_recv()
            pass_on.start()

    def finish(ins, outs, sems):
        table = copies(outs, sems)
        for _, _, _, from_sibling in table:
            from_sibling.wait_recv()
        for send, _, pass_on, _ in table:
            send.wait_send()
            pass_on.wait_send()

    return _Side(fulls, [_sds(f.shape, f.dtype) for f in fulls], {w: w for w in range(n)},
                 [pltpu.SemaphoreType.DMA((n, 6)), pltpu.SemaphoreType.DMA((n, 6))], start, mid, finish)


def _exchange_side(grads, axes):
    n = len(grads)
    out_shapes = []
    for g, ax in zip(grads, axes):
        rows, cols = _shard_dims(g.shape, ax)
        out_shapes.append(_sds((7, rows // 2, cols), g.dtype))

    def copies(ins, outs, sems):
        send_sems, recv_sems = sems
        x, y, c = _coords()
        table = []
        for w in range(n):
            rows, cols = _shard_dims(ins[w].shape, axes[w])
            for k in range(1, 8):
                fx, fy, fc = (k >> 2) & 1, (k >> 1) & 1, k & 1
                px, py, pc = _flip(x, fx), _flip(y, fy), _flip(c, fc)
                piece = _shard_window(ins[w], axes[w], 2 * px + py, pc, rows, cols)
                table.append(pltpu.make_async_remote_copy(
                    src_ref=piece, dst_ref=outs[w].at[k - 1], send_sem=send_sems.at[w, k - 1],
                    recv_sem=recv_sems.at[w, k - 1], device_id=(px, py, pc), device_id_type=MESH))
        return table

    def start(ins, outs, sems):
        for cp in copies(ins, outs, sems):
            cp.start()

    def finish(ins, outs, sems):
        table = copies(ins, outs, sems)
        for cp in table:
            cp.wait_recv()
        for cp in table:
            cp.wait_send()

    return _Side(grads, out_shapes, {}, [pltpu.SemaphoreType.DMA((n, 7)), pltpu.SemaphoreType.DMA((n, 7))],
                 start, None, finish)


def _rs_sum_join(g, got, out_prev, layer, depth, axis, ids, *, name, tr=256):
    _, rows2, cols = got.shape
    tr = _fit(tr, rows2)
    nt = rows2 // tr
    if axis == 1:
        own_map = lambda i, ids_: (ids_[1] * nt + i, ids_[0])
    else:
        own_map = lambda i, ids_: ((ids_[0] * 2 + ids_[1]) * nt + i, 0)

    def body(ids_ref, g_ref, got_ref, *rest):
        out_ref, buf, local_sems, send_sems, recv_sem = rest[-5:]
        i = pl.program_id(0)
        x, y, c = _coords()
        sibling = (x, y, 1 - c)

        def copies(step, slot):
            rows_mine = pl.ds(c * rows2 + step * tr, tr)
            dst = out_ref.at[layer, rows_mine, :]
            keep = pltpu.make_async_copy(buf.at[slot], dst, local_sems.at[slot])
            give = pltpu.make_async_remote_copy(src_ref=buf.at[slot], dst_ref=dst, send_sem=send_sems.at[slot],
                                                recv_sem=recv_sem, device_id=sibling, device_id_type=MESH)
            return keep, give

        def drain(step, slot):
            keep, give = copies(step, slot)
            keep.wait()
            give.wait_send()

        slot = i % 2

        @pl.when(i >= 2)
        def _():
            drain(i - 2, slot)

        acc = g_ref[...].astype(F32)
        for k in range(7):
            acc = acc + got_ref[k].astype(F32)
        buf[slot] = acc
        keep, give = copies(i, slot)
        keep.start()
        give.start()

        @pl.when(i == nt - 1)
        def _():
            if nt >= 2:
                drain(nt - 2, (nt - 2) % 2)
            drain(nt - 1, (nt - 1) % 2)
            theirs = out_ref.at[layer, pl.ds((1 - c) * rows2, rows2), :]
            pltpu.make_async_remote_copy(src_ref=theirs, dst_ref=theirs, send_sem=send_sems.at[0], recv_sem=recv_sem,
                                         device_id=sibling, device_id_type=MESH).wait_recv()

    hbm = pl.BlockSpec(memory_space=pltpu.HBM)
    in_specs = [pl.BlockSpec((tr, cols), own_map), pl.BlockSpec((7, tr, cols), lambda i, ids_: (0, i, 0))]
    operands = [ids, g, got]
    aliases = {}
    if out_prev is not None:
        in_specs.append(hbm)
        operands.append(out_prev)
        aliases = {3: 0}
    grid_spec = pltpu.PrefetchScalarGridSpec(
        num_scalar_prefetch=1, grid=(nt,), in_specs=in_specs, out_specs=hbm,
        scratch_shapes=[pltpu.VMEM((2, tr, cols), F32), pltpu.SemaphoreType.DMA((2,)), pltpu.SemaphoreType.DMA((2,)),
                        pltpu.SemaphoreType.DMA])
    return pl.pallas_call(body, name=name, grid_spec=grid_spec, out_shape=_sds((depth, 2 * rows2, cols), F32),
                          input_output_aliases=aliases, compiler_params=_params(("arbitrary",)))(*operands)


MIX = ("w_proj_conv", "w_proj_attn", "w_out")


class _Schedule:
    def __init__(self, placed, depth, ids):
        self.placed, self.depth, self.ids = placed, depth, ids
        self.full, self.g, self.carried = {}, {}, None
        self.reduced = {k: None for k in BIG}
        first = [(0, "w_in")]
        self._landed(first, _side_call(self._gather(first), name="gather_w_in_0"))

    def _gather(self, keys):
        return _gather_side([self.placed[k] for k in keys], [SHARD_AXIS[k[1]] for k in keys])

    def _landed(self, keys, outs):
        for k, o in zip(keys, outs):
            self.full[k] = o

    def _exchange(self, keys):
        return _exchange_side([self.g[k] for k in keys], [SHARD_AXIS[k[1]] for k in keys])

    def _reduce(self, keys, got):
        for (l, name), pieces in zip(keys, got):
            self.reduced[name] = _rs_sum_join(self.g[(l, name)], pieces, self.reduced[name], l, self.depth,
                                              SHARD_AXIS[name], self.ids, name=f"rs_sum_join_{l}_{name}")

    def weight(self, l, name):
        return self.full[(l, name)]

    def grad(self, l, name, g):
        self.g[(l, name)] = g

    def side(self, kind, l):
        nxt = [(l + 1, "w_in")] if l + 1 < self.depth else []
        if kind == "in_proj":
            keys, make = [(l, k) for k in MIX + ("w_mlp_in",)], self._gather
        elif kind == "attn_fwd":
            keys, make = [(l, "w_mlp_out")] + nxt, self._gather
        elif kind == "attn_bwd":
            keys, make = [(l, k) for k in ("w_mlp_out", "w_mlp_in") + MIX], self._exchange
        else:
            keys, make = [(l, "w_in")], self._exchange
        self.carried = keys
        return make(keys)

    def done(self, kind, l, outs):
        (self._landed if kind in ("in_proj", "attn_fwd") else self._reduce)(self.carried, outs)


def _flat_rows(shape):
    rows = 1
    for s in shape[:-1]:
        rows *= s
    return rows, shape[-1]


def _row_tile(rows, cols, cap_bytes=2 * 1024 * 1024):
    t = rows
    while t * cols * 4 > cap_bytes and t % 16 == 0:
        t //= 2
    return t


def _ada_fwd(c_all, w_ada, b_loc, *, name, tn=512):
    l, d, nl = w_ada.shape
    b = c_all.shape[0]
    tn = min(tn, nl)

    def body(c_ref, w_ref, b_ref, o_ref):
        o_ref[0] = jnp.dot(c_ref[...], w_ref[0], preferred_element_type=F32,
                           precision=lax.Precision.HIGHEST) + b_ref[0]

    return pl.pallas_call(
        body, name=name, grid=(l, nl // tn),
        in_specs=[pl.BlockSpec((b, d), lambda i, j: (0, 0)), pl.BlockSpec((1, d, tn), lambda i, j: (i, 0, j)),
                  pl.BlockSpec((1, 1, tn), lambda i, j: (i, 0, j))],
        out_specs=pl.BlockSpec((1, b, tn), lambda i, j: (i, 0, j)),
        out_shape=_sds((l, b, nl), F32),
        compiler_params=_params(("parallel", "parallel")),
    )(c_all, w_ada, b_loc)


def _ada_bwd(c_t, dmod_loc, *, name, tn=512):
    d, b = c_t.shape
    l, _, nl = dmod_loc.shape
    tn = min(tn, nl)

    def body(c_ref, dm_ref, o_ref):
        cv = c_ref[...]
        dm = dm_ref[0]
        acc = cv[:, 0:1] * dm[0:1, :]
        for k in range(1, b):
            acc = acc + cv[:, k:k + 1] * dm[k:k + 1, :]
        o_ref[0] = acc

    return pl.pallas_call(
        body, name=name, grid=(l, nl // tn),
        in_specs=[pl.BlockSpec((d, b), lambda i, j: (0, 0)), pl.BlockSpec((1, b, tn), lambda i, j: (i, 0, j))],
        out_specs=pl.BlockSpec((1, d, tn), lambda i, j: (i, 0, j)),
        out_shape=_sds((l, d, nl), F32),
        compiler_params=_params(("parallel", "parallel")),
    )(c_t, dmod_loc)


def _sum_devices(p, *, name):
    k, r, c_ = p.shape

    def body(p_ref, o_ref):
        acc = p_ref[0]
        for j in range(1, k):
            acc = acc + p_ref[j]
        o_ref[...] = acc

    return pl.pallas_call(body, name=name, out_shape=_sds((r, c_), F32),
                          in_specs=[pl.BlockSpec(memory_space=pltpu.VMEM)],
                          out_specs=pl.BlockSpec(memory_space=pltpu.VMEM))(p)


def _adamw(w, g, m, v, *, name):
    shape = w.shape
    rows, cols = _flat_rows(shape)
    tr = _row_tile(rows, cols, cap_bytes=1024 * 1024)
    c1 = 1.0 / (1.0 - ADAM_B1 ** ADAM_STEP)
    c2 = 1.0 / (1.0 - ADAM_B2 ** ADAM_STEP)

    def body(w_ref, g_ref, m_ref, v_ref, d_ref, nm_ref, nv_ref):
        gv = g_ref[...]
        nm = ADAM_B1 * m_ref[...] + (1.0 - ADAM_B1) * gv
        nv = ADAM_B2 * v_ref[...] + (1.0 - ADAM_B2) * (gv * gv)
        m_hat = nm * c1
        v_hat = nv * c2
        d_ref[...] = -ADAM_LR * (m_hat / (jnp.sqrt(v_hat) + ADAM_EPS) + ADAM_WD * w_ref[...])
        nm_ref[...] = nm
        nv_ref[...] = nv

    spec = pl.BlockSpec((tr, cols), lambda i: (i, 0))
    flat = lambda a: a.reshape(rows, cols)
    outs = pl.pallas_call(body, name=name, grid=(rows // tr,), in_specs=[spec] * 4, out_specs=[spec] * 3,
                          out_shape=[_sds((rows, cols), F32)] * 3, compiler_params=_params(("parallel",)),
                          )(flat(w), flat(g), flat(m), flat(v))
    return tuple(o.reshape(shape) for o in outs)


WEIGHTS = ("w_ada", "b_ada", "g_pre_mix", "g_post_mix", "g_pre_mlp", "g_post_mlp", "w_in", "conv_w",
           "w_proj_conv", "w_proj_attn", "w_out", "w_mlp_in", "w_mlp_out")
GAINS = ("g_pre_mix", "g_post_mix", "g_pre_mlp", "g_post_mlp")


def kernel(x, c, w_ada, b_ada, g_pre_mix, g_post_mix, g_pre_mlp, g_post_mlp, w_in, conv_w, w_proj_conv, w_proj_attn, w_out, w_mlp_in, w_mlp_out, loss_target, m_w_ada, m_b_ada, m_g_pre_mix, m_g_post_mix, m_g_pre_mlp, m_g_post_mlp, m_w_in, m_conv_w, m_w_proj_conv, m_w_proj_attn, m_w_out, m_w_mlp_in, m_w_mlp_out, v_w_ada, v_b_ada, v_g_pre_mix, v_g_post_mix, v_g_pre_mlp, v_g_post_mlp, v_w_in, v_conv_w, v_w_proj_conv, v_w_proj_attn, v_w_out, v_w_mlp_in, v_w_mlp_out):
    params = dict(w_ada=w_ada, b_ada=b_ada, g_pre_mix=g_pre_mix, g_post_mix=g_post_mix, g_pre_mlp=g_pre_mlp,
                  g_post_mlp=g_post_mlp, w_in=w_in, conv_w=conv_w, w_proj_conv=w_proj_conv, w_proj_attn=w_proj_attn,
                  w_out=w_out, w_mlp_in=w_mlp_in, w_mlp_out=w_mlp_out)
    m_in = dict(w_ada=m_w_ada, b_ada=m_b_ada, g_pre_mix=m_g_pre_mix, g_post_mix=m_g_post_mix, g_pre_mlp=m_g_pre_mlp,
                g_post_mlp=m_g_post_mlp, w_in=m_w_in, conv_w=m_conv_w, w_proj_conv=m_w_proj_conv,
                w_proj_attn=m_w_proj_attn, w_out=m_w_out, w_mlp_in=m_w_mlp_in, w_mlp_out=m_w_mlp_out)
    v_in = dict(w_ada=v_w_ada, b_ada=v_b_ada, g_pre_mix=v_g_pre_mix, g_post_mix=v_g_post_mix, g_pre_mlp=v_g_pre_mlp,
                g_post_mlp=v_g_post_mlp, w_in=v_w_in, conv_w=v_conv_w, w_proj_conv=v_w_proj_conv,
                w_proj_attn=v_w_proj_attn, w_out=v_w_out, w_mlp_in=v_w_mlp_in, w_mlp_out=v_w_mlp_out)

    depth, d, nl_ada = w_ada.shape
    ix, iy, ic = lax.axis_index("x"), lax.axis_index("y"), lax.axis_index("c")
    chip = 2 * ix + iy
    me = 4 * ix + 2 * iy + ic
    xs = x[0]
    target = loss_target[0]

    c_all = _all_gather_small(jnp.broadcast_to(c, (8, d)), name="gather_c")[:, 0, :]
    b_loc = lax.dynamic_slice_in_dim(b_ada, chip * nl_ada, nl_ada, axis=1)[:, None, :]
    mod_loc = _ada_fwd(c_all, w_ada, b_loc, name="ada_fwd")
    mod_all = _all_gather_small(mod_loc.reshape(depth * 8, nl_ada), name="gather_mod")
    mod_all = mod_all.reshape(4, 2, depth, 8, nl_ada)[:, 0]
    mod_me = lax.dynamic_index_in_dim(mod_all, me, axis=2, keepdims=False)
    mods = jnp.transpose(mod_me, (1, 0, 2)).reshape(depth, N_MOD, d)

    chip_arr = jnp.reshape(chip, (1,)).astype(jnp.int32)
    ids = jnp.stack([chip, ic]).astype(jnp.int32)
    placed = {(l, k): _cast_place(params[k], l, SHARD_AXIS[k], chip_arr, name=f"place_{k}_{l}")
              for l in range(depth) for k in BIG}
    conv_full = _all_gather_small(
        jnp.pad(conv_w.reshape(depth * 3, -1), ((0, 8 - depth * 3), (0, 0))), name="gather_conv_w")
    conv_full = conv_full.reshape(4, 2, 8, -1)[:, 0, :depth * 3]
    conv_full = jnp.transpose(conv_full, (1, 0, 2)).reshape(depth, 3, -1)

    gains = jnp.stack([params[k] for k in GAINS], axis=1)
    schedule = _Schedule(placed, depth, ids)
    loss, dx, conv_grads, dmods, dgains = _local_step(xs, target, mods, gains, conv_full, schedule)

    cw = conv_full.shape[2]
    rows = [dmods.reshape(depth * N_MOD, d), dgains.reshape(depth * 4, d),
            conv_grads.reshape(-1, d), jnp.broadcast_to(loss, (1, d))]
    payload = jnp.concatenate(rows, axis=0)
    n_rows = payload.shape[0]
    pad = (-n_rows) % 8
    payload = jnp.pad(payload, ((0, pad), (0, 0)))
    everyone = _all_gather_small(payload, name="gather_small_grads")
    total = _sum_devices(everyone, name="sum_small_grads")
    r0 = depth * N_MOD
    grads = {}
    grads["b_ada"] = total[:r0].reshape(depth, N_MOD * d)
    gsum = total[r0:r0 + depth * 4].reshape(depth, 4, d)
    for k, name in enumerate(GAINS):
        grads[name] = gsum[:, k]
    r1 = r0 + depth * 4
    n_conv = (depth * 3 * cw) // d
    conv_g = total[r1:r1 + n_conv].reshape(depth, 3, cw)
    grads["conv_w"] = lax.dynamic_slice_in_dim(conv_g, chip * (cw // 4), cw // 4, axis=2)
    loss_out = total[r1 + n_conv, 0]
    dmod_all = everyone[:, :r0].reshape(8, depth, N_MOD * d)
    dmod_loc = lax.dynamic_slice_in_dim(dmod_all, chip * nl_ada, nl_ada, axis=2)
    grads["w_ada"] = _ada_bwd(c_all.T, jnp.transpose(dmod_loc, (1, 0, 2)), name="ada_bwd")

    for k in BIG:
        grads[k] = schedule.reduced[k]

    deltas, new_m, new_v = {}, {}, {}
    for k in WEIGHTS:
        deltas[k], new_m[k], new_v[k] = _adamw(params[k], grads[k], m_in[k], v_in[k], name=f"adamw_{k}")

    return (loss_out, dx[None], *[grads[k] for k in WEIGHTS], *[deltas[k] for k in WEIGHTS],
            *[new_m[k] for k in WEIGHTS], *[new_v[k] for k in WEIGHTS])
```

```python
import functools

import jax
import jax.numpy as jnp
from jax import lax
from jax.experimental import pallas as pl
from jax.experimental.pallas import tpu as pltpu

F32 = jnp.float32
BF16 = jnp.bfloat16
EPS = 1e-6
N_MOD = 6
HEAD_DIM = 64
LANES = 128
ATTN_SCALE = 1.0 / 8.0
UNDERFLOW_LOG = -90.0
ATTN_BLOCK = 256
ATTN_FIRST_BLOCKS = 1
ATTN_CHAINS = (1, 2)
ADAM_LR = 0.001
ADAM_B1 = 0.9
ADAM_B2 = 0.999
ADAM_EPS = 1e-08
ADAM_WD = 0.01
ADAM_STEP = 10
VMEM_LIMIT = 56 * 1024 * 1024
MESH = pl.DeviceIdType.MESH
OTHER_CHIPS = ((1, 0), (0, 1), (1, 1))

_NT = (((1,), (1,)), ((), ()))
_TN = (((0,), (0,)), ((), ()))


def _sds(shape, dtype):
    return jax.ShapeDtypeStruct(shape, dtype)


def _params(sem):
    return pltpu.CompilerParams(dimension_semantics=sem, vmem_limit_bytes=VMEM_LIMIT)


def _fit(t, n):
    t = min(t, n)
    while n % t:
        t //= 2
    return t


def _vec_spec(d, nargs):
    if nargs == 1:
        return pl.BlockSpec((1, d), lambda i: (0, 0))
    return pl.BlockSpec((1, d), lambda i, j: (0, 0))


def _log_one_minus_sigmoid(z):
    return -jnp.log(1.0 + jnp.exp(-jnp.abs(z))) - jnp.maximum(z, 0.0)


def _sigmoid(z):
    return 0.5 * jnp.tanh(0.5 * z) + 0.5


def _split_bf16(a):
    hi = a.astype(BF16)
    lo = (a - hi.astype(F32)).astype(BF16)
    return hi, lo


def _rms_bwd(dn, xin, g):
    r = lax.rsqrt(jnp.mean(xin * xin, axis=-1, keepdims=True) + EPS)
    xh = xin * r
    dxh = dn * g
    dxin = r * (dxh - xh * jnp.mean(dxh * xh, axis=-1, keepdims=True))
    return dxin, xh


def _colsum(a):
    return jnp.sum(a, axis=0, keepdims=True)


def _norm_mod_matmul(x, g, sc, sh, w, *, name, tm=256, side=None):
    s, d = x.shape
    n = w.shape[1]
    tm = _fit(tm, s)
    nt = s // tm

    def body(*refs):
        i = pl.program_id(0)
        (x_ref, g_ref, sc_ref, sh_ref, w_ref, h_ref, o_ref), late_phases = _host_side(
            side, 5, 2, 0, refs, i == 0, i == (3 * nt) // 4, i == nt - 1)
        xv = x_ref[...]
        r = lax.rsqrt(jnp.mean(xv * xv, axis=-1, keepdims=True) + EPS)
        h = ((xv * r * g_ref[...]) * (1.0 + sc_ref[...]) + sh_ref[...]).astype(BF16)
        h_ref[...] = h
        o_ref[...] = jnp.dot(h, w_ref[...], preferred_element_type=F32).astype(BF16)
        late_phases()

    s_in, s_out, s_shapes, aliases, s_scratch = _side_specs(side, 5, 2)
    res = pl.pallas_call(
        body, name=name, grid=(nt,),
        in_specs=[pl.BlockSpec((tm, d), lambda i: (i, 0)), _vec_spec(d, 1), _vec_spec(d, 1), _vec_spec(d, 1),
                  pl.BlockSpec((d, n), lambda i: (0, 0))] + s_in,
        out_specs=[pl.BlockSpec((tm, d), lambda i: (i, 0)), pl.BlockSpec((tm, n), lambda i: (i, 0))] + s_out,
        out_shape=[_sds((s, d), BF16), _sds((s, n), BF16)] + s_shapes,
        input_output_aliases=aliases, scratch_shapes=s_scratch,
        compiler_params=_params(("arbitrary",)),
    )(x, g, sc, sh, w, *([] if side is None else side.operands))
    return res[:2], res[2:]


HALO = 16


def _conv_fwd(proj, conv_w, *, name, tm=512):
    s = proj.shape[0]
    cw = conv_w.shape[1]
    tm = min(tm, s)
    nb = tm // HALO

    def body(bg_ref, cg_ref, u_ref, cgh_ref, uh_ref, w_ref, yc_ref, vbuf):
        i = pl.program_id(0)
        vv = cg_ref[...].astype(F32) * u_ref[...].astype(F32)
        halo = cgh_ref[...].astype(F32) * uh_ref[...].astype(F32)
        vbuf[0:HALO, :] = jnp.where(i > 0, halo, 0.0)
        vbuf[HALO:HALO + tm, :] = vv
        v1 = vbuf[HALO - 1:HALO - 1 + tm, :]
        v2 = vbuf[HALO - 2:HALO - 2 + tm, :]
        w = w_ref[...]
        y = w[2:3, :] * vv + w[1:2, :] * v1 + w[0:1, :] * v2
        yc_ref[...] = (bg_ref[...].astype(F32) * y).astype(BF16)

    def prev(i):
        return jnp.maximum(i * nb - 1, 0)

    return pl.pallas_call(
        body, name=name, grid=(s // tm,),
        in_specs=[pl.BlockSpec((tm, cw), lambda i: (i, 0)), pl.BlockSpec((tm, cw), lambda i: (i, 1)),
                  pl.BlockSpec((tm, cw), lambda i: (i, 2)),
                  pl.BlockSpec((HALO, cw), lambda i: (prev(i), 1)), pl.BlockSpec((HALO, cw), lambda i: (prev(i), 2)),
                  pl.BlockSpec((3, cw), lambda i: (0, 0))],
        out_specs=pl.BlockSpec((tm, cw), lambda i: (i, 0)),
        out_shape=_sds((s, cw), BF16),
        scratch_shapes=[pltpu.VMEM((HALO + tm, cw), F32)],
        compiler_params=_params(("arbitrary",)),
    )(proj, proj, proj, proj, proj, conv_w)


def _tri(qb):
    r = lax.broadcasted_iota(jnp.int32, (qb, qb), 0)
    c = lax.broadcasted_iota(jnp.int32, (qb, qb), 1)
    return (r >= c).astype(BF16)


def _head_mask(h):
    lane = lax.broadcasted_iota(jnp.int32, (1, LANES), 1)
    return (lane >= HEAD_DIM * h) & (lane < HEAD_DIM * (h + 1))


def _stack_heads(a, masks):
    return jnp.concatenate([jnp.where(m, a, 0).astype(BF16) for m in masks], axis=0)


def _heads_to_lanes(a, qb):
    return jnp.concatenate([a[:qb], a[qb:]], axis=1)


def _stacked_causal(qb, width, first_key, first_query):
    row = lax.broadcasted_iota(jnp.int32, (2 * qb, width), 0)
    col = lax.broadcasted_iota(jnp.int32, (2 * qb, width), 1)
    return first_key + col < first_query + jnp.where(row >= qb, row - qb, row)


def _running_sum(a, tri_m):
    rows, qb = a.shape[0], tri_m.shape[0]
    n = a.shape[1] // qb
    hi, lo = _split_bf16(a)
    stacked = jnp.concatenate([p[:, s * qb:(s + 1) * qb] for s in range(n) for p in (hi, lo)], axis=0)
    both = jnp.dot(stacked, tri_m, preferred_element_type=F32)
    parts = [both[(2 * s) * rows:(2 * s + 1) * rows] + both[(2 * s + 1) * rows:(2 * s + 2) * rows] for s in range(n)]
    later = None
    for s in reversed(range(n)):
        if later is not None:
            parts[s] = parts[s] + later
        later = parts[s][:, 0:1]
    return (parts[0] if n == 1 else jnp.concatenate(parts, axis=1)), later


def _attn_cols(d):
    cw = d // 2
    hp = (d // 2) // LANES
    q0 = (3 * cw) // LANES
    return q0, q0 + hp, q0 + 2 * hp, hp


class _Side:
    def __init__(self, operands, out_shapes, aliases, scratch, start, mid, finish):
        self.operands, self.out_shapes, self.aliases, self.scratch = list(operands), list(out_shapes), aliases, list(scratch)
        self.start, self.mid, self.finish = start, mid, finish


def _side_call(side, *, name):
    n_in, n_out = len(side.operands), len(side.out_shapes)

    def body(*refs):
        parts = refs[:n_in], refs[n_in:n_in + n_out], refs[n_in + n_out:]
        side.start(*parts)
        if side.mid is not None:
            side.mid(*parts)
        side.finish(*parts)

    hbm = pl.BlockSpec(memory_space=pltpu.HBM)
    return pl.pallas_call(body, name=name, in_specs=[hbm] * n_in, out_specs=[hbm] * n_out, out_shape=side.out_shapes,
                          input_output_aliases=dict(side.aliases), scratch_shapes=side.scratch)(*side.operands)


def _host_side(side, n_in, n_out, n_scratch, refs, first, late, last):
    if side is None:
        return refs, lambda: None
    s_in, s_out = len(side.operands), len(side.out_shapes)
    ins = refs[:n_in]
    side_in = refs[n_in:n_in + s_in]
    outs = refs[n_in + s_in:n_in + s_in + n_out]
    side_out = refs[n_in + s_in + n_out:n_in + s_in + n_out + s_out]
    rest = refs[n_in + s_in + n_out + s_out:]
    scratch, sems = rest[:n_scratch], rest[n_scratch:]
    parts = (side_in, side_out, sems)
    pl.when(first)(lambda: side.start(*parts))

    def run_late_phases():
        if side.mid is not None:
            pl.when(late)(lambda: side.mid(*parts))
        pl.when(last)(lambda: side.finish(*parts))

    return (*ins, *outs, *scratch), run_late_phases


def _side_specs(side, n_in, n_out):
    if side is None:
        return [], [], [], {}, []
    hbm = pl.BlockSpec(memory_space=pltpu.HBM)
    s_in = len(side.operands)
    aliases = {n_in + a: n_out + b for a, b in side.aliases.items()}
    return [hbm] * s_in, [hbm] * len(side.out_shapes), side.out_shapes, aliases, side.scratch


def _attn_fwd(proj, tri, *, d, name, side=None):
    s = proj.shape[0]
    qb = tri.shape[0]
    chains = _fit(ATTN_CHAINS[0], s // qb)
    ng = s // (qb * chains)
    q0, k0, v0, hp = _attn_cols(d)

    def body(*refs):
        p, g = pl.program_id(0), pl.program_id(1)
        (q_ref, k_ref, v_ref, tri_ref, o_ref), late_phases = _host_side(
            side, 4, 1, 0, refs, (p == 0) & (g == 0), (p == hp - 1) & (g == 0), (p == hp - 1) & (g == ng - 1))
        tri_m = tri_ref[...]
        masks = [_head_mask(h) for h in range(2)]

        def first_step(u):
            i = g * chains + u
            qs = _stack_heads(q_ref[u * qb:(u + 1) * qb, :] * ATTN_SCALE, masks)

            def strip(first_key, width, state, causal):
                run, acc = state
                rows = pl.ds(pl.multiple_of(first_key, qb), width)
                z = lax.dot_general(qs, k_ref[rows, :], _NT, preferred_element_type=F32)
                lg = _log_one_minus_sigmoid(z)
                if causal is not None:
                    lg = jnp.where(causal, lg, 0.0)
                cs, total = _running_sum(lg, tri_m)
                a = jnp.exp(z + cs + run)
                if causal is not None:
                    a = jnp.where(causal, a, 0.0)
                ab = a.astype(BF16)
                acc = acc + jnp.dot(_heads_to_lanes(ab, qb), _stack_heads(v_ref[rows, :], masks),
                                    preferred_element_type=F32)
                return run + total, acc

            first_key = jnp.maximum(i - (ATTN_FIRST_BLOCKS - 1), 0) * qb
            width = ATTN_FIRST_BLOCKS * qb
            state = strip(first_key, width, (jnp.zeros((2 * qb, 1), F32), jnp.zeros((qb, LANES), F32)),
                          _stacked_causal(qb, width, first_key, i * qb))
            return i, strip, state

        started = [first_step(u) for u in range(chains)]
        for u, (i, strip, state) in enumerate(started):
            state = lax.while_loop(
                lambda st: (st[0] >= 0) & (jnp.max(st[1]) > UNDERFLOW_LOG),
                lambda st, strip=strip: (st[0] - 1, *strip(st[0] * qb, qb, st[1:], None)),
                (i - ATTN_FIRST_BLOCKS, *state))
            o_ref[u * qb:(u + 1) * qb, :] = state[2]
        late_phases()

    s_in, s_out, s_shapes, aliases, s_scratch = _side_specs(side, 4, 1)
    tq = qb * chains
    res = pl.pallas_call(
        body, name=name, grid=(hp, ng),
        in_specs=[pl.BlockSpec((tq, LANES), lambda p, i: (i, q0 + p)),
                  pl.BlockSpec((s, LANES), lambda p, i: (0, k0 + p)),
                  pl.BlockSpec((s, LANES), lambda p, i: (0, v0 + p)),
                  pl.BlockSpec((qb, qb), lambda p, i: (0, 0))] + s_in,
        out_specs=[pl.BlockSpec((tq, LANES), lambda p, i: (i, p))] + s_out,
        out_shape=[_sds((s, hp * LANES), F32)] + s_shapes,
        input_output_aliases=aliases, scratch_shapes=s_scratch,
        compiler_params=_params(("arbitrary", "arbitrary")),
    )(proj, proj, proj, tri, *([] if side is None else side.operands))
    return res[0], res[1:]


def _mix_out(yc, o, proj, x, wpc, wpa, wout, g, gt, *, name, tm=256):
    s, d = x.shape
    cw = yc.shape[1]
    tm = min(tm, s)
    ga_blk = (3 * cw + 3 * (d // 2)) // d

    def body(yc_ref, o_ref, ga_ref, gb_ref, x_ref, wpc_ref, wpa_ref, wout_ref, g_ref, gt_ref,
             ycv_ref, yat_ref, mg_ref, mix_ref, x1_ref):
        y_conv = jnp.dot(yc_ref[...], wpc_ref[...], preferred_element_type=F32)
        y_attn = jnp.dot(o_ref[...].astype(BF16), wpa_ref[...], preferred_element_type=F32)
        merged = (_sigmoid(ga_ref[...].astype(F32)) * y_conv + _sigmoid(gb_ref[...].astype(F32)) * y_attn)
        mg = merged.astype(BF16)
        mix = jnp.dot(mg, wout_ref[...], preferred_element_type=F32)
        r = lax.rsqrt(jnp.mean(mix * mix, axis=-1, keepdims=True) + EPS)
        ycv_ref[...] = y_conv.astype(BF16)
        yat_ref[...] = y_attn.astype(BF16)
        mg_ref[...] = mg
        mix_ref[...] = mix
        x1_ref[...] = x_ref[...] + gt_ref[...] * (mix * r * g_ref[...])

    def rows(w):
        return pl.BlockSpec((tm, w), lambda i: (i, 0))

    def full(a):
        return pl.BlockSpec(a.shape, lambda i: (0, 0))

    return pl.pallas_call(
        body, name=name, grid=(s // tm,),
        in_specs=[rows(cw), rows(d // 2), pl.BlockSpec((tm, d), lambda i: (i, ga_blk)),
                  pl.BlockSpec((tm, d), lambda i: (i, ga_blk + 1)), rows(d),
                  full(wpc), full(wpa), full(wout), _vec_spec(d, 1), _vec_spec(d, 1)],
        out_specs=[rows(d), rows(d), rows(d), rows(d), rows(d)],
        out_shape=[_sds((s, d), BF16), _sds((s, d), BF16), _sds((s, d), BF16), _sds((s, d), F32), _sds((s, d), F32)],
        compiler_params=_params(("parallel",)),
    )(yc, o, proj, proj, x, wpc, wpa, wout, g, gt)


def _relu2(a):
    r = jnp.maximum(a.astype(F32), 0.0)
    return (r * r).astype(BF16)


def _mlp_out(a, x, w2, g, gt, *, name, tm=512):
    s, d = x.shape
    dff = a.shape[1]
    tm = min(tm, s)

    def body(a_ref, x_ref, w_ref, g_ref, gt_ref, ff_ref, x2_ref):
        ff = jnp.dot(_relu2(a_ref[...]), w_ref[...], preferred_element_type=F32)
        r = lax.rsqrt(jnp.mean(ff * ff, axis=-1, keepdims=True) + EPS)
        ff_ref[...] = ff
        x2_ref[...] = x_ref[...] + gt_ref[...] * (ff * r * g_ref[...])

    return pl.pallas_call(
        body, name=name, grid=(s // tm,),
        in_specs=[pl.BlockSpec((tm, dff), lambda i: (i, 0)), pl.BlockSpec((tm, d), lambda i: (i, 0)),
                  pl.BlockSpec((dff, d), lambda i: (0, 0)), _vec_spec(d, 1), _vec_spec(d, 1)],
        out_specs=[pl.BlockSpec((tm, d), lambda i: (i, 0)), pl.BlockSpec((tm, d), lambda i: (i, 0))],
        out_shape=[_sds((s, d), F32), _sds((s, d), F32)],
        compiler_params=_params(("parallel",)),
    )(a, x, w2, g, gt)


def _loss_grad(y, target, *, name, tm=512):
    s, d = y.shape
    tm = min(tm, s)

    def body(y_ref, t_ref, dy_ref, loss_ref):
        @pl.when(pl.program_id(0) == 0)
        def _():
            loss_ref[...] = jnp.zeros_like(loss_ref)
        e = y_ref[...] - t_ref[...]
        dy_ref[...] = e * (1.0 / d)
        loss_ref[...] += 0.5 * jnp.sum(jnp.mean(e * e, axis=-1, keepdims=True), axis=0, keepdims=True)

    return pl.pallas_call(
        body, name=name, grid=(s // tm,),
        in_specs=[pl.BlockSpec((tm, d), lambda i: (i, 0)), pl.BlockSpec((tm, d), lambda i: (i, 0))],
        out_specs=[pl.BlockSpec((tm, d), lambda i: (i, 0)), pl.BlockSpec((1, 1), lambda i: (0, 0))],
        out_shape=[_sds((s, d), F32), _sds((1, 1), F32)],
        compiler_params=_params(("arbitrary",)),
    )(y, target)


def _mlp_out_bwd(dx, ff, a, w2, g, gt, *, name, tm=256):
    s, d = dx.shape
    dff = a.shape[1]
    tm = min(tm, s)

    def body(dx_ref, ff_ref, a_ref, w_ref, g_ref, gt_ref, dff_ref, da_ref, dgt_ref, dg_ref):
        @pl.when(pl.program_id(0) == 0)
        def _():
            dgt_ref[...] = jnp.zeros_like(dgt_ref)
            dg_ref[...] = jnp.zeros_like(dg_ref)
        dxv = dx_ref[...]
        dn = dxv * gt_ref[...]
        dffv, xh = _rms_bwd(dn, ff_ref[...], g_ref[...])
        dgt_ref[...] += _colsum(dxv * (xh * g_ref[...]))
        dg_ref[...] += _colsum(dn * xh)
        dffb = dffv.astype(BF16)
        dff_ref[...] = dffb
        drr = lax.dot_general(dffb, w_ref[...], _NT, preferred_element_type=F32)
        da_ref[...] = (drr * (2.0 * jnp.maximum(a_ref[...].astype(F32), 0.0))).astype(BF16)

    return pl.pallas_call(
        body, name=name, grid=(s // tm,),
        in_specs=[pl.BlockSpec((tm, d), lambda i: (i, 0)), pl.BlockSpec((tm, d), lambda i: (i, 0)),
                  pl.BlockSpec((tm, dff), lambda i: (i, 0)), pl.BlockSpec((dff, d), lambda i: (0, 0)),
                  _vec_spec(d, 1), _vec_spec(d, 1)],
        out_specs=[pl.BlockSpec((tm, d), lambda i: (i, 0)), pl.BlockSpec((tm, dff), lambda i: (i, 0)),
                   _vec_spec(d, 1), _vec_spec(d, 1)],
        out_shape=[_sds((s, d), BF16), _sds((s, dff), BF16), _sds((1, d), F32), _sds((1, d), F32)],
        compiler_params=_params(("arbitrary",)),
    )(dx, ff, a, w2, g, gt)


def _matmul_nt_norm_bwd(dys, w, x, dres, g, sc, *, name, tm=256, side=None):
    s = dys[0].shape[0]
    widths = [dy.shape[1] for dy in dys]
    d, n = w.shape
    assert sum(widths) == n, (widths, n)
    tm = _fit(tm, s)
    nt = s // tm
    np_ = len(dys)

    def body(*refs):
        i = pl.program_id(0)
        own, late_phases = _host_side(side, np_ + 5, 4, 0, refs, i == 0, i == (3 * nt) // 4, i == nt - 1)
        dy_refs = own[:np_]
        w_ref, x_ref, dres_ref, g_ref, sc_ref, dx_ref, dsh_ref, dsc_ref, dg_ref = own[np_:]

        @pl.when(i == 0)
        def _():
            dsh_ref[...] = jnp.zeros_like(dsh_ref)
            dsc_ref[...] = jnp.zeros_like(dsc_ref)
            dg_ref[...] = jnp.zeros_like(dg_ref)

        dh = None
        for p, dy_ref in enumerate(dy_refs):
            cols = slice(sum(widths[:p]), sum(widths[:p + 1]))
            part = lax.dot_general(dy_ref[...], w_ref[:, cols], _NT, preferred_element_type=F32)
            dh = part if dh is None else dh + part
        dn = dh * (1.0 + sc_ref[...])
        dxin, xh = _rms_bwd(dn, x_ref[...], g_ref[...])
        dsh_ref[...] += _colsum(dh)
        dsc_ref[...] += _colsum(dh * (xh * g_ref[...]))
        dg_ref[...] += _colsum(dn * xh)
        dx_ref[...] = dres_ref[...] + dxin
        late_phases()

    s_in, s_out, s_shapes, aliases, s_scratch = _side_specs(side, np_ + 5, 4)
    res = pl.pallas_call(
        body, name=name, grid=(nt,),
        in_specs=[pl.BlockSpec((tm, wd), lambda i: (i, 0)) for wd in widths]
        + [pl.BlockSpec((d, n), lambda i: (0, 0)),
           pl.BlockSpec((tm, d), lambda i: (i, 0)), pl.BlockSpec((tm, d), lambda i: (i, 0)),
           _vec_spec(d, 1), _vec_spec(d, 1)] + s_in,
        out_specs=[pl.BlockSpec((tm, d), lambda i: (i, 0)), _vec_spec(d, 1), _vec_spec(d, 1), _vec_spec(d, 1)] + s_out,
        out_shape=[_sds((s, d), F32), _sds((1, d), F32), _sds((1, d), F32), _sds((1, d), F32)] + s_shapes,
        input_output_aliases=aliases, scratch_shapes=s_scratch,
        compiler_params=_params(("arbitrary",)),
    )(*dys, w, x, dres, g, sc, *([] if side is None else side.operands))
    return res[:4], res[4:]


def _matmul_tn(a, bs, *, name, tk=1024, tn=1024, ts=512, relu2=False, into=None, col0=0, n_total=None):
    s, k = a.shape
    widths = [b.shape[1] for b in bs]
    n = sum(widths)
    tk, ts = _fit(tk, k), _fit(ts, s)
    for w in widths:
        tn = _fit(tn, w)
    while col0 % tn:
        tn //= 2
    nt = s // ts
    assert tn % LANES == 0 and all(sum(widths[:p]) % tn == 0 for p in range(len(bs))), (widths, tn)
    first = [sum(widths[:p]) // tn for p in range(len(bs))]
    tiles = [w // tn for w in widths]
    tile0 = col0 // tn

    def body(a_ref, *rest):
        b_refs, o_ref, acc = rest[:len(bs)], rest[-2], rest[-1]
        j, t = pl.program_id(1), pl.program_id(2)

        @pl.when(t == 0)
        def _():
            acc[...] = jnp.zeros_like(acc)
        av = a_ref[...]
        av = _relu2(av) if relu2 else av.astype(BF16)
        for p, b_ref in enumerate(b_refs):
            def add(b_ref=b_ref):
                acc[...] += lax.dot_general(av, b_ref[...], _TN, preferred_element_type=F32)
            if len(bs) == 1:
                add()
            else:
                pl.when((j >= first[p]) & (j < first[p] + tiles[p]))(add)

        @pl.when(t == nt - 1)
        def _():
            o_ref[...] = acc[...].astype(BF16)

    def piece_spec(p):
        def index(i, j, t):
            mine = (j >= first[p]) & (j < first[p] + tiles[p])
            return jnp.where(mine, t, 0), jnp.where(mine, j - first[p], 0)
        return pl.BlockSpec((ts, tn), index)

    operands, extra_specs, aliases = [a, *bs], [], {}
    if into is not None:
        operands.append(into)
        extra_specs = [pl.BlockSpec(memory_space=pltpu.HBM)]
        aliases = {len(operands) - 1: 0}
    return pl.pallas_call(
        body, name=name, grid=(k // tk, n // tn, nt),
        in_specs=[pl.BlockSpec((ts, tk), lambda i, j, t: (t, i))] + [piece_spec(p) for p in range(len(bs))] + extra_specs,
        out_specs=pl.BlockSpec((tk, tn), lambda i, j, t: (i, tile0 + j)),
        out_shape=_sds((k, n_total or n), BF16),
        input_output_aliases=aliases,
        scratch_shapes=[pltpu.VMEM((tk, tn), F32)],
        compiler_params=_params(("parallel", "parallel", "arbitrary")),
    )(*operands)


def _mix_out_bwd(dx, mix, proj, ycv, yat, wout, wpc, wpa, g, gt, *, name, tm=256):
    s, d = dx.shape
    cw = wpc.shape[0]
    aw = wpa.shape[0]
    tm = min(tm, s)
    ga_blk = (3 * cw + 3 * aw) // d

    def body(dx_ref, mix_ref, ga_ref, gb_ref, ycv_ref, yat_ref, wout_ref, wpc_ref, wpa_ref, g_ref, gt_ref,
             dmix_ref, dycv_ref, dyat_ref, dyc_ref, do_ref, dgate_ref, dgt_ref, dg_ref):
        @pl.when(pl.program_id(0) == 0)
        def _():
            dgt_ref[...] = jnp.zeros_like(dgt_ref)
            dg_ref[...] = jnp.zeros_like(dg_ref)
        dxv = dx_ref[...]
        dn = dxv * gt_ref[...]
        dmix, xh = _rms_bwd(dn, mix_ref[...], g_ref[...])
        dgt_ref[...] += _colsum(dxv * (xh * g_ref[...]))
        dg_ref[...] += _colsum(dn * xh)
        dmixb = dmix.astype(BF16)
        dmix_ref[...] = dmixb
        dmerged = lax.dot_general(dmixb, wout_ref[...], _NT, preferred_element_type=F32)
        sga = _sigmoid(ga_ref[...].astype(F32))
        sgb = _sigmoid(gb_ref[...].astype(F32))
        dycv = (dmerged * sga).astype(BF16)
        dyat = (dmerged * sgb).astype(BF16)
        dycv_ref[...] = dycv
        dyat_ref[...] = dyat
        dgate_ref[:, 0:d] = (dmerged * ycv_ref[...].astype(F32) * (sga * (1.0 - sga))).astype(BF16)
        dgate_ref[:, d:2 * d] = (dmerged * yat_ref[...].astype(F32) * (sgb * (1.0 - sgb))).astype(BF16)
        dyc_ref[...] = lax.dot_general(dycv, wpc_ref[...], _NT, preferred_element_type=F32).astype(BF16)
        do_ref[...] = lax.dot_general(dyat, wpa_ref[...], _NT, preferred_element_type=F32).astype(BF16)

    def rows(w):
        return pl.BlockSpec((tm, w), lambda i: (i, 0))

    def full(a):
        return pl.BlockSpec(a.shape, lambda i: (0, 0))

    return pl.pallas_call(
        body, name=name, grid=(s // tm,),
        in_specs=[rows(d), rows(d), pl.BlockSpec((tm, d), lambda i: (i, ga_blk)),
                  pl.BlockSpec((tm, d), lambda i: (i, ga_blk + 1)), rows(d), rows(d),
                  full(wout), full(wpc), full(wpa), _vec_spec(d, 1), _vec_spec(d, 1)],
        out_specs=[rows(d), rows(d), rows(d), rows(cw), rows(aw), rows(2 * d), _vec_spec(d, 1), _vec_spec(d, 1)],
        out_shape=[_sds((s, d), BF16), _sds((s, d), BF16), _sds((s, d), BF16), _sds((s, cw), BF16),
                   _sds((s, aw), BF16), _sds((s, 2 * d), BF16), _sds((1, d), F32), _sds((1, d), F32)],
        compiler_params=_params(("arbitrary",)),
    )(dx, mix, proj, proj, ycv, yat, wout, wpc, wpa, g, gt)


def _conv_bwd(dyc, proj, conv_w, *, name, tm=512):
    s = proj.shape[0]
    cw = conv_w.shape[1]
    tm = min(tm, s)
    nb = tm // HALO
    nt = s // tm
    last_blk = s // HALO - 1

    def body(dyc_ref, bg_ref, cg_ref, u_ref, cgh_ref, uh_ref, dych_ref, bgh_ref, w_ref,
             dconv_ref, dw_ref, vbuf, gbuf):
        i = pl.program_id(0)

        @pl.when(i == 0)
        def _():
            dw_ref[...] = jnp.zeros_like(dw_ref)

        cg = cg_ref[...].astype(F32)
        u = u_ref[...].astype(F32)
        vv = cg * u
        halo = cgh_ref[...].astype(F32) * uh_ref[...].astype(F32)
        vbuf[0:HALO, :] = jnp.where(i > 0, halo, 0.0)
        vbuf[HALO:HALO + tm, :] = vv
        v1 = vbuf[HALO - 1:HALO - 1 + tm, :]
        v2 = vbuf[HALO - 2:HALO - 2 + tm, :]
        w = w_ref[...]
        y = w[2:3, :] * vv + w[1:2, :] * v1 + w[0:1, :] * v2
        dyc = dyc_ref[...].astype(F32)
        dconv_ref[:, 0:cw] = (dyc * y).astype(BF16)
        gy = dyc * bg_ref[...].astype(F32)
        nxt = dych_ref[...].astype(F32) * bgh_ref[...].astype(F32)
        gbuf[0:tm, :] = gy
        gbuf[tm:tm + HALO, :] = jnp.where(i < nt - 1, nxt, 0.0)
        g1 = gbuf[1:1 + tm, :]
        g2 = gbuf[2:2 + tm, :]
        dvv = w[2:3, :] * gy + w[1:2, :] * g1 + w[0:1, :] * g2
        dconv_ref[:, cw:2 * cw] = (dvv * u).astype(BF16)
        dconv_ref[:, 2 * cw:3 * cw] = (dvv * cg).astype(BF16)
        dw_ref[0:1, :] += _colsum(gy * v2)
        dw_ref[1:2, :] += _colsum(gy * v1)
        dw_ref[2:3, :] += _colsum(gy * vv)

    def prev(i):
        return jnp.maximum(i * nb - 1, 0)

    def nxt_blk(i):
        return jnp.minimum((i + 1) * nb, last_blk)

    def col(c):
        return pl.BlockSpec((tm, cw), lambda i: (i, c))

    return pl.pallas_call(
        body, name=name, grid=(nt,),
        in_specs=[col(0), col(0), col(1), col(2),
                  pl.BlockSpec((HALO, cw), lambda i: (prev(i), 1)), pl.BlockSpec((HALO, cw), lambda i: (prev(i), 2)),
                  pl.BlockSpec((HALO, cw), lambda i: (nxt_blk(i), 0)), pl.BlockSpec((HALO, cw), lambda i: (nxt_blk(i), 0)),
                  pl.BlockSpec((3, cw), lambda i: (0, 0))],
        out_specs=[pl.BlockSpec((tm, 3 * cw), lambda i: (i, 0)), pl.BlockSpec((3, cw), lambda i: (0, 0))],
        out_shape=[_sds((s, 3 * cw), BF16), _sds((3, cw), F32)],
        scratch_shapes=[pltpu.VMEM((HALO + tm, cw), F32), pltpu.VMEM((tm + HALO, cw), F32)],
        compiler_params=_params(("arbitrary",)),
    )(dyc, proj, proj, proj, proj, proj, dyc, proj, conv_w)


def _attn_bwd(proj, o, do, tri, *, d, name, side=None):
    s = proj.shape[0]
    qb = tri.shape[0]
    chains = _fit(ATTN_CHAINS[1], s // qb)
    ng = s // (qb * chains)
    q0, k0, v0, hp = _attn_cols(d)

    def body(*refs):
        p, g = pl.program_id(0), pl.program_id(1)
        own, late_phases = _host_side(
            side, 6, 3, 2, refs, (p == 0) & (g == 0), (p == hp - 1) & (g == 0), (p == hp - 1) & (g == ng - 1))
        q_ref, k_ref, v_ref, o_ref, do_ref, tri_ref, dq_ref, dk_ref, dv_ref, dk_acc, dv_acc = own

        @pl.when(g == 0)
        def _():
            dk_acc[...] = jnp.zeros_like(dk_acc)
            dv_acc[...] = jnp.zeros_like(dv_acc)

        tri_m = tri_ref[...]
        masks = [_head_mask(h) for h in range(2)]

        def first_step(u):
            i = g * chains + u
            mine = slice(u * qb, (u + 1) * qb)
            dov = do_ref[mine, :]
            qs = _stack_heads(q_ref[mine, :] * ATTN_SCALE, masks)
            dos = _stack_heads(dov, masks)
            dprod = dov.astype(F32) * o_ref[mine, :]
            dtot = jnp.concatenate([jnp.sum(jnp.where(m, dprod, 0.0), axis=-1, keepdims=True) for m in masks], axis=0)

            def strip(first_key, width, state, causal):
                run, grun, dq_acc = state
                rows = pl.ds(pl.multiple_of(first_key, qb), width)
                kb = k_ref[rows, :]
                z = lax.dot_general(qs, kb, _NT, preferred_element_type=F32)
                lg = _log_one_minus_sigmoid(z)
                beta = 1.0 - jnp.exp(lg)
                if causal is not None:
                    lg = jnp.where(causal, lg, 0.0)
                cs, total = _running_sum(lg, tri_m)
                a = jnp.exp(z + cs + run)
                if causal is not None:
                    a = jnp.where(causal, a, 0.0)
                ab = a.astype(BF16)
                da = lax.dot_general(dos, v_ref[rows, :], _NT, preferred_element_type=F32)
                gg = ab.astype(F32) * da
                gcs, gtotal = _running_sum(gg, tri_m)
                left = jnp.where(run > UNDERFLOW_LOG, dtot - grun, 0.0)
                dz = gg - beta * (gg + (left - gcs))
                if causal is not None:
                    dz = jnp.where(causal, dz, 0.0)
                dzb = dz.astype(BF16)
                dq_acc = dq_acc + jnp.dot(_heads_to_lanes(dzb, qb), _stack_heads(kb, masks),
                                          preferred_element_type=F32)
                dk_add = lax.dot_general(dzb, qs, _TN, preferred_element_type=F32)
                dv_add = lax.dot_general(ab, dos, _TN, preferred_element_type=F32)
                return (run + total, grun + gtotal, dq_acc), rows, dk_add, dv_add

            zero = jnp.zeros((2 * qb, 1), F32)
            first_key = jnp.maximum(i - (ATTN_FIRST_BLOCKS - 1), 0) * qb
            width = ATTN_FIRST_BLOCKS * qb
            return (i, strip) + strip(first_key, width, (zero, zero, jnp.zeros((qb, LANES), F32)),
                                      _stacked_causal(qb, width, first_key, i * qb))

        started = [first_step(u) for u in range(chains)]
        for u, (i, strip, state, rows, dk_add, dv_add) in enumerate(started):
            dk_acc[rows, :] += dk_add
            dv_acc[rows, :] += dv_add

            def more(st, strip=strip):
                state, rows, dk_add, dv_add = strip(st[0] * qb, qb, st[1:], None)
                dk_acc[rows, :] += dk_add
                dv_acc[rows, :] += dv_add
                return (st[0] - 1, *state)

            state = lax.while_loop(lambda st: (st[0] >= 0) & (jnp.max(st[1]) > UNDERFLOW_LOG), more,
                                   (i - ATTN_FIRST_BLOCKS, *state))
            dq_ref[u * qb:(u + 1) * qb, :] = (state[3] * ATTN_SCALE).astype(BF16)

        @pl.when(g == ng - 1)
        def _():
            dk_ref[...] = dk_acc[...].astype(BF16)
            dv_ref[...] = dv_acc[...].astype(BF16)

        late_phases()

    aw = hp * LANES
    tq = qb * chains
    s_in, s_out, s_shapes, aliases, s_scratch = _side_specs(side, 6, 3)
    res = pl.pallas_call(
        body, name=name, grid=(hp, ng),
        in_specs=[pl.BlockSpec((tq, LANES), lambda p, i: (i, q0 + p)),
                  pl.BlockSpec((s, LANES), lambda p, i: (0, k0 + p)),
                  pl.BlockSpec((s, LANES), lambda p, i: (0, v0 + p)),
                  pl.BlockSpec((tq, LANES), lambda p, i: (i, p)),
                  pl.BlockSpec((tq, LANES), lambda p, i: (i, p)),
                  pl.BlockSpec((qb, qb), lambda p, i: (0, 0))] + s_in,
        out_specs=[pl.BlockSpec((tq, LANES), lambda p, i: (i, p)),
                   pl.BlockSpec((s, LANES), lambda p, i: (0, p)),
                   pl.BlockSpec((s, LANES), lambda p, i: (0, p))] + s_out,
        out_shape=[_sds((s, aw), BF16), _sds((s, aw), BF16), _sds((s, aw), BF16)] + s_shapes,
        input_output_aliases=aliases,
        scratch_shapes=[pltpu.VMEM((s, LANES), F32), pltpu.VMEM((s, LANES), F32)] + s_scratch,
        compiler_params=_params(("arbitrary", "arbitrary")),
    )(proj, proj, proj, o, do, tri, *([] if side is None else side.operands))
    return res[:3], res[3:]


def _hosted(hooks, kind, l, fn, *args, **kw):
    res, side_out = fn(*args, side=hooks.side(kind, l), **kw)
    hooks.done(kind, l, side_out)
    return res


def _layer_fwd(x, mod, gains, conv_w, tri, *, l, hooks):
    sh1, sc1, gt1, sh2, sc2, gt2 = mod
    g_pre_mix, g_post_mix, g_pre_mlp, g_post_mlp = gains
    d = x.shape[1]
    w = functools.partial(hooks.weight, l)
    h, proj = _hosted(hooks, "in_proj", l, _norm_mod_matmul, x, g_pre_mix, sc1, sh1, w("w_in"), name=f"in_proj_{l}")
    yc = _conv_fwd(proj, conv_w, name=f"conv_fwd_{l}")
    o = _hosted(hooks, "attn_fwd", l, _attn_fwd, proj, tri, d=d, name=f"attn_fwd_{l}")
    ycv, yat, merged, mix, x1 = _mix_out(yc, o, proj, x, w("w_proj_conv"), w("w_proj_attn"), w("w_out"),
                                         g_post_mix, gt1, name=f"mix_out_{l}")
    (h2, a), _ = _norm_mod_matmul(x1, g_pre_mlp, sc2, sh2, w("w_mlp_in"), name=f"mlp_in_{l}")
    ff, x2 = _mlp_out(a, x1, w("w_mlp_out"), g_post_mlp, gt2, name=f"mlp_out_{l}")
    saved = dict(x=x, h=h, proj=proj, yc=yc, o=o, ycv=ycv, yat=yat, merged=merged, mix=mix, x1=x1, h2=h2, a=a, ff=ff,
                 conv_w=conv_w, **{k: w(k) for k in BIG})
    return x2, saved


def _layer_bwd(dx2, sv, mod, gains, tri, *, l, hooks):
    sh1, sc1, gt1, sh2, sc2, gt2 = mod
    g_pre_mix, g_post_mix, g_pre_mlp, g_post_mlp = gains
    d = dx2.shape[1]
    dff, da, dgt2, dg_post_mlp = _mlp_out_bwd(dx2, sv["ff"], sv["a"], sv["w_mlp_out"], g_post_mlp, gt2,
                                              name=f"mlp_out_bwd_{l}")
    hooks.grad(l, "w_mlp_out", _matmul_tn(sv["a"], [dff], relu2=True, name=f"gw_mlp_out_{l}"))
    (dx1, dsh2, dsc2, dg_pre_mlp), _ = _matmul_nt_norm_bwd([da], sv["w_mlp_in"], sv["x1"], dx2, g_pre_mlp, sc2,
                                                           name=f"mlp_in_bwd_{l}")
    hooks.grad(l, "w_mlp_in", _matmul_tn(sv["h2"], [da], name=f"gw_mlp_in_{l}"))
    dmix, dycv, dyat, dyc, do, dgate, dgt1, dg_post_mix = _mix_out_bwd(
        dx1, sv["mix"], sv["proj"], sv["ycv"], sv["yat"], sv["w_out"], sv["w_proj_conv"], sv["w_proj_attn"],
        g_post_mix, gt1, name=f"mix_out_bwd_{l}")
    hooks.grad(l, "w_out", _matmul_tn(sv["merged"], [dmix], name=f"gw_out_{l}"))
    hooks.grad(l, "w_proj_conv", _matmul_tn(sv["yc"], [dycv], name=f"gw_proj_conv_{l}"))
    hooks.grad(l, "w_proj_attn", _matmul_tn(sv["o"], [dyat], name=f"gw_proj_attn_{l}"))
    dconv, g_conv_w = _conv_bwd(dyc, sv["proj"], sv["conv_w"], name=f"conv_bwd_{l}")
    dq, dk, dv = _hosted(hooks, "attn_bwd", l, _attn_bwd, sv["proj"], sv["o"], do, tri, d=d, name=f"attn_bwd_{l}")
    dproj = [dconv, dq, dk, dv, dgate]
    n_in = sv["w_in"].shape[1]
    gw_in = _matmul_tn(sv["h"], [dconv], tn=768, n_total=n_in, name=f"gw_in_conv_{l}")
    gw_in = _matmul_tn(sv["h"], [dq, dk, dv], into=gw_in, col0=dconv.shape[1], n_total=n_in, name=f"gw_in_attn_{l}")
    gw_in = _matmul_tn(sv["h"], [dgate], into=gw_in, col0=n_in - dgate.shape[1], n_total=n_in, name=f"gw_in_gate_{l}")
    hooks.grad(l, "w_in", gw_in)
    dx0, dsh1, dsc1, dg_pre_mix = _hosted(hooks, "in_proj_bwd", l, _matmul_nt_norm_bwd, dproj, sv["w_in"], sv["x"], dx1,
                                          g_pre_mix, sc1, name=f"in_proj_bwd_{l}")
    dmod = jnp.concatenate([dsh1, dsc1, dgt1, dsh2, dsc2, dgt2], axis=0)
    dgains = jnp.concatenate([dg_pre_mix, dg_post_mix, dg_pre_mlp, dg_post_mlp], axis=0)
    return dx0, g_conv_w, dmod, dgains


BIG = ("w_in", "w_proj_conv", "w_proj_attn", "w_out", "w_mlp_in", "w_mlp_out")
SHARD_AXIS = dict(w_in=1, w_proj_conv=1, w_proj_attn=1, w_out=0, w_mlp_in=1, w_mlp_out=0)


class _LocalWeights:
    def __init__(self, wlayers):
        self.wlayers = wlayers
        self.grads = {}

    def weight(self, l, name):
        return self.wlayers[l][name]

    def side(self, kind, l):
        return None

    def done(self, kind, l, outs):
        pass

    def grad(self, l, name, g):
        self.grads[(l, name)] = g


def _local_step(x, target, mods, gains, conv_w, hooks):
    depth = mods.shape[0]
    tri = _tri(ATTN_BLOCK)
    saved = []
    for l in range(depth):
        mod = [mods[l, k:k + 1] for k in range(N_MOD)]
        gl = [gains[l, k:k + 1] for k in range(4)]
        x, sv = _layer_fwd(x, mod, gl, conv_w[l], tri, l=l, hooks=hooks)
        saved.append((sv, mod, gl))
    dx, loss = _loss_grad(x, target, name="loss_grad")
    dconv, dmods, dgains = [None] * depth, [None] * depth, [None] * depth
    for l in reversed(range(depth)):
        sv, mod, gl = saved[l]
        dx, dconv[l], dmods[l], dgains[l] = _layer_bwd(dx, sv, mod, gl, tri, l=l, hooks=hooks)
    return loss, dx, jnp.stack(dconv), jnp.stack(dmods), jnp.stack(dgains)


def _coords():
    return lax.axis_index("x"), lax.axis_index("y"), lax.axis_index("c")


def _flip(v, f):
    return 1 - v if f else v


def _all_gather_small(v, *, name):
    r, c_ = v.shape

    def body(v_ref, out_ref, send_sems, recv_sems, local_sem):
        x, y, c = _coords()
        me = 4 * x + 2 * y + c
        mine = pltpu.make_async_copy(v_ref, out_ref.at[me], local_sem)
        mine.start()
        copies = []
        for k in range(1, 8):
            fx, fy, fc = (k >> 2) & 1, (k >> 1) & 1, k & 1
            px, py, pc = _flip(x, fx), _flip(y, fy), _flip(c, fc)
            out = pltpu.make_async_remote_copy(src_ref=v_ref, dst_ref=out_ref.at[me], send_sem=send_sems.at[k - 1],
                                               recv_sem=recv_sems.at[k - 1], device_id=(px, py, pc), device_id_type=MESH)
            out.start()
            back = pltpu.make_async_remote_copy(src_ref=v_ref, dst_ref=out_ref.at[4 * px + 2 * py + pc],
                                                send_sem=send_sems.at[k - 1], recv_sem=recv_sems.at[k - 1],
                                                device_id=(px, py, pc), device_id_type=MESH)
            copies.append((out, back))
        for out, back in copies:
            back.wait_recv()
        for out, back in copies:
            out.wait_send()
        mine.wait()

    return pl.pallas_call(
        body, name=name,
        in_specs=[pl.BlockSpec(memory_space=pltpu.VMEM)],
        out_specs=pl.BlockSpec(memory_space=pltpu.VMEM),
        out_shape=_sds((8, r, c_), F32),
        scratch_shapes=[pltpu.SemaphoreType.DMA((7,)), pltpu.SemaphoreType.DMA((7,)), pltpu.SemaphoreType.DMA],
    )(v)


def _shard_dims(full_shape, axis):
    k, n = full_shape
    return (k // 4, n) if axis == 0 else (k, n // 4)


def _shard_window(ref, axis, chip, half, rows, cols):
    r0, rn = (0, rows) if half is None else (half * (rows // 2), rows // 2)
    if axis == 1:
        return ref.at[pl.ds(r0, rn), pl.ds(chip * cols, cols)]
    return ref.at[pl.ds(chip * rows + r0, rn), :]


def _cast_place(w, layer, axis, chip_arr, *, name, tr=256):
    _, rows, cols = w.shape
    tr = _fit(tr, rows)
    nb = rows // tr
    full = (rows * 4, cols) if axis == 0 else (rows, cols * 4)

    def body(chip_ref, w_ref, o_ref):
        o_ref[...] = w_ref[0].astype(BF16)

    if axis == 1:
        out_map = lambda i, chip: (i, chip[0])
    else:
        out_map = lambda i, chip: (chip[0] * nb + i, 0)
    grid_spec = pltpu.PrefetchScalarGridSpec(
        num_scalar_prefetch=1, grid=(nb,),
        in_specs=[pl.BlockSpec((1, tr, cols), lambda i, chip: (layer, i, 0))],
        out_specs=pl.BlockSpec((tr, cols), out_map))
    return pl.pallas_call(body, name=name, grid_spec=grid_spec, out_shape=_sds(full, BF16),
                          compiler_params=_params(("arbitrary",)))(chip_arr, w)


def _gather_side(fulls, axes):
    n = len(fulls)

    def copies(outs, sems):
        send_sems, recv_sems = sems
        x, y, c = _coords()
        chip = 2 * x + y
        sibling = (x, y, 1 - c)
        table = []
        for w in range(n):
            rows, cols = _shard_dims(outs[w].shape, axes[w])
            win = functools.partial(_shard_window, outs[w], axes[w], rows=rows, cols=cols)
            for j, (fx, fy) in enumerate(OTHER_CHIPS):
                px, py = _flip(x, fx), _flip(y, fy)
                pchip = 2 * px + py

                def copy(piece, sem, to):
                    return pltpu.make_async_remote_copy(src_ref=piece, dst_ref=piece, send_sem=send_sems.at[w, sem],
                                                        recv_sem=recv_sems.at[w, sem], device_id=to, device_id_type=MESH)

                table.append((copy(win(chip, c), j, (px, py, c)), copy(win(pchip, c), j, (px, py, c)),
                              copy(win(pchip, c), 3 + j, sibling), copy(win(pchip, 1 - c), 3 + j, sibling)))
        return table

    def start(ins, outs, sems):
        for send, _, _, _ in copies(outs, sems):
            send.start()

    def mid(ins, outs, sems):
        for _, landed, pass_on, _ in copies(outs, sems):
            landed.wait_recv()
            pass_on.start()

    def finish(ins, outs, sems):
        table = copies(outs, sems)
        for _, _, _, from_sibling in table:
            from_sibling.wait_recv()
        for send, _, pass_on, _ in table:
            send.wait_send()
            pass_on.wait_send()

    return _Side(fulls, [_sds(f.shape, f.dtype) for f in fulls], {w: w for w in range(n)},
                 [pltpu.SemaphoreType.DMA((n, 6)), pltpu.SemaphoreType.DMA((n, 6))], start, mid, finish)


def _exchange_side(grads, axes):
    n = len(grads)
    out_shapes = []
    for g, ax in zip(grads, axes):
        rows, cols = _shard_dims(g.shape, ax)
        out_shapes.append(_sds((7, rows // 2, cols), g.dtype))

    def copies(ins, outs, sems):
        send_sems, recv_sems = sems
        x, y, c = _coords()
        table = []
        for w in range(n):
            rows, cols = _shard_dims(ins[w].shape, axes[w])
            for k in range(1, 8):
                fx, fy, fc = (k >> 2) & 1, (k >> 1) & 1, k & 1
                px, py, pc = _flip(x, fx), _flip(y, fy), _flip(c, fc)
                piece = _shard_window(ins[w], axes[w], 2 * px + py, pc, rows, cols)
                table.append(pltpu.make_async_remote_copy(
                    src_ref=piece, dst_ref=outs[w].at[k - 1], send_sem=send_sems.at[w, k - 1],
                    recv_sem=recv_sems.at[w, k - 1], device_id=(px, py, pc), device_id_type=MESH))
        return table

    def start(ins, outs, sems):
        for cp in copies(ins, outs, sems):
            cp.start()

    def finish(ins, outs, sems):
        table = copies(ins, outs, sems)
        for cp in table:
            cp.wait_recv()
        for cp in table:
            cp.wait_send()

    return _Side(grads, out_shapes, {}, [pltpu.SemaphoreType.DMA((n, 7)), pltpu.SemaphoreType.DMA((n, 7))],
                 start, None, finish)


def _rs_sum_join(g, got, out_prev, layer, depth, axis, ids, *, name, tr=256):
    _, rows2, cols = got.shape
    tr = _fit(tr, rows2)
    nt = rows2 // tr
    if axis == 1:
        own_map = lambda i, ids_: (ids_[1] * nt + i, ids_[0])
    else:
        own_map = lambda i, ids_: ((ids_[0] * 2 + ids_[1]) * nt + i, 0)

    def body(ids_ref, g_ref, got_ref, *rest):
        out_ref, buf, local_sems, send_sems, recv_sem = rest[-5:]
        i = pl.program_id(0)
        x, y, c = _coords()
        sibling = (x, y, 1 - c)

        def copies(step, slot):
            rows_mine = pl.ds(c * rows2 + step * tr, tr)
            dst = out_ref.at[layer, rows_mine, :]
            keep = pltpu.make_async_copy(buf.at[slot], dst, local_sems.at[slot])
            give = pltpu.make_async_remote_copy(src_ref=buf.at[slot], dst_ref=dst, send_sem=send_sems.at[slot],
                                                recv_sem=recv_sem, device_id=sibling, device_id_type=MESH)
            return keep, give

        def drain(step, slot):
            keep, give = copies(step, slot)
            keep.wait()
            give.wait_send()

        slot = i % 2

        @pl.when(i >= 2)
        def _():
            drain(i - 2, slot)

        acc = g_ref[...].astype(F32)
        for k in range(7):
            acc = acc + got_ref[k].astype(F32)
        buf[slot] = acc
        keep, give = copies(i, slot)
        keep.start()
        give.start()

        @pl.when(i == nt - 1)
        def _():
            if nt >= 2:
                drain(nt - 2, (nt - 2) % 2)
            drain(nt - 1, (nt - 1) % 2)
            theirs = out_ref.at[layer, pl.ds((1 - c) * rows2, rows2), :]
            pltpu.make_async_remote_copy(src_ref=theirs, dst_ref=theirs, send_sem=send_sems.at[0], recv_sem=recv_sem,
                                         device_id=sibling, device_id_type=MESH).wait_recv()

    hbm = pl.BlockSpec(memory_space=pltpu.HBM)
    in_specs = [pl.BlockSpec((tr, cols), own_map), pl.BlockSpec((7, tr, cols), lambda i, ids_: (0, i, 0))]
    operands = [ids, g, got]
    aliases = {}
    if out_prev is not None:
        in_specs.append(hbm)
        operands.append(out_prev)
        aliases = {3: 0}
    grid_spec = pltpu.PrefetchScalarGridSpec(
        num_scalar_prefetch=1, grid=(nt,), in_specs=in_specs, out_specs=hbm,
        scratch_shapes=[pltpu.VMEM((2, tr, cols), F32), pltpu.SemaphoreType.DMA((2,)), pltpu.SemaphoreType.DMA((2,)),
                        pltpu.SemaphoreType.DMA])
    return pl.pallas_call(body, name=name, grid_spec=grid_spec, out_shape=_sds((depth, 2 * rows2, cols), F32),
                          input_output_aliases=aliases, compiler_params=_params(("arbitrary",)))(*operands)


MIX = ("w_proj_conv", "w_proj_attn", "w_out")


class _Schedule:
    def __init__(self, placed, depth, ids):
        self.placed, self.depth, self.ids = placed, depth, ids
        self.full, self.g, self.carried = {}, {}, None
        self.reduced = {k: None for k in BIG}
        first = [(0, "w_in")]
        self._landed(first, _side_call(self._gather(first), name="gather_w_in_0"))

    def _gather(self, keys):
        return _gather_side([self.placed[k] for k in keys], [SHARD_AXIS[k[1]] for k in keys])

    def _landed(self, keys, outs):
        for k, o in zip(keys, outs):
            self.full[k] = o

    def _exchange(self, keys):
        return _exchange_side([self.g[k] for k in keys], [SHARD_AXIS[k[1]] for k in keys])

    def _reduce(self, keys, got):
        for (l, name), pieces in zip(keys, got):
            self.reduced[name] = _rs_sum_join(self.g[(l, name)], pieces, self.reduced[name], l, self.depth,
                                              SHARD_AXIS[name], self.ids, name=f"rs_sum_join_{l}_{name}")

    def weight(self, l, name):
        return self.full[(l, name)]

    def grad(self, l, name, g):
        self.g[(l, name)] = g

    def side(self, kind, l):
        nxt = [(l + 1, "w_in")] if l + 1 < self.depth else []
        if kind == "in_proj":
            keys, make = [(l, k) for k in MIX + ("w_mlp_in",)], self._gather
        elif kind == "attn_fwd":
            keys, make = [(l, "w_mlp_out")] + nxt, self._gather
        elif kind == "attn_bwd":
            keys, make = [(l, k) for k in ("w_mlp_out", "w_mlp_in") + MIX], self._exchange
        else:
            keys, make = [(l, "w_in")], self._exchange
        self.carried = keys
        return make(keys)

    def done(self, kind, l, outs):
        (self._landed if kind in ("in_proj", "attn_fwd") else self._reduce)(self.carried, outs)


def _flat_rows(shape):
    rows = 1
    for s in shape[:-1]:
        rows *= s
    return rows, shape[-1]


def _row_tile(rows, cols, cap_bytes=2 * 1024 * 1024):
    t = rows
    while t * cols * 4 > cap_bytes and t % 16 == 0:
        t //= 2
    return t


def _ada_fwd(c_all, w_ada, b_loc, *, name, tn=512):
    l, d, nl = w_ada.shape
    b = c_all.shape[0]
    tn = min(tn, nl)

    def body(c_ref, w_ref, b_ref, o_ref):
        o_ref[0] = jnp.dot(c_ref[...], w_ref[0], preferred_element_type=F32,
                           precision=lax.Precision.HIGHEST) + b_ref[0]

    return pl.pallas_call(
        body, name=name, grid=(l, nl // tn),
        in_specs=[pl.BlockSpec((b, d), lambda i, j: (0, 0)), pl.BlockSpec((1, d, tn), lambda i, j: (i, 0, j)),
                  pl.BlockSpec((1, 1, tn), lambda i, j: (i, 0, j))],
        out_specs=pl.BlockSpec((1, b, tn), lambda i, j: (i, 0, j)),
        out_shape=_sds((l, b, nl), F32),
        compiler_params=_params(("parallel", "parallel")),
    )(c_all, w_ada, b_loc)


def _ada_bwd(c_t, dmod_loc, *, name, tn=512):
    d, b = c_t.shape
    l, _, nl = dmod_loc.shape
    tn = min(tn, nl)

    def body(c_ref, dm_ref, o_ref):
        cv = c_ref[...]
        dm = dm_ref[0]
        acc = cv[:, 0:1] * dm[0:1, :]
        for k in range(1, b):
            acc = acc + cv[:, k:k + 1] * dm[k:k + 1, :]
        o_ref[0] = acc

    return pl.pallas_call(
        body, name=name, grid=(l, nl // tn),
        in_specs=[pl.BlockSpec((d, b), lambda i, j: (0, 0)), pl.BlockSpec((1, b, tn), lambda i, j: (i, 0, j))],
        out_specs=pl.BlockSpec((1, d, tn), lambda i, j: (i, 0, j)),
        out_shape=_sds((l, d, nl), F32),
        compiler_params=_params(("parallel", "parallel")),
    )(c_t, dmod_loc)


def _sum_devices(p, *, name):
    k, r, c_ = p.shape

    def body(p_ref, o_ref):
        acc = p_ref[0]
        for j in range(1, k):
            acc = acc + p_ref[j]
        o_ref[...] = acc

    return pl.pallas_call(body, name=name, out_shape=_sds((r, c_), F32),
                          in_specs=[pl.BlockSpec(memory_space=pltpu.VMEM)],
                          out_specs=pl.BlockSpec(memory_space=pltpu.VMEM))(p)


def _adamw(w, g, m, v, *, name):
    shape = w.shape
    rows, cols = _flat_rows(shape)
    tr = _row_tile(rows, cols, cap_bytes=1024 * 1024)
    c1 = 1.0 / (1.0 - ADAM_B1 ** ADAM_STEP)
    c2 = 1.0 / (1.0 - ADAM_B2 ** ADAM_STEP)

    def body(w_ref, g_ref, m_ref, v_ref, d_ref, nm_ref, nv_ref):
        gv = g_ref[...]
        nm = ADAM_B1 * m_ref[...] + (1.0 - ADAM_B1) * gv
        nv = ADAM_B2 * v_ref[...] + (1.0 - ADAM_B2) * (gv * gv)
        m_hat = nm * c1
        v_hat = nv * c2
        d_ref[...] = -ADAM_LR * (m_hat / (jnp.sqrt(v_hat) + ADAM_EPS) + ADAM_WD * w_ref[...])
        nm_ref[...] = nm
        nv_ref[...] = nv

    spec = pl.BlockSpec((tr, cols), lambda i: (i, 0))
    flat = lambda a: a.reshape(rows, cols)
    outs = pl.pallas_call(body, name=name, grid=(rows // tr,), in_specs=[spec] * 4, out_specs=[spec] * 3,
                          out_shape=[_sds((rows, cols), F32)] * 3, compiler_params=_params(("parallel",)),
                          )(flat(w), flat(g), flat(m), flat(v))
    return tuple(o.reshape(shape) for o in outs)


WEIGHTS = ("w_ada", "b_ada", "g_pre_mix", "g_post_mix", "g_pre_mlp", "g_post_mlp", "w_in", "conv_w",
           "w_proj_conv", "w_proj_attn", "w_out", "w_mlp_in", "w_mlp_out")
GAINS = ("g_pre_mix", "g_post_mix", "g_pre_mlp", "g_post_mlp")


def kernel(x, c, w_ada, b_ada, g_pre_mix, g_post_mix, g_pre_mlp, g_post_mlp, w_in, conv_w, w_proj_conv, w_proj_attn, w_out, w_mlp_in, w_mlp_out, loss_target, m_w_ada, m_b_ada, m_g_pre_mix, m_g_post_mix, m_g_pre_mlp, m_g_post_mlp, m_w_in, m_conv_w, m_w_proj_conv, m_w_proj_attn, m_w_out, m_w_mlp_in, m_w_mlp_out, v_w_ada, v_b_ada, v_g_pre_mix, v_g_post_mix, v_g_pre_mlp, v_g_post_mlp, v_w_in, v_conv_w, v_w_proj_conv, v_w_proj_attn, v_w_out, v_w_mlp_in, v_w_mlp_out):
    params = dict(w_ada=w_ada, b_ada=b_ada, g_pre_mix=g_pre_mix, g_post_mix=g_post_mix, g_pre_mlp=g_pre_mlp,
                  g_post_mlp=g_post_mlp, w_in=w_in, conv_w=conv_w, w_proj_conv=w_proj_conv, w_proj_attn=w_proj_attn,
                  w_out=w_out, w_mlp_in=w_mlp_in, w_mlp_out=w_mlp_out)
    m_in = dict(w_ada=m_w_ada, b_ada=m_b_ada, g_pre_mix=m_g_pre_mix, g_post_mix=m_g_post_mix, g_pre_mlp=m_g_pre_mlp,
                g_post_mlp=m_g_post_mlp, w_in=m_w_in, conv_w=m_conv_w, w_proj_conv=m_w_proj_conv,
                w_proj_attn=m_w_proj_attn, w_out=m_w_out, w_mlp_in=m_w_mlp_in, w_mlp_out=m_w_mlp_out)
    v_in = dict(w_ada=v_w_ada, b_ada=v_b_ada, g_pre_mix=v_g_pre_mix, g_post_mix=v_g_post_mix, g_pre_mlp=v_g_pre_mlp,
                g_post_mlp=v_g_post_mlp, w_in=v_w_in, conv_w=v_conv_w, w_proj_conv=v_w_proj_conv,
                w_proj_attn=v_w_proj_attn, w_out=v_w_out, w_mlp_in=v_w_mlp_in, w_mlp_out=v_w_mlp_out)

    depth, d, nl_ada = w_ada.shape
    ix, iy, ic = lax.axis_index("x"), lax.axis_index("y"), lax.axis_index("c")
    chip = 2 * ix + iy
    me = 4 * ix + 2 * iy + ic
    xs = x[0]
    target = loss_target[0]

    c_all = _all_gather_small(jnp.broadcast_to(c, (8, d)), name="gather_c")[:, 0, :]
    b_loc = lax.dynamic_slice_in_dim(b_ada, chip * nl_ada, nl_ada, axis=1)[:, None, :]
    mod_loc = _ada_fwd(c_all, w_ada, b_loc, name="ada_fwd")
    mod_all = _all_gather_small(mod_loc.reshape(depth * 8, nl_ada), name="gather_mod")
    mod_all = mod_all.reshape(4, 2, depth, 8, nl_ada)[:, 0]
    mod_me = lax.dynamic_index_in_dim(mod_all, me, axis=2, keepdims=False)
    mods = jnp.transpose(mod_me, (1, 0, 2)).reshape(depth, N_MOD, d)

    chip_arr = jnp.reshape(chip, (1,)).astype(jnp.int32)
    ids = jnp.stack([chip, ic]).astype(jnp.int32)
    placed = {(l, k): _cast_place(params[k], l, SHARD_AXIS[k], chip_arr, name=f"place_{k}_{l}")
              for l in range(depth) for k in BIG}
    conv_full = _all_gather_small(
        jnp.pad(conv_w.reshape(depth * 3, -1), ((0, 8 - depth * 3), (0, 0))), name="gather_conv_w")
    conv_full = conv_full.reshape(4, 2, 8, -1)[:, 0, :depth * 3]
    conv_full = jnp.transpose(conv_full, (1, 0, 2)).reshape(depth, 3, -1)

    gains = jnp.stack([params[k] for k in GAINS], axis=1)
    schedule = _Schedule(placed, depth, ids)
    loss, dx, conv_grads, dmods, dgains = _local_step(xs, target, mods, gains, conv_full, schedule)

    cw = conv_full.shape[2]
    rows = [dmods.reshape(depth * N_MOD, d), dgains.reshape(depth * 4, d),
            conv_grads.reshape(-1, d), jnp.broadcast_to(loss, (1, d))]
    payload = jnp.concatenate(rows, axis=0)
    n_rows = payload.shape[0]
    pad = (-n_rows) % 8
    payload = jnp.pad(payload, ((0, pad), (0, 0)))
    everyone = _all_gather_small(payload, name="gather_small_grads")
    total = _sum_devices(everyone, name="sum_small_grads")
    r0 = depth * N_MOD
    grads = {}
    grads["b_ada"] = total[:r0].reshape(depth, N_MOD * d)
    gsum = total[r0:r0 + depth * 4].reshape(depth, 4, d)
    for k, name in enumerate(GAINS):
        grads[name] = gsum[:, k]
    r1 = r0 + depth * 4
    n_conv = (depth * 3 * cw) // d
    conv_g = total[r1:r1 + n_conv].reshape(depth, 3, cw)
    grads["conv_w"] = lax.dynamic_slice_in_dim(conv_g, chip * (cw // 4), cw // 4, axis=2)
    loss_out = total[r1 + n_conv, 0]
    dmod_all = everyone[:, :r0].reshape(8, depth, N_MOD * d)
    dmod_loc = lax.dynamic_slice_in_dim(dmod_all, chip * nl_ada, nl_ada, axis=2)
    grads["w_ada"] = _ada_bwd(c_all.T, jnp.transpose(dmod_loc, (1, 0, 2)), name="ada_bwd")

    for k in BIG:
        grads[k] = schedule.reduced[k]

    deltas, new_m, new_v = {}, {}, {}
    for k in WEIGHTS:
        deltas[k], new_m[k], new_v[k] = _adamw(params[k], grads[k], m_in[k], v_in[k], name=f"adamw_{k}")

    return (loss_out, dx[None], *[grads[k] for k in WEIGHTS], *[deltas[k] for k in WEIGHTS],
            *[new_m[k] for k in WEIGHTS], *[new_v[k] for k in WEIGHTS])
```

```python
import functools

import jax
import jax.numpy as jnp
from jax import lax
from jax.experimental import pallas as pl
from jax.experimental.pallas import tpu as pltpu

F32 = jnp.float32
BF16 = jnp.bfloat16
EPS = 1e-6
N_MOD = 6
HEAD_DIM = 64
LANES = 128
ATTN_SCALE = 1.0 / 8.0
UNDERFLOW_LOG = -90.0
ATTN_BLOCK = 256
ATTN_CHAINS = (1, 2)
ADAM_LR = 0.001
ADAM_B1 = 0.9
ADAM_B2 = 0.999
ADAM_EPS = 1e-08
ADAM_WD = 0.01
ADAM_STEP = 10
VMEM_LIMIT = 56 * 1024 * 1024
MESH = pl.DeviceIdType.MESH
OTHER_CHIPS = ((1, 0), (0, 1), (1, 1))

_NT = (((1,), (1,)), ((), ()))
_TN = (((0,), (0,)), ((), ()))


def _sds(shape, dtype):
    return jax.ShapeDtypeStruct(shape, dtype)


def _params(sem):
    return pltpu.CompilerParams(dimension_semantics=sem, vmem_limit_bytes=VMEM_LIMIT)


def _fit(t, n):
    t = min(t, n)
    while n % t:
        t //= 2
    return t


def _vec_spec(d, nargs):
    if nargs == 1:
        return pl.BlockSpec((1, d), lambda i: (0, 0))
    return pl.BlockSpec((1, d), lambda i, j: (0, 0))


def _log_one_minus_sigmoid(z):
    return -jnp.log(1.0 + jnp.exp(-jnp.abs(z))) - jnp.maximum(z, 0.0)


def _sigmoid(z):
    return 0.5 * jnp.tanh(0.5 * z) + 0.5


def _split_bf16(a):
    hi = a.astype(BF16)
    lo = (a - hi.astype(F32)).astype(BF16)
    return hi, lo


def _rms_bwd(dn, xin, g):
    r = lax.rsqrt(jnp.mean(xin * xin, axis=-1, keepdims=True) + EPS)
    xh = xin * r
    dxh = dn * g
    dxin = r * (dxh - xh * jnp.mean(dxh * xh, axis=-1, keepdims=True))
    return dxin, xh


def _colsum(a):
    return jnp.sum(a, axis=0, keepdims=True)


def _norm_mod_matmul(x, g, sc, sh, w, *, name, tm=256, side=None):
    s, d = x.shape
    n = w.shape[1]
    tm = _fit(tm, s)
    nt = s // tm

    def body(*refs):
        i = pl.program_id(0)
        (x_ref, g_ref, sc_ref, sh_ref, w_ref, h_ref, o_ref), late_phases = _host_side(
            side, 5, 2, 0, refs, i == 0, i == (3 * nt) // 4, i == nt - 1)
        xv = x_ref[...]
        r = lax.rsqrt(jnp.mean(xv * xv, axis=-1, keepdims=True) + EPS)
        h = ((xv * r * g_ref[...]) * (1.0 + sc_ref[...]) + sh_ref[...]).astype(BF16)
        h_ref[...] = h
        o_ref[...] = jnp.dot(h, w_ref[...], preferred_element_type=F32).astype(BF16)
        late_phases()

    s_in, s_out, s_shapes, aliases, s_scratch = _side_specs(side, 5, 2)
    res = pl.pallas_call(
        body, name=name, grid=(nt,),
        in_specs=[pl.BlockSpec((tm, d), lambda i: (i, 0)), _vec_spec(d, 1), _vec_spec(d, 1), _vec_spec(d, 1),
                  pl.BlockSpec((d, n), lambda i: (0, 0))] + s_in,
        out_specs=[pl.BlockSpec((tm, d), lambda i: (i, 0)), pl.BlockSpec((tm, n), lambda i: (i, 0))] + s_out,
        out_shape=[_sds((s, d), BF16), _sds((s, n), BF16)] + s_shapes,
        input_output_aliases=aliases, scratch_shapes=s_scratch,
        compiler_params=_params(("arbitrary",)),
    )(x, g, sc, sh, w, *([] if side is None else side.operands))
    return res[:2], res[2:]


HALO = 16


def _conv_fwd(proj, conv_w, *, name, tm=512):
    s = proj.shape[0]
    cw = conv_w.shape[1]
    tm = min(tm, s)
    nb = tm // HALO

    def body(bg_ref, cg_ref, u_ref, cgh_ref, uh_ref, w_ref, yc_ref, vbuf):
        i = pl.program_id(0)
        vv = cg_ref[...].astype(F32) * u_ref[...].astype(F32)
        halo = cgh_ref[...].astype(F32) * uh_ref[...].astype(F32)
        vbuf[0:HALO, :] = jnp.where(i > 0, halo, 0.0)
        vbuf[HALO:HALO + tm, :] = vv
        v1 = vbuf[HALO - 1:HALO - 1 + tm, :]
        v2 = vbuf[HALO - 2:HALO - 2 + tm, :]
        w = w_ref[...]
        y = w[2:3, :] * vv + w[1:2, :] * v1 + w[0:1, :] * v2
        yc_ref[...] = (bg_ref[...].astype(F32) * y).astype(BF16)

    def prev(i):
        return jnp.maximum(i * nb - 1, 0)

    return pl.pallas_call(
        body, name=name, grid=(s // tm,),
        in_specs=[pl.BlockSpec((tm, cw), lambda i: (i, 0)), pl.BlockSpec((tm, cw), lambda i: (i, 1)),
                  pl.BlockSpec((tm, cw), lambda i: (i, 2)),
                  pl.BlockSpec((HALO, cw), lambda i: (prev(i), 1)), pl.BlockSpec((HALO, cw), lambda i: (prev(i), 2)),
                  pl.BlockSpec((3, cw), lambda i: (0, 0))],
        out_specs=pl.BlockSpec((tm, cw), lambda i: (i, 0)),
        out_shape=_sds((s, cw), BF16),
        scratch_shapes=[pltpu.VMEM((HALO + tm, cw), F32)],
        compiler_params=_params(("arbitrary",)),
    )(proj, proj, proj, proj, proj, conv_w)


def _tri(qb):
    r = lax.broadcasted_iota(jnp.int32, (qb, qb), 0)
    c = lax.broadcasted_iota(jnp.int32, (qb, qb), 1)
    return (r >= c).astype(BF16)


def _head_mask(h):
    lane = lax.broadcasted_iota(jnp.int32, (1, LANES), 1)
    return (lane >= HEAD_DIM * h) & (lane < HEAD_DIM * (h + 1))


def _stack_heads(a, masks):
    return jnp.concatenate([jnp.where(m, a, 0).astype(BF16) for m in masks], axis=0)


def _heads_to_lanes(a, qb):
    return jnp.concatenate([a[:qb], a[qb:]], axis=1)


def _stacked_causal(qb, width, first_key, first_query):
    row = lax.broadcasted_iota(jnp.int32, (2 * qb, width), 0)
    col = lax.broadcasted_iota(jnp.int32, (2 * qb, width), 1)
    return first_key + col < first_query + jnp.where(row >= qb, row - qb, row)


def _running_sum(a, tri_m):
    rows, qb = a.shape[0], tri_m.shape[0]
    n = a.shape[1] // qb
    hi, lo = _split_bf16(a)
    stacked = jnp.concatenate([p[:, s * qb:(s + 1) * qb] for s in range(n) for p in (hi, lo)], axis=0)
    both = jnp.dot(stacked, tri_m, preferred_element_type=F32)
    parts = [both[(2 * s) * rows:(2 * s + 1) * rows] + both[(2 * s + 1) * rows:(2 * s + 2) * rows] for s in range(n)]
    later = None
    for s in reversed(range(n)):
        if later is not None:
            parts[s] = parts[s] + later
        later = parts[s][:, 0:1]
    return (parts[0] if n == 1 else jnp.concatenate(parts, axis=1)), later


def _attn_cols(d):
    cw = d // 2
    hp = (d // 2) // LANES
    q0 = (3 * cw) // LANES
    return q0, q0 + hp, q0 + 2 * hp, hp


class _Side:
    def __init__(self, operands, out_shapes, aliases, scratch, start, mid, finish):
        self.operands, self.out_shapes, self.aliases, self.scratch = list(operands), list(out_shapes), aliases, list(scratch)
        self.start, self.mid, self.finish = start, mid, finish


def _side_call(side, *, name):
    n_in, n_out = len(side.operands), len(side.out_shapes)

    def body(*refs):
        parts = refs[:n_in], refs[n_in:n_in + n_out], refs[n_in + n_out:]
        side.start(*parts)
        if side.mid is not None:
            side.mid(*parts)
        side.finish(*parts)

    hbm = pl.BlockSpec(memory_space=pltpu.HBM)
    return pl.pallas_call(body, name=name, in_specs=[hbm] * n_in, out_specs=[hbm] * n_out, out_shape=side.out_shapes,
                          input_output_aliases=dict(side.aliases), scratch_shapes=side.scratch)(*side.operands)


def _host_side(side, n_in, n_out, n_scratch, refs, first, late, last):
    if side is None:
        return refs, lambda: None
    s_in, s_out = len(side.operands), len(side.out_shapes)
    ins = refs[:n_in]
    side_in = refs[n_in:n_in + s_in]
    outs = refs[n_in + s_in:n_in + s_in + n_out]
    side_out = refs[n_in + s_in + n_out:n_in + s_in + n_out + s_out]
    rest = refs[n_in + s_in + n_out + s_out:]
    scratch, sems = rest[:n_scratch], rest[n_scratch:]
    parts = (side_in, side_out, sems)
    pl.when(first)(lambda: side.start(*parts))

    def run_late_phases():
        if side.mid is not None:
            pl.when(late)(lambda: side.mid(*parts))
        pl.when(last)(lambda: side.finish(*parts))

    return (*ins, *outs, *scratch), run_late_phases


def _side_specs(side, n_in, n_out):
    if side is None:
        return [], [], [], {}, []
    hbm = pl.BlockSpec(memory_space=pltpu.HBM)
    s_in = len(side.operands)
    aliases = {n_in + a: n_out + b for a, b in side.aliases.items()}
    return [hbm] * s_in, [hbm] * len(side.out_shapes), side.out_shapes, aliases, side.scratch


def _attn_fwd(proj, tri, *, d, name, side=None):
    s = proj.shape[0]
    qb = tri.shape[0]
    chains = _fit(ATTN_CHAINS[0], s // qb)
    ng = s // (qb * chains)
    q0, k0, v0, hp = _attn_cols(d)

    def body(*refs):
        p, g = pl.program_id(0), pl.program_id(1)
        (q_ref, k_ref, v_ref, tri_ref, o_ref, a_ref, b_ref, run_ref), late_phases = _host_side(
            side, 4, 4, 0, refs, (p == 0) & (g == 0), (p == hp - 1) & (g == 0), (p == hp - 1) & (g == ng - 1))
        tri_m = tri_ref[...]
        masks = [_head_mask(h) for h in range(2)]

        def first_steps(u):
            i = g * chains + u
            qs = _stack_heads(q_ref[u * qb:(u + 1) * qb, :] * ATTN_SCALE, masks)

            def strip(j, state, causal=None, keep=None, live=None):
                run, acc = state
                rows = pl.ds(pl.multiple_of(j * qb, qb), qb)
                z = lax.dot_general(qs, k_ref[rows, :], _NT, preferred_element_type=F32)
                lg = _log_one_minus_sigmoid(z)
                beta = 1.0 - jnp.exp(lg) if keep is not None else None
                if causal is not None:
                    lg = jnp.where(causal, lg, 0.0)
                cs, total = _running_sum(lg, tri_m)
                a = jnp.exp(z + cs + run)
                if causal is not None:
                    a = jnp.where(causal, a, 0.0)
                    beta = jnp.where(causal, beta, 0.0)
                if live is not None:
                    a = jnp.where(live, a, 0.0)
                    beta = jnp.where(live, beta, 0.0)
                    total = jnp.where(live, total, 0.0)
                ab = a.astype(BF16)
                if keep is not None:
                    a_ref[0, u, keep] = ab
                    b_ref[0, u, keep] = beta.astype(BF16)
                acc = acc + jnp.dot(_heads_to_lanes(ab, qb), _stack_heads(v_ref[rows, :], masks),
                                    preferred_element_type=F32)
                return run + total, acc

            state = strip(i, (jnp.zeros((2 * qb, 1), F32), jnp.zeros((qb, LANES), F32)),
                          causal=_stacked_causal(qb, qb, 0, 0), keep=0)
            state = strip(jnp.maximum(i - 1, 0), state, keep=1, live=i >= 1)
            run_ref[0, u] = jnp.broadcast_to(state[0], (2 * qb, LANES))
            return i, strip, state

        started = [first_steps(u) for u in range(chains)]
        for u, (i, strip, state) in enumerate(started):
            state = lax.while_loop(
                lambda st: (st[0] >= 0) & (jnp.max(st[1]) > UNDERFLOW_LOG),
                lambda st, strip=strip: (st[0] - 1, *strip(st[0], st[1:])),
                (i - 2, *state))
            o_ref[u * qb:(u + 1) * qb, :] = state[2]
        late_phases()

    s_in, s_out, s_shapes, aliases, s_scratch = _side_specs(side, 4, 4)
    tq = qb * chains
    nq = s // qb
    res = pl.pallas_call(
        body, name=name, grid=(hp, ng),
        in_specs=[pl.BlockSpec((tq, LANES), lambda p, i: (i, q0 + p)),
                  pl.BlockSpec((s, LANES), lambda p, i: (0, k0 + p)),
                  pl.BlockSpec((s, LANES), lambda p, i: (0, v0 + p)),
                  pl.BlockSpec((qb, qb), lambda p, i: (0, 0))] + s_in,
        out_specs=[pl.BlockSpec((tq, LANES), lambda p, i: (i, p)),
                   pl.BlockSpec((1, chains, 2, 2 * qb, qb), lambda p, i: (p, i, 0, 0, 0)),
                   pl.BlockSpec((1, chains, 2, 2 * qb, qb), lambda p, i: (p, i, 0, 0, 0)),
                   pl.BlockSpec((1, chains, 2 * qb, LANES), lambda p, i: (p, i, 0, 0))] + s_out,
        out_shape=[_sds((s, hp * LANES), F32), _sds((hp, nq, 2, 2 * qb, qb), BF16), _sds((hp, nq, 2, 2 * qb, qb), BF16),
                   _sds((hp, nq, 2 * qb, LANES), F32)] + s_shapes,
        input_output_aliases=aliases, scratch_shapes=s_scratch,
        compiler_params=_params(("arbitrary", "arbitrary")),
    )(proj, proj, proj, tri, *([] if side is None else side.operands))
    return (res[0], tuple(res[1:4])), res[4:]


def _mix_out(yc, o, proj, x, wpc, wpa, wout, g, gt, *, name, tm=256):
    s, d = x.shape
    cw = yc.shape[1]
    tm = min(tm, s)
    ga_blk = (3 * cw + 3 * (d // 2)) // d

    def body(yc_ref, o_ref, ga_ref, gb_ref, x_ref, wpc_ref, wpa_ref, wout_ref, g_ref, gt_ref,
             ycv_ref, yat_ref, mg_ref, mix_ref, x1_ref):
        y_conv = jnp.dot(yc_ref[...], wpc_ref[...], preferred_element_type=F32)
        y_attn = jnp.dot(o_ref[...].astype(BF16), wpa_ref[...], preferred_element_type=F32)
        merged = (_sigmoid(ga_ref[...].astype(F32)) * y_conv + _sigmoid(gb_ref[...].astype(F32)) * y_attn)
        mg = merged.astype(BF16)
        mix = jnp.dot(mg, wout_ref[...], preferred_element_type=F32)
        r = lax.rsqrt(jnp.mean(mix * mix, axis=-1, keepdims=True) + EPS)
        ycv_ref[...] = y_conv.astype(BF16)
        yat_ref[...] = y_attn.astype(BF16)
        mg_ref[...] = mg
        mix_ref[...] = mix
        x1_ref[...] = x_ref[...] + gt_ref[...] * (mix * r * g_ref[...])

    def rows(w):
        return pl.BlockSpec((tm, w), lambda i: (i, 0))

    def full(a):
        return pl.BlockSpec(a.shape, lambda i: (0, 0))

    return pl.pallas_call(
        body, name=name, grid=(s // tm,),
        in_specs=[rows(cw), rows(d // 2), pl.BlockSpec((tm, d), lambda i: (i, ga_blk)),
                  pl.BlockSpec((tm, d), lambda i: (i, ga_blk + 1)), rows(d),
                  full(wpc), full(wpa), full(wout), _vec_spec(d, 1), _vec_spec(d, 1)],
        out_specs=[rows(d), rows(d), rows(d), rows(d), rows(d)],
        out_shape=[_sds((s, d), BF16), _sds((s, d), BF16), _sds((s, d), BF16), _sds((s, d), F32), _sds((s, d), F32)],
        compiler_params=_params(("parallel",)),
    )(yc, o, proj, proj, x, wpc, wpa, wout, g, gt)


def _relu2(a):
    r = jnp.maximum(a.astype(F32), 0.0)
    return (r * r).astype(BF16)


def _mlp_out(a, x, w2, g, gt, *, name, tm=512):
    s, d = x.shape
    dff = a.shape[1]
    tm = min(tm, s)

    def body(a_ref, x_ref, w_ref, g_ref, gt_ref, ff_ref, x2_ref):
        ff = jnp.dot(_relu2(a_ref[...]), w_ref[...], preferred_element_type=F32)
        r = lax.rsqrt(jnp.mean(ff * ff, axis=-1, keepdims=True) + EPS)
        ff_ref[...] = ff
        x2_ref[...] = x_ref[...] + gt_ref[...] * (ff * r * g_ref[...])

    return pl.pallas_call(
        body, name=name, grid=(s // tm,),
        in_specs=[pl.BlockSpec((tm, dff), lambda i: (i, 0)), pl.BlockSpec((tm, d), lambda i: (i, 0)),
                  pl.BlockSpec((dff, d), lambda i: (0, 0)), _vec_spec(d, 1), _vec_spec(d, 1)],
        out_specs=[pl.BlockSpec((tm, d), lambda i: (i, 0)), pl.BlockSpec((tm, d), lambda i: (i, 0))],
        out_shape=[_sds((s, d), F32), _sds((s, d), F32)],
        compiler_params=_params(("parallel",)),
    )(a, x, w2, g, gt)


def _loss_grad(y, target, *, name, tm=512):
    s, d = y.shape
    tm = min(tm, s)

    def body(y_ref, t_ref, dy_ref, loss_ref):
        @pl.when(pl.program_id(0) == 0)
        def _():
            loss_ref[...] = jnp.zeros_like(loss_ref)
        e = y_ref[...] - t_ref[...]
        dy_ref[...] = e * (1.0 / d)
        loss_ref[...] += 0.5 * jnp.sum(jnp.mean(e * e, axis=-1, keepdims=True), axis=0, keepdims=True)

    return pl.pallas_call(
        body, name=name, grid=(s // tm,),
        in_specs=[pl.BlockSpec((tm, d), lambda i: (i, 0)), pl.BlockSpec((tm, d), lambda i: (i, 0))],
        out_specs=[pl.BlockSpec((tm, d), lambda i: (i, 0)), pl.BlockSpec((1, 1), lambda i: (0, 0))],
        out_shape=[_sds((s, d), F32), _sds((1, 1), F32)],
        compiler_params=_params(("arbitrary",)),
    )(y, target)


def _mlp_out_bwd(dx, ff, a, w2, g, gt, *, name, tm=256):
    s, d = dx.shape
    dff = a.shape[1]
    tm = min(tm, s)

    def body(dx_ref, ff_ref, a_ref, w_ref, g_ref, gt_ref, dff_ref, da_ref, dgt_ref, dg_ref):
        @pl.when(pl.program_id(0) == 0)
        def _():
            dgt_ref[...] = jnp.zeros_like(dgt_ref)
            dg_ref[...] = jnp.zeros_like(dg_ref)
        dxv = dx_ref[...]
        dn = dxv * gt_ref[...]
        dffv, xh = _rms_bwd(dn, ff_ref[...], g_ref[...])
        dgt_ref[...] += _colsum(dxv * (xh * g_ref[...]))
        dg_ref[...] += _colsum(dn * xh)
        dffb = dffv.astype(BF16)
        dff_ref[...] = dffb
        drr = lax.dot_general(dffb, w_ref[...], _NT, preferred_element_type=F32)
        da_ref[...] = (drr * (2.0 * jnp.maximum(a_ref[...].astype(F32), 0.0))).astype(BF16)

    return pl.pallas_call(
        body, name=name, grid=(s // tm,),
        in_specs=[pl.BlockSpec((tm, d), lambda i: (i, 0)), pl.BlockSpec((tm, d), lambda i: (i, 0)),
                  pl.BlockSpec((tm, dff), lambda i: (i, 0)), pl.BlockSpec((dff, d), lambda i: (0, 0)),
                  _vec_spec(d, 1), _vec_spec(d, 1)],
        out_specs=[pl.BlockSpec((tm, d), lambda i: (i, 0)), pl.BlockSpec((tm, dff), lambda i: (i, 0)),
                   _vec_spec(d, 1), _vec_spec(d, 1)],
        out_shape=[_sds((s, d), BF16), _sds((s, dff), BF16), _sds((1, d), F32), _sds((1, d), F32)],
        compiler_params=_params(("arbitrary",)),
    )(dx, ff, a, w2, g, gt)


def _matmul_nt_norm_bwd(dys, w, x, dres, g, sc, *, name, tm=256, side=None):
    s = dys[0].shape[0]
    widths = [dy.shape[1] for dy in dys]
    d, n = w.shape
    assert sum(widths) == n, (widths, n)
    tm = _fit(tm, s)
    nt = s // tm
    np_ = len(dys)

    def body(*refs):
        i = pl.program_id(0)
        own, late_phases = _host_side(side, np_ + 5, 4, 0, refs, i == 0, i == (3 * nt) // 4, i == nt - 1)
        dy_refs = own[:np_]
        w_ref, x_ref, dres_ref, g_ref, sc_ref, dx_ref, dsh_ref, dsc_ref, dg_ref = own[np_:]

        @pl.when(i == 0)
        def _():
            dsh_ref[...] = jnp.zeros_like(dsh_ref)
            dsc_ref[...] = jnp.zeros_like(dsc_ref)
            dg_ref[...] = jnp.zeros_like(dg_ref)

        dh = None
        for p, dy_ref in enumerate(dy_refs):
            cols = slice(sum(widths[:p]), sum(widths[:p + 1]))
            part = lax.dot_general(dy_ref[...], w_ref[:, cols], _NT, preferred_element_type=F32)
            dh = part if dh is None else dh + part
        dn = dh * (1.0 + sc_ref[...])
        dxin, xh = _rms_bwd(dn, x_ref[...], g_ref[...])
        dsh_ref[...] += _colsum(dh)
        dsc_ref[...] += _colsum(dh * (xh * g_ref[...]))
        dg_ref[...] += _colsum(dn * xh)
        dx_ref[...] = dres_ref[...] + dxin
        late_phases()

    s_in, s_out, s_shapes, aliases, s_scratch = _side_specs(side, np_ + 5, 4)
    res = pl.pallas_call(
        body, name=name, grid=(nt,),
        in_specs=[pl.BlockSpec((tm, wd), lambda i: (i, 0)) for wd in widths]
        + [pl.BlockSpec((d, n), lambda i: (0, 0)),
           pl.BlockSpec((tm, d), lambda i: (i, 0)), pl.BlockSpec((tm, d), lambda i: (i, 0)),
           _vec_spec(d, 1), _vec_spec(d, 1)] + s_in,
        out_specs=[pl.BlockSpec((tm, d), lambda i: (i, 0)), _vec_spec(d, 1), _vec_spec(d, 1), _vec_spec(d, 1)] + s_out,
        out_shape=[_sds((s, d), F32), _sds((1, d), F32), _sds((1, d), F32), _sds((1, d), F32)] + s_shapes,
        input_output_aliases=aliases, scratch_shapes=s_scratch,
        compiler_params=_params(("arbitrary",)),
    )(*dys, w, x, dres, g, sc, *([] if side is None else side.operands))
    return res[:4], res[4:]


def _matmul_tn(a, bs, *, name, tk=1024, tn=1024, ts=512, relu2=False, into=None, col0=0, n_total=None):
    s, k = a.shape
    widths = [b.shape[1] for b in bs]
    n = sum(widths)
    tk, ts = _fit(tk, k), _fit(ts, s)
    for w in widths:
        tn = _fit(tn, w)
    while col0 % tn:
        tn //= 2
    nt = s // ts
    assert tn % LANES == 0 and all(sum(widths[:p]) % tn == 0 for p in range(len(bs))), (widths, tn)
    first = [sum(widths[:p]) // tn for p in range(len(bs))]
    tiles = [w // tn for w in widths]
    tile0 = col0 // tn

    def body(a_ref, *rest):
        b_refs, o_ref, acc = rest[:len(bs)], rest[-2], rest[-1]
        j, t = pl.program_id(1), pl.program_id(2)

        @pl.when(t == 0)
        def _():
            acc[...] = jnp.zeros_like(acc)
        av = a_ref[...]
        av = _relu2(av) if relu2 else av.astype(BF16)
        for p, b_ref in enumerate(b_refs):
            def add(b_ref=b_ref):
                acc[...] += lax.dot_general(av, b_ref[...], _TN, preferred_element_type=F32)
            if len(bs) == 1:
                add()
            else:
                pl.when((j >= first[p]) & (j < first[p] + tiles[p]))(add)

        @pl.when(t == nt - 1)
        def _():
            o_ref[...] = acc[...].astype(BF16)

    def piece_spec(p):
        def index(i, j, t):
            mine = (j >= first[p]) & (j < first[p] + tiles[p])
            return jnp.where(mine, t, 0), jnp.where(mine, j - first[p], 0)
        return pl.BlockSpec((ts, tn), index)

    operands, extra_specs, aliases = [a, *bs], [], {}
    if into is not None:
        operands.append(into)
        extra_specs = [pl.BlockSpec(memory_space=pltpu.HBM)]
        aliases = {len(operands) - 1: 0}
    return pl.pallas_call(
        body, name=name, grid=(k // tk, n // tn, nt),
        in_specs=[pl.BlockSpec((ts, tk), lambda i, j, t: (t, i))] + [piece_spec(p) for p in range(len(bs))] + extra_specs,
        out_specs=pl.BlockSpec((tk, tn), lambda i, j, t: (i, tile0 + j)),
        out_shape=_sds((k, n_total or n), BF16),
        input_output_aliases=aliases,
        scratch_shapes=[pltpu.VMEM((tk, tn), F32)],
        compiler_params=_params(("parallel", "parallel", "arbitrary")),
    )(*operands)


def _mix_out_bwd(dx, mix, proj, ycv, yat, wout, wpc, wpa, g, gt, *, name, tm=256):
    s, d = dx.shape
    cw = wpc.shape[0]
    aw = wpa.shape[0]
    tm = min(tm, s)
    ga_blk = (3 * cw + 3 * aw) // d

    def body(dx_ref, mix_ref, ga_ref, gb_ref, ycv_ref, yat_ref, wout_ref, wpc_ref, wpa_ref, g_ref, gt_ref,
             dmix_ref, dycv_ref, dyat_ref, dyc_ref, do_ref, dgate_ref, dgt_ref, dg_ref):
        @pl.when(pl.program_id(0) == 0)
        def _():
            dgt_ref[...] = jnp.zeros_like(dgt_ref)
            dg_ref[...] = jnp.zeros_like(dg_ref)
        dxv = dx_ref[...]
        dn = dxv * gt_ref[...]
        dmix, xh = _rms_bwd(dn, mix_ref[...], g_ref[...])
        dgt_ref[...] += _colsum(dxv * (xh * g_ref[...]))
        dg_ref[...] += _colsum(dn * xh)
        dmixb = dmix.astype(BF16)
        dmix_ref[...] = dmixb
        dmerged = lax.dot_general(dmixb, wout_ref[...], _NT, preferred_element_type=F32)
        sga = _sigmoid(ga_ref[...].astype(F32))
        sgb = _sigmoid(gb_ref[...].astype(F32))
        dycv = (dmerged * sga).astype(BF16)
        dyat = (dmerged * sgb).astype(BF16)
        dycv_ref[...] = dycv
        dyat_ref[...] = dyat
        dgate_ref[:, 0:d] = (dmerged * ycv_ref[...].astype(F32) * (sga * (1.0 - sga))).astype(BF16)
        dgate_ref[:, d:2 * d] = (dmerged * yat_ref[...].astype(F32) * (sgb * (1.0 - sgb))).astype(BF16)
        dyc_ref[...] = lax.dot_general(dycv, wpc_ref[...], _NT, preferred_element_type=F32).astype(BF16)
        do_ref[...] = lax.dot_general(dyat, wpa_ref[...], _NT, preferred_element_type=F32).astype(BF16)

    def rows(w):
        return pl.BlockSpec((tm, w), lambda i: (i, 0))

    def full(a):
        return pl.BlockSpec(a.shape, lambda i: (0, 0))

    return pl.pallas_call(
        body, name=name, grid=(s // tm,),
        in_specs=[rows(d), rows(d), pl.BlockSpec((tm, d), lambda i: (i, ga_blk)),
                  pl.BlockSpec((tm, d), lambda i: (i, ga_blk + 1)), rows(d), rows(d),
                  full(wout), full(wpc), full(wpa), _vec_spec(d, 1), _vec_spec(d, 1)],
        out_specs=[rows(d), rows(d), rows(d), rows(cw), rows(aw), rows(2 * d), _vec_spec(d, 1), _vec_spec(d, 1)],
        out_shape=[_sds((s, d), BF16), _sds((s, d), BF16), _sds((s, d), BF16), _sds((s, cw), BF16),
                   _sds((s, aw), BF16), _sds((s, 2 * d), BF16), _sds((1, d), F32), _sds((1, d), F32)],
        compiler_params=_params(("arbitrary",)),
    )(dx, mix, proj, proj, ycv, yat, wout, wpc, wpa, g, gt)


def _conv_bwd(dyc, proj, conv_w, *, name, tm=512):
    s = proj.shape[0]
    cw = conv_w.shape[1]
    tm = min(tm, s)
    nb = tm // HALO
    nt = s // tm
    last_blk = s // HALO - 1

    def body(dyc_ref, bg_ref, cg_ref, u_ref, cgh_ref, uh_ref, dych_ref, bgh_ref, w_ref,
             dconv_ref, dw_ref, vbuf, gbuf):
        i = pl.program_id(0)

        @pl.when(i == 0)
        def _():
            dw_ref[...] = jnp.zeros_like(dw_ref)

        cg = cg_ref[...].astype(F32)
        u = u_ref[...].astype(F32)
        vv = cg * u
        halo = cgh_ref[...].astype(F32) * uh_ref[...].astype(F32)
        vbuf[0:HALO, :] = jnp.where(i > 0, halo, 0.0)
        vbuf[HALO:HALO + tm, :] = vv
        v1 = vbuf[HALO - 1:HALO - 1 + tm, :]
        v2 = vbuf[HALO - 2:HALO - 2 + tm, :]
        w = w_ref[...]
        y = w[2:3, :] * vv + w[1:2, :] * v1 + w[0:1, :] * v2
        dyc = dyc_ref[...].astype(F32)
        dconv_ref[:, 0:cw] = (dyc * y).astype(BF16)
        gy = dyc * bg_ref[...].astype(F32)
        nxt = dych_ref[...].astype(F32) * bgh_ref[...].astype(F32)
        gbuf[0:tm, :] = gy
        gbuf[tm:tm + HALO, :] = jnp.where(i < nt - 1, nxt, 0.0)
        g1 = gbuf[1:1 + tm, :]
        g2 = gbuf[2:2 + tm, :]
        dvv = w[2:3, :] * gy + w[1:2, :] * g1 + w[0:1, :] * g2
        dconv_ref[:, cw:2 * cw] = (dvv * u).astype(BF16)
        dconv_ref[:, 2 * cw:3 * cw] = (dvv * cg).astype(BF16)
        dw_ref[0:1, :] += _colsum(gy * v2)
        dw_ref[1:2, :] += _colsum(gy * v1)
        dw_ref[2:3, :] += _colsum(gy * vv)

    def prev(i):
        return jnp.maximum(i * nb - 1, 0)

    def nxt_blk(i):
        return jnp.minimum((i + 1) * nb, last_blk)

    def col(c):
        return pl.BlockSpec((tm, cw), lambda i: (i, c))

    return pl.pallas_call(
        body, name=name, grid=(nt,),
        in_specs=[col(0), col(0), col(1), col(2),
                  pl.BlockSpec((HALO, cw), lambda i: (prev(i), 1)), pl.BlockSpec((HALO, cw), lambda i: (prev(i), 2)),
                  pl.BlockSpec((HALO, cw), lambda i: (nxt_blk(i), 0)), pl.BlockSpec((HALO, cw), lambda i: (nxt_blk(i), 0)),
                  pl.BlockSpec((3, cw), lambda i: (0, 0))],
        out_specs=[pl.BlockSpec((tm, 3 * cw), lambda i: (i, 0)), pl.BlockSpec((3, cw), lambda i: (0, 0))],
        out_shape=[_sds((s, 3 * cw), BF16), _sds((3, cw), F32)],
        scratch_shapes=[pltpu.VMEM((HALO + tm, cw), F32), pltpu.VMEM((tm + HALO, cw), F32)],
        compiler_params=_params(("arbitrary",)),
    )(dyc, proj, proj, proj, proj, proj, dyc, proj, conv_w)


def _attn_bwd(proj, o, kept, do, tri, *, d, name, side=None):
    s = proj.shape[0]
    qb = tri.shape[0]
    chains = _fit(ATTN_CHAINS[1], s // qb)
    ng = s // (qb * chains)
    q0, k0, v0, hp = _attn_cols(d)

    def body(*refs):
        p, g = pl.program_id(0), pl.program_id(1)
        own, late_phases = _host_side(
            side, 9, 3, 2, refs, (p == 0) & (g == 0), (p == hp - 1) & (g == 0), (p == hp - 1) & (g == ng - 1))
        (q_ref, k_ref, v_ref, o_ref, do_ref, tri_ref, a_ref, b_ref, run_ref,
         dq_ref, dk_ref, dv_ref, dk_acc, dv_acc) = own

        @pl.when(g == 0)
        def _():
            dk_acc[...] = jnp.zeros_like(dk_acc)
            dv_acc[...] = jnp.zeros_like(dv_acc)

        tri_m = tri_ref[...]
        masks = [_head_mask(h) for h in range(2)]

        def first_steps(u):
            i = g * chains + u
            mine = slice(u * qb, (u + 1) * qb)
            dov = do_ref[mine, :]
            qs = _stack_heads(q_ref[mine, :] * ATTN_SCALE, masks)
            dos = _stack_heads(dov, masks)
            dprod = dov.astype(F32) * o_ref[mine, :]
            dtot = jnp.concatenate([jnp.sum(jnp.where(m, dprod, 0.0), axis=-1, keepdims=True) for m in masks], axis=0)

            def through(j, ab, beta, left, grun, dq_acc):
                rows = pl.ds(pl.multiple_of(j * qb, qb), qb)
                kb = k_ref[rows, :]
                da = lax.dot_general(dos, v_ref[rows, :], _NT, preferred_element_type=F32)
                gg = ab.astype(F32) * da
                gcs, gtotal = _running_sum(gg, tri_m)
                dzb = (gg - beta * (gg + (left - gcs))).astype(BF16)
                dq_acc = dq_acc + jnp.dot(_heads_to_lanes(dzb, qb), _stack_heads(kb, masks),
                                          preferred_element_type=F32)
                dk_add = lax.dot_general(dzb, qs, _TN, preferred_element_type=F32)
                dv_add = lax.dot_general(ab, dos, _TN, preferred_element_type=F32)
                return (grun + gtotal, dq_acc), (rows, dk_add, dv_add)

            def kept(slot, j, state):
                grun, dq_acc = state
                return through(j, a_ref[0, u, slot], b_ref[0, u, slot].astype(F32), dtot - grun, grun, dq_acc)

            def strip(j, state):
                run, grun, dq_acc = state
                z = lax.dot_general(qs, k_ref[pl.ds(pl.multiple_of(j * qb, qb), qb), :], _NT, preferred_element_type=F32)
                lg = _log_one_minus_sigmoid(z)
                cs, total = _running_sum(lg, tri_m)
                ab = jnp.exp(z + cs + run).astype(BF16)
                left = jnp.where(run > UNDERFLOW_LOG, dtot - grun, 0.0)
                (grun, dq_acc), adds = through(j, ab, 1.0 - jnp.exp(lg), left, grun, dq_acc)
                return (run + total, grun, dq_acc), adds

            state, adds0 = kept(0, i, (jnp.zeros((2 * qb, 1), F32), jnp.zeros((qb, LANES), F32)))
            state, adds1 = kept(1, jnp.maximum(i - 1, 0), state)
            return i, strip, (run_ref[0, u][:, 0:1], *state), (adds0, adds1)

        started = [first_steps(u) for u in range(chains)]
        for u, (i, strip, state, adds) in enumerate(started):
            for rows, dk_add, dv_add in adds:
                dk_acc[rows, :] += dk_add
                dv_acc[rows, :] += dv_add

            def more(st, strip=strip):
                state, (rows, dk_add, dv_add) = strip(st[0], st[1:])
                dk_acc[rows, :] += dk_add
                dv_acc[rows, :] += dv_add
                return (st[0] - 1, *state)

            state = lax.while_loop(lambda st: (st[0] >= 0) & (jnp.max(st[1]) > UNDERFLOW_LOG), more, (i - 2, *state))
            dq_ref[u * qb:(u + 1) * qb, :] = (state[3] * ATTN_SCALE).astype(BF16)

        @pl.when(g == ng - 1)
        def _():
            dk_ref[...] = dk_acc[...].astype(BF16)
            dv_ref[...] = dv_acc[...].astype(BF16)

        late_phases()

    aw = hp * LANES
    tq = qb * chains
    s_in, s_out, s_shapes, aliases, s_scratch = _side_specs(side, 9, 3)
    res = pl.pallas_call(
        body, name=name, grid=(hp, ng),
        in_specs=[pl.BlockSpec((tq, LANES), lambda p, i: (i, q0 + p)),
                  pl.BlockSpec((s, LANES), lambda p, i: (0, k0 + p)),
                  pl.BlockSpec((s, LANES), lambda p, i: (0, v0 + p)),
                  pl.BlockSpec((tq, LANES), lambda p, i: (i, p)),
                  pl.BlockSpec((tq, LANES), lambda p, i: (i, p)),
                  pl.BlockSpec((qb, qb), lambda p, i: (0, 0)),
                  pl.BlockSpec((1, chains, 2, 2 * qb, qb), lambda p, i: (p, i, 0, 0, 0)),
                  pl.BlockSpec((1, chains, 2, 2 * qb, qb), lambda p, i: (p, i, 0, 0, 0)),
                  pl.BlockSpec((1, chains, 2 * qb, LANES), lambda p, i: (p, i, 0, 0))] + s_in,
        out_specs=[pl.BlockSpec((tq, LANES), lambda p, i: (i, p)),
                   pl.BlockSpec((s, LANES), lambda p, i: (0, p)),
                   pl.BlockSpec((s, LANES), lambda p, i: (0, p))] + s_out,
        out_shape=[_sds((s, aw), BF16), _sds((s, aw), BF16), _sds((s, aw), BF16)] + s_shapes,
        input_output_aliases=aliases,
        scratch_shapes=[pltpu.VMEM((s, LANES), F32), pltpu.VMEM((s, LANES), F32)] + s_scratch,
        compiler_params=_params(("arbitrary", "arbitrary")),
    )(proj, proj, proj, o, do, tri, *kept, *([] if side is None else side.operands))
    return res[:3], res[3:]


def _hosted(hooks, kind, l, fn, *args, **kw):
    res, side_out = fn(*args, side=hooks.side(kind, l), **kw)
    hooks.done(kind, l, side_out)
    return res


def _layer_fwd(x, mod, gains, conv_w, tri, *, l, hooks):
    sh1, sc1, gt1, sh2, sc2, gt2 = mod
    g_pre_mix, g_post_mix, g_pre_mlp, g_post_mlp = gains
    d = x.shape[1]
    w = functools.partial(hooks.weight, l)
    h, proj = _hosted(hooks, "in_proj", l, _norm_mod_matmul, x, g_pre_mix, sc1, sh1, w("w_in"), name=f"in_proj_{l}")
    yc = _conv_fwd(proj, conv_w, name=f"conv_fwd_{l}")
    o, kept = _hosted(hooks, "attn_fwd", l, _attn_fwd, proj, tri, d=d, name=f"attn_fwd_{l}")
    ycv, yat, merged, mix, x1 = _mix_out(yc, o, proj, x, w("w_proj_conv"), w("w_proj_attn"), w("w_out"),
                                         g_post_mix, gt1, name=f"mix_out_{l}")
    (h2, a), _ = _norm_mod_matmul(x1, g_pre_mlp, sc2, sh2, w("w_mlp_in"), name=f"mlp_in_{l}")
    ff, x2 = _mlp_out(a, x1, w("w_mlp_out"), g_post_mlp, gt2, name=f"mlp_out_{l}")
    saved = dict(x=x, h=h, proj=proj, yc=yc, o=o, kept=kept, ycv=ycv, yat=yat, merged=merged, mix=mix, x1=x1, h2=h2, a=a, ff=ff,
                 conv_w=conv_w, **{k: w(k) for k in BIG})
    return x2, saved


def _layer_bwd(dx2, sv, mod, gains, tri, *, l, hooks):
    sh1, sc1, gt1, sh2, sc2, gt2 = mod
    g_pre_mix, g_post_mix, g_pre_mlp, g_post_mlp = gains
    d = dx2.shape[1]
    dff, da, dgt2, dg_post_mlp = _mlp_out_bwd(dx2, sv["ff"], sv["a"], sv["w_mlp_out"], g_post_mlp, gt2,
                                              name=f"mlp_out_bwd_{l}")
    hooks.grad(l, "w_mlp_out", _matmul_tn(sv["a"], [dff], relu2=True, name=f"gw_mlp_out_{l}"))
    (dx1, dsh2, dsc2, dg_pre_mlp), _ = _matmul_nt_norm_bwd([da], sv["w_mlp_in"], sv["x1"], dx2, g_pre_mlp, sc2,
                                                           name=f"mlp_in_bwd_{l}")
    hooks.grad(l, "w_mlp_in", _matmul_tn(sv["h2"], [da], name=f"gw_mlp_in_{l}"))
    dmix, dycv, dyat, dyc, do, dgate, dgt1, dg_post_mix = _mix_out_bwd(
        dx1, sv["mix"], sv["proj"], sv["ycv"], sv["yat"], sv["w_out"], sv["w_proj_conv"], sv["w_proj_attn"],
        g_post_mix, gt1, name=f"mix_out_bwd_{l}")
    hooks.grad(l, "w_out", _matmul_tn(sv["merged"], [dmix], name=f"gw_out_{l}"))
    hooks.grad(l, "w_proj_conv", _matmul_tn(sv["yc"], [dycv], name=f"gw_proj_conv_{l}"))
    hooks.grad(l, "w_proj_attn", _matmul_tn(sv["o"], [dyat], name=f"gw_proj_attn_{l}"))
    dconv, g_conv_w = _conv_bwd(dyc, sv["proj"], sv["conv_w"], name=f"conv_bwd_{l}")
    dq, dk, dv = _hosted(hooks, "attn_bwd", l, _attn_bwd, sv["proj"], sv["o"], sv["kept"], do, tri, d=d,
                         name=f"attn_bwd_{l}")
    dproj = [dconv, dq, dk, dv, dgate]
    n_in = sv["w_in"].shape[1]
    gw_in = _matmul_tn(sv["h"], [dconv], tn=768, n_total=n_in, name=f"gw_in_conv_{l}")
    gw_in = _matmul_tn(sv["h"], [dq, dk, dv], into=gw_in, col0=dconv.shape[1], n_total=n_in, name=f"gw_in_attn_{l}")
    gw_in = _matmul_tn(sv["h"], [dgate], into=gw_in, col0=n_in - dgate.shape[1], n_total=n_in, name=f"gw_in_gate_{l}")
    hooks.grad(l, "w_in", gw_in)
    dx0, dsh1, dsc1, dg_pre_mix = _hosted(hooks, "in_proj_bwd", l, _matmul_nt_norm_bwd, dproj, sv["w_in"], sv["x"], dx1,
                                          g_pre_mix, sc1, name=f"in_proj_bwd_{l}")
    dmod = jnp.concatenate([dsh1, dsc1, dgt1, dsh2, dsc2, dgt2], axis=0)
    dgains = jnp.concatenate([dg_pre_mix, dg_post_mix, dg_pre_mlp, dg_post_mlp], axis=0)
    return dx0, g_conv_w, dmod, dgains


BIG = ("w_in", "w_proj_conv", "w_proj_attn", "w_out", "w_mlp_in", "w_mlp_out")
SHARD_AXIS = dict(w_in=1, w_proj_conv=1, w_proj_attn=1, w_out=0, w_mlp_in=1, w_mlp_out=0)


class _LocalWeights:
    def __init__(self, wlayers):
        self.wlayers = wlayers
        self.grads = {}

    def weight(self, l, name):
        return self.wlayers[l][name]

    def side(self, kind, l):
        return None

    def done(self, kind, l, outs):
        pass

    def grad(self, l, name, g):
        self.grads[(l, name)] = g


def _local_step(x, target, mods, gains, conv_w, hooks):
    depth = mods.shape[0]
    tri = _tri(ATTN_BLOCK)
    saved = []
    for l in range(depth):
        mod = [mods[l, k:k + 1] for k in range(N_MOD)]
        gl = [gains[l, k:k + 1] for k in range(4)]
        x, sv = _layer_fwd(x, mod, gl, conv_w[l], tri, l=l, hooks=hooks)
        saved.append((sv, mod, gl))
    dx, loss = _loss_grad(x, target, name="loss_grad")
    dconv, dmods, dgains = [None] * depth, [None] * depth, [None] * depth
    for l in reversed(range(depth)):
        sv, mod, gl = saved[l]
        dx, dconv[l], dmods[l], dgains[l] = _layer_bwd(dx, sv, mod, gl, tri, l=l, hooks=hooks)
    return loss, dx, jnp.stack(dconv), jnp.stack(dmods), jnp.stack(dgains)


def _coords():
    return lax.axis_index("x"), lax.axis_index("y"), lax.axis_index("c")


def _flip(v, f):
    return 1 - v if f else v


def _all_gather_small(v, *, name):
    r, c_ = v.shape

    def body(v_ref, out_ref, send_sems, recv_sems, local_sem):
        x, y, c = _coords()
        me = 4 * x + 2 * y + c
        mine = pltpu.make_async_copy(v_ref, out_ref.at[me], local_sem)
        mine.start()
        copies = []
        for k in range(1, 8):
            fx, fy, fc = (k >> 2) & 1, (k >> 1) & 1, k & 1
            px, py, pc = _flip(x, fx), _flip(y, fy), _flip(c, fc)
            out = pltpu.make_async_remote_copy(src_ref=v_ref, dst_ref=out_ref.at[me], send_sem=send_sems.at[k - 1],
                                               recv_sem=recv_sems.at[k - 1], device_id=(px, py, pc), device_id_type=MESH)
            out.start()
            back = pltpu.make_async_remote_copy(src_ref=v_ref, dst_ref=out_ref.at[4 * px + 2 * py + pc],
                                                send_sem=send_sems.at[k - 1], recv_sem=recv_sems.at[k - 1],
                                                device_id=(px, py, pc), device_id_type=MESH)
            copies.append((out, back))
        for out, back in copies:
            back.wait_recv()
        for out, back in copies:
            out.wait_send()
        mine.wait()

    return pl.pallas_call(
        body, name=name,
        in_specs=[pl.BlockSpec(memory_space=pltpu.VMEM)],
        out_specs=pl.BlockSpec(memory_space=pltpu.VMEM),
        out_shape=_sds((8, r, c_), F32),
        scratch_shapes=[pltpu.SemaphoreType.DMA((7,)), pltpu.SemaphoreType.DMA((7,)), pltpu.SemaphoreType.DMA],
    )(v)


def _shard_dims(full_shape, axis):
    k, n = full_shape
    return (k // 4, n) if axis == 0 else (k, n // 4)


def _shard_window(ref, axis, chip, half, rows, cols):
    r0, rn = (0, rows) if half is None else (half * (rows // 2), rows // 2)
    if axis == 1:
        return ref.at[pl.ds(r0, rn), pl.ds(chip * cols, cols)]
    return ref.at[pl.ds(chip * rows + r0, rn), :]


def _cast_place(w, layer, axis, chip_arr, *, name, tr=256):
    _, rows, cols = w.shape
    tr = _fit(tr, rows)
    nb = rows // tr
    full = (rows * 4, cols) if axis == 0 else (rows, cols * 4)

    def body(chip_ref, w_ref, o_ref):
        o_ref[...] = w_ref[0].astype(BF16)

    if axis == 1:
        out_map = lambda i, chip: (i, chip[0])
    else:
        out_map = lambda i, chip: (chip[0] * nb + i, 0)
    grid_spec = pltpu.PrefetchScalarGridSpec(
        num_scalar_prefetch=1, grid=(nb,),
        in_specs=[pl.BlockSpec((1, tr, cols), lambda i, chip: (layer, i, 0))],
        out_specs=pl.BlockSpec((tr, cols), out_map))
    return pl.pallas_call(body, name=name, grid_spec=grid_spec, out_shape=_sds(full, BF16),
                          compiler_params=_params(("arbitrary",)))(chip_arr, w)


def _gather_side(fulls, axes):
    n = len(fulls)

    def copies(outs, sems):
        send_sems, recv_sems = sems
        x, y, c = _coords()
        chip = 2 * x + y
        sibling = (x, y, 1 - c)
        table = []
        for w in range(n):
            rows, cols = _shard_dims(outs[w].shape, axes[w])
            win = functools.partial(_shard_window, outs[w], axes[w], rows=rows, cols=cols)
            for j, (fx, fy) in enumerate(OTHER_CHIPS):
                px, py = _flip(x, fx), _flip(y, fy)
                pchip = 2 * px + py

                def copy(piece, sem, to):
                    return pltpu.make_async_remote_copy(src_ref=piece, dst_ref=piece, send_sem=send_sems.at[w, sem],
                                                        recv_sem=recv_sems.at[w, sem], device_id=to, device_id_type=MESH)

                table.append((copy(win(chip, c), j, (px, py, c)), copy(win(pchip, c), j, (px, py, c)),
                              copy(win(pchip, c), 3 + j, sibling), copy(win(pchip, 1 - c), 3 + j, sibling)))
        return table

    def start(ins, outs, sems):
        for send, _, _, _ in copies(outs, sems):
            send.start()

    def mid(ins, outs, sems):
        for _, landed, pass_on, _ in copies(outs, sems):
            landed.wait_recv()
            pass_on.start()

    def finish(ins, outs, sems):
        table = copies(outs, sems)
        for _, _, _, from_sibling in table:
            from_sibling.wait_recv()
        for send, _, pass_on, _ in table:
            send.wait_send()
            pass_on.wait_send()

    return _Side(fulls, [_sds(f.shape, f.dtype) for f in fulls], {w: w for w in range(n)},
                 [pltpu.SemaphoreType.DMA((n, 6)), pltpu.SemaphoreType.DMA((n, 6))], start, mid, finish)


def _exchange_side(grads, axes):
    n = len(grads)
    out_shapes = []
    for g, ax in zip(grads, axes):
        rows, cols = _shard_dims(g.shape, ax)
        out_shapes.append(_sds((7, rows // 2, cols), g.dtype))

    def copies(ins, outs, sems):
        send_sems, recv_sems = sems
        x, y, c = _coords()
        table = []
        for w in range(n):
            rows, cols = _shard_dims(ins[w].shape, axes[w])
            for k in range(1, 8):
                fx, fy, fc = (k >> 2) & 1, (k >> 1) & 1, k & 1
                px, py, pc = _flip(x, fx), _flip(y, fy), _flip(c, fc)
                piece = _shard_window(ins[w], axes[w], 2 * px + py, pc, rows, cols)
                table.append(pltpu.make_async_remote_copy(
                    src_ref=piece, dst_ref=outs[w].at[k - 1], send_sem=send_sems.at[w, k - 1],
                    recv_sem=recv_sems.at[w, k - 1], device_id=(px, py, pc), device_id_type=MESH))
        return table

    def start(ins, outs, sems):
        for cp in copies(ins, outs, sems):
            cp.start()

    def finish(ins, outs, sems):
        table = copies(ins, outs, sems)
        for cp in table:
            cp.wait_recv()
        for cp in table:
            cp.wait_send()

    return _Side(grads, out_shapes, {}, [pltpu.SemaphoreType.DMA((n, 7)), pltpu.SemaphoreType.DMA((n, 7))],
                 start, None, finish)


def _rs_sum_join(g, got, out_prev, layer, depth, axis, ids, *, name, tr=256):
    _, rows2, cols = got.shape
    tr = _fit(tr, rows2)
    nt = rows2 // tr
    if axis == 1:
        own_map = lambda i, ids_: (ids_[1] * nt + i, ids_[0])
    else:
        own_map = lambda i, ids_: ((ids_[0] * 2 + ids_[1]) * nt + i, 0)

    def body(ids_ref, g_ref, got_ref, *rest):
        out_ref, buf, local_sems, send_sems, recv_sem = rest[-5:]
        i = pl.program_id(0)
        x, y, c = _coords()
        sibling = (x, y, 1 - c)

        def copies(step, slot):
            rows_mine = pl.ds(c * rows2 + step * tr, tr)
            dst = out_ref.at[layer, rows_mine, :]
            keep = pltpu.make_async_copy(buf.at[slot], dst, local_sems.at[slot])
            give = pltpu.make_async_remote_copy(src_ref=buf.at[slot], dst_ref=dst, send_sem=send_sems.at[slot],
                                                recv_sem=recv_sem, device_id=sibling, device_id_type=MESH)
            return keep, give

        def drain(step, slot):
            keep, give = copies(step, slot)
            keep.wait()
            give.wait_send()

        slot = i % 2

        @pl.when(i >= 2)
        def _():
            drain(i - 2, slot)

        acc = g_ref[...].astype(F32)
        for k in range(7):
            acc = acc + got_ref[k].astype(F32)
        buf[slot] = acc
        keep, give = copies(i, slot)
        keep.start()
        give.start()

        @pl.when(i == nt - 1)
        def _():
            if nt >= 2:
                drain(nt - 2, (nt - 2) % 2)
            drain(nt - 1, (nt - 1) % 2)
            theirs = out_ref.at[layer, pl.ds((1 - c) * rows2, rows2), :]
            pltpu.make_async_remote_copy(src_ref=theirs, dst_ref=theirs, send_sem=send_sems.at[0], recv_sem=recv_sem,
                                         device_id=sibling, device_id_type=MESH).wait_recv()

    hbm = pl.BlockSpec(memory_space=pltpu.HBM)
    in_specs = [pl.BlockSpec((tr, cols), own_map), pl.BlockSpec((7, tr, cols), lambda i, ids_: (0, i, 0))]
    operands = [ids, g, got]
    aliases = {}
    if out_prev is not None:
        in_specs.append(hbm)
        operands.append(out_prev)
        aliases = {3: 0}
    grid_spec = pltpu.PrefetchScalarGridSpec(
        num_scalar_prefetch=1, grid=(nt,), in_specs=in_specs, out_specs=hbm,
        scratch_shapes=[pltpu.VMEM((2, tr, cols), F32), pltpu.SemaphoreType.DMA((2,)), pltpu.SemaphoreType.DMA((2,)),
                        pltpu.SemaphoreType.DMA])
    return pl.pallas_call(body, name=name, grid_spec=grid_spec, out_shape=_sds((depth, 2 * rows2, cols), F32),
                          input_output_aliases=aliases, compiler_params=_params(("arbitrary",)))(*operands)


MIX = ("w_proj_conv", "w_proj_attn", "w_out")


class _Schedule:
    def __init__(self, placed, depth, ids):
        self.placed, self.depth, self.ids = placed, depth, ids
        self.full, self.g, self.carried = {}, {}, None
        self.reduced = {k: None for k in BIG}
        first = [(0, "w_in")]
        self._landed(first, _side_call(self._gather(first), name="gather_w_in_0"))

    def _gather(self, keys):
        return _gather_side([self.placed[k] for k in keys], [SHARD_AXIS[k[1]] for k in keys])

    def _landed(self, keys, outs):
        for k, o in zip(keys, outs):
            self.full[k] = o

    def _exchange(self, keys):
        return _exchange_side([self.g[k] for k in keys], [SHARD_AXIS[k[1]] for k in keys])

    def _reduce(self, keys, got):
        for (l, name), pieces in zip(keys, got):
            self.reduced[name] = _rs_sum_join(self.g[(l, name)], pieces, self.reduced[name], l, self.depth,
                                              SHARD_AXIS[name], self.ids, name=f"rs_sum_join_{l}_{name}")

    def weight(self, l, name):
        return self.full[(l, name)]

    def grad(self, l, name, g):
        self.g[(l, name)] = g

    def side(self, kind, l):
        nxt = [(l + 1, "w_in")] if l + 1 < self.depth else []
        if kind == "in_proj":
            keys, make = [(l, k) for k in MIX + ("w_mlp_in",)], self._gather
        elif kind == "attn_fwd":
            keys, make = [(l, "w_mlp_out")] + nxt, self._gather
        elif kind == "attn_bwd":
            keys, make = [(l, k) for k in ("w_mlp_out", "w_mlp_in") + MIX], self._exchange
        else:
            keys, make = [(l, "w_in")], self._exchange
        self.carried = keys
        return make(keys)

    def done(self, kind, l, outs):
        (self._landed if kind in ("in_proj", "attn_fwd") else self._reduce)(self.carried, outs)


def _flat_rows(shape):
    rows = 1
    for s in shape[:-1]:
        rows *= s
    return rows, shape[-1]


def _row_tile(rows, cols, cap_bytes=2 * 1024 * 1024):
    t = rows
    while t * cols * 4 > cap_bytes and t % 16 == 0:
        t //= 2
    return t


def _ada_fwd(c_all, w_ada, b_loc, *, name, tn=512):
    l, d, nl = w_ada.shape
    b = c_all.shape[0]
    tn = min(tn, nl)

    def body(c_ref, w_ref, b_ref, o_ref):
        o_ref[0] = jnp.dot(c_ref[...], w_ref[0], preferred_element_type=F32,
                           precision=lax.Precision.HIGHEST) + b_ref[0]

    return pl.pallas_call(
        body, name=name, grid=(l, nl // tn),
        in_specs=[pl.BlockSpec((b, d), lambda i, j: (0, 0)), pl.BlockSpec((1, d, tn), lambda i, j: (i, 0, j)),
                  pl.BlockSpec((1, 1, tn), lambda i, j: (i, 0, j))],
        out_specs=pl.BlockSpec((1, b, tn), lambda i, j: (i, 0, j)),
        out_shape=_sds((l, b, nl), F32),
        compiler_params=_params(("parallel", "parallel")),
    )(c_all, w_ada, b_loc)


def _ada_bwd(c_t, dmod_loc, *, name, tn=512):
    d, b = c_t.shape
    l, _, nl = dmod_loc.shape
    tn = min(tn, nl)

    def body(c_ref, dm_ref, o_ref):
        cv = c_ref[...]
        dm = dm_ref[0]
        acc = cv[:, 0:1] * dm[0:1, :]
        for k in range(1, b):
            acc = acc + cv[:, k:k + 1] * dm[k:k + 1, :]
        o_ref[0] = acc

    return pl.pallas_call(
        body, name=name, grid=(l, nl // tn),
        in_specs=[pl.BlockSpec((d, b), lambda i, j: (0, 0)), pl.BlockSpec((1, b, tn), lambda i, j: (i, 0, j))],
        out_specs=pl.BlockSpec((1, d, tn), lambda i, j: (i, 0, j)),
        out_shape=_sds((l, d, nl), F32),
        compiler_params=_params(("parallel", "parallel")),
    )(c_t, dmod_loc)


def _sum_devices(p, *, name):
    k, r, c_ = p.shape

    def body(p_ref, o_ref):
        acc = p_ref[0]
        for j in range(1, k):
            acc = acc + p_ref[j]
        o_ref[...] = acc

    return pl.pallas_call(body, name=name, out_shape=_sds((r, c_), F32),
                          in_specs=[pl.BlockSpec(memory_space=pltpu.VMEM)],
                          out_specs=pl.BlockSpec(memory_space=pltpu.VMEM))(p)


def _adamw(w, g, m, v, *, name):
    shape = w.shape
    rows, cols = _flat_rows(shape)
    tr = _row_tile(rows, cols, cap_bytes=1024 * 1024)
    c1 = 1.0 / (1.0 - ADAM_B1 ** ADAM_STEP)
    c2 = 1.0 / (1.0 - ADAM_B2 ** ADAM_STEP)

    def body(w_ref, g_ref, m_ref, v_ref, d_ref, nm_ref, nv_ref):
        gv = g_ref[...]
        nm = ADAM_B1 * m_ref[...] + (1.0 - ADAM_B1) * gv
        nv = ADAM_B2 * v_ref[...] + (1.0 - ADAM_B2) * (gv * gv)
        m_hat = nm * c1
        v_hat = nv * c2
        d_ref[...] = -ADAM_LR * (m_hat / (jnp.sqrt(v_hat) + ADAM_EPS) + ADAM_WD * w_ref[...])
        nm_ref[...] = nm
        nv_ref[...] = nv

    spec = pl.BlockSpec((tr, cols), lambda i: (i, 0))
    flat = lambda a: a.reshape(rows, cols)
    outs = pl.pallas_call(body, name=name, grid=(rows // tr,), in_specs=[spec] * 4, out_specs=[spec] * 3,
                          out_shape=[_sds((rows, cols), F32)] * 3, compiler_params=_params(("parallel",)),
                          )(flat(w), flat(g), flat(m), flat(v))
    return tuple(o.reshape(shape) for o in outs)


WEIGHTS = ("w_ada", "b_ada", "g_pre_mix", "g_post_mix", "g_pre_mlp", "g_post_mlp", "w_in", "conv_w",
           "w_proj_conv", "w_proj_attn", "w_out", "w_mlp_in", "w_mlp_out")
GAINS = ("g_pre_mix", "g_post_mix", "g_pre_mlp", "g_post_mlp")


def kernel(x, c, w_ada, b_ada, g_pre_mix, g_post_mix, g_pre_mlp, g_post_mlp, w_in, conv_w, w_proj_conv, w_proj_attn, w_out, w_mlp_in, w_mlp_out, loss_target, m_w_ada, m_b_ada, m_g_pre_mix, m_g_post_mix, m_g_pre_mlp, m_g_post_mlp, m_w_in, m_conv_w, m_w_proj_conv, m_w_proj_attn, m_w_out, m_w_mlp_in, m_w_mlp_out, v_w_ada, v_b_ada, v_g_pre_mix, v_g_post_mix, v_g_pre_mlp, v_g_post_mlp, v_w_in, v_conv_w, v_w_proj_conv, v_w_proj_attn, v_w_out, v_w_mlp_in, v_w_mlp_out):
    params = dict(w_ada=w_ada, b_ada=b_ada, g_pre_mix=g_pre_mix, g_post_mix=g_post_mix, g_pre_mlp=g_pre_mlp,
                  g_post_mlp=g_post_mlp, w_in=w_in, conv_w=conv_w, w_proj_conv=w_proj_conv, w_proj_attn=w_proj_attn,
                  w_out=w_out, w_mlp_in=w_mlp_in, w_mlp_out=w_mlp_out)
    m_in = dict(w_ada=m_w_ada, b_ada=m_b_ada, g_pre_mix=m_g_pre_mix, g_post_mix=m_g_post_mix, g_pre_mlp=m_g_pre_mlp,
                g_post_mlp=m_g_post_mlp, w_in=m_w_in, conv_w=m_conv_w, w_proj_conv=m_w_proj_conv,
                w_proj_attn=m_w_proj_attn, w_out=m_w_out, w_mlp_in=m_w_mlp_in, w_mlp_out=m_w_mlp_out)
    v_in = dict(w_ada=v_w_ada, b_ada=v_b_ada, g_pre_mix=v_g_pre_mix, g_post_mix=v_g_post_mix, g_pre_mlp=v_g_pre_mlp,
                g_post_mlp=v_g_post_mlp, w_in=v_w_in, conv_w=v_conv_w, w_proj_conv=v_w_proj_conv,
                w_proj_attn=v_w_proj_attn, w_out=v_w_out, w_mlp_in=v_w_mlp_in, w_mlp_out=v_w_mlp_out)

    depth, d, nl_ada = w_ada.shape
    ix, iy, ic = lax.axis_index("x"), lax.axis_index("y"), lax.axis_index("c")
    chip = 2 * ix + iy
    me = 4 * ix + 2 * iy + ic
    xs = x[0]
    target = loss_target[0]

    c_all = _all_gather_small(jnp.broadcast_to(c, (8, d)), name="gather_c")[:, 0, :]
    b_loc = lax.dynamic_slice_in_dim(b_ada, chip * nl_ada, nl_ada, axis=1)[:, None, :]
    mod_loc = _ada_fwd(c_all, w_ada, b_loc, name="ada_fwd")
    mod_all = _all_gather_small(mod_loc.reshape(depth * 8, nl_ada), name="gather_mod")
    mod_all = mod_all.reshape(4, 2, depth, 8, nl_ada)[:, 0]
    mod_me = lax.dynamic_index_in_dim(mod_all, me, axis=2, keepdims=False)
    mods = jnp.transpose(mod_me, (1, 0, 2)).reshape(depth, N_MOD, d)

    chip_arr = jnp.reshape(chip, (1,)).astype(jnp.int32)
    ids = jnp.stack([chip, ic]).astype(jnp.int32)
    placed = {(l, k): _cast_place(params[k], l, SHARD_AXIS[k], chip_arr, name=f"place_{k}_{l}")
              for l in range(depth) for k in BIG}
    conv_full = _all_gather_small(
        jnp.pad(conv_w.reshape(depth * 3, -1), ((0, 8 - depth * 3), (0, 0))), name="gather_conv_w")
    conv_full = conv_full.reshape(4, 2, 8, -1)[:, 0, :depth * 3]
    conv_full = jnp.transpose(conv_full, (1, 0, 2)).reshape(depth, 3, -1)

    gains = jnp.stack([params[k] for k in GAINS], axis=1)
    schedule = _Schedule(placed, depth, ids)
    loss, dx, conv_grads, dmods, dgains = _local_step(xs, target, mods, gains, conv_full, schedule)

    cw = conv_full.shape[2]
    rows = [dmods.reshape(depth * N_MOD, d), dgains.reshape(depth * 4, d),
            conv_grads.reshape(-1, d), jnp.broadcast_to(loss, (1, d))]
    payload = jnp.concatenate(rows, axis=0)
    n_rows = payload.shape[0]
    pad = (-n_rows) % 8
    payload = jnp.pad(payload, ((0, pad), (0, 0)))
    everyone = _all_gather_small(payload, name="gather_small_grads")
    total = _sum_devices(everyone, name="sum_small_grads")
    r0 = depth * N_MOD
    grads = {}
    grads["b_ada"] = total[:r0].reshape(depth, N_MOD * d)
    gsum = total[r0:r0 + depth * 4].reshape(depth, 4, d)
    for k, name in enumerate(GAINS):
        grads[name] = gsum[:, k]
    r1 = r0 + depth * 4
    n_conv = (depth * 3 * cw) // d
    conv_g = total[r1:r1 + n_conv].reshape(depth, 3, cw)
    grads["conv_w"] = lax.dynamic_slice_in_dim(conv_g, chip * (cw // 4), cw // 4, axis=2)
    loss_out = total[r1 + n_conv, 0]
    dmod_all = everyone[:, :r0].reshape(8, depth, N_MOD * d)
    dmod_loc = lax.dynamic_slice_in_dim(dmod_all, chip * nl_ada, nl_ada, axis=2)
    grads["w_ada"] = _ada_bwd(c_all.T, jnp.transpose(dmod_loc, (1, 0, 2)), name="ada_bwd")

    for k in BIG:
        grads[k] = schedule.reduced[k]

    deltas, new_m, new_v = {}, {}, {}
    for k in WEIGHTS:
        deltas[k], new_m[k], new_v[k] = _adamw(params[k], grads[k], m_in[k], v_in[k], name=f"adamw_{k}")

    return (loss_out, dx[None], *[grads[k] for k in WEIGHTS], *[deltas[k] for k in WEIGHTS],
            *[new_m[k] for k in WEIGHTS], *[new_v[k] for k in WEIGHTS])
```

```python
import functools

import jax
import jax.numpy as jnp
from jax import lax
from jax.experimental import pallas as pl
from jax.experimental.pallas import tpu as pltpu

F32 = jnp.float32
BF16 = jnp.bfloat16
EPS = 1e-6
N_MOD = 6
HEAD_DIM = 64
LANES = 128
ATTN_SCALE = 1.0 / 8.0
UNDERFLOW_LOG = -90.0
ATTN_BLOCK = 256
ATTN_CHAINS = (4, 4)
ADAM_LR = 0.001
ADAM_B1 = 0.9
ADAM_B2 = 0.999
ADAM_EPS = 1e-08
ADAM_WD = 0.01
ADAM_STEP = 10
VMEM_LIMIT = 56 * 1024 * 1024
MESH = pl.DeviceIdType.MESH
OTHER_CHIPS = ((1, 0), (0, 1), (1, 1))

_NT = (((1,), (1,)), ((), ()))
_TN = (((0,), (0,)), ((), ()))


def _sds(shape, dtype):
    return jax.ShapeDtypeStruct(shape, dtype)


def _params(sem):
    return pltpu.CompilerParams(dimension_semantics=sem, vmem_limit_bytes=VMEM_LIMIT)


def _fit(t, n):
    t = min(t, n)
    while n % t:
        t //= 2
    return t


def _vec_spec(d, nargs):
    if nargs == 1:
        return pl.BlockSpec((1, d), lambda i: (0, 0))
    return pl.BlockSpec((1, d), lambda i, j: (0, 0))


def _log_one_minus_sigmoid(z):
    return -jnp.log(1.0 + jnp.exp(-jnp.abs(z))) - jnp.maximum(z, 0.0)


def _sigmoid(z):
    return 0.5 * jnp.tanh(0.5 * z) + 0.5


def _split_bf16(a):
    hi = a.astype(BF16)
    lo = (a - hi.astype(F32)).astype(BF16)
    return hi, lo


def _rms_bwd(dn, xin, g):
    r = lax.rsqrt(jnp.mean(xin * xin, axis=-1, keepdims=True) + EPS)
    xh = xin * r
    dxh = dn * g
    dxin = r * (dxh - xh * jnp.mean(dxh * xh, axis=-1, keepdims=True))
    return dxin, xh


def _colsum(a):
    return jnp.sum(a, axis=0, keepdims=True)


def _norm_mod_matmul(x, g, sc, sh, w, *, name, tm=512, side=None):
    s, d = x.shape
    n = w.shape[1]
    tm = _fit(tm, s)
    nt = s // tm

    def body(*refs):
        i = pl.program_id(0)
        (x_ref, g_ref, sc_ref, sh_ref, w_ref, h_ref, o_ref), late_phases = _host_side(
            side, 5, 2, 0, refs, i == 0, i == (3 * nt) // 4, i == nt - 1)
        xv = x_ref[...]
        r = lax.rsqrt(jnp.mean(xv * xv, axis=-1, keepdims=True) + EPS)
        h = ((xv * r * g_ref[...]) * (1.0 + sc_ref[...]) + sh_ref[...]).astype(BF16)
        h_ref[...] = h
        o_ref[...] = jnp.dot(h, w_ref[...], preferred_element_type=F32).astype(BF16)
        late_phases()

    s_in, s_out, s_shapes, aliases, s_scratch = _side_specs(side, 5, 2)
    res = pl.pallas_call(
        body, name=name, grid=(nt,),
        in_specs=[pl.BlockSpec((tm, d), lambda i: (i, 0)), _vec_spec(d, 1), _vec_spec(d, 1), _vec_spec(d, 1),
                  pl.BlockSpec((d, n), lambda i: (0, 0))] + s_in,
        out_specs=[pl.BlockSpec((tm, d), lambda i: (i, 0)), pl.BlockSpec((tm, n), lambda i: (i, 0))] + s_out,
        out_shape=[_sds((s, d), BF16), _sds((s, n), BF16)] + s_shapes,
        input_output_aliases=aliases, scratch_shapes=s_scratch,
        compiler_params=_params(("arbitrary",)),
    )(x, g, sc, sh, w, *([] if side is None else side.operands))
    return res[:2], res[2:]


HALO = 16


def _conv_fwd(proj, conv_w, *, name, tm=512):
    s = proj.shape[0]
    cw = conv_w.shape[1]
    tm = min(tm, s)
    nb = tm // HALO

    def body(bg_ref, cg_ref, u_ref, cgh_ref, uh_ref, w_ref, yc_ref, vbuf):
        i = pl.program_id(0)
        vv = cg_ref[...].astype(F32) * u_ref[...].astype(F32)
        halo = cgh_ref[...].astype(F32) * uh_ref[...].astype(F32)
        vbuf[0:HALO, :] = jnp.where(i > 0, halo, 0.0)
        vbuf[HALO:HALO + tm, :] = vv
        v1 = vbuf[HALO - 1:HALO - 1 + tm, :]
        v2 = vbuf[HALO - 2:HALO - 2 + tm, :]
        w = w_ref[...]
        y = w[2:3, :] * vv + w[1:2, :] * v1 + w[0:1, :] * v2
        yc_ref[...] = (bg_ref[...].astype(F32) * y).astype(BF16)

    def prev(i):
        return jnp.maximum(i * nb - 1, 0)

    return pl.pallas_call(
        body, name=name, grid=(s // tm,),
        in_specs=[pl.BlockSpec((tm, cw), lambda i: (i, 0)), pl.BlockSpec((tm, cw), lambda i: (i, 1)),
                  pl.BlockSpec((tm, cw), lambda i: (i, 2)),
                  pl.BlockSpec((HALO, cw), lambda i: (prev(i), 1)), pl.BlockSpec((HALO, cw), lambda i: (prev(i), 2)),
                  pl.BlockSpec((3, cw), lambda i: (0, 0))],
        out_specs=pl.BlockSpec((tm, cw), lambda i: (i, 0)),
        out_shape=_sds((s, cw), BF16),
        scratch_shapes=[pltpu.VMEM((HALO + tm, cw), F32)],
        compiler_params=_params(("arbitrary",)),
    )(proj, proj, proj, proj, proj, conv_w)


def _tri(qb):
    r = lax.broadcasted_iota(jnp.int32, (qb, qb), 0)
    c = lax.broadcasted_iota(jnp.int32, (qb, qb), 1)
    return (r >= c).astype(BF16)


def _head_mask(h):
    lane = lax.broadcasted_iota(jnp.int32, (1, LANES), 1)
    return (lane >= HEAD_DIM * h) & (lane < HEAD_DIM * (h + 1))


def _stack_heads(a, masks):
    return jnp.concatenate([jnp.where(m, a, 0).astype(BF16) for m in masks], axis=0)


def _heads_to_lanes(a, qb):
    return jnp.concatenate([a[:qb], a[qb:]], axis=1)


def _stacked_causal(qb, width, first_key, first_query):
    row = lax.broadcasted_iota(jnp.int32, (2 * qb, width), 0)
    col = lax.broadcasted_iota(jnp.int32, (2 * qb, width), 1)
    return first_key + col < first_query + jnp.where(row >= qb, row - qb, row)


def _running_sum(a, tri_m):
    rows, qb = a.shape[0], tri_m.shape[0]
    n = a.shape[1] // qb
    hi, lo = _split_bf16(a)
    stacked = jnp.concatenate([p[:, s * qb:(s + 1) * qb] for s in range(n) for p in (hi, lo)], axis=0)
    both = jnp.dot(stacked, tri_m, preferred_element_type=F32)
    parts = [both[(2 * s) * rows:(2 * s + 1) * rows] + both[(2 * s + 1) * rows:(2 * s + 2) * rows] for s in range(n)]
    later = None
    for s in reversed(range(n)):
        if later is not None:
            parts[s] = parts[s] + later
        later = parts[s][:, 0:1]
    return (parts[0] if n == 1 else jnp.concatenate(parts, axis=1)), later


def _attn_cols(d):
    cw = d // 2
    hp = (d // 2) // LANES
    q0 = (3 * cw) // LANES
    return q0, q0 + hp, q0 + 2 * hp, hp


class _Side:
    def __init__(self, operands, out_shapes, aliases, scratch, start, mid, finish):
        self.operands, self.out_shapes, self.aliases, self.scratch = list(operands), list(out_shapes), aliases, list(scratch)
        self.start, self.mid, self.finish = start, mid, finish


def _side_call(side, *, name):
    n_in, n_out = len(side.operands), len(side.out_shapes)

    def body(*refs):
        parts = refs[:n_in], refs[n_in:n_in + n_out], refs[n_in + n_out:]
        side.start(*parts)
        if side.mid is not None:
            side.mid(*parts)
        side.finish(*parts)

    hbm = pl.BlockSpec(memory_space=pltpu.HBM)
    return pl.pallas_call(body, name=name, in_specs=[hbm] * n_in, out_specs=[hbm] * n_out, out_shape=side.out_shapes,
                          input_output_aliases=dict(side.aliases), scratch_shapes=side.scratch)(*side.operands)


def _host_side(side, n_in, n_out, n_scratch, refs, first, late, last):
    if side is None:
        return refs, lambda: None
    s_in, s_out = len(side.operands), len(side.out_shapes)
    ins = refs[:n_in]
    side_in = refs[n_in:n_in + s_in]
    outs = refs[n_in + s_in:n_in + s_in + n_out]
    side_out = refs[n_in + s_in + n_out:n_in + s_in + n_out + s_out]
    rest = refs[n_in + s_in + n_out + s_out:]
    scratch, sems = rest[:n_scratch], rest[n_scratch:]
    parts = (side_in, side_out, sems)
    pl.when(first)(lambda: side.start(*parts))

    def run_late_phases():
        if side.mid is not None:
            pl.when(late)(lambda: side.mid(*parts))
        pl.when(last)(lambda: side.finish(*parts))

    return (*ins, *outs, *scratch), run_late_phases


def _side_specs(side, n_in, n_out):
    if side is None:
        return [], [], [], {}, []
    hbm = pl.BlockSpec(memory_space=pltpu.HBM)
    s_in = len(side.operands)
    aliases = {n_in + a: n_out + b for a, b in side.aliases.items()}
    return [hbm] * s_in, [hbm] * len(side.out_shapes), side.out_shapes, aliases, side.scratch


def _attn_fwd(proj, tri, *, d, name, side=None):
    s = proj.shape[0]
    qb = tri.shape[0]
    chains = _fit(ATTN_CHAINS[0], s // qb)
    ng = s // (qb * chains)
    q0, k0, v0, hp = _attn_cols(d)

    def body(*refs):
        p, g = pl.program_id(0), pl.program_id(1)
        (q_ref, k_ref, v_ref, tri_ref, o_ref, a_ref, b_ref, run_ref), late_phases = _host_side(
            side, 4, 4, 0, refs, (p == 0) & (g == 0), (p == hp - 1) & (g == 0), (p == hp - 1) & (g == ng - 1))
        tri_m = tri_ref[...]
        masks = [_head_mask(h) for h in range(2)]

        def first_steps(u):
            i = g * chains + u
            qs = _stack_heads(q_ref[u * qb:(u + 1) * qb, :] * ATTN_SCALE, masks)

            def strip(j, state, causal=None, keep=None, live=None):
                run, acc = state
                rows = pl.ds(pl.multiple_of(j * qb, qb), qb)
                z = lax.dot_general(qs, k_ref[rows, :], _NT, preferred_element_type=F32)
                lg = _log_one_minus_sigmoid(z)
                beta = 1.0 - jnp.exp(lg) if keep is not None else None
                if causal is not None:
                    lg = jnp.where(causal, lg, 0.0)
                cs, total = _running_sum(lg, tri_m)
                a = jnp.exp(z + cs + run)
                if causal is not None:
                    a = jnp.where(causal, a, 0.0)
                    beta = jnp.where(causal, beta, 0.0)
                if live is not None:
                    a = jnp.where(live, a, 0.0)
                    beta = jnp.where(live, beta, 0.0)
                    total = jnp.where(live, total, 0.0)
                ab = a.astype(BF16)
                if keep is not None:
                    a_ref[0, u, keep] = ab
                    b_ref[0, u, keep] = beta.astype(BF16)
                acc = acc + jnp.dot(_heads_to_lanes(ab, qb), _stack_heads(v_ref[rows, :], masks),
                                    preferred_element_type=F32)
                return run + total, acc

            state = strip(i, (jnp.zeros((2 * qb, 1), F32), jnp.zeros((qb, LANES), F32)),
                          causal=_stacked_causal(qb, qb, 0, 0), keep=0)
            state = strip(jnp.maximum(i - 1, 0), state, keep=1, live=i >= 1)
            run_ref[0, u] = jnp.broadcast_to(state[0], (2 * qb, LANES))
            return i, strip, state

        started = [first_steps(u) for u in range(chains)]
        for u, (i, strip, state) in enumerate(started):
            state = lax.while_loop(
                lambda st: (st[0] >= 0) & (jnp.max(st[1]) > UNDERFLOW_LOG),
                lambda st, strip=strip: (st[0] - 1, *strip(st[0], st[1:])),
                (i - 2, *state))
            o_ref[u * qb:(u + 1) * qb, :] = state[2]
        late_phases()

    s_in, s_out, s_shapes, aliases, s_scratch = _side_specs(side, 4, 4)
    tq = qb * chains
    nq = s // qb
    res = pl.pallas_call(
        body, name=name, grid=(hp, ng),
        in_specs=[pl.BlockSpec((tq, LANES), lambda p, i: (i, q0 + p)),
                  pl.BlockSpec((s, LANES), lambda p, i: (0, k0 + p)),
                  pl.BlockSpec((s, LANES), lambda p, i: (0, v0 + p)),
                  pl.BlockSpec((qb, qb), lambda p, i: (0, 0))] + s_in,
        out_specs=[pl.BlockSpec((tq, LANES), lambda p, i: (i, p)),
                   pl.BlockSpec((1, chains, 2, 2 * qb, qb), lambda p, i: (p, i, 0, 0, 0)),
                   pl.BlockSpec((1, chains, 2, 2 * qb, qb), lambda p, i: (p, i, 0, 0, 0)),
                   pl.BlockSpec((1, chains, 2 * qb, LANES), lambda p, i: (p, i, 0, 0))] + s_out,
        out_shape=[_sds((s, hp * LANES), F32), _sds((hp, nq, 2, 2 * qb, qb), BF16), _sds((hp, nq, 2, 2 * qb, qb), BF16),
                   _sds((hp, nq, 2 * qb, LANES), F32)] + s_shapes,
        input_output_aliases=aliases, scratch_shapes=s_scratch,
        compiler_params=_params(("arbitrary", "arbitrary")),
    )(proj, proj, proj, tri, *([] if side is None else side.operands))
    return (res[0], tuple(res[1:4])), res[4:]


def _mix_out(yc, o, proj, x, wpc, wpa, wout, g, gt, *, name, tm=256):
    s, d = x.shape
    cw = yc.shape[1]
    tm = min(tm, s)
    ga_blk = (3 * cw + 3 * (d // 2)) // d

    def body(yc_ref, o_ref, ga_ref, gb_ref, x_ref, wpc_ref, wpa_ref, wout_ref, g_ref, gt_ref,
             ycv_ref, yat_ref, mg_ref, mix_ref, x1_ref):
        y_conv = jnp.dot(yc_ref[...], wpc_ref[...], preferred_element_type=F32)
        y_attn = jnp.dot(o_ref[...].astype(BF16), wpa_ref[...], preferred_element_type=F32)
        merged = (_sigmoid(ga_ref[...].astype(F32)) * y_conv + _sigmoid(gb_ref[...].astype(F32)) * y_attn)
        mg = merged.astype(BF16)
        mix = jnp.dot(mg, wout_ref[...], preferred_element_type=F32)
        r = lax.rsqrt(jnp.mean(mix * mix, axis=-1, keepdims=True) + EPS)
        ycv_ref[...] = y_conv.astype(BF16)
        yat_ref[...] = y_attn.astype(BF16)
        mg_ref[...] = mg
        mix_ref[...] = mix
        x1_ref[...] = x_ref[...] + gt_ref[...] * (mix * r * g_ref[...])

    def rows(w):
        return pl.BlockSpec((tm, w), lambda i: (i, 0))

    def full(a):
        return pl.BlockSpec(a.shape, lambda i: (0, 0))

    return pl.pallas_call(
        body, name=name, grid=(s // tm,),
        in_specs=[rows(cw), rows(d // 2), pl.BlockSpec((tm, d), lambda i: (i, ga_blk)),
                  pl.BlockSpec((tm, d), lambda i: (i, ga_blk + 1)), rows(d),
                  full(wpc), full(wpa), full(wout), _vec_spec(d, 1), _vec_spec(d, 1)],
        out_specs=[rows(d), rows(d), rows(d), rows(d), rows(d)],
        out_shape=[_sds((s, d), BF16), _sds((s, d), BF16), _sds((s, d), BF16), _sds((s, d), F32), _sds((s, d), F32)],
        compiler_params=_params(("parallel",)),
    )(yc, o, proj, proj, x, wpc, wpa, wout, g, gt)


def _relu2(a):
    r = jnp.maximum(a.astype(F32), 0.0)
    return (r * r).astype(BF16)


def _mlp_out(a, x, w2, g, gt, *, name, tm=512):
    s, d = x.shape
    dff = a.shape[1]
    tm = min(tm, s)

    def body(a_ref, x_ref, w_ref, g_ref, gt_ref, ff_ref, x2_ref):
        ff = jnp.dot(_relu2(a_ref[...]), w_ref[...], preferred_element_type=F32)
        r = lax.rsqrt(jnp.mean(ff * ff, axis=-1, keepdims=True) + EPS)
        ff_ref[...] = ff
        x2_ref[...] = x_ref[...] + gt_ref[...] * (ff * r * g_ref[...])

    return pl.pallas_call(
        body, name=name, grid=(s // tm,),
        in_specs=[pl.BlockSpec((tm, dff), lambda i: (i, 0)), pl.BlockSpec((tm, d), lambda i: (i, 0)),
                  pl.BlockSpec((dff, d), lambda i: (0, 0)), _vec_spec(d, 1), _vec_spec(d, 1)],
        out_specs=[pl.BlockSpec((tm, d), lambda i: (i, 0)), pl.BlockSpec((tm, d), lambda i: (i, 0))],
        out_shape=[_sds((s, d), F32), _sds((s, d), F32)],
        compiler_params=_params(("parallel",)),
    )(a, x, w2, g, gt)


def _loss_grad(y, target, *, name, tm=512):
    s, d = y.shape
    tm = min(tm, s)

    def body(y_ref, t_ref, dy_ref, loss_ref):
        @pl.when(pl.program_id(0) == 0)
        def _():
            loss_ref[...] = jnp.zeros_like(loss_ref)
        e = y_ref[...] - t_ref[...]
        dy_ref[...] = e * (1.0 / d)
        loss_ref[...] += 0.5 * jnp.sum(jnp.mean(e * e, axis=-1, keepdims=True), axis=0, keepdims=True)

    return pl.pallas_call(
        body, name=name, grid=(s // tm,),
        in_specs=[pl.BlockSpec((tm, d), lambda i: (i, 0)), pl.BlockSpec((tm, d), lambda i: (i, 0))],
        out_specs=[pl.BlockSpec((tm, d), lambda i: (i, 0)), pl.BlockSpec((1, 1), lambda i: (0, 0))],
        out_shape=[_sds((s, d), F32), _sds((1, 1), F32)],
        compiler_params=_params(("arbitrary",)),
    )(y, target)


def _mlp_out_bwd(dx, ff, a, w2, g, gt, *, name, tm=512):
    s, d = dx.shape
    dff = a.shape[1]
    tm = min(tm, s)

    def body(dx_ref, ff_ref, a_ref, w_ref, g_ref, gt_ref, dff_ref, da_ref, dgt_ref, dg_ref):
        @pl.when(pl.program_id(0) == 0)
        def _():
            dgt_ref[...] = jnp.zeros_like(dgt_ref)
            dg_ref[...] = jnp.zeros_like(dg_ref)
        dxv = dx_ref[...]
        dn = dxv * gt_ref[...]
        dffv, xh = _rms_bwd(dn, ff_ref[...], g_ref[...])
        dgt_ref[...] += _colsum(dxv * (xh * g_ref[...]))
        dg_ref[...] += _colsum(dn * xh)
        dffb = dffv.astype(BF16)
        dff_ref[...] = dffb
        drr = lax.dot_general(dffb, w_ref[...], _NT, preferred_element_type=F32)
        da_ref[...] = (drr * (2.0 * jnp.maximum(a_ref[...].astype(F32), 0.0))).astype(BF16)

    return pl.pallas_call(
        body, name=name, grid=(s // tm,),
        in_specs=[pl.BlockSpec((tm, d), lambda i: (i, 0)), pl.BlockSpec((tm, d), lambda i: (i, 0)),
                  pl.BlockSpec((tm, dff), lambda i: (i, 0)), pl.BlockSpec((dff, d), lambda i: (0, 0)),
                  _vec_spec(d, 1), _vec_spec(d, 1)],
        out_specs=[pl.BlockSpec((tm, d), lambda i: (i, 0)), pl.BlockSpec((tm, dff), lambda i: (i, 0)),
                   _vec_spec(d, 1), _vec_spec(d, 1)],
        out_shape=[_sds((s, d), BF16), _sds((s, dff), BF16), _sds((1, d), F32), _sds((1, d), F32)],
        compiler_params=_params(("arbitrary",)),
    )(dx, ff, a, w2, g, gt)


def _matmul_nt_norm_bwd(dys, w, x, dres, g, sc, *, name, tm=512, side=None):
    s = dys[0].shape[0]
    widths = [dy.shape[1] for dy in dys]
    d, n = w.shape
    assert sum(widths) == n, (widths, n)
    tm = _fit(tm, s)
    nt = s // tm
    np_ = len(dys)

    def body(*refs):
        i = pl.program_id(0)
        own, late_phases = _host_side(side, np_ + 5, 4, 0, refs, i == 0, i == (3 * nt) // 4, i == nt - 1)
        dy_refs = own[:np_]
        w_ref, x_ref, dres_ref, g_ref, sc_ref, dx_ref, dsh_ref, dsc_ref, dg_ref = own[np_:]

        @pl.when(i == 0)
        def _():
            dsh_ref[...] = jnp.zeros_like(dsh_ref)
            dsc_ref[...] = jnp.zeros_like(dsc_ref)
            dg_ref[...] = jnp.zeros_like(dg_ref)

        dh = None
        for p, dy_ref in enumerate(dy_refs):
            cols = slice(sum(widths[:p]), sum(widths[:p + 1]))
            part = lax.dot_general(dy_ref[...], w_ref[:, cols], _NT, preferred_element_type=F32)
            dh = part if dh is None else dh + part
        dn = dh * (1.0 + sc_ref[...])
        dxin, xh = _rms_bwd(dn, x_ref[...], g_ref[...])
        dsh_ref[...] += _colsum(dh)
        dsc_ref[...] += _colsum(dh * (xh * g_ref[...]))
        dg_ref[...] += _colsum(dn * xh)
        dx_ref[...] = dres_ref[...] + dxin
        late_phases()

    s_in, s_out, s_shapes, aliases, s_scratch = _side_specs(side, np_ + 5, 4)
    res = pl.pallas_call(
        body, name=name, grid=(nt,),
        in_specs=[pl.BlockSpec((tm, wd), lambda i: (i, 0)) for wd in widths]
        + [pl.BlockSpec((d, n), lambda i: (0, 0)),
           pl.BlockSpec((tm, d), lambda i: (i, 0)), pl.BlockSpec((tm, d), lambda i: (i, 0)),
           _vec_spec(d, 1), _vec_spec(d, 1)] + s_in,
        out_specs=[pl.BlockSpec((tm, d), lambda i: (i, 0)), _vec_spec(d, 1), _vec_spec(d, 1), _vec_spec(d, 1)] + s_out,
        out_shape=[_sds((s, d), F32), _sds((1, d), F32), _sds((1, d), F32), _sds((1, d), F32)] + s_shapes,
        input_output_aliases=aliases, scratch_shapes=s_scratch,
        compiler_params=_params(("arbitrary",)),
    )(*dys, w, x, dres, g, sc, *([] if side is None else side.operands))
    return res[:4], res[4:]


def _matmul_tn(a, bs, *, name, tk=1024, tn=1024, ts=512, relu2=False, into=None, col0=0, n_total=None):
    s, k = a.shape
    widths = [b.shape[1] for b in bs]
    n = sum(widths)
    tk, ts = _fit(tk, k), _fit(ts, s)
    for w in widths:
        tn = _fit(tn, w)
    while col0 % tn:
        tn //= 2
    nt = s // ts
    assert tn % LANES == 0 and all(sum(widths[:p]) % tn == 0 for p in range(len(bs))), (widths, tn)
    first = [sum(widths[:p]) // tn for p in range(len(bs))]
    tiles = [w // tn for w in widths]
    tile0 = col0 // tn

    def body(a_ref, *rest):
        b_refs, o_ref, acc = rest[:len(bs)], rest[-2], rest[-1]
        j, t = pl.program_id(1), pl.program_id(2)

        @pl.when(t == 0)
        def _():
            acc[...] = jnp.zeros_like(acc)
        av = a_ref[...]
        av = _relu2(av) if relu2 else av.astype(BF16)
        for p, b_ref in enumerate(b_refs):
            def add(b_ref=b_ref):
                acc[...] += lax.dot_general(av, b_ref[...], _TN, preferred_element_type=F32)
            if len(bs) == 1:
                add()
            else:
                pl.when((j >= first[p]) & (j < first[p] + tiles[p]))(add)

        @pl.when(t == nt - 1)
        def _():
            o_ref[...] = acc[...].astype(BF16)

    def piece_spec(p):
        def index(i, j, t):
            mine = (j >= first[p]) & (j < first[p] + tiles[p])
            return jnp.where(mine, t, 0), jnp.where(mine, j - first[p], 0)
        return pl.BlockSpec((ts, tn), index)

    operands, extra_specs, aliases = [a, *bs], [], {}
    if into is not None:
        operands.append(into)
        extra_specs = [pl.BlockSpec(memory_space=pltpu.HBM)]
        aliases = {len(operands) - 1: 0}
    return pl.pallas_call(
        body, name=name, grid=(k // tk, n // tn, nt),
        in_specs=[pl.BlockSpec((ts, tk), lambda i, j, t: (t, i))] + [piece_spec(p) for p in range(len(bs))] + extra_specs,
        out_specs=pl.BlockSpec((tk, tn), lambda i, j, t: (i, tile0 + j)),
        out_shape=_sds((k, n_total or n), BF16),
        input_output_aliases=aliases,
        scratch_shapes=[pltpu.VMEM((tk, tn), F32)],
        compiler_params=_params(("parallel", "parallel", "arbitrary")),
    )(*operands)


def _mix_out_bwd(dx, mix, proj, ycv, yat, wout, wpc, wpa, g, gt, *, name, tm=256):
    s, d = dx.shape
    cw = wpc.shape[0]
    aw = wpa.shape[0]
    tm = min(tm, s)
    ga_blk = (3 * cw + 3 * aw) // d

    def body(dx_ref, mix_ref, ga_ref, gb_ref, ycv_ref, yat_ref, wout_ref, wpc_ref, wpa_ref, g_ref, gt_ref,
             dmix_ref, dycv_ref, dyat_ref, dyc_ref, do_ref, dgate_ref, dgt_ref, dg_ref):
        @pl.when(pl.program_id(0) == 0)
        def _():
            dgt_ref[...] = jnp.zeros_like(dgt_ref)
            dg_ref[...] = jnp.zeros_like(dg_ref)
        dxv = dx_ref[...]
        dn = dxv * gt_ref[...]
        dmix, xh = _rms_bwd(dn, mix_ref[...], g_ref[...])
        dgt_ref[...] += _colsum(dxv * (xh * g_ref[...]))
        dg_ref[...] += _colsum(dn * xh)
        dmixb = dmix.astype(BF16)
        dmix_ref[...] = dmixb
        dmerged = lax.dot_general(dmixb, wout_ref[...], _NT, preferred_element_type=F32)
        sga = _sigmoid(ga_ref[...].astype(F32))
        sgb = _sigmoid(gb_ref[...].astype(F32))
        dycv = (dmerged * sga).astype(BF16)
        dyat = (dmerged * sgb).astype(BF16)
        dycv_ref[...] = dycv
        dyat_ref[...] = dyat
        dgate_ref[:, 0:d] = (dmerged * ycv_ref[...].astype(F32) * (sga * (1.0 - sga))).astype(BF16)
        dgate_ref[:, d:2 * d] = (dmerged * yat_ref[...].astype(F32) * (sgb * (1.0 - sgb))).astype(BF16)
        dyc_ref[...] = lax.dot_general(dycv, wpc_ref[...], _NT, preferred_element_type=F32).astype(BF16)
        do_ref[...] = lax.dot_general(dyat, wpa_ref[...], _NT, preferred_element_type=F32).astype(BF16)

    def rows(w):
        return pl.BlockSpec((tm, w), lambda i: (i, 0))

    def full(a):
        return pl.BlockSpec(a.shape, lambda i: (0, 0))

    return pl.pallas_call(
        body, name=name, grid=(s // tm,),
        in_specs=[rows(d), rows(d), pl.BlockSpec((tm, d), lambda i: (i, ga_blk)),
                  pl.BlockSpec((tm, d), lambda i: (i, ga_blk + 1)), rows(d), rows(d),
                  full(wout), full(wpc), full(wpa), _vec_spec(d, 1), _vec_spec(d, 1)],
        out_specs=[rows(d), rows(d), rows(d), rows(cw), rows(aw), rows(2 * d), _vec_spec(d, 1), _vec_spec(d, 1)],
        out_shape=[_sds((s, d), BF16), _sds((s, d), BF16), _sds((s, d), BF16), _sds((s, cw), BF16),
                   _sds((s, aw), BF16), _sds((s, 2 * d), BF16), _sds((1, d), F32), _sds((1, d), F32)],
        compiler_params=_params(("arbitrary",)),
    )(dx, mix, proj, proj, ycv, yat, wout, wpc, wpa, g, gt)


def _conv_bwd(dyc, proj, conv_w, *, name, tm=512):
    s = proj.shape[0]
    cw = conv_w.shape[1]
    tm = min(tm, s)
    nb = tm // HALO
    nt = s // tm
    last_blk = s // HALO - 1

    def body(dyc_ref, bg_ref, cg_ref, u_ref, cgh_ref, uh_ref, dych_ref, bgh_ref, w_ref,
             dconv_ref, dw_ref, vbuf, gbuf):
        i = pl.program_id(0)

        @pl.when(i == 0)
        def _():
            dw_ref[...] = jnp.zeros_like(dw_ref)

        cg = cg_ref[...].astype(F32)
        u = u_ref[...].astype(F32)
        vv = cg * u
        halo = cgh_ref[...].astype(F32) * uh_ref[...].astype(F32)
        vbuf[0:HALO, :] = jnp.where(i > 0, halo, 0.0)
        vbuf[HALO:HALO + tm, :] = vv
        v1 = vbuf[HALO - 1:HALO - 1 + tm, :]
        v2 = vbuf[HALO - 2:HALO - 2 + tm, :]
        w = w_ref[...]
        y = w[2:3, :] * vv + w[1:2, :] * v1 + w[0:1, :] * v2
        dyc = dyc_ref[...].astype(F32)
        dconv_ref[:, 0:cw] = (dyc * y).astype(BF16)
        gy = dyc * bg_ref[...].astype(F32)
        nxt = dych_ref[...].astype(F32) * bgh_ref[...].astype(F32)
        gbuf[0:tm, :] = gy
        gbuf[tm:tm + HALO, :] = jnp.where(i < nt - 1, nxt, 0.0)
        g1 = gbuf[1:1 + tm, :]
        g2 = gbuf[2:2 + tm, :]
        dvv = w[2:3, :] * gy + w[1:2, :] * g1 + w[0:1, :] * g2
        dconv_ref[:, cw:2 * cw] = (dvv * u).astype(BF16)
        dconv_ref[:, 2 * cw:3 * cw] = (dvv * cg).astype(BF16)
        dw_ref[0:1, :] += _colsum(gy * v2)
        dw_ref[1:2, :] += _colsum(gy * v1)
        dw_ref[2:3, :] += _colsum(gy * vv)

    def prev(i):
        return jnp.maximum(i * nb - 1, 0)

    def nxt_blk(i):
        return jnp.minimum((i + 1) * nb, last_blk)

    def col(c):
        return pl.BlockSpec((tm, cw), lambda i: (i, c))

    return pl.pallas_call(
        body, name=name, grid=(nt,),
        in_specs=[col(0), col(0), col(1), col(2),
                  pl.BlockSpec((HALO, cw), lambda i: (prev(i), 1)), pl.BlockSpec((HALO, cw), lambda i: (prev(i), 2)),
                  pl.BlockSpec((HALO, cw), lambda i: (nxt_blk(i), 0)), pl.BlockSpec((HALO, cw), lambda i: (nxt_blk(i), 0)),
                  pl.BlockSpec((3, cw), lambda i: (0, 0))],
        out_specs=[pl.BlockSpec((tm, 3 * cw), lambda i: (i, 0)), pl.BlockSpec((3, cw), lambda i: (0, 0))],
        out_shape=[_sds((s, 3 * cw), BF16), _sds((3, cw), F32)],
        scratch_shapes=[pltpu.VMEM((HALO + tm, cw), F32), pltpu.VMEM((tm + HALO, cw), F32)],
        compiler_params=_params(("arbitrary",)),
    )(dyc, proj, proj, proj, proj, proj, dyc, proj, conv_w)


def _attn_bwd(proj, o, kept, do, tri, *, d, name, side=None):
    s = proj.shape[0]
    qb = tri.shape[0]
    chains = _fit(ATTN_CHAINS[1], s // qb)
    ng = s // (qb * chains)
    q0, k0, v0, hp = _attn_cols(d)

    def body(*refs):
        p, g = pl.program_id(0), pl.program_id(1)
        own, late_phases = _host_side(
            side, 9, 3, 2, refs, (p == 0) & (g == 0), (p == hp - 1) & (g == 0), (p == hp - 1) & (g == ng - 1))
        (q_ref, k_ref, v_ref, o_ref, do_ref, tri_ref, a_ref, b_ref, run_ref,
         dq_ref, dk_ref, dv_ref, dk_acc, dv_acc) = own

        @pl.when(g == 0)
        def _():
            dk_acc[...] = jnp.zeros_like(dk_acc)
            dv_acc[...] = jnp.zeros_like(dv_acc)

        tri_m = tri_ref[...]
        masks = [_head_mask(h) for h in range(2)]

        def first_steps(u):
            i = g * chains + u
            mine = slice(u * qb, (u + 1) * qb)
            dov = do_ref[mine, :]
            qs = _stack_heads(q_ref[mine, :] * ATTN_SCALE, masks)
            dos = _stack_heads(dov, masks)
            dprod = dov.astype(F32) * o_ref[mine, :]
            dtot = jnp.concatenate([jnp.sum(jnp.where(m, dprod, 0.0), axis=-1, keepdims=True) for m in masks], axis=0)

            def through(j, ab, beta, left, grun, dq_acc):
                rows = pl.ds(pl.multiple_of(j * qb, qb), qb)
                kb = k_ref[rows, :]
                da = lax.dot_general(dos, v_ref[rows, :], _NT, preferred_element_type=F32)
                gg = ab.astype(F32) * da
                gcs, gtotal = _running_sum(gg, tri_m)
                dzb = (gg - beta * (gg + (left - gcs))).astype(BF16)
                dq_acc = dq_acc + jnp.dot(_heads_to_lanes(dzb, qb), _stack_heads(kb, masks),
                                          preferred_element_type=F32)
                dk_add = lax.dot_general(dzb, qs, _TN, preferred_element_type=F32)
                dv_add = lax.dot_general(ab, dos, _TN, preferred_element_type=F32)
                return (grun + gtotal, dq_acc), (rows, dk_add, dv_add)

            def kept(slot, j, state):
                grun, dq_acc = state
                return through(j, a_ref[0, u, slot], b_ref[0, u, slot].astype(F32), dtot - grun, grun, dq_acc)

            def strip(j, state):
                run, grun, dq_acc = state
                z = lax.dot_general(qs, k_ref[pl.ds(pl.multiple_of(j * qb, qb), qb), :], _NT, preferred_element_type=F32)
                lg = _log_one_minus_sigmoid(z)
                cs, total = _running_sum(lg, tri_m)
                ab = jnp.exp(z + cs + run).astype(BF16)
                left = jnp.where(run > UNDERFLOW_LOG, dtot - grun, 0.0)
                (grun, dq_acc), adds = through(j, ab, 1.0 - jnp.exp(lg), left, grun, dq_acc)
                return (run + total, grun, dq_acc), adds

            state, adds0 = kept(0, i, (jnp.zeros((2 * qb, 1), F32), jnp.zeros((qb, LANES), F32)))
            state, adds1 = kept(1, jnp.maximum(i - 1, 0), state)
            return i, strip, (run_ref[0, u][:, 0:1], *state), (adds0, adds1)

        started = [first_steps(u) for u in range(chains)]
        for u, (i, strip, state, adds) in enumerate(started):
            for rows, dk_add, dv_add in adds:
                dk_acc[rows, :] += dk_add
                dv_acc[rows, :] += dv_add

            def more(st, strip=strip):
                state, (rows, dk_add, dv_add) = strip(st[0], st[1:])
                dk_acc[rows, :] += dk_add
                dv_acc[rows, :] += dv_add
                return (st[0] - 1, *state)

            state = lax.while_loop(lambda st: (st[0] >= 0) & (jnp.max(st[1]) > UNDERFLOW_LOG), more, (i - 2, *state))
            dq_ref[u * qb:(u + 1) * qb, :] = (state[3] * ATTN_SCALE).astype(BF16)

        @pl.when(g == ng - 1)
        def _():
            dk_ref[...] = dk_acc[...].astype(BF16)
            dv_ref[...] = dv_acc[...].astype(BF16)

        late_phases()

    aw = hp * LANES
    tq = qb * chains
    s_in, s_out, s_shapes, aliases, s_scratch = _side_specs(side, 9, 3)
    res = pl.pallas_call(
        body, name=name, grid=(hp, ng),
        in_specs=[pl.BlockSpec((tq, LANES), lambda p, i: (i, q0 + p)),
                  pl.BlockSpec((s, LANES), lambda p, i: (0, k0 + p)),
                  pl.BlockSpec((s, LANES), lambda p, i: (0, v0 + p)),
                  pl.BlockSpec((tq, LANES), lambda p, i: (i, p)),
                  pl.BlockSpec((tq, LANES), lambda p, i: (i, p)),
                  pl.BlockSpec((qb, qb), lambda p, i: (0, 0)),
                  pl.BlockSpec((1, chains, 2, 2 * qb, qb), lambda p, i: (p, i, 0, 0, 0)),
                  pl.BlockSpec((1, chains, 2, 2 * qb, qb), lambda p, i: (p, i, 0, 0, 0)),
                  pl.BlockSpec((1, chains, 2 * qb, LANES), lambda p, i: (p, i, 0, 0))] + s_in,
        out_specs=[pl.BlockSpec((tq, LANES), lambda p, i: (i, p)),
                   pl.BlockSpec((s, LANES), lambda p, i: (0, p)),
                   pl.BlockSpec((s, LANES), lambda p, i: (0, p))] + s_out,
        out_shape=[_sds((s, aw), BF16), _sds((s, aw), BF16), _sds((s, aw), BF16)] + s_shapes,
        input_output_aliases=aliases,
        scratch_shapes=[pltpu.VMEM((s, LANES), F32), pltpu.VMEM((s, LANES), F32)] + s_scratch,
        compiler_params=_params(("arbitrary", "arbitrary")),
    )(proj, proj, proj, o, do, tri, *kept, *([] if side is None else side.operands))
    return res[:3], res[3:]


def _hosted(hooks, kind, l, fn, *args, **kw):
    res, side_out = fn(*args, side=hooks.side(kind, l), **kw)
    hooks.done(kind, l, side_out)
    return res


def _layer_fwd(x, mod, gains, conv_w, tri, *, l, hooks):
    sh1, sc1, gt1, sh2, sc2, gt2 = mod
    g_pre_mix, g_post_mix, g_pre_mlp, g_post_mlp = gains
    d = x.shape[1]
    w = functools.partial(hooks.weight, l)
    h, proj = _hosted(hooks, "in_proj", l, _norm_mod_matmul, x, g_pre_mix, sc1, sh1, w("w_in"), name=f"in_proj_{l}")
    yc = _conv_fwd(proj, conv_w, name=f"conv_fwd_{l}")
    o, kept = _hosted(hooks, "attn_fwd", l, _attn_fwd, proj, tri, d=d, name=f"attn_fwd_{l}")
    ycv, yat, merged, mix, x1 = _mix_out(yc, o, proj, x, w("w_proj_conv"), w("w_proj_attn"), w("w_out"),
                                         g_post_mix, gt1, name=f"mix_out_{l}")
    (h2, a), _ = _norm_mod_matmul(x1, g_pre_mlp, sc2, sh2, w("w_mlp_in"), name=f"mlp_in_{l}")
    ff, x2 = _mlp_out(a, x1, w("w_mlp_out"), g_post_mlp, gt2, name=f"mlp_out_{l}")
    saved = dict(x=x, h=h, proj=proj, yc=yc, o=o, kept=kept, ycv=ycv, yat=yat, merged=merged, mix=mix, x1=x1, h2=h2, a=a, ff=ff,
                 conv_w=conv_w, **{k: w(k) for k in BIG})
    return x2, saved


def _layer_bwd(dx2, sv, mod, gains, tri, *, l, hooks):
    sh1, sc1, gt1, sh2, sc2, gt2 = mod
    g_pre_mix, g_post_mix, g_pre_mlp, g_post_mlp = gains
    d = dx2.shape[1]
    dff, da, dgt2, dg_post_mlp = _mlp_out_bwd(dx2, sv["ff"], sv["a"], sv["w_mlp_out"], g_post_mlp, gt2,
                                              name=f"mlp_out_bwd_{l}")
    hooks.grad(l, "w_mlp_out", _matmul_tn(sv["a"], [dff], relu2=True, name=f"gw_mlp_out_{l}"))
    (dx1, dsh2, dsc2, dg_pre_mlp), _ = _matmul_nt_norm_bwd([da], sv["w_mlp_in"], sv["x1"], dx2, g_pre_mlp, sc2,
                                                           name=f"mlp_in_bwd_{l}")
    hooks.grad(l, "w_mlp_in", _matmul_tn(sv["h2"], [da], name=f"gw_mlp_in_{l}"))
    dmix, dycv, dyat, dyc, do, dgate, dgt1, dg_post_mix = _mix_out_bwd(
        dx1, sv["mix"], sv["proj"], sv["ycv"], sv["yat"], sv["w_out"], sv["w_proj_conv"], sv["w_proj_attn"],
        g_post_mix, gt1, name=f"mix_out_bwd_{l}")
    hooks.grad(l, "w_out", _matmul_tn(sv["merged"], [dmix], name=f"gw_out_{l}"))
    hooks.grad(l, "w_proj_conv", _matmul_tn(sv["yc"], [dycv], name=f"gw_proj_conv_{l}"))
    hooks.grad(l, "w_proj_attn", _matmul_tn(sv["o"], [dyat], name=f"gw_proj_attn_{l}"))
    dconv, g_conv_w = _conv_bwd(dyc, sv["proj"], sv["conv_w"], name=f"conv_bwd_{l}")
    dq, dk, dv = _hosted(hooks, "attn_bwd", l, _attn_bwd, sv["proj"], sv["o"], sv["kept"], do, tri, d=d,
                         name=f"attn_bwd_{l}")
    dproj = [dconv, dq, dk, dv, dgate]
    n_in = sv["w_in"].shape[1]
    gw_in = _matmul_tn(sv["h"], [dconv], tn=768, n_total=n_in, name=f"gw_in_conv_{l}")
    gw_in = _matmul_tn(sv["h"], [dq, dk, dv], into=gw_in, col0=dconv.shape[1], n_total=n_in, name=f"gw_in_attn_{l}")
    gw_in = _matmul_tn(sv["h"], [dgate], into=gw_in, col0=n_in - dgate.shape[1], n_total=n_in, name=f"gw_in_gate_{l}")
    hooks.grad(l, "w_in", gw_in)
    dx0, dsh1, dsc1, dg_pre_mix = _hosted(hooks, "in_proj_bwd", l, _matmul_nt_norm_bwd, dproj, sv["w_in"], sv["x"], dx1,
                                          g_pre_mix, sc1, name=f"in_proj_bwd_{l}")
    dmod = jnp.concatenate([dsh1, dsc1, dgt1, dsh2, dsc2, dgt2], axis=0)
    dgains = jnp.concatenate([dg_pre_mix, dg_post_mix, dg_pre_mlp, dg_post_mlp], axis=0)
    return dx0, g_conv_w, dmod, dgains


BIG = ("w_in", "w_proj_conv", "w_proj_attn", "w_out", "w_mlp_in", "w_mlp_out")
SHARD_AXIS = dict(w_in=1, w_proj_conv=1, w_proj_attn=1, w_out=0, w_mlp_in=1, w_mlp_out=0)


class _LocalWeights:
    def __init__(self, wlayers):
        self.wlayers = wlayers
        self.grads = {}

    def weight(self, l, name):
        return self.wlayers[l][name]

    def side(self, kind, l):
        return None

    def done(self, kind, l, outs):
        pass

    def grad(self, l, name, g):
        self.grads[(l, name)] = g


def _local_step(x, target, mods, gains, conv_w, hooks):
    depth = mods.shape[0]
    tri = _tri(ATTN_BLOCK)
    saved = []
    for l in range(depth):
        mod = [mods[l, k:k + 1] for k in range(N_MOD)]
        gl = [gains[l, k:k + 1] for k in range(4)]
        x, sv = _layer_fwd(x, mod, gl, conv_w[l], tri, l=l, hooks=hooks)
        saved.append((sv, mod, gl))
    dx, loss = _loss_grad(x, target, name="loss_grad")
    dconv, dmods, dgains = [None] * depth, [None] * depth, [None] * depth
    for l in reversed(range(depth)):
        sv, mod, gl = saved[l]
        dx, dconv[l], dmods[l], dgains[l] = _layer_bwd(dx, sv, mod, gl, tri, l=l, hooks=hooks)
    return loss, dx, jnp.stack(dconv), jnp.stack(dmods), jnp.stack(dgains)


def _coords():
    return lax.axis_index("x"), lax.axis_index("y"), lax.axis_index("c")


def _flip(v, f):
    return 1 - v if f else v


def _all_gather_small(v, *, name):
    r, c_ = v.shape

    def body(v_ref, out_ref, send_sems, recv_sems, local_sem):
        x, y, c = _coords()
        me = 4 * x + 2 * y + c
        mine = pltpu.make_async_copy(v_ref, out_ref.at[me], local_sem)
        mine.start()
        copies = []
        for k in range(1, 8):
            fx, fy, fc = (k >> 2) & 1, (k >> 1) & 1, k & 1
            px, py, pc = _flip(x, fx), _flip(y, fy), _flip(c, fc)
            out = pltpu.make_async_remote_copy(src_ref=v_ref, dst_ref=out_ref.at[me], send_sem=send_sems.at[k - 1],
                                               recv_sem=recv_sems.at[k - 1], device_id=(px, py, pc), device_id_type=MESH)
            out.start()
            back = pltpu.make_async_remote_copy(src_ref=v_ref, dst_ref=out_ref.at[4 * px + 2 * py + pc],
                                                send_sem=send_sems.at[k - 1], recv_sem=recv_sems.at[k - 1],
                                                device_id=(px, py, pc), device_id_type=MESH)
            copies.append((out, back))
        for out, back in copies:
            back.wait_recv()
        for out, back in copies:
            out.wait_send()
        mine.wait()

    return pl.pallas_call(
        body, name=name,
        in_specs=[pl.BlockSpec(memory_space=pltpu.VMEM)],
        out_specs=pl.BlockSpec(memory_space=pltpu.VMEM),
        out_shape=_sds((8, r, c_), F32),
        scratch_shapes=[pltpu.SemaphoreType.DMA((7,)), pltpu.SemaphoreType.DMA((7,)), pltpu.SemaphoreType.DMA],
    )(v)


def _shard_dims(full_shape, axis):
    k, n = full_shape
    return (k // 4, n) if axis == 0 else (k, n // 4)


def _shard_window(ref, axis, chip, half, rows, cols):
    r0, rn = (0, rows) if half is None else (half * (rows // 2), rows // 2)
    if axis == 1:
        return ref.at[pl.ds(r0, rn), pl.ds(chip * cols, cols)]
    return ref.at[pl.ds(chip * rows + r0, rn), :]


def _cast_place(w, layer, axis, chip_arr, *, name, tr=256):
    _, rows, cols = w.shape
    tr = _fit(tr, rows)
    nb = rows // tr
    full = (rows * 4, cols) if axis == 0 else (rows, cols * 4)

    def body(chip_ref, w_ref, o_ref):
        o_ref[...] = w_ref[0].astype(BF16)

    if axis == 1:
        out_map = lambda i, chip: (i, chip[0])
    else:
        out_map = lambda i, chip: (chip[0] * nb + i, 0)
    grid_spec = pltpu.PrefetchScalarGridSpec(
        num_scalar_prefetch=1, grid=(nb,),
        in_specs=[pl.BlockSpec((1, tr, cols), lambda i, chip: (layer, i, 0))],
        out_specs=pl.BlockSpec((tr, cols), out_map))
    return pl.pallas_call(body, name=name, grid_spec=grid_spec, out_shape=_sds(full, BF16),
                          compiler_params=_params(("arbitrary",)))(chip_arr, w)


def _gather_side(fulls, axes):
    n = len(fulls)

    def copies(outs, sems):
        send_sems, recv_sems = sems
        x, y, c = _coords()
        chip = 2 * x + y
        sibling = (x, y, 1 - c)
        table = []
        for w in range(n):
            rows, cols = _shard_dims(outs[w].shape, axes[w])
            win = functools.partial(_shard_window, outs[w], axes[w], rows=rows, cols=cols)
            for j, (fx, fy) in enumerate(OTHER_CHIPS):
                px, py = _flip(x, fx), _flip(y, fy)
                pchip = 2 * px + py

                def copy(piece, sem, to):
                    return pltpu.make_async_remote_copy(src_ref=piece, dst_ref=piece, send_sem=send_sems.at[w, sem],
                                                        recv_sem=recv_sems.at[w, sem], device_id=to, device_id_type=MESH)

                table.append((copy(win(chip, c), j, (px, py, c)), copy(win(pchip, c), j, (px, py, c)),
                              copy(win(pchip, c), 3 + j, sibling), copy(win(pchip, 1 - c), 3 + j, sibling)))
        return table

    def start(ins, outs, sems):
        for send, _, _, _ in copies(outs, sems):
            send.start()

    def mid(ins, outs, sems):
        for _, landed, pass_on, _ in copies(outs, sems):
            landed.wait_recv()
            pass_on.start()

    def finish(ins, outs, sems):
        table = copies(outs, sems)
        for _, _, _, from_sibling in table:
            from_sibling.wait_recv()
        for send, _, pass_on, _ in table:
            send.wait_send()
            pass_on.wait_send()

    return _Side(fulls, [_sds(f.shape, f.dtype) for f in fulls], {w: w for w in range(n)},
                 [pltpu.SemaphoreType.DMA((n, 6)), pltpu.SemaphoreType.DMA((n, 6))], start, mid, finish)


def _exchange_side(grads, axes):
    n = len(grads)
    out_shapes = []
    for g, ax in zip(grads, axes):
        rows, cols = _shard_dims(g.shape, ax)
        out_shapes.append(_sds((7, rows // 2, cols), g.dtype))

    def copies(ins, outs, sems):
        send_sems, recv_sems = sems
        x, y, c = _coords()
        table = []
        for w in range(n):
            rows, cols = _shard_dims(ins[w].shape, axes[w])
            for k in range(1, 8):
                fx, fy, fc = (k >> 2) & 1, (k >> 1) & 1, k & 1
                px, py, pc = _flip(x, fx), _flip(y, fy), _flip(c, fc)
                piece = _shard_window(ins[w], axes[w], 2 * px + py, pc, rows, cols)
                table.append(pltpu.make_async_remote_copy(
                    src_ref=piece, dst_ref=outs[w].at[k - 1], send_sem=send_sems.at[w, k - 1],
                    recv_sem=recv_sems.at[w, k - 1], device_id=(px, py, pc), device_id_type=MESH))
        return table

    def start(ins, outs, sems):
        for cp in copies(ins, outs, sems):
            cp.start()

    def finish(ins, outs, sems):
        table = copies(ins, outs, sems)
        for cp in table:
            cp.wait_recv()
        for cp in table:
            cp.wait_send()

    return _Side(grads, out_shapes, {}, [pltpu.SemaphoreType.DMA((n, 7)), pltpu.SemaphoreType.DMA((n, 7))],
                 start, None, finish)


def _rs_sum_join(g, got, out_prev, layer, depth, axis, ids, *, name, tr=256):
    _, rows2, cols = got.shape
    tr = _fit(tr, rows2)
    nt = rows2 // tr
    if axis == 1:
        own_map = lambda i, ids_: (ids_[1] * nt + i, ids_[0])
    else:
        own_map = lambda i, ids_: ((ids_[0] * 2 + ids_[1]) * nt + i, 0)

    def body(ids_ref, g_ref, got_ref, *rest):
        out_ref, buf, local_sems, send_sems, recv_sem = rest[-5:]
        i = pl.program_id(0)
        x, y, c = _coords()
        sibling = (x, y, 1 - c)

        def copies(step, slot):
            rows_mine = pl.ds(c * rows2 + step * tr, tr)
            dst = out_ref.at[layer, rows_mine, :]
            keep = pltpu.make_async_copy(buf.at[slot], dst, local_sems.at[slot])
            give = pltpu.make_async_remote_copy(src_ref=buf.at[slot], dst_ref=dst, send_sem=send_sems.at[slot],
                                                recv_sem=recv_sem, device_id=sibling, device_id_type=MESH)
            return keep, give

        def drain(step, slot):
            keep, give = copies(step, slot)
            keep.wait()
            give.wait_send()

        slot = i % 2

        @pl.when(i >= 2)
        def _():
            drain(i - 2, slot)

        acc = g_ref[...].astype(F32)
        for k in range(7):
            acc = acc + got_ref[k].astype(F32)
        buf[slot] = acc
        keep, give = copies(i, slot)
        keep.start()
        give.start()

        @pl.when(i == nt - 1)
        def _():
            if nt >= 2:
                drain(nt - 2, (nt - 2) % 2)
            drain(nt - 1, (nt - 1) % 2)
            theirs = out_ref.at[layer, pl.ds((1 - c) * rows2, rows2), :]
            pltpu.make_async_remote_copy(src_ref=theirs, dst_ref=theirs, send_sem=send_sems.at[0], recv_sem=recv_sem,
                                         device_id=sibling, device_id_type=MESH).wait_recv()

    hbm = pl.BlockSpec(memory_space=pltpu.HBM)
    in_specs = [pl.BlockSpec((tr, cols), own_map), pl.BlockSpec((7, tr, cols), lambda i, ids_: (0, i, 0))]
    operands = [ids, g, got]
    aliases = {}
    if out_prev is not None:
        in_specs.append(hbm)
        operands.append(out_prev)
        aliases = {3: 0}
    grid_spec = pltpu.PrefetchScalarGridSpec(
        num_scalar_prefetch=1, grid=(nt,), in_specs=in_specs, out_specs=hbm,
        scratch_shapes=[pltpu.VMEM((2, tr, cols), F32), pltpu.SemaphoreType.DMA((2,)), pltpu.SemaphoreType.DMA((2,)),
                        pltpu.SemaphoreType.DMA])
    return pl.pallas_call(body, name=name, grid_spec=grid_spec, out_shape=_sds((depth, 2 * rows2, cols), F32),
                          input_output_aliases=aliases, compiler_params=_params(("arbitrary",)))(*operands)


MIX = ("w_proj_conv", "w_proj_attn", "w_out")


class _Schedule:
    def __init__(self, placed, depth, ids):
        self.placed, self.depth, self.ids = placed, depth, ids
        self.full, self.g, self.carried = {}, {}, None
        self.reduced = {k: None for k in BIG}
        first = [(0, "w_in")]
        self._landed(first, _side_call(self._gather(first), name="gather_w_in_0"))

    def _gather(self, keys):
        return _gather_side([self.placed[k] for k in keys], [SHARD_AXIS[k[1]] for k in keys])

    def _landed(self, keys, outs):
        for k, o in zip(keys, outs):
            self.full[k] = o

    def _exchange(self, keys):
        return _exchange_side([self.g[k] for k in keys], [SHARD_AXIS[k[1]] for k in keys])

    def _reduce(self, keys, got):
        for (l, name), pieces in zip(keys, got):
            self.reduced[name] = _rs_sum_join(self.g[(l, name)], pieces, self.reduced[name], l, self.depth,
                                              SHARD_AXIS[name], self.ids, name=f"rs_sum_join_{l}_{name}")

    def weight(self, l, name):
        return self.full[(l, name)]

    def grad(self, l, name, g):
        self.g[(l, name)] = g

    def side(self, kind, l):
        nxt = [(l + 1, "w_in")] if l + 1 < self.depth else []
        if kind == "in_proj":
            keys, make = [(l, k) for k in MIX + ("w_mlp_in",)], self._gather
        elif kind == "attn_fwd":
            keys, make = [(l, "w_mlp_out")] + nxt, self._gather
        elif kind == "attn_bwd":
            keys, make = [(l, k) for k in ("w_mlp_out", "w_mlp_in") + MIX], self._exchange
        else:
            keys, make = [(l, "w_in")], self._exchange
        self.carried = keys
        return make(keys)

    def done(self, kind, l, outs):
        (self._landed if kind in ("in_proj", "attn_fwd") else self._reduce)(self.carried, outs)


def _flat_rows(shape):
    rows = 1
    for s in shape[:-1]:
        rows *= s
    return rows, shape[-1]


def _row_tile(rows, cols, cap_bytes=2 * 1024 * 1024):
    t = rows
    while t * cols * 4 > cap_bytes and t % 16 == 0:
        t //= 2
    return t


def _ada_fwd(c_all, w_ada, b_loc, *, name, tn=512):
    l, d, nl = w_ada.shape
    b = c_all.shape[0]
    tn = min(tn, nl)

    def body(c_ref, w_ref, b_ref, o_ref):
        o_ref[0] = jnp.dot(c_ref[...], w_ref[0], preferred_element_type=F32,
                           precision=lax.Precision.HIGHEST) + b_ref[0]

    return pl.pallas_call(
        body, name=name, grid=(l, nl // tn),
        in_specs=[pl.BlockSpec((b, d), lambda i, j: (0, 0)), pl.BlockSpec((1, d, tn), lambda i, j: (i, 0, j)),
                  pl.BlockSpec((1, 1, tn), lambda i, j: (i, 0, j))],
        out_specs=pl.BlockSpec((1, b, tn), lambda i, j: (i, 0, j)),
        out_shape=_sds((l, b, nl), F32),
        compiler_params=_params(("parallel", "parallel")),
    )(c_all, w_ada, b_loc)


def _ada_bwd(c_t, dmod_loc, *, name, tn=512):
    d, b = c_t.shape
    l, _, nl = dmod_loc.shape
    tn = min(tn, nl)

    def body(c_ref, dm_ref, o_ref):
        cv = c_ref[...]
        dm = dm_ref[0]
        acc = cv[:, 0:1] * dm[0:1, :]
        for k in range(1, b):
            acc = acc + cv[:, k:k + 1] * dm[k:k + 1, :]
        o_ref[0] = acc

    return pl.pallas_call(
        body, name=name, grid=(l, nl // tn),
        in_specs=[pl.BlockSpec((d, b), lambda i, j: (0, 0)), pl.BlockSpec((1, b, tn), lambda i, j: (i, 0, j))],
        out_specs=pl.BlockSpec((1, d, tn), lambda i, j: (i, 0, j)),
        out_shape=_sds((l, d, nl), F32),
        compiler_params=_params(("parallel", "parallel")),
    )(c_t, dmod_loc)


def _sum_devices(p, *, name):
    k, r, c_ = p.shape

    def body(p_ref, o_ref):
        acc = p_ref[0]
        for j in range(1, k):
            acc = acc + p_ref[j]
        o_ref[...] = acc

    return pl.pallas_call(body, name=name, out_shape=_sds((r, c_), F32),
                          in_specs=[pl.BlockSpec(memory_space=pltpu.VMEM)],
                          out_specs=pl.BlockSpec(memory_space=pltpu.VMEM))(p)


def _adamw(w, g, m, v, *, name):
    shape = w.shape
    rows, cols = _flat_rows(shape)
    tr = _row_tile(rows, cols, cap_bytes=1024 * 1024)
    c1 = 1.0 / (1.0 - ADAM_B1 ** ADAM_STEP)
    c2 = 1.0 / (1.0 - ADAM_B2 ** ADAM_STEP)

    def body(w_ref, g_ref, m_ref, v_ref, d_ref, nm_ref, nv_ref):
        gv = g_ref[...]
        nm = ADAM_B1 * m_ref[...] + (1.0 - ADAM_B1) * gv
        nv = ADAM_B2 * v_ref[...] + (1.0 - ADAM_B2) * (gv * gv)
        m_hat = nm * c1
        v_hat = nv * c2
        d_ref[...] = -ADAM_LR * (m_hat / (jnp.sqrt(v_hat) + ADAM_EPS) + ADAM_WD * w_ref[...])
        nm_ref[...] = nm
        nv_ref[...] = nv

    spec = pl.BlockSpec((tr, cols), lambda i: (i, 0))
    flat = lambda a: a.reshape(rows, cols)
    outs = pl.pallas_call(body, name=name, grid=(rows // tr,), in_specs=[spec] * 4, out_specs=[spec] * 3,
                          out_shape=[_sds((rows, cols), F32)] * 3, compiler_params=_params(("parallel",)),
                          )(flat(w), flat(g), flat(m), flat(v))
    return tuple(o.reshape(shape) for o in outs)


WEIGHTS = ("w_ada", "b_ada", "g_pre_mix", "g_post_mix", "g_pre_mlp", "g_post_mlp", "w_in", "conv_w",
           "w_proj_conv", "w_proj_attn", "w_out", "w_mlp_in", "w_mlp_out")
GAINS = ("g_pre_mix", "g_post_mix", "g_pre_mlp", "g_post_mlp")


def kernel(x, c, w_ada, b_ada, g_pre_mix, g_post_mix, g_pre_mlp, g_post_mlp, w_in, conv_w, w_proj_conv, w_proj_attn, w_out, w_mlp_in, w_mlp_out, loss_target, m_w_ada, m_b_ada, m_g_pre_mix, m_g_post_mix, m_g_pre_mlp, m_g_post_mlp, m_w_in, m_conv_w, m_w_proj_conv, m_w_proj_attn, m_w_out, m_w_mlp_in, m_w_mlp_out, v_w_ada, v_b_ada, v_g_pre_mix, v_g_post_mix, v_g_pre_mlp, v_g_post_mlp, v_w_in, v_conv_w, v_w_proj_conv, v_w_proj_attn, v_w_out, v_w_mlp_in, v_w_mlp_out):
    params = dict(w_ada=w_ada, b_ada=b_ada, g_pre_mix=g_pre_mix, g_post_mix=g_post_mix, g_pre_mlp=g_pre_mlp,
                  g_post_mlp=g_post_mlp, w_in=w_in, conv_w=conv_w, w_proj_conv=w_proj_conv, w_proj_attn=w_proj_attn,
                  w_out=w_out, w_mlp_in=w_mlp_in, w_mlp_out=w_mlp_out)
    m_in = dict(w_ada=m_w_ada, b_ada=m_b_ada, g_pre_mix=m_g_pre_mix, g_post_mix=m_g_post_mix, g_pre_mlp=m_g_pre_mlp,
                g_post_mlp=m_g_post_mlp, w_in=m_w_in, conv_w=m_conv_w, w_proj_conv=m_w_proj_conv,
                w_proj_attn=m_w_proj_attn, w_out=m_w_out, w_mlp_in=m_w_mlp_in, w_mlp_out=m_w_mlp_out)
    v_in = dict(w_ada=v_w_ada, b_ada=v_b_ada, g_pre_mix=v_g_pre_mix, g_post_mix=v_g_post_mix, g_pre_mlp=v_g_pre_mlp,
                g_post_mlp=v_g_post_mlp, w_in=v_w_in, conv_w=v_conv_w, w_proj_conv=v_w_proj_conv,
                w_proj_attn=v_w_proj_attn, w_out=v_w_out, w_mlp_in=v_w_mlp_in, w_mlp_out=v_w_mlp_out)

    depth, d, nl_ada = w_ada.shape
    ix, iy, ic = lax.axis_index("x"), lax.axis_index("y"), lax.axis_index("c")
    chip = 2 * ix + iy
    me = 4 * ix + 2 * iy + ic
    xs = x[0]
    target = loss_target[0]

    c_all = _all_gather_small(jnp.broadcast_to(c, (8, d)), name="gather_c")[:, 0, :]
    b_loc = lax.dynamic_slice_in_dim(b_ada, chip * nl_ada, nl_ada, axis=1)[:, None, :]
    mod_loc = _ada_fwd(c_all, w_ada, b_loc, name="ada_fwd")
    mod_all = _all_gather_small(mod_loc.reshape(depth * 8, nl_ada), name="gather_mod")
    mod_all = mod_all.reshape(4, 2, depth, 8, nl_ada)[:, 0]
    mod_me = lax.dynamic_index_in_dim(mod_all, me, axis=2, keepdims=False)
    mods = jnp.transpose(mod_me, (1, 0, 2)).reshape(depth, N_MOD, d)

    chip_arr = jnp.reshape(chip, (1,)).astype(jnp.int32)
    ids = jnp.stack([chip, ic]).astype(jnp.int32)
    placed = {(l, k): _cast_place(params[k], l, SHARD_AXIS[k], chip_arr, name=f"place_{k}_{l}")
              for l in range(depth) for k in BIG}
    conv_full = _all_gather_small(
        jnp.pad(conv_w.reshape(depth * 3, -1), ((0, 8 - depth * 3), (0, 0))), name="gather_conv_w")
    conv_full = conv_full.reshape(4, 2, 8, -1)[:, 0, :depth * 3]
    conv_full = jnp.transpose(conv_full, (1, 0, 2)).reshape(depth, 3, -1)

    gains = jnp.stack([params[k] for k in GAINS], axis=1)
    schedule = _Schedule(placed, depth, ids)
    loss, dx, conv_grads, dmods, dgains = _local_step(xs, target, mods, gains, conv_full, schedule)

    cw = conv_full.shape[2]
    rows = [dmods.reshape(depth * N_MOD, d), dgains.reshape(depth * 4, d),
            conv_grads.reshape(-1, d), jnp.broadcast_to(loss, (1, d))]
    payload = jnp.concatenate(rows, axis=0)
    n_rows = payload.shape[0]
    pad = (-n_rows) % 8
    payload = jnp.pad(payload, ((0, pad), (0, 0)))
    everyone = _all_gather_small(payload, name="gather_small_grads")
    total = _sum_devices(everyone, name="sum_small_grads")
    r0 = depth * N_MOD
    grads = {}
    grads["b_ada"] = total[:r0].reshape(depth, N_MOD * d)
    gsum = total[r0:r0 + depth * 4].reshape(depth, 4, d)
    for k, name in enumerate(GAINS):
        grads[name] = gsum[:, k]
    r1 = r0 + depth * 4
    n_conv = (depth * 3 * cw) // d
    conv_g = total[r1:r1 + n_conv].reshape(depth, 3, cw)
    grads["conv_w"] = lax.dynamic_slice_in_dim(conv_g, chip * (cw // 4), cw // 4, axis=2)
    loss_out = total[r1 + n_conv, 0]
    dmod_all = everyone[:, :r0].reshape(8, depth, N_MOD * d)
    dmod_loc = lax.dynamic_slice_in_dim(dmod_all, chip * nl_ada, nl_ada, axis=2)
    grads["w_ada"] = _ada_bwd(c_all.T, jnp.transpose(dmod_loc, (1, 0, 2)), name="ada_bwd")

    for k in BIG:
        grads[k] = schedule.reduced[k]

    deltas, new_m, new_v = {}, {}, {}
    for k in WEIGHTS:
        deltas[k], new_m[k], new_v[k] = _adamw(params[k], grads[k], m_in[k], v_in[k], name=f"adamw_{k}")

    return (loss_out, dx[None], *[grads[k] for k in WEIGHTS], *[deltas[k] for k in WEIGHTS],
            *[new_m[k] for k in WEIGHTS], *[new_v[k] for k in WEIGHTS])
```

```python
import functools

import jax
import jax.numpy as jnp
from jax import lax
from jax.experimental import pallas as pl
from jax.experimental.pallas import tpu as pltpu

F32 = jnp.float32
BF16 = jnp.bfloat16
EPS = 1e-6
N_MOD = 6
HEAD_DIM = 64
LANES = 128
ATTN_SCALE = 1.0 / 8.0
UNDERFLOW_LOG = -90.0
ATTN_BLOCK = 256
ATTN_CHAINS = (4, 4)
ADAM_LR = 0.001
ADAM_B1 = 0.9
ADAM_B2 = 0.999
ADAM_EPS = 1e-08
ADAM_WD = 0.01
ADAM_STEP = 10
VMEM_LIMIT = 56 * 1024 * 1024
MESH = pl.DeviceIdType.MESH
OTHER_CHIPS = ((1, 0), (0, 1), (1, 1))

_NT = (((1,), (1,)), ((), ()))
_TN = (((0,), (0,)), ((), ()))


def _sds(shape, dtype):
    return jax.ShapeDtypeStruct(shape, dtype)


def _params(sem):
    return pltpu.CompilerParams(dimension_semantics=sem, vmem_limit_bytes=VMEM_LIMIT)


def _fit(t, n):
    t = min(t, n)
    while n % t:
        t //= 2
    return t


def _vec_spec(d, nargs):
    if nargs == 1:
        return pl.BlockSpec((1, d), lambda i: (0, 0))
    return pl.BlockSpec((1, d), lambda i, j: (0, 0))


def _log_one_minus_sigmoid(z):
    return -jnp.log(1.0 + jnp.exp(-jnp.abs(z))) - jnp.maximum(z, 0.0)


def _sigmoid(z):
    return 0.5 * jnp.tanh(0.5 * z) + 0.5


def _split_bf16(a):
    hi = a.astype(BF16)
    lo = (a - hi.astype(F32)).astype(BF16)
    return hi, lo


def _rms_bwd(dn, xin, g):
    r = lax.rsqrt(jnp.mean(xin * xin, axis=-1, keepdims=True) + EPS)
    xh = xin * r
    dxh = dn * g
    dxin = r * (dxh - xh * jnp.mean(dxh * xh, axis=-1, keepdims=True))
    return dxin, xh


def _colsum(a):
    return jnp.sum(a, axis=0, keepdims=True)


def _norm_mod_matmul(x, g, sc, sh, w, *, name, tm=512, side=None):
    s, d = x.shape
    n = w.shape[1]
    tm = _fit(tm, s)
    nt = s // tm

    def body(*refs):
        i = pl.program_id(0)
        (x_ref, g_ref, sc_ref, sh_ref, w_ref, h_ref, o_ref), late_phases = _host_side(
            side, 5, 2, 0, refs, i == 0, i == (3 * nt) // 4, i == nt - 1)
        xv = x_ref[...]
        r = lax.rsqrt(jnp.mean(xv * xv, axis=-1, keepdims=True) + EPS)
        h = ((xv * r * g_ref[...]) * (1.0 + sc_ref[...]) + sh_ref[...]).astype(BF16)
        h_ref[...] = h
        o_ref[...] = jnp.dot(h, w_ref[...], preferred_element_type=F32).astype(BF16)
        late_phases()

    s_in, s_out, s_shapes, aliases, s_scratch = _side_specs(side, 5, 2)
    res = pl.pallas_call(
        body, name=name, grid=(nt,),
        in_specs=[pl.BlockSpec((tm, d), lambda i: (i, 0)), _vec_spec(d, 1), _vec_spec(d, 1), _vec_spec(d, 1),
                  pl.BlockSpec((d, n), lambda i: (0, 0))] + s_in,
        out_specs=[pl.BlockSpec((tm, d), lambda i: (i, 0)), pl.BlockSpec((tm, n), lambda i: (i, 0))] + s_out,
        out_shape=[_sds((s, d), BF16), _sds((s, n), BF16)] + s_shapes,
        input_output_aliases=aliases, scratch_shapes=s_scratch,
        compiler_params=_params(("arbitrary",)),
    )(x, g, sc, sh, w, *([] if side is None else side.operands))
    return res[:2], res[2:]


HALO = 16


def _conv_fwd(proj, conv_w, *, name, tm=512):
    s = proj.shape[0]
    cw = conv_w.shape[1]
    tm = min(tm, s)
    nb = tm // HALO

    def body(bg_ref, cg_ref, u_ref, cgh_ref, uh_ref, w_ref, yc_ref, vbuf):
        i = pl.program_id(0)
        vv = cg_ref[...].astype(F32) * u_ref[...].astype(F32)
        halo = cgh_ref[...].astype(F32) * uh_ref[...].astype(F32)
        vbuf[0:HALO, :] = jnp.where(i > 0, halo, 0.0)
        vbuf[HALO:HALO + tm, :] = vv
        v1 = vbuf[HALO - 1:HALO - 1 + tm, :]
        v2 = vbuf[HALO - 2:HALO - 2 + tm, :]
        w = w_ref[...]
        y = w[2:3, :] * vv + w[1:2, :] * v1 + w[0:1, :] * v2
        yc_ref[...] = (bg_ref[...].astype(F32) * y).astype(BF16)

    def prev(i):
        return jnp.maximum(i * nb - 1, 0)

    return pl.pallas_call(
        body, name=name, grid=(s // tm,),
        in_specs=[pl.BlockSpec((tm, cw), lambda i: (i, 0)), pl.BlockSpec((tm, cw), lambda i: (i, 1)),
                  pl.BlockSpec((tm, cw), lambda i: (i, 2)),
                  pl.BlockSpec((HALO, cw), lambda i: (prev(i), 1)), pl.BlockSpec((HALO, cw), lambda i: (prev(i), 2)),
                  pl.BlockSpec((3, cw), lambda i: (0, 0))],
        out_specs=pl.BlockSpec((tm, cw), lambda i: (i, 0)),
        out_shape=_sds((s, cw), BF16),
        scratch_shapes=[pltpu.VMEM((HALO + tm, cw), F32)],
        compiler_params=_params(("arbitrary",)),
    )(proj, proj, proj, proj, proj, conv_w)


def _tri(qb):
    r = lax.broadcasted_iota(jnp.int32, (qb, qb), 0)
    c = lax.broadcasted_iota(jnp.int32, (qb, qb), 1)
    return (r >= c).astype(BF16)


def _head_mask(h):
    lane = lax.broadcasted_iota(jnp.int32, (1, LANES), 1)
    return (lane >= HEAD_DIM * h) & (lane < HEAD_DIM * (h + 1))


def _stack_heads(a, masks):
    return jnp.concatenate([jnp.where(m, a, 0).astype(BF16) for m in masks], axis=0)


def _heads_to_lanes(a, qb):
    return jnp.concatenate([a[:qb], a[qb:]], axis=1)


def _stacked_causal(qb, width, first_key, first_query):
    row = lax.broadcasted_iota(jnp.int32, (2 * qb, width), 0)
    col = lax.broadcasted_iota(jnp.int32, (2 * qb, width), 1)
    return first_key + col < first_query + jnp.where(row >= qb, row - qb, row)


def _running_sum(a, tri_m):
    rows, qb = a.shape[0], tri_m.shape[0]
    n = a.shape[1] // qb
    hi, lo = _split_bf16(a)
    stacked = jnp.concatenate([p[:, s * qb:(s + 1) * qb] for s in range(n) for p in (hi, lo)], axis=0)
    both = jnp.dot(stacked, tri_m, preferred_element_type=F32)
    parts = [both[(2 * s) * rows:(2 * s + 1) * rows] + both[(2 * s + 1) * rows:(2 * s + 2) * rows] for s in range(n)]
    later = None
    for s in reversed(range(n)):
        if later is not None:
            parts[s] = parts[s] + later
        later = parts[s][:, 0:1]
    return (parts[0] if n == 1 else jnp.concatenate(parts, axis=1)), later


def _attn_cols(d):
    cw = d // 2
    hp = (d // 2) // LANES
    q0 = (3 * cw) // LANES
    return q0, q0 + hp, q0 + 2 * hp, hp


class _Side:
    def __init__(self, operands, out_shapes, aliases, scratch, start, mid, finish):
        self.operands, self.out_shapes, self.aliases, self.scratch = list(operands), list(out_shapes), aliases, list(scratch)
        self.start, self.mid, self.finish = start, mid, finish


def _side_call(side, *, name):
    n_in, n_out = len(side.operands), len(side.out_shapes)

    def body(*refs):
        parts = refs[:n_in], refs[n_in:n_in + n_out], refs[n_in + n_out:]
        side.start(*parts)
        if side.mid is not None:
            side.mid(*parts)
        side.finish(*parts)

    hbm = pl.BlockSpec(memory_space=pltpu.HBM)
    return pl.pallas_call(body, name=name, in_specs=[hbm] * n_in, out_specs=[hbm] * n_out, out_shape=side.out_shapes,
                          input_output_aliases=dict(side.aliases), scratch_shapes=side.scratch)(*side.operands)


def _host_side(side, n_in, n_out, n_scratch, refs, first, late, last):
    if side is None:
        return refs, lambda: None
    s_in, s_out = len(side.operands), len(side.out_shapes)
    ins = refs[:n_in]
    side_in = refs[n_in:n_in + s_in]
    outs = refs[n_in + s_in:n_in + s_in + n_out]
    side_out = refs[n_in + s_in + n_out:n_in + s_in + n_out + s_out]
    rest = refs[n_in + s_in + n_out + s_out:]
    scratch, sems = rest[:n_scratch], rest[n_scratch:]
    parts = (side_in, side_out, sems)
    pl.when(first)(lambda: side.start(*parts))

    def run_late_phases():
        if side.mid is not None:
            pl.when(late)(lambda: side.mid(*parts))
        pl.when(last)(lambda: side.finish(*parts))

    return (*ins, *outs, *scratch), run_late_phases


def _side_specs(side, n_in, n_out):
    if side is None:
        return [], [], [], {}, []
    hbm = pl.BlockSpec(memory_space=pltpu.HBM)
    s_in = len(side.operands)
    aliases = {n_in + a: n_out + b for a, b in side.aliases.items()}
    return [hbm] * s_in, [hbm] * len(side.out_shapes), side.out_shapes, aliases, side.scratch


def _attn_fwd(proj, tri, *, d, name, side=None):
    s = proj.shape[0]
    qb = tri.shape[0]
    chains = _fit(ATTN_CHAINS[0], s // qb)
    ng = s // (qb * chains)
    q0, k0, v0, hp = _attn_cols(d)

    def body(*refs):
        p, g = pl.program_id(0), pl.program_id(1)
        (q_ref, k_ref, v_ref, tri_ref, o_ref, a_ref, b_ref, run_ref), late_phases = _host_side(
            side, 4, 4, 0, refs, (p == 0) & (g == 0), (p == hp - 1) & (g == 0), (p == hp - 1) & (g == ng - 1))
        tri_m = tri_ref[...]
        masks = [_head_mask(h) for h in range(2)]

        def first_steps(u):
            i = g * chains + u
            qs = _stack_heads(q_ref[u * qb:(u + 1) * qb, :] * ATTN_SCALE, masks)

            def strip(j, state, causal=None, keep=None, live=None):
                run, acc = state
                rows = pl.ds(pl.multiple_of(j * qb, qb), qb)
                z = lax.dot_general(qs, k_ref[rows, :], _NT, preferred_element_type=F32)
                lg = _log_one_minus_sigmoid(z)
                beta = 1.0 - jnp.exp(lg) if keep is not None else None
                if causal is not None:
                    lg = jnp.where(causal, lg, 0.0)
                cs, total = _running_sum(lg, tri_m)
                a = jnp.exp(z + cs + run)
                if causal is not None:
                    a = jnp.where(causal, a, 0.0)
                    beta = jnp.where(causal, beta, 0.0)
                if live is not None:
                    a = jnp.where(live, a, 0.0)
                    beta = jnp.where(live, beta, 0.0)
                    total = jnp.where(live, total, 0.0)
                ab = a.astype(BF16)
                if keep is not None:
                    a_ref[0, u, keep] = ab
                    b_ref[0, u, keep] = beta.astype(BF16)
                acc = acc + jnp.dot(_heads_to_lanes(ab, qb), _stack_heads(v_ref[rows, :], masks),
                                    preferred_element_type=F32)
                return run + total, acc

            state = strip(i, (jnp.zeros((2 * qb, 1), F32), jnp.zeros((qb, LANES), F32)),
                          causal=_stacked_causal(qb, qb, 0, 0), keep=0)
            state = strip(jnp.maximum(i - 1, 0), state, keep=1, live=i >= 1)
            run_ref[0, u] = jnp.broadcast_to(state[0], (2 * qb, LANES))
            return i, strip, state

        started = [first_steps(u) for u in range(chains)]
        for u, (i, strip, state) in enumerate(started):
            state = lax.while_loop(
                lambda st: (st[0] >= 0) & (jnp.max(st[1]) > UNDERFLOW_LOG),
                lambda st, strip=strip: (st[0] - 1, *strip(st[0], st[1:])),
                (i - 2, *state))
            o_ref[u * qb:(u + 1) * qb, :] = state[2]
        late_phases()

    s_in, s_out, s_shapes, aliases, s_scratch = _side_specs(side, 4, 4)
    tq = qb * chains
    nq = s // qb
    res = pl.pallas_call(
        body, name=name, grid=(hp, ng),
        in_specs=[pl.BlockSpec((tq, LANES), lambda p, i: (i, q0 + p)),
                  pl.BlockSpec((s, LANES), lambda p, i: (0, k0 + p)),
                  pl.BlockSpec((s, LANES), lambda p, i: (0, v0 + p)),
                  pl.BlockSpec((qb, qb), lambda p, i: (0, 0))] + s_in,
        out_specs=[pl.BlockSpec((tq, LANES), lambda p, i: (i, p)),
                   pl.BlockSpec((1, chains, 2, 2 * qb, qb), lambda p, i: (p, i, 0, 0, 0)),
                   pl.BlockSpec((1, chains, 2, 2 * qb, qb), lambda p, i: (p, i, 0, 0, 0)),
                   pl.BlockSpec((1, chains, 2 * qb, LANES), lambda p, i: (p, i, 0, 0))] + s_out,
        out_shape=[_sds((s, hp * LANES), F32), _sds((hp, nq, 2, 2 * qb, qb), BF16), _sds((hp, nq, 2, 2 * qb, qb), BF16),
                   _sds((hp, nq, 2 * qb, LANES), F32)] + s_shapes,
        input_output_aliases=aliases, scratch_shapes=s_scratch,
        compiler_params=_params(("arbitrary", "arbitrary")),
    )(proj, proj, proj, tri, *([] if side is None else side.operands))
    return (res[0], tuple(res[1:4])), res[4:]


def _mix_out(yc, o, proj, x, wpc, wpa, wout, g, gt, *, name, tm=256):
    s, d = x.shape
    cw = yc.shape[1]
    tm = min(tm, s)
    ga_blk = (3 * cw + 3 * (d // 2)) // d

    def body(yc_ref, o_ref, ga_ref, gb_ref, x_ref, wpc_ref, wpa_ref, wout_ref, g_ref, gt_ref,
             ycv_ref, yat_ref, mg_ref, mix_ref, x1_ref):
        y_conv = jnp.dot(yc_ref[...], wpc_ref[...], preferred_element_type=F32)
        y_attn = jnp.dot(o_ref[...].astype(BF16), wpa_ref[...], preferred_element_type=F32)
        merged = (_sigmoid(ga_ref[...].astype(F32)) * y_conv + _sigmoid(gb_ref[...].astype(F32)) * y_attn)
        mg = merged.astype(BF16)
        mix = jnp.dot(mg, wout_ref[...], preferred_element_type=F32)
        r = lax.rsqrt(jnp.mean(mix * mix, axis=-1, keepdims=True) + EPS)
        ycv_ref[...] = y_conv.astype(BF16)
        yat_ref[...] = y_attn.astype(BF16)
        mg_ref[...] = mg
        mix_ref[...] = mix
        x1_ref[...] = x_ref[...] + gt_ref[...] * (mix * r * g_ref[...])

    def rows(w):
        return pl.BlockSpec((tm, w), lambda i: (i, 0))

    def full(a):
        return pl.BlockSpec(a.shape, lambda i: (0, 0))

    return pl.pallas_call(
        body, name=name, grid=(s // tm,),
        in_specs=[rows(cw), rows(d // 2), pl.BlockSpec((tm, d), lambda i: (i, ga_blk)),
                  pl.BlockSpec((tm, d), lambda i: (i, ga_blk + 1)), rows(d),
                  full(wpc), full(wpa), full(wout), _vec_spec(d, 1), _vec_spec(d, 1)],
        out_specs=[rows(d), rows(d), rows(d), rows(d), rows(d)],
        out_shape=[_sds((s, d), BF16), _sds((s, d), BF16), _sds((s, d), BF16), _sds((s, d), F32), _sds((s, d), F32)],
        compiler_params=_params(("parallel",)),
    )(yc, o, proj, proj, x, wpc, wpa, wout, g, gt)


def _relu2(a):
    r = jnp.maximum(a.astype(F32), 0.0)
    return (r * r).astype(BF16)


def _mlp_out(a, x, w2, g, gt, *, name, tm=512):
    s, d = x.shape
    dff = a.shape[1]
    tm = min(tm, s)

    def body(a_ref, x_ref, w_ref, g_ref, gt_ref, ff_ref, x2_ref):
        ff = jnp.dot(_relu2(a_ref[...]), w_ref[...], preferred_element_type=F32)
        r = lax.rsqrt(jnp.mean(ff * ff, axis=-1, keepdims=True) + EPS)
        ff_ref[...] = ff
        x2_ref[...] = x_ref[...] + gt_ref[...] * (ff * r * g_ref[...])

    return pl.pallas_call(
        body, name=name, grid=(s // tm,),
        in_specs=[pl.BlockSpec((tm, dff), lambda i: (i, 0)), pl.BlockSpec((tm, d), lambda i: (i, 0)),
                  pl.BlockSpec((dff, d), lambda i: (0, 0)), _vec_spec(d, 1), _vec_spec(d, 1)],
        out_specs=[pl.BlockSpec((tm, d), lambda i: (i, 0)), pl.BlockSpec((tm, d), lambda i: (i, 0))],
        out_shape=[_sds((s, d), F32), _sds((s, d), F32)],
        compiler_params=_params(("parallel",)),
    )(a, x, w2, g, gt)


def _loss_grad(y, target, *, name, tm=512):
    s, d = y.shape
    tm = min(tm, s)

    def body(y_ref, t_ref, dy_ref, loss_ref):
        @pl.when(pl.program_id(0) == 0)
        def _():
            loss_ref[...] = jnp.zeros_like(loss_ref)
        e = y_ref[...] - t_ref[...]
        dy_ref[...] = e * (1.0 / d)
        loss_ref[...] += 0.5 * jnp.sum(jnp.mean(e * e, axis=-1, keepdims=True), axis=0, keepdims=True)

    return pl.pallas_call(
        body, name=name, grid=(s // tm,),
        in_specs=[pl.BlockSpec((tm, d), lambda i: (i, 0)), pl.BlockSpec((tm, d), lambda i: (i, 0))],
        out_specs=[pl.BlockSpec((tm, d), lambda i: (i, 0)), pl.BlockSpec((1, 1), lambda i: (0, 0))],
        out_shape=[_sds((s, d), F32), _sds((1, 1), F32)],
        compiler_params=_params(("arbitrary",)),
    )(y, target)


def _mlp_out_bwd(dx, ff, a, w2, g, gt, *, name, tm=512):
    s, d = dx.shape
    dff = a.shape[1]
    tm = min(tm, s)

    def body(dx_ref, ff_ref, a_ref, w_ref, g_ref, gt_ref, dff_ref, da_ref, dgt_ref, dg_ref):
        @pl.when(pl.program_id(0) == 0)
        def _():
            dgt_ref[...] = jnp.zeros_like(dgt_ref)
            dg_ref[...] = jnp.zeros_like(dg_ref)
        dxv = dx_ref[...]
        dn = dxv * gt_ref[...]
        dffv, xh = _rms_bwd(dn, ff_ref[...], g_ref[...])
        dgt_ref[...] += _colsum(dxv * (xh * g_ref[...]))
        dg_ref[...] += _colsum(dn * xh)
        dffb = dffv.astype(BF16)
        dff_ref[...] = dffb
        drr = lax.dot_general(dffb, w_ref[...], _NT, preferred_element_type=F32)
        da_ref[...] = (drr * (2.0 * jnp.maximum(a_ref[...].astype(F32), 0.0))).astype(BF16)

    return pl.pallas_call(
        body, name=name, grid=(s // tm,),
        in_specs=[pl.BlockSpec((tm, d), lambda i: (i, 0)), pl.BlockSpec((tm, d), lambda i: (i, 0)),
                  pl.BlockSpec((tm, dff), lambda i: (i, 0)), pl.BlockSpec((dff, d), lambda i: (0, 0)),
                  _vec_spec(d, 1), _vec_spec(d, 1)],
        out_specs=[pl.BlockSpec((tm, d), lambda i: (i, 0)), pl.BlockSpec((tm, dff), lambda i: (i, 0)),
                   _vec_spec(d, 1), _vec_spec(d, 1)],
        out_shape=[_sds((s, d), BF16), _sds((s, dff), BF16), _sds((1, d), F32), _sds((1, d), F32)],
        compiler_params=_params(("arbitrary",)),
    )(dx, ff, a, w2, g, gt)


def _matmul_nt_norm_bwd(dys, w, x, dres, g, sc, *, name, tm=512, side=None):
    s = dys[0].shape[0]
    widths = [dy.shape[1] for dy in dys]
    d, n = w.shape
    assert sum(widths) == n, (widths, n)
    tm = _fit(tm, s)
    nt = s // tm
    np_ = len(dys)

    def body(*refs):
        i = pl.program_id(0)
        own, late_phases = _host_side(side, np_ + 5, 4, 0, refs, i == 0, i == (3 * nt) // 4, i == nt - 1)
        dy_refs = own[:np_]
        w_ref, x_ref, dres_ref, g_ref, sc_ref, dx_ref, dsh_ref, dsc_ref, dg_ref = own[np_:]

        @pl.when(i == 0)
        def _():
            dsh_ref[...] = jnp.zeros_like(dsh_ref)
            dsc_ref[...] = jnp.zeros_like(dsc_ref)
            dg_ref[...] = jnp.zeros_like(dg_ref)

        dh = None
        for p, dy_ref in enumerate(dy_refs):
            cols = slice(sum(widths[:p]), sum(widths[:p + 1]))
            part = lax.dot_general(dy_ref[...], w_ref[:, cols], _NT, preferred_element_type=F32)
            dh = part if dh is None else dh + part
        dn = dh * (1.0 + sc_ref[...])
        dxin, xh = _rms_bwd(dn, x_ref[...], g_ref[...])
        dsh_ref[...] += _colsum(dh)
        dsc_ref[...] += _colsum(dh * (xh * g_ref[...]))
        dg_ref[...] += _colsum(dn * xh)
        dx_ref[...] = dres_ref[...] + dxin
        late_phases()

    s_in, s_out, s_shapes, aliases, s_scratch = _side_specs(side, np_ + 5, 4)
    res = pl.pallas_call(
        body, name=name, grid=(nt,),
        in_specs=[pl.BlockSpec((tm, wd), lambda i: (i, 0)) for wd in widths]
        + [pl.BlockSpec((d, n), lambda i: (0, 0)),
           pl.BlockSpec((tm, d), lambda i: (i, 0)), pl.BlockSpec((tm, d), lambda i: (i, 0)),
           _vec_spec(d, 1), _vec_spec(d, 1)] + s_in,
        out_specs=[pl.BlockSpec((tm, d), lambda i: (i, 0)), _vec_spec(d, 1), _vec_spec(d, 1), _vec_spec(d, 1)] + s_out,
        out_shape=[_sds((s, d), F32), _sds((1, d), F32), _sds((1, d), F32), _sds((1, d), F32)] + s_shapes,
        input_output_aliases=aliases, scratch_shapes=s_scratch,
        compiler_params=_params(("arbitrary",)),
    )(*dys, w, x, dres, g, sc, *([] if side is None else side.operands))
    return res[:4], res[4:]


def _matmul_tn(a, bs, *, name, tk=1024, tn=1024, ts=512, relu2=False, into=None, col0=0, n_total=None):
    s, k = a.shape
    widths = [b.shape[1] for b in bs]
    n = sum(widths)
    tk, ts = _fit(tk, k), _fit(ts, s)
    for w in widths:
        tn = _fit(tn, w)
    while col0 % tn:
        tn //= 2
    nt = s // ts
    assert tn % LANES == 0 and all(sum(widths[:p]) % tn == 0 for p in range(len(bs))), (widths, tn)
    first = [sum(widths[:p]) // tn for p in range(len(bs))]
    tiles = [w // tn for w in widths]
    tile0 = col0 // tn

    def body(a_ref, *rest):
        b_refs, o_ref, acc = rest[:len(bs)], rest[-2], rest[-1]
        j, t = pl.program_id(1), pl.program_id(2)

        @pl.when(t == 0)
        def _():
            acc[...] = jnp.zeros_like(acc)
        av = a_ref[...]
        av = _relu2(av) if relu2 else av.astype(BF16)
        for p, b_ref in enumerate(b_refs):
            def add(b_ref=b_ref):
                acc[...] += lax.dot_general(av, b_ref[...], _TN, preferred_element_type=F32)
            if len(bs) == 1:
                add()
            else:
                pl.when((j >= first[p]) & (j < first[p] + tiles[p]))(add)

        @pl.when(t == nt - 1)
        def _():
            o_ref[...] = acc[...].astype(BF16)

    def piece_spec(p):
        def index(i, j, t):
            mine = (j >= first[p]) & (j < first[p] + tiles[p])
            return jnp.where(mine, t, 0), jnp.where(mine, j - first[p], 0)
        return pl.BlockSpec((ts, tn), index)

    operands, extra_specs, aliases = [a, *bs], [], {}
    if into is not None:
        operands.append(into)
        extra_specs = [pl.BlockSpec(memory_space=pltpu.HBM)]
        aliases = {len(operands) - 1: 0}
    return pl.pallas_call(
        body, name=name, grid=(k // tk, n // tn, nt),
        in_specs=[pl.BlockSpec((ts, tk), lambda i, j, t: (t, i))] + [piece_spec(p) for p in range(len(bs))] + extra_specs,
        out_specs=pl.BlockSpec((tk, tn), lambda i, j, t: (i, tile0 + j)),
        out_shape=_sds((k, n_total or n), BF16),
        input_output_aliases=aliases,
        scratch_shapes=[pltpu.VMEM((tk, tn), F32)],
        compiler_params=_params(("parallel", "parallel", "arbitrary")),
    )(*operands)


def _mix_out_bwd(dx, mix, proj, ycv, yat, wout, wpc, wpa, g, gt, *, name, tm=256):
    s, d = dx.shape
    cw = wpc.shape[0]
    aw = wpa.shape[0]
    tm = min(tm, s)
    ga_blk = (3 * cw + 3 * aw) // d

    def body(dx_ref, mix_ref, ga_ref, gb_ref, ycv_ref, yat_ref, wout_ref, wpc_ref, wpa_ref, g_ref, gt_ref,
             dmix_ref, dycv_ref, dyat_ref, dyc_ref, do_ref, dgate_ref, dgt_ref, dg_ref):
        @pl.when(pl.program_id(0) == 0)
        def _():
            dgt_ref[...] = jnp.zeros_like(dgt_ref)
            dg_ref[...] = jnp.zeros_like(dg_ref)
        dxv = dx_ref[...]
        dn = dxv * gt_ref[...]
        dmix, xh = _rms_bwd(dn, mix_ref[...], g_ref[...])
        dgt_ref[...] += _colsum(dxv * (xh * g_ref[...]))
        dg_ref[...] += _colsum(dn * xh)
        dmixb = dmix.astype(BF16)
        dmix_ref[...] = dmixb
        dmerged = lax.dot_general(dmixb, wout_ref[...], _NT, preferred_element_type=F32)
        sga = _sigmoid(ga_ref[...].astype(F32))
        sgb = _sigmoid(gb_ref[...].astype(F32))
        dycv = (dmerged * sga).astype(BF16)
        dyat = (dmerged * sgb).astype(BF16)
        dycv_ref[...] = dycv
        dyat_ref[...] = dyat
        dgate_ref[:, 0:d] = (dmerged * ycv_ref[...].astype(F32) * (sga * (1.0 - sga))).astype(BF16)
        dgate_ref[:, d:2 * d] = (dmerged * yat_ref[...].astype(F32) * (sgb * (1.0 - sgb))).astype(BF16)
        dyc_ref[...] = lax.dot_general(dycv, wpc_ref[...], _NT, preferred_element_type=F32).astype(BF16)
        do_ref[...] = lax.dot_general(dyat, wpa_ref[...], _NT, preferred_element_type=F32).astype(BF16)

    def rows(w):
        return pl.BlockSpec((tm, w), lambda i: (i, 0))

    def full(a):
        return pl.BlockSpec(a.shape, lambda i: (0, 0))

    return pl.pallas_call(
        body, name=name, grid=(s // tm,),
        in_specs=[rows(d), rows(d), pl.BlockSpec((tm, d), lambda i: (i, ga_blk)),
                  pl.BlockSpec((tm, d), lambda i: (i, ga_blk + 1)), rows(d), rows(d),
                  full(wout), full(wpc), full(wpa), _vec_spec(d, 1), _vec_spec(d, 1)],
        out_specs=[rows(d), rows(d), rows(d), rows(cw), rows(aw), rows(2 * d), _vec_spec(d, 1), _vec_spec(d, 1)],
        out_shape=[_sds((s, d), BF16), _sds((s, d), BF16), _sds((s, d), BF16), _sds((s, cw), BF16),
                   _sds((s, aw), BF16), _sds((s, 2 * d), BF16), _sds((1, d), F32), _sds((1, d), F32)],
        compiler_params=_params(("arbitrary",)),
    )(dx, mix, proj, proj, ycv, yat, wout, wpc, wpa, g, gt)


def _conv_bwd(dyc, proj, conv_w, *, name, tm=512):
    s = proj.shape[0]
    cw = conv_w.shape[1]
    tm = min(tm, s)
    nb = tm // HALO
    nt = s // tm
    last_blk = s // HALO - 1

    def body(dyc_ref, bg_ref, cg_ref, u_ref, cgh_ref, uh_ref, dych_ref, bgh_ref, w_ref,
             dconv_ref, dw_ref, vbuf, gbuf):
        i = pl.program_id(0)

        @pl.when(i == 0)
        def _():
            dw_ref[...] = jnp.zeros_like(dw_ref)

        cg = cg_ref[...].astype(F32)
        u = u_ref[...].astype(F32)
        vv = cg * u
        halo = cgh_ref[...].astype(F32) * uh_ref[...].astype(F32)
        vbuf[0:HALO, :] = jnp.where(i > 0, halo, 0.0)
        vbuf[HALO:HALO + tm, :] = vv
        v1 = vbuf[HALO - 1:HALO - 1 + tm, :]
        v2 = vbuf[HALO - 2:HALO - 2 + tm, :]
        w = w_ref[...]
        y = w[2:3, :] * vv + w[1:2, :] * v1 + w[0:1, :] * v2
        dyc = dyc_ref[...].astype(F32)
        dconv_ref[:, 0:cw] = (dyc * y).astype(BF16)
        gy = dyc * bg_ref[...].astype(F32)
        nxt = dych_ref[...].astype(F32) * bgh_ref[...].astype(F32)
        gbuf[0:tm, :] = gy
        gbuf[tm:tm + HALO, :] = jnp.where(i < nt - 1, nxt, 0.0)
        g1 = gbuf[1:1 + tm, :]
        g2 = gbuf[2:2 + tm, :]
        dvv = w[2:3, :] * gy + w[1:2, :] * g1 + w[0:1, :] * g2
        dconv_ref[:, cw:2 * cw] = (dvv * u).astype(BF16)
        dconv_ref[:, 2 * cw:3 * cw] = (dvv * cg).astype(BF16)
        dw_ref[0:1, :] += _colsum(gy * v2)
        dw_ref[1:2, :] += _colsum(gy * v1)
        dw_ref[2:3, :] += _colsum(gy * vv)

    def prev(i):
        return jnp.maximum(i * nb - 1, 0)

    def nxt_blk(i):
        return jnp.minimum((i + 1) * nb, last_blk)

    def col(c):
        return pl.BlockSpec((tm, cw), lambda i: (i, c))

    return pl.pallas_call(
        body, name=name, grid=(nt,),
        in_specs=[col(0), col(0), col(1), col(2),
                  pl.BlockSpec((HALO, cw), lambda i: (prev(i), 1)), pl.BlockSpec((HALO, cw), lambda i: (prev(i), 2)),
                  pl.BlockSpec((HALO, cw), lambda i: (nxt_blk(i), 0)), pl.BlockSpec((HALO, cw), lambda i: (nxt_blk(i), 0)),
                  pl.BlockSpec((3, cw), lambda i: (0, 0))],
        out_specs=[pl.BlockSpec((tm, 3 * cw), lambda i: (i, 0)), pl.BlockSpec((3, cw), lambda i: (0, 0))],
        out_shape=[_sds((s, 3 * cw), BF16), _sds((3, cw), F32)],
        scratch_shapes=[pltpu.VMEM((HALO + tm, cw), F32), pltpu.VMEM((tm + HALO, cw), F32)],
        compiler_params=_params(("arbitrary",)),
    )(dyc, proj, proj, proj, proj, proj, dyc, proj, conv_w)


def _attn_bwd(proj, o, kept, do, tri, *, d, name, side=None):
    s = proj.shape[0]
    qb = tri.shape[0]
    chains = _fit(ATTN_CHAINS[1], s // qb)
    ng = s // (qb * chains)
    q0, k0, v0, hp = _attn_cols(d)

    def body(*refs):
        p, g = pl.program_id(0), pl.program_id(1)
        own, late_phases = _host_side(
            side, 9, 3, 2, refs, (p == 0) & (g == 0), (p == hp - 1) & (g == 0), (p == hp - 1) & (g == ng - 1))
        (q_ref, k_ref, v_ref, o_ref, do_ref, tri_ref, a_ref, b_ref, run_ref,
         dq_ref, dk_ref, dv_ref, dk_acc, dv_acc) = own

        @pl.when(g == 0)
        def _():
            dk_acc[...] = jnp.zeros_like(dk_acc)
            dv_acc[...] = jnp.zeros_like(dv_acc)

        tri_m = tri_ref[...]
        masks = [_head_mask(h) for h in range(2)]

        def first_steps(u):
            i = g * chains + u
            mine = slice(u * qb, (u + 1) * qb)
            dov = do_ref[mine, :]
            qs = _stack_heads(q_ref[mine, :] * ATTN_SCALE, masks)
            dos = _stack_heads(dov, masks)
            dprod = dov.astype(F32) * o_ref[mine, :]
            dtot = jnp.concatenate([jnp.sum(jnp.where(m, dprod, 0.0), axis=-1, keepdims=True) for m in masks], axis=0)

            def through(j, ab, beta, left, grun, dq_acc):
                rows = pl.ds(pl.multiple_of(j * qb, qb), qb)
                kb = k_ref[rows, :]
                da = lax.dot_general(dos, v_ref[rows, :], _NT, preferred_element_type=F32)
                gg = ab.astype(F32) * da
                gcs, gtotal = _running_sum(gg, tri_m)
                dzb = (gg - beta * (gg + (left - gcs))).astype(BF16)
                dq_acc = dq_acc + jnp.dot(_heads_to_lanes(dzb, qb), _stack_heads(kb, masks),
                                          preferred_element_type=F32)
                dk_add = lax.dot_general(dzb, qs, _TN, preferred_element_type=F32)
                dv_add = lax.dot_general(ab, dos, _TN, preferred_element_type=F32)
                return (grun + gtotal, dq_acc), (rows, dk_add, dv_add)

            def kept(slot, j, state):
                grun, dq_acc = state
                return through(j, a_ref[0, u, slot], b_ref[0, u, slot].astype(F32), dtot - grun, grun, dq_acc)

            def strip(j, state):
                run, grun, dq_acc = state
                z = lax.dot_general(qs, k_ref[pl.ds(pl.multiple_of(j * qb, qb), qb), :], _NT, preferred_element_type=F32)
                lg = _log_one_minus_sigmoid(z)
                cs, total = _running_sum(lg, tri_m)
                ab = jnp.exp(z + cs + run).astype(BF16)
                left = jnp.where(run > UNDERFLOW_LOG, dtot - grun, 0.0)
                (grun, dq_acc), adds = through(j, ab, 1.0 - jnp.exp(lg), left, grun, dq_acc)
                return (run + total, grun, dq_acc), adds

            state, adds0 = kept(0, i, (jnp.zeros((2 * qb, 1), F32), jnp.zeros((qb, LANES), F32)))
            state, adds1 = kept(1, jnp.maximum(i - 1, 0), state)
            return i, strip, (run_ref[0, u][:, 0:1], *state), (adds0, adds1)

        started = [first_steps(u) for u in range(chains)]
        for u, (i, strip, state, adds) in enumerate(started):
            for rows, dk_add, dv_add in adds:
                dk_acc[rows, :] += dk_add
                dv_acc[rows, :] += dv_add

            def more(st, strip=strip):
                state, (rows, dk_add, dv_add) = strip(st[0], st[1:])
                dk_acc[rows, :] += dk_add
                dv_acc[rows, :] += dv_add
                return (st[0] - 1, *state)

            state = lax.while_loop(lambda st: (st[0] >= 0) & (jnp.max(st[1]) > UNDERFLOW_LOG), more, (i - 2, *state))
            dq_ref[u * qb:(u + 1) * qb, :] = (state[3] * ATTN_SCALE).astype(BF16)

        @pl.when(g == ng - 1)
        def _():
            dk_ref[...] = dk_acc[...].astype(BF16)
            dv_ref[...] = dv_acc[...].astype(BF16)

        late_phases()

    aw = hp * LANES
    tq = qb * chains
    s_in, s_out, s_shapes, aliases, s_scratch = _side_specs(side, 9, 3)
    res = pl.pallas_call(
        body, name=name, grid=(hp, ng),
        in_specs=[pl.BlockSpec((tq, LANES), lambda p, i: (i, q0 + p)),
                  pl.BlockSpec((s, LANES), lambda p, i: (0, k0 + p)),
                  pl.BlockSpec((s, LANES), lambda p, i: (0, v0 + p)),
                  pl.BlockSpec((tq, LANES), lambda p, i: (i, p)),
                  pl.BlockSpec((tq, LANES), lambda p, i: (i, p)),
                  pl.BlockSpec((qb, qb), lambda p, i: (0, 0)),
                  pl.BlockSpec((1, chains, 2, 2 * qb, qb), lambda p, i: (p, i, 0, 0, 0)),
                  pl.BlockSpec((1, chains, 2, 2 * qb, qb), lambda p, i: (p, i, 0, 0, 0)),
                  pl.BlockSpec((1, chains, 2 * qb, LANES), lambda p, i: (p, i, 0, 0))] + s_in,
        out_specs=[pl.BlockSpec((tq, LANES), lambda p, i: (i, p)),
                   pl.BlockSpec((s, LANES), lambda p, i: (0, p)),
                   pl.BlockSpec((s, LANES), lambda p, i: (0, p))] + s_out,
        out_shape=[_sds((s, aw), BF16), _sds((s, aw), BF16), _sds((s, aw), BF16)] + s_shapes,
        input_output_aliases=aliases,
        scratch_shapes=[pltpu.VMEM((s, LANES), F32), pltpu.VMEM((s, LANES), F32)] + s_scratch,
        compiler_params=_params(("arbitrary", "arbitrary")),
    )(proj, proj, proj, o, do, tri, *kept, *([] if side is None else side.operands))
    return res[:3], res[3:]


def _hosted(hooks, kind, l, fn, *args, **kw):
    res, side_out = fn(*args, side=hooks.side(kind, l), **kw)
    hooks.done(kind, l, side_out)
    return res


def _layer_fwd(x, mod, gains, conv_w, tri, *, l, hooks):
    sh1, sc1, gt1, sh2, sc2, gt2 = mod
    g_pre_mix, g_post_mix, g_pre_mlp, g_post_mlp = gains
    d = x.shape[1]
    w = functools.partial(hooks.weight, l)
    h, proj = _hosted(hooks, "in_proj", l, _norm_mod_matmul, x, g_pre_mix, sc1, sh1, w("w_in"), name=f"in_proj_{l}")
    yc = _conv_fwd(proj, conv_w, name=f"conv_fwd_{l}")
    o, kept = _hosted(hooks, "attn_fwd", l, _attn_fwd, proj, tri, d=d, name=f"attn_fwd_{l}")
    ycv, yat, merged, mix, x1 = _mix_out(yc, o, proj, x, w("w_proj_conv"), w("w_proj_attn"), w("w_out"),
                                         g_post_mix, gt1, name=f"mix_out_{l}")
    (h2, a), _ = _norm_mod_matmul(x1, g_pre_mlp, sc2, sh2, w("w_mlp_in"), name=f"mlp_in_{l}")
    ff, x2 = _mlp_out(a, x1, w("w_mlp_out"), g_post_mlp, gt2, name=f"mlp_out_{l}")
    saved = dict(x=x, h=h, proj=proj, yc=yc, o=o, kept=kept, ycv=ycv, yat=yat, merged=merged, mix=mix, x1=x1, h2=h2, a=a, ff=ff,
                 conv_w=conv_w, **{k: w(k) for k in BIG})
    return x2, saved


def _layer_bwd(dx2, sv, mod, gains, tri, *, l, hooks):
    sh1, sc1, gt1, sh2, sc2, gt2 = mod
    g_pre_mix, g_post_mix, g_pre_mlp, g_post_mlp = gains
    d = dx2.shape[1]
    dff, da, dgt2, dg_post_mlp = _mlp_out_bwd(dx2, sv["ff"], sv["a"], sv["w_mlp_out"], g_post_mlp, gt2,
                                              name=f"mlp_out_bwd_{l}")
    hooks.grad(l, "w_mlp_out", _matmul_tn(sv["a"], [dff], relu2=True, name=f"gw_mlp_out_{l}"))
    (dx1, dsh2, dsc2, dg_pre_mlp), _ = _matmul_nt_norm_bwd([da], sv["w_mlp_in"], sv["x1"], dx2, g_pre_mlp, sc2,
                                                           name=f"mlp_in_bwd_{l}")
    hooks.grad(l, "w_mlp_in", _matmul_tn(sv["h2"], [da], name=f"gw_mlp_in_{l}"))
    dmix, dycv, dyat, dyc, do, dgate, dgt1, dg_post_mix = _mix_out_bwd(
        dx1, sv["mix"], sv["proj"], sv["ycv"], sv["yat"], sv["w_out"], sv["w_proj_conv"], sv["w_proj_attn"],
        g_post_mix, gt1, name=f"mix_out_bwd_{l}")
    hooks.grad(l, "w_out", _matmul_tn(sv["merged"], [dmix], name=f"gw_out_{l}"))
    hooks.grad(l, "w_proj_conv", _matmul_tn(sv["yc"], [dycv], name=f"gw_proj_conv_{l}"))
    hooks.grad(l, "w_proj_attn", _matmul_tn(sv["o"], [dyat], name=f"gw_proj_attn_{l}"))
    dconv, g_conv_w = _conv_bwd(dyc, sv["proj"], sv["conv_w"], name=f"conv_bwd_{l}")
    dq, dk, dv = _hosted(hooks, "attn_bwd", l, _attn_bwd, sv["proj"], sv["o"], sv["kept"], do, tri, d=d,
                         name=f"attn_bwd_{l}")
    dproj = [dconv, dq, dk, dv, dgate]
    n_in = sv["w_in"].shape[1]
    gw_in = _matmul_tn(sv["h"], [dconv], tn=768, n_total=n_in, name=f"gw_in_conv_{l}")
    gw_in = _matmul_tn(sv["h"], [dq, dk, dv], into=gw_in, col0=dconv.shape[1], n_total=n_in, name=f"gw_in_attn_{l}")
    gw_in = _matmul_tn(sv["h"], [dgate], into=gw_in, col0=n_in - dgate.shape[1], n_total=n_in, name=f"gw_in_gate_{l}")
    hooks.grad(l, "w_in", gw_in)
    dx0, dsh1, dsc1, dg_pre_mix = _hosted(hooks, "in_proj_bwd", l, _matmul_nt_norm_bwd, dproj, sv["w_in"], sv["x"], dx1,
                                          g_pre_mix, sc1, name=f"in_proj_bwd_{l}")
    dmod = jnp.concatenate([dsh1, dsc1, dgt1, dsh2, dsc2, dgt2], axis=0)
    dgains = jnp.concatenate([dg_pre_mix, dg_post_mix, dg_pre_mlp, dg_post_mlp], axis=0)
    return dx0, g_conv_w, dmod, dgains


BIG = ("w_in", "w_proj_conv", "w_proj_attn", "w_out", "w_mlp_in", "w_mlp_out")
SHARD_AXIS = dict(w_in=1, w_proj_conv=1, w_proj_attn=1, w_out=0, w_mlp_in=1, w_mlp_out=0)


class _LocalWeights:
    def __init__(self, wlayers):
        self.wlayers = wlayers
        self.grads = {}

    def weight(self, l, name):
        return self.wlayers[l][name]

    def side(self, kind, l):
        return None

    def done(self, kind, l, outs):
        pass

    def grad(self, l, name, g):
        self.grads[(l, name)] = g


def _local_step(x, target, mods, gains, conv_w, hooks):
    depth = mods.shape[0]
    tri = _tri(ATTN_BLOCK)
    saved = []
    for l in range(depth):
        mod = [mods[l, k:k + 1] for k in range(N_MOD)]
        gl = [gains[l, k:k + 1] for k in range(4)]
        x, sv = _layer_fwd(x, mod, gl, conv_w[l], tri, l=l, hooks=hooks)
        saved.append((sv, mod, gl))
    dx, loss = _loss_grad(x, target, name="loss_grad")
    dconv, dmods, dgains = [None] * depth, [None] * depth, [None] * depth
    for l in reversed(range(depth)):
        sv, mod, gl = saved[l]
        dx, dconv[l], dmods[l], dgains[l] = _layer_bwd(dx, sv, mod, gl, tri, l=l, hooks=hooks)
    return loss, dx, jnp.stack(dconv), jnp.stack(dmods), jnp.stack(dgains)


def _coords():
    return lax.axis_index("x"), lax.axis_index("y"), lax.axis_index("c")


def _flip(v, f):
    return 1 - v if f else v


def _all_gather_small(v, *, name):
    r, c_ = v.shape

    def body(v_ref, out_ref, send_sems, recv_sems, local_sem):
        x, y, c = _coords()
        me = 4 * x + 2 * y + c
        mine = pltpu.make_async_copy(v_ref, out_ref.at[me], local_sem)
        mine.start()
        copies = []
        for k in range(1, 8):
            fx, fy, fc = (k >> 2) & 1, (k >> 1) & 1, k & 1
            px, py, pc = _flip(x, fx), _flip(y, fy), _flip(c, fc)
            out = pltpu.make_async_remote_copy(src_ref=v_ref, dst_ref=out_ref.at[me], send_sem=send_sems.at[k - 1],
                                               recv_sem=recv_sems.at[k - 1], device_id=(px, py, pc), device_id_type=MESH)
            out.start()
            back = pltpu.make_async_remote_copy(src_ref=v_ref, dst_ref=out_ref.at[4 * px + 2 * py + pc],
                                                send_sem=send_sems.at[k - 1], recv_sem=recv_sems.at[k - 1],
                                                device_id=(px, py, pc), device_id_type=MESH)
            copies.append((out, back))
        for out, back in copies:
            back.wait_recv()
        for out, back in copies:
            out.wait_send()
        mine.wait()

    return pl.pallas_call(
        body, name=name,
        in_specs=[pl.BlockSpec(memory_space=pltpu.VMEM)],
        out_specs=pl.BlockSpec(memory_space=pltpu.VMEM),
        out_shape=_sds((8, r, c_), F32),
        scratch_shapes=[pltpu.SemaphoreType.DMA((7,)), pltpu.SemaphoreType.DMA((7,)), pltpu.SemaphoreType.DMA],
    )(v)


def _shard_dims(full_shape, axis):
    k, n = full_shape
    return (k // 4, n) if axis == 0 else (k, n // 4)


def _shard_window(ref, axis, chip, half, rows, cols):
    r0, rn = (0, rows) if half is None else (half * (rows // 2), rows // 2)
    if axis == 1:
        return ref.at[pl.ds(r0, rn), pl.ds(chip * cols, cols)]
    return ref.at[pl.ds(chip * rows + r0, rn), :]


def _cast_place(w, layer, axis, chip_arr, *, name, tr=256):
    _, rows, cols = w.shape
    tr = _fit(tr, rows)
    nb = rows // tr
    full = (rows * 4, cols) if axis == 0 else (rows, cols * 4)

    def body(chip_ref, w_ref, o_ref):
        o_ref[...] = w_ref[0].astype(BF16)

    if axis == 1:
        out_map = lambda i, chip: (i, chip[0])
    else:
        out_map = lambda i, chip: (chip[0] * nb + i, 0)
    grid_spec = pltpu.PrefetchScalarGridSpec(
        num_scalar_prefetch=1, grid=(nb,),
        in_specs=[pl.BlockSpec((1, tr, cols), lambda i, chip: (layer, i, 0))],
        out_specs=pl.BlockSpec((tr, cols), out_map))
    return pl.pallas_call(body, name=name, grid_spec=grid_spec, out_shape=_sds(full, BF16),
                          compiler_params=_params(("arbitrary",)))(chip_arr, w)


def _gather_side(fulls, axes):
    n = len(fulls)

    def copies(outs, sems):
        send_sems, recv_sems = sems
        x, y, c = _coords()
        chip = 2 * x + y
        sibling = (x, y, 1 - c)
        table = []
        for w in range(n):
            rows, cols = _shard_dims(outs[w].shape, axes[w])
            win = functools.partial(_shard_window, outs[w], axes[w], rows=rows, cols=cols)
            for j, (fx, fy) in enumerate(OTHER_CHIPS):
                px, py = _flip(x, fx), _flip(y, fy)
                pchip = 2 * px + py

                def copy(piece, sem, to):
                    return pltpu.make_async_remote_copy(src_ref=piece, dst_ref=piece, send_sem=send_sems.at[w, sem],
                                                        recv_sem=recv_sems.at[w, sem], device_id=to, device_id_type=MESH)

                table.append((copy(win(chip, c), j, (px, py, c)), copy(win(pchip, c), j, (px, py, c)),
                              copy(win(pchip, c), 3 + j, sibling), copy(win(pchip, 1 - c), 3 + j, sibling)))
        return table

    def start(ins, outs, sems):
        for send, _, _, _ in copies(outs, sems):
            send.start()

    def mid(ins, outs, sems):
        for _, landed, pass_on, _ in copies(outs, sems):
            landed.wait_recv()
            pass_on.start()

    def finish(ins, outs, sems):
        table = copies(outs, sems)
        for _, _, _, from_sibling in table:
            from_sibling.wait_recv()
        for send, _, pass_on, _ in table:
            send.wait_send()
            pass_on.wait_send()

    return _Side(fulls, [_sds(f.shape, f.dtype) for f in fulls], {w: w for w in range(n)},
                 [pltpu.SemaphoreType.DMA((n, 6)), pltpu.SemaphoreType.DMA((n, 6))], start, mid, finish)


def _exchange_side(grads, axes):
    n = len(grads)
    out_shapes = []
    for g, ax in zip(grads, axes):
        rows, cols = _shard_dims(g.shape, ax)
        out_shapes.append(_sds((7, rows // 2, cols), g.dtype))

    def copies(ins, outs, sems):
        send_sems, recv_sems = sems
        x, y, c = _coords()
        table = []
        for w in range(n):
            rows, cols = _shard_dims(ins[w].shape, axes[w])
            for k in range(1, 8):
                fx, fy, fc = (k >> 2) & 1, (k >> 1) & 1, k & 1
                px, py, pc = _flip(x, fx), _flip(y, fy), _flip(c, fc)
                piece = _shard_window(ins[w], axes[w], 2 * px + py, pc, rows, cols)
                table.append(pltpu.make_async_remote_copy(
                    src_ref=piece, dst_ref=outs[w].at[k - 1], send_sem=send_sems.at[w, k - 1],
                    recv_sem=recv_sems.at[w, k - 1], device_id=(px, py, pc), device_id_type=MESH))
        return table

    def start(ins, outs, sems):
        for cp in copies(ins, outs, sems):
            cp.start()

    def finish(ins, outs, sems):
        table = copies(ins, outs, sems)
        for cp in table:
            cp.wait_recv()
        for cp in table:
            cp.wait_send()

    return _Side(grads, out_shapes, {}, [pltpu.SemaphoreType.DMA((n, 7)), pltpu.SemaphoreType.DMA((n, 7))],
                 start, None, finish)


def _rs_sum_join(g, got, out_prev, layer, depth, axis, ids, *, name, tr=256):
    _, rows2, cols = got.shape
    tr = _fit(tr, rows2)
    nt = rows2 // tr
    if axis == 1:
        own_map = lambda i, ids_: (ids_[1] * nt + i, ids_[0])
    else:
        own_map = lambda i, ids_: ((ids_[0] * 2 + ids_[1]) * nt + i, 0)

    def body(ids_ref, g_ref, got_ref, *rest):
        out_ref, buf, local_sems, send_sems, recv_sem = rest[-5:]
        i = pl.program_id(0)
        x, y, c = _coords()
        sibling = (x, y, 1 - c)

        def copies(step, slot):
            rows_mine = pl.ds(c * rows2 + step * tr, tr)
            dst = out_ref.at[layer, rows_mine, :]
            keep = pltpu.make_async_copy(buf.at[slot], dst, local_sems.at[slot])
            give = pltpu.make_async_remote_copy(src_ref=buf.at[slot], dst_ref=dst, send_sem=send_sems.at[slot],
                                                recv_sem=recv_sem, device_id=sibling, device_id_type=MESH)
            return keep, give

        def drain(step, slot):
            keep, give = copies(step, slot)
            keep.wait()
            give.wait_send()

        slot = i % 2

        @pl.when(i >= 2)
        def _():
            drain(i - 2, slot)

        acc = g_ref[...].astype(F32)
        for k in range(7):
            acc = acc + got_ref[k].astype(F32)
        buf[slot] = acc
        keep, give = copies(i, slot)
        keep.start()
        give.start()

        @pl.when(i == nt - 1)
        def _():
            if nt >= 2:
                drain(nt - 2, (nt - 2) % 2)
            drain(nt - 1, (nt - 1) % 2)
            theirs = out_ref.at[layer, pl.ds((1 - c) * rows2, rows2), :]
            pltpu.make_async_remote_copy(src_ref=theirs, dst_ref=theirs, send_sem=send_sems.at[0], recv_sem=recv_sem,
                                         device_id=sibling, device_id_type=MESH).wait_recv()

    hbm = pl.BlockSpec(memory_space=pltpu.HBM)
    in_specs = [pl.BlockSpec((tr, cols), own_map), pl.BlockSpec((7, tr, cols), lambda i, ids_: (0, i, 0))]
    operands = [ids, g, got]
    aliases = {}
    if out_prev is not None:
        in_specs.append(hbm)
        operands.append(out_prev)
        aliases = {3: 0}
    grid_spec = pltpu.PrefetchScalarGridSpec(
        num_scalar_prefetch=1, grid=(nt,), in_specs=in_specs, out_specs=hbm,
        scratch_shapes=[pltpu.VMEM((2, tr, cols), F32), pltpu.SemaphoreType.DMA((2,)), pltpu.SemaphoreType.DMA((2,)),
                        pltpu.SemaphoreType.DMA])
    return pl.pallas_call(body, name=name, grid_spec=grid_spec, out_shape=_sds((depth, 2 * rows2, cols), F32),
                          input_output_aliases=aliases, compiler_params=_params(("arbitrary",)))(*operands)


MIX = ("w_proj_conv", "w_proj_attn", "w_out")


class _Schedule:
    def __init__(self, placed, depth, ids):
        self.placed, self.depth, self.ids = placed, depth, ids
        self.full, self.g, self.carried = {}, {}, None
        self.reduced = {k: None for k in BIG}
        first = [(0, "w_in")]
        self._landed(first, _side_call(self._gather(first), name="gather_w_in_0"))

    def _gather(self, keys):
        return _gather_side([self.placed[k] for k in keys], [SHARD_AXIS[k[1]] for k in keys])

    def _landed(self, keys, outs):
        for k, o in zip(keys, outs):
            self.full[k] = o

    def _exchange(self, keys):
        return _exchange_side([self.g[k] for k in keys], [SHARD_AXIS[k[1]] for k in keys])

    def _reduce(self, keys, got):
        for (l, name), pieces in zip(keys, got):
            self.reduced[name] = _rs_sum_join(self.g[(l, name)], pieces, self.reduced[name], l, self.depth,
                                              SHARD_AXIS[name], self.ids, name=f"rs_sum_join_{l}_{name}")

    def weight(self, l, name):
        return self.full[(l, name)]

    def grad(self, l, name, g):
        self.g[(l, name)] = g

    def side(self, kind, l):
        nxt = [(l + 1, "w_in")] if l + 1 < self.depth else []
        if kind == "in_proj":
            keys, make = [(l, k) for k in MIX + ("w_mlp_in",)], self._gather
        elif kind == "attn_fwd":
            keys, make = [(l, "w_mlp_out")] + nxt, self._gather
        elif kind == "attn_bwd":
            keys, make = [(l, k) for k in ("w_mlp_out", "w_mlp_in") + MIX], self._exchange
        else:
            keys, make = [(l, "w_in")], self._exchange
        self.carried = keys
        return make(keys)

    def done(self, kind, l, outs):
        (self._landed if kind in ("in_proj", "attn_fwd") else self._reduce)(self.carried, outs)


def _flat_rows(shape):
    rows = 1
    for s in shape[:-1]:
        rows *= s
    return rows, shape[-1]


def _row_tile(rows, cols, cap_bytes=2 * 1024 * 1024):
    t = rows
    while t * cols * 4 > cap_bytes and t % 16 == 0:
        t //= 2
    return t


def _ada_fwd(c_all, w_ada, b_loc, *, name, tn=512):
    l, d, nl = w_ada.shape
    b = c_all.shape[0]
    tn = min(tn, nl)

    def body(c_ref, w_ref, b_ref, o_ref):
        o_ref[0] = jnp.dot(c_ref[...], w_ref[0], preferred_element_type=F32,
                           precision=lax.Precision.HIGHEST) + b_ref[0]

    return pl.pallas_call(
        body, name=name, grid=(l, nl // tn),
        in_specs=[pl.BlockSpec((b, d), lambda i, j: (0, 0)), pl.BlockSpec((1, d, tn), lambda i, j: (i, 0, j)),
                  pl.BlockSpec((1, 1, tn), lambda i, j: (i, 0, j))],
        out_specs=pl.BlockSpec((1, b, tn), lambda i, j: (i, 0, j)),
        out_shape=_sds((l, b, nl), F32),
        compiler_params=_params(("parallel", "parallel")),
    )(c_all, w_ada, b_loc)


def _ada_bwd(c_t, dmod_loc, *, name, tn=512):
    d, b = c_t.shape
    l, _, nl = dmod_loc.shape
    tn = min(tn, nl)

    def body(c_ref, dm_ref, o_ref):
        cv = c_ref[...]
        dm = dm_ref[0]
        acc = cv[:, 0:1] * dm[0:1, :]
        for k in range(1, b):
            acc = acc + cv[:, k:k + 1] * dm[k:k + 1, :]
        o_ref[0] = acc

    return pl.pallas_call(
        body, name=name, grid=(l, nl // tn),
        in_specs=[pl.BlockSpec((d, b), lambda i, j: (0, 0)), pl.BlockSpec((1, b, tn), lambda i, j: (i, 0, j))],
        out_specs=pl.BlockSpec((1, d, tn), lambda i, j: (i, 0, j)),
        out_shape=_sds((l, d, nl), F32),
        compiler_params=_params(("parallel", "parallel")),
    )(c_t, dmod_loc)


def _sum_devices(p, *, name):
    k, r, c_ = p.shape

    def body(p_ref, o_ref):
        acc = p_ref[0]
        for j in range(1, k):
            acc = acc + p_ref[j]
        o_ref[...] = acc

    return pl.pallas_call(body, name=name, out_shape=_sds((r, c_), F32),
                          in_specs=[pl.BlockSpec(memory_space=pltpu.VMEM)],
                          out_specs=pl.BlockSpec(memory_space=pltpu.VMEM))(p)


def _adamw(w, g, m, v, *, name):
    shape = w.shape
    rows, cols = _flat_rows(shape)
    tr = _row_tile(rows, cols, cap_bytes=1024 * 1024)
    c1 = 1.0 / (1.0 - ADAM_B1 ** ADAM_STEP)
    c2 = 1.0 / (1.0 - ADAM_B2 ** ADAM_STEP)

    def body(w_ref, g_ref, m_ref, v_ref, go_ref, d_ref, nm_ref, nv_ref):
        gv = g_ref[...]
        nm = ADAM_B1 * m_ref[...] + (1.0 - ADAM_B1) * gv
        nv = ADAM_B2 * v_ref[...] + (1.0 - ADAM_B2) * (gv * gv)
        m_hat = nm * c1
        v_hat = nv * c2
        go_ref[...] = gv
        d_ref[...] = -ADAM_LR * (m_hat / (jnp.sqrt(v_hat) + ADAM_EPS) + ADAM_WD * w_ref[...])
        nm_ref[...] = nm
        nv_ref[...] = nv

    spec = pl.BlockSpec((tr, cols), lambda i: (i, 0))
    flat = lambda a: a.reshape(rows, cols)
    outs = pl.pallas_call(body, name=name, grid=(rows // tr,), in_specs=[spec] * 4, out_specs=[spec] * 4,
                          out_shape=[_sds((rows, cols), F32)] * 4, compiler_params=_params(("parallel",)),
                          )(flat(w), flat(g), flat(m), flat(v))
    return tuple(o.reshape(shape) for o in outs)


WEIGHTS = ("w_ada", "b_ada", "g_pre_mix", "g_post_mix", "g_pre_mlp", "g_post_mlp", "w_in", "conv_w",
           "w_proj_conv", "w_proj_attn", "w_out", "w_mlp_in", "w_mlp_out")
GAINS = ("g_pre_mix", "g_post_mix", "g_pre_mlp", "g_post_mlp")


def kernel(x, c, w_ada, b_ada, g_pre_mix, g_post_mix, g_pre_mlp, g_post_mlp, w_in, conv_w, w_proj_conv, w_proj_attn, w_out, w_mlp_in, w_mlp_out, loss_target, m_w_ada, m_b_ada, m_g_pre_mix, m_g_post_mix, m_g_pre_mlp, m_g_post_mlp, m_w_in, m_conv_w, m_w_proj_conv, m_w_proj_attn, m_w_out, m_w_mlp_in, m_w_mlp_out, v_w_ada, v_b_ada, v_g_pre_mix, v_g_post_mix, v_g_pre_mlp, v_g_post_mlp, v_w_in, v_conv_w, v_w_proj_conv, v_w_proj_attn, v_w_out, v_w_mlp_in, v_w_mlp_out):
    params = dict(w_ada=w_ada, b_ada=b_ada, g_pre_mix=g_pre_mix, g_post_mix=g_post_mix, g_pre_mlp=g_pre_mlp,
                  g_post_mlp=g_post_mlp, w_in=w_in, conv_w=conv_w, w_proj_conv=w_proj_conv, w_proj_attn=w_proj_attn,
                  w_out=w_out, w_mlp_in=w_mlp_in, w_mlp_out=w_mlp_out)
    m_in = dict(w_ada=m_w_ada, b_ada=m_b_ada, g_pre_mix=m_g_pre_mix, g_post_mix=m_g_post_mix, g_pre_mlp=m_g_pre_mlp,
                g_post_mlp=m_g_post_mlp, w_in=m_w_in, conv_w=m_conv_w, w_proj_conv=m_w_proj_conv,
                w_proj_attn=m_w_proj_attn, w_out=m_w_out, w_mlp_in=m_w_mlp_in, w_mlp_out=m_w_mlp_out)
    v_in = dict(w_ada=v_w_ada, b_ada=v_b_ada, g_pre_mix=v_g_pre_mix, g_post_mix=v_g_post_mix, g_pre_mlp=v_g_pre_mlp,
                g_post_mlp=v_g_post_mlp, w_in=v_w_in, conv_w=v_conv_w, w_proj_conv=v_w_proj_conv,
                w_proj_attn=v_w_proj_attn, w_out=v_w_out, w_mlp_in=v_w_mlp_in, w_mlp_out=v_w_mlp_out)

    depth, d, nl_ada = w_ada.shape
    ix, iy, ic = lax.axis_index("x"), lax.axis_index("y"), lax.axis_index("c")
    chip = 2 * ix + iy
    me = 4 * ix + 2 * iy + ic
    xs = x[0]
    target = loss_target[0]

    c_all = _all_gather_small(jnp.broadcast_to(c, (8, d)), name="gather_c")[:, 0, :]
    b_loc = lax.dynamic_slice_in_dim(b_ada, chip * nl_ada, nl_ada, axis=1)[:, None, :]
    mod_loc = _ada_fwd(c_all, w_ada, b_loc, name="ada_fwd")
    mod_all = _all_gather_small(mod_loc.reshape(depth * 8, nl_ada), name="gather_mod")
    mod_all = mod_all.reshape(4, 2, depth, 8, nl_ada)[:, 0]
    mod_me = lax.dynamic_index_in_dim(mod_all, me, axis=2, keepdims=False)
    mods = jnp.transpose(mod_me, (1, 0, 2)).reshape(depth, N_MOD, d)

    chip_arr = jnp.reshape(chip, (1,)).astype(jnp.int32)
    ids = jnp.stack([chip, ic]).astype(jnp.int32)
    placed = {(l, k): _cast_place(params[k], l, SHARD_AXIS[k], chip_arr, name=f"place_{k}_{l}")
              for l in range(depth) for k in BIG}
    conv_full = _all_gather_small(
        jnp.pad(conv_w.reshape(depth * 3, -1), ((0, 8 - depth * 3), (0, 0))), name="gather_conv_w")
    conv_full = conv_full.reshape(4, 2, 8, -1)[:, 0, :depth * 3]
    conv_full = jnp.transpose(conv_full, (1, 0, 2)).reshape(depth, 3, -1)

    gains = jnp.stack([params[k] for k in GAINS], axis=1)
    schedule = _Schedule(placed, depth, ids)
    loss, dx, conv_grads, dmods, dgains = _local_step(xs, target, mods, gains, conv_full, schedule)

    cw = conv_full.shape[2]
    rows = [dmods.reshape(depth * N_MOD, d), dgains.reshape(depth * 4, d),
            conv_grads.reshape(-1, d), jnp.broadcast_to(loss, (1, d))]
    payload = jnp.concatenate(rows, axis=0)
    n_rows = payload.shape[0]
    pad = (-n_rows) % 8
    payload = jnp.pad(payload, ((0, pad), (0, 0)))
    everyone = _all_gather_small(payload, name="gather_small_grads")
    total = _sum_devices(everyone, name="sum_small_grads")
    r0 = depth * N_MOD
    grads = {}
    grads["b_ada"] = total[:r0].reshape(depth, N_MOD * d)
    gsum = total[r0:r0 + depth * 4].reshape(depth, 4, d)
    for k, name in enumerate(GAINS):
        grads[name] = gsum[:, k]
    r1 = r0 + depth * 4
    n_conv = (depth * 3 * cw) // d
    conv_g = total[r1:r1 + n_conv].reshape(depth, 3, cw)
    grads["conv_w"] = lax.dynamic_slice_in_dim(conv_g, chip * (cw // 4), cw // 4, axis=2)
    loss_out = total[r1 + n_conv, 0]
    dmod_all = everyone[:, :r0].reshape(8, depth, N_MOD * d)
    dmod_loc = lax.dynamic_slice_in_dim(dmod_all, chip * nl_ada, nl_ada, axis=2)
    grads["w_ada"] = _ada_bwd(c_all.T, jnp.transpose(dmod_loc, (1, 0, 2)), name="ada_bwd")

    for k in BIG:
        grads[k] = schedule.reduced[k]

    deltas, new_m, new_v = {}, {}, {}
    for k in WEIGHTS:
        grads[k], deltas[k], new_m[k], new_v[k] = _adamw(params[k], grads[k], m_in[k], v_in[k], name=f"adamw_{k}")

    return (loss_out, dx[None], *[grads[k] for k in WEIGHTS], *[deltas[k] for k in WEIGHTS],
            *[new_m[k] for k in WEIGHTS], *[new_v[k] for k in WEIGHTS])
```

```python
import functools

import jax
import jax.numpy as jnp
from jax import lax
from jax.experimental import pallas as pl
from jax.experimental.pallas import tpu as pltpu

F32 = jnp.float32
BF16 = jnp.bfloat16
EPS = 1e-6
N_MOD = 6
HEAD_DIM = 64
LANES = 128
ATTN_SCALE = 1.0 / 8.0
UNDERFLOW_LOG = -90.0
ATTN_BLOCK = 256
ATTN_CHAINS = (4, 4)
ADAM_LR = 0.001
ADAM_B1 = 0.9
ADAM_B2 = 0.999
ADAM_EPS = 1e-08
ADAM_WD = 0.01
ADAM_STEP = 10
VMEM_LIMIT = 56 * 1024 * 1024
MESH = pl.DeviceIdType.MESH
OTHER_CHIPS = ((1, 0), (0, 1), (1, 1))

_NT = (((1,), (1,)), ((), ()))
_TN = (((0,), (0,)), ((), ()))


def _sds(shape, dtype):
    return jax.ShapeDtypeStruct(shape, dtype)


def _params(sem):
    return pltpu.CompilerParams(dimension_semantics=sem, vmem_limit_bytes=VMEM_LIMIT)


def _fit(t, n):
    t = min(t, n)
    while n % t:
        t //= 2
    return t


def _vec_spec(d, nargs=1):
    assert nargs == 1
    return pl.BlockSpec((1, d), lambda i: (0, 0))


def _log_one_minus_sigmoid(z):
    return -jnp.log(1.0 + jnp.exp(-jnp.abs(z))) - jnp.maximum(z, 0.0)


def _sigmoid(z):
    return 0.5 * jnp.tanh(0.5 * z) + 0.5


def _split_bf16(a):
    hi = a.astype(BF16)
    lo = (a - hi.astype(F32)).astype(BF16)
    return hi, lo


def _rms_bwd(dn, xin, g):
    r = lax.rsqrt(jnp.mean(xin * xin, axis=-1, keepdims=True) + EPS)
    xh = xin * r
    dxh = dn * g
    dxin = r * (dxh - xh * jnp.mean(dxh * xh, axis=-1, keepdims=True))
    return dxin, xh


def _colsum(a):
    return jnp.sum(a, axis=0, keepdims=True)


def _norm_mod_matmul(x, g, sc, sh, w, *, name, tm=512, side=None):
    s, d = x.shape
    n = w.shape[1]
    tm = _fit(tm, s)
    nt = s // tm

    def body(*refs):
        i = pl.program_id(0)
        (x_ref, g_ref, sc_ref, sh_ref, w_ref, h_ref, o_ref), late_phases = _host_side(
            side, 5, 2, 0, refs, i == 0, i == (3 * nt) // 4, i == nt - 1)
        xv = x_ref[...]
        r = lax.rsqrt(jnp.mean(xv * xv, axis=-1, keepdims=True) + EPS)
        h = ((xv * r * g_ref[...]) * (1.0 + sc_ref[...]) + sh_ref[...]).astype(BF16)
        h_ref[...] = h
        o_ref[...] = jnp.dot(h, w_ref[...], preferred_element_type=F32).astype(BF16)
        late_phases()

    s_in, s_out, s_shapes, aliases, s_scratch = _side_specs(side, 5, 2)
    res = pl.pallas_call(
        body, name=name, grid=(nt,),
        in_specs=[pl.BlockSpec((tm, d), lambda i: (i, 0)), _vec_spec(d, 1), _vec_spec(d, 1), _vec_spec(d, 1),
                  pl.BlockSpec((d, n), lambda i: (0, 0))] + s_in,
        out_specs=[pl.BlockSpec((tm, d), lambda i: (i, 0)), pl.BlockSpec((tm, n), lambda i: (i, 0))] + s_out,
        out_shape=[_sds((s, d), BF16), _sds((s, n), BF16)] + s_shapes,
        input_output_aliases=aliases, scratch_shapes=s_scratch,
        compiler_params=_params(("arbitrary",)),
    )(x, g, sc, sh, w, *([] if side is None else side.operands))
    return res[:2], res[2:]


HALO = 16


def _conv_fwd(proj, conv_w, *, name, tm=512):
    s = proj.shape[0]
    cw = conv_w.shape[1]
    tm = min(tm, s)
    nb = tm // HALO

    def body(bg_ref, cg_ref, u_ref, cgh_ref, uh_ref, w_ref, yc_ref, vbuf):
        i = pl.program_id(0)
        vv = cg_ref[...].astype(F32) * u_ref[...].astype(F32)
        halo = cgh_ref[...].astype(F32) * uh_ref[...].astype(F32)
        vbuf[0:HALO, :] = jnp.where(i > 0, halo, 0.0)
        vbuf[HALO:HALO + tm, :] = vv
        v1 = vbuf[HALO - 1:HALO - 1 + tm, :]
        v2 = vbuf[HALO - 2:HALO - 2 + tm, :]
        w = w_ref[...]
        y = w[2:3, :] * vv + w[1:2, :] * v1 + w[0:1, :] * v2
        yc_ref[...] = (bg_ref[...].astype(F32) * y).astype(BF16)

    def prev(i):
        return jnp.maximum(i * nb - 1, 0)

    return pl.pallas_call(
        body, name=name, grid=(s // tm,),
        in_specs=[pl.BlockSpec((tm, cw), lambda i: (i, 0)), pl.BlockSpec((tm, cw), lambda i: (i, 1)),
                  pl.BlockSpec((tm, cw), lambda i: (i, 2)),
                  pl.BlockSpec((HALO, cw), lambda i: (prev(i), 1)), pl.BlockSpec((HALO, cw), lambda i: (prev(i), 2)),
                  pl.BlockSpec((3, cw), lambda i: (0, 0))],
        out_specs=pl.BlockSpec((tm, cw), lambda i: (i, 0)),
        out_shape=_sds((s, cw), BF16),
        scratch_shapes=[pltpu.VMEM((HALO + tm, cw), F32)],
        compiler_params=_params(("arbitrary",)),
    )(proj, proj, proj, proj, proj, conv_w)


def _tri(qb):
    r = lax.broadcasted_iota(jnp.int32, (qb, qb), 0)
    c = lax.broadcasted_iota(jnp.int32, (qb, qb), 1)
    return (r >= c).astype(BF16)


def _head_mask(h):
    lane = lax.broadcasted_iota(jnp.int32, (1, LANES), 1)
    return (lane >= HEAD_DIM * h) & (lane < HEAD_DIM * (h + 1))


def _stack_heads(a, masks):
    return jnp.concatenate([jnp.where(m, a, 0).astype(BF16) for m in masks], axis=0)


def _heads_to_lanes(a, qb):
    return jnp.concatenate([a[:qb], a[qb:]], axis=1)


def _stacked_causal(qb, width, first_key, first_query):
    row = lax.broadcasted_iota(jnp.int32, (2 * qb, width), 0)
    col = lax.broadcasted_iota(jnp.int32, (2 * qb, width), 1)
    return first_key + col < first_query + jnp.where(row >= qb, row - qb, row)


def _running_sum(a, tri_m):
    rows, qb = a.shape[0], tri_m.shape[0]
    n = a.shape[1] // qb
    hi, lo = _split_bf16(a)
    stacked = jnp.concatenate([p[:, s * qb:(s + 1) * qb] for s in range(n) for p in (hi, lo)], axis=0)
    both = jnp.dot(stacked, tri_m, preferred_element_type=F32)
    parts = [both[(2 * s) * rows:(2 * s + 1) * rows] + both[(2 * s + 1) * rows:(2 * s + 2) * rows] for s in range(n)]
    later = None
    for s in reversed(range(n)):
        if later is not None:
            parts[s] = parts[s] + later
        later = parts[s][:, 0:1]
    return (parts[0] if n == 1 else jnp.concatenate(parts, axis=1)), later


def _attn_cols(d):
    cw = d // 2
    hp = (d // 2) // LANES
    q0 = (3 * cw) // LANES
    return q0, q0 + hp, q0 + 2 * hp, hp


class _Side:
    def __init__(self, operands, out_shapes, aliases, scratch, start, mid, finish):
        self.operands, self.out_shapes, self.aliases, self.scratch = list(operands), list(out_shapes), aliases, list(scratch)
        self.start, self.mid, self.finish = start, mid, finish


def _side_call(side, *, name):
    n_in, n_out = len(side.operands), len(side.out_shapes)

    def body(*refs):
        parts = refs[:n_in], refs[n_in:n_in + n_out], refs[n_in + n_out:]
        side.start(*parts)
        if side.mid is not None:
            side.mid(*parts)
        side.finish(*parts)

    hbm = pl.BlockSpec(memory_space=pltpu.HBM)
    return pl.pallas_call(body, name=name, in_specs=[hbm] * n_in, out_specs=[hbm] * n_out, out_shape=side.out_shapes,
                          input_output_aliases=dict(side.aliases), scratch_shapes=side.scratch)(*side.operands)


def _host_side(side, n_in, n_out, n_scratch, refs, first, late, last):
    if side is None:
        return refs, lambda: None
    s_in, s_out = len(side.operands), len(side.out_shapes)
    ins = refs[:n_in]
    side_in = refs[n_in:n_in + s_in]
    outs = refs[n_in + s_in:n_in + s_in + n_out]
    side_out = refs[n_in + s_in + n_out:n_in + s_in + n_out + s_out]
    rest = refs[n_in + s_in + n_out + s_out:]
    scratch, sems = rest[:n_scratch], rest[n_scratch:]
    parts = (side_in, side_out, sems)
    pl.when(first)(lambda: side.start(*parts))

    def run_late_phases():
        if side.mid is not None:
            pl.when(late)(lambda: side.mid(*parts))
        pl.when(last)(lambda: side.finish(*parts))

    return (*ins, *outs, *scratch), run_late_phases


def _side_specs(side, n_in, n_out):
    if side is None:
        return [], [], [], {}, []
    hbm = pl.BlockSpec(memory_space=pltpu.HBM)
    s_in = len(side.operands)
    aliases = {n_in + a: n_out + b for a, b in side.aliases.items()}
    return [hbm] * s_in, [hbm] * len(side.out_shapes), side.out_shapes, aliases, side.scratch


def _attn_fwd(proj, tri, *, d, name, side=None):
    s = proj.shape[0]
    qb = tri.shape[0]
    chains = _fit(ATTN_CHAINS[0], s // qb)
    ng = s // (qb * chains)
    q0, k0, v0, hp = _attn_cols(d)

    def body(*refs):
        p, g = pl.program_id(0), pl.program_id(1)
        (q_ref, k_ref, v_ref, tri_ref, o_ref, a_ref, b_ref, run_ref), late_phases = _host_side(
            side, 4, 4, 0, refs, (p == 0) & (g == 0), (p == hp - 1) & (g == 0), (p == hp - 1) & (g == ng - 1))
        tri_m = tri_ref[...]
        masks = [_head_mask(h) for h in range(2)]

        def first_steps(u):
            i = g * chains + u
            qs = _stack_heads(q_ref[u * qb:(u + 1) * qb, :] * ATTN_SCALE, masks)

            def strip(j, state, causal=None, keep=None, live=None):
                run, acc = state
                rows = pl.ds(pl.multiple_of(j * qb, qb), qb)
                z = lax.dot_general(qs, k_ref[rows, :], _NT, preferred_element_type=F32)
                lg = _log_one_minus_sigmoid(z)
                beta = 1.0 - jnp.exp(lg) if keep is not None else None
                if causal is not None:
                    lg = jnp.where(causal, lg, 0.0)
                cs, total = _running_sum(lg, tri_m)
                a = jnp.exp(z + cs + run)
                if causal is not None:
                    a = jnp.where(causal, a, 0.0)
                    beta = jnp.where(causal, beta, 0.0)
                if live is not None:
                    a = jnp.where(live, a, 0.0)
                    beta = jnp.where(live, beta, 0.0)
                    total = jnp.where(live, total, 0.0)
                ab = a.astype(BF16)
                if keep is not None:
                    a_ref[0, u, keep] = ab
                    b_ref[0, u, keep] = beta.astype(BF16)
                acc = acc + jnp.dot(_heads_to_lanes(ab, qb), _stack_heads(v_ref[rows, :], masks),
                                    preferred_element_type=F32)
                return run + total, acc

            state = strip(i, (jnp.zeros((2 * qb, 1), F32), jnp.zeros((qb, LANES), F32)),
                          causal=_stacked_causal(qb, qb, 0, 0), keep=0)
            state = strip(jnp.maximum(i - 1, 0), state, keep=1, live=i >= 1)
            run_ref[0, u] = jnp.broadcast_to(state[0], (2 * qb, LANES))
            return i, strip, state

        started = [first_steps(u) for u in range(chains)]
        for u, (i, strip, state) in enumerate(started):
            state = lax.while_loop(
                lambda st: (st[0] >= 0) & (jnp.max(st[1]) > UNDERFLOW_LOG),
                lambda st, strip=strip: (st[0] - 1, *strip(st[0], st[1:])),
                (i - 2, *state))
            o_ref[u * qb:(u + 1) * qb, :] = state[2]
        late_phases()

    s_in, s_out, s_shapes, aliases, s_scratch = _side_specs(side, 4, 4)
    tq = qb * chains
    nq = s // qb
    res = pl.pallas_call(
        body, name=name, grid=(hp, ng),
        in_specs=[pl.BlockSpec((tq, LANES), lambda p, i: (i, q0 + p)),
                  pl.BlockSpec((s, LANES), lambda p, i: (0, k0 + p)),
                  pl.BlockSpec((s, LANES), lambda p, i: (0, v0 + p)),
                  pl.BlockSpec((qb, qb), lambda p, i: (0, 0))] + s_in,
        out_specs=[pl.BlockSpec((tq, LANES), lambda p, i: (i, p)),
                   pl.BlockSpec((1, chains, 2, 2 * qb, qb), lambda p, i: (p, i, 0, 0, 0)),
                   pl.BlockSpec((1, chains, 2, 2 * qb, qb), lambda p, i: (p, i, 0, 0, 0)),
                   pl.BlockSpec((1, chains, 2 * qb, LANES), lambda p, i: (p, i, 0, 0))] + s_out,
        out_shape=[_sds((s, hp * LANES), F32), _sds((hp, nq, 2, 2 * qb, qb), BF16), _sds((hp, nq, 2, 2 * qb, qb), BF16),
                   _sds((hp, nq, 2 * qb, LANES), F32)] + s_shapes,
        input_output_aliases=aliases, scratch_shapes=s_scratch,
        compiler_params=_params(("arbitrary", "arbitrary")),
    )(proj, proj, proj, tri, *([] if side is None else side.operands))
    return (res[0], tuple(res[1:4])), res[4:]


def _mix_out(yc, o, proj, x, wpc, wpa, wout, g, gt, *, name, tm=512):
    s, d = x.shape
    cw = yc.shape[1]
    tm = min(tm, s)
    ga_blk = (3 * cw + 3 * (d // 2)) // d

    def body(yc_ref, o_ref, ga_ref, gb_ref, x_ref, wpc_ref, wpa_ref, wout_ref, g_ref, gt_ref,
             ycv_ref, yat_ref, mg_ref, mix_ref, x1_ref):
        y_conv = jnp.dot(yc_ref[...], wpc_ref[...], preferred_element_type=F32)
        y_attn = jnp.dot(o_ref[...].astype(BF16), wpa_ref[...], preferred_element_type=F32)
        merged = (_sigmoid(ga_ref[...].astype(F32)) * y_conv + _sigmoid(gb_ref[...].astype(F32)) * y_attn)
        mg = merged.astype(BF16)
        mix = jnp.dot(mg, wout_ref[...], preferred_element_type=F32)
        r = lax.rsqrt(jnp.mean(mix * mix, axis=-1, keepdims=True) + EPS)
        ycv_ref[...] = y_conv.astype(BF16)
        yat_ref[...] = y_attn.astype(BF16)
        mg_ref[...] = mg
        mix_ref[...] = mix
        x1_ref[...] = x_ref[...] + gt_ref[...] * (mix * r * g_ref[...])

    def rows(w):
        return pl.BlockSpec((tm, w), lambda i: (i, 0))

    def full(a):
        return pl.BlockSpec(a.shape, lambda i: (0, 0))

    return pl.pallas_call(
        body, name=name, grid=(s // tm,),
        in_specs=[rows(cw), rows(d // 2), pl.BlockSpec((tm, d), lambda i: (i, ga_blk)),
                  pl.BlockSpec((tm, d), lambda i: (i, ga_blk + 1)), rows(d),
                  full(wpc), full(wpa), full(wout), _vec_spec(d, 1), _vec_spec(d, 1)],
        out_specs=[rows(d), rows(d), rows(d), rows(d), rows(d)],
        out_shape=[_sds((s, d), BF16), _sds((s, d), BF16), _sds((s, d), BF16), _sds((s, d), F32), _sds((s, d), F32)],
        compiler_params=_params(("parallel",)),
    )(yc, o, proj, proj, x, wpc, wpa, wout, g, gt)


def _relu2(a):
    r = jnp.maximum(a.astype(F32), 0.0)
    return (r * r).astype(BF16)


def _mlp_out(a, x, w2, g, gt, *, name, tm=512):
    s, d = x.shape
    dff = a.shape[1]
    tm = min(tm, s)

    def body(a_ref, x_ref, w_ref, g_ref, gt_ref, ff_ref, x2_ref):
        ff = jnp.dot(_relu2(a_ref[...]), w_ref[...], preferred_element_type=F32)
        r = lax.rsqrt(jnp.mean(ff * ff, axis=-1, keepdims=True) + EPS)
        ff_ref[...] = ff
        x2_ref[...] = x_ref[...] + gt_ref[...] * (ff * r * g_ref[...])

    return pl.pallas_call(
        body, name=name, grid=(s // tm,),
        in_specs=[pl.BlockSpec((tm, dff), lambda i: (i, 0)), pl.BlockSpec((tm, d), lambda i: (i, 0)),
                  pl.BlockSpec((dff, d), lambda i: (0, 0)), _vec_spec(d, 1), _vec_spec(d, 1)],
        out_specs=[pl.BlockSpec((tm, d), lambda i: (i, 0)), pl.BlockSpec((tm, d), lambda i: (i, 0))],
        out_shape=[_sds((s, d), F32), _sds((s, d), F32)],
        compiler_params=_params(("parallel",)),
    )(a, x, w2, g, gt)


def _loss_grad(y, target, *, name, tm=512):
    s, d = y.shape
    tm = min(tm, s)

    def body(y_ref, t_ref, dy_ref, loss_ref):
        @pl.when(pl.program_id(0) == 0)
        def _():
            loss_ref[...] = jnp.zeros_like(loss_ref)
        e = y_ref[...] - t_ref[...]
        dy_ref[...] = e * (1.0 / d)
        loss_ref[...] += 0.5 * jnp.sum(jnp.mean(e * e, axis=-1, keepdims=True), axis=0, keepdims=True)

    return pl.pallas_call(
        body, name=name, grid=(s // tm,),
        in_specs=[pl.BlockSpec((tm, d), lambda i: (i, 0)), pl.BlockSpec((tm, d), lambda i: (i, 0))],
        out_specs=[pl.BlockSpec((tm, d), lambda i: (i, 0)), pl.BlockSpec((1, 1), lambda i: (0, 0))],
        out_shape=[_sds((s, d), F32), _sds((1, 1), F32)],
        compiler_params=_params(("arbitrary",)),
    )(y, target)


def _mlp_out_bwd(dx, ff, a, w2, g, gt, *, name, tm=512):
    s, d = dx.shape
    dff = a.shape[1]
    tm = min(tm, s)

    def body(dx_ref, ff_ref, a_ref, w_ref, g_ref, gt_ref, dff_ref, da_ref, dgt_ref, dg_ref):
        @pl.when(pl.program_id(0) == 0)
        def _():
            dgt_ref[...] = jnp.zeros_like(dgt_ref)
            dg_ref[...] = jnp.zeros_like(dg_ref)
        dxv = dx_ref[...]
        dn = dxv * gt_ref[...]
        dffv, xh = _rms_bwd(dn, ff_ref[...], g_ref[...])
        dgt_ref[...] += _colsum(dxv * (xh * g_ref[...]))
        dg_ref[...] += _colsum(dn * xh)
        dffb = dffv.astype(BF16)
        dff_ref[...] = dffb
        drr = lax.dot_general(dffb, w_ref[...], _NT, preferred_element_type=F32)
        da_ref[...] = (drr * (2.0 * jnp.maximum(a_ref[...].astype(F32), 0.0))).astype(BF16)

    return pl.pallas_call(
        body, name=name, grid=(s // tm,),
        in_specs=[pl.BlockSpec((tm, d), lambda i: (i, 0)), pl.BlockSpec((tm, d), lambda i: (i, 0)),
                  pl.BlockSpec((tm, dff), lambda i: (i, 0)), pl.BlockSpec((dff, d), lambda i: (0, 0)),
                  _vec_spec(d, 1), _vec_spec(d, 1)],
        out_specs=[pl.BlockSpec((tm, d), lambda i: (i, 0)), pl.BlockSpec((tm, dff), lambda i: (i, 0)),
                   _vec_spec(d, 1), _vec_spec(d, 1)],
        out_shape=[_sds((s, d), BF16), _sds((s, dff), BF16), _sds((1, d), F32), _sds((1, d), F32)],
        compiler_params=_params(("arbitrary",)),
    )(dx, ff, a, w2, g, gt)


def _matmul_nt_norm_bwd(dys, w, x, dres, g, sc, *, name, tm=512, side=None):
    s = dys[0].shape[0]
    widths = [dy.shape[1] for dy in dys]
    d, n = w.shape
    assert sum(widths) == n, (widths, n)
    tm = _fit(tm, s)
    nt = s // tm
    np_ = len(dys)

    def body(*refs):
        i = pl.program_id(0)
        own, late_phases = _host_side(side, np_ + 5, 4, 0, refs, i == 0, i == (3 * nt) // 4, i == nt - 1)
        dy_refs = own[:np_]
        w_ref, x_ref, dres_ref, g_ref, sc_ref, dx_ref, dsh_ref, dsc_ref, dg_ref = own[np_:]

        @pl.when(i == 0)
        def _():
            dsh_ref[...] = jnp.zeros_like(dsh_ref)
            dsc_ref[...] = jnp.zeros_like(dsc_ref)
            dg_ref[...] = jnp.zeros_like(dg_ref)

        dh = None
        for p, dy_ref in enumerate(dy_refs):
            cols = slice(sum(widths[:p]), sum(widths[:p + 1]))
            part = lax.dot_general(dy_ref[...], w_ref[:, cols], _NT, preferred_element_type=F32)
            dh = part if dh is None else dh + part
        dn = dh * (1.0 + sc_ref[...])
        dxin, xh = _rms_bwd(dn, x_ref[...], g_ref[...])
        dsh_ref[...] += _colsum(dh)
        dsc_ref[...] += _colsum(dh * (xh * g_ref[...]))
        dg_ref[...] += _colsum(dn * xh)
        dx_ref[...] = dres_ref[...] + dxin
        late_phases()

    s_in, s_out, s_shapes, aliases, s_scratch = _side_specs(side, np_ + 5, 4)
    res = pl.pallas_call(
        body, name=name, grid=(nt,),
        in_specs=[pl.BlockSpec((tm, wd), lambda i: (i, 0)) for wd in widths]
        + [pl.BlockSpec((d, n), lambda i: (0, 0)),
           pl.BlockSpec((tm, d), lambda i: (i, 0)), pl.BlockSpec((tm, d), lambda i: (i, 0)),
           _vec_spec(d, 1), _vec_spec(d, 1)] + s_in,
        out_specs=[pl.BlockSpec((tm, d), lambda i: (i, 0)), _vec_spec(d, 1), _vec_spec(d, 1), _vec_spec(d, 1)] + s_out,
        out_shape=[_sds((s, d), F32), _sds((1, d), F32), _sds((1, d), F32), _sds((1, d), F32)] + s_shapes,
        input_output_aliases=aliases, scratch_shapes=s_scratch,
        compiler_params=_params(("arbitrary",)),
    )(*dys, w, x, dres, g, sc, *([] if side is None else side.operands))
    return res[:4], res[4:]


def _matmul_tn(a, bs, *, name, tk=1024, tn=1024, ts=2048, relu2=False, into=None, col0=0, n_total=None):
    s, k = a.shape
    widths = [b.shape[1] for b in bs]
    n = sum(widths)
    tk, ts = _fit(tk, k), _fit(ts, s)
    for w in widths:
        tn = _fit(tn, w)
    while col0 % tn:
        tn //= 2
    nt = s // ts
    assert tn % LANES == 0 and all(sum(widths[:p]) % tn == 0 for p in range(len(bs))), (widths, tn)
    first = [sum(widths[:p]) // tn for p in range(len(bs))]
    tiles = [w // tn for w in widths]
    tile0 = col0 // tn

    def body(a_ref, *rest):
        b_refs, o_ref, acc = rest[:len(bs)], rest[-2], rest[-1]
        j, t = pl.program_id(1), pl.program_id(2)

        @pl.when(t == 0)
        def _():
            acc[...] = jnp.zeros_like(acc)
        av = a_ref[...]
        av = _relu2(av) if relu2 else av.astype(BF16)
        for p, b_ref in enumerate(b_refs):
            def add(b_ref=b_ref):
                acc[...] += lax.dot_general(av, b_ref[...], _TN, preferred_element_type=F32)
            if len(bs) == 1:
                add()
            else:
                pl.when((j >= first[p]) & (j < first[p] + tiles[p]))(add)

        @pl.when(t == nt - 1)
        def _():
            o_ref[...] = acc[...].astype(BF16)

    def piece_spec(p):
        def index(i, j, t):
            mine = (j >= first[p]) & (j < first[p] + tiles[p])
            return jnp.where(mine, t, 0), jnp.where(mine, j - first[p], 0)
        return pl.BlockSpec((ts, tn), index)

    operands, extra_specs, aliases = [a, *bs], [], {}
    if into is not None:
        operands.append(into)
        extra_specs = [pl.BlockSpec(memory_space=pltpu.HBM)]
        aliases = {len(operands) - 1: 0}
    return pl.pallas_call(
        body, name=name, grid=(k // tk, n // tn, nt),
        in_specs=[pl.BlockSpec((ts, tk), lambda i, j, t: (t, i))] + [piece_spec(p) for p in range(len(bs))] + extra_specs,
        out_specs=pl.BlockSpec((tk, tn), lambda i, j, t: (i, tile0 + j)),
        out_shape=_sds((k, n_total or n), BF16),
        input_output_aliases=aliases,
        scratch_shapes=[pltpu.VMEM((tk, tn), F32)],
        compiler_params=_params(("parallel", "parallel", "arbitrary")),
    )(*operands)


def _mix_out_bwd(dx, mix, proj, ycv, yat, wout, wpc, wpa, g, gt, *, name, tm=512):
    s, d = dx.shape
    cw = wpc.shape[0]
    aw = wpa.shape[0]
    tm = min(tm, s)
    ga_blk = (3 * cw + 3 * aw) // d

    def body(dx_ref, mix_ref, ga_ref, gb_ref, ycv_ref, yat_ref, wout_ref, wpc_ref, wpa_ref, g_ref, gt_ref,
             dmix_ref, dycv_ref, dyat_ref, dyc_ref, do_ref, dgate_ref, dgt_ref, dg_ref):
        @pl.when(pl.program_id(0) == 0)
        def _():
            dgt_ref[...] = jnp.zeros_like(dgt_ref)
            dg_ref[...] = jnp.zeros_like(dg_ref)
        dxv = dx_ref[...]
        dn = dxv * gt_ref[...]
        dmix, xh = _rms_bwd(dn, mix_ref[...], g_ref[...])
        dgt_ref[...] += _colsum(dxv * (xh * g_ref[...]))
        dg_ref[...] += _colsum(dn * xh)
        dmixb = dmix.astype(BF16)
        dmix_ref[...] = dmixb
        dmerged = lax.dot_general(dmixb, wout_ref[...], _NT, preferred_element_type=F32)
        sga = _sigmoid(ga_ref[...].astype(F32))
        sgb = _sigmoid(gb_ref[...].astype(F32))
        dycv = (dmerged * sga).astype(BF16)
        dyat = (dmerged * sgb).astype(BF16)
        dycv_ref[...] = dycv
        dyat_ref[...] = dyat
        dgate_ref[:, 0:d] = (dmerged * ycv_ref[...].astype(F32) * (sga * (1.0 - sga))).astype(BF16)
        dgate_ref[:, d:2 * d] = (dmerged * yat_ref[...].astype(F32) * (sgb * (1.0 - sgb))).astype(BF16)
        dyc_ref[...] = lax.dot_general(dycv, wpc_ref[...], _NT, preferred_element_type=F32).astype(BF16)
        do_ref[...] = lax.dot_general(dyat, wpa_ref[...], _NT, preferred_element_type=F32).astype(BF16)

    def rows(w):
        return pl.BlockSpec((tm, w), lambda i: (i, 0))

    def full(a):
        return pl.BlockSpec(a.shape, lambda i: (0, 0))

    return pl.pallas_call(
        body, name=name, grid=(s // tm,),
        in_specs=[rows(d), rows(d), pl.BlockSpec((tm, d), lambda i: (i, ga_blk)),
                  pl.BlockSpec((tm, d), lambda i: (i, ga_blk + 1)), rows(d), rows(d),
                  full(wout), full(wpc), full(wpa), _vec_spec(d, 1), _vec_spec(d, 1)],
        out_specs=[rows(d), rows(d), rows(d), rows(cw), rows(aw), rows(2 * d), _vec_spec(d, 1), _vec_spec(d, 1)],
        out_shape=[_sds((s, d), BF16), _sds((s, d), BF16), _sds((s, d), BF16), _sds((s, cw), BF16),
                   _sds((s, aw), BF16), _sds((s, 2 * d), BF16), _sds((1, d), F32), _sds((1, d), F32)],
        compiler_params=_params(("arbitrary",)),
    )(dx, mix, proj, proj, ycv, yat, wout, wpc, wpa, g, gt)


def _conv_bwd(dyc, proj, conv_w, *, name, tm=512):
    s = proj.shape[0]
    cw = conv_w.shape[1]
    tm = min(tm, s)
    nb = tm // HALO
    nt = s // tm
    last_blk = s // HALO - 1

    def body(dyc_ref, bg_ref, cg_ref, u_ref, cgh_ref, uh_ref, dych_ref, bgh_ref, w_ref,
             dconv_ref, dw_ref, vbuf, gbuf):
        i = pl.program_id(0)

        @pl.when(i == 0)
        def _():
            dw_ref[...] = jnp.zeros_like(dw_ref)

        cg = cg_ref[...].astype(F32)
        u = u_ref[...].astype(F32)
        vv = cg * u
        halo = cgh_ref[...].astype(F32) * uh_ref[...].astype(F32)
        vbuf[0:HALO, :] = jnp.where(i > 0, halo, 0.0)
        vbuf[HALO:HALO + tm, :] = vv
        v1 = vbuf[HALO - 1:HALO - 1 + tm, :]
        v2 = vbuf[HALO - 2:HALO - 2 + tm, :]
        w = w_ref[...]
        y = w[2:3, :] * vv + w[1:2, :] * v1 + w[0:1, :] * v2
        dyc = dyc_ref[...].astype(F32)
        dconv_ref[:, 0:cw] = (dyc * y).astype(BF16)
        gy = dyc * bg_ref[...].astype(F32)
        nxt = dych_ref[...].astype(F32) * bgh_ref[...].astype(F32)
        gbuf[0:tm, :] = gy
        gbuf[tm:tm + HALO, :] = jnp.where(i < nt - 1, nxt, 0.0)
        g1 = gbuf[1:1 + tm, :]
        g2 = gbuf[2:2 + tm, :]
        dvv = w[2:3, :] * gy + w[1:2, :] * g1 + w[0:1, :] * g2
        dconv_ref[:, cw:2 * cw] = (dvv * u).astype(BF16)
        dconv_ref[:, 2 * cw:3 * cw] = (dvv * cg).astype(BF16)
        dw_ref[0:1, :] += _colsum(gy * v2)
        dw_ref[1:2, :] += _colsum(gy * v1)
        dw_ref[2:3, :] += _colsum(gy * vv)

    def prev(i):
        return jnp.maximum(i * nb - 1, 0)

    def nxt_blk(i):
        return jnp.minimum((i + 1) * nb, last_blk)

    def col(c):
        return pl.BlockSpec((tm, cw), lambda i: (i, c))

    return pl.pallas_call(
        body, name=name, grid=(nt,),
        in_specs=[col(0), col(0), col(1), col(2),
                  pl.BlockSpec((HALO, cw), lambda i: (prev(i), 1)), pl.BlockSpec((HALO, cw), lambda i: (prev(i), 2)),
                  pl.BlockSpec((HALO, cw), lambda i: (nxt_blk(i), 0)), pl.BlockSpec((HALO, cw), lambda i: (nxt_blk(i), 0)),
                  pl.BlockSpec((3, cw), lambda i: (0, 0))],
        out_specs=[pl.BlockSpec((tm, 3 * cw), lambda i: (i, 0)), pl.BlockSpec((3, cw), lambda i: (0, 0))],
        out_shape=[_sds((s, 3 * cw), BF16), _sds((3, cw), F32)],
        scratch_shapes=[pltpu.VMEM((HALO + tm, cw), F32), pltpu.VMEM((tm + HALO, cw), F32)],
        compiler_params=_params(("arbitrary",)),
    )(dyc, proj, proj, proj, proj, proj, dyc, proj, conv_w)


def _attn_bwd(proj, o, kept, do, tri, *, d, name, side=None):
    s = proj.shape[0]
    qb = tri.shape[0]
    chains = _fit(ATTN_CHAINS[1], s // qb)
    ng = s // (qb * chains)
    q0, k0, v0, hp = _attn_cols(d)

    def body(*refs):
        p, g = pl.program_id(0), pl.program_id(1)
        own, late_phases = _host_side(
            side, 9, 3, 2, refs, (p == 0) & (g == 0), (p == hp - 1) & (g == 0), (p == hp - 1) & (g == ng - 1))
        (q_ref, k_ref, v_ref, o_ref, do_ref, tri_ref, a_ref, b_ref, run_ref,
         dq_ref, dk_ref, dv_ref, dk_acc, dv_acc) = own

        @pl.when(g == 0)
        def _():
            dk_acc[...] = jnp.zeros_like(dk_acc)
            dv_acc[...] = jnp.zeros_like(dv_acc)

        tri_m = tri_ref[...]
        masks = [_head_mask(h) for h in range(2)]

        def first_steps(u):
            i = g * chains + u
            mine = slice(u * qb, (u + 1) * qb)
            dov = do_ref[mine, :]
            qs = _stack_heads(q_ref[mine, :] * ATTN_SCALE, masks)
            dos = _stack_heads(dov, masks)
            dprod = dov.astype(F32) * o_ref[mine, :]
            dtot = jnp.concatenate([jnp.sum(jnp.where(m, dprod, 0.0), axis=-1, keepdims=True) for m in masks], axis=0)

            def through(j, ab, beta, left, grun, dq_acc):
                rows = pl.ds(pl.multiple_of(j * qb, qb), qb)
                kb = k_ref[rows, :]
                da = lax.dot_general(dos, v_ref[rows, :], _NT, preferred_element_type=F32)
                gg = ab.astype(F32) * da
                gcs, gtotal = _running_sum(gg, tri_m)
                dzb = (gg - beta * (gg + (left - gcs))).astype(BF16)
                dq_acc = dq_acc + jnp.dot(_heads_to_lanes(dzb, qb), _stack_heads(kb, masks),
                                          preferred_element_type=F32)
                dk_add = lax.dot_general(dzb, qs, _TN, preferred_element_type=F32)
                dv_add = lax.dot_general(ab, dos, _TN, preferred_element_type=F32)
                return (grun + gtotal, dq_acc), (rows, dk_add, dv_add)

            def kept(slot, j, state):
                grun, dq_acc = state
                return through(j, a_ref[0, u, slot], b_ref[0, u, slot].astype(F32), dtot - grun, grun, dq_acc)

            def strip(j, state):
                run, grun, dq_acc = state
                z = lax.dot_general(qs, k_ref[pl.ds(pl.multiple_of(j * qb, qb), qb), :], _NT, preferred_element_type=F32)
                lg = _log_one_minus_sigmoid(z)
                cs, total = _running_sum(lg, tri_m)
                ab = jnp.exp(z + cs + run).astype(BF16)
                left = jnp.where(run > UNDERFLOW_LOG, dtot - grun, 0.0)
                (grun, dq_acc), adds = through(j, ab, 1.0 - jnp.exp(lg), left, grun, dq_acc)
                return (run + total, grun, dq_acc), adds

            state, adds0 = kept(0, i, (jnp.zeros((2 * qb, 1), F32), jnp.zeros((qb, LANES), F32)))
            state, adds1 = kept(1, jnp.maximum(i - 1, 0), state)
            return i, strip, (run_ref[0, u][:, 0:1], *state), (adds0, adds1)

        started = [first_steps(u) for u in range(chains)]
        for u, (i, strip, state, adds) in enumerate(started):
            for rows, dk_add, dv_add in adds:
                dk_acc[rows, :] += dk_add
                dv_acc[rows, :] += dv_add

            def more(st, strip=strip):
                state, (rows, dk_add, dv_add) = strip(st[0], st[1:])
                dk_acc[rows, :] += dk_add
                dv_acc[rows, :] += dv_add
                return (st[0] - 1, *state)

            state = lax.while_loop(lambda st: (st[0] >= 0) & (jnp.max(st[1]) > UNDERFLOW_LOG), more, (i - 2, *state))
            dq_ref[u * qb:(u + 1) * qb, :] = (state[3] * ATTN_SCALE).astype(BF16)

        @pl.when(g == ng - 1)
        def _():
            dk_ref[...] = dk_acc[...].astype(BF16)
            dv_ref[...] = dv_acc[...].astype(BF16)

        late_phases()

    aw = hp * LANES
    tq = qb * chains
    s_in, s_out, s_shapes, aliases, s_scratch = _side_specs(side, 9, 3)
    res = pl.pallas_call(
        body, name=name, grid=(hp, ng),
        in_specs=[pl.BlockSpec((tq, LANES), lambda p, i: (i, q0 + p)),
                  pl.BlockSpec((s, LANES), lambda p, i: (0, k0 + p)),
                  pl.BlockSpec((s, LANES), lambda p, i: (0, v0 + p)),
                  pl.BlockSpec((tq, LANES), lambda p, i: (i, p)),
                  pl.BlockSpec((tq, LANES), lambda p, i: (i, p)),
                  pl.BlockSpec((qb, qb), lambda p, i: (0, 0)),
                  pl.BlockSpec((1, chains, 2, 2 * qb, qb), lambda p, i: (p, i, 0, 0, 0)),
                  pl.BlockSpec((1, chains, 2, 2 * qb, qb), lambda p, i: (p, i, 0, 0, 0)),
                  pl.BlockSpec((1, chains, 2 * qb, LANES), lambda p, i: (p, i, 0, 0))] + s_in,
        out_specs=[pl.BlockSpec((tq, LANES), lambda p, i: (i, p)),
                   pl.BlockSpec((s, LANES), lambda p, i: (0, p)),
                   pl.BlockSpec((s, LANES), lambda p, i: (0, p))] + s_out,
        out_shape=[_sds((s, aw), BF16), _sds((s, aw), BF16), _sds((s, aw), BF16)] + s_shapes,
        input_output_aliases=aliases,
        scratch_shapes=[pltpu.VMEM((s, LANES), F32), pltpu.VMEM((s, LANES), F32)] + s_scratch,
        compiler_params=_params(("arbitrary", "arbitrary")),
    )(proj, proj, proj, o, do, tri, *kept, *([] if side is None else side.operands))
    return res[:3], res[3:]


def _hosted(hooks, kind, l, fn, *args, **kw):
    res, side_out = fn(*args, side=hooks.side(kind, l), **kw)
    hooks.done(kind, l, side_out)
    return res


def _layer_fwd(x, mod, gains, conv_w, tri, *, l, hooks):
    sh1, sc1, gt1, sh2, sc2, gt2 = mod
    g_pre_mix, g_post_mix, g_pre_mlp, g_post_mlp = gains
    d = x.shape[1]
    w = functools.partial(hooks.weight, l)
    h, proj = _hosted(hooks, "in_proj", l, _norm_mod_matmul, x, g_pre_mix, sc1, sh1, w("w_in"), name=f"in_proj_{l}")
    yc = _conv_fwd(proj, conv_w, name=f"conv_fwd_{l}")
    o, kept = _hosted(hooks, "attn_fwd", l, _attn_fwd, proj, tri, d=d, name=f"attn_fwd_{l}")
    ycv, yat, merged, mix, x1 = _mix_out(yc, o, proj, x, w("w_proj_conv"), w("w_proj_attn"), w("w_out"),
                                         g_post_mix, gt1, name=f"mix_out_{l}")
    (h2, a), _ = _norm_mod_matmul(x1, g_pre_mlp, sc2, sh2, w("w_mlp_in"), name=f"mlp_in_{l}")
    ff, x2 = _mlp_out(a, x1, w("w_mlp_out"), g_post_mlp, gt2, name=f"mlp_out_{l}")
    saved = dict(x=x, h=h, proj=proj, yc=yc, o=o, kept=kept, ycv=ycv, yat=yat, merged=merged, mix=mix, x1=x1, h2=h2, a=a, ff=ff,
                 conv_w=conv_w, **{k: w(k) for k in BIG})
    return x2, saved


def _layer_bwd(dx2, sv, mod, gains, tri, *, l, hooks):
    sh1, sc1, gt1, sh2, sc2, gt2 = mod
    g_pre_mix, g_post_mix, g_pre_mlp, g_post_mlp = gains
    d = dx2.shape[1]
    dff, da, dgt2, dg_post_mlp = _mlp_out_bwd(dx2, sv["ff"], sv["a"], sv["w_mlp_out"], g_post_mlp, gt2,
                                              name=f"mlp_out_bwd_{l}")
    hooks.grad(l, "w_mlp_out", _matmul_tn(sv["a"], [dff], relu2=True, name=f"gw_mlp_out_{l}"))
    (dx1, dsh2, dsc2, dg_pre_mlp), _ = _matmul_nt_norm_bwd([da], sv["w_mlp_in"], sv["x1"], dx2, g_pre_mlp, sc2,
                                                           name=f"mlp_in_bwd_{l}")
    hooks.grad(l, "w_mlp_in", _matmul_tn(sv["h2"], [da], name=f"gw_mlp_in_{l}"))
    dmix, dycv, dyat, dyc, do, dgate, dgt1, dg_post_mix = _mix_out_bwd(
        dx1, sv["mix"], sv["proj"], sv["ycv"], sv["yat"], sv["w_out"], sv["w_proj_conv"], sv["w_proj_attn"],
        g_post_mix, gt1, name=f"mix_out_bwd_{l}")
    hooks.grad(l, "w_out", _matmul_tn(sv["merged"], [dmix], name=f"gw_out_{l}"))
    hooks.grad(l, "w_proj_conv", _matmul_tn(sv["yc"], [dycv], name=f"gw_proj_conv_{l}"))
    hooks.grad(l, "w_proj_attn", _matmul_tn(sv["o"], [dyat], name=f"gw_proj_attn_{l}"))
    dconv, g_conv_w = _conv_bwd(dyc, sv["proj"], sv["conv_w"], name=f"conv_bwd_{l}")
    dq, dk, dv = _hosted(hooks, "attn_bwd", l, _attn_bwd, sv["proj"], sv["o"], sv["kept"], do, tri, d=d,
                         name=f"attn_bwd_{l}")
    dproj = [dconv, dq, dk, dv, dgate]
    n_in = sv["w_in"].shape[1]
    gw_in = _matmul_tn(sv["h"], [dconv], tn=768, n_total=n_in, name=f"gw_in_conv_{l}")
    gw_in = _matmul_tn(sv["h"], [dq, dk, dv], into=gw_in, col0=dconv.shape[1], n_total=n_in, name=f"gw_in_attn_{l}")
    gw_in = _matmul_tn(sv["h"], [dgate], into=gw_in, col0=n_in - dgate.shape[1], n_total=n_in, name=f"gw_in_gate_{l}")
    hooks.grad(l, "w_in", gw_in)
    dx0, dsh1, dsc1, dg_pre_mix = _hosted(hooks, "in_proj_bwd", l, _matmul_nt_norm_bwd, dproj, sv["w_in"], sv["x"], dx1,
                                          g_pre_mix, sc1, name=f"in_proj_bwd_{l}")
    dmod = jnp.concatenate([dsh1, dsc1, dgt1, dsh2, dsc2, dgt2], axis=0)
    dgains = jnp.concatenate([dg_pre_mix, dg_post_mix, dg_pre_mlp, dg_post_mlp], axis=0)
    return dx0, g_conv_w, dmod, dgains


BIG = ("w_in", "w_proj_conv", "w_proj_attn", "w_out", "w_mlp_in", "w_mlp_out")
SHARD_AXIS = dict(w_in=1, w_proj_conv=1, w_proj_attn=1, w_out=0, w_mlp_in=1, w_mlp_out=0)


def _local_step(x, target, mods, gains, conv_w, hooks):
    depth = mods.shape[0]
    tri = _tri(ATTN_BLOCK)
    saved = []
    for l in range(depth):
        mod = [mods[l, k:k + 1] for k in range(N_MOD)]
        gl = [gains[l, k:k + 1] for k in range(4)]
        x, sv = _layer_fwd(x, mod, gl, conv_w[l], tri, l=l, hooks=hooks)
        saved.append((sv, mod, gl))
    dx, loss = _loss_grad(x, target, name="loss_grad")
    dconv, dmods, dgains = [None] * depth, [None] * depth, [None] * depth
    for l in reversed(range(depth)):
        sv, mod, gl = saved[l]
        dx, dconv[l], dmods[l], dgains[l] = _layer_bwd(dx, sv, mod, gl, tri, l=l, hooks=hooks)
    return loss, dx, jnp.stack(dconv), jnp.stack(dmods), jnp.stack(dgains)


def _coords():
    return lax.axis_index("x"), lax.axis_index("y"), lax.axis_index("c")


def _flip(v, f):
    return 1 - v if f else v


def _all_gather_small(v, *, name):
    r, c_ = v.shape

    def body(v_ref, out_ref, send_sems, recv_sems, local_sem):
        x, y, c = _coords()
        me = 4 * x + 2 * y + c
        mine = pltpu.make_async_copy(v_ref, out_ref.at[me], local_sem)
        mine.start()
        copies = []
        for k in range(1, 8):
            fx, fy, fc = (k >> 2) & 1, (k >> 1) & 1, k & 1
            px, py, pc = _flip(x, fx), _flip(y, fy), _flip(c, fc)
            out = pltpu.make_async_remote_copy(src_ref=v_ref, dst_ref=out_ref.at[me], send_sem=send_sems.at[k - 1],
                                               recv_sem=recv_sems.at[k - 1], device_id=(px, py, pc), device_id_type=MESH)
            out.start()
            back = pltpu.make_async_remote_copy(src_ref=v_ref, dst_ref=out_ref.at[4 * px + 2 * py + pc],
                                                send_sem=send_sems.at[k - 1], recv_sem=recv_sems.at[k - 1],
                                                device_id=(px, py, pc), device_id_type=MESH)
            copies.append((out, back))
        for out, back in copies:
            back.wait_recv()
        for out, back in copies:
            out.wait_send()
        mine.wait()

    return pl.pallas_call(
        body, name=name,
        in_specs=[pl.BlockSpec(memory_space=pltpu.VMEM)],
        out_specs=pl.BlockSpec(memory_space=pltpu.VMEM),
        out_shape=_sds((8, r, c_), F32),
        scratch_shapes=[pltpu.SemaphoreType.DMA((7,)), pltpu.SemaphoreType.DMA((7,)), pltpu.SemaphoreType.DMA],
    )(v)


def _shard_dims(full_shape, axis):
    k, n = full_shape
    return (k // 4, n) if axis == 0 else (k, n // 4)


def _shard_window(ref, axis, chip, half, rows, cols):
    r0, rn = (0, rows) if half is None else (half * (rows // 2), rows // 2)
    if axis == 1:
        return ref.at[pl.ds(r0, rn), pl.ds(chip * cols, cols)]
    return ref.at[pl.ds(chip * rows + r0, rn), :]


def _cast_place(w, layer, axis, chip_arr, *, name, tr=256):
    _, rows, cols = w.shape
    tr = _fit(tr, rows)
    nb = rows // tr
    full = (rows * 4, cols) if axis == 0 else (rows, cols * 4)

    def body(chip_ref, w_ref, o_ref):
        o_ref[...] = w_ref[0].astype(BF16)

    if axis == 1:
        out_map = lambda i, chip: (i, chip[0])
    else:
        out_map = lambda i, chip: (chip[0] * nb + i, 0)
    grid_spec = pltpu.PrefetchScalarGridSpec(
        num_scalar_prefetch=1, grid=(nb,),
        in_specs=[pl.BlockSpec((1, tr, cols), lambda i, chip: (layer, i, 0))],
        out_specs=pl.BlockSpec((tr, cols), out_map))
    return pl.pallas_call(body, name=name, grid_spec=grid_spec, out_shape=_sds(full, BF16),
                          compiler_params=_params(("arbitrary",)))(chip_arr, w)


def _gather_side(fulls, axes):
    n = len(fulls)

    def copies(outs, sems):
        send_sems, recv_sems = sems
        x, y, c = _coords()
        chip = 2 * x + y
        sibling = (x, y, 1 - c)
        table = []
        for w in range(n):
            rows, cols = _shard_dims(outs[w].shape, axes[w])
            win = functools.partial(_shard_window, outs[w], axes[w], rows=rows, cols=cols)
            for j, (fx, fy) in enumerate(OTHER_CHIPS):
                px, py = _flip(x, fx), _flip(y, fy)
                pchip = 2 * px + py

                def copy(piece, sem, to):
                    return pltpu.make_async_remote_copy(src_ref=piece, dst_ref=piece, send_sem=send_sems.at[w, sem],
                                                        recv_sem=recv_sems.at[w, sem], device_id=to, device_id_type=MESH)

                table.append((copy(win(chip, c), j, (px, py, c)), copy(win(pchip, c), j, (px, py, c)),
                              copy(win(pchip, c), 3 + j, sibling), copy(win(pchip, 1 - c), 3 + j, sibling)))
        return table

    def start(ins, outs, sems):
        for send, _, _, _ in copies(outs, sems):
            send.start()

    def mid(ins, outs, sems):
        for _, landed, pass_on, _ in copies(outs, sems):
            landed.wait_recv()
            pass_on.start()

    def finish(ins, outs, sems):
        table = copies(outs, sems)
        for _, _, _, from_sibling in table:
            from_sibling.wait_recv()
        for send, _, pass_on, _ in table:
            send.wait_send()
            pass_on.wait_send()

    return _Side(fulls, [_sds(f.shape, f.dtype) for f in fulls], {w: w for w in range(n)},
                 [pltpu.SemaphoreType.DMA((n, 6)), pltpu.SemaphoreType.DMA((n, 6))], start, mid, finish)


def _exchange_side(grads, axes):
    n = len(grads)
    out_shapes = []
    for g, ax in zip(grads, axes):
        rows, cols = _shard_dims(g.shape, ax)
        out_shapes.append(_sds((7, rows // 2, cols), g.dtype))

    def copies(ins, outs, sems):
        send_sems, recv_sems = sems
        x, y, c = _coords()
        table = []
        for w in range(n):
            rows, cols = _shard_dims(ins[w].shape, axes[w])
            for k in range(1, 8):
                fx, fy, fc = (k >> 2) & 1, (k >> 1) & 1, k & 1
                px, py, pc = _flip(x, fx), _flip(y, fy), _flip(c, fc)
                piece = _shard_window(ins[w], axes[w], 2 * px + py, pc, rows, cols)
                table.append(pltpu.make_async_remote_copy(
                    src_ref=piece, dst_ref=outs[w].at[k - 1], send_sem=send_sems.at[w, k - 1],
                    recv_sem=recv_sems.at[w, k - 1], device_id=(px, py, pc), device_id_type=MESH))
        return table

    def start(ins, outs, sems):
        for cp in copies(ins, outs, sems):
            cp.start()

    def finish(ins, outs, sems):
        table = copies(ins, outs, sems)
        for cp in table:
            cp.wait_recv()
        for cp in table:
            cp.wait_send()

    return _Side(grads, out_shapes, {}, [pltpu.SemaphoreType.DMA((n, 7)), pltpu.SemaphoreType.DMA((n, 7))],
                 start, None, finish)


def _rs_sum_join(g, got, out_prev, layer, depth, axis, ids, *, name, tr=256):
    _, rows2, cols = got.shape
    tr = _fit(tr, rows2)
    nt = rows2 // tr
    if axis == 1:
        own_map = lambda i, ids_: (ids_[1] * nt + i, ids_[0])
    else:
        own_map = lambda i, ids_: ((ids_[0] * 2 + ids_[1]) * nt + i, 0)

    def body(ids_ref, g_ref, got_ref, *rest):
        out_ref, buf, local_sems, send_sems, recv_sem = rest[-5:]
        i = pl.program_id(0)
        x, y, c = _coords()
        sibling = (x, y, 1 - c)

        def copies(step, slot):
            rows_mine = pl.ds(c * rows2 + step * tr, tr)
            dst = out_ref.at[layer, rows_mine, :]
            keep = pltpu.make_async_copy(buf.at[slot], dst, local_sems.at[slot])
            give = pltpu.make_async_remote_copy(src_ref=buf.at[slot], dst_ref=dst, send_sem=send_sems.at[slot],
                                                recv_sem=recv_sem, device_id=sibling, device_id_type=MESH)
            return keep, give

        def drain(step, slot):
            keep, give = copies(step, slot)
            keep.wait()
            give.wait_send()

        slot = i % 2

        @pl.when(i >= 2)
        def _():
            drain(i - 2, slot)

        acc = g_ref[...].astype(F32)
        for k in range(7):
            acc = acc + got_ref[k].astype(F32)
        buf[slot] = acc
        keep, give = copies(i, slot)
        keep.start()
        give.start()

        @pl.when(i == nt - 1)
        def _():
            if nt >= 2:
                drain(nt - 2, (nt - 2) % 2)
            drain(nt - 1, (nt - 1) % 2)
            theirs = out_ref.at[layer, pl.ds((1 - c) * rows2, rows2), :]
            pltpu.make_async_remote_copy(src_ref=theirs, dst_ref=theirs, send_sem=send_sems.at[0], recv_sem=recv_sem,
                                         device_id=sibling, device_id_type=MESH).wait_recv()

    hbm = pl.BlockSpec(memory_space=pltpu.HBM)
    in_specs = [pl.BlockSpec((tr, cols), own_map), pl.BlockSpec((7, tr, cols), lambda i, ids_: (0, i, 0))]
    operands = [ids, g, got]
    aliases = {}
    if out_prev is not None:
        in_specs.append(hbm)
        operands.append(out_prev)
        aliases = {3: 0}
    grid_spec = pltpu.PrefetchScalarGridSpec(
        num_scalar_prefetch=1, grid=(nt,), in_specs=in_specs, out_specs=hbm,
        scratch_shapes=[pltpu.VMEM((2, tr, cols), F32), pltpu.SemaphoreType.DMA((2,)), pltpu.SemaphoreType.DMA((2,)),
                        pltpu.SemaphoreType.DMA])
    return pl.pallas_call(body, name=name, grid_spec=grid_spec, out_shape=_sds((depth, 2 * rows2, cols), F32),
                          input_output_aliases=aliases, compiler_params=_params(("arbitrary",)))(*operands)


MIX = ("w_proj_conv", "w_proj_attn", "w_out")


class _Schedule:
    def __init__(self, placed, depth, ids):
        self.placed, self.depth, self.ids = placed, depth, ids
        self.full, self.g, self.carried = {}, {}, None
        self.reduced = {k: None for k in BIG}
        first = [(0, "w_in")]
        self._landed(first, _side_call(self._gather(first), name="gather_w_in_0"))

    def _gather(self, keys):
        return _gather_side([self.placed[k] for k in keys], [SHARD_AXIS[k[1]] for k in keys])

    def _landed(self, keys, outs):
        for k, o in zip(keys, outs):
            self.full[k] = o

    def _exchange(self, keys):
        return _exchange_side([self.g[k] for k in keys], [SHARD_AXIS[k[1]] for k in keys])

    def _reduce(self, keys, got):
        for (l, name), pieces in zip(keys, got):
            self.reduced[name] = _rs_sum_join(self.g[(l, name)], pieces, self.reduced[name], l, self.depth,
                                              SHARD_AXIS[name], self.ids, name=f"rs_sum_join_{l}_{name}")

    def weight(self, l, name):
        return self.full[(l, name)]

    def grad(self, l, name, g):
        self.g[(l, name)] = g

    def side(self, kind, l):
        nxt = [(l + 1, "w_in")] if l + 1 < self.depth else []
        if kind == "in_proj":
            keys, make = [(l, k) for k in MIX + ("w_mlp_in",)], self._gather
        elif kind == "attn_fwd":
            keys, make = [(l, "w_mlp_out")] + nxt, self._gather
        elif kind == "attn_bwd":
            keys, make = [(l, k) for k in ("w_mlp_out", "w_mlp_in") + MIX], self._exchange
        else:
            keys, make = [(l, "w_in")], self._exchange
        self.carried = keys
        return make(keys)

    def done(self, kind, l, outs):
        (self._landed if kind in ("in_proj", "attn_fwd") else self._reduce)(self.carried, outs)


def _flat_rows(shape):
    rows = 1
    for s in shape[:-1]:
        rows *= s
    return rows, shape[-1]


def _row_tile(rows, cols, cap_bytes=2 * 1024 * 1024):
    t = rows
    while t * cols * 4 > cap_bytes and t % 16 == 0:
        t //= 2
    return t


def _ada_fwd(c_all, w_ada, b_loc, *, name, tn=512):
    l, d, nl = w_ada.shape
    b = c_all.shape[0]
    tn = min(tn, nl)

    def body(c_ref, w_ref, b_ref, o_ref):
        o_ref[0] = jnp.dot(c_ref[...], w_ref[0], preferred_element_type=F32,
                           precision=lax.Precision.HIGHEST) + b_ref[0]

    return pl.pallas_call(
        body, name=name, grid=(l, nl // tn),
        in_specs=[pl.BlockSpec((b, d), lambda i, j: (0, 0)), pl.BlockSpec((1, d, tn), lambda i, j: (i, 0, j)),
                  pl.BlockSpec((1, 1, tn), lambda i, j: (i, 0, j))],
        out_specs=pl.BlockSpec((1, b, tn), lambda i, j: (i, 0, j)),
        out_shape=_sds((l, b, nl), F32),
        compiler_params=_params(("parallel", "parallel")),
    )(c_all, w_ada, b_loc)


def _ada_bwd(c_t, dmod_loc, *, name, tn=512):
    d, b = c_t.shape
    l, _, nl = dmod_loc.shape
    tn = min(tn, nl)

    def body(c_ref, dm_ref, o_ref):
        cv = c_ref[...]
        dm = dm_ref[0]
        acc = cv[:, 0:1] * dm[0:1, :]
        for k in range(1, b):
            acc = acc + cv[:, k:k + 1] * dm[k:k + 1, :]
        o_ref[0] = acc

    return pl.pallas_call(
        body, name=name, grid=(l, nl // tn),
        in_specs=[pl.BlockSpec((d, b), lambda i, j: (0, 0)), pl.BlockSpec((1, b, tn), lambda i, j: (i, 0, j))],
        out_specs=pl.BlockSpec((1, d, tn), lambda i, j: (i, 0, j)),
        out_shape=_sds((l, d, nl), F32),
        compiler_params=_params(("parallel", "parallel")),
    )(c_t, dmod_loc)


def _sum_devices(p, *, name):
    k, r, c_ = p.shape

    def body(p_ref, o_ref):
        acc = p_ref[0]
        for j in range(1, k):
            acc = acc + p_ref[j]
        o_ref[...] = acc

    return pl.pallas_call(body, name=name, out_shape=_sds((r, c_), F32),
                          in_specs=[pl.BlockSpec(memory_space=pltpu.VMEM)],
                          out_specs=pl.BlockSpec(memory_space=pltpu.VMEM))(p)


def _adamw(w, g, m, v, *, name):
    shape = w.shape
    rows, cols = _flat_rows(shape)
    tr = _row_tile(rows, cols, cap_bytes=1024 * 1024)
    c1 = 1.0 / (1.0 - ADAM_B1 ** ADAM_STEP)
    c2 = 1.0 / (1.0 - ADAM_B2 ** ADAM_STEP)

    def body(w_ref, g_ref, m_ref, v_ref, go_ref, d_ref, nm_ref, nv_ref):
        gv = g_ref[...]
        nm = ADAM_B1 * m_ref[...] + (1.0 - ADAM_B1) * gv
        nv = ADAM_B2 * v_ref[...] + (1.0 - ADAM_B2) * (gv * gv)
        m_hat = nm * c1
        v_hat = nv * c2
        go_ref[...] = gv
        d_ref[...] = -ADAM_LR * (m_hat / (jnp.sqrt(v_hat) + ADAM_EPS) + ADAM_WD * w_ref[...])
        nm_ref[...] = nm
        nv_ref[...] = nv

    spec = pl.BlockSpec((tr, cols), lambda i: (i, 0))
    flat = lambda a: a.reshape(rows, cols)
    outs = pl.pallas_call(body, name=name, grid=(rows // tr,), in_specs=[spec] * 4, out_specs=[spec] * 4,
                          out_shape=[_sds((rows, cols), F32)] * 4, compiler_params=_params(("parallel",)),
                          )(flat(w), flat(g), flat(m), flat(v))
    return tuple(o.reshape(shape) for o in outs)


WEIGHTS = ("w_ada", "b_ada", "g_pre_mix", "g_post_mix", "g_pre_mlp", "g_post_mlp", "w_in", "conv_w",
           "w_proj_conv", "w_proj_attn", "w_out", "w_mlp_in", "w_mlp_out")
GAINS = ("g_pre_mix", "g_post_mix", "g_pre_mlp", "g_post_mlp")


def kernel(x, c, w_ada, b_ada, g_pre_mix, g_post_mix, g_pre_mlp, g_post_mlp, w_in, conv_w, w_proj_conv, w_proj_attn, w_out, w_mlp_in, w_mlp_out, loss_target, m_w_ada, m_b_ada, m_g_pre_mix, m_g_post_mix, m_g_pre_mlp, m_g_post_mlp, m_w_in, m_conv_w, m_w_proj_conv, m_w_proj_attn, m_w_out, m_w_mlp_in, m_w_mlp_out, v_w_ada, v_b_ada, v_g_pre_mix, v_g_post_mix, v_g_pre_mlp, v_g_post_mlp, v_w_in, v_conv_w, v_w_proj_conv, v_w_proj_attn, v_w_out, v_w_mlp_in, v_w_mlp_out):
    params = dict(w_ada=w_ada, b_ada=b_ada, g_pre_mix=g_pre_mix, g_post_mix=g_post_mix, g_pre_mlp=g_pre_mlp,
                  g_post_mlp=g_post_mlp, w_in=w_in, conv_w=conv_w, w_proj_conv=w_proj_conv, w_proj_attn=w_proj_attn,
                  w_out=w_out, w_mlp_in=w_mlp_in, w_mlp_out=w_mlp_out)
    m_in = dict(w_ada=m_w_ada, b_ada=m_b_ada, g_pre_mix=m_g_pre_mix, g_post_mix=m_g_post_mix, g_pre_mlp=m_g_pre_mlp,
                g_post_mlp=m_g_post_mlp, w_in=m_w_in, conv_w=m_conv_w, w_proj_conv=m_w_proj_conv,
                w_proj_attn=m_w_proj_attn, w_out=m_w_out, w_mlp_in=m_w_mlp_in, w_mlp_out=m_w_mlp_out)
    v_in = dict(w_ada=v_w_ada, b_ada=v_b_ada, g_pre_mix=v_g_pre_mix, g_post_mix=v_g_post_mix, g_pre_mlp=v_g_pre_mlp,
                g_post_mlp=v_g_post_mlp, w_in=v_w_in, conv_w=v_conv_w, w_proj_conv=v_w_proj_conv,
                w_proj_attn=v_w_proj_attn, w_out=v_w_out, w_mlp_in=v_w_mlp_in, w_mlp_out=v_w_mlp_out)

    depth, d, nl_ada = w_ada.shape
    ix, iy, ic = lax.axis_index("x"), lax.axis_index("y"), lax.axis_index("c")
    chip = 2 * ix + iy
    me = 4 * ix + 2 * iy + ic
    xs = x[0]
    target = loss_target[0]

    c_all = _all_gather_small(jnp.broadcast_to(c, (8, d)), name="gather_c")[:, 0, :]
    b_loc = lax.dynamic_slice_in_dim(b_ada, chip * nl_ada, nl_ada, axis=1)[:, None, :]
    mod_loc = _ada_fwd(c_all, w_ada, b_loc, name="ada_fwd")
    mod_all = _all_gather_small(mod_loc.reshape(depth * 8, nl_ada), name="gather_mod")
    mod_all = mod_all.reshape(4, 2, depth, 8, nl_ada)[:, 0]
    mod_me = lax.dynamic_index_in_dim(mod_all, me, axis=2, keepdims=False)
    mods = jnp.transpose(mod_me, (1, 0, 2)).reshape(depth, N_MOD, d)

    chip_arr = jnp.reshape(chip, (1,)).astype(jnp.int32)
    ids = jnp.stack([chip, ic]).astype(jnp.int32)
    placed = {(l, k): _cast_place(params[k], l, SHARD_AXIS[k], chip_arr, name=f"place_{k}_{l}")
              for l in range(depth) for k in BIG}
    conv_full = _all_gather_small(
        jnp.pad(conv_w.reshape(depth * 3, -1), ((0, 8 - depth * 3), (0, 0))), name="gather_conv_w")
    conv_full = conv_full.reshape(4, 2, 8, -1)[:, 0, :depth * 3]
    conv_full = jnp.transpose(conv_full, (1, 0, 2)).reshape(depth, 3, -1)

    gains = jnp.stack([params[k] for k in GAINS], axis=1)
    schedule = _Schedule(placed, depth, ids)
    loss, dx, conv_grads, dmods, dgains = _local_step(xs, target, mods, gains, conv_full, schedule)

    cw = conv_full.shape[2]
    rows = [dmods.reshape(depth * N_MOD, d), dgains.reshape(depth * 4, d),
            conv_grads.reshape(-1, d), jnp.broadcast_to(loss, (1, d))]
    payload = jnp.concatenate(rows, axis=0)
    n_rows = payload.shape[0]
    pad = (-n_rows) % 8
    payload = jnp.pad(payload, ((0, pad), (0, 0)))
    everyone = _all_gather_small(payload, name="gather_small_grads")
    total = _sum_devices(everyone, name="sum_small_grads")
    r0 = depth * N_MOD
    grads = {}
    grads["b_ada"] = total[:r0].reshape(depth, N_MOD * d)
    gsum = total[r0:r0 + depth * 4].reshape(depth, 4, d)
    for k, name in enumerate(GAINS):
        grads[name] = gsum[:, k]
    r1 = r0 + depth * 4
    n_conv = (depth * 3 * cw) // d
    conv_g = total[r1:r1 + n_conv].reshape(depth, 3, cw)
    grads["conv_w"] = lax.dynamic_slice_in_dim(conv_g, chip * (cw // 4), cw // 4, axis=2)
    loss_out = total[r1 + n_conv, 0]
    dmod_all = everyone[:, :r0].reshape(8, depth, N_MOD * d)
    dmod_loc = lax.dynamic_slice_in_dim(dmod_all, chip * nl_ada, nl_ada, axis=2)
    grads["w_ada"] = _ada_bwd(c_all.T, jnp.transpose(dmod_loc, (1, 0, 2)), name="ada_bwd")

    for k in BIG:
        grads[k] = schedule.reduced[k]

    deltas, new_m, new_v = {}, {}, {}
    for k in WEIGHTS:
        grads[k], deltas[k], new_m[k], new_v[k] = _adamw(params[k], grads[k], m_in[k], v_in[k], name=f"adamw_{k}")

    return (loss_out, dx[None], *[grads[k] for k in WEIGHTS], *[deltas[k] for k in WEIGHTS],
            *[new_m[k] for k in WEIGHTS], *[new_v[k] for k in WEIGHTS])
```

```python
import functools

import jax
import jax.numpy as jnp
from jax import lax
from jax.experimental import pallas as pl
from jax.experimental.pallas import tpu as pltpu

F32 = jnp.float32
BF16 = jnp.bfloat16
EPS = 1e-6
N_MOD = 6
HEAD_DIM = 64
LANES = 128
ATTN_SCALE = 1.0 / 8.0
UNDERFLOW_LOG = -90.0
ATTN_BLOCK = 256
ATTN_CHAINS = (4, 4)
ADAM_LR = 0.001
ADAM_B1 = 0.9
ADAM_B2 = 0.999
ADAM_EPS = 1e-08
ADAM_WD = 0.01
ADAM_STEP = 10
VMEM_LIMIT = 56 * 1024 * 1024
MESH = pl.DeviceIdType.MESH
OTHER_CHIPS = ((1, 0), (0, 1), (1, 1))

_NT = (((1,), (1,)), ((), ()))
_TN = (((0,), (0,)), ((), ()))


def _sds(shape, dtype):
    return jax.ShapeDtypeStruct(shape, dtype)


def _params(sem):
    return pltpu.CompilerParams(dimension_semantics=sem, vmem_limit_bytes=VMEM_LIMIT)


def _fit(t, n):
    t = min(t, n)
    while n % t:
        t //= 2
    return t


def _vec_spec(d, nargs=1):
    assert nargs == 1
    return pl.BlockSpec((1, d), lambda i: (0, 0))


def _log_one_minus_sigmoid(z):
    return -jnp.log(1.0 + jnp.exp(-jnp.abs(z))) - jnp.maximum(z, 0.0)


def _sigmoid(z):
    return 0.5 * jnp.tanh(0.5 * z) + 0.5


def _split_bf16(a):
    hi = a.astype(BF16)
    lo = (a - hi.astype(F32)).astype(BF16)
    return hi, lo


def _rms_bwd(dn, xin, g):
    r = lax.rsqrt(jnp.mean(xin * xin, axis=-1, keepdims=True) + EPS)
    xh = xin * r
    dxh = dn * g
    dxin = r * (dxh - xh * jnp.mean(dxh * xh, axis=-1, keepdims=True))
    return dxin, xh


def _colsum(a):
    return jnp.sum(a, axis=0, keepdims=True)


def _norm_mod_matmul(x, g, sc, sh, w, *, name, tm=512, side=None):
    s, d = x.shape
    n = w.shape[1]
    tm = _fit(tm, s)
    nt = s // tm

    def body(*refs):
        i = pl.program_id(0)
        (x_ref, g_ref, sc_ref, sh_ref, w_ref, h_ref, o_ref), late_phases = _host_side(
            side, 5, 2, 0, refs, i == 0, i == (3 * nt) // 4, i == nt - 1)
        xv = x_ref[...]
        r = lax.rsqrt(jnp.mean(xv * xv, axis=-1, keepdims=True) + EPS)
        h = ((xv * r * g_ref[...]) * (1.0 + sc_ref[...]) + sh_ref[...]).astype(BF16)
        h_ref[...] = h
        o_ref[...] = jnp.dot(h, w_ref[...], preferred_element_type=F32).astype(BF16)
        late_phases()

    s_in, s_out, s_shapes, aliases, s_scratch = _side_specs(side, 5, 2)
    res = pl.pallas_call(
        body, name=name, grid=(nt,),
        in_specs=[pl.BlockSpec((tm, d), lambda i: (i, 0)), _vec_spec(d, 1), _vec_spec(d, 1), _vec_spec(d, 1),
                  pl.BlockSpec((d, n), lambda i: (0, 0))] + s_in,
        out_specs=[pl.BlockSpec((tm, d), lambda i: (i, 0)), pl.BlockSpec((tm, n), lambda i: (i, 0))] + s_out,
        out_shape=[_sds((s, d), BF16), _sds((s, n), BF16)] + s_shapes,
        input_output_aliases=aliases, scratch_shapes=s_scratch,
        compiler_params=_params(("arbitrary",)),
    )(x, g, sc, sh, w, *([] if side is None else side.operands))
    return res[:2], res[2:]


HALO = 16


def _shifted_down(a, before):
    row = lax.broadcasted_iota(jnp.int32, a.shape, 0)
    last, last2 = before[-1:, :], before[-2:-1, :]
    one = jnp.where(row == 0, last, pltpu.roll(a, 1, axis=0))
    two = jnp.where(row == 0, last2, jnp.where(row == 1, last, pltpu.roll(a, 2, axis=0)))
    return one, two


def _shifted_up(a, after):
    n = a.shape[0]
    row = lax.broadcasted_iota(jnp.int32, a.shape, 0)
    first, second = after[0:1, :], after[1:2, :]
    one = jnp.where(row == n - 1, first, pltpu.roll(a, n - 1, axis=0))
    two = jnp.where(row == n - 1, second, jnp.where(row == n - 2, first, pltpu.roll(a, n - 2, axis=0)))
    return one, two


def _conv_fwd(proj, conv_w, *, name, tm=512):
    s = proj.shape[0]
    cw = conv_w.shape[1]
    tm = min(tm, s)
    nb = tm // HALO

    def body(bg_ref, cg_ref, u_ref, cgh_ref, uh_ref, w_ref, yc_ref):
        i = pl.program_id(0)
        vv = cg_ref[...].astype(F32) * u_ref[...].astype(F32)
        halo = jnp.where(i > 0, cgh_ref[...].astype(F32) * uh_ref[...].astype(F32), 0.0)
        v1, v2 = _shifted_down(vv, halo)
        w = w_ref[...]
        y = w[2:3, :] * vv + w[1:2, :] * v1 + w[0:1, :] * v2
        yc_ref[...] = (bg_ref[...].astype(F32) * y).astype(BF16)

    def prev(i):
        return jnp.maximum(i * nb - 1, 0)

    return pl.pallas_call(
        body, name=name, grid=(s // tm,),
        in_specs=[pl.BlockSpec((tm, cw), lambda i: (i, 0)), pl.BlockSpec((tm, cw), lambda i: (i, 1)),
                  pl.BlockSpec((tm, cw), lambda i: (i, 2)),
                  pl.BlockSpec((HALO, cw), lambda i: (prev(i), 1)), pl.BlockSpec((HALO, cw), lambda i: (prev(i), 2)),
                  pl.BlockSpec((3, cw), lambda i: (0, 0))],
        out_specs=pl.BlockSpec((tm, cw), lambda i: (i, 0)),
        out_shape=_sds((s, cw), BF16),
        compiler_params=_params(("arbitrary",)),
    )(proj, proj, proj, proj, proj, conv_w)


def _tri(qb):
    r = lax.broadcasted_iota(jnp.int32, (qb, qb), 0)
    c = lax.broadcasted_iota(jnp.int32, (qb, qb), 1)
    return (r >= c).astype(BF16)


def _head_mask(h):
    lane = lax.broadcasted_iota(jnp.int32, (1, LANES), 1)
    return (lane >= HEAD_DIM * h) & (lane < HEAD_DIM * (h + 1))


def _stack_heads(a, masks):
    return jnp.concatenate([jnp.where(m, a, 0).astype(BF16) for m in masks], axis=0)


def _heads_to_lanes(a, qb):
    return jnp.concatenate([a[:qb], a[qb:]], axis=1)


def _stacked_causal(qb, width, first_key, first_query):
    row = lax.broadcasted_iota(jnp.int32, (2 * qb, width), 0)
    col = lax.broadcasted_iota(jnp.int32, (2 * qb, width), 1)
    return first_key + col < first_query + jnp.where(row >= qb, row - qb, row)


def _running_sum(a, tri_m):
    rows, qb = a.shape[0], tri_m.shape[0]
    n = a.shape[1] // qb
    hi, lo = _split_bf16(a)
    stacked = jnp.concatenate([p[:, s * qb:(s + 1) * qb] for s in range(n) for p in (hi, lo)], axis=0)
    both = jnp.dot(stacked, tri_m, preferred_element_type=F32)
    parts = [both[(2 * s) * rows:(2 * s + 1) * rows] + both[(2 * s + 1) * rows:(2 * s + 2) * rows] for s in range(n)]
    later = None
    for s in reversed(range(n)):
        if later is not None:
            parts[s] = parts[s] + later
        later = parts[s][:, 0:1]
    return (parts[0] if n == 1 else jnp.concatenate(parts, axis=1)), later


def _attn_cols(d):
    cw = d // 2
    hp = (d // 2) // LANES
    q0 = (3 * cw) // LANES
    return q0, q0 + hp, q0 + 2 * hp, hp


class _Side:
    def __init__(self, operands, out_shapes, aliases, scratch, start, mid, finish):
        self.operands, self.out_shapes, self.aliases, self.scratch = list(operands), list(out_shapes), aliases, list(scratch)
        self.start, self.mid, self.finish = start, mid, finish


def _side_call(side, *, name):
    n_in, n_out = len(side.operands), len(side.out_shapes)

    def body(*refs):
        parts = refs[:n_in], refs[n_in:n_in + n_out], refs[n_in + n_out:]
        side.start(*parts)
        if side.mid is not None:
            side.mid(*parts)
        side.finish(*parts)

    hbm = pl.BlockSpec(memory_space=pltpu.HBM)
    return pl.pallas_call(body, name=name, in_specs=[hbm] * n_in, out_specs=[hbm] * n_out, out_shape=side.out_shapes,
                          input_output_aliases=dict(side.aliases), scratch_shapes=side.scratch)(*side.operands)


def _host_side(side, n_in, n_out, n_scratch, refs, first, late, last):
    if side is None:
        return refs, lambda: None
    s_in, s_out = len(side.operands), len(side.out_shapes)
    ins = refs[:n_in]
    side_in = refs[n_in:n_in + s_in]
    outs = refs[n_in + s_in:n_in + s_in + n_out]
    side_out = refs[n_in + s_in + n_out:n_in + s_in + n_out + s_out]
    rest = refs[n_in + s_in + n_out + s_out:]
    scratch, sems = rest[:n_scratch], rest[n_scratch:]
    parts = (side_in, side_out, sems)
    pl.when(first)(lambda: side.start(*parts))

    def run_late_phases():
        if side.mid is not None:
            pl.when(late)(lambda: side.mid(*parts))
        pl.when(last)(lambda: side.finish(*parts))

    return (*ins, *outs, *scratch), run_late_phases


def _side_specs(side, n_in, n_out):
    if side is None:
        return [], [], [], {}, []
    hbm = pl.BlockSpec(memory_space=pltpu.HBM)
    s_in = len(side.operands)
    aliases = {n_in + a: n_out + b for a, b in side.aliases.items()}
    return [hbm] * s_in, [hbm] * len(side.out_shapes), side.out_shapes, aliases, side.scratch


def _attn_fwd(proj, tri, *, d, name, side=None):
    s = proj.shape[0]
    qb = tri.shape[0]
    chains = _fit(ATTN_CHAINS[0], s // qb)
    ng = s // (qb * chains)
    q0, k0, v0, hp = _attn_cols(d)

    def body(*refs):
        p, g = pl.program_id(0), pl.program_id(1)
        (q_ref, k_ref, v_ref, tri_ref, o_ref, a_ref, b_ref, run_ref), late_phases = _host_side(
            side, 4, 4, 0, refs, (p == 0) & (g == 0), (p == hp - 1) & (g == 0), (p == hp - 1) & (g == ng - 1))
        tri_m = tri_ref[...]
        masks = [_head_mask(h) for h in range(2)]

        def first_steps(u):
            i = g * chains + u
            qs = _stack_heads(q_ref[u * qb:(u + 1) * qb, :] * ATTN_SCALE, masks)

            def strip(j, state, causal=None, keep=None, live=None):
                run, acc = state
                rows = pl.ds(pl.multiple_of(j * qb, qb), qb)
                z = lax.dot_general(qs, k_ref[rows, :], _NT, preferred_element_type=F32)
                lg = _log_one_minus_sigmoid(z)
                beta = 1.0 - jnp.exp(lg) if keep is not None else None
                if causal is not None:
                    lg = jnp.where(causal, lg, 0.0)
                cs, total = _running_sum(lg, tri_m)
                a = jnp.exp(z + cs + run)
                if causal is not None:
                    a = jnp.where(causal, a, 0.0)
                    beta = jnp.where(causal, beta, 0.0)
                if live is not None:
                    a = jnp.where(live, a, 0.0)
                    beta = jnp.where(live, beta, 0.0)
                    total = jnp.where(live, total, 0.0)
                ab = a.astype(BF16)
                if keep is not None:
                    a_ref[0, u, keep] = ab
                    b_ref[0, u, keep] = beta.astype(BF16)
                acc = acc + jnp.dot(_heads_to_lanes(ab, qb), _stack_heads(v_ref[rows, :], masks),
                                    preferred_element_type=F32)
                return run + total, acc

            state = strip(i, (jnp.zeros((2 * qb, 1), F32), jnp.zeros((qb, LANES), F32)),
                          causal=_stacked_causal(qb, qb, 0, 0), keep=0)
            state = strip(jnp.maximum(i - 1, 0), state, keep=1, live=i >= 1)
            run_ref[0, u] = jnp.broadcast_to(state[0], (2 * qb, LANES))
            return i, strip, state

        started = [first_steps(u) for u in range(chains)]
        for u, (i, strip, state) in enumerate(started):
            state = lax.while_loop(
                lambda st: (st[0] >= 0) & (jnp.max(st[1]) > UNDERFLOW_LOG),
                lambda st, strip=strip: (st[0] - 1, *strip(st[0], st[1:])),
                (i - 2, *state))
            o_ref[u * qb:(u + 1) * qb, :] = state[2]
        late_phases()

    s_in, s_out, s_shapes, aliases, s_scratch = _side_specs(side, 4, 4)
    tq = qb * chains
    nq = s // qb
    res = pl.pallas_call(
        body, name=name, grid=(hp, ng),
        in_specs=[pl.BlockSpec((tq, LANES), lambda p, i: (i, q0 + p)),
                  pl.BlockSpec((s, LANES), lambda p, i: (0, k0 + p)),
                  pl.BlockSpec((s, LANES), lambda p, i: (0, v0 + p)),
                  pl.BlockSpec((qb, qb), lambda p, i: (0, 0))] + s_in,
        out_specs=[pl.BlockSpec((tq, LANES), lambda p, i: (i, p)),
                   pl.BlockSpec((1, chains, 2, 2 * qb, qb), lambda p, i: (p, i, 0, 0, 0)),
                   pl.BlockSpec((1, chains, 2, 2 * qb, qb), lambda p, i: (p, i, 0, 0, 0)),
                   pl.BlockSpec((1, chains, 2 * qb, LANES), lambda p, i: (p, i, 0, 0))] + s_out,
        out_shape=[_sds((s, hp * LANES), F32), _sds((hp, nq, 2, 2 * qb, qb), BF16), _sds((hp, nq, 2, 2 * qb, qb), BF16),
                   _sds((hp, nq, 2 * qb, LANES), F32)] + s_shapes,
        input_output_aliases=aliases, scratch_shapes=s_scratch,
        compiler_params=_params(("arbitrary", "arbitrary")),
    )(proj, proj, proj, tri, *([] if side is None else side.operands))
    return (res[0], tuple(res[1:4])), res[4:]


def _mix_out(yc, o, proj, x, wpc, wpa, wout, g, gt, *, name, tm=512):
    s, d = x.shape
    cw = yc.shape[1]
    tm = min(tm, s)
    ga_blk = (3 * cw + 3 * (d // 2)) // d

    def body(yc_ref, o_ref, ga_ref, gb_ref, x_ref, wpc_ref, wpa_ref, wout_ref, g_ref, gt_ref,
             ycv_ref, yat_ref, mg_ref, mix_ref, x1_ref):
        y_conv = jnp.dot(yc_ref[...], wpc_ref[...], preferred_element_type=F32)
        y_attn = jnp.dot(o_ref[...].astype(BF16), wpa_ref[...], preferred_element_type=F32)
        merged = (_sigmoid(ga_ref[...].astype(F32)) * y_conv + _sigmoid(gb_ref[...].astype(F32)) * y_attn)
        mg = merged.astype(BF16)
        mix = jnp.dot(mg, wout_ref[...], preferred_element_type=F32)
        r = lax.rsqrt(jnp.mean(mix * mix, axis=-1, keepdims=True) + EPS)
        ycv_ref[...] = y_conv.astype(BF16)
        yat_ref[...] = y_attn.astype(BF16)
        mg_ref[...] = mg
        mix_ref[...] = mix
        x1_ref[...] = x_ref[...] + gt_ref[...] * (mix * r * g_ref[...])

    def rows(w):
        return pl.BlockSpec((tm, w), lambda i: (i, 0))

    def full(a):
        return pl.BlockSpec(a.shape, lambda i: (0, 0))

    return pl.pallas_call(
        body, name=name, grid=(s // tm,),
        in_specs=[rows(cw), rows(d // 2), pl.BlockSpec((tm, d), lambda i: (i, ga_blk)),
                  pl.BlockSpec((tm, d), lambda i: (i, ga_blk + 1)), rows(d),
                  full(wpc), full(wpa), full(wout), _vec_spec(d, 1), _vec_spec(d, 1)],
        out_specs=[rows(d), rows(d), rows(d), rows(d), rows(d)],
        out_shape=[_sds((s, d), BF16), _sds((s, d), BF16), _sds((s, d), BF16), _sds((s, d), F32), _sds((s, d), F32)],
        compiler_params=_params(("parallel",)),
    )(yc, o, proj, proj, x, wpc, wpa, wout, g, gt)


def _relu2(a):
    r = jnp.maximum(a.astype(F32), 0.0)
    return (r * r).astype(BF16)


def _mlp_out(a, x, w2, g, gt, *, name, tm=512):
    s, d = x.shape
    dff = a.shape[1]
    tm = min(tm, s)

    def body(a_ref, x_ref, w_ref, g_ref, gt_ref, ff_ref, x2_ref):
        ff = jnp.dot(_relu2(a_ref[...]), w_ref[...], preferred_element_type=F32)
        r = lax.rsqrt(jnp.mean(ff * ff, axis=-1, keepdims=True) + EPS)
        ff_ref[...] = ff
        x2_ref[...] = x_ref[...] + gt_ref[...] * (ff * r * g_ref[...])

    return pl.pallas_call(
        body, name=name, grid=(s // tm,),
        in_specs=[pl.BlockSpec((tm, dff), lambda i: (i, 0)), pl.BlockSpec((tm, d), lambda i: (i, 0)),
                  pl.BlockSpec((dff, d), lambda i: (0, 0)), _vec_spec(d, 1), _vec_spec(d, 1)],
        out_specs=[pl.BlockSpec((tm, d), lambda i: (i, 0)), pl.BlockSpec((tm, d), lambda i: (i, 0))],
        out_shape=[_sds((s, d), F32), _sds((s, d), F32)],
        compiler_params=_params(("parallel",)),
    )(a, x, w2, g, gt)


def _mlp_out_loss(a, x, w2, g, gt, target, *, name, tm=512):
    s, d = x.shape
    dff = a.shape[1]
    tm = min(tm, s)

    def body(a_ref, x_ref, w_ref, g_ref, gt_ref, t_ref, ff_ref, dy_ref, loss_ref):
        @pl.when(pl.program_id(0) == 0)
        def _():
            loss_ref[...] = jnp.zeros_like(loss_ref)
        ff = jnp.dot(_relu2(a_ref[...]), w_ref[...], preferred_element_type=F32)
        r = lax.rsqrt(jnp.mean(ff * ff, axis=-1, keepdims=True) + EPS)
        ff_ref[...] = ff
        e = (x_ref[...] + gt_ref[...] * (ff * r * g_ref[...])) - t_ref[...]
        dy_ref[...] = e * (1.0 / d)
        loss_ref[...] += 0.5 * jnp.sum(jnp.mean(e * e, axis=-1, keepdims=True), axis=0, keepdims=True)

    return pl.pallas_call(
        body, name=name, grid=(s // tm,),
        in_specs=[pl.BlockSpec((tm, dff), lambda i: (i, 0)), pl.BlockSpec((tm, d), lambda i: (i, 0)),
                  pl.BlockSpec((dff, d), lambda i: (0, 0)), _vec_spec(d, 1), _vec_spec(d, 1),
                  pl.BlockSpec((tm, d), lambda i: (i, 0))],
        out_specs=[pl.BlockSpec((tm, d), lambda i: (i, 0)), pl.BlockSpec((tm, d), lambda i: (i, 0)),
                   pl.BlockSpec((1, 1), lambda i: (0, 0))],
        out_shape=[_sds((s, d), F32), _sds((s, d), F32), _sds((1, 1), F32)],
        compiler_params=_params(("arbitrary",)),
    )(a, x, w2, g, gt, target)


def _mlp_out_bwd(dx, ff, a, w2, g, gt, *, name, tm=512):
    s, d = dx.shape
    dff = a.shape[1]
    tm = min(tm, s)

    def body(dx_ref, ff_ref, a_ref, w_ref, g_ref, gt_ref, dff_ref, da_ref, dgt_ref, dg_ref):
        @pl.when(pl.program_id(0) == 0)
        def _():
            dgt_ref[...] = jnp.zeros_like(dgt_ref)
            dg_ref[...] = jnp.zeros_like(dg_ref)
        dxv = dx_ref[...]
        dn = dxv * gt_ref[...]
        dffv, xh = _rms_bwd(dn, ff_ref[...], g_ref[...])
        dgt_ref[...] += _colsum(dxv * (xh * g_ref[...]))
        dg_ref[...] += _colsum(dn * xh)
        dffb = dffv.astype(BF16)
        dff_ref[...] = dffb
        drr = lax.dot_general(dffb, w_ref[...], _NT, preferred_element_type=F32)
        da_ref[...] = (drr * (2.0 * jnp.maximum(a_ref[...].astype(F32), 0.0))).astype(BF16)

    return pl.pallas_call(
        body, name=name, grid=(s // tm,),
        in_specs=[pl.BlockSpec((tm, d), lambda i: (i, 0)), pl.BlockSpec((tm, d), lambda i: (i, 0)),
                  pl.BlockSpec((tm, dff), lambda i: (i, 0)), pl.BlockSpec((dff, d), lambda i: (0, 0)),
                  _vec_spec(d, 1), _vec_spec(d, 1)],
        out_specs=[pl.BlockSpec((tm, d), lambda i: (i, 0)), pl.BlockSpec((tm, dff), lambda i: (i, 0)),
                   _vec_spec(d, 1), _vec_spec(d, 1)],
        out_shape=[_sds((s, d), BF16), _sds((s, dff), BF16), _sds((1, d), F32), _sds((1, d), F32)],
        compiler_params=_params(("arbitrary",)),
    )(dx, ff, a, w2, g, gt)


def _matmul_nt_norm_bwd(dys, w, x, dres, g, sc, *, name, tm=512, side=None):
    s = dys[0].shape[0]
    widths = [dy.shape[1] for dy in dys]
    d, n = w.shape
    assert sum(widths) == n, (widths, n)
    tm = _fit(tm, s)
    nt = s // tm
    np_ = len(dys)

    def body(*refs):
        i = pl.program_id(0)
        own, late_phases = _host_side(side, np_ + 5, 4, 0, refs, i == 0, i == (3 * nt) // 4, i == nt - 1)
        dy_refs = own[:np_]
        w_ref, x_ref, dres_ref, g_ref, sc_ref, dx_ref, dsh_ref, dsc_ref, dg_ref = own[np_:]

        @pl.when(i == 0)
        def _():
            dsh_ref[...] = jnp.zeros_like(dsh_ref)
            dsc_ref[...] = jnp.zeros_like(dsc_ref)
            dg_ref[...] = jnp.zeros_like(dg_ref)

        dh = None
        for p, dy_ref in enumerate(dy_refs):
            cols = slice(sum(widths[:p]), sum(widths[:p + 1]))
            part = lax.dot_general(dy_ref[...], w_ref[:, cols], _NT, preferred_element_type=F32)
            dh = part if dh is None else dh + part
        dn = dh * (1.0 + sc_ref[...])
        dxin, xh = _rms_bwd(dn, x_ref[...], g_ref[...])
        dsh_ref[...] += _colsum(dh)
        dsc_ref[...] += _colsum(dh * (xh * g_ref[...]))
        dg_ref[...] += _colsum(dn * xh)
        dx_ref[...] = dres_ref[...] + dxin
        late_phases()

    s_in, s_out, s_shapes, aliases, s_scratch = _side_specs(side, np_ + 5, 4)
    res = pl.pallas_call(
        body, name=name, grid=(nt,),
        in_specs=[pl.BlockSpec((tm, wd), lambda i: (i, 0)) for wd in widths]
        + [pl.BlockSpec((d, n), lambda i: (0, 0)),
           pl.BlockSpec((tm, d), lambda i: (i, 0)), pl.BlockSpec((tm, d), lambda i: (i, 0)),
           _vec_spec(d, 1), _vec_spec(d, 1)] + s_in,
        out_specs=[pl.BlockSpec((tm, d), lambda i: (i, 0)), _vec_spec(d, 1), _vec_spec(d, 1), _vec_spec(d, 1)] + s_out,
        out_shape=[_sds((s, d), F32), _sds((1, d), F32), _sds((1, d), F32), _sds((1, d), F32)] + s_shapes,
        input_output_aliases=aliases, scratch_shapes=s_scratch,
        compiler_params=_params(("arbitrary",)),
    )(*dys, w, x, dres, g, sc, *([] if side is None else side.operands))
    return res[:4], res[4:]


def _matmul_tn(a, bs, *, name, tk=1024, tn=1024, ts=2048, relu2=False, into=None, col0=0, n_total=None):
    s, k = a.shape
    widths = [b.shape[1] for b in bs]
    n = sum(widths)
    tk, ts = _fit(tk, k), _fit(ts, s)
    for w in widths:
        tn = _fit(tn, w)
    while col0 % tn:
        tn //= 2
    nt = s // ts
    assert tn % LANES == 0 and all(sum(widths[:p]) % tn == 0 for p in range(len(bs))), (widths, tn)
    first = [sum(widths[:p]) // tn for p in range(len(bs))]
    tiles = [w // tn for w in widths]
    tile0 = col0 // tn

    def body(a_ref, *rest):
        b_refs, o_ref, acc = rest[:len(bs)], rest[-2], rest[-1]
        j, t = pl.program_id(1), pl.program_id(2)

        @pl.when(t == 0)
        def _():
            acc[...] = jnp.zeros_like(acc)
        av = a_ref[...]
        av = _relu2(av) if relu2 else av.astype(BF16)
        for p, b_ref in enumerate(b_refs):
            def add(b_ref=b_ref):
                acc[...] += lax.dot_general(av, b_ref[...], _TN, preferred_element_type=F32)
            if len(bs) == 1:
                add()
            else:
                pl.when((j >= first[p]) & (j < first[p] + tiles[p]))(add)

        @pl.when(t == nt - 1)
        def _():
            o_ref[...] = acc[...].astype(BF16)

    def piece_spec(p):
        def index(i, j, t):
            mine = (j >= first[p]) & (j < first[p] + tiles[p])
            return jnp.where(mine, t, 0), jnp.where(mine, j - first[p], 0)
        return pl.BlockSpec((ts, tn), index)

    operands, extra_specs, aliases = [a, *bs], [], {}
    if into is not None:
        operands.append(into)
        extra_specs = [pl.BlockSpec(memory_space=pltpu.HBM)]
        aliases = {len(operands) - 1: 0}
    return pl.pallas_call(
        body, name=name, grid=(k // tk, n // tn, nt),
        in_specs=[pl.BlockSpec((ts, tk), lambda i, j, t: (t, i))] + [piece_spec(p) for p in range(len(bs))] + extra_specs,
        out_specs=pl.BlockSpec((tk, tn), lambda i, j, t: (i, tile0 + j)),
        out_shape=_sds((k, n_total or n), BF16),
        input_output_aliases=aliases,
        scratch_shapes=[pltpu.VMEM((tk, tn), F32)],
        compiler_params=_params(("parallel", "parallel", "arbitrary")),
    )(*operands)


def _mix_out_bwd(dx, mix, proj, ycv, yat, wout, wpc, wpa, g, gt, *, name, tm=512):
    s, d = dx.shape
    cw = wpc.shape[0]
    aw = wpa.shape[0]
    tm = min(tm, s)
    ga_blk = (3 * cw + 3 * aw) // d

    def body(dx_ref, mix_ref, ga_ref, gb_ref, ycv_ref, yat_ref, wout_ref, wpc_ref, wpa_ref, g_ref, gt_ref,
             dmix_ref, dycv_ref, dyat_ref, dyc_ref, do_ref, dgate_ref, dgt_ref, dg_ref):
        @pl.when(pl.program_id(0) == 0)
        def _():
            dgt_ref[...] = jnp.zeros_like(dgt_ref)
            dg_ref[...] = jnp.zeros_like(dg_ref)
        dxv = dx_ref[...]
        dn = dxv * gt_ref[...]
        dmix, xh = _rms_bwd(dn, mix_ref[...], g_ref[...])
        dgt_ref[...] += _colsum(dxv * (xh * g_ref[...]))
        dg_ref[...] += _colsum(dn * xh)
        dmixb = dmix.astype(BF16)
        dmix_ref[...] = dmixb
        dmerged = lax.dot_general(dmixb, wout_ref[...], _NT, preferred_element_type=F32)
        sga = _sigmoid(ga_ref[...].astype(F32))
        sgb = _sigmoid(gb_ref[...].astype(F32))
        dycv = (dmerged * sga).astype(BF16)
        dyat = (dmerged * sgb).astype(BF16)
        dycv_ref[...] = dycv
        dyat_ref[...] = dyat
        dgate_ref[:, 0:d] = (dmerged * ycv_ref[...].astype(F32) * (sga * (1.0 - sga))).astype(BF16)
        dgate_ref[:, d:2 * d] = (dmerged * yat_ref[...].astype(F32) * (sgb * (1.0 - sgb))).astype(BF16)
        dyc_ref[...] = lax.dot_general(dycv, wpc_ref[...], _NT, preferred_element_type=F32).astype(BF16)
        do_ref[...] = lax.dot_general(dyat, wpa_ref[...], _NT, preferred_element_type=F32).astype(BF16)

    def rows(w):
        return pl.BlockSpec((tm, w), lambda i: (i, 0))

    def full(a):
        return pl.BlockSpec(a.shape, lambda i: (0, 0))

    return pl.pallas_call(
        body, name=name, grid=(s // tm,),
        in_specs=[rows(d), rows(d), pl.BlockSpec((tm, d), lambda i: (i, ga_blk)),
                  pl.BlockSpec((tm, d), lambda i: (i, ga_blk + 1)), rows(d), rows(d),
                  full(wout), full(wpc), full(wpa), _vec_spec(d, 1), _vec_spec(d, 1)],
        out_specs=[rows(d), rows(d), rows(d), rows(cw), rows(aw), rows(2 * d), _vec_spec(d, 1), _vec_spec(d, 1)],
        out_shape=[_sds((s, d), BF16), _sds((s, d), BF16), _sds((s, d), BF16), _sds((s, cw), BF16),
                   _sds((s, aw), BF16), _sds((s, 2 * d), BF16), _sds((1, d), F32), _sds((1, d), F32)],
        compiler_params=_params(("arbitrary",)),
    )(dx, mix, proj, proj, ycv, yat, wout, wpc, wpa, g, gt)


def _conv_bwd(dyc, proj, conv_w, *, name, tm=512):
    s = proj.shape[0]
    cw = conv_w.shape[1]
    tm = min(tm, s)
    nb = tm // HALO
    nt = s // tm
    last_blk = s // HALO - 1

    def body(dyc_ref, bg_ref, cg_ref, u_ref, cgh_ref, uh_ref, dych_ref, bgh_ref, w_ref,
             dconv_ref, dw_ref):
        i = pl.program_id(0)

        @pl.when(i == 0)
        def _():
            dw_ref[...] = jnp.zeros_like(dw_ref)

        cg = cg_ref[...].astype(F32)
        u = u_ref[...].astype(F32)
        vv = cg * u
        halo = jnp.where(i > 0, cgh_ref[...].astype(F32) * uh_ref[...].astype(F32), 0.0)
        v1, v2 = _shifted_down(vv, halo)
        w = w_ref[...]
        y = w[2:3, :] * vv + w[1:2, :] * v1 + w[0:1, :] * v2
        dyc = dyc_ref[...].astype(F32)
        dconv_ref[:, 0:cw] = (dyc * y).astype(BF16)
        gy = dyc * bg_ref[...].astype(F32)
        nxt = jnp.where(i < nt - 1, dych_ref[...].astype(F32) * bgh_ref[...].astype(F32), 0.0)
        g1, g2 = _shifted_up(gy, nxt)
        dvv = w[2:3, :] * gy + w[1:2, :] * g1 + w[0:1, :] * g2
        dconv_ref[:, cw:2 * cw] = (dvv * u).astype(BF16)
        dconv_ref[:, 2 * cw:3 * cw] = (dvv * cg).astype(BF16)
        dw_ref[0:1, :] += _colsum(gy * v2)
        dw_ref[1:2, :] += _colsum(gy * v1)
        dw_ref[2:3, :] += _colsum(gy * vv)

    def prev(i):
        return jnp.maximum(i * nb - 1, 0)

    def nxt_blk(i):
        return jnp.minimum((i + 1) * nb, last_blk)

    def col(c):
        return pl.BlockSpec((tm, cw), lambda i: (i, c))

    return pl.pallas_call(
        body, name=name, grid=(nt,),
        in_specs=[col(0), col(0), col(1), col(2),
                  pl.BlockSpec((HALO, cw), lambda i: (prev(i), 1)), pl.BlockSpec((HALO, cw), lambda i: (prev(i), 2)),
                  pl.BlockSpec((HALO, cw), lambda i: (nxt_blk(i), 0)), pl.BlockSpec((HALO, cw), lambda i: (nxt_blk(i), 0)),
                  pl.BlockSpec((3, cw), lambda i: (0, 0))],
        out_specs=[pl.BlockSpec((tm, 3 * cw), lambda i: (i, 0)), pl.BlockSpec((3, cw), lambda i: (0, 0))],
        out_shape=[_sds((s, 3 * cw), BF16), _sds((3, cw), F32)],
        compiler_params=_params(("arbitrary",)),
    )(dyc, proj, proj, proj, proj, proj, dyc, proj, conv_w)


def _attn_bwd(proj, o, kept, do, tri, *, d, name, side=None):
    s = proj.shape[0]
    qb = tri.shape[0]
    chains = _fit(ATTN_CHAINS[1], s // qb)
    ng = s // (qb * chains)
    q0, k0, v0, hp = _attn_cols(d)

    def body(*refs):
        p, g = pl.program_id(0), pl.program_id(1)
        own, late_phases = _host_side(
            side, 9, 3, 2, refs, (p == 0) & (g == 0), (p == hp - 1) & (g == 0), (p == hp - 1) & (g == ng - 1))
        (q_ref, k_ref, v_ref, o_ref, do_ref, tri_ref, a_ref, b_ref, run_ref,
         dq_ref, dk_ref, dv_ref, dk_acc, dv_acc) = own

        @pl.when(g == 0)
        def _():
            dk_acc[...] = jnp.zeros_like(dk_acc)
            dv_acc[...] = jnp.zeros_like(dv_acc)

        tri_m = tri_ref[...]
        masks = [_head_mask(h) for h in range(2)]

        def first_steps(u):
            i = g * chains + u
            mine = slice(u * qb, (u + 1) * qb)
            dov = do_ref[mine, :]
            qs = _stack_heads(q_ref[mine, :] * ATTN_SCALE, masks)
            dos = _stack_heads(dov, masks)
            dprod = dov.astype(F32) * o_ref[mine, :]
            dtot = jnp.concatenate([jnp.sum(jnp.where(m, dprod, 0.0), axis=-1, keepdims=True) for m in masks], axis=0)

            def through(j, ab, beta, left, grun, dq_acc):
                rows = pl.ds(pl.multiple_of(j * qb, qb), qb)
                kb = k_ref[rows, :]
                da = lax.dot_general(dos, v_ref[rows, :], _NT, preferred_element_type=F32)
                gg = ab.astype(F32) * da
                gcs, gtotal = _running_sum(gg, tri_m)
                dzb = (gg - beta * (gg + (left - gcs))).astype(BF16)
                dq_acc = dq_acc + jnp.dot(_heads_to_lanes(dzb, qb), _stack_heads(kb, masks),
                                          preferred_element_type=F32)
                dk_add = lax.dot_general(dzb, qs, _TN, preferred_element_type=F32)
                dv_add = lax.dot_general(ab, dos, _TN, preferred_element_type=F32)
                return (grun + gtotal, dq_acc), (rows, dk_add, dv_add)

            def kept(slot, j, state):
                grun, dq_acc = state
                return through(j, a_ref[0, u, slot], b_ref[0, u, slot].astype(F32), dtot - grun, grun, dq_acc)

            def strip(j, state):
                run, grun, dq_acc = state
                z = lax.dot_general(qs, k_ref[pl.ds(pl.multiple_of(j * qb, qb), qb), :], _NT, preferred_element_type=F32)
                lg = _log_one_minus_sigmoid(z)
                cs, total = _running_sum(lg, tri_m)
                ab = jnp.exp(z + cs + run).astype(BF16)
                left = jnp.where(run > UNDERFLOW_LOG, dtot - grun, 0.0)
                (grun, dq_acc), adds = through(j, ab, 1.0 - jnp.exp(lg), left, grun, dq_acc)
                return (run + total, grun, dq_acc), adds

            state, adds0 = kept(0, i, (jnp.zeros((2 * qb, 1), F32), jnp.zeros((qb, LANES), F32)))
            state, adds1 = kept(1, jnp.maximum(i - 1, 0), state)
            return i, strip, (run_ref[0, u][:, 0:1], *state), (adds0, adds1)

        started = [first_steps(u) for u in range(chains)]
        for u, (i, strip, state, adds) in enumerate(started):
            for rows, dk_add, dv_add in adds:
                dk_acc[rows, :] += dk_add
                dv_acc[rows, :] += dv_add

            def more(st, strip=strip):
                state, (rows, dk_add, dv_add) = strip(st[0], st[1:])
                dk_acc[rows, :] += dk_add
                dv_acc[rows, :] += dv_add
                return (st[0] - 1, *state)

            state = lax.while_loop(lambda st: (st[0] >= 0) & (jnp.max(st[1]) > UNDERFLOW_LOG), more, (i - 2, *state))
            dq_ref[u * qb:(u + 1) * qb, :] = (state[3] * ATTN_SCALE).astype(BF16)

        @pl.when(g == ng - 1)
        def _():
            dk_ref[...] = dk_acc[...].astype(BF16)
            dv_ref[...] = dv_acc[...].astype(BF16)

        late_phases()

    aw = hp * LANES
    tq = qb * chains
    s_in, s_out, s_shapes, aliases, s_scratch = _side_specs(side, 9, 3)
    res = pl.pallas_call(
        body, name=name, grid=(hp, ng),
        in_specs=[pl.BlockSpec((tq, LANES), lambda p, i: (i, q0 + p)),
                  pl.BlockSpec((s, LANES), lambda p, i: (0, k0 + p)),
                  pl.BlockSpec((s, LANES), lambda p, i: (0, v0 + p)),
                  pl.BlockSpec((tq, LANES), lambda p, i: (i, p)),
                  pl.BlockSpec((tq, LANES), lambda p, i: (i, p)),
                  pl.BlockSpec((qb, qb), lambda p, i: (0, 0)),
                  pl.BlockSpec((1, chains, 2, 2 * qb, qb), lambda p, i: (p, i, 0, 0, 0)),
                  pl.BlockSpec((1, chains, 2, 2 * qb, qb), lambda p, i: (p, i, 0, 0, 0)),
                  pl.BlockSpec((1, chains, 2 * qb, LANES), lambda p, i: (p, i, 0, 0))] + s_in,
        out_specs=[pl.BlockSpec((tq, LANES), lambda p, i: (i, p)),
                   pl.BlockSpec((s, LANES), lambda p, i: (0, p)),
                   pl.BlockSpec((s, LANES), lambda p, i: (0, p))] + s_out,
        out_shape=[_sds((s, aw), BF16), _sds((s, aw), BF16), _sds((s, aw), BF16)] + s_shapes,
        input_output_aliases=aliases,
        scratch_shapes=[pltpu.VMEM((s, LANES), F32), pltpu.VMEM((s, LANES), F32)] + s_scratch,
        compiler_params=_params(("arbitrary", "arbitrary")),
    )(proj, proj, proj, o, do, tri, *kept, *([] if side is None else side.operands))
    return res[:3], res[3:]


def _hosted(hooks, kind, l, fn, *args, **kw):
    res, side_out = fn(*args, side=hooks.side(kind, l), **kw)
    hooks.done(kind, l, side_out)
    return res


def _layer_fwd(x, mod, gains, conv_w, tri, *, l, hooks, target=None):
    sh1, sc1, gt1, sh2, sc2, gt2 = mod
    g_pre_mix, g_post_mix, g_pre_mlp, g_post_mlp = gains
    d = x.shape[1]
    w = functools.partial(hooks.weight, l)
    h, proj = _hosted(hooks, "in_proj", l, _norm_mod_matmul, x, g_pre_mix, sc1, sh1, w("w_in"), name=f"in_proj_{l}")
    yc = _conv_fwd(proj, conv_w, name=f"conv_fwd_{l}")
    o, kept = _hosted(hooks, "attn_fwd", l, _attn_fwd, proj, tri, d=d, name=f"attn_fwd_{l}")
    ycv, yat, merged, mix, x1 = _mix_out(yc, o, proj, x, w("w_proj_conv"), w("w_proj_attn"), w("w_out"),
                                         g_post_mix, gt1, name=f"mix_out_{l}")
    (h2, a), _ = _norm_mod_matmul(x1, g_pre_mlp, sc2, sh2, w("w_mlp_in"), name=f"mlp_in_{l}")
    if target is None:
        ff, x2 = _mlp_out(a, x1, w("w_mlp_out"), g_post_mlp, gt2, name=f"mlp_out_{l}")
    else:
        ff, *x2 = _mlp_out_loss(a, x1, w("w_mlp_out"), g_post_mlp, gt2, target, name=f"mlp_out_{l}")
    saved = dict(x=x, h=h, proj=proj, yc=yc, o=o, kept=kept, ycv=ycv, yat=yat, merged=merged, mix=mix, x1=x1, h2=h2, a=a, ff=ff,
                 conv_w=conv_w, **{k: w(k) for k in BIG})
    return x2, saved


def _layer_bwd(dx2, sv, mod, gains, tri, *, l, hooks):
    sh1, sc1, gt1, sh2, sc2, gt2 = mod
    g_pre_mix, g_post_mix, g_pre_mlp, g_post_mlp = gains
    d = dx2.shape[1]
    dff, da, dgt2, dg_post_mlp = _mlp_out_bwd(dx2, sv["ff"], sv["a"], sv["w_mlp_out"], g_post_mlp, gt2,
                                              name=f"mlp_out_bwd_{l}")
    hooks.grad(l, "w_mlp_out", _matmul_tn(sv["a"], [dff], relu2=True, name=f"gw_mlp_out_{l}"))
    (dx1, dsh2, dsc2, dg_pre_mlp), _ = _matmul_nt_norm_bwd([da], sv["w_mlp_in"], sv["x1"], dx2, g_pre_mlp, sc2,
                                                           name=f"mlp_in_bwd_{l}")
    hooks.grad(l, "w_mlp_in", _matmul_tn(sv["h2"], [da], name=f"gw_mlp_in_{l}"))
    dmix, dycv, dyat, dyc, do, dgate, dgt1, dg_post_mix = _mix_out_bwd(
        dx1, sv["mix"], sv["proj"], sv["ycv"], sv["yat"], sv["w_out"], sv["w_proj_conv"], sv["w_proj_attn"],
        g_post_mix, gt1, name=f"mix_out_bwd_{l}")
    hooks.grad(l, "w_out", _matmul_tn(sv["merged"], [dmix], name=f"gw_out_{l}"))
    hooks.grad(l, "w_proj_conv", _matmul_tn(sv["yc"], [dycv], name=f"gw_proj_conv_{l}"))
    hooks.grad(l, "w_proj_attn", _matmul_tn(sv["o"], [dyat], name=f"gw_proj_attn_{l}"))
    dconv, g_conv_w = _conv_bwd(dyc, sv["proj"], sv["conv_w"], name=f"conv_bwd_{l}")
    dq, dk, dv = _hosted(hooks, "attn_bwd", l, _attn_bwd, sv["proj"], sv["o"], sv["kept"], do, tri, d=d,
                         name=f"attn_bwd_{l}")
    dproj = [dconv, dq, dk, dv, dgate]
    n_in = sv["w_in"].shape[1]
    gw_in = _matmul_tn(sv["h"], [dconv], tn=768, n_total=n_in, name=f"gw_in_conv_{l}")
    gw_in = _matmul_tn(sv["h"], [dq, dk, dv], into=gw_in, col0=dconv.shape[1], n_total=n_in, name=f"gw_in_attn_{l}")
    gw_in = _matmul_tn(sv["h"], [dgate], into=gw_in, col0=n_in - dgate.shape[1], n_total=n_in, name=f"gw_in_gate_{l}")
    hooks.grad(l, "w_in", gw_in)
    dx0, dsh1, dsc1, dg_pre_mix = _hosted(hooks, "in_proj_bwd", l, _matmul_nt_norm_bwd, dproj, sv["w_in"], sv["x"], dx1,
                                          g_pre_mix, sc1, name=f"in_proj_bwd_{l}")
    dmod = jnp.concatenate([dsh1, dsc1, dgt1, dsh2, dsc2, dgt2], axis=0)
    dgains = jnp.concatenate([dg_pre_mix, dg_post_mix, dg_pre_mlp, dg_post_mlp], axis=0)
    return dx0, g_conv_w, dmod, dgains


BIG = ("w_in", "w_proj_conv", "w_proj_attn", "w_out", "w_mlp_in", "w_mlp_out")
SHARD_AXIS = dict(w_in=1, w_proj_conv=1, w_proj_attn=1, w_out=0, w_mlp_in=1, w_mlp_out=0)


def _local_step(x, target, mods, gains, conv_w, hooks):
    depth = mods.shape[0]
    tri = _tri(ATTN_BLOCK)
    saved = []
    for l in range(depth):
        mod = [mods[l, k:k + 1] for k in range(N_MOD)]
        gl = [gains[l, k:k + 1] for k in range(4)]
        x, sv = _layer_fwd(x, mod, gl, conv_w[l], tri, l=l, hooks=hooks, target=target if l == depth - 1 else None)
        saved.append((sv, mod, gl))
    dx, loss = x
    dconv, dmods, dgains = [None] * depth, [None] * depth, [None] * depth
    for l in reversed(range(depth)):
        sv, mod, gl = saved[l]
        dx, dconv[l], dmods[l], dgains[l] = _layer_bwd(dx, sv, mod, gl, tri, l=l, hooks=hooks)
    return loss, dx, jnp.stack(dconv), jnp.stack(dmods), jnp.stack(dgains)


def _coords():
    return lax.axis_index("x"), lax.axis_index("y"), lax.axis_index("c")


def _flip(v, f):
    return 1 - v if f else v


def _all_gather_small(v, *, name):
    r, c_ = v.shape

    def body(v_ref, out_ref, send_sems, recv_sems, local_sem):
        x, y, c = _coords()
        me = 4 * x + 2 * y + c
        mine = pltpu.make_async_copy(v_ref, out_ref.at[me], local_sem)
        mine.start()
        copies = []
        for k in range(1, 8):
            fx, fy, fc = (k >> 2) & 1, (k >> 1) & 1, k & 1
            px, py, pc = _flip(x, fx), _flip(y, fy), _flip(c, fc)
            out = pltpu.make_async_remote_copy(src_ref=v_ref, dst_ref=out_ref.at[me], send_sem=send_sems.at[k - 1],
                                               recv_sem=recv_sems.at[k - 1], device_id=(px, py, pc), device_id_type=MESH)
            out.start()
            back = pltpu.make_async_remote_copy(src_ref=v_ref, dst_ref=out_ref.at[4 * px + 2 * py + pc],
                                                send_sem=send_sems.at[k - 1], recv_sem=recv_sems.at[k - 1],
                                                device_id=(px, py, pc), device_id_type=MESH)
            copies.append((out, back))
        for out, back in copies:
            back.wait_recv()
        for out, back in copies:
            out.wait_send()
        mine.wait()

    return pl.pallas_call(
        body, name=name,
        in_specs=[pl.BlockSpec(memory_space=pltpu.VMEM)],
        out_specs=pl.BlockSpec(memory_space=pltpu.VMEM),
        out_shape=_sds((8, r, c_), F32),
        scratch_shapes=[pltpu.SemaphoreType.DMA((7,)), pltpu.SemaphoreType.DMA((7,)), pltpu.SemaphoreType.DMA],
    )(v)


def _shard_dims(full_shape, axis):
    k, n = full_shape
    return (k // 4, n) if axis == 0 else (k, n // 4)


def _shard_window(ref, axis, chip, half, rows, cols):
    r0, rn = (0, rows) if half is None else (half * (rows // 2), rows // 2)
    if axis == 1:
        return ref.at[pl.ds(r0, rn), pl.ds(chip * cols, cols)]
    return ref.at[pl.ds(chip * rows + r0, rn), :]


def _cast_place(w, layer, axis, chip_arr, *, name, tr=256):
    _, rows, cols = w.shape
    tr = _fit(tr, rows)
    nb = rows // tr
    full = (rows * 4, cols) if axis == 0 else (rows, cols * 4)

    def body(chip_ref, w_ref, o_ref):
        o_ref[...] = w_ref[0].astype(BF16)

    if axis == 1:
        out_map = lambda i, chip: (i, chip[0])
    else:
        out_map = lambda i, chip: (chip[0] * nb + i, 0)
    grid_spec = pltpu.PrefetchScalarGridSpec(
        num_scalar_prefetch=1, grid=(nb,),
        in_specs=[pl.BlockSpec((1, tr, cols), lambda i, chip: (layer, i, 0))],
        out_specs=pl.BlockSpec((tr, cols), out_map))
    return pl.pallas_call(body, name=name, grid_spec=grid_spec, out_shape=_sds(full, BF16),
                          compiler_params=_params(("arbitrary",)))(chip_arr, w)


def _gather_side(fulls, axes):
    n = len(fulls)

    def copies(outs, sems):
        send_sems, recv_sems = sems
        x, y, c = _coords()
        chip = 2 * x + y
        sibling = (x, y, 1 - c)
        table = []
        for w in range(n):
            rows, cols = _shard_dims(outs[w].shape, axes[w])
            win = functools.partial(_shard_window, outs[w], axes[w], rows=rows, cols=cols)
            for j, (fx, fy) in enumerate(OTHER_CHIPS):
                px, py = _flip(x, fx), _flip(y, fy)
                pchip = 2 * px + py

                def copy(piece, sem, to):
                    return pltpu.make_async_remote_copy(src_ref=piece, dst_ref=piece, send_sem=send_sems.at[w, sem],
                                                        recv_sem=recv_sems.at[w, sem], device_id=to, device_id_type=MESH)

                table.append((copy(win(chip, c), j, (px, py, c)), copy(win(pchip, c), j, (px, py, c)),
                              copy(win(pchip, c), 3 + j, sibling), copy(win(pchip, 1 - c), 3 + j, sibling)))
        return table

    def start(ins, outs, sems):
        for send, _, _, _ in copies(outs, sems):
            send.start()

    def mid(ins, outs, sems):
        for _, landed, pass_on, _ in copies(outs, sems):
            landed.wait_recv()
            pass_on.start()

    def finish(ins, outs, sems):
        table = copies(outs, sems)
        for _, _, _, from_sibling in table:
            from_sibling.wait_recv()
        for send, _, pass_on, _ in table:
            send.wait_send()
            pass_on.wait_send()

    return _Side(fulls, [_sds(f.shape, f.dtype) for f in fulls], {w: w for w in range(n)},
                 [pltpu.SemaphoreType.DMA((n, 6)), pltpu.SemaphoreType.DMA((n, 6))], start, mid, finish)


def _exchange_side(grads, axes):
    n = len(grads)
    out_shapes = []
    for g, ax in zip(grads, axes):
        rows, cols = _shard_dims(g.shape, ax)
        out_shapes.append(_sds((7, rows // 2, cols), g.dtype))

    def copies(ins, outs, sems):
        send_sems, recv_sems = sems
        x, y, c = _coords()
        table = []
        for w in range(n):
            rows, cols = _shard_dims(ins[w].shape, axes[w])
            for k in range(1, 8):
                fx, fy, fc = (k >> 2) & 1, (k >> 1) & 1, k & 1
                px, py, pc = _flip(x, fx), _flip(y, fy), _flip(c, fc)
                piece = _shard_window(ins[w], axes[w], 2 * px + py, pc, rows, cols)
                table.append(pltpu.make_async_remote_copy(
                    src_ref=piece, dst_ref=outs[w].at[k - 1], send_sem=send_sems.at[w, k - 1],
                    recv_sem=recv_sems.at[w, k - 1], device_id=(px, py, pc), device_id_type=MESH))
        return table

    def start(ins, outs, sems):
        for cp in copies(ins, outs, sems):
            cp.start()

    def finish(ins, outs, sems):
        table = copies(ins, outs, sems)
        for cp in table:
            cp.wait_recv()
        for cp in table:
            cp.wait_send()

    return _Side(grads, out_shapes, {}, [pltpu.SemaphoreType.DMA((n, 7)), pltpu.SemaphoreType.DMA((n, 7))],
                 start, None, finish)


def _rs_sum_join(g, got, out_prev, layer, depth, axis, ids, *, name, tr=256):
    _, rows2, cols = got.shape
    tr = _fit(tr, rows2)
    nt = rows2 // tr
    if axis == 1:
        own_map = lambda i, ids_: (ids_[1] * nt + i, ids_[0])
    else:
        own_map = lambda i, ids_: ((ids_[0] * 2 + ids_[1]) * nt + i, 0)

    def body(ids_ref, g_ref, got_ref, *rest):
        out_ref, buf, local_sems, send_sems, recv_sem = rest[-5:]
        i = pl.program_id(0)
        x, y, c = _coords()
        sibling = (x, y, 1 - c)

        def copies(step, slot):
            rows_mine = pl.ds(c * rows2 + step * tr, tr)
            dst = out_ref.at[layer, rows_mine, :]
            keep = pltpu.make_async_copy(buf.at[slot], dst, local_sems.at[slot])
            give = pltpu.make_async_remote_copy(src_ref=buf.at[slot], dst_ref=dst, send_sem=send_sems.at[slot],
                                                recv_sem=recv_sem, device_id=sibling, device_id_type=MESH)
            return keep, give

        def drain(step, slot):
            keep, give = copies(step, slot)
            keep.wait()
            give.wait_send()

        slot = i % 2

        @pl.when(i >= 2)
        def _():
            drain(i - 2, slot)

        acc = g_ref[...].astype(F32)
        for k in range(7):
            acc = acc + got_ref[k].astype(F32)
        buf[slot] = acc
        keep, give = copies(i, slot)
        keep.start()
        give.start()

        @pl.when(i == nt - 1)
        def _():
            if nt >= 2:
                drain(nt - 2, (nt - 2) % 2)
            drain(nt - 1, (nt - 1) % 2)
            theirs = out_ref.at[layer, pl.ds((1 - c) * rows2, rows2), :]
            pltpu.make_async_remote_copy(src_ref=theirs, dst_ref=theirs, send_sem=send_sems.at[0], recv_sem=recv_sem,
                                         device_id=sibling, device_id_type=MESH).wait_recv()

    hbm = pl.BlockSpec(memory_space=pltpu.HBM)
    in_specs = [pl.BlockSpec((tr, cols), own_map), pl.BlockSpec((7, tr, cols), lambda i, ids_: (0, i, 0))]
    operands = [ids, g, got]
    aliases = {}
    if out_prev is not None:
        in_specs.append(hbm)
        operands.append(out_prev)
        aliases = {3: 0}
    grid_spec = pltpu.PrefetchScalarGridSpec(
        num_scalar_prefetch=1, grid=(nt,), in_specs=in_specs, out_specs=hbm,
        scratch_shapes=[pltpu.VMEM((2, tr, cols), F32), pltpu.SemaphoreType.DMA((2,)), pltpu.SemaphoreType.DMA((2,)),
                        pltpu.SemaphoreType.DMA])
    return pl.pallas_call(body, name=name, grid_spec=grid_spec, out_shape=_sds((depth, 2 * rows2, cols), F32),
                          input_output_aliases=aliases, compiler_params=_params(("arbitrary",)))(*operands)


MIX = ("w_proj_conv", "w_proj_attn", "w_out")


class _Schedule:
    def __init__(self, placed, depth, ids):
        self.placed, self.depth, self.ids = placed, depth, ids
        self.full, self.g, self.carried = {}, {}, None
        self.reduced = {k: None for k in BIG}
        first = [(0, "w_in")]
        self._landed(first, _side_call(self._gather(first), name="gather_w_in_0"))

    def _gather(self, keys):
        return _gather_side([self.placed[k] for k in keys], [SHARD_AXIS[k[1]] for k in keys])

    def _landed(self, keys, outs):
        for k, o in zip(keys, outs):
            self.full[k] = o

    def _exchange(self, keys):
        return _exchange_side([self.g[k] for k in keys], [SHARD_AXIS[k[1]] for k in keys])

    def _reduce(self, keys, got):
        for (l, name), pieces in zip(keys, got):
            self.reduced[name] = _rs_sum_join(self.g[(l, name)], pieces, self.reduced[name], l, self.depth,
                                              SHARD_AXIS[name], self.ids, name=f"rs_sum_join_{l}_{name}")

    def weight(self, l, name):
        return self.full[(l, name)]

    def grad(self, l, name, g):
        self.g[(l, name)] = g

    def side(self, kind, l):
        nxt = [(l + 1, "w_in")] if l + 1 < self.depth else []
        if kind == "in_proj":
            keys, make = [(l, k) for k in MIX + ("w_mlp_in",)], self._gather
        elif kind == "attn_fwd":
            keys, make = [(l, "w_mlp_out")] + nxt, self._gather
        elif kind == "attn_bwd":
            keys, make = [(l, k) for k in ("w_mlp_out", "w_mlp_in") + MIX], self._exchange
        else:
            keys, make = [(l, "w_in")], self._exchange
        self.carried = keys
        return make(keys)

    def done(self, kind, l, outs):
        (self._landed if kind in ("in_proj", "attn_fwd") else self._reduce)(self.carried, outs)


def _flat_rows(shape):
    rows = 1
    for s in shape[:-1]:
        rows *= s
    return rows, shape[-1]


def _row_tile(rows, cols, cap_bytes=2 * 1024 * 1024):
    t = rows
    while t * cols * 4 > cap_bytes and t % 16 == 0:
        t //= 2
    return t


def _ada_fwd(c_all, w_ada, b_loc, *, name, tn=512):
    l, d, nl = w_ada.shape
    b = c_all.shape[0]
    tn = min(tn, nl)

    def body(c_ref, w_ref, b_ref, o_ref):
        o_ref[0] = jnp.dot(c_ref[...], w_ref[0], preferred_element_type=F32,
                           precision=lax.Precision.HIGHEST) + b_ref[0]

    return pl.pallas_call(
        body, name=name, grid=(l, nl // tn),
        in_specs=[pl.BlockSpec((b, d), lambda i, j: (0, 0)), pl.BlockSpec((1, d, tn), lambda i, j: (i, 0, j)),
                  pl.BlockSpec((1, 1, tn), lambda i, j: (i, 0, j))],
        out_specs=pl.BlockSpec((1, b, tn), lambda i, j: (i, 0, j)),
        out_shape=_sds((l, b, nl), F32),
        compiler_params=_params(("parallel", "parallel")),
    )(c_all, w_ada, b_loc)


def _ada_bwd(c_t, dmod_loc, *, name, tn=512):
    d, b = c_t.shape
    l, _, nl = dmod_loc.shape
    tn = min(tn, nl)

    def body(c_ref, dm_ref, o_ref):
        cv = c_ref[...]
        dm = dm_ref[0]
        acc = cv[:, 0:1] * dm[0:1, :]
        for k in range(1, b):
            acc = acc + cv[:, k:k + 1] * dm[k:k + 1, :]
        o_ref[0] = acc

    return pl.pallas_call(
        body, name=name, grid=(l, nl // tn),
        in_specs=[pl.BlockSpec((d, b), lambda i, j: (0, 0)), pl.BlockSpec((1, b, tn), lambda i, j: (i, 0, j))],
        out_specs=pl.BlockSpec((1, d, tn), lambda i, j: (i, 0, j)),
        out_shape=_sds((l, d, nl), F32),
        compiler_params=_params(("parallel", "parallel")),
    )(c_t, dmod_loc)


def _sum_devices(p, *, name):
    k, r, c_ = p.shape

    def body(p_ref, o_ref):
        acc = p_ref[0]
        for j in range(1, k):
            acc = acc + p_ref[j]
        o_ref[...] = acc

    return pl.pallas_call(body, name=name, out_shape=_sds((r, c_), F32),
                          in_specs=[pl.BlockSpec(memory_space=pltpu.VMEM)],
                          out_specs=pl.BlockSpec(memory_space=pltpu.VMEM))(p)


def _adamw(w, g, m, v, *, name):
    shape = w.shape
    rows, cols = _flat_rows(shape)
    tr = _row_tile(rows, cols, cap_bytes=1024 * 1024)
    c1 = 1.0 / (1.0 - ADAM_B1 ** ADAM_STEP)
    c2 = 1.0 / (1.0 - ADAM_B2 ** ADAM_STEP)

    def body(w_ref, g_ref, m_ref, v_ref, go_ref, d_ref, nm_ref, nv_ref):
        gv = g_ref[...]
        nm = ADAM_B1 * m_ref[...] + (1.0 - ADAM_B1) * gv
        nv = ADAM_B2 * v_ref[...] + (1.0 - ADAM_B2) * (gv * gv)
        m_hat = nm * c1
        v_hat = nv * c2
        go_ref[...] = gv
        d_ref[...] = -ADAM_LR * (m_hat / (jnp.sqrt(v_hat) + ADAM_EPS) + ADAM_WD * w_ref[...])
        nm_ref[...] = nm
        nv_ref[...] = nv

    spec = pl.BlockSpec((tr, cols), lambda i: (i, 0))
    flat = lambda a: a.reshape(rows, cols)
    outs = pl.pallas_call(body, name=name, grid=(rows // tr,), in_specs=[spec] * 4, out_specs=[spec] * 4,
                          out_shape=[_sds((rows, cols), F32)] * 4, compiler_params=_params(("parallel",)),
                          )(flat(w), flat(g), flat(m), flat(v))
    return tuple(o.reshape(shape) for o in outs)


WEIGHTS = ("w_ada", "b_ada", "g_pre_mix", "g_post_mix", "g_pre_mlp", "g_post_mlp", "w_in", "conv_w",
           "w_proj_conv", "w_proj_attn", "w_out", "w_mlp_in", "w_mlp_out")
GAINS = ("g_pre_mix", "g_post_mix", "g_pre_mlp", "g_post_mlp")


def kernel(x, c, w_ada, b_ada, g_pre_mix, g_post_mix, g_pre_mlp, g_post_mlp, w_in, conv_w, w_proj_conv, w_proj_attn, w_out, w_mlp_in, w_mlp_out, loss_target, m_w_ada, m_b_ada, m_g_pre_mix, m_g_post_mix, m_g_pre_mlp, m_g_post_mlp, m_w_in, m_conv_w, m_w_proj_conv, m_w_proj_attn, m_w_out, m_w_mlp_in, m_w_mlp_out, v_w_ada, v_b_ada, v_g_pre_mix, v_g_post_mix, v_g_pre_mlp, v_g_post_mlp, v_w_in, v_conv_w, v_w_proj_conv, v_w_proj_attn, v_w_out, v_w_mlp_in, v_w_mlp_out):
    params = dict(w_ada=w_ada, b_ada=b_ada, g_pre_mix=g_pre_mix, g_post_mix=g_post_mix, g_pre_mlp=g_pre_mlp,
                  g_post_mlp=g_post_mlp, w_in=w_in, conv_w=conv_w, w_proj_conv=w_proj_conv, w_proj_attn=w_proj_attn,
                  w_out=w_out, w_mlp_in=w_mlp_in, w_mlp_out=w_mlp_out)
    m_in = dict(w_ada=m_w_ada, b_ada=m_b_ada, g_pre_mix=m_g_pre_mix, g_post_mix=m_g_post_mix, g_pre_mlp=m_g_pre_mlp,
                g_post_mlp=m_g_post_mlp, w_in=m_w_in, conv_w=m_conv_w, w_proj_conv=m_w_proj_conv,
                w_proj_attn=m_w_proj_attn, w_out=m_w_out, w_mlp_in=m_w_mlp_in, w_mlp_out=m_w_mlp_out)
    v_in = dict(w_ada=v_w_ada, b_ada=v_b_ada, g_pre_mix=v_g_pre_mix, g_post_mix=v_g_post_mix, g_pre_mlp=v_g_pre_mlp,
                g_post_mlp=v_g_post_mlp, w_in=v_w_in, conv_w=v_conv_w, w_proj_conv=v_w_proj_conv,
                w_proj_attn=v_w_proj_attn, w_out=v_w_out, w_mlp_in=v_w_mlp_in, w_mlp_out=v_w_mlp_out)

    depth, d, nl_ada = w_ada.shape
    ix, iy, ic = lax.axis_index("x"), lax.axis_index("y"), lax.axis_index("c")
    chip = 2 * ix + iy
    me = 4 * ix + 2 * iy + ic
    xs = x[0]
    target = loss_target[0]

    c_all = _all_gather_small(jnp.broadcast_to(c, (8, d)), name="gather_c")[:, 0, :]
    b_loc = lax.dynamic_slice_in_dim(b_ada, chip * nl_ada, nl_ada, axis=1)[:, None, :]
    mod_loc = _ada_fwd(c_all, w_ada, b_loc, name="ada_fwd")
    mod_all = _all_gather_small(mod_loc.reshape(depth * 8, nl_ada), name="gather_mod")
    mod_all = mod_all.reshape(4, 2, depth, 8, nl_ada)[:, 0]
    mod_me = lax.dynamic_index_in_dim(mod_all, me, axis=2, keepdims=False)
    mods = jnp.transpose(mod_me, (1, 0, 2)).reshape(depth, N_MOD, d)

    chip_arr = jnp.reshape(chip, (1,)).astype(jnp.int32)
    ids = jnp.stack([chip, ic]).astype(jnp.int32)
    placed = {(l, k): _cast_place(params[k], l, SHARD_AXIS[k], chip_arr, name=f"place_{k}_{l}")
              for l in range(depth) for k in BIG}
    conv_full = _all_gather_small(
        jnp.pad(conv_w.reshape(depth * 3, -1), ((0, 8 - depth * 3), (0, 0))), name="gather_conv_w")
    conv_full = conv_full.reshape(4, 2, 8, -1)[:, 0, :depth * 3]
    conv_full = jnp.transpose(conv_full, (1, 0, 2)).reshape(depth, 3, -1)

    gains = jnp.stack([params[k] for k in GAINS], axis=1)
    schedule = _Schedule(placed, depth, ids)
    loss, dx, conv_grads, dmods, dgains = _local_step(xs, target, mods, gains, conv_full, schedule)

    cw = conv_full.shape[2]
    rows = [dmods.reshape(depth * N_MOD, d), dgains.reshape(depth * 4, d),
            conv_grads.reshape(-1, d), jnp.broadcast_to(loss, (1, d))]
    payload = jnp.concatenate(rows, axis=0)
    n_rows = payload.shape[0]
    pad = (-n_rows) % 8
    payload = jnp.pad(payload, ((0, pad), (0, 0)))
    everyone = _all_gather_small(payload, name="gather_small_grads")
    total = _sum_devices(everyone, name="sum_small_grads")
    r0 = depth * N_MOD
    grads = {}
    grads["b_ada"] = total[:r0].reshape(depth, N_MOD * d)
    gsum = total[r0:r0 + depth * 4].reshape(depth, 4, d)
    for k, name in enumerate(GAINS):
        grads[name] = gsum[:, k]
    r1 = r0 + depth * 4
    n_conv = (depth * 3 * cw) // d
    conv_g = total[r1:r1 + n_conv].reshape(depth, 3, cw)
    grads["conv_w"] = lax.dynamic_slice_in_dim(conv_g, chip * (cw // 4), cw // 4, axis=2)
    loss_out = total[r1 + n_conv, 0]
    dmod_all = everyone[:, :r0].reshape(8, depth, N_MOD * d)
    dmod_loc = lax.dynamic_slice_in_dim(dmod_all, chip * nl_ada, nl_ada, axis=2)
    grads["w_ada"] = _ada_bwd(c_all.T, jnp.transpose(dmod_loc, (1, 0, 2)), name="ada_bwd")

    for k in BIG:
        grads[k] = schedule.reduced[k]

    deltas, new_m, new_v = {}, {}, {}
    for k in WEIGHTS:
        grads[k], deltas[k], new_m[k], new_v[k] = _adamw(params[k], grads[k], m_in[k], v_in[k], name=f"adamw_{k}")

    return (loss_out, dx[None], *[grads[k] for k in WEIGHTS], *[deltas[k] for k in WEIGHTS],
            *[new_m[k] for k in WEIGHTS], *[new_v[k] for k in WEIGHTS])
```

```python
import functools

import jax
import jax.numpy as jnp
from jax import lax
from jax.experimental import pallas as pl
from jax.experimental.pallas import tpu as pltpu

F32 = jnp.float32
BF16 = jnp.bfloat16
EPS = 1e-6
N_MOD = 6
HEAD_DIM = 64
LANES = 128
ATTN_SCALE = 1.0 / 8.0
UNDERFLOW_LOG = -90.0
ATTN_BLOCK = 256
ATTN_CHAINS = (8, 4)
ADAM_LR = 0.001
ADAM_B1 = 0.9
ADAM_B2 = 0.999
ADAM_EPS = 1e-08
ADAM_WD = 0.01
ADAM_STEP = 10
VMEM_LIMIT = 56 * 1024 * 1024
MESH = pl.DeviceIdType.MESH
OTHER_CHIPS = ((1, 0), (0, 1), (1, 1))

_NT = (((1,), (1,)), ((), ()))
_TN = (((0,), (0,)), ((), ()))


def _sds(shape, dtype):
    return jax.ShapeDtypeStruct(shape, dtype)


def _params(sem):
    return pltpu.CompilerParams(dimension_semantics=sem, vmem_limit_bytes=VMEM_LIMIT)


def _fit(t, n):
    t = min(t, n)
    while n % t:
        t //= 2
    return t


def _vec_spec(d, nargs=1):
    assert nargs == 1
    return pl.BlockSpec((1, d), lambda i: (0, 0))


def _log_one_minus_sigmoid(z):
    return -jnp.log(1.0 + jnp.exp(-jnp.abs(z))) - jnp.maximum(z, 0.0)


def _sigmoid(z):
    return 0.5 * jnp.tanh(0.5 * z) + 0.5


def _split_bf16(a):
    hi = a.astype(BF16)
    lo = (a - hi.astype(F32)).astype(BF16)
    return hi, lo


def _rms_bwd(dn, xin, g):
    r = lax.rsqrt(jnp.mean(xin * xin, axis=-1, keepdims=True) + EPS)
    xh = xin * r
    dxh = dn * g
    dxin = r * (dxh - xh * jnp.mean(dxh * xh, axis=-1, keepdims=True))
    return dxin, xh


def _colsum(a):
    return jnp.sum(a, axis=0, keepdims=True)


def _norm_mod_matmul(x, g, sc, sh, w, *, name, tm=512, side=None):
    s, d = x.shape
    n = w.shape[1]
    tm = _fit(tm, s)
    nt = s // tm

    def body(*refs):
        i = pl.program_id(0)
        (x_ref, g_ref, sc_ref, sh_ref, w_ref, h_ref, o_ref), late_phases = _host_side(
            side, 5, 2, 0, refs, i == 0, i == (3 * nt) // 4, i == nt - 1)
        xv = x_ref[...]
        r = lax.rsqrt(jnp.mean(xv * xv, axis=-1, keepdims=True) + EPS)
        h = ((xv * r * g_ref[...]) * (1.0 + sc_ref[...]) + sh_ref[...]).astype(BF16)
        h_ref[...] = h
        o_ref[...] = jnp.dot(h, w_ref[...], preferred_element_type=F32).astype(BF16)
        late_phases()

    s_in, s_out, s_shapes, aliases, s_scratch = _side_specs(side, 5, 2)
    res = pl.pallas_call(
        body, name=name, grid=(nt,),
        in_specs=[pl.BlockSpec((tm, d), lambda i: (i, 0)), _vec_spec(d, 1), _vec_spec(d, 1), _vec_spec(d, 1),
                  pl.BlockSpec((d, n), lambda i: (0, 0))] + s_in,
        out_specs=[pl.BlockSpec((tm, d), lambda i: (i, 0)), pl.BlockSpec((tm, n), lambda i: (i, 0))] + s_out,
        out_shape=[_sds((s, d), BF16), _sds((s, n), BF16)] + s_shapes,
        input_output_aliases=aliases, scratch_shapes=s_scratch,
        compiler_params=_params(("arbitrary",)),
    )(x, g, sc, sh, w, *([] if side is None else side.operands))
    return res[:2], res[2:]


HALO = 16


def _shifted_down(a, before):
    row = lax.broadcasted_iota(jnp.int32, a.shape, 0)
    last, last2 = before[-1:, :], before[-2:-1, :]
    one = jnp.where(row == 0, last, pltpu.roll(a, 1, axis=0))
    two = jnp.where(row == 0, last2, jnp.where(row == 1, last, pltpu.roll(a, 2, axis=0)))
    return one, two


def _shifted_up(a, after):
    n = a.shape[0]
    row = lax.broadcasted_iota(jnp.int32, a.shape, 0)
    first, second = after[0:1, :], after[1:2, :]
    one = jnp.where(row == n - 1, first, pltpu.roll(a, n - 1, axis=0))
    two = jnp.where(row == n - 1, second, jnp.where(row == n - 2, first, pltpu.roll(a, n - 2, axis=0)))
    return one, two


def _conv_fwd(proj, conv_w, *, name, tm=512):
    s = proj.shape[0]
    cw = conv_w.shape[1]
    tm = min(tm, s)
    nb = tm // HALO

    def body(bg_ref, cg_ref, u_ref, cgh_ref, uh_ref, w_ref, yc_ref):
        i = pl.program_id(0)
        vv = cg_ref[...].astype(F32) * u_ref[...].astype(F32)
        halo = jnp.where(i > 0, cgh_ref[...].astype(F32) * uh_ref[...].astype(F32), 0.0)
        v1, v2 = _shifted_down(vv, halo)
        w = w_ref[...]
        y = w[2:3, :] * vv + w[1:2, :] * v1 + w[0:1, :] * v2
        yc_ref[...] = (bg_ref[...].astype(F32) * y).astype(BF16)

    def prev(i):
        return jnp.maximum(i * nb - 1, 0)

    return pl.pallas_call(
        body, name=name, grid=(s // tm,),
        in_specs=[pl.BlockSpec((tm, cw), lambda i: (i, 0)), pl.BlockSpec((tm, cw), lambda i: (i, 1)),
                  pl.BlockSpec((tm, cw), lambda i: (i, 2)),
                  pl.BlockSpec((HALO, cw), lambda i: (prev(i), 1)), pl.BlockSpec((HALO, cw), lambda i: (prev(i), 2)),
                  pl.BlockSpec((3, cw), lambda i: (0, 0))],
        out_specs=pl.BlockSpec((tm, cw), lambda i: (i, 0)),
        out_shape=_sds((s, cw), BF16),
        compiler_params=_params(("arbitrary",)),
    )(proj, proj, proj, proj, proj, conv_w)


def _tri(qb):
    r = lax.broadcasted_iota(jnp.int32, (qb, qb), 0)
    c = lax.broadcasted_iota(jnp.int32, (qb, qb), 1)
    return (r >= c).astype(BF16)


def _head_mask(h):
    lane = lax.broadcasted_iota(jnp.int32, (1, LANES), 1)
    return (lane >= HEAD_DIM * h) & (lane < HEAD_DIM * (h + 1))


def _stack_heads(a, masks):
    return jnp.concatenate([jnp.where(m, a, 0).astype(BF16) for m in masks], axis=0)


def _heads_to_lanes(a, qb):
    return jnp.concatenate([a[:qb], a[qb:]], axis=1)


def _stacked_causal(qb, width, first_key, first_query):
    row = lax.broadcasted_iota(jnp.int32, (2 * qb, width), 0)
    col = lax.broadcasted_iota(jnp.int32, (2 * qb, width), 1)
    return first_key + col < first_query + jnp.where(row >= qb, row - qb, row)


def _running_sum(a, tri_m):
    rows, qb = a.shape[0], tri_m.shape[0]
    n = a.shape[1] // qb
    hi, lo = _split_bf16(a)
    stacked = jnp.concatenate([p[:, s * qb:(s + 1) * qb] for s in range(n) for p in (hi, lo)], axis=0)
    both = jnp.dot(stacked, tri_m, preferred_element_type=F32)
    parts = [both[(2 * s) * rows:(2 * s + 1) * rows] + both[(2 * s + 1) * rows:(2 * s + 2) * rows] for s in range(n)]
    later = None
    for s in reversed(range(n)):
        if later is not None:
            parts[s] = parts[s] + later
        later = parts[s][:, 0:1]
    return (parts[0] if n == 1 else jnp.concatenate(parts, axis=1)), later


def _attn_cols(d):
    cw = d // 2
    hp = (d // 2) // LANES
    q0 = (3 * cw) // LANES
    return q0, q0 + hp, q0 + 2 * hp, hp


class _Side:
    def __init__(self, operands, out_shapes, aliases, scratch, start, mid, finish):
        self.operands, self.out_shapes, self.aliases, self.scratch = list(operands), list(out_shapes), aliases, list(scratch)
        self.start, self.mid, self.finish = start, mid, finish


def _side_call(side, *, name):
    n_in, n_out = len(side.operands), len(side.out_shapes)

    def body(*refs):
        parts = refs[:n_in], refs[n_in:n_in + n_out], refs[n_in + n_out:]
        side.start(*parts)
        if side.mid is not None:
            side.mid(*parts)
        side.finish(*parts)

    hbm = pl.BlockSpec(memory_space=pltpu.HBM)
    return pl.pallas_call(body, name=name, in_specs=[hbm] * n_in, out_specs=[hbm] * n_out, out_shape=side.out_shapes,
                          input_output_aliases=dict(side.aliases), scratch_shapes=side.scratch)(*side.operands)


def _host_side(side, n_in, n_out, n_scratch, refs, first, late, last):
    if side is None:
        return refs, lambda: None
    s_in, s_out = len(side.operands), len(side.out_shapes)
    ins = refs[:n_in]
    side_in = refs[n_in:n_in + s_in]
    outs = refs[n_in + s_in:n_in + s_in + n_out]
    side_out = refs[n_in + s_in + n_out:n_in + s_in + n_out + s_out]
    rest = refs[n_in + s_in + n_out + s_out:]
    scratch, sems = rest[:n_scratch], rest[n_scratch:]
    parts = (side_in, side_out, sems)
    pl.when(first)(lambda: side.start(*parts))

    def run_late_phases():
        if side.mid is not None:
            pl.when(late)(lambda: side.mid(*parts))
        pl.when(last)(lambda: side.finish(*parts))

    return (*ins, *outs, *scratch), run_late_phases


def _side_specs(side, n_in, n_out):
    if side is None:
        return [], [], [], {}, []
    hbm = pl.BlockSpec(memory_space=pltpu.HBM)
    s_in = len(side.operands)
    aliases = {n_in + a: n_out + b for a, b in side.aliases.items()}
    return [hbm] * s_in, [hbm] * len(side.out_shapes), side.out_shapes, aliases, side.scratch


def _attn_fwd(proj, tri, *, d, name, side=None):
    s = proj.shape[0]
    qb = tri.shape[0]
    chains = _fit(ATTN_CHAINS[0], s // qb)
    ng = s // (qb * chains)
    q0, k0, v0, hp = _attn_cols(d)

    def body(*refs):
        p, g = pl.program_id(0), pl.program_id(1)
        (q_ref, k_ref, v_ref, tri_ref, o_ref, a_ref, b_ref, run_ref), late_phases = _host_side(
            side, 4, 4, 0, refs, (p == 0) & (g == 0), (p == hp - 1) & (g == 0), (p == hp - 1) & (g == ng - 1))
        tri_m = tri_ref[...]
        masks = [_head_mask(h) for h in range(2)]

        def first_steps(u):
            i = g * chains + u
            qs = _stack_heads(q_ref[u * qb:(u + 1) * qb, :] * ATTN_SCALE, masks)

            def strip(j, state, causal=None, keep=None, live=None):
                run, acc = state
                rows = pl.ds(pl.multiple_of(j * qb, qb), qb)
                z = lax.dot_general(qs, k_ref[rows, :], _NT, preferred_element_type=F32)
                lg = _log_one_minus_sigmoid(z)
                beta = 1.0 - jnp.exp(lg) if keep is not None else None
                if causal is not None:
                    lg = jnp.where(causal, lg, 0.0)
                cs, total = _running_sum(lg, tri_m)
                a = jnp.exp(z + cs + run)
                if causal is not None:
                    a = jnp.where(causal, a, 0.0)
                    beta = jnp.where(causal, beta, 0.0)
                if live is not None:
                    a = jnp.where(live, a, 0.0)
                    beta = jnp.where(live, beta, 0.0)
                    total = jnp.where(live, total, 0.0)
                ab = a.astype(BF16)
                if keep is not None:
                    a_ref[0, u, keep] = ab
                    b_ref[0, u, keep] = beta.astype(BF16)
                acc = acc + jnp.dot(_heads_to_lanes(ab, qb), _stack_heads(v_ref[rows, :], masks),
                                    preferred_element_type=F32)
                return run + total, acc

            state = strip(i, (jnp.zeros((2 * qb, 1), F32), jnp.zeros((qb, LANES), F32)),
                          causal=_stacked_causal(qb, qb, 0, 0), keep=0)
            state = strip(jnp.maximum(i - 1, 0), state, keep=1, live=i >= 1)
            run_ref[0, u] = jnp.broadcast_to(state[0], (2 * qb, LANES))
            return i, strip, state

        started = [first_steps(u) for u in range(chains)]
        for u, (i, strip, state) in enumerate(started):
            state = lax.while_loop(
                lambda st: (st[0] >= 0) & (jnp.max(st[1]) > UNDERFLOW_LOG),
                lambda st, strip=strip: (st[0] - 1, *strip(st[0], st[1:])),
                (i - 2, *state))
            o_ref[u * qb:(u + 1) * qb, :] = state[2]
        late_phases()

    s_in, s_out, s_shapes, aliases, s_scratch = _side_specs(side, 4, 4)
    tq = qb * chains
    nq = s // qb
    res = pl.pallas_call(
        body, name=name, grid=(hp, ng),
        in_specs=[pl.BlockSpec((tq, LANES), lambda p, i: (i, q0 + p)),
                  pl.BlockSpec((s, LANES), lambda p, i: (0, k0 + p)),
                  pl.BlockSpec((s, LANES), lambda p, i: (0, v0 + p)),
                  pl.BlockSpec((qb, qb), lambda p, i: (0, 0))] + s_in,
        out_specs=[pl.BlockSpec((tq, LANES), lambda p, i: (i, p)),
                   pl.BlockSpec((1, chains, 2, 2 * qb, qb), lambda p, i: (p, i, 0, 0, 0)),
                   pl.BlockSpec((1, chains, 2, 2 * qb, qb), lambda p, i: (p, i, 0, 0, 0)),
                   pl.BlockSpec((1, chains, 2 * qb, LANES), lambda p, i: (p, i, 0, 0))] + s_out,
        out_shape=[_sds((s, hp * LANES), F32), _sds((hp, nq, 2, 2 * qb, qb), BF16), _sds((hp, nq, 2, 2 * qb, qb), BF16),
                   _sds((hp, nq, 2 * qb, LANES), F32)] + s_shapes,
        input_output_aliases=aliases, scratch_shapes=s_scratch,
        compiler_params=_params(("arbitrary", "arbitrary")),
    )(proj, proj, proj, tri, *([] if side is None else side.operands))
    return (res[0], tuple(res[1:4])), res[4:]


def _mix_out(yc, o, proj, x, wpc, wpa, wout, g, gt, *, name, tm=512):
    s, d = x.shape
    cw = yc.shape[1]
    tm = min(tm, s)
    ga_blk = (3 * cw + 3 * (d // 2)) // d

    def body(yc_ref, o_ref, ga_ref, gb_ref, x_ref, wpc_ref, wpa_ref, wout_ref, g_ref, gt_ref,
             ycv_ref, yat_ref, mg_ref, mix_ref, x1_ref):
        y_conv = jnp.dot(yc_ref[...], wpc_ref[...], preferred_element_type=F32)
        y_attn = jnp.dot(o_ref[...].astype(BF16), wpa_ref[...], preferred_element_type=F32)
        merged = (_sigmoid(ga_ref[...].astype(F32)) * y_conv + _sigmoid(gb_ref[...].astype(F32)) * y_attn)
        mg = merged.astype(BF16)
        mix = jnp.dot(mg, wout_ref[...], preferred_element_type=F32)
        r = lax.rsqrt(jnp.mean(mix * mix, axis=-1, keepdims=True) + EPS)
        ycv_ref[...] = y_conv.astype(BF16)
        yat_ref[...] = y_attn.astype(BF16)
        mg_ref[...] = mg
        mix_ref[...] = mix
        x1_ref[...] = x_ref[...] + gt_ref[...] * (mix * r * g_ref[...])

    def rows(w):
        return pl.BlockSpec((tm, w), lambda i: (i, 0))

    def full(a):
        return pl.BlockSpec(a.shape, lambda i: (0, 0))

    return pl.pallas_call(
        body, name=name, grid=(s // tm,),
        in_specs=[rows(cw), rows(d // 2), pl.BlockSpec((tm, d), lambda i: (i, ga_blk)),
                  pl.BlockSpec((tm, d), lambda i: (i, ga_blk + 1)), rows(d),
                  full(wpc), full(wpa), full(wout), _vec_spec(d, 1), _vec_spec(d, 1)],
        out_specs=[rows(d), rows(d), rows(d), rows(d), rows(d)],
        out_shape=[_sds((s, d), BF16), _sds((s, d), BF16), _sds((s, d), BF16), _sds((s, d), F32), _sds((s, d), F32)],
        compiler_params=_params(("parallel",)),
    )(yc, o, proj, proj, x, wpc, wpa, wout, g, gt)


def _relu2(a):
    r = jnp.maximum(a.astype(F32), 0.0)
    return (r * r).astype(BF16)


def _mlp_out(a, x, w2, g, gt, *, name, tm=512):
    s, d = x.shape
    dff = a.shape[1]
    tm = min(tm, s)

    def body(a_ref, x_ref, w_ref, g_ref, gt_ref, ff_ref, x2_ref):
        ff = jnp.dot(_relu2(a_ref[...]), w_ref[...], preferred_element_type=F32)
        r = lax.rsqrt(jnp.mean(ff * ff, axis=-1, keepdims=True) + EPS)
        ff_ref[...] = ff
        x2_ref[...] = x_ref[...] + gt_ref[...] * (ff * r * g_ref[...])

    return pl.pallas_call(
        body, name=name, grid=(s // tm,),
        in_specs=[pl.BlockSpec((tm, dff), lambda i: (i, 0)), pl.BlockSpec((tm, d), lambda i: (i, 0)),
                  pl.BlockSpec((dff, d), lambda i: (0, 0)), _vec_spec(d, 1), _vec_spec(d, 1)],
        out_specs=[pl.BlockSpec((tm, d), lambda i: (i, 0)), pl.BlockSpec((tm, d), lambda i: (i, 0))],
        out_shape=[_sds((s, d), F32), _sds((s, d), F32)],
        compiler_params=_params(("parallel",)),
    )(a, x, w2, g, gt)


def _mlp_out_loss(a, x, w2, g, gt, target, *, name, tm=512):
    s, d = x.shape
    dff = a.shape[1]
    tm = min(tm, s)

    def body(a_ref, x_ref, w_ref, g_ref, gt_ref, t_ref, ff_ref, dy_ref, loss_ref):
        @pl.when(pl.program_id(0) == 0)
        def _():
            loss_ref[...] = jnp.zeros_like(loss_ref)
        ff = jnp.dot(_relu2(a_ref[...]), w_ref[...], preferred_element_type=F32)
        r = lax.rsqrt(jnp.mean(ff * ff, axis=-1, keepdims=True) + EPS)
        ff_ref[...] = ff
        e = (x_ref[...] + gt_ref[...] * (ff * r * g_ref[...])) - t_ref[...]
        dy_ref[...] = e * (1.0 / d)
        loss_ref[...] += 0.5 * jnp.sum(jnp.mean(e * e, axis=-1, keepdims=True), axis=0, keepdims=True)

    return pl.pallas_call(
        body, name=name, grid=(s // tm,),
        in_specs=[pl.BlockSpec((tm, dff), lambda i: (i, 0)), pl.BlockSpec((tm, d), lambda i: (i, 0)),
                  pl.BlockSpec((dff, d), lambda i: (0, 0)), _vec_spec(d, 1), _vec_spec(d, 1),
                  pl.BlockSpec((tm, d), lambda i: (i, 0))],
        out_specs=[pl.BlockSpec((tm, d), lambda i: (i, 0)), pl.BlockSpec((tm, d), lambda i: (i, 0)),
                   pl.BlockSpec((1, 1), lambda i: (0, 0))],
        out_shape=[_sds((s, d), F32), _sds((s, d), F32), _sds((1, 1), F32)],
        compiler_params=_params(("arbitrary",)),
    )(a, x, w2, g, gt, target)


def _mlp_out_bwd(dx, ff, a, w2, g, gt, *, name, tm=512):
    s, d = dx.shape
    dff = a.shape[1]
    tm = min(tm, s)

    def body(dx_ref, ff_ref, a_ref, w_ref, g_ref, gt_ref, dff_ref, da_ref, dgt_ref, dg_ref):
        @pl.when(pl.program_id(0) == 0)
        def _():
            dgt_ref[...] = jnp.zeros_like(dgt_ref)
            dg_ref[...] = jnp.zeros_like(dg_ref)
        dxv = dx_ref[...]
        dn = dxv * gt_ref[...]
        dffv, xh = _rms_bwd(dn, ff_ref[...], g_ref[...])
        dgt_ref[...] += _colsum(dxv * (xh * g_ref[...]))
        dg_ref[...] += _colsum(dn * xh)
        dffb = dffv.astype(BF16)
        dff_ref[...] = dffb
        drr = lax.dot_general(dffb, w_ref[...], _NT, preferred_element_type=F32)
        da_ref[...] = (drr * (2.0 * jnp.maximum(a_ref[...].astype(F32), 0.0))).astype(BF16)

    return pl.pallas_call(
        body, name=name, grid=(s // tm,),
        in_specs=[pl.BlockSpec((tm, d), lambda i: (i, 0)), pl.BlockSpec((tm, d), lambda i: (i, 0)),
                  pl.BlockSpec((tm, dff), lambda i: (i, 0)), pl.BlockSpec((dff, d), lambda i: (0, 0)),
                  _vec_spec(d, 1), _vec_spec(d, 1)],
        out_specs=[pl.BlockSpec((tm, d), lambda i: (i, 0)), pl.BlockSpec((tm, dff), lambda i: (i, 0)),
                   _vec_spec(d, 1), _vec_spec(d, 1)],
        out_shape=[_sds((s, d), BF16), _sds((s, dff), BF16), _sds((1, d), F32), _sds((1, d), F32)],
        compiler_params=_params(("arbitrary",)),
    )(dx, ff, a, w2, g, gt)


def _matmul_nt_norm_bwd(dys, w, x, dres, g, sc, *, name, tm=512, side=None):
    s = dys[0].shape[0]
    widths = [dy.shape[1] for dy in dys]
    d, n = w.shape
    assert sum(widths) == n, (widths, n)
    tm = _fit(tm, s)
    nt = s // tm
    np_ = len(dys)

    def body(*refs):
        i = pl.program_id(0)
        own, late_phases = _host_side(side, np_ + 5, 4, 0, refs, i == 0, i == (3 * nt) // 4, i == nt - 1)
        dy_refs = own[:np_]
        w_ref, x_ref, dres_ref, g_ref, sc_ref, dx_ref, dsh_ref, dsc_ref, dg_ref = own[np_:]

        @pl.when(i == 0)
        def _():
            dsh_ref[...] = jnp.zeros_like(dsh_ref)
            dsc_ref[...] = jnp.zeros_like(dsc_ref)
            dg_ref[...] = jnp.zeros_like(dg_ref)

        dh = None
        for p, dy_ref in enumerate(dy_refs):
            cols = slice(sum(widths[:p]), sum(widths[:p + 1]))
            part = lax.dot_general(dy_ref[...], w_ref[:, cols], _NT, preferred_element_type=F32)
            dh = part if dh is None else dh + part
        dn = dh * (1.0 + sc_ref[...])
        dxin, xh = _rms_bwd(dn, x_ref[...], g_ref[...])
        dsh_ref[...] += _colsum(dh)
        dsc_ref[...] += _colsum(dh * (xh * g_ref[...]))
        dg_ref[...] += _colsum(dn * xh)
        dx_ref[...] = dres_ref[...] + dxin
        late_phases()

    s_in, s_out, s_shapes, aliases, s_scratch = _side_specs(side, np_ + 5, 4)
    res = pl.pallas_call(
        body, name=name, grid=(nt,),
        in_specs=[pl.BlockSpec((tm, wd), lambda i: (i, 0)) for wd in widths]
        + [pl.BlockSpec((d, n), lambda i: (0, 0)),
           pl.BlockSpec((tm, d), lambda i: (i, 0)), pl.BlockSpec((tm, d), lambda i: (i, 0)),
           _vec_spec(d, 1), _vec_spec(d, 1)] + s_in,
        out_specs=[pl.BlockSpec((tm, d), lambda i: (i, 0)), _vec_spec(d, 1), _vec_spec(d, 1), _vec_spec(d, 1)] + s_out,
        out_shape=[_sds((s, d), F32), _sds((1, d), F32), _sds((1, d), F32), _sds((1, d), F32)] + s_shapes,
        input_output_aliases=aliases, scratch_shapes=s_scratch,
        compiler_params=_params(("arbitrary",)),
    )(*dys, w, x, dres, g, sc, *([] if side is None else side.operands))
    return res[:4], res[4:]


def _matmul_tn(a, bs, *, name, tk=1024, tn=1024, ts=2048, relu2=False, into=None, col0=0, n_total=None):
    s, k = a.shape
    widths = [b.shape[1] for b in bs]
    n = sum(widths)
    tk, ts = _fit(tk, k), _fit(ts, s)
    for w in widths:
        tn = _fit(tn, w)
    while col0 % tn:
        tn //= 2
    nt = s // ts
    assert tn % LANES == 0 and all(sum(widths[:p]) % tn == 0 for p in range(len(bs))), (widths, tn)
    first = [sum(widths[:p]) // tn for p in range(len(bs))]
    tiles = [w // tn for w in widths]
    tile0 = col0 // tn

    def body(a_ref, *rest):
        b_refs, o_ref, acc = rest[:len(bs)], rest[-2], rest[-1]
        j, t = pl.program_id(1), pl.program_id(2)

        @pl.when(t == 0)
        def _():
            acc[...] = jnp.zeros_like(acc)
        av = a_ref[...]
        av = _relu2(av) if relu2 else av.astype(BF16)
        for p, b_ref in enumerate(b_refs):
            def add(b_ref=b_ref):
                acc[...] += lax.dot_general(av, b_ref[...], _TN, preferred_element_type=F32)
            if len(bs) == 1:
                add()
            else:
                pl.when((j >= first[p]) & (j < first[p] + tiles[p]))(add)

        @pl.when(t == nt - 1)
        def _():
            o_ref[...] = acc[...].astype(BF16)

    def piece_spec(p):
        def index(i, j, t):
            mine = (j >= first[p]) & (j < first[p] + tiles[p])
            return jnp.where(mine, t, 0), jnp.where(mine, j - first[p], 0)
        return pl.BlockSpec((ts, tn), index)

    operands, extra_specs, aliases = [a, *bs], [], {}
    if into is not None:
        operands.append(into)
        extra_specs = [pl.BlockSpec(memory_space=pltpu.HBM)]
        aliases = {len(operands) - 1: 0}
    return pl.pallas_call(
        body, name=name, grid=(k // tk, n // tn, nt),
        in_specs=[pl.BlockSpec((ts, tk), lambda i, j, t: (t, i))] + [piece_spec(p) for p in range(len(bs))] + extra_specs,
        out_specs=pl.BlockSpec((tk, tn), lambda i, j, t: (i, tile0 + j)),
        out_shape=_sds((k, n_total or n), BF16),
        input_output_aliases=aliases,
        scratch_shapes=[pltpu.VMEM((tk, tn), F32)],
        compiler_params=_params(("parallel", "parallel", "arbitrary")),
    )(*operands)


def _mix_out_bwd(dx, mix, proj, ycv, yat, wout, wpc, wpa, g, gt, *, name, tm=512):
    s, d = dx.shape
    cw = wpc.shape[0]
    aw = wpa.shape[0]
    tm = min(tm, s)
    ga_blk = (3 * cw + 3 * aw) // d

    def body(dx_ref, mix_ref, ga_ref, gb_ref, ycv_ref, yat_ref, wout_ref, wpc_ref, wpa_ref, g_ref, gt_ref,
             dmix_ref, dycv_ref, dyat_ref, dyc_ref, do_ref, dgate_ref, dgt_ref, dg_ref):
        @pl.when(pl.program_id(0) == 0)
        def _():
            dgt_ref[...] = jnp.zeros_like(dgt_ref)
            dg_ref[...] = jnp.zeros_like(dg_ref)
        dxv = dx_ref[...]
        dn = dxv * gt_ref[...]
        dmix, xh = _rms_bwd(dn, mix_ref[...], g_ref[...])
        dgt_ref[...] += _colsum(dxv * (xh * g_ref[...]))
        dg_ref[...] += _colsum(dn * xh)
        dmixb = dmix.astype(BF16)
        dmix_ref[...] = dmixb
        dmerged = lax.dot_general(dmixb, wout_ref[...], _NT, preferred_element_type=F32)
        sga = _sigmoid(ga_ref[...].astype(F32))
        sgb = _sigmoid(gb_ref[...].astype(F32))
        dycv = (dmerged * sga).astype(BF16)
        dyat = (dmerged * sgb).astype(BF16)
        dycv_ref[...] = dycv
        dyat_ref[...] = dyat
        dgate_ref[:, 0:d] = (dmerged * ycv_ref[...].astype(F32) * (sga * (1.0 - sga))).astype(BF16)
        dgate_ref[:, d:2 * d] = (dmerged * yat_ref[...].astype(F32) * (sgb * (1.0 - sgb))).astype(BF16)
        dyc_ref[...] = lax.dot_general(dycv, wpc_ref[...], _NT, preferred_element_type=F32).astype(BF16)
        do_ref[...] = lax.dot_general(dyat, wpa_ref[...], _NT, preferred_element_type=F32).astype(BF16)

    def rows(w):
        return pl.BlockSpec((tm, w), lambda i: (i, 0))

    def full(a):
        return pl.BlockSpec(a.shape, lambda i: (0, 0))

    return pl.pallas_call(
        body, name=name, grid=(s // tm,),
        in_specs=[rows(d), rows(d), pl.BlockSpec((tm, d), lambda i: (i, ga_blk)),
                  pl.BlockSpec((tm, d), lambda i: (i, ga_blk + 1)), rows(d), rows(d),
                  full(wout), full(wpc), full(wpa), _vec_spec(d, 1), _vec_spec(d, 1)],
        out_specs=[rows(d), rows(d), rows(d), rows(cw), rows(aw), rows(2 * d), _vec_spec(d, 1), _vec_spec(d, 1)],
        out_shape=[_sds((s, d), BF16), _sds((s, d), BF16), _sds((s, d), BF16), _sds((s, cw), BF16),
                   _sds((s, aw), BF16), _sds((s, 2 * d), BF16), _sds((1, d), F32), _sds((1, d), F32)],
        compiler_params=_params(("arbitrary",)),
    )(dx, mix, proj, proj, ycv, yat, wout, wpc, wpa, g, gt)


def _conv_bwd(dyc, proj, conv_w, *, name, tm=512):
    s = proj.shape[0]
    cw = conv_w.shape[1]
    tm = min(tm, s)
    nb = tm // HALO
    nt = s // tm
    last_blk = s // HALO - 1

    def body(dyc_ref, bg_ref, cg_ref, u_ref, cgh_ref, uh_ref, dych_ref, bgh_ref, w_ref,
             dconv_ref, dw_ref):
        i = pl.program_id(0)

        @pl.when(i == 0)
        def _():
            dw_ref[...] = jnp.zeros_like(dw_ref)

        cg = cg_ref[...].astype(F32)
        u = u_ref[...].astype(F32)
        vv = cg * u
        halo = jnp.where(i > 0, cgh_ref[...].astype(F32) * uh_ref[...].astype(F32), 0.0)
        v1, v2 = _shifted_down(vv, halo)
        w = w_ref[...]
        y = w[2:3, :] * vv + w[1:2, :] * v1 + w[0:1, :] * v2
        dyc = dyc_ref[...].astype(F32)
        dconv_ref[:, 0:cw] = (dyc * y).astype(BF16)
        gy = dyc * bg_ref[...].astype(F32)
        nxt = jnp.where(i < nt - 1, dych_ref[...].astype(F32) * bgh_ref[...].astype(F32), 0.0)
        g1, g2 = _shifted_up(gy, nxt)
        dvv = w[2:3, :] * gy + w[1:2, :] * g1 + w[0:1, :] * g2
        dconv_ref[:, cw:2 * cw] = (dvv * u).astype(BF16)
        dconv_ref[:, 2 * cw:3 * cw] = (dvv * cg).astype(BF16)
        dw_ref[0:1, :] += _colsum(gy * v2)
        dw_ref[1:2, :] += _colsum(gy * v1)
        dw_ref[2:3, :] += _colsum(gy * vv)

    def prev(i):
        return jnp.maximum(i * nb - 1, 0)

    def nxt_blk(i):
        return jnp.minimum((i + 1) * nb, last_blk)

    def col(c):
        return pl.BlockSpec((tm, cw), lambda i: (i, c))

    return pl.pallas_call(
        body, name=name, grid=(nt,),
        in_specs=[col(0), col(0), col(1), col(2),
                  pl.BlockSpec((HALO, cw), lambda i: (prev(i), 1)), pl.BlockSpec((HALO, cw), lambda i: (prev(i), 2)),
                  pl.BlockSpec((HALO, cw), lambda i: (nxt_blk(i), 0)), pl.BlockSpec((HALO, cw), lambda i: (nxt_blk(i), 0)),
                  pl.BlockSpec((3, cw), lambda i: (0, 0))],
        out_specs=[pl.BlockSpec((tm, 3 * cw), lambda i: (i, 0)), pl.BlockSpec((3, cw), lambda i: (0, 0))],
        out_shape=[_sds((s, 3 * cw), BF16), _sds((3, cw), F32)],
        compiler_params=_params(("arbitrary",)),
    )(dyc, proj, proj, proj, proj, proj, dyc, proj, conv_w)


def _attn_bwd(proj, o, kept, do, tri, *, d, name, side=None):
    s = proj.shape[0]
    qb = tri.shape[0]
    chains = _fit(ATTN_CHAINS[1], s // qb)
    ng = s // (qb * chains)
    q0, k0, v0, hp = _attn_cols(d)

    def body(*refs):
        p, g = pl.program_id(0), pl.program_id(1)
        own, late_phases = _host_side(
            side, 9, 3, 2, refs, (p == 0) & (g == 0), (p == hp - 1) & (g == 0), (p == hp - 1) & (g == ng - 1))
        (q_ref, k_ref, v_ref, o_ref, do_ref, tri_ref, a_ref, b_ref, run_ref,
         dq_ref, dk_ref, dv_ref, dk_acc, dv_acc) = own

        @pl.when(g == 0)
        def _():
            dk_acc[...] = jnp.zeros_like(dk_acc)
            dv_acc[...] = jnp.zeros_like(dv_acc)

        tri_m = tri_ref[...]
        masks = [_head_mask(h) for h in range(2)]

        def first_steps(u):
            i = g * chains + u
            mine = slice(u * qb, (u + 1) * qb)
            dov = do_ref[mine, :]
            qs = _stack_heads(q_ref[mine, :] * ATTN_SCALE, masks)
            dos = _stack_heads(dov, masks)
            dprod = dov.astype(F32) * o_ref[mine, :]
            dtot = jnp.concatenate([jnp.sum(jnp.where(m, dprod, 0.0), axis=-1, keepdims=True) for m in masks], axis=0)

            def through(j, ab, beta, left, grun, dq_acc):
                rows = pl.ds(pl.multiple_of(j * qb, qb), qb)
                kb = k_ref[rows, :]
                da = lax.dot_general(dos, v_ref[rows, :], _NT, preferred_element_type=F32)
                gg = ab.astype(F32) * da
                gcs, gtotal = _running_sum(gg, tri_m)
                dzb = (gg - beta * (gg + (left - gcs))).astype(BF16)
                dq_acc = dq_acc + jnp.dot(_heads_to_lanes(dzb, qb), _stack_heads(kb, masks),
                                          preferred_element_type=F32)
                dk_add = lax.dot_general(dzb, qs, _TN, preferred_element_type=F32)
                dv_add = lax.dot_general(ab, dos, _TN, preferred_element_type=F32)
                return (grun + gtotal, dq_acc), (rows, dk_add, dv_add)

            def kept(slot, j, state):
                grun, dq_acc = state
                return through(j, a_ref[0, u, slot], b_ref[0, u, slot].astype(F32), dtot - grun, grun, dq_acc)

            def strip(j, state):
                run, grun, dq_acc = state
                z = lax.dot_general(qs, k_ref[pl.ds(pl.multiple_of(j * qb, qb), qb), :], _NT, preferred_element_type=F32)
                lg = _log_one_minus_sigmoid(z)
                cs, total = _running_sum(lg, tri_m)
                ab = jnp.exp(z + cs + run).astype(BF16)
                left = jnp.where(run > UNDERFLOW_LOG, dtot - grun, 0.0)
                (grun, dq_acc), adds = through(j, ab, 1.0 - jnp.exp(lg), left, grun, dq_acc)
                return (run + total, grun, dq_acc), adds

            state, adds0 = kept(0, i, (jnp.zeros((2 * qb, 1), F32), jnp.zeros((qb, LANES), F32)))
            state, adds1 = kept(1, jnp.maximum(i - 1, 0), state)
            return i, strip, (run_ref[0, u][:, 0:1], *state), (adds0, adds1)

        started = [first_steps(u) for u in range(chains)]
        for u, (i, strip, state, adds) in enumerate(started):
            for rows, dk_add, dv_add in adds:
                dk_acc[rows, :] += dk_add
                dv_acc[rows, :] += dv_add

            def more(st, strip=strip):
                state, (rows, dk_add, dv_add) = strip(st[0], st[1:])
                dk_acc[rows, :] += dk_add
                dv_acc[rows, :] += dv_add
                return (st[0] - 1, *state)

            state = lax.while_loop(lambda st: (st[0] >= 0) & (jnp.max(st[1]) > UNDERFLOW_LOG), more, (i - 2, *state))
            dq_ref[u * qb:(u + 1) * qb, :] = (state[3] * ATTN_SCALE).astype(BF16)

        @pl.when(g == ng - 1)
        def _():
            dk_ref[...] = dk_acc[...].astype(BF16)
            dv_ref[...] = dv_acc[...].astype(BF16)

        late_phases()

    aw = hp * LANES
    tq = qb * chains
    s_in, s_out, s_shapes, aliases, s_scratch = _side_specs(side, 9, 3)
    res = pl.pallas_call(
        body, name=name, grid=(hp, ng),
        in_specs=[pl.BlockSpec((tq, LANES), lambda p, i: (i, q0 + p)),
                  pl.BlockSpec((s, LANES), lambda p, i: (0, k0 + p)),
                  pl.BlockSpec((s, LANES), lambda p, i: (0, v0 + p)),
                  pl.BlockSpec((tq, LANES), lambda p, i: (i, p)),
                  pl.BlockSpec((tq, LANES), lambda p, i: (i, p)),
                  pl.BlockSpec((qb, qb), lambda p, i: (0, 0)),
                  pl.BlockSpec((1, chains, 2, 2 * qb, qb), lambda p, i: (p, i, 0, 0, 0)),
                  pl.BlockSpec((1, chains, 2, 2 * qb, qb), lambda p, i: (p, i, 0, 0, 0)),
                  pl.BlockSpec((1, chains, 2 * qb, LANES), lambda p, i: (p, i, 0, 0))] + s_in,
        out_specs=[pl.BlockSpec((tq, LANES), lambda p, i: (i, p)),
                   pl.BlockSpec((s, LANES), lambda p, i: (0, p)),
                   pl.BlockSpec((s, LANES), lambda p, i: (0, p))] + s_out,
        out_shape=[_sds((s, aw), BF16), _sds((s, aw), BF16), _sds((s, aw), BF16)] + s_shapes,
        input_output_aliases=aliases,
        scratch_shapes=[pltpu.VMEM((s, LANES), F32), pltpu.VMEM((s, LANES), F32)] + s_scratch,
        compiler_params=_params(("arbitrary", "arbitrary")),
    )(proj, proj, proj, o, do, tri, *kept, *([] if side is None else side.operands))
    return res[:3], res[3:]


def _hosted(hooks, kind, l, fn, *args, **kw):
    res, side_out = fn(*args, side=hooks.side(kind, l), **kw)
    hooks.done(kind, l, side_out)
    return res


def _layer_fwd(x, mod, gains, conv_w, tri, *, l, hooks, target=None):
    sh1, sc1, gt1, sh2, sc2, gt2 = mod
    g_pre_mix, g_post_mix, g_pre_mlp, g_post_mlp = gains
    d = x.shape[1]
    w = functools.partial(hooks.weight, l)
    h, proj = _hosted(hooks, "in_proj", l, _norm_mod_matmul, x, g_pre_mix, sc1, sh1, w("w_in"), name=f"in_proj_{l}")
    yc = _conv_fwd(proj, conv_w, name=f"conv_fwd_{l}")
    o, kept = _hosted(hooks, "attn_fwd", l, _attn_fwd, proj, tri, d=d, name=f"attn_fwd_{l}")
    ycv, yat, merged, mix, x1 = _mix_out(yc, o, proj, x, w("w_proj_conv"), w("w_proj_attn"), w("w_out"),
                                         g_post_mix, gt1, name=f"mix_out_{l}")
    (h2, a), _ = _norm_mod_matmul(x1, g_pre_mlp, sc2, sh2, w("w_mlp_in"), name=f"mlp_in_{l}")
    if target is None:
        ff, x2 = _mlp_out(a, x1, w("w_mlp_out"), g_post_mlp, gt2, name=f"mlp_out_{l}")
    else:
        ff, *x2 = _mlp_out_loss(a, x1, w("w_mlp_out"), g_post_mlp, gt2, target, name=f"mlp_out_{l}")
    saved = dict(x=x, h=h, proj=proj, yc=yc, o=o, kept=kept, ycv=ycv, yat=yat, merged=merged, mix=mix, x1=x1, h2=h2, a=a, ff=ff,
                 conv_w=conv_w, **{k: w(k) for k in BIG})
    return x2, saved


def _layer_bwd(dx2, sv, mod, gains, tri, *, l, hooks):
    sh1, sc1, gt1, sh2, sc2, gt2 = mod
    g_pre_mix, g_post_mix, g_pre_mlp, g_post_mlp = gains
    d = dx2.shape[1]
    dff, da, dgt2, dg_post_mlp = _mlp_out_bwd(dx2, sv["ff"], sv["a"], sv["w_mlp_out"], g_post_mlp, gt2,
                                              name=f"mlp_out_bwd_{l}")
    hooks.grad(l, "w_mlp_out", _matmul_tn(sv["a"], [dff], relu2=True, name=f"gw_mlp_out_{l}"))
    (dx1, dsh2, dsc2, dg_pre_mlp), _ = _matmul_nt_norm_bwd([da], sv["w_mlp_in"], sv["x1"], dx2, g_pre_mlp, sc2,
                                                           name=f"mlp_in_bwd_{l}")
    hooks.grad(l, "w_mlp_in", _matmul_tn(sv["h2"], [da], name=f"gw_mlp_in_{l}"))
    dmix, dycv, dyat, dyc, do, dgate, dgt1, dg_post_mix = _mix_out_bwd(
        dx1, sv["mix"], sv["proj"], sv["ycv"], sv["yat"], sv["w_out"], sv["w_proj_conv"], sv["w_proj_attn"],
        g_post_mix, gt1, name=f"mix_out_bwd_{l}")
    hooks.grad(l, "w_out", _matmul_tn(sv["merged"], [dmix], name=f"gw_out_{l}"))
    hooks.grad(l, "w_proj_conv", _matmul_tn(sv["yc"], [dycv], name=f"gw_proj_conv_{l}"))
    hooks.grad(l, "w_proj_attn", _matmul_tn(sv["o"], [dyat], name=f"gw_proj_attn_{l}"))
    dconv, g_conv_w = _conv_bwd(dyc, sv["proj"], sv["conv_w"], name=f"conv_bwd_{l}")
    dq, dk, dv = _hosted(hooks, "attn_bwd", l, _attn_bwd, sv["proj"], sv["o"], sv["kept"], do, tri, d=d,
                         name=f"attn_bwd_{l}")
    dproj = [dconv, dq, dk, dv, dgate]
    n_in = sv["w_in"].shape[1]
    gw_in = _matmul_tn(sv["h"], [dconv], tn=768, n_total=n_in, name=f"gw_in_conv_{l}")
    gw_in = _matmul_tn(sv["h"], [dq, dk, dv], into=gw_in, col0=dconv.shape[1], n_total=n_in, name=f"gw_in_attn_{l}")
    gw_in = _matmul_tn(sv["h"], [dgate], into=gw_in, col0=n_in - dgate.shape[1], n_total=n_in, name=f"gw_in_gate_{l}")
    hooks.grad(l, "w_in", gw_in)
    dx0, dsh1, dsc1, dg_pre_mix = _hosted(hooks, "in_proj_bwd", l, _matmul_nt_norm_bwd, dproj, sv["w_in"], sv["x"], dx1,
                                          g_pre_mix, sc1, name=f"in_proj_bwd_{l}")
    dmod = jnp.concatenate([dsh1, dsc1, dgt1, dsh2, dsc2, dgt2], axis=0)
    dgains = jnp.concatenate([dg_pre_mix, dg_post_mix, dg_pre_mlp, dg_post_mlp], axis=0)
    return dx0, g_conv_w, dmod, dgains


BIG = ("w_in", "w_proj_conv", "w_proj_attn", "w_out", "w_mlp_in", "w_mlp_out")
SHARD_AXIS = dict(w_in=1, w_proj_conv=1, w_proj_attn=1, w_out=0, w_mlp_in=1, w_mlp_out=0)


def _local_step(x, target, mods, gains, conv_w, hooks):
    depth = mods.shape[0]
    tri = _tri(ATTN_BLOCK)
    saved = []
    for l in range(depth):
        mod = [mods[l, k:k + 1] for k in range(N_MOD)]
        gl = [gains[l, k:k + 1] for k in range(4)]
        x, sv = _layer_fwd(x, mod, gl, conv_w[l], tri, l=l, hooks=hooks, target=target if l == depth - 1 else None)
        saved.append((sv, mod, gl))
    dx, loss = x
    dconv, dmods, dgains = [None] * depth, [None] * depth, [None] * depth
    for l in reversed(range(depth)):
        sv, mod, gl = saved[l]
        dx, dconv[l], dmods[l], dgains[l] = _layer_bwd(dx, sv, mod, gl, tri, l=l, hooks=hooks)
    return loss, dx, jnp.stack(dconv), jnp.stack(dmods), jnp.stack(dgains)


def _coords():
    return lax.axis_index("x"), lax.axis_index("y"), lax.axis_index("c")


def _flip(v, f):
    return 1 - v if f else v


def _all_gather_small(v, *, name):
    r, c_ = v.shape

    def body(v_ref, out_ref, send_sems, recv_sems, local_sem):
        x, y, c = _coords()
        me = 4 * x + 2 * y + c
        mine = pltpu.make_async_copy(v_ref, out_ref.at[me], local_sem)
        mine.start()
        copies = []
        for k in range(1, 8):
            fx, fy, fc = (k >> 2) & 1, (k >> 1) & 1, k & 1
            px, py, pc = _flip(x, fx), _flip(y, fy), _flip(c, fc)
            out = pltpu.make_async_remote_copy(src_ref=v_ref, dst_ref=out_ref.at[me], send_sem=send_sems.at[k - 1],
                                               recv_sem=recv_sems.at[k - 1], device_id=(px, py, pc), device_id_type=MESH)
            out.start()
            back = pltpu.make_async_remote_copy(src_ref=v_ref, dst_ref=out_ref.at[4 * px + 2 * py + pc],
                                                send_sem=send_sems.at[k - 1], recv_sem=recv_sems.at[k - 1],
                                                device_id=(px, py, pc), device_id_type=MESH)
            copies.append((out, back))
        for out, back in copies:
            back.wait_recv()
        for out, back in copies:
            out.wait_send()
        mine.wait()

    return pl.pallas_call(
        body, name=name,
        in_specs=[pl.BlockSpec(memory_space=pltpu.VMEM)],
        out_specs=pl.BlockSpec(memory_space=pltpu.VMEM),
        out_shape=_sds((8, r, c_), F32),
        scratch_shapes=[pltpu.SemaphoreType.DMA((7,)), pltpu.SemaphoreType.DMA((7,)), pltpu.SemaphoreType.DMA],
    )(v)


def _shard_dims(full_shape, axis):
    k, n = full_shape
    return (k // 4, n) if axis == 0 else (k, n // 4)


def _shard_window(ref, axis, chip, half, rows, cols):
    r0, rn = (0, rows) if half is None else (half * (rows // 2), rows // 2)
    if axis == 1:
        return ref.at[pl.ds(r0, rn), pl.ds(chip * cols, cols)]
    return ref.at[pl.ds(chip * rows + r0, rn), :]


def _cast_place(w, layer, axis, chip_arr, *, name, tr=512):
    _, rows, cols = w.shape
    tr = _fit(tr, rows)
    nb = rows // tr
    full = (rows * 4, cols) if axis == 0 else (rows, cols * 4)

    def body(chip_ref, w_ref, o_ref):
        o_ref[...] = w_ref[0].astype(BF16)

    if axis == 1:
        out_map = lambda i, chip: (i, chip[0])
    else:
        out_map = lambda i, chip: (chip[0] * nb + i, 0)
    grid_spec = pltpu.PrefetchScalarGridSpec(
        num_scalar_prefetch=1, grid=(nb,),
        in_specs=[pl.BlockSpec((1, tr, cols), lambda i, chip: (layer, i, 0))],
        out_specs=pl.BlockSpec((tr, cols), out_map))
    return pl.pallas_call(body, name=name, grid_spec=grid_spec, out_shape=_sds(full, BF16),
                          compiler_params=_params(("arbitrary",)))(chip_arr, w)


def _gather_side(fulls, axes):
    n = len(fulls)

    def copies(outs, sems):
        send_sems, recv_sems = sems
        x, y, c = _coords()
        chip = 2 * x + y
        sibling = (x, y, 1 - c)
        table = []
        for w in range(n):
            rows, cols = _shard_dims(outs[w].shape, axes[w])
            win = functools.partial(_shard_window, outs[w], axes[w], rows=rows, cols=cols)
            for j, (fx, fy) in enumerate(OTHER_CHIPS):
                px, py = _flip(x, fx), _flip(y, fy)
                pchip = 2 * px + py

                def copy(piece, sem, to):
                    return pltpu.make_async_remote_copy(src_ref=piece, dst_ref=piece, send_sem=send_sems.at[w, sem],
                                                        recv_sem=recv_sems.at[w, sem], device_id=to, device_id_type=MESH)

                table.append((copy(win(chip, c), j, (px, py, c)), copy(win(pchip, c), j, (px, py, c)),
                              copy(win(pchip, c), 3 + j, sibling), copy(win(pchip, 1 - c), 3 + j, sibling)))
        return table

    def start(ins, outs, sems):
        for send, _, _, _ in copies(outs, sems):
            send.start()

    def mid(ins, outs, sems):
        for _, landed, pass_on, _ in copies(outs, sems):
            landed.wait_recv()
            pass_on.start()

    def finish(ins, outs, sems):
        table = copies(outs, sems)
        for _, _, _, from_sibling in table:
            from_sibling.wait_recv()
        for send, _, pass_on, _ in table:
            send.wait_send()
            pass_on.wait_send()

    return _Side(fulls, [_sds(f.shape, f.dtype) for f in fulls], {w: w for w in range(n)},
                 [pltpu.SemaphoreType.DMA((n, 6)), pltpu.SemaphoreType.DMA((n, 6))], start, mid, finish)


def _exchange_side(grads, axes):
    n = len(grads)
    out_shapes = []
    for g, ax in zip(grads, axes):
        rows, cols = _shard_dims(g.shape, ax)
        out_shapes.append(_sds((7, rows // 2, cols), g.dtype))

    def copies(ins, outs, sems):
        send_sems, recv_sems = sems
        x, y, c = _coords()
        table = []
        for w in range(n):
            rows, cols = _shard_dims(ins[w].shape, axes[w])
            for k in range(1, 8):
                fx, fy, fc = (k >> 2) & 1, (k >> 1) & 1, k & 1
                px, py, pc = _flip(x, fx), _flip(y, fy), _flip(c, fc)
                piece = _shard_window(ins[w], axes[w], 2 * px + py, pc, rows, cols)
                table.append(pltpu.make_async_remote_copy(
                    src_ref=piece, dst_ref=outs[w].at[k - 1], send_sem=send_sems.at[w, k - 1],
                    recv_sem=recv_sems.at[w, k - 1], device_id=(px, py, pc), device_id_type=MESH))
        return table

    def start(ins, outs, sems):
        for cp in copies(ins, outs, sems):
            cp.start()

    def finish(ins, outs, sems):
        table = copies(ins, outs, sems)
        for cp in table:
            cp.wait_recv()
        for cp in table:
            cp.wait_send()

    return _Side(grads, out_shapes, {}, [pltpu.SemaphoreType.DMA((n, 7)), pltpu.SemaphoreType.DMA((n, 7))],
                 start, None, finish)


def _rs_sum_join(g, got, out_prev, layer, depth, axis, ids, *, name, tr=512):
    _, rows2, cols = got.shape
    tr = _fit(tr, rows2)
    nt = rows2 // tr
    if axis == 1:
        own_map = lambda i, ids_: (ids_[1] * nt + i, ids_[0])
    else:
        own_map = lambda i, ids_: ((ids_[0] * 2 + ids_[1]) * nt + i, 0)

    def body(ids_ref, g_ref, got_ref, *rest):
        out_ref, buf, local_sems, send_sems, recv_sem = rest[-5:]
        i = pl.program_id(0)
        x, y, c = _coords()
        sibling = (x, y, 1 - c)

        def copies(step, slot):
            rows_mine = pl.ds(c * rows2 + step * tr, tr)
            dst = out_ref.at[layer, rows_mine, :]
            keep = pltpu.make_async_copy(buf.at[slot], dst, local_sems.at[slot])
            give = pltpu.make_async_remote_copy(src_ref=buf.at[slot], dst_ref=dst, send_sem=send_sems.at[slot],
                                                recv_sem=recv_sem, device_id=sibling, device_id_type=MESH)
            return keep, give

        def drain(step, slot):
            keep, give = copies(step, slot)
            keep.wait()
            give.wait_send()

        slot = i % 2

        @pl.when(i >= 2)
        def _():
            drain(i - 2, slot)

        acc = g_ref[...].astype(F32)
        for k in range(7):
            acc = acc + got_ref[k].astype(F32)
        buf[slot] = acc
        keep, give = copies(i, slot)
        keep.start()
        give.start()

        @pl.when(i == nt - 1)
        def _():
            if nt >= 2:
                drain(nt - 2, (nt - 2) % 2)
            drain(nt - 1, (nt - 1) % 2)
            theirs = out_ref.at[layer, pl.ds((1 - c) * rows2, rows2), :]
            pltpu.make_async_remote_copy(src_ref=theirs, dst_ref=theirs, send_sem=send_sems.at[0], recv_sem=recv_sem,
                                         device_id=sibling, device_id_type=MESH).wait_recv()

    hbm = pl.BlockSpec(memory_space=pltpu.HBM)
    in_specs = [pl.BlockSpec((tr, cols), own_map), pl.BlockSpec((7, tr, cols), lambda i, ids_: (0, i, 0))]
    operands = [ids, g, got]
    aliases = {}
    if out_prev is not None:
        in_specs.append(hbm)
        operands.append(out_prev)
        aliases = {3: 0}
    grid_spec = pltpu.PrefetchScalarGridSpec(
        num_scalar_prefetch=1, grid=(nt,), in_specs=in_specs, out_specs=hbm,
        scratch_shapes=[pltpu.VMEM((2, tr, cols), F32), pltpu.SemaphoreType.DMA((2,)), pltpu.SemaphoreType.DMA((2,)),
                        pltpu.SemaphoreType.DMA])
    return pl.pallas_call(body, name=name, grid_spec=grid_spec, out_shape=_sds((depth, 2 * rows2, cols), F32),
                          input_output_aliases=aliases, compiler_params=_params(("arbitrary",)))(*operands)


MIX = ("w_proj_conv", "w_proj_attn", "w_out")


class _Schedule:
    def __init__(self, placed, depth, ids):
        self.placed, self.depth, self.ids = placed, depth, ids
        self.full, self.g, self.carried = {}, {}, None
        self.reduced = {k: None for k in BIG}
        first = [(0, "w_in")]
        self._landed(first, _side_call(self._gather(first), name="gather_w_in_0"))

    def _gather(self, keys):
        return _gather_side([self.placed[k] for k in keys], [SHARD_AXIS[k[1]] for k in keys])

    def _landed(self, keys, outs):
        for k, o in zip(keys, outs):
            self.full[k] = o

    def _exchange(self, keys):
        return _exchange_side([self.g[k] for k in keys], [SHARD_AXIS[k[1]] for k in keys])

    def _reduce(self, keys, got):
        for (l, name), pieces in zip(keys, got):
            self.reduced[name] = _rs_sum_join(self.g[(l, name)], pieces, self.reduced[name], l, self.depth,
                                              SHARD_AXIS[name], self.ids, name=f"rs_sum_join_{l}_{name}")

    def weight(self, l, name):
        return self.full[(l, name)]

    def grad(self, l, name, g):
        self.g[(l, name)] = g

    def side(self, kind, l):
        nxt = [(l + 1, "w_in")] if l + 1 < self.depth else []
        if kind == "in_proj":
            keys, make = [(l, k) for k in MIX + ("w_mlp_in",)], self._gather
        elif kind == "attn_fwd":
            keys, make = [(l, "w_mlp_out")] + nxt, self._gather
        elif kind == "attn_bwd":
            keys, make = [(l, k) for k in ("w_mlp_out", "w_mlp_in") + MIX], self._exchange
        else:
            keys, make = [(l, "w_in")], self._exchange
        self.carried = keys
        return make(keys)

    def done(self, kind, l, outs):
        (self._landed if kind in ("in_proj", "attn_fwd") else self._reduce)(self.carried, outs)


def _flat_rows(shape):
    rows = 1
    for s in shape[:-1]:
        rows *= s
    return rows, shape[-1]


def _row_tile(rows, cols, cap_bytes=2 * 1024 * 1024):
    t = rows
    while t * cols * 4 > cap_bytes and t % 16 == 0:
        t //= 2
    return t


def _ada_fwd(c_all, w_ada, b_loc, *, name, tn=512):
    l, d, nl = w_ada.shape
    b = c_all.shape[0]
    tn = min(tn, nl)

    def body(c_ref, w_ref, b_ref, o_ref):
        o_ref[0] = jnp.dot(c_ref[...], w_ref[0], preferred_element_type=F32,
                           precision=lax.Precision.HIGHEST) + b_ref[0]

    return pl.pallas_call(
        body, name=name, grid=(l, nl // tn),
        in_specs=[pl.BlockSpec((b, d), lambda i, j: (0, 0)), pl.BlockSpec((1, d, tn), lambda i, j: (i, 0, j)),
                  pl.BlockSpec((1, 1, tn), lambda i, j: (i, 0, j))],
        out_specs=pl.BlockSpec((1, b, tn), lambda i, j: (i, 0, j)),
        out_shape=_sds((l, b, nl), F32),
        compiler_params=_params(("parallel", "parallel")),
    )(c_all, w_ada, b_loc)


def _ada_bwd(c_t, dmod_loc, *, name, tn=512):
    d, b = c_t.shape
    l, _, nl = dmod_loc.shape
    tn = min(tn, nl)

    def body(c_ref, dm_ref, o_ref):
        cv = c_ref[...]
        dm = dm_ref[0]
        acc = cv[:, 0:1] * dm[0:1, :]
        for k in range(1, b):
            acc = acc + cv[:, k:k + 1] * dm[k:k + 1, :]
        o_ref[0] = acc

    return pl.pallas_call(
        body, name=name, grid=(l, nl // tn),
        in_specs=[pl.BlockSpec((d, b), lambda i, j: (0, 0)), pl.BlockSpec((1, b, tn), lambda i, j: (i, 0, j))],
        out_specs=pl.BlockSpec((1, d, tn), lambda i, j: (i, 0, j)),
        out_shape=_sds((l, d, nl), F32),
        compiler_params=_params(("parallel", "parallel")),
    )(c_t, dmod_loc)


def _sum_devices(p, *, name):
    k, r, c_ = p.shape

    def body(p_ref, o_ref):
        acc = p_ref[0]
        for j in range(1, k):
            acc = acc + p_ref[j]
        o_ref[...] = acc

    return pl.pallas_call(body, name=name, out_shape=_sds((r, c_), F32),
                          in_specs=[pl.BlockSpec(memory_space=pltpu.VMEM)],
                          out_specs=pl.BlockSpec(memory_space=pltpu.VMEM))(p)


def _adamw(w, g, m, v, *, name):
    shape = w.shape
    rows, cols = _flat_rows(shape)
    tr = _row_tile(rows, cols, cap_bytes=1024 * 1024)
    c1 = 1.0 / (1.0 - ADAM_B1 ** ADAM_STEP)
    c2 = 1.0 / (1.0 - ADAM_B2 ** ADAM_STEP)

    def body(w_ref, g_ref, m_ref, v_ref, go_ref, d_ref, nm_ref, nv_ref):
        gv = g_ref[...]
        nm = ADAM_B1 * m_ref[...] + (1.0 - ADAM_B1) * gv
        nv = ADAM_B2 * v_ref[...] + (1.0 - ADAM_B2) * (gv * gv)
        m_hat = nm * c1
        v_hat = nv * c2
        go_ref[...] = gv
        d_ref[...] = -ADAM_LR * (m_hat / (jnp.sqrt(v_hat) + ADAM_EPS) + ADAM_WD * w_ref[...])
        nm_ref[...] = nm
        nv_ref[...] = nv

    spec = pl.BlockSpec((tr, cols), lambda i: (i, 0))
    flat = lambda a: a.reshape(rows, cols)
    outs = pl.pallas_call(body, name=name, grid=(rows // tr,), in_specs=[spec] * 4, out_specs=[spec] * 4,
                          out_shape=[_sds((rows, cols), F32)] * 4, compiler_params=_params(("parallel",)),
                          )(flat(w), flat(g), flat(m), flat(v))
    return tuple(o.reshape(shape) for o in outs)


WEIGHTS = ("w_ada", "b_ada", "g_pre_mix", "g_post_mix", "g_pre_mlp", "g_post_mlp", "w_in", "conv_w",
           "w_proj_conv", "w_proj_attn", "w_out", "w_mlp_in", "w_mlp_out")
GAINS = ("g_pre_mix", "g_post_mix", "g_pre_mlp", "g_post_mlp")


def kernel(x, c, w_ada, b_ada, g_pre_mix, g_post_mix, g_pre_mlp, g_post_mlp, w_in, conv_w, w_proj_conv, w_proj_attn, w_out, w_mlp_in, w_mlp_out, loss_target, m_w_ada, m_b_ada, m_g_pre_mix, m_g_post_mix, m_g_pre_mlp, m_g_post_mlp, m_w_in, m_conv_w, m_w_proj_conv, m_w_proj_attn, m_w_out, m_w_mlp_in, m_w_mlp_out, v_w_ada, v_b_ada, v_g_pre_mix, v_g_post_mix, v_g_pre_mlp, v_g_post_mlp, v_w_in, v_conv_w, v_w_proj_conv, v_w_proj_attn, v_w_out, v_w_mlp_in, v_w_mlp_out):
    params = dict(w_ada=w_ada, b_ada=b_ada, g_pre_mix=g_pre_mix, g_post_mix=g_post_mix, g_pre_mlp=g_pre_mlp,
                  g_post_mlp=g_post_mlp, w_in=w_in, conv_w=conv_w, w_proj_conv=w_proj_conv, w_proj_attn=w_proj_attn,
                  w_out=w_out, w_mlp_in=w_mlp_in, w_mlp_out=w_mlp_out)
    m_in = dict(w_ada=m_w_ada, b_ada=m_b_ada, g_pre_mix=m_g_pre_mix, g_post_mix=m_g_post_mix, g_pre_mlp=m_g_pre_mlp,
                g_post_mlp=m_g_post_mlp, w_in=m_w_in, conv_w=m_conv_w, w_proj_conv=m_w_proj_conv,
                w_proj_attn=m_w_proj_attn, w_out=m_w_out, w_mlp_in=m_w_mlp_in, w_mlp_out=m_w_mlp_out)
    v_in = dict(w_ada=v_w_ada, b_ada=v_b_ada, g_pre_mix=v_g_pre_mix, g_post_mix=v_g_post_mix, g_pre_mlp=v_g_pre_mlp,
                g_post_mlp=v_g_post_mlp, w_in=v_w_in, conv_w=v_conv_w, w_proj_conv=v_w_proj_conv,
                w_proj_attn=v_w_proj_attn, w_out=v_w_out, w_mlp_in=v_w_mlp_in, w_mlp_out=v_w_mlp_out)

    depth, d, nl_ada = w_ada.shape
    ix, iy, ic = lax.axis_index("x"), lax.axis_index("y"), lax.axis_index("c")
    chip = 2 * ix + iy
    me = 4 * ix + 2 * iy + ic
    xs = x[0]
    target = loss_target[0]

    c_all = _all_gather_small(jnp.broadcast_to(c, (8, d)), name="gather_c")[:, 0, :]
    b_loc = lax.dynamic_slice_in_dim(b_ada, chip * nl_ada, nl_ada, axis=1)[:, None, :]
    mod_loc = _ada_fwd(c_all, w_ada, b_loc, name="ada_fwd")
    mod_all = _all_gather_small(mod_loc.reshape(depth * 8, nl_ada), name="gather_mod")
    mod_all = mod_all.reshape(4, 2, depth, 8, nl_ada)[:, 0]
    mod_me = lax.dynamic_index_in_dim(mod_all, me, axis=2, keepdims=False)
    mods = jnp.transpose(mod_me, (1, 0, 2)).reshape(depth, N_MOD, d)

    chip_arr = jnp.reshape(chip, (1,)).astype(jnp.int32)
    ids = jnp.stack([chip, ic]).astype(jnp.int32)
    placed = {(l, k): _cast_place(params[k], l, SHARD_AXIS[k], chip_arr, name=f"place_{k}_{l}")
              for l in range(depth) for k in BIG}
    conv_full = _all_gather_small(
        jnp.pad(conv_w.reshape(depth * 3, -1), ((0, 8 - depth * 3), (0, 0))), name="gather_conv_w")
    conv_full = conv_full.reshape(4, 2, 8, -1)[:, 0, :depth * 3]
    conv_full = jnp.transpose(conv_full, (1, 0, 2)).reshape(depth, 3, -1)

    gains = jnp.stack([params[k] for k in GAINS], axis=1)
    schedule = _Schedule(placed, depth, ids)
    loss, dx, conv_grads, dmods, dgains = _local_step(xs, target, mods, gains, conv_full, schedule)

    cw = conv_full.shape[2]
    rows = [dmods.reshape(depth * N_MOD, d), dgains.reshape(depth * 4, d),
            conv_grads.reshape(-1, d), jnp.broadcast_to(loss, (1, d))]
    payload = jnp.concatenate(rows, axis=0)
    n_rows = payload.shape[0]
    pad = (-n_rows) % 8
    payload = jnp.pad(payload, ((0, pad), (0, 0)))
    everyone = _all_gather_small(payload, name="gather_small_grads")
    total = _sum_devices(everyone, name="sum_small_grads")
    r0 = depth * N_MOD
    grads = {}
    grads["b_ada"] = total[:r0].reshape(depth, N_MOD * d)
    gsum = total[r0:r0 + depth * 4].reshape(depth, 4, d)
    for k, name in enumerate(GAINS):
        grads[name] = gsum[:, k]
    r1 = r0 + depth * 4
    n_conv = (depth * 3 * cw) // d
    conv_g = total[r1:r1 + n_conv].reshape(depth, 3, cw)
    grads["conv_w"] = lax.dynamic_slice_in_dim(conv_g, chip * (cw // 4), cw // 4, axis=2)
    loss_out = total[r1 + n_conv, 0]
    dmod_all = everyone[:, :r0].reshape(8, depth, N_MOD * d)
    dmod_loc = lax.dynamic_slice_in_dim(dmod_all, chip * nl_ada, nl_ada, axis=2)
    grads["w_ada"] = _ada_bwd(c_all.T, jnp.transpose(dmod_loc, (1, 0, 2)), name="ada_bwd")

    for k in BIG:
        grads[k] = schedule.reduced[k]

    deltas, new_m, new_v = {}, {}, {}
    for k in WEIGHTS:
        grads[k], deltas[k], new_m[k], new_v[k] = _adamw(params[k], grads[k], m_in[k], v_in[k], name=f"adamw_{k}")

    return (loss_out, dx[None], *[grads[k] for k in WEIGHTS], *[deltas[k] for k in WEIGHTS],
            *[new_m[k] for k in WEIGHTS], *[new_v[k] for k in WEIGHTS])
```

```python
import functools

import jax
import jax.numpy as jnp
from jax import lax
from jax.experimental import pallas as pl
from jax.experimental.pallas import tpu as pltpu

F32 = jnp.float32
BF16 = jnp.bfloat16
EPS = 1e-6
N_MOD = 6
HEAD_DIM = 64
LANES = 128
ATTN_SCALE = 1.0 / 8.0
UNDERFLOW_LOG = -90.0
ATTN_BLOCK = 256
ATTN_CHAINS = (4, 4)
ADAM_LR = 0.001
ADAM_B1 = 0.9
ADAM_B2 = 0.999
ADAM_EPS = 1e-08
ADAM_WD = 0.01
ADAM_STEP = 10
VMEM_LIMIT = 56 * 1024 * 1024
MESH = pl.DeviceIdType.MESH
OTHER_CHIPS = ((1, 0), (0, 1), (1, 1))

_NT = (((1,), (1,)), ((), ()))
_TN = (((0,), (0,)), ((), ()))


def _sds(shape, dtype):
    return jax.ShapeDtypeStruct(shape, dtype)


def _params(sem):
    return pltpu.CompilerParams(dimension_semantics=sem, vmem_limit_bytes=VMEM_LIMIT)


def _fit(t, n):
    t = min(t, n)
    while n % t:
        t //= 2
    return t


def _vec_spec(d, nargs=1):
    assert nargs == 1
    return pl.BlockSpec((1, d), lambda i: (0, 0))


def _log_one_minus_sigmoid(z):
    return -jnp.log(1.0 + jnp.exp(-jnp.abs(z))) - jnp.maximum(z, 0.0)


def _sigmoid(z):
    return 0.5 * jnp.tanh(0.5 * z) + 0.5


def _split_bf16(a):
    hi = a.astype(BF16)
    lo = (a - hi.astype(F32)).astype(BF16)
    return hi, lo


def _rms_bwd(dn, xin, g):
    r = lax.rsqrt(jnp.mean(xin * xin, axis=-1, keepdims=True) + EPS)
    xh = xin * r
    dxh = dn * g
    dxin = r * (dxh - xh * jnp.mean(dxh * xh, axis=-1, keepdims=True))
    return dxin, xh


def _colsum(a):
    return jnp.sum(a, axis=0, keepdims=True)


def _norm_mod_matmul(x, g, sc, sh, w, *, name, tm=512, side=None):
    s, d = x.shape
    n = w.shape[1]
    tm = _fit(tm, s)
    nt = s // tm

    def body(*refs):
        i = pl.program_id(0)
        (x_ref, g_ref, sc_ref, sh_ref, w_ref, h_ref, o_ref), late_phases = _host_side(
            side, 5, 2, 0, refs, i == 0, i == (3 * nt) // 4, i == nt - 1)
        xv = x_ref[...]
        r = lax.rsqrt(jnp.mean(xv * xv, axis=-1, keepdims=True) + EPS)
        h = ((xv * r * g_ref[...]) * (1.0 + sc_ref[...]) + sh_ref[...]).astype(BF16)
        h_ref[...] = h
        o_ref[...] = jnp.dot(h, w_ref[...], preferred_element_type=F32).astype(BF16)
        late_phases()

    s_in, s_out, s_shapes, aliases, s_scratch = _side_specs(side, 5, 2)
    res = pl.pallas_call(
        body, name=name, grid=(nt,),
        in_specs=[pl.BlockSpec((tm, d), lambda i: (i, 0)), _vec_spec(d, 1), _vec_spec(d, 1), _vec_spec(d, 1),
                  pl.BlockSpec((d, n), lambda i: (0, 0))] + s_in,
        out_specs=[pl.BlockSpec((tm, d), lambda i: (i, 0)), pl.BlockSpec((tm, n), lambda i: (i, 0))] + s_out,
        out_shape=[_sds((s, d), BF16), _sds((s, n), BF16)] + s_shapes,
        input_output_aliases=aliases, scratch_shapes=s_scratch,
        compiler_params=_params(("arbitrary",)),
    )(x, g, sc, sh, w, *([] if side is None else side.operands))
    return res[:2], res[2:]


HALO = 16


def _shifted_down(a, before):
    row = lax.broadcasted_iota(jnp.int32, a.shape, 0)
    last, last2 = before[-1:, :], before[-2:-1, :]
    one = jnp.where(row == 0, last, pltpu.roll(a, 1, axis=0))
    two = jnp.where(row == 0, last2, jnp.where(row == 1, last, pltpu.roll(a, 2, axis=0)))
    return one, two


def _shifted_up(a, after):
    n = a.shape[0]
    row = lax.broadcasted_iota(jnp.int32, a.shape, 0)
    first, second = after[0:1, :], after[1:2, :]
    one = jnp.where(row == n - 1, first, pltpu.roll(a, n - 1, axis=0))
    two = jnp.where(row == n - 1, second, jnp.where(row == n - 2, first, pltpu.roll(a, n - 2, axis=0)))
    return one, two


def _conv_fwd(proj, conv_w, *, name, tm=512):
    s = proj.shape[0]
    cw = conv_w.shape[1]
    tm = min(tm, s)
    nb = tm // HALO

    def body(bg_ref, cg_ref, u_ref, cgh_ref, uh_ref, w_ref, yc_ref):
        i = pl.program_id(0)
        vv = cg_ref[...].astype(F32) * u_ref[...].astype(F32)
        halo = jnp.where(i > 0, cgh_ref[...].astype(F32) * uh_ref[...].astype(F32), 0.0)
        v1, v2 = _shifted_down(vv, halo)
        w = w_ref[...]
        y = w[2:3, :] * vv + w[1:2, :] * v1 + w[0:1, :] * v2
        yc_ref[...] = (bg_ref[...].astype(F32) * y).astype(BF16)

    def prev(i):
        return jnp.maximum(i * nb - 1, 0)

    return pl.pallas_call(
        body, name=name, grid=(s // tm,),
        in_specs=[pl.BlockSpec((tm, cw), lambda i: (i, 0)), pl.BlockSpec((tm, cw), lambda i: (i, 1)),
                  pl.BlockSpec((tm, cw), lambda i: (i, 2)),
                  pl.BlockSpec((HALO, cw), lambda i: (prev(i), 1)), pl.BlockSpec((HALO, cw), lambda i: (prev(i), 2)),
                  pl.BlockSpec((3, cw), lambda i: (0, 0))],
        out_specs=pl.BlockSpec((tm, cw), lambda i: (i, 0)),
        out_shape=_sds((s, cw), BF16),
        compiler_params=_params(("arbitrary",)),
    )(proj, proj, proj, proj, proj, conv_w)


def _tri(qb):
    r = lax.broadcasted_iota(jnp.int32, (qb, qb), 0)
    c = lax.broadcasted_iota(jnp.int32, (qb, qb), 1)
    return (r >= c).astype(BF16)


def _head_mask(h):
    lane = lax.broadcasted_iota(jnp.int32, (1, LANES), 1)
    return (lane >= HEAD_DIM * h) & (lane < HEAD_DIM * (h + 1))


def _stack_heads(a, masks):
    return jnp.concatenate([jnp.where(m, a, 0).astype(BF16) for m in masks], axis=0)


def _heads_to_lanes(a, qb):
    return jnp.concatenate([a[:qb], a[qb:]], axis=1)


def _stacked_causal(qb, width, first_key, first_query):
    row = lax.broadcasted_iota(jnp.int32, (2 * qb, width), 0)
    col = lax.broadcasted_iota(jnp.int32, (2 * qb, width), 1)
    return first_key + col < first_query + jnp.where(row >= qb, row - qb, row)


def _running_sum(a, tri_m):
    rows, qb = a.shape[0], tri_m.shape[0]
    n = a.shape[1] // qb
    hi, lo = _split_bf16(a)
    stacked = jnp.concatenate([p[:, s * qb:(s + 1) * qb] for s in range(n) for p in (hi, lo)], axis=0)
    both = jnp.dot(stacked, tri_m, preferred_element_type=F32)
    parts = [both[(2 * s) * rows:(2 * s + 1) * rows] + both[(2 * s + 1) * rows:(2 * s + 2) * rows] for s in range(n)]
    later = None
    for s in reversed(range(n)):
        if later is not None:
            parts[s] = parts[s] + later
        later = parts[s][:, 0:1]
    return (parts[0] if n == 1 else jnp.concatenate(parts, axis=1)), later


def _attn_cols(d):
    cw = d // 2
    hp = (d // 2) // LANES
    q0 = (3 * cw) // LANES
    return q0, q0 + hp, q0 + 2 * hp, hp


class _Side:
    def __init__(self, operands, out_shapes, aliases, scratch, start, mid, finish):
        self.operands, self.out_shapes, self.aliases, self.scratch = list(operands), list(out_shapes), aliases, list(scratch)
        self.start, self.mid, self.finish = start, mid, finish


def _side_call(side, *, name):
    n_in, n_out = len(side.operands), len(side.out_shapes)

    def body(*refs):
        parts = refs[:n_in], refs[n_in:n_in + n_out], refs[n_in + n_out:]
        side.start(*parts)
        if side.mid is not None:
            side.mid(*parts)
        side.finish(*parts)

    hbm = pl.BlockSpec(memory_space=pltpu.HBM)
    return pl.pallas_call(body, name=name, in_specs=[hbm] * n_in, out_specs=[hbm] * n_out, out_shape=side.out_shapes,
                          input_output_aliases=dict(side.aliases), scratch_shapes=side.scratch)(*side.operands)


def _host_side(side, n_in, n_out, n_scratch, refs, first, late, last):
    if side is None:
        return refs, lambda: None
    s_in, s_out = len(side.operands), len(side.out_shapes)
    ins = refs[:n_in]
    side_in = refs[n_in:n_in + s_in]
    outs = refs[n_in + s_in:n_in + s_in + n_out]
    side_out = refs[n_in + s_in + n_out:n_in + s_in + n_out + s_out]
    rest = refs[n_in + s_in + n_out + s_out:]
    scratch, sems = rest[:n_scratch], rest[n_scratch:]
    parts = (side_in, side_out, sems)
    pl.when(first)(lambda: side.start(*parts))

    def run_late_phases():
        if side.mid is not None:
            pl.when(late)(lambda: side.mid(*parts))
        pl.when(last)(lambda: side.finish(*parts))

    return (*ins, *outs, *scratch), run_late_phases


def _side_specs(side, n_in, n_out):
    if side is None:
        return [], [], [], {}, []
    hbm = pl.BlockSpec(memory_space=pltpu.HBM)
    s_in = len(side.operands)
    aliases = {n_in + a: n_out + b for a, b in side.aliases.items()}
    return [hbm] * s_in, [hbm] * len(side.out_shapes), side.out_shapes, aliases, side.scratch


def _attn_fwd(proj, tri, *, d, name, side=None):
    s = proj.shape[0]
    qb = tri.shape[0]
    chains = _fit(ATTN_CHAINS[0], s // qb)
    ng = s // (qb * chains)
    q0, k0, v0, hp = _attn_cols(d)

    def body(*refs):
        p, g = pl.program_id(0), pl.program_id(1)
        (q_ref, k_ref, v_ref, tri_ref, o_ref, a_ref, b_ref, run_ref), late_phases = _host_side(
            side, 4, 4, 0, refs, (p == 0) & (g == 0), (p == hp - 1) & (g == 0), (p == hp - 1) & (g == ng - 1))
        tri_m = tri_ref[...]
        masks = [_head_mask(h) for h in range(2)]

        def first_steps(u):
            i = g * chains + u
            qs = _stack_heads(q_ref[u * qb:(u + 1) * qb, :] * ATTN_SCALE, masks)

            def strip(j, state, causal=None, keep=None, live=None):
                run, acc = state
                rows = pl.ds(pl.multiple_of(j * qb, qb), qb)
                z = lax.dot_general(qs, k_ref[rows, :], _NT, preferred_element_type=F32)
                lg = _log_one_minus_sigmoid(z)
                beta = 1.0 - jnp.exp(lg) if keep is not None else None
                if causal is not None:
                    lg = jnp.where(causal, lg, 0.0)
                cs, total = _running_sum(lg, tri_m)
                a = jnp.exp(z + cs + run)
                if causal is not None:
                    a = jnp.where(causal, a, 0.0)
                    beta = jnp.where(causal, beta, 0.0)
                if live is not None:
                    a = jnp.where(live, a, 0.0)
                    beta = jnp.where(live, beta, 0.0)
                    total = jnp.where(live, total, 0.0)
                ab = a.astype(BF16)
                if keep is not None:
                    a_ref[0, u, keep] = ab
                    b_ref[0, u, keep] = beta.astype(BF16)
                acc = acc + jnp.dot(_heads_to_lanes(ab, qb), _stack_heads(v_ref[rows, :], masks),
                                    preferred_element_type=F32)
                return run + total, acc

            state = strip(i, (jnp.zeros((2 * qb, 1), F32), jnp.zeros((qb, LANES), F32)),
                          causal=_stacked_causal(qb, qb, 0, 0), keep=0)
            state = strip(jnp.maximum(i - 1, 0), state, keep=1, live=i >= 1)
            run_ref[0, u] = jnp.broadcast_to(state[0], (2 * qb, LANES))
            return i, strip, state

        started = [first_steps(u) for u in range(chains)]
        for u, (i, strip, state) in enumerate(started):
            state = lax.while_loop(
                lambda st: (st[0] >= 0) & (jnp.max(st[1]) > UNDERFLOW_LOG),
                lambda st, strip=strip: (st[0] - 1, *strip(st[0], st[1:])),
                (i - 2, *state))
            o_ref[u * qb:(u + 1) * qb, :] = state[2]
        late_phases()

    s_in, s_out, s_shapes, aliases, s_scratch = _side_specs(side, 4, 4)
    tq = qb * chains
    nq = s // qb
    res = pl.pallas_call(
        body, name=name, grid=(hp, ng),
        in_specs=[pl.BlockSpec((tq, LANES), lambda p, i: (i, q0 + p)),
                  pl.BlockSpec((s, LANES), lambda p, i: (0, k0 + p)),
                  pl.BlockSpec((s, LANES), lambda p, i: (0, v0 + p)),
                  pl.BlockSpec((qb, qb), lambda p, i: (0, 0))] + s_in,
        out_specs=[pl.BlockSpec((tq, LANES), lambda p, i: (i, p)),
                   pl.BlockSpec((1, chains, 2, 2 * qb, qb), lambda p, i: (p, i, 0, 0, 0)),
                   pl.BlockSpec((1, chains, 2, 2 * qb, qb), lambda p, i: (p, i, 0, 0, 0)),
                   pl.BlockSpec((1, chains, 2 * qb, LANES), lambda p, i: (p, i, 0, 0))] + s_out,
        out_shape=[_sds((s, hp * LANES), F32), _sds((hp, nq, 2, 2 * qb, qb), BF16), _sds((hp, nq, 2, 2 * qb, qb), BF16),
                   _sds((hp, nq, 2 * qb, LANES), F32)] + s_shapes,
        input_output_aliases=aliases, scratch_shapes=s_scratch,
        compiler_params=_params(("arbitrary", "arbitrary")),
    )(proj, proj, proj, tri, *([] if side is None else side.operands))
    return (res[0], tuple(res[1:4])), res[4:]


def _mix_out(yc, o, proj, x, wpc, wpa, wout, g, gt, *, name, tm=512):
    s, d = x.shape
    cw = yc.shape[1]
    tm = min(tm, s)
    ga_blk = (3 * cw + 3 * (d // 2)) // d

    def body(yc_ref, o_ref, ga_ref, gb_ref, x_ref, wpc_ref, wpa_ref, wout_ref, g_ref, gt_ref,
             ycv_ref, yat_ref, mg_ref, mix_ref, x1_ref):
        y_conv = jnp.dot(yc_ref[...], wpc_ref[...], preferred_element_type=F32)
        y_attn = jnp.dot(o_ref[...].astype(BF16), wpa_ref[...], preferred_element_type=F32)
        merged = (_sigmoid(ga_ref[...].astype(F32)) * y_conv + _sigmoid(gb_ref[...].astype(F32)) * y_attn)
        mg = merged.astype(BF16)
        mix = jnp.dot(mg, wout_ref[...], preferred_element_type=F32)
        r = lax.rsqrt(jnp.mean(mix * mix, axis=-1, keepdims=True) + EPS)
        ycv_ref[...] = y_conv.astype(BF16)
        yat_ref[...] = y_attn.astype(BF16)
        mg_ref[...] = mg
        mix_ref[...] = mix
        x1_ref[...] = x_ref[...] + gt_ref[...] * (mix * r * g_ref[...])

    def rows(w):
        return pl.BlockSpec((tm, w), lambda i: (i, 0))

    def full(a):
        return pl.BlockSpec(a.shape, lambda i: (0, 0))

    return pl.pallas_call(
        body, name=name, grid=(s // tm,),
        in_specs=[rows(cw), rows(d // 2), pl.BlockSpec((tm, d), lambda i: (i, ga_blk)),
                  pl.BlockSpec((tm, d), lambda i: (i, ga_blk + 1)), rows(d),
                  full(wpc), full(wpa), full(wout), _vec_spec(d, 1), _vec_spec(d, 1)],
        out_specs=[rows(d), rows(d), rows(d), rows(d), rows(d)],
        out_shape=[_sds((s, d), BF16), _sds((s, d), BF16), _sds((s, d), BF16), _sds((s, d), F32), _sds((s, d), F32)],
        compiler_params=_params(("parallel",)),
    )(yc, o, proj, proj, x, wpc, wpa, wout, g, gt)


def _relu2(a):
    r = jnp.maximum(a.astype(F32), 0.0)
    return (r * r).astype(BF16)


def _mlp_out(a, x, w2, g, gt, *, name, tm=512):
    s, d = x.shape
    dff = a.shape[1]
    tm = min(tm, s)

    def body(a_ref, x_ref, w_ref, g_ref, gt_ref, ff_ref, x2_ref):
        ff = jnp.dot(_relu2(a_ref[...]), w_ref[...], preferred_element_type=F32)
        r = lax.rsqrt(jnp.mean(ff * ff, axis=-1, keepdims=True) + EPS)
        ff_ref[...] = ff
        x2_ref[...] = x_ref[...] + gt_ref[...] * (ff * r * g_ref[...])

    return pl.pallas_call(
        body, name=name, grid=(s // tm,),
        in_specs=[pl.BlockSpec((tm, dff), lambda i: (i, 0)), pl.BlockSpec((tm, d), lambda i: (i, 0)),
                  pl.BlockSpec((dff, d), lambda i: (0, 0)), _vec_spec(d, 1), _vec_spec(d, 1)],
        out_specs=[pl.BlockSpec((tm, d), lambda i: (i, 0)), pl.BlockSpec((tm, d), lambda i: (i, 0))],
        out_shape=[_sds((s, d), F32), _sds((s, d), F32)],
        compiler_params=_params(("parallel",)),
    )(a, x, w2, g, gt)


def _mlp_out_loss(a, x, w2, g, gt, target, *, name, tm=512):
    s, d = x.shape
    dff = a.shape[1]
    tm = min(tm, s)

    def body(a_ref, x_ref, w_ref, g_ref, gt_ref, t_ref, ff_ref, dy_ref, loss_ref):
        @pl.when(pl.program_id(0) == 0)
        def _():
            loss_ref[...] = jnp.zeros_like(loss_ref)
        ff = jnp.dot(_relu2(a_ref[...]), w_ref[...], preferred_element_type=F32)
        r = lax.rsqrt(jnp.mean(ff * ff, axis=-1, keepdims=True) + EPS)
        ff_ref[...] = ff
        e = (x_ref[...] + gt_ref[...] * (ff * r * g_ref[...])) - t_ref[...]
        dy_ref[...] = e * (1.0 / d)
        loss_ref[...] += 0.5 * jnp.sum(jnp.mean(e * e, axis=-1, keepdims=True), axis=0, keepdims=True)

    return pl.pallas_call(
        body, name=name, grid=(s // tm,),
        in_specs=[pl.BlockSpec((tm, dff), lambda i: (i, 0)), pl.BlockSpec((tm, d), lambda i: (i, 0)),
                  pl.BlockSpec((dff, d), lambda i: (0, 0)), _vec_spec(d, 1), _vec_spec(d, 1),
                  pl.BlockSpec((tm, d), lambda i: (i, 0))],
        out_specs=[pl.BlockSpec((tm, d), lambda i: (i, 0)), pl.BlockSpec((tm, d), lambda i: (i, 0)),
                   pl.BlockSpec((1, 1), lambda i: (0, 0))],
        out_shape=[_sds((s, d), F32), _sds((s, d), F32), _sds((1, 1), F32)],
        compiler_params=_params(("arbitrary",)),
    )(a, x, w2, g, gt, target)


def _mlp_out_bwd(dx, ff, a, w2, g, gt, *, name, tm=512):
    s, d = dx.shape
    dff = a.shape[1]
    tm = min(tm, s)

    def body(dx_ref, ff_ref, a_ref, w_ref, g_ref, gt_ref, dff_ref, da_ref, dgt_ref, dg_ref):
        @pl.when(pl.program_id(0) == 0)
        def _():
            dgt_ref[...] = jnp.zeros_like(dgt_ref)
            dg_ref[...] = jnp.zeros_like(dg_ref)
        dxv = dx_ref[...]
        dn = dxv * gt_ref[...]
        dffv, xh = _rms_bwd(dn, ff_ref[...], g_ref[...])
        dgt_ref[...] += _colsum(dxv * (xh * g_ref[...]))
        dg_ref[...] += _colsum(dn * xh)
        dffb = dffv.astype(BF16)
        dff_ref[...] = dffb
        drr = lax.dot_general(dffb, w_ref[...], _NT, preferred_element_type=F32)
        da_ref[...] = (drr * (2.0 * jnp.maximum(a_ref[...].astype(F32), 0.0))).astype(BF16)

    return pl.pallas_call(
        body, name=name, grid=(s // tm,),
        in_specs=[pl.BlockSpec((tm, d), lambda i: (i, 0)), pl.BlockSpec((tm, d), lambda i: (i, 0)),
                  pl.BlockSpec((tm, dff), lambda i: (i, 0)), pl.BlockSpec((dff, d), lambda i: (0, 0)),
                  _vec_spec(d, 1), _vec_spec(d, 1)],
        out_specs=[pl.BlockSpec((tm, d), lambda i: (i, 0)), pl.BlockSpec((tm, dff), lambda i: (i, 0)),
                   _vec_spec(d, 1), _vec_spec(d, 1)],
        out_shape=[_sds((s, d), BF16), _sds((s, dff), BF16), _sds((1, d), F32), _sds((1, d), F32)],
        compiler_params=_params(("arbitrary",)),
    )(dx, ff, a, w2, g, gt)


def _matmul_nt_norm_bwd(dys, w, x, dres, g, sc, *, name, tm=512, side=None):
    s = dys[0].shape[0]
    widths = [dy.shape[1] for dy in dys]
    d, n = w.shape
    assert sum(widths) == n, (widths, n)
    tm = _fit(tm, s)
    nt = s // tm
    np_ = len(dys)

    def body(*refs):
        i = pl.program_id(0)
        own, late_phases = _host_side(side, np_ + 5, 4, 0, refs, i == 0, i == (3 * nt) // 4, i == nt - 1)
        dy_refs = own[:np_]
        w_ref, x_ref, dres_ref, g_ref, sc_ref, dx_ref, dsh_ref, dsc_ref, dg_ref = own[np_:]

        @pl.when(i == 0)
        def _():
            dsh_ref[...] = jnp.zeros_like(dsh_ref)
            dsc_ref[...] = jnp.zeros_like(dsc_ref)
            dg_ref[...] = jnp.zeros_like(dg_ref)

        dh = None
        for p, dy_ref in enumerate(dy_refs):
            cols = slice(sum(widths[:p]), sum(widths[:p + 1]))
            part = lax.dot_general(dy_ref[...], w_ref[:, cols], _NT, preferred_element_type=F32)
            dh = part if dh is None else dh + part
        dn = dh * (1.0 + sc_ref[...])
        dxin, xh = _rms_bwd(dn, x_ref[...], g_ref[...])
        dsh_ref[...] += _colsum(dh)
        dsc_ref[...] += _colsum(dh * (xh * g_ref[...]))
        dg_ref[...] += _colsum(dn * xh)
        dx_ref[...] = dres_ref[...] + dxin
        late_phases()

    s_in, s_out, s_shapes, aliases, s_scratch = _side_specs(side, np_ + 5, 4)
    res = pl.pallas_call(
        body, name=name, grid=(nt,),
        in_specs=[pl.BlockSpec((tm, wd), lambda i: (i, 0)) for wd in widths]
        + [pl.BlockSpec((d, n), lambda i: (0, 0)),
           pl.BlockSpec((tm, d), lambda i: (i, 0)), pl.BlockSpec((tm, d), lambda i: (i, 0)),
           _vec_spec(d, 1), _vec_spec(d, 1)] + s_in,
        out_specs=[pl.BlockSpec((tm, d), lambda i: (i, 0)), _vec_spec(d, 1), _vec_spec(d, 1), _vec_spec(d, 1)] + s_out,
        out_shape=[_sds((s, d), F32), _sds((1, d), F32), _sds((1, d), F32), _sds((1, d), F32)] + s_shapes,
        input_output_aliases=aliases, scratch_shapes=s_scratch,
        compiler_params=_params(("arbitrary",)),
    )(*dys, w, x, dres, g, sc, *([] if side is None else side.operands))
    return res[:4], res[4:]


def _matmul_tn(a, bs, *, name, tk=1024, tn=1024, ts=2048, relu2=False, into=None, col0=0, n_total=None):
    s, k = a.shape
    widths = [b.shape[1] for b in bs]
    n = sum(widths)
    tk, ts = _fit(tk, k), _fit(ts, s)
    for w in widths:
        tn = _fit(tn, w)
    while col0 % tn:
        tn //= 2
    nt = s // ts
    assert tn % LANES == 0 and all(sum(widths[:p]) % tn == 0 for p in range(len(bs))), (widths, tn)
    first = [sum(widths[:p]) // tn for p in range(len(bs))]
    tiles = [w // tn for w in widths]
    tile0 = col0 // tn

    def body(a_ref, *rest):
        b_refs, o_ref, acc = rest[:len(bs)], rest[-2], rest[-1]
        j, t = pl.program_id(1), pl.program_id(2)

        @pl.when(t == 0)
        def _():
            acc[...] = jnp.zeros_like(acc)
        av = a_ref[...]
        av = _relu2(av) if relu2 else av.astype(BF16)
        for p, b_ref in enumerate(b_refs):
            def add(b_ref=b_ref):
                acc[...] += lax.dot_general(av, b_ref[...], _TN, preferred_element_type=F32)
            if len(bs) == 1:
                add()
            else:
                pl.when((j >= first[p]) & (j < first[p] + tiles[p]))(add)

        @pl.when(t == nt - 1)
        def _():
            o_ref[...] = acc[...].astype(BF16)

    def piece_spec(p):
        def index(i, j, t):
            mine = (j >= first[p]) & (j < first[p] + tiles[p])
            return jnp.where(mine, t, 0), jnp.where(mine, j - first[p], 0)
        return pl.BlockSpec((ts, tn), index)

    operands, extra_specs, aliases = [a, *bs], [], {}
    if into is not None:
        operands.append(into)
        extra_specs = [pl.BlockSpec(memory_space=pltpu.HBM)]
        aliases = {len(operands) - 1: 0}
    return pl.pallas_call(
        body, name=name, grid=(k // tk, n // tn, nt),
        in_specs=[pl.BlockSpec((ts, tk), lambda i, j, t: (t, i))] + [piece_spec(p) for p in range(len(bs))] + extra_specs,
        out_specs=pl.BlockSpec((tk, tn), lambda i, j, t: (i, tile0 + j)),
        out_shape=_sds((k, n_total or n), BF16),
        input_output_aliases=aliases,
        scratch_shapes=[pltpu.VMEM((tk, tn), F32)],
        compiler_params=_params(("parallel", "parallel", "arbitrary")),
    )(*operands)


def _mix_out_bwd(dx, mix, proj, ycv, yat, wout, wpc, wpa, g, gt, *, name, tm=512):
    s, d = dx.shape
    cw = wpc.shape[0]
    aw = wpa.shape[0]
    tm = min(tm, s)
    ga_blk = (3 * cw + 3 * aw) // d

    def body(dx_ref, mix_ref, ga_ref, gb_ref, ycv_ref, yat_ref, wout_ref, wpc_ref, wpa_ref, g_ref, gt_ref,
             dmix_ref, dycv_ref, dyat_ref, dyc_ref, do_ref, dgate_ref, dgt_ref, dg_ref):
        @pl.when(pl.program_id(0) == 0)
        def _():
            dgt_ref[...] = jnp.zeros_like(dgt_ref)
            dg_ref[...] = jnp.zeros_like(dg_ref)
        dxv = dx_ref[...]
        dn = dxv * gt_ref[...]
        dmix, xh = _rms_bwd(dn, mix_ref[...], g_ref[...])
        dgt_ref[...] += _colsum(dxv * (xh * g_ref[...]))
        dg_ref[...] += _colsum(dn * xh)
        dmixb = dmix.astype(BF16)
        dmix_ref[...] = dmixb
        dmerged = lax.dot_general(dmixb, wout_ref[...], _NT, preferred_element_type=F32)
        sga = _sigmoid(ga_ref[...].astype(F32))
        sgb = _sigmoid(gb_ref[...].astype(F32))
        dycv = (dmerged * sga).astype(BF16)
        dyat = (dmerged * sgb).astype(BF16)
        dycv_ref[...] = dycv
        dyat_ref[...] = dyat
        dgate_ref[:, 0:d] = (dmerged * ycv_ref[...].astype(F32) * (sga * (1.0 - sga))).astype(BF16)
        dgate_ref[:, d:2 * d] = (dmerged * yat_ref[...].astype(F32) * (sgb * (1.0 - sgb))).astype(BF16)
        dyc_ref[...] = lax.dot_general(dycv, wpc_ref[...], _NT, preferred_element_type=F32).astype(BF16)
        do_ref[...] = lax.dot_general(dyat, wpa_ref[...], _NT, preferred_element_type=F32).astype(BF16)

    def rows(w):
        return pl.BlockSpec((tm, w), lambda i: (i, 0))

    def full(a):
        return pl.BlockSpec(a.shape, lambda i: (0, 0))

    return pl.pallas_call(
        body, name=name, grid=(s // tm,),
        in_specs=[rows(d), rows(d), pl.BlockSpec((tm, d), lambda i: (i, ga_blk)),
                  pl.BlockSpec((tm, d), lambda i: (i, ga_blk + 1)), rows(d), rows(d),
                  full(wout), full(wpc), full(wpa), _vec_spec(d, 1), _vec_spec(d, 1)],
        out_specs=[rows(d), rows(d), rows(d), rows(cw), rows(aw), rows(2 * d), _vec_spec(d, 1), _vec_spec(d, 1)],
        out_shape=[_sds((s, d), BF16), _sds((s, d), BF16), _sds((s, d), BF16), _sds((s, cw), BF16),
                   _sds((s, aw), BF16), _sds((s, 2 * d), BF16), _sds((1, d), F32), _sds((1, d), F32)],
        compiler_params=_params(("arbitrary",)),
    )(dx, mix, proj, proj, ycv, yat, wout, wpc, wpa, g, gt)


def _conv_bwd(dyc, proj, conv_w, *, name, tm=512):
    s = proj.shape[0]
    cw = conv_w.shape[1]
    tm = min(tm, s)
    nb = tm // HALO
    nt = s // tm
    last_blk = s // HALO - 1

    def body(dyc_ref, bg_ref, cg_ref, u_ref, cgh_ref, uh_ref, dych_ref, bgh_ref, w_ref,
             dconv_ref, dw_ref):
        i = pl.program_id(0)

        @pl.when(i == 0)
        def _():
            dw_ref[...] = jnp.zeros_like(dw_ref)

        cg = cg_ref[...].astype(F32)
        u = u_ref[...].astype(F32)
        vv = cg * u
        halo = jnp.where(i > 0, cgh_ref[...].astype(F32) * uh_ref[...].astype(F32), 0.0)
        v1, v2 = _shifted_down(vv, halo)
        w = w_ref[...]
        y = w[2:3, :] * vv + w[1:2, :] * v1 + w[0:1, :] * v2
        dyc = dyc_ref[...].astype(F32)
        dconv_ref[:, 0:cw] = (dyc * y).astype(BF16)
        gy = dyc * bg_ref[...].astype(F32)
        nxt = jnp.where(i < nt - 1, dych_ref[...].astype(F32) * bgh_ref[...].astype(F32), 0.0)
        g1, g2 = _shifted_up(gy, nxt)
        dvv = w[2:3, :] * gy + w[1:2, :] * g1 + w[0:1, :] * g2
        dconv_ref[:, cw:2 * cw] = (dvv * u).astype(BF16)
        dconv_ref[:, 2 * cw:3 * cw] = (dvv * cg).astype(BF16)
        dw_ref[0:1, :] += _colsum(gy * v2)
        dw_ref[1:2, :] += _colsum(gy * v1)
        dw_ref[2:3, :] += _colsum(gy * vv)

    def prev(i):
        return jnp.maximum(i * nb - 1, 0)

    def nxt_blk(i):
        return jnp.minimum((i + 1) * nb, last_blk)

    def col(c):
        return pl.BlockSpec((tm, cw), lambda i: (i, c))

    return pl.pallas_call(
        body, name=name, grid=(nt,),
        in_specs=[col(0), col(0), col(1), col(2),
                  pl.BlockSpec((HALO, cw), lambda i: (prev(i), 1)), pl.BlockSpec((HALO, cw), lambda i: (prev(i), 2)),
                  pl.BlockSpec((HALO, cw), lambda i: (nxt_blk(i), 0)), pl.BlockSpec((HALO, cw), lambda i: (nxt_blk(i), 0)),
                  pl.BlockSpec((3, cw), lambda i: (0, 0))],
        out_specs=[pl.BlockSpec((tm, 3 * cw), lambda i: (i, 0)), pl.BlockSpec((3, cw), lambda i: (0, 0))],
        out_shape=[_sds((s, 3 * cw), BF16), _sds((3, cw), F32)],
        compiler_params=_params(("arbitrary",)),
    )(dyc, proj, proj, proj, proj, proj, dyc, proj, conv_w)


def _attn_bwd(proj, o, kept, do, tri, *, d, name, side=None):
    s = proj.shape[0]
    qb = tri.shape[0]
    chains = _fit(ATTN_CHAINS[1], s // qb)
    ng = s // (qb * chains)
    q0, k0, v0, hp = _attn_cols(d)

    def body(*refs):
        p, g = pl.program_id(0), pl.program_id(1)
        own, late_phases = _host_side(
            side, 9, 3, 2, refs, (p == 0) & (g == 0), (p == hp - 1) & (g == 0), (p == hp - 1) & (g == ng - 1))
        (q_ref, k_ref, v_ref, o_ref, do_ref, tri_ref, a_ref, b_ref, run_ref,
         dq_ref, dk_ref, dv_ref, dk_acc, dv_acc) = own

        @pl.when(g == 0)
        def _():
            dk_acc[...] = jnp.zeros_like(dk_acc)
            dv_acc[...] = jnp.zeros_like(dv_acc)

        tri_m = tri_ref[...]
        masks = [_head_mask(h) for h in range(2)]

        def first_steps(u):
            i = g * chains + u
            mine = slice(u * qb, (u + 1) * qb)
            dov = do_ref[mine, :]
            qs = _stack_heads(q_ref[mine, :] * ATTN_SCALE, masks)
            dos = _stack_heads(dov, masks)
            dprod = dov.astype(F32) * o_ref[mine, :]
            dtot = jnp.concatenate([jnp.sum(jnp.where(m, dprod, 0.0), axis=-1, keepdims=True) for m in masks], axis=0)

            def through(j, ab, beta, left, grun, dq_acc):
                rows = pl.ds(pl.multiple_of(j * qb, qb), qb)
                kb = k_ref[rows, :]
                da = lax.dot_general(dos, v_ref[rows, :], _NT, preferred_element_type=F32)
                gg = ab.astype(F32) * da
                gcs, gtotal = _running_sum(gg, tri_m)
                dzb = (gg - beta * (gg + (left - gcs))).astype(BF16)
                dq_acc = dq_acc + jnp.dot(_heads_to_lanes(dzb, qb), _stack_heads(kb, masks),
                                          preferred_element_type=F32)
                dk_add = lax.dot_general(dzb, qs, _TN, preferred_element_type=F32)
                dv_add = lax.dot_general(ab, dos, _TN, preferred_element_type=F32)
                return (grun + gtotal, dq_acc), (rows, dk_add, dv_add)

            def kept(slot, j, state):
                grun, dq_acc = state
                return through(j, a_ref[0, u, slot], b_ref[0, u, slot].astype(F32), dtot - grun, grun, dq_acc)

            def strip(j, state):
                run, grun, dq_acc = state
                z = lax.dot_general(qs, k_ref[pl.ds(pl.multiple_of(j * qb, qb), qb), :], _NT, preferred_element_type=F32)
                lg = _log_one_minus_sigmoid(z)
                cs, total = _running_sum(lg, tri_m)
                ab = jnp.exp(z + cs + run).astype(BF16)
                left = jnp.where(run > UNDERFLOW_LOG, dtot - grun, 0.0)
                (grun, dq_acc), adds = through(j, ab, 1.0 - jnp.exp(lg), left, grun, dq_acc)
                return (run + total, grun, dq_acc), adds

            state, adds0 = kept(0, i, (jnp.zeros((2 * qb, 1), F32), jnp.zeros((qb, LANES), F32)))
            state, adds1 = kept(1, jnp.maximum(i - 1, 0), state)
            return i, strip, (run_ref[0, u][:, 0:1], *state), (adds0, adds1)

        started = [first_steps(u) for u in range(chains)]
        for u, (i, strip, state, adds) in enumerate(started):
            for rows, dk_add, dv_add in adds:
                dk_acc[rows, :] += dk_add
                dv_acc[rows, :] += dv_add

            def more(st, strip=strip):
                state, (rows, dk_add, dv_add) = strip(st[0], st[1:])
                dk_acc[rows, :] += dk_add
                dv_acc[rows, :] += dv_add
                return (st[0] - 1, *state)

            state = lax.while_loop(lambda st: (st[0] >= 0) & (jnp.max(st[1]) > UNDERFLOW_LOG), more, (i - 2, *state))
            dq_ref[u * qb:(u + 1) * qb, :] = (state[3] * ATTN_SCALE).astype(BF16)

        @pl.when(g == ng - 1)
        def _():
            dk_ref[...] = dk_acc[...].astype(BF16)
            dv_ref[...] = dv_acc[...].astype(BF16)

        late_phases()

    aw = hp * LANES
    tq = qb * chains
    s_in, s_out, s_shapes, aliases, s_scratch = _side_specs(side, 9, 3)
    res = pl.pallas_call(
        body, name=name, grid=(hp, ng),
        in_specs=[pl.BlockSpec((tq, LANES), lambda p, i: (i, q0 + p)),
                  pl.BlockSpec((s, LANES), lambda p, i: (0, k0 + p)),
                  pl.BlockSpec((s, LANES), lambda p, i: (0, v0 + p)),
                  pl.BlockSpec((tq, LANES), lambda p, i: (i, p)),
                  pl.BlockSpec((tq, LANES), lambda p, i: (i, p)),
                  pl.BlockSpec((qb, qb), lambda p, i: (0, 0)),
                  pl.BlockSpec((1, chains, 2, 2 * qb, qb), lambda p, i: (p, i, 0, 0, 0)),
                  pl.BlockSpec((1, chains, 2, 2 * qb, qb), lambda p, i: (p, i, 0, 0, 0)),
                  pl.BlockSpec((1, chains, 2 * qb, LANES), lambda p, i: (p, i, 0, 0))] + s_in,
        out_specs=[pl.BlockSpec((tq, LANES), lambda p, i: (i, p)),
                   pl.BlockSpec((s, LANES), lambda p, i: (0, p)),
                   pl.BlockSpec((s, LANES), lambda p, i: (0, p))] + s_out,
        out_shape=[_sds((s, aw), BF16), _sds((s, aw), BF16), _sds((s, aw), BF16)] + s_shapes,
        input_output_aliases=aliases,
        scratch_shapes=[pltpu.VMEM((s, LANES), F32), pltpu.VMEM((s, LANES), F32)] + s_scratch,
        compiler_params=_params(("arbitrary", "arbitrary")),
    )(proj, proj, proj, o, do, tri, *kept, *([] if side is None else side.operands))
    return res[:3], res[3:]


def _hosted(hooks, kind, l, fn, *args, **kw):
    res, side_out = fn(*args, side=hooks.side(kind, l), **kw)
    hooks.done(kind, l, side_out)
    return res


def _layer_fwd(x, mod, gains, conv_w, tri, *, l, hooks, target=None):
    sh1, sc1, gt1, sh2, sc2, gt2 = mod
    g_pre_mix, g_post_mix, g_pre_mlp, g_post_mlp = gains
    d = x.shape[1]
    w = functools.partial(hooks.weight, l)
    h, proj = _hosted(hooks, "in_proj", l, _norm_mod_matmul, x, g_pre_mix, sc1, sh1, w("w_in"), name=f"in_proj_{l}")
    yc = _conv_fwd(proj, conv_w, name=f"conv_fwd_{l}")
    o, kept = _hosted(hooks, "attn_fwd", l, _attn_fwd, proj, tri, d=d, name=f"attn_fwd_{l}")
    ycv, yat, merged, mix, x1 = _mix_out(yc, o, proj, x, w("w_proj_conv"), w("w_proj_attn"), w("w_out"),
                                         g_post_mix, gt1, name=f"mix_out_{l}")
    (h2, a), _ = _norm_mod_matmul(x1, g_pre_mlp, sc2, sh2, w("w_mlp_in"), name=f"mlp_in_{l}")
    if target is None:
        ff, x2 = _mlp_out(a, x1, w("w_mlp_out"), g_post_mlp, gt2, name=f"mlp_out_{l}")
    else:
        ff, *x2 = _mlp_out_loss(a, x1, w("w_mlp_out"), g_post_mlp, gt2, target, name=f"mlp_out_{l}")
    saved = dict(x=x, h=h, proj=proj, yc=yc, o=o, kept=kept, ycv=ycv, yat=yat, merged=merged, mix=mix, x1=x1, h2=h2, a=a, ff=ff,
                 conv_w=conv_w, **{k: w(k) for k in BIG})
    return x2, saved


def _layer_bwd(dx2, sv, mod, gains, tri, *, l, hooks):
    sh1, sc1, gt1, sh2, sc2, gt2 = mod
    g_pre_mix, g_post_mix, g_pre_mlp, g_post_mlp = gains
    d = dx2.shape[1]
    dff, da, dgt2, dg_post_mlp = _mlp_out_bwd(dx2, sv["ff"], sv["a"], sv["w_mlp_out"], g_post_mlp, gt2,
                                              name=f"mlp_out_bwd_{l}")
    hooks.grad(l, "w_mlp_out", _matmul_tn(sv["a"], [dff], relu2=True, name=f"gw_mlp_out_{l}"))
    (dx1, dsh2, dsc2, dg_pre_mlp), _ = _matmul_nt_norm_bwd([da], sv["w_mlp_in"], sv["x1"], dx2, g_pre_mlp, sc2,
                                                           name=f"mlp_in_bwd_{l}")
    hooks.grad(l, "w_mlp_in", _matmul_tn(sv["h2"], [da], name=f"gw_mlp_in_{l}"))
    dmix, dycv, dyat, dyc, do, dgate, dgt1, dg_post_mix = _mix_out_bwd(
        dx1, sv["mix"], sv["proj"], sv["ycv"], sv["yat"], sv["w_out"], sv["w_proj_conv"], sv["w_proj_attn"],
        g_post_mix, gt1, name=f"mix_out_bwd_{l}")
    hooks.grad(l, "w_out", _matmul_tn(sv["merged"], [dmix], name=f"gw_out_{l}"))
    hooks.grad(l, "w_proj_conv", _matmul_tn(sv["yc"], [dycv], name=f"gw_proj_conv_{l}"))
    hooks.grad(l, "w_proj_attn", _matmul_tn(sv["o"], [dyat], name=f"gw_proj_attn_{l}"))
    dconv, g_conv_w = _conv_bwd(dyc, sv["proj"], sv["conv_w"], name=f"conv_bwd_{l}")
    dq, dk, dv = _hosted(hooks, "attn_bwd", l, _attn_bwd, sv["proj"], sv["o"], sv["kept"], do, tri, d=d,
                         name=f"attn_bwd_{l}")
    dproj = [dconv, dq, dk, dv, dgate]
    n_in = sv["w_in"].shape[1]
    gw_in = _matmul_tn(sv["h"], [dconv], tn=768, n_total=n_in, name=f"gw_in_conv_{l}")
    gw_in = _matmul_tn(sv["h"], [dq, dk, dv], into=gw_in, col0=dconv.shape[1], n_total=n_in, name=f"gw_in_attn_{l}")
    gw_in = _matmul_tn(sv["h"], [dgate], into=gw_in, col0=n_in - dgate.shape[1], n_total=n_in, name=f"gw_in_gate_{l}")
    hooks.grad(l, "w_in", gw_in)
    dx0, dsh1, dsc1, dg_pre_mix = _hosted(hooks, "in_proj_bwd", l, _matmul_nt_norm_bwd, dproj, sv["w_in"], sv["x"], dx1,
                                          g_pre_mix, sc1, name=f"in_proj_bwd_{l}")
    dmod = jnp.concatenate([dsh1, dsc1, dgt1, dsh2, dsc2, dgt2], axis=0)
    dgains = jnp.concatenate([dg_pre_mix, dg_post_mix, dg_pre_mlp, dg_post_mlp], axis=0)
    return dx0, g_conv_w, dmod, dgains


BIG = ("w_in", "w_proj_conv", "w_proj_attn", "w_out", "w_mlp_in", "w_mlp_out")
SHARD_AXIS = dict(w_in=1, w_proj_conv=1, w_proj_attn=1, w_out=0, w_mlp_in=1, w_mlp_out=0)


def _local_step(x, target, mods, gains, conv_w, hooks):
    depth = mods.shape[0]
    tri = _tri(ATTN_BLOCK)
    saved = []
    for l in range(depth):
        mod = [mods[l, k:k + 1] for k in range(N_MOD)]
        gl = [gains[l, k:k + 1] for k in range(4)]
        x, sv = _layer_fwd(x, mod, gl, conv_w[l], tri, l=l, hooks=hooks, target=target if l == depth - 1 else None)
        saved.append((sv, mod, gl))
    dx, loss = x
    dconv, dmods, dgains = [None] * depth, [None] * depth, [None] * depth
    for l in reversed(range(depth)):
        sv, mod, gl = saved[l]
        dx, dconv[l], dmods[l], dgains[l] = _layer_bwd(dx, sv, mod, gl, tri, l=l, hooks=hooks)
    return loss, dx, jnp.stack(dconv), jnp.stack(dmods), jnp.stack(dgains)


def _coords():
    return lax.axis_index("x"), lax.axis_index("y"), lax.axis_index("c")


def _flip(v, f):
    return 1 - v if f else v


def _all_gather_small(v, *, name):
    r, c_ = v.shape

    def body(v_ref, out_ref, send_sems, recv_sems, local_sem):
        x, y, c = _coords()
        me = 4 * x + 2 * y + c
        mine = pltpu.make_async_copy(v_ref, out_ref.at[me], local_sem)
        mine.start()
        copies = []
        for k in range(1, 8):
            fx, fy, fc = (k >> 2) & 1, (k >> 1) & 1, k & 1
            px, py, pc = _flip(x, fx), _flip(y, fy), _flip(c, fc)
            out = pltpu.make_async_remote_copy(src_ref=v_ref, dst_ref=out_ref.at[me], send_sem=send_sems.at[k - 1],
                                               recv_sem=recv_sems.at[k - 1], device_id=(px, py, pc), device_id_type=MESH)
            out.start()
            back = pltpu.make_async_remote_copy(src_ref=v_ref, dst_ref=out_ref.at[4 * px + 2 * py + pc],
                                                send_sem=send_sems.at[k - 1], recv_sem=recv_sems.at[k - 1],
                                                device_id=(px, py, pc), device_id_type=MESH)
            copies.append((out, back))
        for out, back in copies:
            back.wait_recv()
        for out, back in copies:
            out.wait_send()
        mine.wait()

    return pl.pallas_call(
        body, name=name,
        in_specs=[pl.BlockSpec(memory_space=pltpu.VMEM)],
        out_specs=pl.BlockSpec(memory_space=pltpu.VMEM),
        out_shape=_sds((8, r, c_), F32),
        scratch_shapes=[pltpu.SemaphoreType.DMA((7,)), pltpu.SemaphoreType.DMA((7,)), pltpu.SemaphoreType.DMA],
    )(v)


def _shard_dims(full_shape, axis):
    k, n = full_shape
    return (k // 4, n) if axis == 0 else (k, n // 4)


def _shard_window(ref, axis, chip, half, rows, cols):
    r0, rn = (0, rows) if half is None else (half * (rows // 2), rows // 2)
    if axis == 1:
        return ref.at[pl.ds(r0, rn), pl.ds(chip * cols, cols)]
    return ref.at[pl.ds(chip * rows + r0, rn), :]


def _cast_place(w, layer, axis, chip_arr, *, name, tr=256):
    _, rows, cols = w.shape
    tr = _fit(tr, rows)
    nb = rows // tr
    full = (rows * 4, cols) if axis == 0 else (rows, cols * 4)

    def body(chip_ref, w_ref, o_ref):
        o_ref[...] = w_ref[0].astype(BF16)

    if axis == 1:
        out_map = lambda i, chip: (i, chip[0])
    else:
        out_map = lambda i, chip: (chip[0] * nb + i, 0)
    grid_spec = pltpu.PrefetchScalarGridSpec(
        num_scalar_prefetch=1, grid=(nb,),
        in_specs=[pl.BlockSpec((1, tr, cols), lambda i, chip: (layer, i, 0))],
        out_specs=pl.BlockSpec((tr, cols), out_map))
    return pl.pallas_call(body, name=name, grid_spec=grid_spec, out_shape=_sds(full, BF16),
                          compiler_params=_params(("arbitrary",)))(chip_arr, w)


def _gather_side(fulls, axes):
    n = len(fulls)

    def copies(outs, sems):
        send_sems, recv_sems = sems
        x, y, c = _coords()
        chip = 2 * x + y
        sibling = (x, y, 1 - c)
        table = []
        for w in range(n):
            rows, cols = _shard_dims(outs[w].shape, axes[w])
            win = functools.partial(_shard_window, outs[w], axes[w], rows=rows, cols=cols)
            for j, (fx, fy) in enumerate(OTHER_CHIPS):
                px, py = _flip(x, fx), _flip(y, fy)
                pchip = 2 * px + py

                def copy(piece, sem, to):
                    return pltpu.make_async_remote_copy(src_ref=piece, dst_ref=piece, send_sem=send_sems.at[w, sem],
                                                        recv_sem=recv_sems.at[w, sem], device_id=to, device_id_type=MESH)

                table.append((copy(win(chip, c), j, (px, py, c)), copy(win(pchip, c), j, (px, py, c)),
                              copy(win(pchip, c), 3 + j, sibling), copy(win(pchip, 1 - c), 3 + j, sibling)))
        return table

    def start(ins, outs, sems):
        for send, _, _, _ in copies(outs, sems):
            send.start()

    def mid(ins, outs, sems):
        for _, landed, pass_on, _ in copies(outs, sems):
            landed.wait_recv()
            pass_on.start()

    def finish(ins, outs, sems):
        table = copies(outs, sems)
        for _, _, _, from_sibling in table:
            from_sibling.wait_recv()
        for send, _, pass_on, _ in table:
            send.wait_send()
            pass_on.wait_send()

    return _Side(fulls, [_sds(f.shape, f.dtype) for f in fulls], {w: w for w in range(n)},
                 [pltpu.SemaphoreType.DMA((n, 6)), pltpu.SemaphoreType.DMA((n, 6))], start, mid, finish)


def _exchange_side(grads, axes):
    n = len(grads)
    out_shapes = []
    for g, ax in zip(grads, axes):
        rows, cols = _shard_dims(g.shape, ax)
        out_shapes.append(_sds((7, rows // 2, cols), g.dtype))

    def copies(ins, outs, sems):
        send_sems, recv_sems = sems
        x, y, c = _coords()
        table = []
        for w in range(n):
            rows, cols = _shard_dims(ins[w].shape, axes[w])
            for k in range(1, 8):
                fx, fy, fc = (k >> 2) & 1, (k >> 1) & 1, k & 1
                px, py, pc = _flip(x, fx), _flip(y, fy), _flip(c, fc)
                piece = _shard_window(ins[w], axes[w], 2 * px + py, pc, rows, cols)
                table.append(pltpu.make_async_remote_copy(
                    src_ref=piece, dst_ref=outs[w].at[k - 1], send_sem=send_sems.at[w, k - 1],
                    recv_sem=recv_sems.at[w, k - 1], device_id=(px, py, pc), device_id_type=MESH))
        return table

    def start(ins, outs, sems):
        for cp in copies(ins, outs, sems):
            cp.start()

    def finish(ins, outs, sems):
        table = copies(ins, outs, sems)
        for cp in table:
            cp.wait_recv()
        for cp in table:
            cp.wait_send()

    return _Side(grads, out_shapes, {}, [pltpu.SemaphoreType.DMA((n, 7)), pltpu.SemaphoreType.DMA((n, 7))],
                 start, None, finish)


def _rs_sum_join(g, got, out_prev, layer, depth, axis, ids, *, name, tr=256):
    _, rows2, cols = got.shape
    tr = _fit(tr, rows2)
    nt = rows2 // tr
    if axis == 1:
        own_map = lambda i, ids_: (ids_[1] * nt + i, ids_[0])
    else:
        own_map = lambda i, ids_: ((ids_[0] * 2 + ids_[1]) * nt + i, 0)

    def body(ids_ref, g_ref, got_ref, *rest):
        out_ref, buf, local_sems, send_sems, recv_sem = rest[-5:]
        i = pl.program_id(0)
        x, y, c = _coords()
        sibling = (x, y, 1 - c)

        def copies(step, slot):
            rows_mine = pl.ds(c * rows2 + step * tr, tr)
            dst = out_ref.at[layer, rows_mine, :]
            keep = pltpu.make_async_copy(buf.at[slot], dst, local_sems.at[slot])
            give = pltpu.make_async_remote_copy(src_ref=buf.at[slot], dst_ref=dst, send_sem=send_sems.at[slot],
                                                recv_sem=recv_sem, device_id=sibling, device_id_type=MESH)
            return keep, give

        def drain(step, slot):
            keep, give = copies(step, slot)
            keep.wait()
            give.wait_send()

        slot = i % 2

        @pl.when(i >= 2)
        def _():
            drain(i - 2, slot)

        acc = g_ref[...].astype(F32)
        for k in range(7):
            acc = acc + got_ref[k].astype(F32)
        buf[slot] = acc
        keep, give = copies(i, slot)
        keep.start()
        give.start()

        @pl.when(i == nt - 1)
        def _():
            if nt >= 2:
                drain(nt - 2, (nt - 2) % 2)
            drain(nt - 1, (nt - 1) % 2)
            theirs = out_ref.at[layer, pl.ds((1 - c) * rows2, rows2), :]
            pltpu.make_async_remote_copy(src_ref=theirs, dst_ref=theirs, send_sem=send_sems.at[0], recv_sem=recv_sem,
                                         device_id=sibling, device_id_type=MESH).wait_recv()

    hbm = pl.BlockSpec(memory_space=pltpu.HBM)
    in_specs = [pl.BlockSpec((tr, cols), own_map), pl.BlockSpec((7, tr, cols), lambda i, ids_: (0, i, 0))]
    operands = [ids, g, got]
    aliases = {}
    if out_prev is not None:
        in_specs.append(hbm)
        operands.append(out_prev)
        aliases = {3: 0}
    grid_spec = pltpu.PrefetchScalarGridSpec(
        num_scalar_prefetch=1, grid=(nt,), in_specs=in_specs, out_specs=hbm,
        scratch_shapes=[pltpu.VMEM((2, tr, cols), F32), pltpu.SemaphoreType.DMA((2,)), pltpu.SemaphoreType.DMA((2,)),
                        pltpu.SemaphoreType.DMA])
    return pl.pallas_call(body, name=name, grid_spec=grid_spec, out_shape=_sds((depth, 2 * rows2, cols), F32),
                          input_output_aliases=aliases, compiler_params=_params(("arbitrary",)))(*operands)


MIX = ("w_proj_conv", "w_proj_attn", "w_out")


class _Schedule:
    def __init__(self, placed, depth, ids):
        self.placed, self.depth, self.ids = placed, depth, ids
        self.full, self.g, self.carried = {}, {}, None
        self.reduced = {k: None for k in BIG}
        first = [(0, "w_in")]
        self._landed(first, _side_call(self._gather(first), name="gather_w_in_0"))

    def _gather(self, keys):
        return _gather_side([self.placed[k] for k in keys], [SHARD_AXIS[k[1]] for k in keys])

    def _landed(self, keys, outs):
        for k, o in zip(keys, outs):
            self.full[k] = o

    def _exchange(self, keys):
        return _exchange_side([self.g[k] for k in keys], [SHARD_AXIS[k[1]] for k in keys])

    def _reduce(self, keys, got):
        for (l, name), pieces in zip(keys, got):
            self.reduced[name] = _rs_sum_join(self.g[(l, name)], pieces, self.reduced[name], l, self.depth,
                                              SHARD_AXIS[name], self.ids, name=f"rs_sum_join_{l}_{name}")

    def weight(self, l, name):
        return self.full[(l, name)]

    def grad(self, l, name, g):
        self.g[(l, name)] = g

    def side(self, kind, l):
        early = ("w_in",) + MIX + ("w_mlp_in",)
        if kind == "in_proj":
            keys, make = ([(l, k) for k in early[1:]] if l == 0 else []), self._gather
        elif kind == "attn_fwd":
            nxt = [(l + 1, k) for k in early] if l + 1 < self.depth else []
            keys, make = [(l, "w_mlp_out")] + nxt, self._gather
        elif kind == "attn_bwd":
            keys, make = [(l, k) for k in ("w_mlp_out", "w_mlp_in") + MIX], self._exchange
        else:
            keys, make = [(l, "w_in")], self._exchange
        self.carried = keys
        return make(keys) if keys else None

    def done(self, kind, l, outs):
        (self._landed if kind in ("in_proj", "attn_fwd") else self._reduce)(self.carried, outs)


def _flat_rows(shape):
    rows = 1
    for s in shape[:-1]:
        rows *= s
    return rows, shape[-1]


def _row_tile(rows, cols, cap_bytes=2 * 1024 * 1024):
    t = rows
    while t * cols * 4 > cap_bytes and t % 16 == 0:
        t //= 2
    return t


def _ada_fwd(c_all, w_ada, b_loc, *, name, tn=512):
    l, d, nl = w_ada.shape
    b = c_all.shape[0]
    tn = min(tn, nl)

    def body(c_ref, w_ref, b_ref, o_ref):
        o_ref[0] = jnp.dot(c_ref[...], w_ref[0], preferred_element_type=F32,
                           precision=lax.Precision.HIGHEST) + b_ref[0]

    return pl.pallas_call(
        body, name=name, grid=(l, nl // tn),
        in_specs=[pl.BlockSpec((b, d), lambda i, j: (0, 0)), pl.BlockSpec((1, d, tn), lambda i, j: (i, 0, j)),
                  pl.BlockSpec((1, 1, tn), lambda i, j: (i, 0, j))],
        out_specs=pl.BlockSpec((1, b, tn), lambda i, j: (i, 0, j)),
        out_shape=_sds((l, b, nl), F32),
        compiler_params=_params(("parallel", "parallel")),
    )(c_all, w_ada, b_loc)


def _ada_bwd(c_t, dmod_loc, *, name, tn=512):
    d, b = c_t.shape
    l, _, nl = dmod_loc.shape
    tn = min(tn, nl)

    def body(c_ref, dm_ref, o_ref):
        cv = c_ref[...]
        dm = dm_ref[0]
        acc = cv[:, 0:1] * dm[0:1, :]
        for k in range(1, b):
            acc = acc + cv[:, k:k + 1] * dm[k:k + 1, :]
        o_ref[0] = acc

    return pl.pallas_call(
        body, name=name, grid=(l, nl // tn),
        in_specs=[pl.BlockSpec((d, b), lambda i, j: (0, 0)), pl.BlockSpec((1, b, tn), lambda i, j: (i, 0, j))],
        out_specs=pl.BlockSpec((1, d, tn), lambda i, j: (i, 0, j)),
        out_shape=_sds((l, d, nl), F32),
        compiler_params=_params(("parallel", "parallel")),
    )(c_t, dmod_loc)


def _sum_devices(p, *, name):
    k, r, c_ = p.shape

    def body(p_ref, o_ref):
        acc = p_ref[0]
        for j in range(1, k):
            acc = acc + p_ref[j]
        o_ref[...] = acc

    return pl.pallas_call(body, name=name, out_shape=_sds((r, c_), F32),
                          in_specs=[pl.BlockSpec(memory_space=pltpu.VMEM)],
                          out_specs=pl.BlockSpec(memory_space=pltpu.VMEM))(p)


def _adamw(w, g, m, v, *, name):
    shape = w.shape
    rows, cols = _flat_rows(shape)
    tr = _row_tile(rows, cols, cap_bytes=1024 * 1024)
    c1 = 1.0 / (1.0 - ADAM_B1 ** ADAM_STEP)
    c2 = 1.0 / (1.0 - ADAM_B2 ** ADAM_STEP)

    def body(w_ref, g_ref, m_ref, v_ref, go_ref, d_ref, nm_ref, nv_ref):
        gv = g_ref[...]
        nm = ADAM_B1 * m_ref[...] + (1.0 - ADAM_B1) * gv
        nv = ADAM_B2 * v_ref[...] + (1.0 - ADAM_B2) * (gv * gv)
        m_hat = nm * c1
        v_hat = nv * c2
        go_ref[...] = gv
        d_ref[...] = -ADAM_LR * (m_hat / (jnp.sqrt(v_hat) + ADAM_EPS) + ADAM_WD * w_ref[...])
        nm_ref[...] = nm
        nv_ref[...] = nv

    spec = pl.BlockSpec((tr, cols), lambda i: (i, 0))
    flat = lambda a: a.reshape(rows, cols)
    outs = pl.pallas_call(body, name=name, grid=(rows // tr,), in_specs=[spec] * 4, out_specs=[spec] * 4,
                          out_shape=[_sds((rows, cols), F32)] * 4, compiler_params=_params(("parallel",)),
                          )(flat(w), flat(g), flat(m), flat(v))
    return tuple(o.reshape(shape) for o in outs)


WEIGHTS = ("w_ada", "b_ada", "g_pre_mix", "g_post_mix", "g_pre_mlp", "g_post_mlp", "w_in", "conv_w",
           "w_proj_conv", "w_proj_attn", "w_out", "w_mlp_in", "w_mlp_out")
GAINS = ("g_pre_mix", "g_post_mix", "g_pre_mlp", "g_post_mlp")


def kernel(x, c, w_ada, b_ada, g_pre_mix, g_post_mix, g_pre_mlp, g_post_mlp, w_in, conv_w, w_proj_conv, w_proj_attn, w_out, w_mlp_in, w_mlp_out, loss_target, m_w_ada, m_b_ada, m_g_pre_mix, m_g_post_mix, m_g_pre_mlp, m_g_post_mlp, m_w_in, m_conv_w, m_w_proj_conv, m_w_proj_attn, m_w_out, m_w_mlp_in, m_w_mlp_out, v_w_ada, v_b_ada, v_g_pre_mix, v_g_post_mix, v_g_pre_mlp, v_g_post_mlp, v_w_in, v_conv_w, v_w_proj_conv, v_w_proj_attn, v_w_out, v_w_mlp_in, v_w_mlp_out):
    params = dict(w_ada=w_ada, b_ada=b_ada, g_pre_mix=g_pre_mix, g_post_mix=g_post_mix, g_pre_mlp=g_pre_mlp,
                  g_post_mlp=g_post_mlp, w_in=w_in, conv_w=conv_w, w_proj_conv=w_proj_conv, w_proj_attn=w_proj_attn,
                  w_out=w_out, w_mlp_in=w_mlp_in, w_mlp_out=w_mlp_out)
    m_in = dict(w_ada=m_w_ada, b_ada=m_b_ada, g_pre_mix=m_g_pre_mix, g_post_mix=m_g_post_mix, g_pre_mlp=m_g_pre_mlp,
                g_post_mlp=m_g_post_mlp, w_in=m_w_in, conv_w=m_conv_w, w_proj_conv=m_w_proj_conv,
                w_proj_attn=m_w_proj_attn, w_out=m_w_out, w_mlp_in=m_w_mlp_in, w_mlp_out=m_w_mlp_out)
    v_in = dict(w_ada=v_w_ada, b_ada=v_b_ada, g_pre_mix=v_g_pre_mix, g_post_mix=v_g_post_mix, g_pre_mlp=v_g_pre_mlp,
                g_post_mlp=v_g_post_mlp, w_in=v_w_in, conv_w=v_conv_w, w_proj_conv=v_w_proj_conv,
                w_proj_attn=v_w_proj_attn, w_out=v_w_out, w_mlp_in=v_w_mlp_in, w_mlp_out=v_w_mlp_out)

    depth, d, nl_ada = w_ada.shape
    ix, iy, ic = lax.axis_index("x"), lax.axis_index("y"), lax.axis_index("c")
    chip = 2 * ix + iy
    me = 4 * ix + 2 * iy + ic
    xs = x[0]
    target = loss_target[0]

    c_all = _all_gather_small(jnp.broadcast_to(c, (8, d)), name="gather_c")[:, 0, :]
    b_loc = lax.dynamic_slice_in_dim(b_ada, chip * nl_ada, nl_ada, axis=1)[:, None, :]
    mod_loc = _ada_fwd(c_all, w_ada, b_loc, name="ada_fwd")
    mod_all = _all_gather_small(mod_loc.reshape(depth * 8, nl_ada), name="gather_mod")
    mod_all = mod_all.reshape(4, 2, depth, 8, nl_ada)[:, 0]
    mod_me = lax.dynamic_index_in_dim(mod_all, me, axis=2, keepdims=False)
    mods = jnp.transpose(mod_me, (1, 0, 2)).reshape(depth, N_MOD, d)

    chip_arr = jnp.reshape(chip, (1,)).astype(jnp.int32)
    ids = jnp.stack([chip, ic]).astype(jnp.int32)
    placed = {(l, k): _cast_place(params[k], l, SHARD_AXIS[k], chip_arr, name=f"place_{k}_{l}")
              for l in range(depth) for k in BIG}
    conv_full = _all_gather_small(
        jnp.pad(conv_w.reshape(depth * 3, -1), ((0, 8 - depth * 3), (0, 0))), name="gather_conv_w")
    conv_full = conv_full.reshape(4, 2, 8, -1)[:, 0, :depth * 3]
    conv_full = jnp.transpose(conv_full, (1, 0, 2)).reshape(depth, 3, -1)

    gains = jnp.stack([params[k] for k in GAINS], axis=1)
    schedule = _Schedule(placed, depth, ids)
    loss, dx, conv_grads, dmods, dgains = _local_step(xs, target, mods, gains, conv_full, schedule)

    cw = conv_full.shape[2]
    rows = [dmods.reshape(depth * N_MOD, d), dgains.reshape(depth * 4, d),
            conv_grads.reshape(-1, d), jnp.broadcast_to(loss, (1, d))]
    payload = jnp.concatenate(rows, axis=0)
    n_rows = payload.shape[0]
    pad = (-n_rows) % 8
    payload = jnp.pad(payload, ((0, pad), (0, 0)))
    everyone = _all_gather_small(payload, name="gather_small_grads")
    total = _sum_devices(everyone, name="sum_small_grads")
    r0 = depth * N_MOD
    grads = {}
    grads["b_ada"] = total[:r0].reshape(depth, N_MOD * d)
    gsum = total[r0:r0 + depth * 4].reshape(depth, 4, d)
    for k, name in enumerate(GAINS):
        grads[name] = gsum[:, k]
    r1 = r0 + depth * 4
    n_conv = (depth * 3 * cw) // d
    conv_g = total[r1:r1 + n_conv].reshape(depth, 3, cw)
    grads["conv_w"] = lax.dynamic_slice_in_dim(conv_g, chip * (cw // 4), cw // 4, axis=2)
    loss_out = total[r1 + n_conv, 0]
    dmod_all = everyone[:, :r0].reshape(8, depth, N_MOD * d)
    dmod_loc = lax.dynamic_slice_in_dim(dmod_all, chip * nl_ada, nl_ada, axis=2)
    grads["w_ada"] = _ada_bwd(c_all.T, jnp.transpose(dmod_loc, (1, 0, 2)), name="ada_bwd")

    for k in BIG:
        grads[k] = schedule.reduced[k]

    deltas, new_m, new_v = {}, {}, {}
    for k in WEIGHTS:
        grads[k], deltas[k], new_m[k], new_v[k] = _adamw(params[k], grads[k], m_in[k], v_in[k], name=f"adamw_{k}")

    return (loss_out, dx[None], *[grads[k] for k in WEIGHTS], *[deltas[k] for k in WEIGHTS],
            *[new_m[k] for k in WEIGHTS], *[new_v[k] for k in WEIGHTS])
```
